```python
import math
import jax, jax.numpy as jnp
from jax import lax
import numpy as np

D_MODEL = 2048
BATCH = 8
SEQ = 4096
DEPTH = 1

CHUNK = 64
MEM_LEN = 256
EPS = 1e-6
GDN_HEAD_DIM = 128
GDN_WIDTH = D_MODEL // 2
GDN_HEADS = GDN_WIDTH // GDN_HEAD_DIM
CONV_WIDTH = 4
S5_WIDTH = D_MODEL // 2
S5_GROUP = 16
S5_GROUPS = S5_WIDTH // S5_GROUP
S5_STATE = 64
XA_HEADS = 4
XA_WIDTH = D_MODEL // 2
XA_HEAD_DIM = XA_WIDTH // XA_HEADS
N_BRANCHES = 3
IN_SPLITS = (3 * GDN_WIDTH, GDN_WIDTH, GDN_HEADS, GDN_HEADS, S5_WIDTH, S5_WIDTH, XA_WIDTH, XA_WIDTH, N_BRANCHES * D_MODEL)

kernel_name = 'hybrid_gdn_s5_memxattn_block'


def rmsnorm(x, g):
    xf = x.astype(jnp.float32)
    y = xf * lax.rsqrt(jnp.mean(xf * xf, axis=-1, keepdims=True) + EPS)
    return (y * g.astype(jnp.float32)).astype(x.dtype)


def l2norm(x):
    return x * lax.rsqrt(jnp.sum(x * x, axis=-1, keepdims=True) + EPS)


def split_columns(t, sizes):
    out = []
    start = 0
    for s in sizes:
        out.append(t[..., start:start + s])
        start += s
    return out


def causal_depthwise_conv(x, w):
    c = x.shape[-1]
    return lax.conv_general_dilated(x, w[:, None, :], window_strides=(1,), padding=[(w.shape[0] - 1, 0)],
                                    dimension_numbers=('NWC', 'WIO', 'NWC'), feature_group_count=c)


def chunked_gated_delta_rule(q, k, v, g, beta):
    bsz, s, h, dk = q.shape
    dv = v.shape[-1]
    n = s // CHUNK
    def to_chunks(t):
        t = t.reshape((bsz, n, CHUNK, h) + t.shape[3:])
        return jnp.swapaxes(t, 2, 3)
    q, k, v, g, beta = (to_chunks(t) for t in (q, k, v, g, beta))
    gc = jnp.cumsum(g, axis=-1)
    incl = jnp.tril(jnp.ones((CHUNK, CHUNK), dtype=bool))
    strict = jnp.tril(jnp.ones((CHUNK, CHUNK), dtype=bool), -1)
    decay = jnp.exp(jnp.where(incl, gc[..., :, None] - gc[..., None, :], -jnp.inf))
    k_beta = k * beta[..., None]
    lower = jnp.where(strict, jnp.einsum('bnhid,bnhjd->bnhij', k_beta, k) * decay, 0.0)
    eye = jnp.eye(CHUNK, dtype=q.dtype)
    rhs = jnp.concatenate([v * beta[..., None], k_beta * jnp.exp(gc)[..., None]], axis=-1)
    sol = lax.linalg.triangular_solve(eye + lower, rhs, left_side=True, lower=True, unit_diagonal=True)
    u_val, w_dec = sol[..., :dv], sol[..., dv:]
    qk = jnp.einsum('bnhid,bnhjd->bnhij', q, k) * decay
    q_dec = q * jnp.exp(gc)[..., None]
    k_dec = k * jnp.exp(gc[..., -1:] - gc)[..., None]
    g_last = jnp.exp(gc[..., -1])

    def step(state, inp):
        w_n, u_n, qd_n, qk_n, kd_n, gl_n = inp
        v_new = u_n - jnp.einsum('bhcd,bhde->bhce', w_n, state)
        o = jnp.einsum('bhcd,bhde->bhce', qd_n, state) + jnp.einsum('bhij,bhje->bhie', qk_n, v_new)
        state = state * gl_n[..., None, None] + jnp.einsum('bhcd,bhce->bhde', kd_n, v_new)
        return state, o

    xs = tuple(jnp.moveaxis(t, 1, 0) for t in (w_dec, u_val, q_dec, qk, k_dec, g_last))
    state0 = jnp.zeros((bsz, h, dk, dv), dtype=q.dtype)
    _, o = lax.scan(step, state0, xs)
    o = jnp.transpose(o, (1, 0, 3, 2, 4))
    return o.reshape(bsz, s, h, dv)


def gated_deltanet(qkv_raw, beta_raw, a_raw, conv_w, a_log, dt_bias, out_norm_g):
    bsz, s, _ = qkv_raw.shape
    qkv = jax.nn.silu(causal_depthwise_conv(qkv_raw, conv_w)).astype(jnp.float32)
    q, k, v = jnp.split(qkv, 3, axis=-1)
    q = q.reshape(bsz, s, GDN_HEADS, GDN_HEAD_DIM)
    k = k.reshape(bsz, s, GDN_HEADS, GDN_HEAD_DIM)
    v = v.reshape(bsz, s, GDN_HEADS, GDN_HEAD_DIM)
    q = l2norm(q) * (GDN_HEAD_DIM ** -0.5)
    k = l2norm(k)
    beta = jax.nn.sigmoid(beta_raw.astype(jnp.float32))
    g = -jnp.exp(a_log.astype(jnp.float32)) * jax.nn.softplus(a_raw.astype(jnp.float32) + dt_bias.astype(jnp.float32))
    o = chunked_gated_delta_rule(q, k, v, g, beta)
    o = rmsnorm(o, out_norm_g)
    return o.reshape(bsz, s, GDN_WIDTH).astype(qkv_raw.dtype)


def complex_linear_combine(e1, e2):
    a1r, a1i, b1r, b1i = e1
    a2r, a2i, b2r, b2i = e2
    return (a2r * a1r - a2i * a1i,
            a2r * a1i + a2i * a1r,
            a2r * b1r - a2i * b1i + b2r,
            a2r * b1i + a2i * b1r + b2i)


def s5_layer(xb, lam_re, lam_im, log_dt, b_re, b_im, c_re, c_im, d, w_glu):
    f32 = jnp.float32
    bsz, s, _ = xb.shape
    xf = xb.astype(f32)
    xg = xf.reshape(bsz, s, S5_GROUPS, S5_GROUP)
    lr, li = lam_re.astype(f32), lam_im.astype(f32)
    dt = jnp.exp(log_dt.astype(f32))[:, None]
    mag = jnp.exp(lr * dt)
    ab_re, ab_im = mag * jnp.cos(li * dt), mag * jnp.sin(li * dt)
    den = lr * lr + li * li
    nr, ni = ab_re - 1.0, ab_im
    coef_re = (nr * lr + ni * li) / den
    coef_im = (ni * lr - nr * li) / den
    br, bi = b_re.astype(f32), b_im.astype(f32)
    bb_re = coef_re[..., None] * br - coef_im[..., None] * bi
    bb_im = coef_re[..., None] * bi + coef_im[..., None] * br
    bu_re = jnp.einsum('bsgi,gpi->bsgp', xg, bb_re)
    bu_im = jnp.einsum('bsgi,gpi->bsgp', xg, bb_im)
    a_re = jnp.broadcast_to(ab_re, bu_re.shape)
    a_im = jnp.broadcast_to(ab_im, bu_im.shape)
    _, _, h_re, h_im = lax.associative_scan(complex_linear_combine, (a_re, a_im, bu_re, bu_im), axis=1)
    y = jnp.einsum('bsgp,gip->bsgi', h_re, c_re.astype(f32)) - jnp.einsum('bsgp,gip->bsgi', h_im, c_im.astype(f32))
    y = y.reshape(bsz, s, S5_WIDTH) + d.astype(f32) * xf
    y = jax.nn.gelu(y)
    val, gate = jnp.split(y @ w_glu.astype(f32), 2, axis=-1)
    return (val * jax.nn.sigmoid(gate)).astype(xb.dtype)


def memory_cross_attention(q_raw, mem_n, w_kv):
    bsz, s, _ = q_raw.shape
    k, v = jnp.split(mem_n @ w_kv, 2, axis=-1)
    q = q_raw.reshape(bsz, s, XA_HEADS, XA_HEAD_DIM)
    k = k.reshape(bsz, MEM_LEN, XA_HEADS, XA_HEAD_DIM)
    v = v.reshape(bsz, MEM_LEN, XA_HEADS, XA_HEAD_DIM)
    scores = jnp.einsum('bshd,bmhd->bhsm', q, k).astype(jnp.float32) * (XA_HEAD_DIM ** -0.5)
    p = jax.nn.softmax(scores, axis=-1).astype(v.dtype)
    o = jnp.einsum('bhsm,bmhd->bshd', p, v)
    return o.reshape(bsz, s, XA_WIDTH)


def _fwd_setup_inputs(seed: int = 0) -> dict:
    key = jax.random.key(seed)
    ks = jax.random.split(key, 28)
    f32 = jnp.float32
    L, G, P = DEPTH, S5_GROUPS, S5_STATE
    def nrm(k, shape, scale):
        return jax.random.normal(k, shape, f32) * scale
    in_width = sum(IN_SPLITS)
    dt = jnp.exp(jax.random.uniform(ks[6], (L, GDN_HEADS), f32, math.log(1e-3), math.log(1e-1)))
    n_idx = jnp.arange(P, dtype=f32)
    return {
        'x': nrm(ks[0], (BATCH, SEQ, D_MODEL), 1.0),
        'mem': nrm(ks[1], (BATCH, MEM_LEN, D_MODEL), 1.0),
        'norm_g': 1.0 + nrm(ks[2], (L, D_MODEL), 0.02),
        'w_in': nrm(ks[3], (L, D_MODEL, in_width), D_MODEL ** -0.5),
        'conv_w': nrm(ks[4], (L, CONV_WIDTH, 3 * GDN_WIDTH), CONV_WIDTH ** -0.5),
        'gdn_a_log': jnp.log(jax.random.uniform(ks[5], (L, GDN_HEADS), f32, 1.0, 16.0)),
        'gdn_dt_bias': dt + jnp.log(-jnp.expm1(-dt)),
        'gdn_norm_g': 1.0 + nrm(ks[7], (L, GDN_HEAD_DIM), 0.02),
        's5_lambda_re': -0.5 + nrm(ks[8], (L, G, P), 0.01),
        's5_lambda_im': math.pi * n_idx + nrm(ks[9], (L, G, P), 0.01),
        's5_log_dt': jax.random.uniform(ks[10], (L, G), f32, math.log(1e-3), math.log(1e-1)),
        's5_b_re': nrm(ks[11], (L, G, P, S5_GROUP), (2 * S5_GROUP) ** -0.5),
        's5_b_im': nrm(ks[12], (L, G, P, S5_GROUP), (2 * S5_GROUP) ** -0.5),
        's5_c_re': nrm(ks[13], (L, G, S5_GROUP, P), (2 * P) ** -0.5),
        's5_c_im': nrm(ks[14], (L, G, S5_GROUP, P), (2 * P) ** -0.5),
        's5_d': nrm(ks[15], (L, S5_WIDTH), 1.0),
        's5_w_glu': nrm(ks[16], (L, S5_WIDTH, 2 * S5_WIDTH), S5_WIDTH ** -0.5),
        'mem_norm_g': 1.0 + nrm(ks[17], (L, D_MODEL), 0.02),
        'w_kv_mem': nrm(ks[18], (L, D_MODEL, 2 * XA_WIDTH), D_MODEL ** -0.5),
        'w_br_a': nrm(ks[19], (L, GDN_WIDTH, D_MODEL), GDN_WIDTH ** -0.5),
        'w_br_b': nrm(ks[20], (L, S5_WIDTH, D_MODEL), S5_WIDTH ** -0.5),
        'w_br_c': nrm(ks[21], (L, XA_WIDTH, D_MODEL), XA_WIDTH ** -0.5),
        'w_out': nrm(ks[22], (L, D_MODEL, D_MODEL), D_MODEL ** -0.5),
        'final_g': 1.0 + nrm(ks[23], (D_MODEL,), 0.02),
    }


def _fwd_reference(x, mem, norm_g, w_in, conv_w, gdn_a_log, gdn_dt_bias, gdn_norm_g,
              s5_lambda_re, s5_lambda_im, s5_log_dt, s5_b_re, s5_b_im, s5_c_re, s5_c_im, s5_d, s5_w_glu,
              mem_norm_g, w_kv_mem, w_br_a, w_br_b, w_br_c, w_out, final_g):
    bsz, s, _ = x.shape
    h = x
    for l in range(DEPTH):
        u = rmsnorm(h, norm_g[l])
        proj = u @ w_in[l]
        qkv_a, z_a, beta_raw, a_raw, x_b, z_b, q_c, z_c, gate_raw = split_columns(proj, IN_SPLITS)
        o_a = gated_deltanet(qkv_a, beta_raw, a_raw, conv_w[l], gdn_a_log[l], gdn_dt_bias[l], gdn_norm_g[l])
        p_a = (o_a * jax.nn.silu(z_a)) @ w_br_a[l]
        o_b = s5_layer(x_b, s5_lambda_re[l], s5_lambda_im[l], s5_log_dt[l], s5_b_re[l], s5_b_im[l],
                       s5_c_re[l], s5_c_im[l], s5_d[l], s5_w_glu[l])
        p_b = (o_b * jax.nn.silu(z_b)) @ w_br_b[l]
        o_c = memory_cross_attention(q_c, rmsnorm(mem, mem_norm_g[l]), w_kv_mem[l])
        p_c = (o_c * jax.nn.silu(z_c)) @ w_br_c[l]
        gates = jax.nn.sigmoid(gate_raw.astype(jnp.float32)).astype(x.dtype).reshape(bsz, s, N_BRANCHES, D_MODEL)
        merged = gates[:, :, 0] * p_a + gates[:, :, 1] * p_b + gates[:, :, 2] * p_c
        h = h + merged @ w_out[l]
    return rmsnorm(h, final_g)


import jax as _jax
import jax.numpy as _jnp

TWIN_FORMAT = 'train_step'
FWD_PARAMS = ['x', 'mem', 'norm_g', 'w_in', 'conv_w', 'gdn_a_log', 'gdn_dt_bias', 'gdn_norm_g', 's5_lambda_re', 's5_lambda_im', 's5_log_dt', 's5_b_re', 's5_b_im', 's5_c_re', 's5_c_im', 's5_d', 's5_w_glu', 'mem_norm_g', 'w_kv_mem', 'w_br_a', 'w_br_b', 'w_br_c', 'w_out', 'final_g']
TWIN_WEIGHTS = ['norm_g', 'w_in', 'conv_w', 'gdn_a_log', 'gdn_dt_bias', 'gdn_norm_g', 's5_lambda_re', 's5_lambda_im', 's5_log_dt', 's5_b_re', 's5_b_im', 's5_c_re', 's5_c_im', 's5_d', 's5_w_glu', 'mem_norm_g', 'w_kv_mem', 'w_br_a', 'w_br_b', 'w_br_c', 'w_out', 'final_g']
TWIN_DIFF_INPUT = 'x'
TWIN_INPUTS = ['x', 'mem', 'norm_g', 'w_in', 'conv_w', 'gdn_a_log', 'gdn_dt_bias', 'gdn_norm_g', 's5_lambda_re', 's5_lambda_im', 's5_log_dt', 's5_b_re', 's5_b_im', 's5_c_re', 's5_c_im', 's5_d', 's5_w_glu', 'mem_norm_g', 'w_kv_mem', 'w_br_a', 'w_br_b', 'w_br_c', 'w_out', 'final_g', 'loss_target', 'm_norm_g', 'm_w_in', 'm_conv_w', 'm_gdn_a_log', 'm_gdn_dt_bias', 'm_gdn_norm_g', 'm_s5_lambda_re', 'm_s5_lambda_im', 'm_s5_log_dt', 'm_s5_b_re', 'm_s5_b_im', 'm_s5_c_re', 'm_s5_c_im', 'm_s5_d', 'm_s5_w_glu', 'm_mem_norm_g', 'm_w_kv_mem', 'm_w_br_a', 'm_w_br_b', 'm_w_br_c', 'm_w_out', 'm_final_g', 'v_norm_g', 'v_w_in', 'v_conv_w', 'v_gdn_a_log', 'v_gdn_dt_bias', 'v_gdn_norm_g', 'v_s5_lambda_re', 'v_s5_lambda_im', 'v_s5_log_dt', 'v_s5_b_re', 'v_s5_b_im', 'v_s5_c_re', 'v_s5_c_im', 'v_s5_d', 'v_s5_w_glu', 'v_mem_norm_g', 'v_w_kv_mem', 'v_w_br_a', 'v_w_br_b', 'v_w_br_c', 'v_w_out', 'v_final_g']
TWIN_OUTPUTS = ['loss', 'grad_x', 'grad_norm_g', 'grad_w_in', 'grad_conv_w', 'grad_gdn_a_log', 'grad_gdn_dt_bias', 'grad_gdn_norm_g', 'grad_s5_lambda_re', 'grad_s5_lambda_im', 'grad_s5_log_dt', 'grad_s5_b_re', 'grad_s5_b_im', 'grad_s5_c_re', 'grad_s5_c_im', 'grad_s5_d', 'grad_s5_w_glu', 'grad_mem_norm_g', 'grad_w_kv_mem', 'grad_w_br_a', 'grad_w_br_b', 'grad_w_br_c', 'grad_w_out', 'grad_final_g', 'delta_norm_g', 'delta_w_in', 'delta_conv_w', 'delta_gdn_a_log', 'delta_gdn_dt_bias', 'delta_gdn_norm_g', 'delta_s5_lambda_re', 'delta_s5_lambda_im', 'delta_s5_log_dt', 'delta_s5_b_re', 'delta_s5_b_im', 'delta_s5_c_re', 'delta_s5_c_im', 'delta_s5_d', 'delta_s5_w_glu', 'delta_mem_norm_g', 'delta_w_kv_mem', 'delta_w_br_a', 'delta_w_br_b', 'delta_w_br_c', 'delta_w_out', 'delta_final_g', 'new_m_norm_g', 'new_m_w_in', 'new_m_conv_w', 'new_m_gdn_a_log', 'new_m_gdn_dt_bias', 'new_m_gdn_norm_g', 'new_m_s5_lambda_re', 'new_m_s5_lambda_im', 'new_m_s5_log_dt', 'new_m_s5_b_re', 'new_m_s5_b_im', 'new_m_s5_c_re', 'new_m_s5_c_im', 'new_m_s5_d', 'new_m_s5_w_glu', 'new_m_mem_norm_g', 'new_m_w_kv_mem', 'new_m_w_br_a', 'new_m_w_br_b', 'new_m_w_br_c', 'new_m_w_out', 'new_m_final_g', 'new_v_norm_g', 'new_v_w_in', 'new_v_conv_w', 'new_v_gdn_a_log', 'new_v_gdn_dt_bias', 'new_v_gdn_norm_g', 'new_v_s5_lambda_re', 'new_v_s5_lambda_im', 'new_v_s5_log_dt', 'new_v_s5_b_re', 'new_v_s5_b_im', 'new_v_s5_c_re', 'new_v_s5_c_im', 'new_v_s5_d', 'new_v_s5_w_glu', 'new_v_mem_norm_g', 'new_v_w_kv_mem', 'new_v_w_br_a', 'new_v_w_br_b', 'new_v_w_br_c', 'new_v_w_out', 'new_v_final_g']
TWIN_LEAF_KINDS = {'loss': 'loss', 'grad_x': 'grad_x', 'grad_norm_g': 'grad_w', 'grad_w_in': 'grad_w', 'grad_conv_w': 'grad_w', 'grad_gdn_a_log': 'grad_w', 'grad_gdn_dt_bias': 'grad_w', 'grad_gdn_norm_g': 'grad_w', 'grad_s5_lambda_re': 'grad_w', 'grad_s5_lambda_im': 'grad_w', 'grad_s5_log_dt': 'grad_w', 'grad_s5_b_re': 'grad_w', 'grad_s5_b_im': 'grad_w', 'grad_s5_c_re': 'grad_w', 'grad_s5_c_im': 'grad_w', 'grad_s5_d': 'grad_w', 'grad_s5_w_glu': 'grad_w', 'grad_mem_norm_g': 'grad_w', 'grad_w_kv_mem': 'grad_w', 'grad_w_br_a': 'grad_w', 'grad_w_br_b': 'grad_w', 'grad_w_br_c': 'grad_w', 'grad_w_out': 'grad_w', 'grad_final_g': 'grad_w', 'delta_norm_g': 'delta_w', 'delta_w_in': 'delta_w', 'delta_conv_w': 'delta_w', 'delta_gdn_a_log': 'delta_w', 'delta_gdn_dt_bias': 'delta_w', 'delta_gdn_norm_g': 'delta_w', 'delta_s5_lambda_re': 'delta_w', 'delta_s5_lambda_im': 'delta_w', 'delta_s5_log_dt': 'delta_w', 'delta_s5_b_re': 'delta_w', 'delta_s5_b_im': 'delta_w', 'delta_s5_c_re': 'delta_w', 'delta_s5_c_im': 'delta_w', 'delta_s5_d': 'delta_w', 'delta_s5_w_glu': 'delta_w', 'delta_mem_norm_g': 'delta_w', 'delta_w_kv_mem': 'delta_w', 'delta_w_br_a': 'delta_w', 'delta_w_br_b': 'delta_w', 'delta_w_br_c': 'delta_w', 'delta_w_out': 'delta_w', 'delta_final_g': 'delta_w', 'new_m_norm_g': 'new_m', 'new_m_w_in': 'new_m', 'new_m_conv_w': 'new_m', 'new_m_gdn_a_log': 'new_m', 'new_m_gdn_dt_bias': 'new_m', 'new_m_gdn_norm_g': 'new_m', 'new_m_s5_lambda_re': 'new_m', 'new_m_s5_lambda_im': 'new_m', 'new_m_s5_log_dt': 'new_m', 'new_m_s5_b_re': 'new_m', 'new_m_s5_b_im': 'new_m', 'new_m_s5_c_re': 'new_m', 'new_m_s5_c_im': 'new_m', 'new_m_s5_d': 'new_m', 'new_m_s5_w_glu': 'new_m', 'new_m_mem_norm_g': 'new_m', 'new_m_w_kv_mem': 'new_m', 'new_m_w_br_a': 'new_m', 'new_m_w_br_b': 'new_m', 'new_m_w_br_c': 'new_m', 'new_m_w_out': 'new_m', 'new_m_final_g': 'new_m', 'new_v_norm_g': 'new_v', 'new_v_w_in': 'new_v', 'new_v_conv_w': 'new_v', 'new_v_gdn_a_log': 'new_v', 'new_v_gdn_dt_bias': 'new_v', 'new_v_gdn_norm_g': 'new_v', 'new_v_s5_lambda_re': 'new_v', 'new_v_s5_lambda_im': 'new_v', 'new_v_s5_log_dt': 'new_v', 'new_v_s5_b_re': 'new_v', 'new_v_s5_b_im': 'new_v', 'new_v_s5_c_re': 'new_v', 'new_v_s5_c_im': 'new_v', 'new_v_s5_d': 'new_v', 'new_v_s5_w_glu': 'new_v', 'new_v_mem_norm_g': 'new_v', 'new_v_w_kv_mem': 'new_v', 'new_v_w_br_a': 'new_v', 'new_v_w_br_b': 'new_v', 'new_v_w_br_c': 'new_v', 'new_v_w_out': 'new_v', 'new_v_final_g': 'new_v'}


def _forward(args):
    return _fwd_reference(*[args[k] for k in FWD_PARAMS])


def _output_shape():
    def fwd():
        inp = _fwd_setup_inputs(0)
        return _fwd_reference(*[inp[k] for k in FWD_PARAMS])
    out = _jax.eval_shape(fwd)
    return out.shape, out.dtype

N_MICROBATCH = 1
ADAM_LR = 0.001
ADAM_B1 = 0.9
ADAM_B2 = 0.999
ADAM_EPS = 1e-08
ADAM_WD = 0.01
ADAM_STEP = 10
PER_EXAMPLE_BATCH_AXIS = {'x': 0, 'mem': 0, 'loss_target': 0}
SHARED_INPUTS = []
_WEIGHT_DTYPES = {'norm_g': _jnp.float32, 'w_in': _jnp.float32, 'conv_w': _jnp.float32, 'gdn_a_log': _jnp.float32, 'gdn_dt_bias': _jnp.float32, 'gdn_norm_g': _jnp.float32, 's5_lambda_re': _jnp.float32, 's5_lambda_im': _jnp.float32, 's5_log_dt': _jnp.float32, 's5_b_re': _jnp.float32, 's5_b_im': _jnp.float32, 's5_c_re': _jnp.float32, 's5_c_im': _jnp.float32, 's5_d': _jnp.float32, 's5_w_glu': _jnp.float32, 'mem_norm_g': _jnp.float32, 'w_kv_mem': _jnp.float32, 'w_br_a': _jnp.float32, 'w_br_b': _jnp.float32, 'w_br_c': _jnp.float32, 'w_out': _jnp.float32, 'final_g': _jnp.float32}
MOMENT_SCALE = {'norm_g': 4.999869e-02, 'w_in': 1.841670e-02, 'conv_w': 2.888355e-02, 'gdn_a_log': 2.031926e-01, 'gdn_dt_bias': 1.959450e-01, 'gdn_norm_g': 1.115353e-01, 's5_lambda_re': 6.513336e-04, 's5_lambda_im': 7.415506e-04, 's5_log_dt': 3.327715e-01, 's5_b_re': 4.488089e-04, 's5_b_im': 4.561612e-04, 's5_c_re': 9.015034e-04, 's5_c_im': 9.077603e-04, 's5_d': 1.456132e-02, 's5_w_glu': 1.032831e-02, 'mem_norm_g': 4.110622e-03, 'w_kv_mem': 3.994858e-03, 'w_br_a': 2.608367e-02, 'w_br_b': 9.828846e-03, 'w_br_c': 2.854004e-03, 'w_out': 2.804224e-02, 'final_g': 1.597937e+01}


def _to_microbatches(a, axis):
    t = _jnp.moveaxis(a, axis, 0)
    t = t.reshape((N_MICROBATCH, t.shape[0] // N_MICROBATCH) + t.shape[1:])
    return _jnp.moveaxis(t, 1, axis + 1)


def setup_inputs(seed: int = 0) -> dict:
    inp = _fwd_setup_inputs(seed)
    key = _jax.random.fold_in(_jax.random.key(seed), 7919)
    shape, _ = _output_shape()
    out = dict(inp)
    out["loss_target"] = _jax.random.normal(_jax.random.fold_in(key, 0), shape, _jnp.float32)
    for i, name in enumerate(TWIN_WEIGHTS):
        w = inp[name].astype(_jnp.float32)
        if MOMENT_SCALE is None:
            s = _jnp.sqrt(_jnp.mean(_jnp.square(w)) + 1e-30)
        else:
            s = MOMENT_SCALE[name]
        km, kv = _jax.random.split(_jax.random.fold_in(key, i + 1))
        out[name] = w
        out["m_" + name] = s * _jax.random.normal(km, w.shape, _jnp.float32)
        out["v_" + name] = (s * s) * _jax.random.uniform(kv, w.shape, _jnp.float32, 0.5, 1.5)
    if N_MICROBATCH > 1:
        for name, axis in PER_EXAMPLE_BATCH_AXIS.items():
            out[name] = _to_microbatches(out[name], axis)
    return {'x': out['x'], 'mem': out['mem'], 'norm_g': out['norm_g'], 'w_in': out['w_in'], 'conv_w': out['conv_w'], 'gdn_a_log': out['gdn_a_log'], 'gdn_dt_bias': out['gdn_dt_bias'], 'gdn_norm_g': out['gdn_norm_g'], 's5_lambda_re': out['s5_lambda_re'], 's5_lambda_im': out['s5_lambda_im'], 's5_log_dt': out['s5_log_dt'], 's5_b_re': out['s5_b_re'], 's5_b_im': out['s5_b_im'], 's5_c_re': out['s5_c_re'], 's5_c_im': out['s5_c_im'], 's5_d': out['s5_d'], 's5_w_glu': out['s5_w_glu'], 'mem_norm_g': out['mem_norm_g'], 'w_kv_mem': out['w_kv_mem'], 'w_br_a': out['w_br_a'], 'w_br_b': out['w_br_b'], 'w_br_c': out['w_br_c'], 'w_out': out['w_out'], 'final_g': out['final_g'], 'loss_target': out['loss_target'], 'm_norm_g': out['m_norm_g'], 'm_w_in': out['m_w_in'], 'm_conv_w': out['m_conv_w'], 'm_gdn_a_log': out['m_gdn_a_log'], 'm_gdn_dt_bias': out['m_gdn_dt_bias'], 'm_gdn_norm_g': out['m_gdn_norm_g'], 'm_s5_lambda_re': out['m_s5_lambda_re'], 'm_s5_lambda_im': out['m_s5_lambda_im'], 'm_s5_log_dt': out['m_s5_log_dt'], 'm_s5_b_re': out['m_s5_b_re'], 'm_s5_b_im': out['m_s5_b_im'], 'm_s5_c_re': out['m_s5_c_re'], 'm_s5_c_im': out['m_s5_c_im'], 'm_s5_d': out['m_s5_d'], 'm_s5_w_glu': out['m_s5_w_glu'], 'm_mem_norm_g': out['m_mem_norm_g'], 'm_w_kv_mem': out['m_w_kv_mem'], 'm_w_br_a': out['m_w_br_a'], 'm_w_br_b': out['m_w_br_b'], 'm_w_br_c': out['m_w_br_c'], 'm_w_out': out['m_w_out'], 'm_final_g': out['m_final_g'], 'v_norm_g': out['v_norm_g'], 'v_w_in': out['v_w_in'], 'v_conv_w': out['v_conv_w'], 'v_gdn_a_log': out['v_gdn_a_log'], 'v_gdn_dt_bias': out['v_gdn_dt_bias'], 'v_gdn_norm_g': out['v_gdn_norm_g'], 'v_s5_lambda_re': out['v_s5_lambda_re'], 'v_s5_lambda_im': out['v_s5_lambda_im'], 'v_s5_log_dt': out['v_s5_log_dt'], 'v_s5_b_re': out['v_s5_b_re'], 'v_s5_b_im': out['v_s5_b_im'], 'v_s5_c_re': out['v_s5_c_re'], 'v_s5_c_im': out['v_s5_c_im'], 'v_s5_d': out['v_s5_d'], 'v_s5_w_glu': out['v_s5_w_glu'], 'v_mem_norm_g': out['v_mem_norm_g'], 'v_w_kv_mem': out['v_w_kv_mem'], 'v_w_br_a': out['v_w_br_a'], 'v_w_br_b': out['v_w_br_b'], 'v_w_br_c': out['v_w_br_c'], 'v_w_out': out['v_w_out'], 'v_final_g': out['v_final_g']}


def _loss(weights, diff, rest, loss_target):
    with _jax.named_scope("forward"):
        args = {**rest, TWIN_DIFF_INPUT: diff, **{k: w.astype(_WEIGHT_DTYPES[k]) for k, w in weights.items()}}
        y = _forward(args)
    with _jax.named_scope("loss_head"):
        err = _jnp.square(y.astype(_jnp.float32) - loss_target)
        return 0.5 * _jnp.sum(_jnp.mean(err, axis=-1)) if err.ndim else 0.5 * err


def _adamw(w, g, m, v):
    m = ADAM_B1 * m + (1.0 - ADAM_B1) * g
    v = ADAM_B2 * v + (1.0 - ADAM_B2) * _jnp.square(g)
    m_hat = m / (1.0 - ADAM_B1 ** ADAM_STEP)
    v_hat = v / (1.0 - ADAM_B2 ** ADAM_STEP)
    delta = -ADAM_LR * (m_hat / (_jnp.sqrt(v_hat) + ADAM_EPS) + ADAM_WD * w)
    return delta, m, v


def reference(x, mem, norm_g, w_in, conv_w, gdn_a_log, gdn_dt_bias, gdn_norm_g, s5_lambda_re, s5_lambda_im, s5_log_dt, s5_b_re, s5_b_im, s5_c_re, s5_c_im, s5_d, s5_w_glu, mem_norm_g, w_kv_mem, w_br_a, w_br_b, w_br_c, w_out, final_g, loss_target, m_norm_g, m_w_in, m_conv_w, m_gdn_a_log, m_gdn_dt_bias, m_gdn_norm_g, m_s5_lambda_re, m_s5_lambda_im, m_s5_log_dt, m_s5_b_re, m_s5_b_im, m_s5_c_re, m_s5_c_im, m_s5_d, m_s5_w_glu, m_mem_norm_g, m_w_kv_mem, m_w_br_a, m_w_br_b, m_w_br_c, m_w_out, m_final_g, v_norm_g, v_w_in, v_conv_w, v_gdn_a_log, v_gdn_dt_bias, v_gdn_norm_g, v_s5_lambda_re, v_s5_lambda_im, v_s5_log_dt, v_s5_b_re, v_s5_b_im, v_s5_c_re, v_s5_c_im, v_s5_d, v_s5_w_glu, v_mem_norm_g, v_w_kv_mem, v_w_br_a, v_w_br_b, v_w_br_c, v_w_out, v_final_g):
    given = dict(x=x, mem=mem, norm_g=norm_g, w_in=w_in, conv_w=conv_w, gdn_a_log=gdn_a_log, gdn_dt_bias=gdn_dt_bias, gdn_norm_g=gdn_norm_g, s5_lambda_re=s5_lambda_re, s5_lambda_im=s5_lambda_im, s5_log_dt=s5_log_dt, s5_b_re=s5_b_re, s5_b_im=s5_b_im, s5_c_re=s5_c_re, s5_c_im=s5_c_im, s5_d=s5_d, s5_w_glu=s5_w_glu, mem_norm_g=mem_norm_g, w_kv_mem=w_kv_mem, w_br_a=w_br_a, w_br_b=w_br_b, w_br_c=w_br_c, w_out=w_out, final_g=final_g, loss_target=loss_target, m_norm_g=m_norm_g, m_w_in=m_w_in, m_conv_w=m_conv_w, m_gdn_a_log=m_gdn_a_log, m_gdn_dt_bias=m_gdn_dt_bias, m_gdn_norm_g=m_gdn_norm_g, m_s5_lambda_re=m_s5_lambda_re, m_s5_lambda_im=m_s5_lambda_im, m_s5_log_dt=m_s5_log_dt, m_s5_b_re=m_s5_b_re, m_s5_b_im=m_s5_b_im, m_s5_c_re=m_s5_c_re, m_s5_c_im=m_s5_c_im, m_s5_d=m_s5_d, m_s5_w_glu=m_s5_w_glu, m_mem_norm_g=m_mem_norm_g, m_w_kv_mem=m_w_kv_mem, m_w_br_a=m_w_br_a, m_w_br_b=m_w_br_b, m_w_br_c=m_w_br_c, m_w_out=m_w_out, m_final_g=m_final_g, v_norm_g=v_norm_g, v_w_in=v_w_in, v_conv_w=v_conv_w, v_gdn_a_log=v_gdn_a_log, v_gdn_dt_bias=v_gdn_dt_bias, v_gdn_norm_g=v_gdn_norm_g, v_s5_lambda_re=v_s5_lambda_re, v_s5_lambda_im=v_s5_lambda_im, v_s5_log_dt=v_s5_log_dt, v_s5_b_re=v_s5_b_re, v_s5_b_im=v_s5_b_im, v_s5_c_re=v_s5_c_re, v_s5_c_im=v_s5_c_im, v_s5_d=v_s5_d, v_s5_w_glu=v_s5_w_glu, v_mem_norm_g=v_mem_norm_g, v_w_kv_mem=v_w_kv_mem, v_w_br_a=v_w_br_a, v_w_br_b=v_w_br_b, v_w_br_c=v_w_br_c, v_w_out=v_w_out, v_final_g=v_final_g)
    weights = {n: given[n] for n in TWIN_WEIGHTS}
    shared = {n: given[n] for n in SHARED_INPUTS}
    per_example = {n: given[n] for n in ['x', 'mem']}
    grad_fn = _jax.value_and_grad(_loss, argnums=(0, 1))

    def one_microbatch(ex, loss_target):
        ex = dict(ex)
        diff = ex.pop(TWIN_DIFF_INPUT)
        return grad_fn(weights, diff, {**shared, **ex}, loss_target)

    if N_MICROBATCH == 1:
        loss, (grad_w, grad_x) = one_microbatch(per_example, given["loss_target"])
    else:
        def body(carry, xs):
            loss_sum, grad_sum = carry
            l_k, (gw_k, gx_k) = one_microbatch(xs[0], xs[1])
            with _jax.named_scope("update"):
                return (loss_sum + l_k, _jax.tree.map(_jnp.add, grad_sum, gw_k)), gx_k

        init = (_jnp.zeros((), _jnp.float32), _jax.tree.map(_jnp.zeros_like, weights))
        (loss, grad_w), grad_x = _jax.lax.scan(body, init, (per_example, given["loss_target"]))
    with _jax.named_scope("update"):
        delta_w, new_m, new_v = {}, {}, {}
        for n in TWIN_WEIGHTS:
            delta_w[n], new_m[n], new_v[n] = _adamw(weights[n], grad_w[n], given["m_" + n], given["v_" + n])
    return (loss, grad_x, *[grad_w[n] for n in TWIN_WEIGHTS], *[delta_w[n] for n in TWIN_WEIGHTS],
            *[new_m[n] for n in TWIN_WEIGHTS], *[new_v[n] for n in TWIN_WEIGHTS])
```

```python
import functools
import math

import jax
import jax.numpy as jnp
from jax import lax
from jax.experimental import pallas as pl
from jax.experimental.pallas import tpu as pltpu

F32 = jnp.float32
BF16 = jnp.bfloat16
HI = lax.Precision.HIGHEST

EPS = 1e-6
CHUNK = 64
GDN_HEAD_DIM = 128
CONV_WIDTH = 4
S5_GROUP = 16
S5_STATE = 64
S5_GROUPS_PER_BLOCK = 8
XA_HEADS = 4
N_DEV = 8
ADAM_LR, ADAM_B1, ADAM_B2, ADAM_EPS, ADAM_WD, ADAM_STEP = 0.001, 0.9, 0.999, 1e-08, 0.01, 10

VMEM_LIMIT_BYTES = 56 * 1024 * 1024
SCAN_LANES = 512
PACK_WIDTH = 512
PACK_ROWS = 256

WEIGHTS = ['norm_g', 'w_in', 'conv_w', 'gdn_a_log', 'gdn_dt_bias', 'gdn_norm_g', 's5_lambda_re', 's5_lambda_im',
           's5_log_dt', 's5_b_re', 's5_b_im', 's5_c_re', 's5_c_im', 's5_d', 's5_w_glu', 'mem_norm_g', 'w_kv_mem',
           'w_br_a', 'w_br_b', 'w_br_c', 'w_out', 'final_g']
SHARDED = {'w_in': 'col', 'conv_w': 'col', 's5_w_glu': 'col', 'w_kv_mem': 'row', 'w_br_a': 'col', 'w_br_b': 'col',
           'w_br_c': 'col', 'w_out': 'row'}
GATHER_BF16 = ['w_in', 's5_w_glu', 'w_kv_mem', 'w_br_a', 'w_br_b', 'w_br_c', 'w_out']
REPLICATED = [n for n in WEIGHTS if n not in SHARDED]


def _cparams(sem=None):
    return pltpu.CompilerParams(dimension_semantics=sem, vmem_limit_bytes=VMEM_LIMIT_BYTES)


def _pick(dim, pref):
    t = (min(pref, dim) // 128) * 128
    while t >= 128:
        if dim % t == 0:
            return t
        t -= 128
    return dim


def _make_dots(prep, precision):
    def raw(a, b, dims):
        return lax.dot_general(prep(a), prep(b), (dims, ((), ())), preferred_element_type=F32, precision=precision)

    @jax.custom_vjp
    def nn(a, b):
        return raw(a, b, ((1,), (0,)))

    @jax.custom_vjp
    def nt(a, b):
        return raw(a, b, ((1,), (1,)))

    @jax.custom_vjp
    def tn(a, b):
        return raw(a, b, ((0,), (0,)))

    nn.defvjp(lambda a, b: (nn(a, b), (a, b)), lambda r, ct: (nt(ct, r[1]), tn(r[0], ct)))
    nt.defvjp(lambda a, b: (nt(a, b), (a, b)), lambda r, ct: (nn(ct, r[1]), tn(ct, r[0])))
    tn.defvjp(lambda a, b: (tn(a, b), (a, b)), lambda r, ct: (nt(r[1], ct), nn(r[0], ct)))
    return nn, nt, tn


_bnn, _bnt, _btn = _make_dots(lambda a: a.astype(BF16), None)
_hnn, _hnt, _htn = _make_dots(lambda a: a.astype(F32), HI)


def _mm(a, b, *, name, ta=False, tb=False, out_dtype=F32, addend=None, tm=512, tn=1024, tk=1024):
    m, k = (a.shape[1], a.shape[0]) if ta else a.shape
    n = b.shape[0] if tb else b.shape[1]
    assert (b.shape[1] if tb else b.shape[0]) == k, (a.shape, b.shape, ta, tb)
    tm, tn, tk = _pick(m, tm), _pick(n, tn), _pick(k, tk)
    nk = k // tk
    dims = ((0 if ta else 1,), (1 if tb else 0,))

    def body(*refs):
        if addend is None:
            a_ref, b_ref, o_ref, acc_ref = refs
        else:
            a_ref, b_ref, add_ref, o_ref, acc_ref = refs
        kk = pl.program_id(2)

        @pl.when(kk == 0)
        def _():
            acc_ref[...] = jnp.zeros_like(acc_ref)

        acc_ref[...] += lax.dot_general(a_ref[...].astype(BF16), b_ref[...].astype(BF16), (dims, ((), ())),
                                        preferred_element_type=F32)

        @pl.when(kk == nk - 1)
        def _():
            r = acc_ref[...]
            if addend is not None:
                r = r + add_ref[...].astype(F32)
            o_ref[...] = r.astype(o_ref.dtype)

    a_spec = pl.BlockSpec((tk, tm), lambda i, j, kk: (kk, i)) if ta else pl.BlockSpec((tm, tk), lambda i, j, kk: (i, kk))
    b_spec = pl.BlockSpec((tn, tk), lambda i, j, kk: (j, kk)) if tb else pl.BlockSpec((tk, tn), lambda i, j, kk: (kk, j))
    o_spec = pl.BlockSpec((tm, tn), lambda i, j, kk: (i, j))
    in_specs = [a_spec, b_spec] + ([o_spec] if addend is not None else [])
    args = (a, b) + ((addend,) if addend is not None else ())
    return pl.pallas_call(
        body, name=name, grid=(m // tm, n // tn, nk), in_specs=in_specs, out_specs=o_spec,
        out_shape=jax.ShapeDtypeStruct((m, n), out_dtype), scratch_shapes=[pltpu.VMEM((tm, tn), F32)],
        compiler_params=_cparams(("parallel", "parallel", "arbitrary")))(*args)


def _rt(arr, ts, cb=0, w=None):
    w = arr.shape[1] if w is None else w
    return (arr, (ts, w), lambda i, cb=cb: (i, cb))


def _whole(p):
    return pl.BlockSpec(p.shape, lambda i, nd=p.ndim: (0,) * nd)


def _tile_fwd(f, name, n, acts, params, outs):
    na, npar = len(acts), len(params)

    def body(*refs):
        res = f(*[r[...] for r in refs[:na + npar]])
        for r, v in zip(refs[na + npar:], res):
            r[...] = v.astype(r.dtype)

    in_specs = [pl.BlockSpec(b, m) for _, b, m in acts] + [_whole(p) for p in params]
    out = pl.pallas_call(
        body, name=name, grid=(n,), in_specs=in_specs,
        out_specs=[pl.BlockSpec(b, m) for _, _, b, m in outs],
        out_shape=[jax.ShapeDtypeStruct(s, d) for s, d, _, _ in outs],
        compiler_params=_cparams(("parallel",)))(*[a for a, _, _ in acts], *params)
    return out


def _tile_bwd(f, name, n, acts, params, cts, agrads, pgrads):
    na, npar, nc = len(acts), len(params), len(cts)

    def body(*refs):
        i = pl.program_id(0)
        ins = [r[...] for r in refs[:na + npar]]
        outs, vjp = jax.vjp(f, *ins)
        g = vjp(tuple(c[...].astype(o.dtype) for c, o in zip(refs[na + npar:na + npar + nc], outs)))
        orefs = refs[na + npar + nc:]
        k = 0
        for j in range(na):
            if agrads[j] is not None:
                orefs[k][...] = g[j].astype(orefs[k].dtype)
                k += 1
        for j in range(npar):
            if pgrads[j]:
                o = orefs[k]

                @pl.when(i == 0)
                def _(o=o):
                    o[...] = jnp.zeros_like(o)

                o[...] += g[na + j].astype(F32)
                k += 1

    in_specs = ([pl.BlockSpec(b, m) for _, b, m in acts] + [_whole(p) for p in params]
                + [pl.BlockSpec(b, m) for _, b, m in cts])
    out_specs = [pl.BlockSpec(g[2], g[3]) for g in agrads if g is not None]
    out_shape = [jax.ShapeDtypeStruct(g[0], g[1]) for g in agrads if g is not None]
    for p, flag in zip(params, pgrads):
        if flag:
            out_specs.append(_whole(p))
            out_shape.append(jax.ShapeDtypeStruct(p.shape, F32))
    return pl.pallas_call(
        body, name=name, grid=(n,), in_specs=in_specs, out_specs=out_specs, out_shape=out_shape,
        compiler_params=_cparams(("arbitrary",)))(*[a for a, _, _ in acts], *params, *[c for c, _, _ in cts])


def _silu(x):
    return x * jax.nn.sigmoid(x)


def _rms(x, g):
    x = x.astype(F32)
    return (x * lax.rsqrt(jnp.mean(x * x, axis=-1, keepdims=True) + EPS) * g,)


def _shift_down(x, s):
    row = lax.broadcasted_iota(jnp.int32, x.shape, 0)
    return jnp.where(row >= s, pltpu.roll(x, s, 0), 0.0)


def _shift_up(x, s):
    n = x.shape[0]
    row = lax.broadcasted_iota(jnp.int32, x.shape, 0)
    return jnp.where(row < n - s, pltpu.roll(x, n - s, 0), 0.0)


@functools.partial(jax.custom_vjp, nondiff_argnums=(1,))
def _shift(x, s):
    return _shift_down(x, s)


_shift.defvjp(lambda x, s: (_shift_down(x, s), None), lambda s, _, ct: (_shift_up(ct, s),))


def _gdn_pre(mode):
    def f(x, w):
        y = x * w[CONV_WIDTH - 1:CONV_WIDTH, :]
        for j in range(CONV_WIDTH - 1):
            y = y + _shift(x, CONV_WIDTH - 1 - j) * w[j:j + 1, :]
        y = _silu(y)
        if mode != 'v':
            y = y * lax.rsqrt(jnp.sum(y * y, axis=-1, keepdims=True) + EPS)
        if mode == 'q':
            y = y * (GDN_HEAD_DIM ** -0.5)
        return (y,)
    return f


def _softplus(x):
    return jnp.maximum(x, 0.0) + jnp.log1p(jnp.exp(-jnp.abs(x)))


def _gdn_gates(ba, alog, dtb, e_beta, e_g):
    beta = jax.nn.sigmoid(ba)
    g = -jnp.exp(alog) * _softplus(ba + dtb)
    return _hnn(beta, lax.stop_gradient(e_beta)), _hnn(g, lax.stop_gradient(e_g))


def _gdn_chunk(q, k, v, gb, bb, state):
    c = q.shape[0]
    ri = lax.broadcasted_iota(jnp.int32, (c, c), 0)
    ci = lax.broadcasted_iota(jnp.int32, (c, c), 1)
    incl, strict = ri >= ci, ri > ci
    tri = incl.astype(F32)
    eye = (ri == ci).astype(F32)
    gc = _hnn(tri, gb)
    diff = jnp.where(incl, gc[:, :c] - gc.T[:c, :], -1e30)
    decay = jnp.exp(diff)
    kb = k * bb
    neg = jnp.where(strict, -(_bnt(kb, k) * decay), 0.0)
    t = eye + neg
    p = neg
    for _ in range(int(math.log2(c)) - 1):
        p = _hnn(p, p)
        t = t + _hnn(t, p)
    egc = jnp.exp(gc)
    u_val = _hnn(t, v * bb)
    w_dec = _hnn(t, kb * egc)
    qk = _bnt(q, k) * decay
    gl = jnp.sum(gb, axis=0, keepdims=True)
    k_dec = k * jnp.exp(gl - gc)
    v_new = u_val - _bnn(w_dec, state)
    o = _bnn(q * egc, state) + _bnn(qk, v_new)
    return o, state * jnp.exp(gl) + _btn(k_dec, v_new)


def _gdn_post(o, z, g):
    parts = []
    for h in range(o.shape[1] // GDN_HEAD_DIM):
        oh = o[:, h * GDN_HEAD_DIM:(h + 1) * GDN_HEAD_DIM]
        parts.append(oh * lax.rsqrt(jnp.mean(oh * oh, axis=-1, keepdims=True) + EPS) * g)
    y = parts[0] if len(parts) == 1 else jnp.concatenate(parts, axis=1)
    return (y * _silu(z),)


def _gelu(x):
    return 0.5 * x * (1.0 + jnp.tanh(0.7978845608028654 * (x + 0.044715 * x * x * x)))


def _s5_post1(ylin, xb, d):
    return (_gelu(ylin + d * xb),)


def _s5_post2(t, z):
    w = t.shape[1] // 2
    return (t[:, :w] * jax.nn.sigmoid(t[:, w:]) * _silu(z),)


def _attn(q, z, kv):
    w = q.shape[1]
    hd = w // XA_HEADS
    parts = []
    for h in range(XA_HEADS):
        s = _bnt(q[:, h * hd:(h + 1) * hd], kv[:, h * hd:(h + 1) * hd]) * (hd ** -0.5)
        s = s - jnp.max(s, axis=-1, keepdims=True)
        e = jnp.exp(s)
        p = e / jnp.sum(e, axis=-1, keepdims=True)
        parts.append(_bnn(p, kv[:, w + h * hd:w + (h + 1) * hd]))
    return (jnp.concatenate(parts, axis=1) * _silu(z),)


def _merge(g0, g1, g2, pa, pb, pc):
    return (jax.nn.sigmoid(g0) * pa + jax.nn.sigmoid(g1) * pb + jax.nn.sigmoid(g2) * pc,)


def _s5_params(lr, li, logdt, br, bi, e):
    dt = jnp.exp(logdt)
    mag = jnp.exp(lr * dt)
    ab_re, ab_im = mag * jnp.cos(li * dt), mag * jnp.sin(li * dt)
    den = lr * lr + li * li
    nr, ni = ab_re - 1.0, ab_im
    e = lax.stop_gradient(e)
    cre = _hnn((nr * lr + ni * li) / den, e)
    cim = _hnn((ni * lr - nr * li) / den, e)
    return ab_re, ab_im, cre * br - cim * bi, cre * bi + cim * br


def _gdn_fwd(q, k, v, gb, bb):
    s, w = q.shape
    nh, nc = w // GDN_HEAD_DIM, s // CHUNK
    hd = GDN_HEAD_DIM

    def body(q_ref, k_ref, v_ref, g_ref, b_ref, o_ref, st_ref, state):
        @pl.when(pl.program_id(0) == 0)
        def _():
            state[...] = jnp.zeros_like(state)

        for h in range(nh):
            sl = slice(h * hd, (h + 1) * hd)
            st = state[h]
            st_ref[0, h] = st
            o, new = _gdn_chunk(q_ref[:, sl], k_ref[:, sl], v_ref[:, sl], g_ref[:, sl], b_ref[:, sl], st)
            o_ref[:, sl] = o
            state[h] = new

    blk = pl.BlockSpec((CHUNK, w), lambda i: (i, 0))
    return pl.pallas_call(
        body, name="gdn_fwd", grid=(nc,), in_specs=[blk] * 5,
        out_specs=[blk, pl.BlockSpec((1, nh, hd, hd), lambda i: (i, 0, 0, 0))],
        out_shape=[jax.ShapeDtypeStruct((s, w), F32), jax.ShapeDtypeStruct((nc, nh, hd, hd), F32)],
        scratch_shapes=[pltpu.VMEM((nh, hd, hd), F32)], compiler_params=_cparams(("arbitrary",)))(q, k, v, gb, bb)


def _gdn_bwd(q, k, v, gb, bb, states, do):
    s, w = q.shape
    nh, nc = w // GDN_HEAD_DIM, s // CHUNK
    hd = GDN_HEAD_DIM

    def body(q_ref, k_ref, v_ref, g_ref, b_ref, st_ref, do_ref, dq_ref, dk_ref, dv_ref, dg_ref, db_ref, dstate):
        @pl.when(pl.program_id(0) == 0)
        def _():
            dstate[...] = jnp.zeros_like(dstate)

        for h in range(nh):
            sl = slice(h * hd, (h + 1) * hd)
            _, vjp = jax.vjp(_gdn_chunk, q_ref[:, sl], k_ref[:, sl], v_ref[:, sl], g_ref[:, sl], b_ref[:, sl],
                             st_ref[0, h])
            dq, dk, dv, dg, db, dst = vjp((do_ref[:, sl], dstate[h]))
            dq_ref[:, sl] = dq
            dk_ref[:, sl] = dk
            dv_ref[:, sl] = dv
            dg_ref[:, sl] = dg
            db_ref[:, sl] = db
            dstate[h] = dst

    blk = pl.BlockSpec((CHUNK, w), lambda i: (nc - 1 - i, 0))
    return pl.pallas_call(
        body, name="gdn_bwd", grid=(nc,),
        in_specs=[blk] * 5 + [pl.BlockSpec((1, nh, hd, hd), lambda i: (nc - 1 - i, 0, 0, 0)), blk],
        out_specs=[blk] * 5, out_shape=[jax.ShapeDtypeStruct((s, w), F32)] * 5,
        scratch_shapes=[pltpu.VMEM((nh, hd, hd), F32)],
        compiler_params=_cparams(("arbitrary",)))(q, k, v, gb, bb, states, do)


def _s5_coef(ar, ai):
    nl = ar.shape[1]

    def body(ar_ref, ai_ref, o_ref):
        row = lax.broadcasted_iota(jnp.int32, (8, nl), 0)
        for base, sign in ((0, 1.0), (8, -1.0)):
            pr = [jnp.broadcast_to(ar_ref[...], (8, nl))]
            pi = [jnp.broadcast_to(ai_ref[...], (8, nl)) * sign]
            for _ in range(7):
                pr.append(pr[-1] * pr[0] - pi[-1] * pi[0])
                pi.append(pr[-2] * pi[0] + pi[-1] * pr[0])
            for j, d in enumerate((1, 2, 4)):
                m = (row >= d) if base == 0 else (row <= 7 - d)
                o_ref[base + 2 * j] = jnp.where(m, pr[d - 1], 0.0)
                o_ref[base + 2 * j + 1] = jnp.where(m, pi[d - 1], 0.0)
            cr, ci = jnp.zeros((8, nl), F32), jnp.zeros((8, nl), F32)
            for t in range(8):
                e = t if base == 0 else 7 - t
                cr = jnp.where(row == t, pr[e], cr)
                ci = jnp.where(row == t, pi[e], ci)
            o_ref[base + 6] = cr
            o_ref[base + 7] = ci

    return pl.pallas_call(body, name="s5_coef", out_shape=jax.ShapeDtypeStruct((16, 8, nl), F32),
                          compiler_params=_cparams())(ar, ai)


def _scan_tile(src_re, src_im, dst_re, dst_im, coef_ref, carry_re, carry_im, ts, reverse, extra=None):
    nl = src_re.shape[1]
    base = 8 if reverse else 0
    ng = ts // 8
    for lc in range(nl // SCAN_LANES):
        ln = slice(lc * SCAN_LANES, (lc + 1) * SCAN_LANES)
        m = [coef_ref[base + j, :, ln] for j in range(8)]
        row = lax.broadcasted_iota(jnp.int32, (8, SCAN_LANES), 0)

        def step(r, carry, ln=ln, m=m, row=row):
            grp = (ng - 1 - r) if reverse else r
            rows = pl.ds(pl.multiple_of(grp * 8, 8), 8)
            xr, xi = src_re[rows, ln], src_im[rows, ln]
            for j, d in enumerate((1, 2, 4)):
                sh = 8 - d if reverse else d
                sr, si = pltpu.roll(xr, sh, 0), pltpu.roll(xi, sh, 0)
                mr, mi = m[2 * j], m[2 * j + 1]
                xr, xi = xr + mr * sr - mi * si, xi + mr * si + mi * sr
            cr, ci = carry[0], carry[1]
            hr = xr + m[6] * cr - m[7] * ci
            hi = xi + m[6] * ci + m[7] * cr
            dst_re[rows, ln] = hr
            dst_im[rows, ln] = hi
            edge = 0 if reverse else 7
            out = (jnp.broadcast_to(hr[edge:edge + 1, :], hr.shape), jnp.broadcast_to(hi[edge:edge + 1, :], hi.shape))
            if extra is not None:
                h_re, h_im, halo_re, halo_im, first, _, _ = extra
                prev = pl.ds(pl.multiple_of(jnp.maximum(grp - 1, 0) * 8, 8), 8)
                use_halo = grp == 0
                pr = jnp.where(use_halo, halo_re[:, ln] * first, h_re[prev, ln])
                pi = jnp.where(use_halo, halo_im[:, ln] * first, h_im[prev, ln])
                qr = jnp.where(row == 0, jnp.broadcast_to(pr[7:8, :], pr.shape), pltpu.roll(h_re[rows, ln], 1, 0))
                qi = jnp.where(row == 0, jnp.broadcast_to(pi[7:8, :], pi.shape), pltpu.roll(h_im[rows, ln], 1, 0))
                out = out + (carry[2] + hr * qr + hi * qi, carry[3] + hi * qr - hr * qi)
            return out

        init = (carry_re[:, ln], carry_im[:, ln])
        if extra is not None:
            init = init + (extra[5][:, ln], extra[6][:, ln])
        fin = lax.fori_loop(0, ng, step, init)
        carry_re[:, ln] = fin[0]
        carry_im[:, ln] = fin[1]
        if extra is not None:
            extra[5][:, ln] = fin[2]
            extra[6][:, ln] = fin[3]


def _s5_fwd(xb, bb_re, bb_im, c_re, c_im, coef, ts):
    s, w = xb.shape
    nb = bb_re.shape[0]
    nl = nb * 512

    def body(x_ref, bre_ref, bim_ref, cre_ref, cim_ref, coef_ref, hre_ref, him_ref, y_ref, ure, uim, car_re, car_im):
        @pl.when(pl.program_id(0) == 0)
        def _():
            car_re[...] = jnp.zeros_like(car_re)
            car_im[...] = jnp.zeros_like(car_im)

        for b in range(nb):
            xs = x_ref[:, b * 128:(b + 1) * 128].astype(BF16)
            ure[:, b * 512:(b + 1) * 512] = jnp.dot(xs, bre_ref[b], preferred_element_type=F32)
            uim[:, b * 512:(b + 1) * 512] = jnp.dot(xs, bim_ref[b], preferred_element_type=F32)
        _scan_tile(ure, uim, hre_ref, him_ref, coef_ref, car_re, car_im, ts, False)
        for b in range(nb):
            hr = hre_ref[:, b * 512:(b + 1) * 512].astype(BF16)
            hi = him_ref[:, b * 512:(b + 1) * 512].astype(BF16)
            y_ref[:, b * 128:(b + 1) * 128] = (jnp.dot(hr, cre_ref[b], preferred_element_type=F32)
                                               - jnp.dot(hi, cim_ref[b], preferred_element_type=F32))

    row = lambda wd: pl.BlockSpec((ts, wd), lambda i: (i, 0))
    return pl.pallas_call(
        body, name="s5_fwd", grid=(s // ts,),
        in_specs=[row(w), _whole(bb_re), _whole(bb_im), _whole(c_re), _whole(c_im), _whole(coef)],
        out_specs=[row(nl), row(nl), row(w)],
        out_shape=[jax.ShapeDtypeStruct((s, nl), F32), jax.ShapeDtypeStruct((s, nl), F32),
                   jax.ShapeDtypeStruct((s, w), F32)],
        scratch_shapes=[pltpu.VMEM((ts, nl), F32), pltpu.VMEM((ts, nl), F32), pltpu.VMEM((8, nl), F32),
                        pltpu.VMEM((8, nl), F32)],
        compiler_params=_cparams(("arbitrary",)))(xb, bb_re, bb_im, c_re, c_im, coef)


def _s5_bwd(dy, xb, h_re, h_im, bb_re, bb_im, c_re, c_im, coef, ts):
    s, w = xb.shape
    nb = bb_re.shape[0]
    nl = nb * 512
    nt = s // ts

    def body(dy_ref, x_ref, hre_ref, him_ref, halo_re, halo_im, bre_ref, bim_ref, cre_ref, cim_ref, coef_ref,
             dx_ref, dbre_ref, dbim_ref, dcre_ref, dcim_ref, dare_ref, daim_ref, gre, gim, car_re, car_im):
        i = pl.program_id(0)

        @pl.when(i == 0)
        def _():
            for r in (car_re, car_im, dbre_ref, dbim_ref, dcre_ref, dcim_ref, dare_ref, daim_ref):
                r[...] = jnp.zeros_like(r)

        for b in range(nb):
            dyb = dy_ref[:, b * 128:(b + 1) * 128].astype(BF16)
            gre[:, b * 512:(b + 1) * 512] = lax.dot_general(dyb, cre_ref[b], (((1,), (1,)), ((), ())),
                                                            preferred_element_type=F32)
            gim[:, b * 512:(b + 1) * 512] = -lax.dot_general(dyb, cim_ref[b], (((1,), (1,)), ((), ())),
                                                             preferred_element_type=F32)
            hr = hre_ref[:, b * 512:(b + 1) * 512].astype(BF16)
            hi = him_ref[:, b * 512:(b + 1) * 512].astype(BF16)
            dcre_ref[b] += lax.dot_general(hr, dyb, (((0,), (0,)), ((), ())), preferred_element_type=F32)
            dcim_ref[b] -= lax.dot_general(hi, dyb, (((0,), (0,)), ((), ())), preferred_element_type=F32)
        first = (i != nt - 1).astype(F32)
        _scan_tile(gre, gim, gre, gim, coef_ref, car_re, car_im, ts, True,
                   extra=(hre_ref, him_ref, halo_re, halo_im, first, dare_ref, daim_ref))
        for b in range(nb):
            gr = gre[:, b * 512:(b + 1) * 512].astype(BF16)
            gi = gim[:, b * 512:(b + 1) * 512].astype(BF16)
            xs = x_ref[:, b * 128:(b + 1) * 128].astype(BF16)
            dx_ref[:, b * 128:(b + 1) * 128] = (
                lax.dot_general(gr, bre_ref[b], (((1,), (1,)), ((), ())), preferred_element_type=F32)
                + lax.dot_general(gi, bim_ref[b], (((1,), (1,)), ((), ())), preferred_element_type=F32))
            dbre_ref[b] += lax.dot_general(xs, gr, (((0,), (0,)), ((), ())), preferred_element_type=F32)
            dbim_ref[b] += lax.dot_general(xs, gi, (((0,), (0,)), ((), ())), preferred_element_type=F32)

    row = lambda wd: pl.BlockSpec((ts, wd), lambda i: (nt - 1 - i, 0))
    halo = pl.BlockSpec((8, nl), lambda i: (jnp.maximum((nt - 1 - i) * (ts // 8) - 1, 0), 0))
    return pl.pallas_call(
        body, name="s5_bwd", grid=(nt,),
        in_specs=[row(w), row(w), row(nl), row(nl), halo, halo, _whole(bb_re), _whole(bb_im), _whole(c_re),
                  _whole(c_im), _whole(coef)],
        out_specs=[row(w), _whole(bb_re), _whole(bb_im), _whole(c_re), _whole(c_im),
                   pl.BlockSpec((8, nl), lambda i: (0, 0)), pl.BlockSpec((8, nl), lambda i: (0, 0))],
        out_shape=[jax.ShapeDtypeStruct((s, w), F32), jax.ShapeDtypeStruct(bb_re.shape, F32),
                   jax.ShapeDtypeStruct(bb_im.shape, F32), jax.ShapeDtypeStruct(c_re.shape, F32),
                   jax.ShapeDtypeStruct(c_im.shape, F32), jax.ShapeDtypeStruct((8, nl), F32),
                   jax.ShapeDtypeStruct((8, nl), F32)],
        scratch_shapes=[pltpu.VMEM((ts, nl), F32), pltpu.VMEM((ts, nl), F32), pltpu.VMEM((8, nl), F32),
                        pltpu.VMEM((8, nl), F32)],
        compiler_params=_cparams(("arbitrary",)))(dy, xb, h_re, h_im, h_re, h_im, bb_re, bb_im, c_re, c_im, coef)


def _final(x, mo, target, fg, ts):
    s, d = x.shape

    def f(x, mo, fg, tgt):
        y = _rms(x + mo, fg)[0]
        err = y - tgt
        return 0.5 * jnp.sum(jnp.mean(err * err, axis=-1, keepdims=True), axis=0, keepdims=True)

    def body(x_ref, mo_ref, t_ref, fg_ref, dh_ref, dfg_ref, loss_ref):
        @pl.when(pl.program_id(0) == 0)
        def _():
            dfg_ref[...] = jnp.zeros_like(dfg_ref)
            loss_ref[...] = jnp.zeros_like(loss_ref)

        loss, vjp = jax.vjp(f, x_ref[...], mo_ref[...], fg_ref[...], t_ref[...])
        _, dmo, dfg, _ = vjp(jnp.ones((1, 1), F32))
        dh_ref[...] = dmo
        dfg_ref[...] += dfg
        loss_ref[...] += jnp.broadcast_to(loss, loss_ref.shape)

    row = pl.BlockSpec((ts, d), lambda i: (i, 0))
    return pl.pallas_call(
        body, name="final", grid=(s // ts,), in_specs=[row, row, row, _whole(fg)],
        out_specs=[row, _whole(fg), pl.BlockSpec((8, 128), lambda i: (0, 0))],
        out_shape=[jax.ShapeDtypeStruct((s, d), F32), jax.ShapeDtypeStruct(fg.shape, F32),
                   jax.ShapeDtypeStruct((8, 128), F32)],
        compiler_params=_cparams(("arbitrary",)))(x, mo, target, fg)


def _peers(x, y, c):
    out = []
    for k in range(1, N_DEV):
        px = 1 - x if k & 4 else x
        py = 1 - y if k & 2 else y
        pc = 1 - c if k & 1 else c
        out.append(((px, py, pc), 4 * px + 2 * py + pc))
    return out


def _exchange(src, name, scatter):
    blk = src.shape[1:]

    def body(src_ref, out_ref, send_sems, recv_sems, local_sem):
        x, y, c = lax.axis_index("x"), lax.axis_index("y"), lax.axis_index("c")
        me = 4 * x + 2 * y + c
        mine = pltpu.make_async_copy(src_ref.at[me if scatter else 0], out_ref.at[me], local_sem)
        mine.start()
        sends = []
        for k, (pos, idx) in enumerate(_peers(x, y, c)):
            cp = pltpu.make_async_remote_copy(
                src_ref=src_ref.at[idx if scatter else 0], dst_ref=out_ref.at[me], send_sem=send_sems.at[k],
                recv_sem=recv_sems.at[k], device_id=pos, device_id_type=pl.DeviceIdType.MESH)
            cp.start()
            sends.append(cp)
        for k, (pos, idx) in enumerate(_peers(x, y, c)):
            pltpu.make_async_remote_copy(
                src_ref=src_ref.at[0], dst_ref=out_ref.at[idx], send_sem=send_sems.at[k], recv_sem=recv_sems.at[k],
                device_id=pos, device_id_type=pl.DeviceIdType.MESH).wait_recv()
        for cp in sends:
            cp.wait_send()
        mine.wait()

    return pl.pallas_call(
        body, name=name, in_specs=[pl.BlockSpec(memory_space=pl.ANY)],
        out_specs=pl.BlockSpec(memory_space=pl.ANY), out_shape=jax.ShapeDtypeStruct((N_DEV,) + blk, src.dtype),
        scratch_shapes=[pltpu.SemaphoreType.DMA((N_DEV - 1,)), pltpu.SemaphoreType.DMA((N_DEV - 1,)),
                        pltpu.SemaphoreType.DMA],
        compiler_params=pltpu.CompilerParams(has_side_effects=True))(src)


def _pack(arrs, dtype, lead=()):
    nlead = len(lead)
    flat = jnp.concatenate([a.astype(dtype).reshape(lead + (-1,)) for a in arrs], axis=nlead)
    n = flat.shape[-1]
    unit = PACK_WIDTH * PACK_ROWS
    pad = (-n) % unit
    flat = jnp.pad(flat, [(0, 0)] * nlead + [(0, pad)])
    return flat.reshape(lead + ((n + pad) // PACK_WIDTH, PACK_WIDTH))


def _unpack(buf, shapes, lead=()):
    flat = buf.reshape(lead + (-1,))
    out, off = [], 0
    for shp in shapes:
        n = math.prod(shp)
        out.append(flat[..., off:off + n].reshape(lead + tuple(shp)))
        off += n
    return out


def _adam_math(w, g, m, v):
    m = ADAM_B1 * m + (1.0 - ADAM_B1) * g
    v = ADAM_B2 * v + (1.0 - ADAM_B2) * (g * g)
    m_hat = m / (1.0 - ADAM_B1 ** ADAM_STEP)
    v_hat = v / (1.0 - ADAM_B2 ** ADAM_STEP)
    delta = -ADAM_LR * (m_hat / (jnp.sqrt(v_hat) + ADAM_EPS) + ADAM_WD * w)
    return delta, m, v


def _sum_adam(parts, w, m, v, name):
    r, c = w.shape
    tr = PACK_ROWS

    def body(p_ref, w_ref, m_ref, v_ref, g_ref, d_ref, nm_ref, nv_ref):
        g = p_ref[0]
        for j in range(1, N_DEV):
            g = g + p_ref[j]
        d, nm, nv = _adam_math(w_ref[...], g, m_ref[...], v_ref[...])
        g_ref[...] = g
        d_ref[...] = d
        nm_ref[...] = nm
        nv_ref[...] = nv

    row = pl.BlockSpec((tr, c), lambda i: (i, 0))
    return pl.pallas_call(
        body, name=name, grid=(r // tr,), in_specs=[pl.BlockSpec((N_DEV, tr, c), lambda i: (0, i, 0)), row, row, row],
        out_specs=[row] * 4, out_shape=[jax.ShapeDtypeStruct((r, c), F32)] * 4,
        compiler_params=_cparams(("parallel",)))(parts, w, m, v)


def _block_diag(t):
    nb, g, a, b = t.shape
    eye = jnp.eye(g, dtype=t.dtype)
    return jnp.einsum('ngab,gh->ngahb', t, eye).reshape(nb, g * a, g * b)


def _diag_blocks(t, a, b):
    nb = t.shape[0]
    g = S5_GROUPS_PER_BLOCK
    t = t.reshape(nb, g, a, g, b)
    return jnp.stack([t[:, j, :, j, :] for j in range(g)], axis=1)


def _local_step(x, mem, target, p):
    s, d = x.shape
    gw = d // 2
    nh = gw // GDN_HEAD_DIM
    ng = gw // S5_GROUP
    nb = ng // S5_GROUPS_PER_BLOCK
    nl = ng * S5_STATE
    ts = min(256, s)
    nt = s // ts
    grads = {}

    w_in = p['w_in']
    w_main = jnp.concatenate([w_in[:, :4 * gw], w_in[:, 4 * gw + 2 * nh:]], axis=1).astype(BF16)
    w_ba = jnp.pad(w_in[:, 4 * gw:4 * gw + 2 * nh], ((0, 0), (0, 128 - 2 * nh))).astype(BF16)
    CB_QKV, CB_ZA, CB_XB, CB_ZB, CB_QC, CB_ZC, CB_G = 0, 3, 4, 5, 6, 7, 8

    u = _tile_fwd(_rms, "rms_fwd", nt, [_rt(x, ts)], [p['norm_g']],
                  [((s, d), BF16, (ts, d), lambda i: (i, 0))])[0]
    proj = _mm(u, w_main, name="proj_main")
    pba = _mm(u, w_ba, name="proj_ba")

    conv_w = p['conv_w']
    col = lambda arr, cb: (arr, (s, GDN_HEAD_DIM), lambda i, cb=cb: (0, cb + i))
    qkv = []
    for j, mode in enumerate(('q', 'k', 'v')):
        off = j * nh
        qkv.append(_tile_fwd(
            _gdn_pre(mode), "gdn_pre_" + mode, nh, [col(proj, off), (conv_w, (CONV_WIDTH, GDN_HEAD_DIM), lambda i, off=off: (0, off + i))],
            [], [((s, gw), F32, (s, GDN_HEAD_DIM), lambda i: (0, i))])[0])
    q, k, v = qkv
    lane = jnp.arange(128)[:, None]
    colh = jnp.arange(gw)[None, :] // GDN_HEAD_DIM
    e_beta = (lane == colh).astype(F32)
    e_g = (lane == colh + nh).astype(F32)
    alog_row = jnp.pad(p['gdn_a_log'], ((0, 0), (nh, 128 - 2 * nh)))
    dtb_row = jnp.pad(p['gdn_dt_bias'], ((0, 0), (nh, 128 - 2 * nh)))
    row_gw = lambda: ((s, gw), F32, (ts, gw), lambda i: (i, 0))
    betab, gb = _tile_fwd(_gdn_gates, "gdn_gates", nt, [_rt(pba, ts)], [alog_row, dtb_row, e_beta, e_g],
                          [row_gw(), row_gw()])
    o_raw, states = _gdn_fwd(q, k, v, gb, betab)
    ga = _tile_fwd(_gdn_post, "gdn_post", nt, [_rt(o_raw, ts), _rt(proj, ts, CB_ZA, gw)], [p['gdn_norm_g']],
                   [((s, gw), BF16, (ts, gw), lambda i: (i, 0))])[0]

    e_rep = (jnp.arange(S5_STATE)[:, None] == jnp.arange(S5_STATE * S5_GROUP)[None, :] // S5_GROUP).astype(F32)
    s5_in = [p['s5_lambda_re'], p['s5_lambda_im'], p['s5_log_dt'].reshape(ng, 1),
             p['s5_b_re'].reshape(ng, S5_STATE * S5_GROUP), p['s5_b_im'].reshape(ng, S5_STATE * S5_GROUP), e_rep]
    one = lambda shp: (shp, F32, shp, lambda i, n=len(shp): (0,) * n)
    ab_re, ab_im, bbr, bbi = _tile_fwd(_s5_params, "s5_params", 1, [], s5_in,
                                       [one((ng, S5_STATE)), one((ng, S5_STATE)), one((ng, S5_STATE * S5_GROUP)),
                                        one((ng, S5_STATE * S5_GROUP))])
    coef = _s5_coef(ab_re.reshape(1, nl), ab_im.reshape(1, nl))
    to_bd_b = lambda t: _block_diag(t.reshape(nb, S5_GROUPS_PER_BLOCK, S5_STATE, S5_GROUP).transpose(0, 1, 3, 2))
    to_bd_c = lambda t: _block_diag(t.reshape(nb, S5_GROUPS_PER_BLOCK, S5_GROUP, S5_STATE).transpose(0, 1, 3, 2))
    bbd_re, bbd_im = to_bd_b(bbr).astype(BF16), to_bd_b(bbi).astype(BF16)
    cbd_re, cbd_im = to_bd_c(p['s5_c_re']).astype(BF16), to_bd_c(p['s5_c_im']).astype(BF16)
    xb_arr = lax.slice_in_dim(proj, CB_XB * gw, (CB_XB + 1) * gw, axis=1)
    h_re, h_im, ylin = _s5_fwd(xb_arr, bbd_re, bbd_im, cbd_re, cbd_im, coef, ts)
    gl = _tile_fwd(_s5_post1, "s5_post1", nt, [_rt(ylin, ts), _rt(proj, ts, CB_XB, gw)], [p['s5_d']],
                   [((s, gw), BF16, (ts, gw), lambda i: (i, 0))])[0]
    tglu = _mm(gl, p['s5_w_glu'], name="s5_glu")
    gbb = _tile_fwd(_s5_post2, "s5_post2", nt, [_rt(tglu, ts), _rt(proj, ts, CB_ZB, gw)], [],
                    [((s, gw), BF16, (ts, gw), lambda i: (i, 0))])[0]

    m_len = mem.shape[0]
    mem_n = _tile_fwd(_rms, "mem_rms", 1, [_rt(mem, m_len)], [p['mem_norm_g']],
                      [((m_len, d), BF16, (m_len, d), lambda i: (i, 0))])[0]
    kv = _mm(mem_n, p['w_kv_mem'], name="mem_kv")
    gcc = _tile_fwd(_attn, "attn", nt, [_rt(proj, ts, CB_QC, gw), _rt(proj, ts, CB_ZC, gw)], [kv],
                    [((s, gw), BF16, (ts, gw), lambda i: (i, 0))])[0]

    p_a = _mm(ga, p['w_br_a'], name="br_a")
    p_b = _mm(gbb, p['w_br_b'], name="br_b")
    p_c = _mm(gcc, p['w_br_c'], name="br_c")
    gate_acts = [_rt(proj, ts, CB_G // 2 + j, d) for j in range(3)]
    merged = _tile_fwd(_merge, "merge", nt, gate_acts + [_rt(p_a, ts), _rt(p_b, ts), _rt(p_c, ts)], [],
                       [((s, d), BF16, (ts, d), lambda i: (i, 0))])[0]
    mo = _mm(merged, p['w_out'], name="out_proj")
    dh, dfg, loss = _final(x, mo, target, p['final_g'].reshape(1, d), ts)
    grads['final_g'] = dfg.reshape(d)

    dmerged = _mm(dh, p['w_out'], tb=True, name="d_merged")
    grads['w_out'] = _mm(merged, dh, ta=True, name="dw_out")
    row_d = lambda dt: ((s, d), dt, (ts, d), lambda i: (i, 0))
    dg0, dg1, dg2, dpa, dpb, dpc = _tile_bwd(
        _merge, "merge_bwd", nt, gate_acts + [_rt(p_a, ts), _rt(p_b, ts), _rt(p_c, ts)], [], [_rt(dmerged, ts)],
        [row_d(BF16)] * 6, [])
    dga = _mm(dpa, p['w_br_a'], tb=True, name="d_ga")
    dgbb = _mm(dpb, p['w_br_b'], tb=True, name="d_gb")
    dgcc = _mm(dpc, p['w_br_c'], tb=True, name="d_gc")
    grads['w_br_a'] = _mm(ga, dpa, ta=True, name="dw_br_a")
    grads['w_br_b'] = _mm(gbb, dpb, ta=True, name="dw_br_b")
    grads['w_br_c'] = _mm(gcc, dpc, ta=True, name="dw_br_c")
    row_h = lambda dt: ((s, gw), dt, (ts, gw), lambda i: (i, 0))

    dqc, dzc, dkv = _tile_bwd(_attn, "attn_bwd", nt, [_rt(proj, ts, CB_QC, gw), _rt(proj, ts, CB_ZC, gw)], [kv],
                              [_rt(dgcc, ts)], [row_h(BF16), row_h(BF16)], [True])
    grads['w_kv_mem'] = _mm(mem_n, dkv, ta=True, name="dw_kv")
    dmem_n = _mm(dkv, p['w_kv_mem'], tb=True, name="d_mem_n")
    grads['mem_norm_g'] = _tile_bwd(_rms, "mem_rms_bwd", 1, [_rt(mem, m_len)], [p['mem_norm_g']],
                                    [_rt(dmem_n, m_len)], [None], [True])[0]

    dtglu, dzb = _tile_bwd(_s5_post2, "s5_post2_bwd", nt, [_rt(tglu, ts), _rt(proj, ts, CB_ZB, gw)], [],
                           [_rt(dgbb, ts)], [((s, 2 * gw), BF16, (ts, 2 * gw), lambda i: (i, 0)), row_h(BF16)], [])
    grads['s5_w_glu'] = _mm(gl, dtglu, ta=True, name="dw_glu")
    dgl = _mm(dtglu, p['s5_w_glu'], tb=True, name="d_gl")
    dylin, dxb1, dd = _tile_bwd(_s5_post1, "s5_post1_bwd", nt, [_rt(ylin, ts), _rt(proj, ts, CB_XB, gw)],
                                [p['s5_d']], [_rt(dgl, ts)], [row_h(F32), row_h(F32)], [True])
    grads['s5_d'] = dd
    dxb2, dbbd_re, dbbd_im, dcbd_re, dcbd_im, da_re, da_im = _s5_bwd(dylin, xb_arr, h_re, h_im, bbd_re, bbd_im,
                                                                    cbd_re, cbd_im, coef, ts)
    from_bd_b = lambda t: _diag_blocks(t, S5_GROUP, S5_STATE).transpose(0, 1, 3, 2).reshape(ng, S5_STATE * S5_GROUP)
    from_bd_c = lambda t: _diag_blocks(t, S5_STATE, S5_GROUP).transpose(0, 1, 3, 2).reshape(1, ng, S5_GROUP, S5_STATE)
    grads['s5_c_re'], grads['s5_c_im'] = from_bd_c(dcbd_re), from_bd_c(dcbd_im)
    s5_cts = [jnp.sum(da_re, axis=0).reshape(ng, S5_STATE), jnp.sum(da_im, axis=0).reshape(ng, S5_STATE),
              from_bd_b(dbbd_re), from_bd_b(dbbd_im)]
    dlr, dli, dlogdt, dbr, dbi = _tile_bwd(_s5_params, "s5_params_bwd", 1, [], s5_in,
                                           [(c, c.shape, lambda i: (0, 0)) for c in s5_cts], [],
                                           [True, True, True, True, True, False])
    grads['s5_lambda_re'], grads['s5_lambda_im'] = dlr[None], dli[None]
    grads['s5_log_dt'] = dlogdt.reshape(1, ng)
    grads['s5_b_re'] = dbr.reshape(1, ng, S5_STATE, S5_GROUP)
    grads['s5_b_im'] = dbi.reshape(1, ng, S5_STATE, S5_GROUP)
    dxb = (dxb1 + dxb2).astype(BF16)

    do_raw, dza, dgng = _tile_bwd(_gdn_post, "gdn_post_bwd", nt, [_rt(o_raw, ts), _rt(proj, ts, CB_ZA, gw)],
                                  [p['gdn_norm_g']], [_rt(dga, ts)], [row_h(F32), row_h(BF16)], [True])
    grads['gdn_norm_g'] = dgng
    dq, dk, dv, dgb, dbetab = _gdn_bwd(q, k, v, gb, betab, states, do_raw)
    dpba, dalog, ddtb = _tile_bwd(_gdn_gates, "gdn_gates_bwd", nt, [_rt(pba, ts)], [alog_row, dtb_row, e_beta, e_g],
                                  [_rt(dbetab, ts), _rt(dgb, ts)], [((s, 128), BF16, (ts, 128), lambda i: (i, 0))],
                                  [True, True, False, False])
    grads['gdn_a_log'] = dalog[:, nh:2 * nh]
    grads['gdn_dt_bias'] = ddtb[:, nh:2 * nh]
    dqkv, dconv = [], []
    for j, (mode, ct) in enumerate((('q', dq), ('k', dk), ('v', dv))):
        off = j * nh
        wspec = (conv_w, (CONV_WIDTH, GDN_HEAD_DIM), lambda i, off=off: (0, off + i))
        dxc, dwc = _tile_bwd(
            _gdn_pre(mode), "gdn_pre_bwd_" + mode, nh, [col(proj, off), wspec], [], [col(ct, 0)],
            [((s, gw), BF16, (s, GDN_HEAD_DIM), lambda i: (0, i)),
             ((CONV_WIDTH, gw), F32, (CONV_WIDTH, GDN_HEAD_DIM), lambda i: (0, i))], [])
        dqkv.append(dxc)
        dconv.append(dwc)
    grads['conv_w'] = jnp.concatenate(dconv, axis=1)

    dproj = jnp.concatenate(dqkv + [dza, dxb, dzb, dqc, dzc, dg0, dg1, dg2], axis=1)
    du = _mm(dpba, w_ba, tb=True, name="du_ba")
    du = _mm(dproj, w_main, tb=True, addend=du, name="du_main")
    dw_main = _mm(u, dproj, ta=True, name="dw_main")
    dw_ba = _mm(u, dpba, ta=True, name="dw_ba")
    grads['w_in'] = jnp.concatenate([dw_main[:, :4 * gw], dw_ba[:, :2 * nh], dw_main[:, 4 * gw:]], axis=1)
    dx, dng = _tile_bwd(_rms, "rms_bwd", nt, [_rt(x, ts)], [p['norm_g']], [_rt(du, ts)], [row_d(F32)], [True])
    grads['norm_g'] = dng
    return loss, dx + dh, grads


def _to_shards(name, g):
    if SHARDED[name] == 'row':
        return g.reshape((N_DEV, g.shape[0] // N_DEV) + g.shape[1:])
    r, c = g.shape
    return g.reshape(r, N_DEV, c // N_DEV).transpose(1, 0, 2)


def _from_shards(name, t):
    if SHARDED[name] == 'row':
        return t.reshape((t.shape[0] * t.shape[1],) + t.shape[2:])
    n, r, c = t.shape
    return t.transpose(1, 0, 2).reshape(r, n * c)


def _step(x, mem, target, w, m, v):
    sharded = list(SHARDED)
    shard_shapes = {n: w[n].shape[1:] for n in sharded}

    big = _exchange(_pack([w[n][0] for n in GATHER_BF16], BF16)[None], "gather_weights", False)
    got = _unpack(big, [shard_shapes[n] for n in GATHER_BF16], lead=(N_DEV,))
    full = {n: _from_shards(n, t) for n, t in zip(GATHER_BF16, got)}
    conv = _exchange(_pack([w['conv_w'][0]], F32)[None], "gather_conv", False)
    full['conv_w'] = _from_shards('conv_w', _unpack(conv, [shard_shapes['conv_w']], lead=(N_DEV,))[0])
    for n in REPLICATED:
        full[n] = w[n]
    for n in ('s5_lambda_re', 's5_lambda_im', 's5_c_re', 's5_c_im'):
        full[n] = w[n][0]

    loss, grad_x, grads = _local_step(x[0], mem[0], target[0], full)

    sent = _pack([_to_shards(n, grads[n]) for n in sharded], F32, lead=(N_DEV,))
    parts = _exchange(sent, "scatter_grads", True)
    shapes = [shard_shapes[n] for n in sharded]
    outs = _sum_adam(parts, *[_pack([t[n][0] for n in sharded], F32) for t in (w, m, v)], name="adam_sharded")
    res = {}
    for kind, buf in zip(('grad', 'delta', 'new_m', 'new_v'), outs):
        for n, t in zip(sharded, _unpack(buf, shapes)):
            res[kind, n] = t[None]

    small = [grads[n].reshape(w[n].shape) for n in REPLICATED] + [loss[:1, :1]]
    allp = _exchange(_pack(small, F32)[None], "gather_small", False)
    zero = jnp.zeros((1, 1), F32)
    outs = _sum_adam(allp, *[_pack([t[n] for n in REPLICATED] + [zero], F32) for t in (w, m, v)], name="adam_small")
    shapes = [w[n].shape for n in REPLICATED] + [(1, 1)]
    for kind, buf in zip(('grad', 'delta', 'new_m', 'new_v'), outs):
        got = _unpack(buf, shapes)
        for n, t in zip(REPLICATED, got):
            res[kind, n] = t
        if kind == 'grad':
            total_loss = got[-1].reshape(())
    out = [total_loss, grad_x[None]]
    for kind in ('grad', 'delta', 'new_m', 'new_v'):
        out += [res[kind, n] for n in WEIGHTS]
    return tuple(out)


def kernel(x, mem, norm_g, w_in, conv_w, gdn_a_log, gdn_dt_bias, gdn_norm_g, s5_lambda_re, s5_lambda_im, s5_log_dt, s5_b_re, s5_b_im, s5_c_re, s5_c_im, s5_d, s5_w_glu, mem_norm_g, w_kv_mem, w_br_a, w_br_b, w_br_c, w_out, final_g, loss_target, m_norm_g, m_w_in, m_conv_w, m_gdn_a_log, m_gdn_dt_bias, m_gdn_norm_g, m_s5_lambda_re, m_s5_lambda_im, m_s5_log_dt, m_s5_b_re, m_s5_b_im, m_s5_c_re, m_s5_c_im, m_s5_d, m_s5_w_glu, m_mem_norm_g, m_w_kv_mem, m_w_br_a, m_w_br_b, m_w_br_c, m_w_out, m_final_g, v_norm_g, v_w_in, v_conv_w, v_gdn_a_log, v_gdn_dt_bias, v_gdn_norm_g, v_s5_lambda_re, v_s5_lambda_im, v_s5_log_dt, v_s5_b_re, v_s5_b_im, v_s5_c_re, v_s5_c_im, v_s5_d, v_s5_w_glu, v_mem_norm_g, v_w_kv_mem, v_w_br_a, v_w_br_b, v_w_br_c, v_w_out, v_final_g):
    a = dict(locals())
    w = {n: a[n] for n in WEIGHTS}
    m = {n: a['m_' + n] for n in WEIGHTS}
    v = {n: a['v_' + n] for n in WEIGHTS}
    return _step(x, mem, loss_target, w, m, v)
```

```python
import functools
import math

import jax
import jax.numpy as jnp
from jax import lax
from jax.experimental import pallas as pl
from jax.experimental.pallas import tpu as pltpu

F32 = jnp.float32
BF16 = jnp.bfloat16
HI = lax.Precision.HIGHEST

EPS = 1e-6
CHUNK = 64
GDN_HEAD_DIM = 128
CONV_WIDTH = 4
S5_GROUP = 16
S5_STATE = 64
S5_GROUPS_PER_BLOCK = 8
XA_HEADS = 4
N_DEV = 8
ADAM_LR, ADAM_B1, ADAM_B2, ADAM_EPS, ADAM_WD, ADAM_STEP = 0.001, 0.9, 0.999, 1e-08, 0.01, 10

VMEM_LIMIT_BYTES = 56 * 1024 * 1024
SCAN_LANES = 512
PACK_WIDTH = 512
PACK_ROWS = 256

WEIGHTS = ['norm_g', 'w_in', 'conv_w', 'gdn_a_log', 'gdn_dt_bias', 'gdn_norm_g', 's5_lambda_re', 's5_lambda_im',
           's5_log_dt', 's5_b_re', 's5_b_im', 's5_c_re', 's5_c_im', 's5_d', 's5_w_glu', 'mem_norm_g', 'w_kv_mem',
           'w_br_a', 'w_br_b', 'w_br_c', 'w_out', 'final_g']
SHARDED = {'w_in': 'col', 'conv_w': 'col', 's5_w_glu': 'col', 'w_kv_mem': 'row', 'w_br_a': 'col', 'w_br_b': 'col',
           'w_br_c': 'col', 'w_out': 'row'}
GATHER_BF16 = ['w_in', 's5_w_glu', 'w_kv_mem', 'w_br_a', 'w_br_b', 'w_br_c', 'w_out']
REPLICATED = [n for n in WEIGHTS if n not in SHARDED]


def _cparams(sem=None):
    return pltpu.CompilerParams(dimension_semantics=sem, vmem_limit_bytes=VMEM_LIMIT_BYTES)


def _pick(dim, pref):
    t = (min(pref, dim) // 128) * 128
    while t >= 128:
        if dim % t == 0:
            return t
        t -= 128
    return dim


def _make_dots(prep, precision):
    def raw(a, b, dims):
        return lax.dot_general(prep(a), prep(b), (dims, ((), ())), preferred_element_type=F32, precision=precision)

    @jax.custom_vjp
    def nn(a, b):
        return raw(a, b, ((1,), (0,)))

    @jax.custom_vjp
    def nt(a, b):
        return raw(a, b, ((1,), (1,)))

    @jax.custom_vjp
    def tn(a, b):
        return raw(a, b, ((0,), (0,)))

    nn.defvjp(lambda a, b: (nn(a, b), (a, b)), lambda r, ct: (nt(ct, r[1]), tn(r[0], ct)))
    nt.defvjp(lambda a, b: (nt(a, b), (a, b)), lambda r, ct: (nn(ct, r[1]), tn(ct, r[0])))
    tn.defvjp(lambda a, b: (tn(a, b), (a, b)), lambda r, ct: (nt(r[1], ct), nn(r[0], ct)))
    return nn, nt, tn


_bnn, _bnt, _btn = _make_dots(lambda a: a.astype(BF16), None)
_hnn, _hnt, _htn = _make_dots(lambda a: a.astype(F32), HI)


def _mm(a, b, *, name, ta=False, tb=False, out_dtype=F32, addend=None, tm=512, tn=1024, tk=1024):
    m, k = (a.shape[1], a.shape[0]) if ta else a.shape
    n = b.shape[0] if tb else b.shape[1]
    assert (b.shape[1] if tb else b.shape[0]) == k, (a.shape, b.shape, ta, tb)
    tm, tn, tk = _pick(m, tm), _pick(n, tn), _pick(k, tk)
    nk = k // tk
    dims = ((0 if ta else 1,), (1 if tb else 0,))

    def body(*refs):
        if addend is None:
            a_ref, b_ref, o_ref, acc_ref = refs
        else:
            a_ref, b_ref, add_ref, o_ref, acc_ref = refs
        kk = pl.program_id(2)

        @pl.when(kk == 0)
        def _():
            acc_ref[...] = jnp.zeros_like(acc_ref)

        acc_ref[...] += lax.dot_general(a_ref[...].astype(BF16), b_ref[...].astype(BF16), (dims, ((), ())),
                                        preferred_element_type=F32)

        @pl.when(kk == nk - 1)
        def _():
            r = acc_ref[...]
            if addend is not None:
                r = r + add_ref[...].astype(F32)
            o_ref[...] = r.astype(o_ref.dtype)

    a_spec = pl.BlockSpec((tk, tm), lambda i, j, kk: (kk, i)) if ta else pl.BlockSpec((tm, tk), lambda i, j, kk: (i, kk))
    b_spec = pl.BlockSpec((tn, tk), lambda i, j, kk: (j, kk)) if tb else pl.BlockSpec((tk, tn), lambda i, j, kk: (kk, j))
    o_spec = pl.BlockSpec((tm, tn), lambda i, j, kk: (i, j))
    in_specs = [a_spec, b_spec] + ([o_spec] if addend is not None else [])
    args = (a, b) + ((addend,) if addend is not None else ())
    return pl.pallas_call(
        body, name=name, grid=(m // tm, n // tn, nk), in_specs=in_specs, out_specs=o_spec,
        out_shape=jax.ShapeDtypeStruct((m, n), out_dtype), scratch_shapes=[pltpu.VMEM((tm, tn), F32)],
        compiler_params=_cparams(("parallel", "parallel", "arbitrary")))(*args)


def _rt(arr, ts, cb=0, w=None):
    w = arr.shape[1] if w is None else w
    return (arr, (ts, w), lambda i, cb=cb: (i, cb))


def _whole(p):
    return pl.BlockSpec(p.shape, lambda i, nd=p.ndim: (0,) * nd)


def _tile_fwd(f, name, n, acts, params, outs):
    na, npar = len(acts), len(params)

    def body(*refs):
        res = f(*[r[...] for r in refs[:na + npar]])
        for r, v in zip(refs[na + npar:], res):
            r[...] = v.astype(r.dtype)

    in_specs = [pl.BlockSpec(b, m) for _, b, m in acts] + [_whole(p) for p in params]
    out = pl.pallas_call(
        body, name=name, grid=(n,), in_specs=in_specs,
        out_specs=[pl.BlockSpec(b, m) for _, _, b, m in outs],
        out_shape=[jax.ShapeDtypeStruct(s, d) for s, d, _, _ in outs],
        compiler_params=_cparams(("parallel",)))(*[a for a, _, _ in acts], *params)
    return out


def _tile_bwd(f, name, n, acts, params, cts, agrads, pgrads):
    na, npar, nc = len(acts), len(params), len(cts)

    def body(*refs):
        i = pl.program_id(0)
        ins = [r[...] for r in refs[:na + npar]]
        outs, vjp = jax.vjp(f, *ins)
        g = vjp(tuple(c[...].astype(o.dtype) for c, o in zip(refs[na + npar:na + npar + nc], outs)))
        orefs = refs[na + npar + nc:]
        k = 0
        for j in range(na):
            if agrads[j] is not None:
                orefs[k][...] = g[j].astype(orefs[k].dtype)
                k += 1
        for j in range(npar):
            if pgrads[j]:
                o = orefs[k]

                @pl.when(i == 0)
                def _(o=o):
                    o[...] = jnp.zeros_like(o)

                o[...] += g[na + j].astype(F32)
                k += 1

    in_specs = ([pl.BlockSpec(b, m) for _, b, m in acts] + [_whole(p) for p in params]
                + [pl.BlockSpec(b, m) for _, b, m in cts])
    out_specs = [pl.BlockSpec(g[2], g[3]) for g in agrads if g is not None]
    out_shape = [jax.ShapeDtypeStruct(g[0], g[1]) for g in agrads if g is not None]
    for p, flag in zip(params, pgrads):
        if flag:
            out_specs.append(_whole(p))
            out_shape.append(jax.ShapeDtypeStruct(p.shape, F32))
    return pl.pallas_call(
        body, name=name, grid=(n,), in_specs=in_specs, out_specs=out_specs, out_shape=out_shape,
        compiler_params=_cparams(("arbitrary",)))(*[a for a, _, _ in acts], *params, *[c for c, _, _ in cts])


def _silu(x):
    return x * jax.nn.sigmoid(x)


def _rms(x, g):
    x = x.astype(F32)
    return (x * lax.rsqrt(jnp.mean(x * x, axis=-1, keepdims=True) + EPS) * g,)


def _shift_down(x, s):
    row = lax.broadcasted_iota(jnp.int32, x.shape, 0)
    return jnp.where(row >= s, pltpu.roll(x, s, 0), 0.0)


def _shift_up(x, s):
    n = x.shape[0]
    row = lax.broadcasted_iota(jnp.int32, x.shape, 0)
    return jnp.where(row < n - s, pltpu.roll(x, n - s, 0), 0.0)


@functools.partial(jax.custom_vjp, nondiff_argnums=(1,))
def _shift(x, s):
    return _shift_down(x, s)


_shift.defvjp(lambda x, s: (_shift_down(x, s), None), lambda s, _, ct: (_shift_up(ct, s),))


def _gdn_pre(mode):
    def f(x, w):
        y = x * w[CONV_WIDTH - 1:CONV_WIDTH, :]
        for j in range(CONV_WIDTH - 1):
            y = y + _shift(x, CONV_WIDTH - 1 - j) * w[j:j + 1, :]
        y = _silu(y)
        if mode != 'v':
            y = y * lax.rsqrt(jnp.sum(y * y, axis=-1, keepdims=True) + EPS)
        if mode == 'q':
            y = y * (GDN_HEAD_DIM ** -0.5)
        return (y,)
    return f


def _softplus(x):
    return jnp.maximum(x, 0.0) + jnp.log1p(jnp.exp(-jnp.abs(x)))


def _gdn_gates(ba, alog, dtb, e_beta, e_g):
    beta = jax.nn.sigmoid(ba)
    g = -jnp.exp(alog) * _softplus(ba + dtb)
    return _hnn(beta, lax.stop_gradient(e_beta)), _hnn(g, lax.stop_gradient(e_g))


def _gdn_chunk(q, k, v, gb, bb, state):
    c = q.shape[0]
    ri = lax.broadcasted_iota(jnp.int32, (c, c), 0)
    ci = lax.broadcasted_iota(jnp.int32, (c, c), 1)
    incl, strict = ri >= ci, ri > ci
    tri = incl.astype(F32)
    eye = (ri == ci).astype(F32)
    gc = _hnn(tri, gb)
    diff = jnp.where(incl, gc[:, :c] - gc.T[:c, :], -1e30)
    decay = jnp.exp(diff)
    kb = k * bb
    neg = jnp.where(strict, -(_bnt(kb, k) * decay), 0.0)
    t = eye + neg
    p = neg
    for _ in range(int(math.log2(c)) - 1):
        p = _hnn(p, p)
        t = t + _hnn(t, p)
    egc = jnp.exp(gc)
    u_val = _hnn(t, v * bb)
    w_dec = _hnn(t, kb * egc)
    qk = _bnt(q, k) * decay
    gl = jnp.sum(gb, axis=0, keepdims=True)
    k_dec = k * jnp.exp(gl - gc)
    v_new = u_val - _bnn(w_dec, state)
    o = _bnn(q * egc, state) + _bnn(qk, v_new)
    return o, state * jnp.exp(gl) + _btn(k_dec, v_new)


def _gdn_post(o, z, g):
    parts = []
    for h in range(o.shape[1] // GDN_HEAD_DIM):
        oh = o[:, h * GDN_HEAD_DIM:(h + 1) * GDN_HEAD_DIM]
        parts.append(oh * lax.rsqrt(jnp.mean(oh * oh, axis=-1, keepdims=True) + EPS) * g)
    y = parts[0] if len(parts) == 1 else jnp.concatenate(parts, axis=1)
    return (y * _silu(z),)


def _gelu(x):
    return 0.5 * x * (1.0 + jnp.tanh(0.7978845608028654 * (x + 0.044715 * x * x * x)))


def _s5_post1(ylin, xb, d):
    return (_gelu(ylin + d * xb),)


def _s5_post2(t, z):
    w = t.shape[1] // 2
    return (t[:, :w] * jax.nn.sigmoid(t[:, w:]) * _silu(z),)


def _attn(q, z, kv):
    w = q.shape[1]
    hd = w // XA_HEADS
    parts = []
    for h in range(XA_HEADS):
        s = _bnt(q[:, h * hd:(h + 1) * hd], kv[:, h * hd:(h + 1) * hd]) * (hd ** -0.5)
        s = s - jnp.max(s, axis=-1, keepdims=True)
        e = jnp.exp(s)
        p = e / jnp.sum(e, axis=-1, keepdims=True)
        parts.append(_bnn(p, kv[:, w + h * hd:w + (h + 1) * hd]))
    return (jnp.concatenate(parts, axis=1) * _silu(z),)


def _merge(g0, g1, g2, pa, pb, pc):
    return (jax.nn.sigmoid(g0) * pa + jax.nn.sigmoid(g1) * pb + jax.nn.sigmoid(g2) * pc,)


def _s5_params(lr, li, logdt, br, bi, e):
    dt = jnp.exp(logdt)
    mag = jnp.exp(lr * dt)
    ab_re, ab_im = mag * jnp.cos(li * dt), mag * jnp.sin(li * dt)
    den = lr * lr + li * li
    nr, ni = ab_re - 1.0, ab_im
    e = lax.stop_gradient(e)
    cre = _hnn((nr * lr + ni * li) / den, e)
    cim = _hnn((ni * lr - nr * li) / den, e)
    return ab_re, ab_im, cre * br - cim * bi, cre * bi + cim * br


def _gdn_fwd(q, k, v, gb, bb):
    s, w = q.shape
    nh, nc = w // GDN_HEAD_DIM, s // CHUNK
    hd = GDN_HEAD_DIM

    def body(q_ref, k_ref, v_ref, g_ref, b_ref, o_ref, st_ref, state):
        @pl.when(pl.program_id(0) == 0)
        def _():
            state[...] = jnp.zeros_like(state)

        for h in range(nh):
            sl = slice(h * hd, (h + 1) * hd)
            st = state[h]
            st_ref[0, h] = st
            o, new = _gdn_chunk(q_ref[:, sl], k_ref[:, sl], v_ref[:, sl], g_ref[:, sl], b_ref[:, sl], st)
            o_ref[:, sl] = o
            state[h] = new

    blk = pl.BlockSpec((CHUNK, w), lambda i: (i, 0))
    return pl.pallas_call(
        body, name="gdn_fwd", grid=(nc,), in_specs=[blk] * 5,
        out_specs=[blk, pl.BlockSpec((1, nh, hd, hd), lambda i: (i, 0, 0, 0))],
        out_shape=[jax.ShapeDtypeStruct((s, w), F32), jax.ShapeDtypeStruct((nc, nh, hd, hd), F32)],
        scratch_shapes=[pltpu.VMEM((nh, hd, hd), F32)], compiler_params=_cparams(("arbitrary",)))(q, k, v, gb, bb)


def _gdn_bwd(q, k, v, gb, bb, states, do):
    s, w = q.shape
    nh, nc = w // GDN_HEAD_DIM, s // CHUNK
    hd = GDN_HEAD_DIM

    def body(q_ref, k_ref, v_ref, g_ref, b_ref, st_ref, do_ref, dq_ref, dk_ref, dv_ref, dg_ref, db_ref, dstate):
        @pl.when(pl.program_id(0) == 0)
        def _():
            dstate[...] = jnp.zeros_like(dstate)

        for h in range(nh):
            sl = slice(h * hd, (h + 1) * hd)
            _, vjp = jax.vjp(_gdn_chunk, q_ref[:, sl], k_ref[:, sl], v_ref[:, sl], g_ref[:, sl], b_ref[:, sl],
                             st_ref[0, h])
            dq, dk, dv, dg, db, dst = vjp((do_ref[:, sl], dstate[h]))
            dq_ref[:, sl] = dq
            dk_ref[:, sl] = dk
            dv_ref[:, sl] = dv
            dg_ref[:, sl] = dg
            db_ref[:, sl] = db
            dstate[h] = dst

    blk = pl.BlockSpec((CHUNK, w), lambda i: (nc - 1 - i, 0))
    return pl.pallas_call(
        body, name="gdn_bwd", grid=(nc,),
        in_specs=[blk] * 5 + [pl.BlockSpec((1, nh, hd, hd), lambda i: (nc - 1 - i, 0, 0, 0)), blk],
        out_specs=[blk] * 5, out_shape=[jax.ShapeDtypeStruct((s, w), F32)] * 5,
        scratch_shapes=[pltpu.VMEM((nh, hd, hd), F32)],
        compiler_params=_cparams(("arbitrary",)))(q, k, v, gb, bb, states, do)


def _s5_coef(ar, ai):
    nl = ar.shape[1]

    def body(ar_ref, ai_ref, o_ref):
        row = lax.broadcasted_iota(jnp.int32, (8, nl), 0)
        for base, sign in ((0, 1.0), (8, -1.0)):
            pr = [jnp.broadcast_to(ar_ref[...], (8, nl))]
            pi = [jnp.broadcast_to(ai_ref[...], (8, nl)) * sign]
            for _ in range(7):
                pr.append(pr[-1] * pr[0] - pi[-1] * pi[0])
                pi.append(pr[-2] * pi[0] + pi[-1] * pr[0])
            for j, d in enumerate((1, 2, 4)):
                m = (row >= d) if base == 0 else (row <= 7 - d)
                o_ref[base + 2 * j] = jnp.where(m, pr[d - 1], 0.0)
                o_ref[base + 2 * j + 1] = jnp.where(m, pi[d - 1], 0.0)
            cr, ci = jnp.zeros((8, nl), F32), jnp.zeros((8, nl), F32)
            for t in range(8):
                e = t if base == 0 else 7 - t
                cr = jnp.where(row == t, pr[e], cr)
                ci = jnp.where(row == t, pi[e], ci)
            o_ref[base + 6] = cr
            o_ref[base + 7] = ci

    return pl.pallas_call(body, name="s5_coef", out_shape=jax.ShapeDtypeStruct((16, 8, nl), F32),
                          compiler_params=_cparams())(ar, ai)


def _scan_tile(src_re, src_im, dst_re, dst_im, coef_ref, carry_re, carry_im, ts, reverse, extra=None):
    nl = src_re.shape[1]
    base = 8 if reverse else 0
    ng = ts // 8
    for lc in range(nl // SCAN_LANES):
        ln = slice(lc * SCAN_LANES, (lc + 1) * SCAN_LANES)
        m = [coef_ref[base + j, :, ln] for j in range(8)]
        row = lax.broadcasted_iota(jnp.int32, (8, SCAN_LANES), 0)

        def step(r, carry, ln=ln, m=m, row=row):
            grp = (ng - 1 - r) if reverse else r
            rows = pl.ds(pl.multiple_of(grp * 8, 8), 8)
            xr, xi = src_re[rows, ln], src_im[rows, ln]
            for j, d in enumerate((1, 2, 4)):
                sh = 8 - d if reverse else d
                sr, si = pltpu.roll(xr, sh, 0), pltpu.roll(xi, sh, 0)
                mr, mi = m[2 * j], m[2 * j + 1]
                xr, xi = xr + mr * sr - mi * si, xi + mr * si + mi * sr
            cr, ci = carry[0], carry[1]
            hr = xr + m[6] * cr - m[7] * ci
            hi = xi + m[6] * ci + m[7] * cr
            dst_re[rows, ln] = hr
            dst_im[rows, ln] = hi
            edge = 0 if reverse else 7
            out = (jnp.broadcast_to(hr[edge:edge + 1, :], hr.shape), jnp.broadcast_to(hi[edge:edge + 1, :], hi.shape))
            if extra is not None:
                h_re, h_im, halo_re, halo_im, first, _, _ = extra
                prev = pl.ds(pl.multiple_of(jnp.maximum(grp - 1, 0) * 8, 8), 8)
                use_halo = grp == 0
                pr = jnp.where(use_halo, halo_re[:, ln] * first, h_re[prev, ln])
                pi = jnp.where(use_halo, halo_im[:, ln] * first, h_im[prev, ln])
                qr = jnp.where(row == 0, jnp.broadcast_to(pr[7:8, :], pr.shape), pltpu.roll(h_re[rows, ln], 1, 0))
                qi = jnp.where(row == 0, jnp.broadcast_to(pi[7:8, :], pi.shape), pltpu.roll(h_im[rows, ln], 1, 0))
                out = out + (carry[2] + hr * qr + hi * qi, carry[3] + hi * qr - hr * qi)
            return out

        init = (carry_re[:, ln], carry_im[:, ln])
        if extra is not None:
            init = init + (extra[5][:, ln], extra[6][:, ln])
        fin = lax.fori_loop(0, ng, step, init)
        carry_re[:, ln] = fin[0]
        carry_im[:, ln] = fin[1]
        if extra is not None:
            extra[5][:, ln] = fin[2]
            extra[6][:, ln] = fin[3]


def _s5_fwd(xb, bb_re, bb_im, c_re, c_im, coef, ts):
    s, w = xb.shape
    nb = bb_re.shape[0]
    nl = nb * 512

    def body(x_ref, bre_ref, bim_ref, cre_ref, cim_ref, coef_ref, hre_ref, him_ref, y_ref, ure, uim, car_re, car_im):
        @pl.when(pl.program_id(0) == 0)
        def _():
            car_re[...] = jnp.zeros_like(car_re)
            car_im[...] = jnp.zeros_like(car_im)

        for b in range(nb):
            xs = x_ref[:, b * 128:(b + 1) * 128].astype(BF16)
            ure[:, b * 512:(b + 1) * 512] = jnp.dot(xs, bre_ref[b], preferred_element_type=F32)
            uim[:, b * 512:(b + 1) * 512] = jnp.dot(xs, bim_ref[b], preferred_element_type=F32)
        _scan_tile(ure, uim, hre_ref, him_ref, coef_ref, car_re, car_im, ts, False)
        for b in range(nb):
            hr = hre_ref[:, b * 512:(b + 1) * 512].astype(BF16)
            hi = him_ref[:, b * 512:(b + 1) * 512].astype(BF16)
            y_ref[:, b * 128:(b + 1) * 128] = (jnp.dot(hr, cre_ref[b], preferred_element_type=F32)
                                               - jnp.dot(hi, cim_ref[b], preferred_element_type=F32))

    row = lambda wd: pl.BlockSpec((ts, wd), lambda i: (i, 0))
    return pl.pallas_call(
        body, name="s5_fwd", grid=(s // ts,),
        in_specs=[row(w), _whole(bb_re), _whole(bb_im), _whole(c_re), _whole(c_im), _whole(coef)],
        out_specs=[row(nl), row(nl), row(w)],
        out_shape=[jax.ShapeDtypeStruct((s, nl), F32), jax.ShapeDtypeStruct((s, nl), F32),
                   jax.ShapeDtypeStruct((s, w), F32)],
        scratch_shapes=[pltpu.VMEM((ts, nl), F32), pltpu.VMEM((ts, nl), F32), pltpu.VMEM((8, nl), F32),
                        pltpu.VMEM((8, nl), F32)],
        compiler_params=_cparams(("arbitrary",)))(xb, bb_re, bb_im, c_re, c_im, coef)


def _s5_bwd(dy, xb, h_re, h_im, bb_re, bb_im, c_re, c_im, coef, ts):
    s, w = xb.shape
    nb = bb_re.shape[0]
    nl = nb * 512
    nt = s // ts

    def body(dy_ref, x_ref, hre_ref, him_ref, halo_re, halo_im, bre_ref, bim_ref, cre_ref, cim_ref, coef_ref,
             dx_ref, dbre_ref, dbim_ref, dcre_ref, dcim_ref, dare_ref, daim_ref, gre, gim, car_re, car_im):
        i = pl.program_id(0)

        @pl.when(i == 0)
        def _():
            for r in (car_re, car_im, dbre_ref, dbim_ref, dcre_ref, dcim_ref, dare_ref, daim_ref):
                r[...] = jnp.zeros_like(r)

        for b in range(nb):
            dyb = dy_ref[:, b * 128:(b + 1) * 128].astype(BF16)
            gre[:, b * 512:(b + 1) * 512] = lax.dot_general(dyb, cre_ref[b], (((1,), (1,)), ((), ())),
                                                            preferred_element_type=F32)
            gim[:, b * 512:(b + 1) * 512] = -lax.dot_general(dyb, cim_ref[b], (((1,), (1,)), ((), ())),
                                                             preferred_element_type=F32)
            hr = hre_ref[:, b * 512:(b + 1) * 512].astype(BF16)
            hi = him_ref[:, b * 512:(b + 1) * 512].astype(BF16)
            dcre_ref[b] += lax.dot_general(hr, dyb, (((0,), (0,)), ((), ())), preferred_element_type=F32)
            dcim_ref[b] -= lax.dot_general(hi, dyb, (((0,), (0,)), ((), ())), preferred_element_type=F32)
        first = (i != nt - 1).astype(F32)
        _scan_tile(gre, gim, gre, gim, coef_ref, car_re, car_im, ts, True,
                   extra=(hre_ref, him_ref, halo_re, halo_im, first, dare_ref, daim_ref))
        for b in range(nb):
            gr = gre[:, b * 512:(b + 1) * 512].astype(BF16)
            gi = gim[:, b * 512:(b + 1) * 512].astype(BF16)
            xs = x_ref[:, b * 128:(b + 1) * 128].astype(BF16)
            dx_ref[:, b * 128:(b + 1) * 128] = (
                lax.dot_general(gr, bre_ref[b], (((1,), (1,)), ((), ())), preferred_element_type=F32)
                + lax.dot_general(gi, bim_ref[b], (((1,), (1,)), ((), ())), preferred_element_type=F32))
            dbre_ref[b] += lax.dot_general(xs, gr, (((0,), (0,)), ((), ())), preferred_element_type=F32)
            dbim_ref[b] += lax.dot_general(xs, gi, (((0,), (0,)), ((), ())), preferred_element_type=F32)

    row = lambda wd: pl.BlockSpec((ts, wd), lambda i: (nt - 1 - i, 0))
    halo = pl.BlockSpec((8, nl), lambda i: (jnp.maximum((nt - 1 - i) * (ts // 8) - 1, 0), 0))
    return pl.pallas_call(
        body, name="s5_bwd", grid=(nt,),
        in_specs=[row(w), row(w), row(nl), row(nl), halo, halo, _whole(bb_re), _whole(bb_im), _whole(c_re),
                  _whole(c_im), _whole(coef)],
        out_specs=[row(w), _whole(bb_re), _whole(bb_im), _whole(c_re), _whole(c_im),
                   pl.BlockSpec((8, nl), lambda i: (0, 0)), pl.BlockSpec((8, nl), lambda i: (0, 0))],
        out_shape=[jax.ShapeDtypeStruct((s, w), F32), jax.ShapeDtypeStruct(bb_re.shape, F32),
                   jax.ShapeDtypeStruct(bb_im.shape, F32), jax.ShapeDtypeStruct(c_re.shape, F32),
                   jax.ShapeDtypeStruct(c_im.shape, F32), jax.ShapeDtypeStruct((8, nl), F32),
                   jax.ShapeDtypeStruct((8, nl), F32)],
        scratch_shapes=[pltpu.VMEM((ts, nl), F32), pltpu.VMEM((ts, nl), F32), pltpu.VMEM((8, nl), F32),
                        pltpu.VMEM((8, nl), F32)],
        compiler_params=_cparams(("arbitrary",)))(dy, xb, h_re, h_im, h_re, h_im, bb_re, bb_im, c_re, c_im, coef)


def _final(x, mo, target, fg, ts):
    s, d = x.shape

    def f(x, mo, fg, tgt):
        y = _rms(x + mo, fg)[0]
        err = y - tgt
        return 0.5 * jnp.sum(jnp.mean(err * err, axis=-1, keepdims=True), axis=0, keepdims=True)

    def body(x_ref, mo_ref, t_ref, fg_ref, dh_ref, dfg_ref, loss_ref):
        @pl.when(pl.program_id(0) == 0)
        def _():
            dfg_ref[...] = jnp.zeros_like(dfg_ref)
            loss_ref[...] = jnp.zeros_like(loss_ref)

        loss, vjp = jax.vjp(f, x_ref[...], mo_ref[...], fg_ref[...], t_ref[...])
        _, dmo, dfg, _ = vjp(jnp.ones((1, 1), F32))
        dh_ref[...] = dmo
        dfg_ref[...] += dfg
        loss_ref[...] += jnp.broadcast_to(loss, loss_ref.shape)

    row = pl.BlockSpec((ts, d), lambda i: (i, 0))
    return pl.pallas_call(
        body, name="final", grid=(s // ts,), in_specs=[row, row, row, _whole(fg)],
        out_specs=[row, _whole(fg), pl.BlockSpec((8, 128), lambda i: (0, 0))],
        out_shape=[jax.ShapeDtypeStruct((s, d), F32), jax.ShapeDtypeStruct(fg.shape, F32),
                   jax.ShapeDtypeStruct((8, 128), F32)],
        compiler_params=_cparams(("arbitrary",)))(x, mo, target, fg)


def _peers(x, y, c):
    out = []
    for k in range(1, N_DEV):
        px = 1 - x if k & 4 else x
        py = 1 - y if k & 2 else y
        pc = 1 - c if k & 1 else c
        out.append(((px, py, pc), 4 * px + 2 * py + pc))
    return out


def _send_all(ref, idx):
    return ref


def _send_slot(ref, idx):
    return ref.at[idx]


def _send_rows(rows):
    return lambda ref, idx: ref.at[pl.ds(pl.multiple_of(idx * rows, 8), rows), :]


def _send_cols(cols):
    return lambda ref, idx: ref.at[:, pl.ds(pl.multiple_of(idx * cols, 128), cols)]


def _exchange(srcs, picks, blocks, name):
    n = len(srcs)

    def body(*refs):
        src_refs, out_refs = refs[:n], refs[n:2 * n]
        send_sems, recv_sems, local_sems = refs[2 * n:]
        x, y, c = lax.axis_index("x"), lax.axis_index("y"), lax.axis_index("c")
        me = 4 * x + 2 * y + c
        peers = _peers(x, y, c)
        local, sends = [], []
        for a in range(n):
            cp = pltpu.make_async_copy(picks[a](src_refs[a], me), out_refs[a].at[me], local_sems.at[a])
            cp.start()
            local.append(cp)
        for k, (pos, idx) in enumerate(peers):
            for a in range(n):
                cp = pltpu.make_async_remote_copy(
                    src_ref=picks[a](src_refs[a], idx), dst_ref=out_refs[a].at[me], send_sem=send_sems.at[a, k],
                    recv_sem=recv_sems.at[a, k], device_id=pos, device_id_type=pl.DeviceIdType.MESH)
                cp.start()
                sends.append(cp)
        for k, (pos, idx) in enumerate(peers):
            for a in range(n):
                pltpu.make_async_remote_copy(
                    src_ref=picks[a](src_refs[a], me), dst_ref=out_refs[a].at[idx], send_sem=send_sems.at[a, k],
                    recv_sem=recv_sems.at[a, k], device_id=pos, device_id_type=pl.DeviceIdType.MESH).wait_recv()
        for cp in sends:
            cp.wait_send()
        for cp in local:
            cp.wait()

    return pl.pallas_call(
        body, name=name, in_specs=[pl.BlockSpec(memory_space=pl.ANY)] * n,
        out_specs=[pl.BlockSpec(memory_space=pl.ANY)] * n,
        out_shape=[jax.ShapeDtypeStruct((N_DEV,) + tuple(b), s.dtype) for s, b in zip(srcs, blocks)],
        scratch_shapes=[pltpu.SemaphoreType.DMA((n, N_DEV - 1)), pltpu.SemaphoreType.DMA((n, N_DEV - 1)),
                        pltpu.SemaphoreType.DMA((n,))],
        compiler_params=pltpu.CompilerParams(has_side_effects=True))(*srcs)


def _pick_rows(r, pref):
    t = (min(pref, r) // 8) * 8
    while t >= 8:
        if r % t == 0:
            return t
        t -= 8
    return r


def _w_in_from_shards(t, lo, hi):
    n, r, cs = t.shape
    tr = _pick_rows(r, 256)
    wm = n * cs - (hi - lo)

    def body(t_ref, m_ref, b_ref):
        full = jnp.concatenate([t_ref[j] for j in range(n)], axis=1)
        m_ref[...] = jnp.concatenate([full[:, :lo], full[:, hi:]], axis=1)
        b_ref[...] = jnp.concatenate([full[:, lo:hi], jnp.zeros((tr, 128 - (hi - lo)), full.dtype)], axis=1)

    return pl.pallas_call(
        body, name="w_in_layout", grid=(r // tr,), in_specs=[pl.BlockSpec((n, tr, cs), lambda i: (0, i, 0))],
        out_specs=[pl.BlockSpec((tr, wm), lambda i: (i, 0)), pl.BlockSpec((tr, 128), lambda i: (i, 0))],
        out_shape=[jax.ShapeDtypeStruct((r, wm), t.dtype), jax.ShapeDtypeStruct((r, 128), t.dtype)],
        compiler_params=_cparams(("parallel",)))(t)


def _w_in_to_shards(gm, gb, lo, hi):
    r, wm = gm.shape
    cs = (wm + hi - lo) // N_DEV
    tr = _pick_rows(r, 64)

    def body(m_ref, b_ref, o_ref):
        m = m_ref[...]
        full = jnp.concatenate([m[:, :lo], b_ref[:, :hi - lo], m[:, lo:]], axis=1)
        for j in range(N_DEV):
            o_ref[j] = full[:, j * cs:(j + 1) * cs]

    return pl.pallas_call(
        body, name="dw_in_layout", grid=(r // tr,),
        in_specs=[pl.BlockSpec((tr, wm), lambda i: (i, 0)), pl.BlockSpec((tr, 128), lambda i: (i, 0))],
        out_specs=pl.BlockSpec((N_DEV, tr, cs), lambda i: (0, i, 0)),
        out_shape=jax.ShapeDtypeStruct((N_DEV, r, cs), gm.dtype), compiler_params=_cparams(("parallel",)))(gm, gb)


def _pack(arrs, dtype, lead=()):
    nlead = len(lead)
    flat = jnp.concatenate([a.astype(dtype).reshape(lead + (-1,)) for a in arrs], axis=nlead)
    n = flat.shape[-1]
    unit = PACK_WIDTH * PACK_ROWS
    pad = (-n) % unit
    flat = jnp.pad(flat, [(0, 0)] * nlead + [(0, pad)])
    return flat.reshape(lead + ((n + pad) // PACK_WIDTH, PACK_WIDTH))


def _unpack(buf, shapes, lead=()):
    flat = buf.reshape(lead + (-1,))
    out, off = [], 0
    for shp in shapes:
        n = math.prod(shp)
        out.append(flat[..., off:off + n].reshape(lead + tuple(shp)))
        off += n
    return out


def _adam_math(w, g, m, v):
    m = ADAM_B1 * m + (1.0 - ADAM_B1) * g
    v = ADAM_B2 * v + (1.0 - ADAM_B2) * (g * g)
    m_hat = m / (1.0 - ADAM_B1 ** ADAM_STEP)
    v_hat = v / (1.0 - ADAM_B2 ** ADAM_STEP)
    delta = -ADAM_LR * (m_hat / (jnp.sqrt(v_hat) + ADAM_EPS) + ADAM_WD * w)
    return delta, m, v


def _sum_adam(parts, w, m, v, name):
    r, c = w.shape
    lanes = -(-c // 128) * 128
    tr = _pick_rows(r, max(8, (6 * 1024 * 1024) // (N_DEV * lanes * 4)))

    def body(p_ref, w_ref, m_ref, v_ref, g_ref, d_ref, nm_ref, nv_ref):
        g = p_ref[0]
        for j in range(1, N_DEV):
            g = g + p_ref[j]
        d, nm, nv = _adam_math(w_ref[...], g, m_ref[...], v_ref[...])
        g_ref[...] = g
        d_ref[...] = d
        nm_ref[...] = nm
        nv_ref[...] = nv

    row = pl.BlockSpec((tr, c), lambda i: (i, 0))
    return pl.pallas_call(
        body, name=name, grid=(r // tr,), in_specs=[pl.BlockSpec((N_DEV, tr, c), lambda i: (0, i, 0)), row, row, row],
        out_specs=[row] * 4, out_shape=[jax.ShapeDtypeStruct((r, c), F32)] * 4,
        compiler_params=_cparams(("parallel",)))(parts, w, m, v)


def _block_diag(t):
    nb, g, a, b = t.shape
    eye = jnp.eye(g, dtype=t.dtype)
    return jnp.einsum('ngab,gh->ngahb', t, eye).reshape(nb, g * a, g * b)


def _diag_blocks(t, a, b):
    nb = t.shape[0]
    g = S5_GROUPS_PER_BLOCK
    t = t.reshape(nb, g, a, g, b)
    return jnp.stack([t[:, j, :, j, :] for j in range(g)], axis=1)


def _local_step(x, mem, target, p):
    s, d = x.shape
    gw = d // 2
    nh = gw // GDN_HEAD_DIM
    ng = gw // S5_GROUP
    nb = ng // S5_GROUPS_PER_BLOCK
    nl = ng * S5_STATE
    ts = min(256, s)
    nt = s // ts
    grads = {}

    w_main, w_ba = p['w_main'], p['w_ba']
    CB_ZA, CB_XB, CB_ZB, CB_QC, CB_ZC, CB_G = 3, 4, 5, 6, 7, 8

    u = _tile_fwd(_rms, "rms_fwd", nt, [_rt(x, ts)], [p['norm_g']],
                  [((s, d), BF16, (ts, d), lambda i: (i, 0))])[0]
    proj = _mm(u, w_main, name="proj_main")
    pba = _mm(u, w_ba, name="proj_ba")

    conv_w = p['conv_w']
    col = lambda arr, cb: (arr, (s, GDN_HEAD_DIM), lambda i, cb=cb: (0, cb + i))
    qkv = []
    for j, mode in enumerate(('q', 'k', 'v')):
        off = j * nh
        qkv.append(_tile_fwd(
            _gdn_pre(mode), "gdn_pre_" + mode, nh, [col(proj, off), (conv_w, (CONV_WIDTH, GDN_HEAD_DIM), lambda i, off=off: (0, off + i))],
            [], [((s, gw), F32, (s, GDN_HEAD_DIM), lambda i: (0, i))])[0])
    q, k, v = qkv
    lane = jnp.arange(128)[:, None]
    colh = jnp.arange(gw)[None, :] // GDN_HEAD_DIM
    e_beta = (lane == colh).astype(F32)
    e_g = (lane == colh + nh).astype(F32)
    alog_row = jnp.pad(p['gdn_a_log'], ((0, 0), (nh, 128 - 2 * nh)))
    dtb_row = jnp.pad(p['gdn_dt_bias'], ((0, 0), (nh, 128 - 2 * nh)))
    row_gw = lambda: ((s, gw), F32, (ts, gw), lambda i: (i, 0))
    betab, gb = _tile_fwd(_gdn_gates, "gdn_gates", nt, [_rt(pba, ts)], [alog_row, dtb_row, e_beta, e_g],
                          [row_gw(), row_gw()])
    o_raw, states = _gdn_fwd(q, k, v, gb, betab)
    ga = _tile_fwd(_gdn_post, "gdn_post", nt, [_rt(o_raw, ts), _rt(proj, ts, CB_ZA, gw)], [p['gdn_norm_g']],
                   [((s, gw), BF16, (ts, gw), lambda i: (i, 0))])[0]

    e_rep = (jnp.arange(S5_STATE)[:, None] == jnp.arange(S5_STATE * S5_GROUP)[None, :] // S5_GROUP).astype(F32)
    s5_in = [p['s5_lambda_re'], p['s5_lambda_im'], p['s5_log_dt'].reshape(ng, 1),
             p['s5_b_re'].reshape(ng, S5_STATE * S5_GROUP), p['s5_b_im'].reshape(ng, S5_STATE * S5_GROUP), e_rep]
    one = lambda shp: (shp, F32, shp, lambda i, n=len(shp): (0,) * n)
    ab_re, ab_im, bbr, bbi = _tile_fwd(_s5_params, "s5_params", 1, [], s5_in,
                                       [one((ng, S5_STATE)), one((ng, S5_STATE)), one((ng, S5_STATE * S5_GROUP)),
                                        one((ng, S5_STATE * S5_GROUP))])
    coef = _s5_coef(ab_re.reshape(1, nl), ab_im.reshape(1, nl))
    to_bd_b = lambda t: _block_diag(t.reshape(nb, S5_GROUPS_PER_BLOCK, S5_STATE, S5_GROUP).transpose(0, 1, 3, 2))
    to_bd_c = lambda t: _block_diag(t.reshape(nb, S5_GROUPS_PER_BLOCK, S5_GROUP, S5_STATE).transpose(0, 1, 3, 2))
    bbd_re, bbd_im = to_bd_b(bbr).astype(BF16), to_bd_b(bbi).astype(BF16)
    cbd_re, cbd_im = to_bd_c(p['s5_c_re']).astype(BF16), to_bd_c(p['s5_c_im']).astype(BF16)
    xb_arr = lax.slice_in_dim(proj, CB_XB * gw, (CB_XB + 1) * gw, axis=1)
    h_re, h_im, ylin = _s5_fwd(xb_arr, bbd_re, bbd_im, cbd_re, cbd_im, coef, ts)
    gl = _tile_fwd(_s5_post1, "s5_post1", nt, [_rt(ylin, ts), _rt(proj, ts, CB_XB, gw)], [p['s5_d']],
                   [((s, gw), BF16, (ts, gw), lambda i: (i, 0))])[0]
    tglu = _mm(gl, p['s5_w_glu'], name="s5_glu")
    gbb = _tile_fwd(_s5_post2, "s5_post2", nt, [_rt(tglu, ts), _rt(proj, ts, CB_ZB, gw)], [],
                    [((s, gw), BF16, (ts, gw), lambda i: (i, 0))])[0]

    m_len = mem.shape[0]
    mem_n = _tile_fwd(_rms, "mem_rms", 1, [_rt(mem, m_len)], [p['mem_norm_g']],
                      [((m_len, d), BF16, (m_len, d), lambda i: (i, 0))])[0]
    kv = _mm(mem_n, p['w_kv_mem'], name="mem_kv")
    gcc = _tile_fwd(_attn, "attn", nt, [_rt(proj, ts, CB_QC, gw), _rt(proj, ts, CB_ZC, gw)], [kv],
                    [((s, gw), BF16, (ts, gw), lambda i: (i, 0))])[0]

    p_a = _mm(ga, p['w_br_a'], name="br_a")
    p_b = _mm(gbb, p['w_br_b'], name="br_b")
    p_c = _mm(gcc, p['w_br_c'], name="br_c")
    gate_acts = [_rt(proj, ts, CB_G // 2 + j, d) for j in range(3)]
    merged = _tile_fwd(_merge, "merge", nt, gate_acts + [_rt(p_a, ts), _rt(p_b, ts), _rt(p_c, ts)], [],
                       [((s, d), BF16, (ts, d), lambda i: (i, 0))])[0]
    mo = _mm(merged, p['w_out'], name="out_proj")
    dh, dfg, loss = _final(x, mo, target, p['final_g'].reshape(1, d), ts)
    grads['final_g'] = dfg.reshape(d)

    dmerged = _mm(dh, p['w_out'], tb=True, name="d_merged")
    grads['w_out'] = _mm(merged, dh, ta=True, name="dw_out")
    row_d = lambda dt: ((s, d), dt, (ts, d), lambda i: (i, 0))
    dg0, dg1, dg2, dpa, dpb, dpc = _tile_bwd(
        _merge, "merge_bwd", nt, gate_acts + [_rt(p_a, ts), _rt(p_b, ts), _rt(p_c, ts)], [], [_rt(dmerged, ts)],
        [row_d(BF16)] * 6, [])
    dga = _mm(dpa, p['w_br_a'], tb=True, name="d_ga")
    dgbb = _mm(dpb, p['w_br_b'], tb=True, name="d_gb")
    dgcc = _mm(dpc, p['w_br_c'], tb=True, name="d_gc")
    grads['w_br_a'] = _mm(ga, dpa, ta=True, name="dw_br_a")
    grads['w_br_b'] = _mm(gbb, dpb, ta=True, name="dw_br_b")
    grads['w_br_c'] = _mm(gcc, dpc, ta=True, name="dw_br_c")
    row_h = lambda dt: ((s, gw), dt, (ts, gw), lambda i: (i, 0))

    dqc, dzc, dkv = _tile_bwd(_attn, "attn_bwd", nt, [_rt(proj, ts, CB_QC, gw), _rt(proj, ts, CB_ZC, gw)], [kv],
                              [_rt(dgcc, ts)], [row_h(BF16), row_h(BF16)], [True])
    grads['w_kv_mem'] = _mm(mem_n, dkv, ta=True, name="dw_kv")
    dmem_n = _mm(dkv, p['w_kv_mem'], tb=True, name="d_mem_n")
    grads['mem_norm_g'] = _tile_bwd(_rms, "mem_rms_bwd", 1, [_rt(mem, m_len)], [p['mem_norm_g']],
                                    [_rt(dmem_n, m_len)], [None], [True])[0]

    dtglu, dzb = _tile_bwd(_s5_post2, "s5_post2_bwd", nt, [_rt(tglu, ts), _rt(proj, ts, CB_ZB, gw)], [],
                           [_rt(dgbb, ts)], [((s, 2 * gw), BF16, (ts, 2 * gw), lambda i: (i, 0)), row_h(BF16)], [])
    grads['s5_w_glu'] = _mm(gl, dtglu, ta=True, name="dw_glu")
    dgl = _mm(dtglu, p['s5_w_glu'], tb=True, name="d_gl")
    dylin, dxb1, dd = _tile_bwd(_s5_post1, "s5_post1_bwd", nt, [_rt(ylin, ts), _rt(proj, ts, CB_XB, gw)],
                                [p['s5_d']], [_rt(dgl, ts)], [row_h(F32), row_h(F32)], [True])
    grads['s5_d'] = dd
    dxb2, dbbd_re, dbbd_im, dcbd_re, dcbd_im, da_re, da_im = _s5_bwd(dylin, xb_arr, h_re, h_im, bbd_re, bbd_im,
                                                                    cbd_re, cbd_im, coef, ts)
    from_bd_b = lambda t: _diag_blocks(t, S5_GROUP, S5_STATE).transpose(0, 1, 3, 2).reshape(ng, S5_STATE * S5_GROUP)
    from_bd_c = lambda t: _diag_blocks(t, S5_STATE, S5_GROUP).transpose(0, 1, 3, 2).reshape(1, ng, S5_GROUP, S5_STATE)
    grads['s5_c_re'], grads['s5_c_im'] = from_bd_c(dcbd_re), from_bd_c(dcbd_im)
    s5_cts = [jnp.sum(da_re, axis=0).reshape(ng, S5_STATE), jnp.sum(da_im, axis=0).reshape(ng, S5_STATE),
              from_bd_b(dbbd_re), from_bd_b(dbbd_im)]
    dlr, dli, dlogdt, dbr, dbi = _tile_bwd(_s5_params, "s5_params_bwd", 1, [], s5_in,
                                           [(c, c.shape, lambda i: (0, 0)) for c in s5_cts], [],
                                           [True, True, True, True, True, False])
    grads['s5_lambda_re'], grads['s5_lambda_im'] = dlr[None], dli[None]
    grads['s5_log_dt'] = dlogdt.reshape(1, ng)
    grads['s5_b_re'] = dbr.reshape(1, ng, S5_STATE, S5_GROUP)
    grads['s5_b_im'] = dbi.reshape(1, ng, S5_STATE, S5_GROUP)
    dxb = (dxb1 + dxb2).astype(BF16)

    do_raw, dza, dgng = _tile_bwd(_gdn_post, "gdn_post_bwd", nt, [_rt(o_raw, ts), _rt(proj, ts, CB_ZA, gw)],
                                  [p['gdn_norm_g']], [_rt(dga, ts)], [row_h(F32), row_h(BF16)], [True])
    grads['gdn_norm_g'] = dgng
    dq, dk, dv, dgb, dbetab = _gdn_bwd(q, k, v, gb, betab, states, do_raw)
    dpba, dalog, ddtb = _tile_bwd(_gdn_gates, "gdn_gates_bwd", nt, [_rt(pba, ts)], [alog_row, dtb_row, e_beta, e_g],
                                  [_rt(dbetab, ts), _rt(dgb, ts)], [((s, 128), BF16, (ts, 128), lambda i: (i, 0))],
                                  [True, True, False, False])
    grads['gdn_a_log'] = dalog[:, nh:2 * nh]
    grads['gdn_dt_bias'] = ddtb[:, nh:2 * nh]
    dqkv, dconv = [], []
    for j, (mode, ct) in enumerate((('q', dq), ('k', dk), ('v', dv))):
        off = j * nh
        wspec = (conv_w, (CONV_WIDTH, GDN_HEAD_DIM), lambda i, off=off: (0, off + i))
        dxc, dwc = _tile_bwd(
            _gdn_pre(mode), "gdn_pre_bwd_" + mode, nh, [col(proj, off), wspec], [], [col(ct, 0)],
            [((s, gw), BF16, (s, GDN_HEAD_DIM), lambda i: (0, i)),
             ((CONV_WIDTH, gw), F32, (CONV_WIDTH, GDN_HEAD_DIM), lambda i: (0, i))], [])
        dqkv.append(dxc)
        dconv.append(dwc)
    grads['conv_w'] = jnp.concatenate(dconv, axis=1)

    dproj = jnp.concatenate(dqkv + [dza, dxb, dzb, dqc, dzc, dg0, dg1, dg2], axis=1)
    du = _mm(dpba, w_ba, tb=True, name="du_ba")
    du = _mm(dproj, w_main, tb=True, addend=du, name="du_main")
    grads['w_main'] = _mm(u, dproj, ta=True, name="dw_main")
    grads['w_ba'] = _mm(u, dpba, ta=True, name="dw_ba")
    dx, dng = _tile_bwd(_rms, "rms_bwd", nt, [_rt(x, ts)], [p['norm_g']], [_rt(du, ts)], [row_d(F32)], [True])
    grads['norm_g'] = dng
    return loss, dx + dh, grads


def _to_shards(name, g):
    if SHARDED[name] == 'row':
        return g.reshape((N_DEV, g.shape[0] // N_DEV) + g.shape[1:])
    r, c = g.shape
    return g.reshape(r, N_DEV, c // N_DEV).transpose(1, 0, 2)


def _from_shards(name, t):
    if SHARDED[name] == 'row':
        return t.reshape((t.shape[0] * t.shape[1],) + t.shape[2:])
    n, r, c = t.shape
    return t.transpose(1, 0, 2).reshape(r, n * c)


def _step(x, mem, target, w, m, v):
    sharded = list(SHARDED)
    shard_shapes = {n: tuple(w[n].shape[1:]) for n in sharded}
    d = x.shape[-1]
    ba_lo = 2 * d
    ba_hi = ba_lo + 2 * (d // 2 // GDN_HEAD_DIM)

    srcs = [w[n][0].astype(BF16) for n in GATHER_BF16] + [w['conv_w'][0]]
    got = _exchange(srcs, [_send_all] * len(srcs), [s.shape for s in srcs], "gather_weights")
    full = {n: t for n, t in zip(GATHER_BF16 + ['conv_w'], got)}
    full['w_main'], full['w_ba'] = _w_in_from_shards(full.pop('w_in'), ba_lo, ba_hi)
    for n in sharded:
        if n != 'w_in':
            full[n] = _from_shards(n, full[n])
    for n in REPLICATED:
        full[n] = w[n]
    for n in ('s5_lambda_re', 's5_lambda_im', 's5_c_re', 's5_c_im'):
        full[n] = w[n][0]

    loss, grad_x, grads = _local_step(x[0], mem[0], target[0], full)

    grads['w_in'] = _w_in_to_shards(grads.pop('w_main'), grads.pop('w_ba'), ba_lo, ba_hi)
    picks = [_send_slot if n == 'w_in' else
             (_send_rows(shard_shapes[n][0]) if SHARDED[n] == 'row' else _send_cols(shard_shapes[n][1]))
             for n in sharded]
    parts = _exchange([grads[n] for n in sharded], picks, [shard_shapes[n] for n in sharded], "scatter_grads")
    res = {}
    for n, part in zip(sharded, parts):
        outs = _sum_adam(part, w[n][0], m[n][0], v[n][0], name="adam_" + n)
        for kind, t in zip(('grad', 'delta', 'new_m', 'new_v'), outs):
            res[kind, n] = t[None]

    small = _pack([grads[n].reshape(w[n].shape) for n in REPLICATED] + [loss[:1, :1]], F32)
    allp = _exchange([small], [_send_all], [small.shape], "gather_small")[0]
    zero = jnp.zeros((1, 1), F32)
    outs = _sum_adam(allp, *[_pack([t[n] for n in REPLICATED] + [zero], F32) for t in (w, m, v)], name="adam_small")
    shapes = [w[n].shape for n in REPLICATED] + [(1, 1)]
    for kind, buf in zip(('grad', 'delta', 'new_m', 'new_v'), outs):
        got = _unpack(buf, shapes)
        for n, t in zip(REPLICATED, got):
            res[kind, n] = t
        if kind == 'grad':
            total_loss = got[-1].reshape(())
    out = [total_loss, grad_x[None]]
    for kind in ('grad', 'delta', 'new_m', 'new_v'):
        out += [res[kind, n] for n in WEIGHTS]
    return tuple(out)


def kernel(x, mem, norm_g, w_in, conv_w, gdn_a_log, gdn_dt_bias, gdn_norm_g, s5_lambda_re, s5_lambda_im, s5_log_dt, s5_b_re, s5_b_im, s5_c_re, s5_c_im, s5_d, s5_w_glu, mem_norm_g, w_kv_mem, w_br_a, w_br_b, w_br_c, w_out, final_g, loss_target, m_norm_g, m_w_in, m_conv_w, m_gdn_a_log, m_gdn_dt_bias, m_gdn_norm_g, m_s5_lambda_re, m_s5_lambda_im, m_s5_log_dt, m_s5_b_re, m_s5_b_im, m_s5_c_re, m_s5_c_im, m_s5_d, m_s5_w_glu, m_mem_norm_g, m_w_kv_mem, m_w_br_a, m_w_br_b, m_w_br_c, m_w_out, m_final_g, v_norm_g, v_w_in, v_conv_w, v_gdn_a_log, v_gdn_dt_bias, v_gdn_norm_g, v_s5_lambda_re, v_s5_lambda_im, v_s5_log_dt, v_s5_b_re, v_s5_b_im, v_s5_c_re, v_s5_c_im, v_s5_d, v_s5_w_glu, v_mem_norm_g, v_w_kv_mem, v_w_br_a, v_w_br_b, v_w_br_c, v_w_out, v_final_g):
    a = dict(locals())
    w = {n: a[n] for n in WEIGHTS}
    m = {n: a['m_' + n] for n in WEIGHTS}
    v = {n: a['v_' + n] for n in WEIGHTS}
    return _step(x, mem, loss_target, w, m, v)
```

```python
import functools
import math

import jax
import jax.numpy as jnp
from jax import lax
from jax.experimental import pallas as pl
from jax.experimental.pallas import tpu as pltpu

F32 = jnp.float32
BF16 = jnp.bfloat16
HI = lax.Precision.HIGHEST

EPS = 1e-6
CHUNK = 64
GDN_HEAD_DIM = 128
CONV_WIDTH = 4
S5_GROUP = 16
S5_STATE = 64
S5_GROUPS_PER_BLOCK = 8
XA_HEADS = 4
N_DEV = 8
ADAM_LR, ADAM_B1, ADAM_B2, ADAM_EPS, ADAM_WD, ADAM_STEP = 0.001, 0.9, 0.999, 1e-08, 0.01, 10

VMEM_LIMIT_BYTES = 56 * 1024 * 1024
SCAN_LANES = 512
PACK_WIDTH = 512
PACK_ROWS = 256

WEIGHTS = ['norm_g', 'w_in', 'conv_w', 'gdn_a_log', 'gdn_dt_bias', 'gdn_norm_g', 's5_lambda_re', 's5_lambda_im',
           's5_log_dt', 's5_b_re', 's5_b_im', 's5_c_re', 's5_c_im', 's5_d', 's5_w_glu', 'mem_norm_g', 'w_kv_mem',
           'w_br_a', 'w_br_b', 'w_br_c', 'w_out', 'final_g']
SHARDED = {'w_in': 'col', 'conv_w': 'col', 's5_w_glu': 'col', 'w_kv_mem': 'row', 'w_br_a': 'col', 'w_br_b': 'col',
           'w_br_c': 'col', 'w_out': 'row'}
GATHER_BF16 = ['w_in', 's5_w_glu', 'w_kv_mem', 'w_br_a', 'w_br_b', 'w_br_c', 'w_out']
REPLICATED = [n for n in WEIGHTS if n not in SHARDED]


def _cparams(sem=None):
    return pltpu.CompilerParams(dimension_semantics=sem, vmem_limit_bytes=VMEM_LIMIT_BYTES)


def _pick(dim, pref):
    t = (min(pref, dim) // 128) * 128
    while t >= 128:
        if dim % t == 0:
            return t
        t -= 128
    return dim


def _make_dots(prep, precision):
    def raw(a, b, dims):
        return lax.dot_general(prep(a), prep(b), (dims, ((), ())), preferred_element_type=F32, precision=precision)

    @jax.custom_vjp
    def nn(a, b):
        return raw(a, b, ((1,), (0,)))

    @jax.custom_vjp
    def nt(a, b):
        return raw(a, b, ((1,), (1,)))

    @jax.custom_vjp
    def tn(a, b):
        return raw(a, b, ((0,), (0,)))

    nn.defvjp(lambda a, b: (nn(a, b), (a, b)), lambda r, ct: (nt(ct, r[1]), tn(r[0], ct)))
    nt.defvjp(lambda a, b: (nt(a, b), (a, b)), lambda r, ct: (nn(ct, r[1]), tn(ct, r[0])))
    tn.defvjp(lambda a, b: (tn(a, b), (a, b)), lambda r, ct: (nt(r[1], ct), nn(r[0], ct)))
    return nn, nt, tn


_bnn, _bnt, _btn = _make_dots(lambda a: a.astype(BF16), None)
_hnn, _hnt, _htn = _make_dots(lambda a: a.astype(F32), HI)
_mnn, _mnt, _mtn = _make_dots(lambda a: a.astype(F32), lax.Precision.HIGH)


def _mm(a, b, *, name, ta=False, tb=False, out_dtype=F32, addend=None, tm=512, tn=1024, tk=1024):
    m, k = (a.shape[1], a.shape[0]) if ta else a.shape
    n = b.shape[0] if tb else b.shape[1]
    assert (b.shape[1] if tb else b.shape[0]) == k, (a.shape, b.shape, ta, tb)
    tm, tn, tk = _pick(m, tm), _pick(n, tn), _pick(k, tk)
    nk = k // tk
    dims = ((0 if ta else 1,), (1 if tb else 0,))

    def body(*refs):
        if addend is None:
            a_ref, b_ref, o_ref, acc_ref = refs
        else:
            a_ref, b_ref, add_ref, o_ref, acc_ref = refs
        kk = pl.program_id(2)

        @pl.when(kk == 0)
        def _():
            acc_ref[...] = jnp.zeros_like(acc_ref)

        acc_ref[...] += lax.dot_general(a_ref[...].astype(BF16), b_ref[...].astype(BF16), (dims, ((), ())),
                                        preferred_element_type=F32)

        @pl.when(kk == nk - 1)
        def _():
            r = acc_ref[...]
            if addend is not None:
                r = r + add_ref[...].astype(F32)
            o_ref[...] = r.astype(o_ref.dtype)

    a_spec = pl.BlockSpec((tk, tm), lambda i, j, kk: (kk, i)) if ta else pl.BlockSpec((tm, tk), lambda i, j, kk: (i, kk))
    b_spec = pl.BlockSpec((tn, tk), lambda i, j, kk: (j, kk)) if tb else pl.BlockSpec((tk, tn), lambda i, j, kk: (kk, j))
    o_spec = pl.BlockSpec((tm, tn), lambda i, j, kk: (i, j))
    in_specs = [a_spec, b_spec] + ([o_spec] if addend is not None else [])
    args = (a, b) + ((addend,) if addend is not None else ())
    return pl.pallas_call(
        body, name=name, grid=(m // tm, n // tn, nk), in_specs=in_specs, out_specs=o_spec,
        out_shape=jax.ShapeDtypeStruct((m, n), out_dtype), scratch_shapes=[pltpu.VMEM((tm, tn), F32)],
        compiler_params=_cparams(("parallel", "parallel", "arbitrary")))(*args)


def _rt(arr, ts, cb=0, w=None):
    w = arr.shape[1] if w is None else w
    return (arr, (ts, w), lambda i, cb=cb: (i, cb))


def _whole(p):
    return pl.BlockSpec(p.shape, lambda i, nd=p.ndim: (0,) * nd)


def _tile_fwd(f, name, n, acts, params, outs):
    na, npar = len(acts), len(params)

    def body(*refs):
        res = f(*[r[...] for r in refs[:na + npar]])
        for r, v in zip(refs[na + npar:], res):
            r[...] = v.astype(r.dtype)

    in_specs = [pl.BlockSpec(b, m) for _, b, m in acts] + [_whole(p) for p in params]
    out = pl.pallas_call(
        body, name=name, grid=(n,), in_specs=in_specs,
        out_specs=[pl.BlockSpec(b, m) for _, _, b, m in outs],
        out_shape=[jax.ShapeDtypeStruct(s, d) for s, d, _, _ in outs],
        compiler_params=_cparams(("parallel",)))(*[a for a, _, _ in acts], *params)
    return out


def _tile_bwd(f, name, n, acts, params, cts, agrads, pgrads):
    na, npar, nc = len(acts), len(params), len(cts)

    def body(*refs):
        i = pl.program_id(0)
        ins = [r[...] for r in refs[:na + npar]]
        outs, vjp = jax.vjp(f, *ins)
        g = vjp(tuple(c[...].astype(o.dtype) for c, o in zip(refs[na + npar:na + npar + nc], outs)))
        orefs = refs[na + npar + nc:]
        k = 0
        for j in range(na):
            if agrads[j] is not None:
                orefs[k][...] = g[j].astype(orefs[k].dtype)
                k += 1
        for j in range(npar):
            if pgrads[j]:
                o = orefs[k]

                @pl.when(i == 0)
                def _(o=o):
                    o[...] = jnp.zeros_like(o)

                o[...] += g[na + j].astype(F32)
                k += 1

    in_specs = ([pl.BlockSpec(b, m) for _, b, m in acts] + [_whole(p) for p in params]
                + [pl.BlockSpec(b, m) for _, b, m in cts])
    out_specs = [pl.BlockSpec(g[2], g[3]) for g in agrads if g is not None]
    out_shape = [jax.ShapeDtypeStruct(g[0], g[1]) for g in agrads if g is not None]
    for p, flag in zip(params, pgrads):
        if flag:
            out_specs.append(_whole(p))
            out_shape.append(jax.ShapeDtypeStruct(p.shape, F32))
    return pl.pallas_call(
        body, name=name, grid=(n,), in_specs=in_specs, out_specs=out_specs, out_shape=out_shape,
        compiler_params=_cparams(("arbitrary",)))(*[a for a, _, _ in acts], *params, *[c for c, _, _ in cts])


def _silu(x):
    return x * jax.nn.sigmoid(x)


def _rms(x, g):
    x = x.astype(F32)
    return (x * lax.rsqrt(jnp.mean(x * x, axis=-1, keepdims=True) + EPS) * g,)


def _shift_down(x, s):
    row = lax.broadcasted_iota(jnp.int32, x.shape, 0)
    return jnp.where(row >= s, pltpu.roll(x, s, 0), 0.0)


def _shift_up(x, s):
    n = x.shape[0]
    row = lax.broadcasted_iota(jnp.int32, x.shape, 0)
    return jnp.where(row < n - s, pltpu.roll(x, n - s, 0), 0.0)


@functools.partial(jax.custom_vjp, nondiff_argnums=(1,))
def _shift(x, s):
    return _shift_down(x, s)


_shift.defvjp(lambda x, s: (_shift_down(x, s), None), lambda s, _, ct: (_shift_up(ct, s),))


def _gdn_pre(mode):
    def f(x, w):
        y = x * w[CONV_WIDTH - 1:CONV_WIDTH, :]
        for j in range(CONV_WIDTH - 1):
            y = y + _shift(x, CONV_WIDTH - 1 - j) * w[j:j + 1, :]
        y = _silu(y)
        if mode != 'v':
            y = y * lax.rsqrt(jnp.sum(y * y, axis=-1, keepdims=True) + EPS)
        if mode == 'q':
            y = y * (GDN_HEAD_DIM ** -0.5)
        return (y,)
    return f


def _softplus(x):
    return jnp.maximum(x, 0.0) + jnp.log1p(jnp.exp(-jnp.abs(x)))


def _gdn_gates(ba, alog, dtb, e_beta, e_g):
    beta = jax.nn.sigmoid(ba)
    g = -jnp.exp(alog) * _softplus(ba + dtb)
    return _hnn(beta, lax.stop_gradient(e_beta)), _hnn(g, lax.stop_gradient(e_g))


def _gdn_intra(q, k, v, gb, bb):
    n, c = len(q), q[0].shape[0]
    ri = lax.broadcasted_iota(jnp.int32, (c, c), 0)
    ci = lax.broadcasted_iota(jnp.int32, (c, c), 1)
    incl, strict = ri >= ci, ri > ci
    tri = incl.astype(F32)
    eye = (ri == ci).astype(F32)
    each = range(n)
    gc = [_hnn(tri, gb[i]) for i in each]
    decay = [jnp.exp(jnp.where(incl, gc[i][:, :c] - gc[i].T[:c, :], -1e30)) for i in each]
    kb = [k[i] * bb[i] for i in each]
    kk = [_bnt(kb[i], k[i]) for i in each]
    qk = [_bnt(q[i], k[i]) for i in each]
    p = [jnp.where(strict, -(kk[i] * decay[i]), 0.0) for i in each]
    t = [eye + p[i] for i in each]
    for _ in range(int(math.log2(c)) - 1):
        p = [_mnn(p[i], p[i]) for i in each]
        tp = [_mnn(t[i], p[i]) for i in each]
        t = [t[i] + tp[i] for i in each]
    egc = [jnp.exp(gc[i]) for i in each]
    u_val = [_mnn(t[i], v[i] * bb[i]) for i in each]
    w_dec = [_mnn(t[i], kb[i] * egc[i]) for i in each]
    qk = [qk[i] * decay[i] for i in each]
    gl = [jnp.sum(gb[i], axis=0, keepdims=True) for i in each]
    return w_dec, u_val, qk, [q[i] * egc[i] for i in each], [k[i] * jnp.exp(gl[i] - gc[i]) for i in each]


def _gdn_inter(w_dec, u_val, qk, q_dec, k_dec, gb, state):
    each = range(len(state))
    ws = [_bnn(w_dec[i], state[i]) for i in each]
    qs = [_bnn(q_dec[i], state[i]) for i in each]
    v_new = [u_val[i] - ws[i] for i in each]
    qv = [_bnn(qk[i], v_new[i]) for i in each]
    kv = [_btn(k_dec[i], v_new[i]) for i in each]
    decayed = [state[i] * jnp.exp(jnp.sum(gb[i], axis=0, keepdims=True)) for i in each]
    return [qs[i] + qv[i] for i in each], [decayed[i] + kv[i] for i in each]


def _gdn_post(o, z, g):
    parts = []
    for h in range(o.shape[1] // GDN_HEAD_DIM):
        oh = o[:, h * GDN_HEAD_DIM:(h + 1) * GDN_HEAD_DIM]
        parts.append(oh * lax.rsqrt(jnp.mean(oh * oh, axis=-1, keepdims=True) + EPS) * g)
    y = parts[0] if len(parts) == 1 else jnp.concatenate(parts, axis=1)
    return (y * _silu(z),)


def _gelu(x):
    return 0.5 * x * (1.0 + jnp.tanh(0.7978845608028654 * (x + 0.044715 * x * x * x)))


def _s5_post1(ylin, xb, d):
    return (_gelu(ylin + d * xb),)


def _s5_post2(t, z):
    w = t.shape[1] // 2
    return (t[:, :w] * jax.nn.sigmoid(t[:, w:]) * _silu(z),)


def _attn(q, z, kv):
    w = q.shape[1]
    hd = w // XA_HEADS
    parts = []
    for h in range(XA_HEADS):
        s = _bnt(q[:, h * hd:(h + 1) * hd], kv[:, h * hd:(h + 1) * hd]) * (hd ** -0.5)
        s = s - jnp.max(s, axis=-1, keepdims=True)
        e = jnp.exp(s)
        p = e / jnp.sum(e, axis=-1, keepdims=True)
        parts.append(_bnn(p, kv[:, w + h * hd:w + (h + 1) * hd]))
    return (jnp.concatenate(parts, axis=1) * _silu(z),)


def _merge(g0, g1, g2, pa, pb, pc):
    return (jax.nn.sigmoid(g0) * pa + jax.nn.sigmoid(g1) * pb + jax.nn.sigmoid(g2) * pc,)


def _s5_params(lr, li, logdt, br, bi, e):
    dt = jnp.exp(logdt)
    mag = jnp.exp(lr * dt)
    ab_re, ab_im = mag * jnp.cos(li * dt), mag * jnp.sin(li * dt)
    den = lr * lr + li * li
    nr, ni = ab_re - 1.0, ab_im
    e = lax.stop_gradient(e)
    cre = _hnn((nr * lr + ni * li) / den, e)
    cim = _hnn((ni * lr - nr * li) / den, e)
    return ab_re, ab_im, cre * br - cim * bi, cre * bi + cim * br


def _gdn_blocks(s, w, per_step):
    nh, nc = w // GDN_HEAD_DIM, s // CHUNK
    cpb = math.gcd(per_step, nc)
    return nh, nc, cpb, nc // cpb, (cpb * CHUNK, w), (cpb * CHUNK, nh * CHUNK)


def _gdn_pairs(cpb, nh):
    wide, narrow = [], []
    for cb in range(cpb):
        rows = slice(cb * CHUNK, (cb + 1) * CHUNK)
        for h in range(nh):
            wide.append((rows, slice(h * GDN_HEAD_DIM, (h + 1) * GDN_HEAD_DIM)))
            narrow.append((rows, slice(h * CHUNK, (h + 1) * CHUNK)))
    return wide, narrow


def _gdn_intra_fwd(q, k, v, gb, bb, per_step=4):
    s, w = q.shape
    nh, nc, cpb, n, wide, narrow = _gdn_blocks(s, w, per_step)
    hd = GDN_HEAD_DIM

    def body(q_ref, k_ref, v_ref, g_ref, b_ref, wd_ref, uv_ref, qk_ref, qd_ref, kd_ref):
        wide, narrow = _gdn_pairs(cpb, nh)
        res = _gdn_intra(*[[r[ix] for ix in wide] for r in (q_ref, k_ref, v_ref, g_ref, b_ref)])
        for ref, vals, where in zip((wd_ref, uv_ref, qk_ref, qd_ref, kd_ref), res, (wide, wide, narrow, wide, wide)):
            for ix, val in zip(where, vals):
                ref[ix] = val

    bw = pl.BlockSpec(wide, lambda i: (i, 0))
    bn = pl.BlockSpec(narrow, lambda i: (i, 0))
    fw = jax.ShapeDtypeStruct((s, w), F32)
    return pl.pallas_call(
        body, name="gdn_intra", grid=(n,), in_specs=[bw] * 5, out_specs=[bw, bw, bn, bw, bw],
        out_shape=[fw, fw, jax.ShapeDtypeStruct((s, nh * CHUNK), F32), fw, fw],
        compiler_params=_cparams(("parallel",)))(q, k, v, gb, bb)


def _gdn_intra_bwd(q, k, v, gb, bb, cts, dgb_inter, per_step=2):
    s, w = q.shape
    nh, nc, cpb, n, wide, narrow = _gdn_blocks(s, w, per_step)
    hd = GDN_HEAD_DIM

    def body(q_ref, k_ref, v_ref, g_ref, b_ref, cwd, cuv, cqk, cqd, ckd, dgi, dq_ref, dk_ref, dv_ref, dg_ref, db_ref):
        wide, narrow = _gdn_pairs(cpb, nh)
        _, vjp = jax.vjp(_gdn_intra, *[[r[ix] for ix in wide] for r in (q_ref, k_ref, v_ref, g_ref, b_ref)])
        cts = tuple([r[ix] for ix in where] for r, where in zip((cwd, cuv, cqk, cqd, ckd),
                                                               (wide, wide, narrow, wide, wide)))
        dq, dk, dv, dg, db = vjp(cts)
        for j, ix in enumerate(wide):
            dq_ref[ix], dk_ref[ix], dv_ref[ix], db_ref[ix] = dq[j], dk[j], dv[j], db[j]
            dg_ref[ix] = dg[j] + dgi[ix]

    bw = pl.BlockSpec(wide, lambda i: (i, 0))
    bn = pl.BlockSpec(narrow, lambda i: (i, 0))
    return pl.pallas_call(
        body, name="gdn_intra_bwd", grid=(n,), in_specs=[bw] * 5 + [bw, bw, bn, bw, bw, bw], out_specs=[bw] * 5,
        out_shape=[jax.ShapeDtypeStruct((s, w), F32)] * 5,
        compiler_params=_cparams(("parallel",)))(q, k, v, gb, bb, *cts, dgb_inter)


def _gdn_inter_fwd(wd, uv, qk, qd, kd, gb, per_step=4):
    s, w = wd.shape
    nh, nc, cpb, n, wide, narrow = _gdn_blocks(s, w, per_step)
    hd = GDN_HEAD_DIM

    def body(wd_ref, uv_ref, qk_ref, qd_ref, kd_ref, g_ref, o_ref, st_ref, state):
        @pl.when(pl.program_id(0) == 0)
        def _():
            state[...] = jnp.zeros_like(state)

        wide, narrow = _gdn_pairs(cpb, nh)
        st = [state[h] for h in range(nh)]
        for cb in range(cpb):
            wi, na = wide[cb * nh:(cb + 1) * nh], narrow[cb * nh:(cb + 1) * nh]
            for h in range(nh):
                st_ref[cb, h] = st[h]
            o, st = _gdn_inter([wd_ref[ix] for ix in wi], [uv_ref[ix] for ix in wi], [qk_ref[ix] for ix in na],
                               [qd_ref[ix] for ix in wi], [kd_ref[ix] for ix in wi], [g_ref[ix] for ix in wi], st)
            for h in range(nh):
                o_ref[wi[h]] = o[h]
        for h in range(nh):
            state[h] = st[h]

    bw = pl.BlockSpec(wide, lambda i: (i, 0))
    bn = pl.BlockSpec(narrow, lambda i: (i, 0))
    return pl.pallas_call(
        body, name="gdn_inter", grid=(n,), in_specs=[bw, bw, bn, bw, bw, bw],
        out_specs=[bw, pl.BlockSpec((cpb, nh, hd, hd), lambda i: (i, 0, 0, 0))],
        out_shape=[jax.ShapeDtypeStruct((s, w), F32), jax.ShapeDtypeStruct((nc, nh, hd, hd), F32)],
        scratch_shapes=[pltpu.VMEM((nh, hd, hd), F32)],
        compiler_params=_cparams(("arbitrary",)))(wd, uv, qk, qd, kd, gb)


def _gdn_inter_bwd(wd, uv, qk, qd, kd, gb, states, do, per_step=4):
    s, w = wd.shape
    nh, nc, cpb, n, wide, narrow = _gdn_blocks(s, w, per_step)
    hd = GDN_HEAD_DIM

    def body(wd_ref, uv_ref, qk_ref, qd_ref, kd_ref, g_ref, st_ref, do_ref, cwd, cuv, cqk, cqd, ckd, dg_ref, dstate):
        @pl.when(pl.program_id(0) == 0)
        def _():
            dstate[...] = jnp.zeros_like(dstate)

        wide, narrow = _gdn_pairs(cpb, nh)
        dst = [dstate[h] for h in range(nh)]
        for cb in reversed(range(cpb)):
            wi, na = wide[cb * nh:(cb + 1) * nh], narrow[cb * nh:(cb + 1) * nh]
            _, vjp = jax.vjp(_gdn_inter, [wd_ref[ix] for ix in wi], [uv_ref[ix] for ix in wi],
                             [qk_ref[ix] for ix in na], [qd_ref[ix] for ix in wi], [kd_ref[ix] for ix in wi],
                             [g_ref[ix] for ix in wi], [st_ref[cb, h] for h in range(nh)])
            dwd, duv, dqk, dqd, dkd, dg, dst = vjp(([do_ref[ix] for ix in wi], dst))
            for h in range(nh):
                cwd[wi[h]], cuv[wi[h]], cqk[na[h]], cqd[wi[h]], ckd[wi[h]], dg_ref[wi[h]] = (
                    dwd[h], duv[h], dqk[h], dqd[h], dkd[h], dg[h])
        for h in range(nh):
            dstate[h] = dst[h]

    bw = pl.BlockSpec(wide, lambda i: (n - 1 - i, 0))
    bn = pl.BlockSpec(narrow, lambda i: (n - 1 - i, 0))
    fw = jax.ShapeDtypeStruct((s, w), F32)
    return pl.pallas_call(
        body, name="gdn_inter_bwd", grid=(n,),
        in_specs=[bw, bw, bn, bw, bw, bw, pl.BlockSpec((cpb, nh, hd, hd), lambda i: (n - 1 - i, 0, 0, 0)), bw],
        out_specs=[bw, bw, bn, bw, bw, bw],
        out_shape=[fw, fw, jax.ShapeDtypeStruct((s, nh * CHUNK), F32), fw, fw, fw],
        scratch_shapes=[pltpu.VMEM((nh, hd, hd), F32)],
        compiler_params=_cparams(("arbitrary",)))(wd, uv, qk, qd, kd, gb, states, do)


def _s5_coef(ar, ai):
    nl = ar.shape[1]

    def body(ar_ref, ai_ref, o_ref):
        row = lax.broadcasted_iota(jnp.int32, (8, nl), 0)
        for base, sign in ((0, 1.0), (8, -1.0)):
            pr = [jnp.broadcast_to(ar_ref[...], (8, nl))]
            pi = [jnp.broadcast_to(ai_ref[...], (8, nl)) * sign]
            for _ in range(7):
                pr.append(pr[-1] * pr[0] - pi[-1] * pi[0])
                pi.append(pr[-2] * pi[0] + pi[-1] * pr[0])
            for j, d in enumerate((1, 2, 4)):
                m = (row >= d) if base == 0 else (row <= 7 - d)
                o_ref[base + 2 * j] = jnp.where(m, pr[d - 1], 0.0)
                o_ref[base + 2 * j + 1] = jnp.where(m, pi[d - 1], 0.0)
            cr, ci = jnp.zeros((8, nl), F32), jnp.zeros((8, nl), F32)
            for t in range(8):
                e = t if base == 0 else 7 - t
                cr = jnp.where(row == t, pr[e], cr)
                ci = jnp.where(row == t, pi[e], ci)
            o_ref[base + 6] = cr
            o_ref[base + 7] = ci

    return pl.pallas_call(body, name="s5_coef", out_shape=jax.ShapeDtypeStruct((16, 8, nl), F32),
                          compiler_params=_cparams())(ar, ai)


def _scan_tile(src_re, src_im, dst_re, dst_im, coef_ref, carry_re, carry_im, ts, reverse, extra=None):
    nl = src_re.shape[1]
    base = 8 if reverse else 0
    ng = ts // 8
    for lc in range(nl // SCAN_LANES):
        ln = slice(lc * SCAN_LANES, (lc + 1) * SCAN_LANES)
        m = [coef_ref[base + j, :, ln] for j in range(8)]
        row = lax.broadcasted_iota(jnp.int32, (8, SCAN_LANES), 0)

        def step(r, carry, ln=ln, m=m, row=row):
            grp = (ng - 1 - r) if reverse else r
            rows = pl.ds(pl.multiple_of(grp * 8, 8), 8)
            xr, xi = src_re[rows, ln], src_im[rows, ln]
            for j, d in enumerate((1, 2, 4)):
                sh = 8 - d if reverse else d
                sr, si = pltpu.roll(xr, sh, 0), pltpu.roll(xi, sh, 0)
                mr, mi = m[2 * j], m[2 * j + 1]
                xr, xi = xr + mr * sr - mi * si, xi + mr * si + mi * sr
            cr, ci = carry[0], carry[1]
            hr = xr + m[6] * cr - m[7] * ci
            hi = xi + m[6] * ci + m[7] * cr
            dst_re[rows, ln] = hr
            dst_im[rows, ln] = hi
            edge = 0 if reverse else 7
            out = (jnp.broadcast_to(hr[edge:edge + 1, :], hr.shape), jnp.broadcast_to(hi[edge:edge + 1, :], hi.shape))
            if extra is not None:
                h_re, h_im, halo_re, halo_im, first, _, _ = extra
                prev = pl.ds(pl.multiple_of(jnp.maximum(grp - 1, 0) * 8, 8), 8)
                use_halo = grp == 0
                pr = jnp.where(use_halo, halo_re[:, ln] * first, h_re[prev, ln])
                pi = jnp.where(use_halo, halo_im[:, ln] * first, h_im[prev, ln])
                qr = jnp.where(row == 0, jnp.broadcast_to(pr[7:8, :], pr.shape), pltpu.roll(h_re[rows, ln], 1, 0))
                qi = jnp.where(row == 0, jnp.broadcast_to(pi[7:8, :], pi.shape), pltpu.roll(h_im[rows, ln], 1, 0))
                out = out + (carry[2] + hr * qr + hi * qi, carry[3] + hi * qr - hr * qi)
            return out

        init = (carry_re[:, ln], carry_im[:, ln])
        if extra is not None:
            init = init + (extra[5][:, ln], extra[6][:, ln])
        fin = lax.fori_loop(0, ng, step, init)
        carry_re[:, ln] = fin[0]
        carry_im[:, ln] = fin[1]
        if extra is not None:
            extra[5][:, ln] = fin[2]
            extra[6][:, ln] = fin[3]


def _s5_fwd(xb, bb_re, bb_im, c_re, c_im, coef, ts):
    s, w = xb.shape
    nb = bb_re.shape[0]
    nl = nb * 512

    def body(x_ref, bre_ref, bim_ref, cre_ref, cim_ref, coef_ref, hre_ref, him_ref, y_ref, ure, uim, car_re, car_im):
        @pl.when(pl.program_id(0) == 0)
        def _():
            car_re[...] = jnp.zeros_like(car_re)
            car_im[...] = jnp.zeros_like(car_im)

        for b in range(nb):
            xs = x_ref[:, b * 128:(b + 1) * 128].astype(BF16)
            ure[:, b * 512:(b + 1) * 512] = jnp.dot(xs, bre_ref[b], preferred_element_type=F32)
            uim[:, b * 512:(b + 1) * 512] = jnp.dot(xs, bim_ref[b], preferred_element_type=F32)
        _scan_tile(ure, uim, hre_ref, him_ref, coef_ref, car_re, car_im, ts, False)
        for b in range(nb):
            hr = hre_ref[:, b * 512:(b + 1) * 512].astype(BF16)
            hi = him_ref[:, b * 512:(b + 1) * 512].astype(BF16)
            y_ref[:, b * 128:(b + 1) * 128] = (jnp.dot(hr, cre_ref[b], preferred_element_type=F32)
                                               - jnp.dot(hi, cim_ref[b], preferred_element_type=F32))

    row = lambda wd: pl.BlockSpec((ts, wd), lambda i: (i, 0))
    return pl.pallas_call(
        body, name="s5_fwd", grid=(s // ts,),
        in_specs=[row(w), _whole(bb_re), _whole(bb_im), _whole(c_re), _whole(c_im), _whole(coef)],
        out_specs=[row(nl), row(nl), row(w)],
        out_shape=[jax.ShapeDtypeStruct((s, nl), F32), jax.ShapeDtypeStruct((s, nl), F32),
                   jax.ShapeDtypeStruct((s, w), F32)],
        scratch_shapes=[pltpu.VMEM((ts, nl), F32), pltpu.VMEM((ts, nl), F32), pltpu.VMEM((8, nl), F32),
                        pltpu.VMEM((8, nl), F32)],
        compiler_params=_cparams(("arbitrary",)))(xb, bb_re, bb_im, c_re, c_im, coef)


def _s5_bwd(dy, xb, h_re, h_im, bb_re, bb_im, c_re, c_im, coef, ts):
    s, w = xb.shape
    nb = bb_re.shape[0]
    nl = nb * 512
    nt = s // ts

    def body(dy_ref, x_ref, hre_ref, him_ref, halo_re, halo_im, bre_ref, bim_ref, cre_ref, cim_ref, coef_ref,
             dx_ref, dbre_ref, dbim_ref, dcre_ref, dcim_ref, dare_ref, daim_ref, gre, gim, car_re, car_im):
        i = pl.program_id(0)

        @pl.when(i == 0)
        def _():
            for r in (car_re, car_im, dbre_ref, dbim_ref, dcre_ref, dcim_ref, dare_ref, daim_ref):
                r[...] = jnp.zeros_like(r)

        for b in range(nb):
            dyb = dy_ref[:, b * 128:(b + 1) * 128].astype(BF16)
            gre[:, b * 512:(b + 1) * 512] = lax.dot_general(dyb, cre_ref[b], (((1,), (1,)), ((), ())),
                                                            preferred_element_type=F32)
            gim[:, b * 512:(b + 1) * 512] = -lax.dot_general(dyb, cim_ref[b], (((1,), (1,)), ((), ())),
                                                             preferred_element_type=F32)
            hr = hre_ref[:, b * 512:(b + 1) * 512].astype(BF16)
            hi = him_ref[:, b * 512:(b + 1) * 512].astype(BF16)
            dcre_ref[b] += lax.dot_general(hr, dyb, (((0,), (0,)), ((), ())), preferred_element_type=F32)
            dcim_ref[b] -= lax.dot_general(hi, dyb, (((0,), (0,)), ((), ())), preferred_element_type=F32)
        first = (i != nt - 1).astype(F32)
        _scan_tile(gre, gim, gre, gim, coef_ref, car_re, car_im, ts, True,
                   extra=(hre_ref, him_ref, halo_re, halo_im, first, dare_ref, daim_ref))
        for b in range(nb):
            gr = gre[:, b * 512:(b + 1) * 512].astype(BF16)
            gi = gim[:, b * 512:(b + 1) * 512].astype(BF16)
            xs = x_ref[:, b * 128:(b + 1) * 128].astype(BF16)
            dx_ref[:, b * 128:(b + 1) * 128] = (
                lax.dot_general(gr, bre_ref[b], (((1,), (1,)), ((), ())), preferred_element_type=F32)
                + lax.dot_general(gi, bim_ref[b], (((1,), (1,)), ((), ())), preferred_element_type=F32))
            dbre_ref[b] += lax.dot_general(xs, gr, (((0,), (0,)), ((), ())), preferred_element_type=F32)
            dbim_ref[b] += lax.dot_general(xs, gi, (((0,), (0,)), ((), ())), preferred_element_type=F32)

    row = lambda wd: pl.BlockSpec((ts, wd), lambda i: (nt - 1 - i, 0))
    halo = pl.BlockSpec((8, nl), lambda i: (jnp.maximum((nt - 1 - i) * (ts // 8) - 1, 0), 0))
    return pl.pallas_call(
        body, name="s5_bwd", grid=(nt,),
        in_specs=[row(w), row(w), row(nl), row(nl), halo, halo, _whole(bb_re), _whole(bb_im), _whole(c_re),
                  _whole(c_im), _whole(coef)],
        out_specs=[row(w), _whole(bb_re), _whole(bb_im), _whole(c_re), _whole(c_im),
                   pl.BlockSpec((8, nl), lambda i: (0, 0)), pl.BlockSpec((8, nl), lambda i: (0, 0))],
        out_shape=[jax.ShapeDtypeStruct((s, w), F32), jax.ShapeDtypeStruct(bb_re.shape, F32),
                   jax.ShapeDtypeStruct(bb_im.shape, F32), jax.ShapeDtypeStruct(c_re.shape, F32),
                   jax.ShapeDtypeStruct(c_im.shape, F32), jax.ShapeDtypeStruct((8, nl), F32),
                   jax.ShapeDtypeStruct((8, nl), F32)],
        scratch_shapes=[pltpu.VMEM((ts, nl), F32), pltpu.VMEM((ts, nl), F32), pltpu.VMEM((8, nl), F32),
                        pltpu.VMEM((8, nl), F32)],
        compiler_params=_cparams(("arbitrary",)))(dy, xb, h_re, h_im, h_re, h_im, bb_re, bb_im, c_re, c_im, coef)


def _final(x, mo, target, fg, ts):
    s, d = x.shape

    def f(x, mo, fg, tgt):
        y = _rms(x + mo, fg)[0]
        err = y - tgt
        return 0.5 * jnp.sum(jnp.mean(err * err, axis=-1, keepdims=True), axis=0, keepdims=True)

    def body(x_ref, mo_ref, t_ref, fg_ref, dh_ref, dfg_ref, loss_ref):
        @pl.when(pl.program_id(0) == 0)
        def _():
            dfg_ref[...] = jnp.zeros_like(dfg_ref)
            loss_ref[...] = jnp.zeros_like(loss_ref)

        loss, vjp = jax.vjp(f, x_ref[...], mo_ref[...], fg_ref[...], t_ref[...])
        _, dmo, dfg, _ = vjp(jnp.ones((1, 1), F32))
        dh_ref[...] = dmo
        dfg_ref[...] += dfg
        loss_ref[...] += jnp.broadcast_to(loss, loss_ref.shape)

    row = pl.BlockSpec((ts, d), lambda i: (i, 0))
    return pl.pallas_call(
        body, name="final", grid=(s // ts,), in_specs=[row, row, row, _whole(fg)],
        out_specs=[row, _whole(fg), pl.BlockSpec((8, 128), lambda i: (0, 0))],
        out_shape=[jax.ShapeDtypeStruct((s, d), F32), jax.ShapeDtypeStruct(fg.shape, F32),
                   jax.ShapeDtypeStruct((8, 128), F32)],
        compiler_params=_cparams(("arbitrary",)))(x, mo, target, fg)


def _peers(x, y, c):
    out = []
    for k in range(1, N_DEV):
        px = 1 - x if k & 4 else x
        py = 1 - y if k & 2 else y
        pc = 1 - c if k & 1 else c
        out.append(((px, py, pc), 4 * px + 2 * py + pc))
    return out


def _send_all(ref, idx):
    return ref


def _send_slot(ref, idx):
    return ref.at[idx]


def _send_rows(rows):
    return lambda ref, idx: ref.at[pl.ds(pl.multiple_of(idx * rows, 8), rows), :]


def _send_cols(cols):
    return lambda ref, idx: ref.at[:, pl.ds(pl.multiple_of(idx * cols, 128), cols)]


def _exchange(srcs, picks, blocks, name):
    n = len(srcs)

    def body(*refs):
        src_refs, out_refs = refs[:n], refs[n:2 * n]
        send_sems, recv_sems, local_sems = refs[2 * n:]
        x, y, c = lax.axis_index("x"), lax.axis_index("y"), lax.axis_index("c")
        me = 4 * x + 2 * y + c
        peers = _peers(x, y, c)
        local, sends = [], []
        for a in range(n):
            cp = pltpu.make_async_copy(picks[a](src_refs[a], me), out_refs[a].at[me], local_sems.at[a])
            cp.start()
            local.append(cp)
        for k, (pos, idx) in enumerate(peers):
            for a in range(n):
                cp = pltpu.make_async_remote_copy(
                    src_ref=picks[a](src_refs[a], idx), dst_ref=out_refs[a].at[me], send_sem=send_sems.at[a, k],
                    recv_sem=recv_sems.at[a, k], device_id=pos, device_id_type=pl.DeviceIdType.MESH)
                cp.start()
                sends.append(cp)
        for k, (pos, idx) in enumerate(peers):
            for a in range(n):
                pltpu.make_async_remote_copy(
                    src_ref=picks[a](src_refs[a], me), dst_ref=out_refs[a].at[idx], send_sem=send_sems.at[a, k],
                    recv_sem=recv_sems.at[a, k], device_id=pos, device_id_type=pl.DeviceIdType.MESH).wait_recv()
        for cp in sends:
            cp.wait_send()
        for cp in local:
            cp.wait()

    return pl.pallas_call(
        body, name=name, in_specs=[pl.BlockSpec(memory_space=pl.ANY)] * n,
        out_specs=[pl.BlockSpec(memory_space=pl.ANY)] * n,
        out_shape=[jax.ShapeDtypeStruct((N_DEV,) + tuple(b), s.dtype) for s, b in zip(srcs, blocks)],
        scratch_shapes=[pltpu.SemaphoreType.DMA((n, N_DEV - 1)), pltpu.SemaphoreType.DMA((n, N_DEV - 1)),
                        pltpu.SemaphoreType.DMA((n,))],
        compiler_params=pltpu.CompilerParams(has_side_effects=True))(*srcs)


def _pick_rows(r, pref):
    t = (min(pref, r) // 8) * 8
    while t >= 8:
        if r % t == 0:
            return t
        t -= 8
    return r


def _w_in_from_shards(t, lo, hi):
    n, r, cs = t.shape
    tr = _pick_rows(r, 256)
    wm = n * cs - (hi - lo)

    def body(t_ref, m_ref, b_ref):
        full = jnp.concatenate([t_ref[j] for j in range(n)], axis=1)
        m_ref[...] = jnp.concatenate([full[:, :lo], full[:, hi:]], axis=1)
        b_ref[...] = jnp.concatenate([full[:, lo:hi], jnp.zeros((tr, 128 - (hi - lo)), full.dtype)], axis=1)

    return pl.pallas_call(
        body, name="w_in_layout", grid=(r // tr,), in_specs=[pl.BlockSpec((n, tr, cs), lambda i: (0, i, 0))],
        out_specs=[pl.BlockSpec((tr, wm), lambda i: (i, 0)), pl.BlockSpec((tr, 128), lambda i: (i, 0))],
        out_shape=[jax.ShapeDtypeStruct((r, wm), t.dtype), jax.ShapeDtypeStruct((r, 128), t.dtype)],
        compiler_params=_cparams(("parallel",)))(t)


def _w_in_to_shards(gm, gb, lo, hi):
    r, wm = gm.shape
    cs = (wm + hi - lo) // N_DEV
    tr = _pick_rows(r, 64)

    def body(m_ref, b_ref, o_ref):
        m = m_ref[...]
        full = jnp.concatenate([m[:, :lo], b_ref[:, :hi - lo], m[:, lo:]], axis=1)
        for j in range(N_DEV):
            o_ref[j] = full[:, j * cs:(j + 1) * cs]

    return pl.pallas_call(
        body, name="dw_in_layout", grid=(r // tr,),
        in_specs=[pl.BlockSpec((tr, wm), lambda i: (i, 0)), pl.BlockSpec((tr, 128), lambda i: (i, 0))],
        out_specs=pl.BlockSpec((N_DEV, tr, cs), lambda i: (0, i, 0)),
        out_shape=jax.ShapeDtypeStruct((N_DEV, r, cs), gm.dtype), compiler_params=_cparams(("parallel",)))(gm, gb)


def _pack(arrs, dtype, lead=()):
    nlead = len(lead)
    flat = jnp.concatenate([a.astype(dtype).reshape(lead + (-1,)) for a in arrs], axis=nlead)
    n = flat.shape[-1]
    unit = PACK_WIDTH * PACK_ROWS
    pad = (-n) % unit
    flat = jnp.pad(flat, [(0, 0)] * nlead + [(0, pad)])
    return flat.reshape(lead + ((n + pad) // PACK_WIDTH, PACK_WIDTH))


def _unpack(buf, shapes, lead=()):
    flat = buf.reshape(lead + (-1,))
    out, off = [], 0
    for shp in shapes:
        n = math.prod(shp)
        out.append(flat[..., off:off + n].reshape(lead + tuple(shp)))
        off += n
    return out


def _adam_math(w, g, m, v):
    m = ADAM_B1 * m + (1.0 - ADAM_B1) * g
    v = ADAM_B2 * v + (1.0 - ADAM_B2) * (g * g)
    m_hat = m / (1.0 - ADAM_B1 ** ADAM_STEP)
    v_hat = v / (1.0 - ADAM_B2 ** ADAM_STEP)
    delta = -ADAM_LR * (m_hat / (jnp.sqrt(v_hat) + ADAM_EPS) + ADAM_WD * w)
    return delta, m, v


def _sum_adam(parts, w, m, v, name):
    r, c = w.shape
    lanes = -(-c // 128) * 128
    tr = _pick_rows(r, max(8, (6 * 1024 * 1024) // (N_DEV * lanes * 4)))

    def body(p_ref, w_ref, m_ref, v_ref, g_ref, d_ref, nm_ref, nv_ref):
        g = p_ref[0]
        for j in range(1, N_DEV):
            g = g + p_ref[j]
        d, nm, nv = _adam_math(w_ref[...], g, m_ref[...], v_ref[...])
        g_ref[...] = g
        d_ref[...] = d
        nm_ref[...] = nm
        nv_ref[...] = nv

    row = pl.BlockSpec((tr, c), lambda i: (i, 0))
    return pl.pallas_call(
        body, name=name, grid=(r // tr,), in_specs=[pl.BlockSpec((N_DEV, tr, c), lambda i: (0, i, 0)), row, row, row],
        out_specs=[row] * 4, out_shape=[jax.ShapeDtypeStruct((r, c), F32)] * 4,
        compiler_params=_cparams(("parallel",)))(parts, w, m, v)


def _block_diag(t):
    nb, g, a, b = t.shape
    eye = jnp.eye(g, dtype=t.dtype)
    return jnp.einsum('ngab,gh->ngahb', t, eye).reshape(nb, g * a, g * b)


def _diag_blocks(t, a, b):
    nb = t.shape[0]
    g = S5_GROUPS_PER_BLOCK
    t = t.reshape(nb, g, a, g, b)
    return jnp.stack([t[:, j, :, j, :] for j in range(g)], axis=1)


def _local_step(x, mem, target, p):
    s, d = x.shape
    gw = d // 2
    nh = gw // GDN_HEAD_DIM
    ng = gw // S5_GROUP
    nb = ng // S5_GROUPS_PER_BLOCK
    nl = ng * S5_STATE
    ts = min(256, s)
    nt = s // ts
    grads = {}

    w_main, w_ba = p['w_main'], p['w_ba']
    CB_ZA, CB_XB, CB_ZB, CB_QC, CB_ZC, CB_G = 3, 4, 5, 6, 7, 8

    u = _tile_fwd(_rms, "rms_fwd", nt, [_rt(x, ts)], [p['norm_g']],
                  [((s, d), BF16, (ts, d), lambda i: (i, 0))])[0]
    proj = _mm(u, w_main, name="proj_main")
    pba = _mm(u, w_ba, name="proj_ba")

    conv_w = p['conv_w']
    col = lambda arr, cb: (arr, (s, GDN_HEAD_DIM), lambda i, cb=cb: (0, cb + i))
    qkv = []
    for j, mode in enumerate(('q', 'k', 'v')):
        off = j * nh
        qkv.append(_tile_fwd(
            _gdn_pre(mode), "gdn_pre_" + mode, nh, [col(proj, off), (conv_w, (CONV_WIDTH, GDN_HEAD_DIM), lambda i, off=off: (0, off + i))],
            [], [((s, gw), F32, (s, GDN_HEAD_DIM), lambda i: (0, i))])[0])
    q, k, v = qkv
    lane = jnp.arange(128)[:, None]
    colh = jnp.arange(gw)[None, :] // GDN_HEAD_DIM
    e_beta = (lane == colh).astype(F32)
    e_g = (lane == colh + nh).astype(F32)
    alog_row = jnp.pad(p['gdn_a_log'], ((0, 0), (nh, 128 - 2 * nh)))
    dtb_row = jnp.pad(p['gdn_dt_bias'], ((0, 0), (nh, 128 - 2 * nh)))
    row_gw = lambda: ((s, gw), F32, (ts, gw), lambda i: (i, 0))
    betab, gb = _tile_fwd(_gdn_gates, "gdn_gates", nt, [_rt(pba, ts)], [alog_row, dtb_row, e_beta, e_g],
                          [row_gw(), row_gw()])
    intra = _gdn_intra_fwd(q, k, v, gb, betab)
    o_raw, states = _gdn_inter_fwd(*intra, gb)
    ga = _tile_fwd(_gdn_post, "gdn_post", nt, [_rt(o_raw, ts), _rt(proj, ts, CB_ZA, gw)], [p['gdn_norm_g']],
                   [((s, gw), BF16, (ts, gw), lambda i: (i, 0))])[0]

    e_rep = (jnp.arange(S5_STATE)[:, None] == jnp.arange(S5_STATE * S5_GROUP)[None, :] // S5_GROUP).astype(F32)
    s5_in = [p['s5_lambda_re'], p['s5_lambda_im'], p['s5_log_dt'].reshape(ng, 1),
             p['s5_b_re'].reshape(ng, S5_STATE * S5_GROUP), p['s5_b_im'].reshape(ng, S5_STATE * S5_GROUP), e_rep]
    one = lambda shp: (shp, F32, shp, lambda i, n=len(shp): (0,) * n)
    ab_re, ab_im, bbr, bbi = _tile_fwd(_s5_params, "s5_params", 1, [], s5_in,
                                       [one((ng, S5_STATE)), one((ng, S5_STATE)), one((ng, S5_STATE * S5_GROUP)),
                                        one((ng, S5_STATE * S5_GROUP))])
    coef = _s5_coef(ab_re.reshape(1, nl), ab_im.reshape(1, nl))
    to_bd_b = lambda t: _block_diag(t.reshape(nb, S5_GROUPS_PER_BLOCK, S5_STATE, S5_GROUP).transpose(0, 1, 3, 2))
    to_bd_c = lambda t: _block_diag(t.reshape(nb, S5_GROUPS_PER_BLOCK, S5_GROUP, S5_STATE).transpose(0, 1, 3, 2))
    bbd_re, bbd_im = to_bd_b(bbr).astype(BF16), to_bd_b(bbi).astype(BF16)
    cbd_re, cbd_im = to_bd_c(p['s5_c_re']).astype(BF16), to_bd_c(p['s5_c_im']).astype(BF16)
    xb_arr = lax.slice_in_dim(proj, CB_XB * gw, (CB_XB + 1) * gw, axis=1)
    h_re, h_im, ylin = _s5_fwd(xb_arr, bbd_re, bbd_im, cbd_re, cbd_im, coef, ts)
    gl = _tile_fwd(_s5_post1, "s5_post1", nt, [_rt(ylin, ts), _rt(proj, ts, CB_XB, gw)], [p['s5_d']],
                   [((s, gw), BF16, (ts, gw), lambda i: (i, 0))])[0]
    tglu = _mm(gl, p['s5_w_glu'], name="s5_glu")
    gbb = _tile_fwd(_s5_post2, "s5_post2", nt, [_rt(tglu, ts), _rt(proj, ts, CB_ZB, gw)], [],
                    [((s, gw), BF16, (ts, gw), lambda i: (i, 0))])[0]

    m_len = mem.shape[0]
    mem_n = _tile_fwd(_rms, "mem_rms", 1, [_rt(mem, m_len)], [p['mem_norm_g']],
                      [((m_len, d), BF16, (m_len, d), lambda i: (i, 0))])[0]
    kv = _mm(mem_n, p['w_kv_mem'], name="mem_kv")
    gcc = _tile_fwd(_attn, "attn", nt, [_rt(proj, ts, CB_QC, gw), _rt(proj, ts, CB_ZC, gw)], [kv],
                    [((s, gw), BF16, (ts, gw), lambda i: (i, 0))])[0]

    p_a = _mm(ga, p['w_br_a'], name="br_a")
    p_b = _mm(gbb, p['w_br_b'], name="br_b")
    p_c = _mm(gcc, p['w_br_c'], name="br_c")
    gate_acts = [_rt(proj, ts, CB_G // 2 + j, d) for j in range(3)]
    merged = _tile_fwd(_merge, "merge", nt, gate_acts + [_rt(p_a, ts), _rt(p_b, ts), _rt(p_c, ts)], [],
                       [((s, d), BF16, (ts, d), lambda i: (i, 0))])[0]
    mo = _mm(merged, p['w_out'], name="out_proj")
    dh, dfg, loss = _final(x, mo, target, p['final_g'].reshape(1, d), ts)
    grads['final_g'] = dfg.reshape(d)

    dmerged = _mm(dh, p['w_out'], tb=True, name="d_merged")
    grads['w_out'] = _mm(merged, dh, ta=True, name="dw_out")
    row_d = lambda dt: ((s, d), dt, (ts, d), lambda i: (i, 0))
    dg0, dg1, dg2, dpa, dpb, dpc = _tile_bwd(
        _merge, "merge_bwd", nt, gate_acts + [_rt(p_a, ts), _rt(p_b, ts), _rt(p_c, ts)], [], [_rt(dmerged, ts)],
        [row_d(BF16)] * 6, [])
    dga = _mm(dpa, p['w_br_a'], tb=True, name="d_ga")
    dgbb = _mm(dpb, p['w_br_b'], tb=True, name="d_gb")
    dgcc = _mm(dpc, p['w_br_c'], tb=True, name="d_gc")
    grads['w_br_a'] = _mm(ga, dpa, ta=True, name="dw_br_a")
    grads['w_br_b'] = _mm(gbb, dpb, ta=True, name="dw_br_b")
    grads['w_br_c'] = _mm(gcc, dpc, ta=True, name="dw_br_c")
    row_h = lambda dt: ((s, gw), dt, (ts, gw), lambda i: (i, 0))

    dqc, dzc, dkv = _tile_bwd(_attn, "attn_bwd", nt, [_rt(proj, ts, CB_QC, gw), _rt(proj, ts, CB_ZC, gw)], [kv],
                              [_rt(dgcc, ts)], [row_h(BF16), row_h(BF16)], [True])
    grads['w_kv_mem'] = _mm(mem_n, dkv, ta=True, name="dw_kv")
    dmem_n = _mm(dkv, p['w_kv_mem'], tb=True, name="d_mem_n")
    grads['mem_norm_g'] = _tile_bwd(_rms, "mem_rms_bwd", 1, [_rt(mem, m_len)], [p['mem_norm_g']],
                                    [_rt(dmem_n, m_len)], [None], [True])[0]

    dtglu, dzb = _tile_bwd(_s5_post2, "s5_post2_bwd", nt, [_rt(tglu, ts), _rt(proj, ts, CB_ZB, gw)], [],
                           [_rt(dgbb, ts)], [((s, 2 * gw), BF16, (ts, 2 * gw), lambda i: (i, 0)), row_h(BF16)], [])
    grads['s5_w_glu'] = _mm(gl, dtglu, ta=True, name="dw_glu")
    dgl = _mm(dtglu, p['s5_w_glu'], tb=True, name="d_gl")
    dylin, dxb1, dd = _tile_bwd(_s5_post1, "s5_post1_bwd", nt, [_rt(ylin, ts), _rt(proj, ts, CB_XB, gw)],
                                [p['s5_d']], [_rt(dgl, ts)], [row_h(F32), row_h(F32)], [True])
    grads['s5_d'] = dd
    dxb2, dbbd_re, dbbd_im, dcbd_re, dcbd_im, da_re, da_im = _s5_bwd(dylin, xb_arr, h_re, h_im, bbd_re, bbd_im,
                                                                    cbd_re, cbd_im, coef, ts)
    from_bd_b = lambda t: _diag_blocks(t, S5_GROUP, S5_STATE).transpose(0, 1, 3, 2).reshape(ng, S5_STATE * S5_GROUP)
    from_bd_c = lambda t: _diag_blocks(t, S5_STATE, S5_GROUP).transpose(0, 1, 3, 2).reshape(1, ng, S5_GROUP, S5_STATE)
    grads['s5_c_re'], grads['s5_c_im'] = from_bd_c(dcbd_re), from_bd_c(dcbd_im)
    s5_cts = [jnp.sum(da_re, axis=0).reshape(ng, S5_STATE), jnp.sum(da_im, axis=0).reshape(ng, S5_STATE),
              from_bd_b(dbbd_re), from_bd_b(dbbd_im)]
    dlr, dli, dlogdt, dbr, dbi = _tile_bwd(_s5_params, "s5_params_bwd", 1, [], s5_in,
                                           [(c, c.shape, lambda i: (0, 0)) for c in s5_cts], [],
                                           [True, True, True, True, True, False])
    grads['s5_lambda_re'], grads['s5_lambda_im'] = dlr[None], dli[None]
    grads['s5_log_dt'] = dlogdt.reshape(1, ng)
    grads['s5_b_re'] = dbr.reshape(1, ng, S5_STATE, S5_GROUP)
    grads['s5_b_im'] = dbi.reshape(1, ng, S5_STATE, S5_GROUP)
    dxb = (dxb1 + dxb2).astype(BF16)

    do_raw, dza, dgng = _tile_bwd(_gdn_post, "gdn_post_bwd", nt, [_rt(o_raw, ts), _rt(proj, ts, CB_ZA, gw)],
                                  [p['gdn_norm_g']], [_rt(dga, ts)], [row_h(F32), row_h(BF16)], [True])
    grads['gdn_norm_g'] = dgng
    *intra_cts, dgb_inter = _gdn_inter_bwd(*intra, gb, states, do_raw)
    dq, dk, dv, dgb, dbetab = _gdn_intra_bwd(q, k, v, gb, betab, intra_cts, dgb_inter)
    dpba, dalog, ddtb = _tile_bwd(_gdn_gates, "gdn_gates_bwd", nt, [_rt(pba, ts)], [alog_row, dtb_row, e_beta, e_g],
                                  [_rt(dbetab, ts), _rt(dgb, ts)], [((s, 128), BF16, (ts, 128), lambda i: (i, 0))],
                                  [True, True, False, False])
    grads['gdn_a_log'] = dalog[:, nh:2 * nh]
    grads['gdn_dt_bias'] = ddtb[:, nh:2 * nh]
    dqkv, dconv = [], []
    for j, (mode, ct) in enumerate((('q', dq), ('k', dk), ('v', dv))):
        off = j * nh
        wspec = (conv_w, (CONV_WIDTH, GDN_HEAD_DIM), lambda i, off=off: (0, off + i))
        dxc, dwc = _tile_bwd(
            _gdn_pre(mode), "gdn_pre_bwd_" + mode, nh, [col(proj, off), wspec], [], [col(ct, 0)],
            [((s, gw), BF16, (s, GDN_HEAD_DIM), lambda i: (0, i)),
             ((CONV_WIDTH, gw), F32, (CONV_WIDTH, GDN_HEAD_DIM), lambda i: (0, i))], [])
        dqkv.append(dxc)
        dconv.append(dwc)
    grads['conv_w'] = jnp.concatenate(dconv, axis=1)

    dproj = jnp.concatenate(dqkv + [dza, dxb, dzb, dqc, dzc, dg0, dg1, dg2], axis=1)
    du = _mm(dpba, w_ba, tb=True, name="du_ba")
    du = _mm(dproj, w_main, tb=True, addend=du, name="du_main")
    grads['w_main'] = _mm(u, dproj, ta=True, name="dw_main")
    grads['w_ba'] = _mm(u, dpba, ta=True, name="dw_ba")
    dx, dng = _tile_bwd(_rms, "rms_bwd", nt, [_rt(x, ts)], [p['norm_g']], [_rt(du, ts)], [row_d(F32)], [True])
    grads['norm_g'] = dng
    return loss, dx + dh, grads


def _to_shards(name, g):
    if SHARDED[name] == 'row':
        return g.reshape((N_DEV, g.shape[0] // N_DEV) + g.shape[1:])
    r, c = g.shape
    return g.reshape(r, N_DEV, c // N_DEV).transpose(1, 0, 2)


def _from_shards(name, t):
    if SHARDED[name] == 'row':
        return t.reshape((t.shape[0] * t.shape[1],) + t.shape[2:])
    n, r, c = t.shape
    return t.transpose(1, 0, 2).reshape(r, n * c)


def _step(x, mem, target, w, m, v):
    sharded = list(SHARDED)
    shard_shapes = {n: tuple(w[n].shape[1:]) for n in sharded}
    d = x.shape[-1]
    ba_lo = 2 * d
    ba_hi = ba_lo + 2 * (d // 2 // GDN_HEAD_DIM)

    srcs = [w[n][0].astype(BF16) for n in GATHER_BF16] + [w['conv_w'][0]]
    got = _exchange(srcs, [_send_all] * len(srcs), [s.shape for s in srcs], "gather_weights")
    full = {n: t for n, t in zip(GATHER_BF16 + ['conv_w'], got)}
    full['w_main'], full['w_ba'] = _w_in_from_shards(full.pop('w_in'), ba_lo, ba_hi)
    for n in sharded:
        if n != 'w_in':
            full[n] = _from_shards(n, full[n])
    for n in REPLICATED:
        full[n] = w[n]
    for n in ('s5_lambda_re', 's5_lambda_im', 's5_c_re', 's5_c_im'):
        full[n] = w[n][0]

    loss, grad_x, grads = _local_step(x[0], mem[0], target[0], full)

    grads['w_in'] = _w_in_to_shards(grads.pop('w_main'), grads.pop('w_ba'), ba_lo, ba_hi)
    picks = [_send_slot if n == 'w_in' else
             (_send_rows(shard_shapes[n][0]) if SHARDED[n] == 'row' else _send_cols(shard_shapes[n][1]))
             for n in sharded]
    parts = _exchange([grads[n] for n in sharded], picks, [shard_shapes[n] for n in sharded], "scatter_grads")
    res = {}
    for n, part in zip(sharded, parts):
        outs = _sum_adam(part, w[n][0], m[n][0], v[n][0], name="adam_" + n)
        for kind, t in zip(('grad', 'delta', 'new_m', 'new_v'), outs):
            res[kind, n] = t[None]

    small = _pack([grads[n].reshape(w[n].shape) for n in REPLICATED] + [loss[:1, :1]], F32)
    allp = _exchange([small], [_send_all], [small.shape], "gather_small")[0]
    zero = jnp.zeros((1, 1), F32)
    outs = _sum_adam(allp, *[_pack([t[n] for n in REPLICATED] + [zero], F32) for t in (w, m, v)], name="adam_small")
    shapes = [w[n].shape for n in REPLICATED] + [(1, 1)]
    for kind, buf in zip(('grad', 'delta', 'new_m', 'new_v'), outs):
        got = _unpack(buf, shapes)
        for n, t in zip(REPLICATED, got):
            res[kind, n] = t
        if kind == 'grad':
            total_loss = got[-1].reshape(())
    out = [total_loss, grad_x[None]]
    for kind in ('grad', 'delta', 'new_m', 'new_v'):
        out += [res[kind, n] for n in WEIGHTS]
    return tuple(out)


def kernel(x, mem, norm_g, w_in, conv_w, gdn_a_log, gdn_dt_bias, gdn_norm_g, s5_lambda_re, s5_lambda_im, s5_log_dt, s5_b_re, s5_b_im, s5_c_re, s5_c_im, s5_d, s5_w_glu, mem_norm_g, w_kv_mem, w_br_a, w_br_b, w_br_c, w_out, final_g, loss_target, m_norm_g, m_w_in, m_conv_w, m_gdn_a_log, m_gdn_dt_bias, m_gdn_norm_g, m_s5_lambda_re, m_s5_lambda_im, m_s5_log_dt, m_s5_b_re, m_s5_b_im, m_s5_c_re, m_s5_c_im, m_s5_d, m_s5_w_glu, m_mem_norm_g, m_w_kv_mem, m_w_br_a, m_w_br_b, m_w_br_c, m_w_out, m_final_g, v_norm_g, v_w_in, v_conv_w, v_gdn_a_log, v_gdn_dt_bias, v_gdn_norm_g, v_s5_lambda_re, v_s5_lambda_im, v_s5_log_dt, v_s5_b_re, v_s5_b_im, v_s5_c_re, v_s5_c_im, v_s5_d, v_s5_w_glu, v_mem_norm_g, v_w_kv_mem, v_w_br_a, v_w_br_b, v_w_br_c, v_w_out, v_final_g):
    a = dict(locals())
    w = {n: a[n] for n in WEIGHTS}
    m = {n: a['m_' + n] for n in WEIGHTS}
    v = {n: a['v_' + n] for n in WEIGHTS}
    return _step(x, mem, loss_target, w, m, v)
```

```python
import functools
import math

import jax
import jax.numpy as jnp
from jax import lax
from jax.experimental import pallas as pl
from jax.experimental.pallas import tpu as pltpu

F32 = jnp.float32
BF16 = jnp.bfloat16
HI = lax.Precision.HIGHEST

EPS = 1e-6
CHUNK = 64
GDN_HEAD_DIM = 128
CONV_WIDTH = 4
S5_GROUP = 16
S5_STATE = 64
S5_GROUPS_PER_BLOCK = 8
XA_HEADS = 4
N_DEV = 8
ADAM_LR, ADAM_B1, ADAM_B2, ADAM_EPS, ADAM_WD, ADAM_STEP = 0.001, 0.9, 0.999, 1e-08, 0.01, 10

VMEM_LIMIT_BYTES = 56 * 1024 * 1024
SCAN_LANES = 512
PACK_WIDTH = 512
PACK_ROWS = 256

WEIGHTS = ['norm_g', 'w_in', 'conv_w', 'gdn_a_log', 'gdn_dt_bias', 'gdn_norm_g', 's5_lambda_re', 's5_lambda_im',
           's5_log_dt', 's5_b_re', 's5_b_im', 's5_c_re', 's5_c_im', 's5_d', 's5_w_glu', 'mem_norm_g', 'w_kv_mem',
           'w_br_a', 'w_br_b', 'w_br_c', 'w_out', 'final_g']
SHARDED = {'w_in': 'col', 'conv_w': 'col', 's5_w_glu': 'col', 'w_kv_mem': 'row', 'w_br_a': 'col', 'w_br_b': 'col',
           'w_br_c': 'col', 'w_out': 'row'}
GATHER_BF16 = ['w_in', 's5_w_glu', 'w_kv_mem', 'w_br_a', 'w_br_b', 'w_br_c', 'w_out']
REPLICATED = [n for n in WEIGHTS if n not in SHARDED]


def _cparams(sem=None):
    return pltpu.CompilerParams(dimension_semantics=sem, vmem_limit_bytes=VMEM_LIMIT_BYTES)


def _pick(dim, pref):
    t = (min(pref, dim) // 128) * 128
    while t >= 128:
        if dim % t == 0:
            return t
        t -= 128
    return dim


def _make_dots(prep, precision):
    def raw(a, b, dims):
        return lax.dot_general(prep(a), prep(b), (dims, ((), ())), preferred_element_type=F32, precision=precision)

    @jax.custom_vjp
    def nn(a, b):
        return raw(a, b, ((1,), (0,)))

    @jax.custom_vjp
    def nt(a, b):
        return raw(a, b, ((1,), (1,)))

    @jax.custom_vjp
    def tn(a, b):
        return raw(a, b, ((0,), (0,)))

    nn.defvjp(lambda a, b: (nn(a, b), (a, b)), lambda r, ct: (nt(ct, r[1]), tn(r[0], ct)))
    nt.defvjp(lambda a, b: (nt(a, b), (a, b)), lambda r, ct: (nn(ct, r[1]), tn(ct, r[0])))
    tn.defvjp(lambda a, b: (tn(a, b), (a, b)), lambda r, ct: (nt(r[1], ct), nn(r[0], ct)))
    return nn, nt, tn


_bnn, _bnt, _btn = _make_dots(lambda a: a.astype(BF16), None)
_hnn, _hnt, _htn = _make_dots(lambda a: a.astype(F32), HI)
_mnn, _mnt, _mtn = _make_dots(lambda a: a.astype(F32), lax.Precision.HIGH)


def _mm(a, b, *, name, ta=False, tb=False, out_dtype=F32, addend=None, tm=512, tn=1024, tk=1024, b_shards=False,
        out_shards=0):
    m, k = (a.shape[1], a.shape[0]) if ta else a.shape
    brows, bcols = (b.shape[1], b.shape[0] * b.shape[2]) if b_shards else b.shape
    n = brows if tb else bcols
    assert (bcols if tb else brows) == k, (a.shape, b.shape, ta, tb)
    tm, tn, tk = _pick(m, tm), _pick(n, tn), _pick(k, tk)
    if b_shards:
        cs = b.shape[2]
        assert cs % 128 == 0
        if tb:
            tk = cs
        else:
            tn = cs
    if out_shards:
        tn = n // out_shards
        assert tn % 128 == 0 and not (b_shards and not tb)
    nk = k // tk
    dims = ((0 if ta else 1,), (1 if tb else 0,))

    def body(*refs):
        if addend is None:
            a_ref, b_ref, o_ref, acc_ref = refs
        else:
            a_ref, b_ref, add_ref, o_ref, acc_ref = refs
        kk = pl.program_id(2)

        @pl.when(kk == 0)
        def _():
            acc_ref[...] = jnp.zeros_like(acc_ref)

        acc_ref[...] += lax.dot_general(a_ref[...].astype(BF16), b_ref[...].astype(BF16), (dims, ((), ())),
                                        preferred_element_type=F32)

        @pl.when(kk == nk - 1)
        def _():
            r = acc_ref[...]
            if addend is not None:
                r = r + add_ref[...].astype(F32)
            o_ref[...] = r.astype(o_ref.dtype)

    a_spec = pl.BlockSpec((tk, tm), lambda i, j, kk: (kk, i)) if ta else pl.BlockSpec((tm, tk), lambda i, j, kk: (i, kk))
    if b_shards:
        b_spec = (pl.BlockSpec((None, tn, tk), lambda i, j, kk: (kk, j, 0)) if tb
                  else pl.BlockSpec((None, tk, tn), lambda i, j, kk: (j, kk, 0)))
    else:
        b_spec = (pl.BlockSpec((tn, tk), lambda i, j, kk: (j, kk)) if tb
                  else pl.BlockSpec((tk, tn), lambda i, j, kk: (kk, j)))
    if out_shards:
        o_spec = pl.BlockSpec((None, tm, tn), lambda i, j, kk: (j, i, 0))
        out_shape = jax.ShapeDtypeStruct((out_shards, m, tn), out_dtype)
    else:
        o_spec = pl.BlockSpec((tm, tn), lambda i, j, kk: (i, j))
        out_shape = jax.ShapeDtypeStruct((m, n), out_dtype)
    in_specs = [a_spec, b_spec] + ([o_spec] if addend is not None else [])
    args = (a, b) + ((addend,) if addend is not None else ())
    return pl.pallas_call(
        body, name=name, grid=(m // tm, n // tn, nk), in_specs=in_specs, out_specs=o_spec,
        out_shape=out_shape, scratch_shapes=[pltpu.VMEM((tm, tn), F32)],
        compiler_params=_cparams(("parallel", "parallel", "arbitrary")))(*args)


def _rt(arr, ts, cb=0, w=None):
    w = arr.shape[1] if w is None else w
    return (arr, (ts, w), lambda i, cb=cb: (i, cb))


def _whole(p):
    return pl.BlockSpec(p.shape, lambda i, nd=p.ndim: (0,) * nd)


def _tile_fwd(f, name, n, acts, params, outs):
    na, npar = len(acts), len(params)

    def body(*refs):
        res = f(*[r[...] for r in refs[:na + npar]])
        for r, v in zip(refs[na + npar:], res):
            r[...] = v.astype(r.dtype)

    in_specs = [pl.BlockSpec(b, m) for _, b, m in acts] + [_whole(p) for p in params]
    out = pl.pallas_call(
        body, name=name, grid=(n,), in_specs=in_specs,
        out_specs=[pl.BlockSpec(b, m) for _, _, b, m in outs],
        out_shape=[jax.ShapeDtypeStruct(s, d) for s, d, _, _ in outs],
        compiler_params=_cparams(("parallel",)))(*[a for a, _, _ in acts], *params)
    return out


def _tile_bwd(f, name, n, acts, params, cts, agrads, pgrads):
    na, npar, nc = len(acts), len(params), len(cts)

    def body(*refs):
        i = pl.program_id(0)
        ins = [r[...] for r in refs[:na + npar]]
        outs, vjp = jax.vjp(f, *ins)
        g = vjp(tuple(c[...].astype(o.dtype) for c, o in zip(refs[na + npar:na + npar + nc], outs)))
        orefs = refs[na + npar + nc:]
        k = 0
        for j in range(na):
            if agrads[j] is not None:
                orefs[k][...] = g[j].astype(orefs[k].dtype)
                k += 1
        for j in range(npar):
            if pgrads[j]:
                o = orefs[k]

                @pl.when(i == 0)
                def _(o=o):
                    o[...] = jnp.zeros_like(o)

                o[...] += g[na + j].astype(F32)
                k += 1

    in_specs = ([pl.BlockSpec(b, m) for _, b, m in acts] + [_whole(p) for p in params]
                + [pl.BlockSpec(b, m) for _, b, m in cts])
    out_specs = [pl.BlockSpec(g[2], g[3]) for g in agrads if g is not None]
    out_shape = [jax.ShapeDtypeStruct(g[0], g[1]) for g in agrads if g is not None]
    for p, flag in zip(params, pgrads):
        if flag:
            out_specs.append(_whole(p))
            out_shape.append(jax.ShapeDtypeStruct(p.shape, F32))
    return pl.pallas_call(
        body, name=name, grid=(n,), in_specs=in_specs, out_specs=out_specs, out_shape=out_shape,
        compiler_params=_cparams(("arbitrary",)))(*[a for a, _, _ in acts], *params, *[c for c, _, _ in cts])


def _silu(x):
    return x * jax.nn.sigmoid(x)


def _rms(x, g):
    x = x.astype(F32)
    return (x * lax.rsqrt(jnp.mean(x * x, axis=-1, keepdims=True) + EPS) * g,)


def _shift_down(x, s):
    row = lax.broadcasted_iota(jnp.int32, x.shape, 0)
    return jnp.where(row >= s, pltpu.roll(x, s, 0), 0.0)


def _shift_up(x, s):
    n = x.shape[0]
    row = lax.broadcasted_iota(jnp.int32, x.shape, 0)
    return jnp.where(row < n - s, pltpu.roll(x, n - s, 0), 0.0)


@functools.partial(jax.custom_vjp, nondiff_argnums=(1,))
def _shift(x, s):
    return _shift_down(x, s)


_shift.defvjp(lambda x, s: (_shift_down(x, s), None), lambda s, _, ct: (_shift_up(ct, s),))


def _gdn_pre(mode):
    def f(x, w):
        y = x * w[CONV_WIDTH - 1:CONV_WIDTH, :]
        for j in range(CONV_WIDTH - 1):
            y = y + _shift(x, CONV_WIDTH - 1 - j) * w[j:j + 1, :]
        y = _silu(y)
        if mode != 'v':
            y = y * lax.rsqrt(jnp.sum(y * y, axis=-1, keepdims=True) + EPS)
        if mode == 'q':
            y = y * (GDN_HEAD_DIM ** -0.5)
        return (y,)
    return f


def _softplus(x):
    return jnp.maximum(x, 0.0) + jnp.log1p(jnp.exp(-jnp.abs(x)))


def _gdn_gates(ba, alog, dtb, e_beta, e_g):
    beta = jax.nn.sigmoid(ba)
    g = -jnp.exp(alog) * _softplus(ba + dtb)
    return _hnn(beta, lax.stop_gradient(e_beta)), _hnn(g, lax.stop_gradient(e_g))


def _gdn_intra(q, k, v, gb, bb):
    n, c = len(q), q[0].shape[0]
    ri = lax.broadcasted_iota(jnp.int32, (c, c), 0)
    ci = lax.broadcasted_iota(jnp.int32, (c, c), 1)
    incl, strict = ri >= ci, ri > ci
    tri = incl.astype(F32)
    eye = (ri == ci).astype(F32)
    each = range(n)
    gc = [_hnn(tri, gb[i]) for i in each]
    decay = [jnp.exp(jnp.where(incl, gc[i][:, :c] - gc[i].T[:c, :], -1e30)) for i in each]
    kb = [k[i] * bb[i] for i in each]
    kk = [_bnt(kb[i], k[i]) for i in each]
    qk = [_bnt(q[i], k[i]) for i in each]
    p = [jnp.where(strict, -(kk[i] * decay[i]), 0.0) for i in each]
    t = [eye + p[i] for i in each]
    for _ in range(int(math.log2(c)) - 1):
        p = [_mnn(p[i], p[i]) for i in each]
        tp = [_mnn(t[i], p[i]) for i in each]
        t = [t[i] + tp[i] for i in each]
    egc = [jnp.exp(gc[i]) for i in each]
    u_val = [_mnn(t[i], v[i] * bb[i]) for i in each]
    w_dec = [_mnn(t[i], kb[i] * egc[i]) for i in each]
    qk = [qk[i] * decay[i] for i in each]
    gl = [jnp.sum(gb[i], axis=0, keepdims=True) for i in each]
    return w_dec, u_val, qk, [q[i] * egc[i] for i in each], [k[i] * jnp.exp(gl[i] - gc[i]) for i in each]


def _gdn_inter(w_dec, u_val, qk, q_dec, k_dec, gb, state):
    each = range(len(state))
    ws = [_bnn(w_dec[i], state[i]) for i in each]
    qs = [_bnn(q_dec[i], state[i]) for i in each]
    v_new = [u_val[i] - ws[i] for i in each]
    qv = [_bnn(qk[i], v_new[i]) for i in each]
    kv = [_btn(k_dec[i], v_new[i]) for i in each]
    decayed = [state[i] * jnp.exp(jnp.sum(gb[i], axis=0, keepdims=True)) for i in each]
    return [qs[i] + qv[i] for i in each], [decayed[i] + kv[i] for i in each]


def _gdn_post(o, z, g):
    parts = []
    for h in range(o.shape[1] // GDN_HEAD_DIM):
        oh = o[:, h * GDN_HEAD_DIM:(h + 1) * GDN_HEAD_DIM]
        parts.append(oh * lax.rsqrt(jnp.mean(oh * oh, axis=-1, keepdims=True) + EPS) * g)
    y = parts[0] if len(parts) == 1 else jnp.concatenate(parts, axis=1)
    return (y * _silu(z),)


def _gelu(x):
    return 0.5 * x * (1.0 + jnp.tanh(0.7978845608028654 * (x + 0.044715 * x * x * x)))


def _s5_post1(ylin, xb, d):
    return (_gelu(ylin + d * xb),)


def _s5_post2(t, z):
    w = t.shape[1] // 2
    return (t[:, :w] * jax.nn.sigmoid(t[:, w:]) * _silu(z),)


def _attn(q, z, kv):
    w = q.shape[1]
    hd = w // XA_HEADS
    parts = []
    for h in range(XA_HEADS):
        s = _bnt(q[:, h * hd:(h + 1) * hd], kv[:, h * hd:(h + 1) * hd]) * (hd ** -0.5)
        s = s - jnp.max(s, axis=-1, keepdims=True)
        e = jnp.exp(s)
        p = e / jnp.sum(e, axis=-1, keepdims=True)
        parts.append(_bnn(p, kv[:, w + h * hd:w + (h + 1) * hd]))
    return (jnp.concatenate(parts, axis=1) * _silu(z),)


def _merge(g0, g1, g2, pa, pb, pc):
    return (jax.nn.sigmoid(g0) * pa + jax.nn.sigmoid(g1) * pb + jax.nn.sigmoid(g2) * pc,)


def _s5_params(lr, li, logdt, br, bi, e):
    dt = jnp.exp(logdt)
    mag = jnp.exp(lr * dt)
    ab_re, ab_im = mag * jnp.cos(li * dt), mag * jnp.sin(li * dt)
    den = lr * lr + li * li
    nr, ni = ab_re - 1.0, ab_im
    e = lax.stop_gradient(e)
    cre = _hnn((nr * lr + ni * li) / den, e)
    cim = _hnn((ni * lr - nr * li) / den, e)
    return ab_re, ab_im, cre * br - cim * bi, cre * bi + cim * br


def _gdn_blocks(s, w, per_step):
    nh, nc = w // GDN_HEAD_DIM, s // CHUNK
    cpb = math.gcd(per_step, nc)
    return nh, nc, cpb, nc // cpb, (cpb * CHUNK, w), (cpb * CHUNK, nh * CHUNK)


def _gdn_pairs(cpb, nh):
    wide, narrow = [], []
    for cb in range(cpb):
        rows = slice(cb * CHUNK, (cb + 1) * CHUNK)
        for h in range(nh):
            wide.append((rows, slice(h * GDN_HEAD_DIM, (h + 1) * GDN_HEAD_DIM)))
            narrow.append((rows, slice(h * CHUNK, (h + 1) * CHUNK)))
    return wide, narrow


def _gdn_intra_fwd(q, k, v, gb, bb, per_step=4):
    s, w = q.shape
    nh, nc, cpb, n, wide, narrow = _gdn_blocks(s, w, per_step)
    hd = GDN_HEAD_DIM

    def body(q_ref, k_ref, v_ref, g_ref, b_ref, wd_ref, uv_ref, qk_ref, qd_ref, kd_ref):
        wide, narrow = _gdn_pairs(cpb, nh)
        res = _gdn_intra(*[[r[ix] for ix in wide] for r in (q_ref, k_ref, v_ref, g_ref, b_ref)])
        for ref, vals, where in zip((wd_ref, uv_ref, qk_ref, qd_ref, kd_ref), res, (wide, wide, narrow, wide, wide)):
            for ix, val in zip(where, vals):
                ref[ix] = val

    bw = pl.BlockSpec(wide, lambda i: (i, 0))
    bn = pl.BlockSpec(narrow, lambda i: (i, 0))
    fw = jax.ShapeDtypeStruct((s, w), F32)
    return pl.pallas_call(
        body, name="gdn_intra", grid=(n,), in_specs=[bw] * 5, out_specs=[bw, bw, bn, bw, bw],
        out_shape=[fw, fw, jax.ShapeDtypeStruct((s, nh * CHUNK), F32), fw, fw],
        compiler_params=_cparams(("parallel",)))(q, k, v, gb, bb)


def _gdn_intra_bwd(q, k, v, gb, bb, cts, dgb_inter, per_step=2):
    s, w = q.shape
    nh, nc, cpb, n, wide, narrow = _gdn_blocks(s, w, per_step)
    hd = GDN_HEAD_DIM

    def body(q_ref, k_ref, v_ref, g_ref, b_ref, cwd, cuv, cqk, cqd, ckd, dgi, dq_ref, dk_ref, dv_ref, dg_ref, db_ref):
        wide, narrow = _gdn_pairs(cpb, nh)
        _, vjp = jax.vjp(_gdn_intra, *[[r[ix] for ix in wide] for r in (q_ref, k_ref, v_ref, g_ref, b_ref)])
        cts = tuple([r[ix] for ix in where] for r, where in zip((cwd, cuv, cqk, cqd, ckd),
                                                               (wide, wide, narrow, wide, wide)))
        dq, dk, dv, dg, db = vjp(cts)
        for j, ix in enumerate(wide):
            dq_ref[ix], dk_ref[ix], dv_ref[ix], db_ref[ix] = dq[j], dk[j], dv[j], db[j]
            dg_ref[ix] = dg[j] + dgi[ix]

    bw = pl.BlockSpec(wide, lambda i: (i, 0))
    bn = pl.BlockSpec(narrow, lambda i: (i, 0))
    return pl.pallas_call(
        body, name="gdn_intra_bwd", grid=(n,), in_specs=[bw] * 5 + [bw, bw, bn, bw, bw, bw], out_specs=[bw] * 5,
        out_shape=[jax.ShapeDtypeStruct((s, w), F32)] * 5,
        compiler_params=_cparams(("parallel",)))(q, k, v, gb, bb, *cts, dgb_inter)


def _gdn_inter_fwd(wd, uv, qk, qd, kd, gb, per_step=4):
    s, w = wd.shape
    nh, nc, cpb, n, wide, narrow = _gdn_blocks(s, w, per_step)
    hd = GDN_HEAD_DIM

    def body(wd_ref, uv_ref, qk_ref, qd_ref, kd_ref, g_ref, o_ref, st_ref, state):
        @pl.when(pl.program_id(0) == 0)
        def _():
            state[...] = jnp.zeros_like(state)

        wide, narrow = _gdn_pairs(cpb, nh)
        st = [state[h] for h in range(nh)]
        for cb in range(cpb):
            wi, na = wide[cb * nh:(cb + 1) * nh], narrow[cb * nh:(cb + 1) * nh]
            for h in range(nh):
                st_ref[cb, h] = st[h]
            o, st = _gdn_inter([wd_ref[ix] for ix in wi], [uv_ref[ix] for ix in wi], [qk_ref[ix] for ix in na],
                               [qd_ref[ix] for ix in wi], [kd_ref[ix] for ix in wi], [g_ref[ix] for ix in wi], st)
            for h in range(nh):
                o_ref[wi[h]] = o[h]
        for h in range(nh):
            state[h] = st[h]

    bw = pl.BlockSpec(wide, lambda i: (i, 0))
    bn = pl.BlockSpec(narrow, lambda i: (i, 0))
    return pl.pallas_call(
        body, name="gdn_inter", grid=(n,), in_specs=[bw, bw, bn, bw, bw, bw],
        out_specs=[bw, pl.BlockSpec((cpb, nh, hd, hd), lambda i: (i, 0, 0, 0))],
        out_shape=[jax.ShapeDtypeStruct((s, w), F32), jax.ShapeDtypeStruct((nc, nh, hd, hd), F32)],
        scratch_shapes=[pltpu.VMEM((nh, hd, hd), F32)],
        compiler_params=_cparams(("arbitrary",)))(wd, uv, qk, qd, kd, gb)


def _gdn_inter_bwd(wd, uv, qk, qd, kd, gb, states, do, per_step=4):
    s, w = wd.shape
    nh, nc, cpb, n, wide, narrow = _gdn_blocks(s, w, per_step)
    hd = GDN_HEAD_DIM

    def body(wd_ref, uv_ref, qk_ref, qd_ref, kd_ref, g_ref, st_ref, do_ref, cwd, cuv, cqk, cqd, ckd, dg_ref, dstate):
        @pl.when(pl.program_id(0) == 0)
        def _():
            dstate[...] = jnp.zeros_like(dstate)

        wide, narrow = _gdn_pairs(cpb, nh)
        dst = [dstate[h] for h in range(nh)]
        for cb in reversed(range(cpb)):
            wi, na = wide[cb * nh:(cb + 1) * nh], narrow[cb * nh:(cb + 1) * nh]
            _, vjp = jax.vjp(_gdn_inter, [wd_ref[ix] for ix in wi], [uv_ref[ix] for ix in wi],
                             [qk_ref[ix] for ix in na], [qd_ref[ix] for ix in wi], [kd_ref[ix] for ix in wi],
                             [g_ref[ix] for ix in wi], [st_ref[cb, h] for h in range(nh)])
            dwd, duv, dqk, dqd, dkd, dg, dst = vjp(([do_ref[ix] for ix in wi], dst))
            for h in range(nh):
                cwd[wi[h]], cuv[wi[h]], cqk[na[h]], cqd[wi[h]], ckd[wi[h]], dg_ref[wi[h]] = (
                    dwd[h], duv[h], dqk[h], dqd[h], dkd[h], dg[h])
        for h in range(nh):
            dstate[h] = dst[h]

    bw = pl.BlockSpec(wide, lambda i: (n - 1 - i, 0))
    bn = pl.BlockSpec(narrow, lambda i: (n - 1 - i, 0))
    fw = jax.ShapeDtypeStruct((s, w), F32)
    return pl.pallas_call(
        body, name="gdn_inter_bwd", grid=(n,),
        in_specs=[bw, bw, bn, bw, bw, bw, pl.BlockSpec((cpb, nh, hd, hd), lambda i: (n - 1 - i, 0, 0, 0)), bw],
        out_specs=[bw, bw, bn, bw, bw, bw],
        out_shape=[fw, fw, jax.ShapeDtypeStruct((s, nh * CHUNK), F32), fw, fw, fw],
        scratch_shapes=[pltpu.VMEM((nh, hd, hd), F32)],
        compiler_params=_cparams(("arbitrary",)))(wd, uv, qk, qd, kd, gb, states, do)


def _s5_coef(ar, ai):
    nl = ar.shape[1]

    def body(ar_ref, ai_ref, o_ref):
        row = lax.broadcasted_iota(jnp.int32, (8, nl), 0)
        for base, sign in ((0, 1.0), (8, -1.0)):
            pr = [jnp.broadcast_to(ar_ref[...], (8, nl))]
            pi = [jnp.broadcast_to(ai_ref[...], (8, nl)) * sign]
            for _ in range(7):
                pr.append(pr[-1] * pr[0] - pi[-1] * pi[0])
                pi.append(pr[-2] * pi[0] + pi[-1] * pr[0])
            for j, d in enumerate((1, 2, 4)):
                m = (row >= d) if base == 0 else (row <= 7 - d)
                o_ref[base + 2 * j] = jnp.where(m, pr[d - 1], 0.0)
                o_ref[base + 2 * j + 1] = jnp.where(m, pi[d - 1], 0.0)
            cr, ci = jnp.zeros((8, nl), F32), jnp.zeros((8, nl), F32)
            for t in range(8):
                e = t if base == 0 else 7 - t
                cr = jnp.where(row == t, pr[e], cr)
                ci = jnp.where(row == t, pi[e], ci)
            o_ref[base + 6] = cr
            o_ref[base + 7] = ci

    return pl.pallas_call(body, name="s5_coef", out_shape=jax.ShapeDtypeStruct((16, 8, nl), F32),
                          compiler_params=_cparams())(ar, ai)


def _scan_tile(src_re, src_im, dst_re, dst_im, coef_ref, carry_re, carry_im, ts, reverse, extra=None):
    nl = src_re.shape[1]
    base = 8 if reverse else 0
    ng = ts // 8
    for lc in range(nl // SCAN_LANES):
        ln = slice(lc * SCAN_LANES, (lc + 1) * SCAN_LANES)
        m = [coef_ref[base + j, :, ln] for j in range(8)]
        row = lax.broadcasted_iota(jnp.int32, (8, SCAN_LANES), 0)

        def step(r, carry, ln=ln, m=m, row=row):
            grp = (ng - 1 - r) if reverse else r
            rows = pl.ds(pl.multiple_of(grp * 8, 8), 8)
            xr, xi = src_re[rows, ln], src_im[rows, ln]
            for j, d in enumerate((1, 2, 4)):
                sh = 8 - d if reverse else d
                sr, si = pltpu.roll(xr, sh, 0), pltpu.roll(xi, sh, 0)
                mr, mi = m[2 * j], m[2 * j + 1]
                xr, xi = xr + mr * sr - mi * si, xi + mr * si + mi * sr
            cr, ci = carry[0], carry[1]
            hr = xr + m[6] * cr - m[7] * ci
            hi = xi + m[6] * ci + m[7] * cr
            dst_re[rows, ln] = hr
            dst_im[rows, ln] = hi
            edge = 0 if reverse else 7
            out = (jnp.broadcast_to(hr[edge:edge + 1, :], hr.shape), jnp.broadcast_to(hi[edge:edge + 1, :], hi.shape))
            if extra is not None:
                h_re, h_im, halo_re, halo_im, first, _, _ = extra
                prev = pl.ds(pl.multiple_of(jnp.maximum(grp - 1, 0) * 8, 8), 8)
                use_halo = grp == 0
                pr = jnp.where(use_halo, halo_re[:, ln] * first, h_re[prev, ln])
                pi = jnp.where(use_halo, halo_im[:, ln] * first, h_im[prev, ln])
                qr = jnp.where(row == 0, jnp.broadcast_to(pr[7:8, :], pr.shape), pltpu.roll(h_re[rows, ln], 1, 0))
                qi = jnp.where(row == 0, jnp.broadcast_to(pi[7:8, :], pi.shape), pltpu.roll(h_im[rows, ln], 1, 0))
                out = out + (carry[2] + hr * qr + hi * qi, carry[3] + hi * qr - hr * qi)
            return out

        init = (carry_re[:, ln], carry_im[:, ln])
        if extra is not None:
            init = init + (extra[5][:, ln], extra[6][:, ln])
        fin = lax.fori_loop(0, ng, step, init)
        carry_re[:, ln] = fin[0]
        carry_im[:, ln] = fin[1]
        if extra is not None:
            extra[5][:, ln] = fin[2]
            extra[6][:, ln] = fin[3]


def _s5_fwd(xb, bb_re, bb_im, c_re, c_im, coef, ts):
    s, w = xb.shape
    nb = bb_re.shape[0]
    nl = nb * 512

    def body(x_ref, bre_ref, bim_ref, cre_ref, cim_ref, coef_ref, hre_ref, him_ref, y_ref, ure, uim, car_re, car_im):
        @pl.when(pl.program_id(0) == 0)
        def _():
            car_re[...] = jnp.zeros_like(car_re)
            car_im[...] = jnp.zeros_like(car_im)

        for b in range(nb):
            xs = x_ref[:, b * 128:(b + 1) * 128].astype(BF16)
            ure[:, b * 512:(b + 1) * 512] = jnp.dot(xs, bre_ref[b], preferred_element_type=F32)
            uim[:, b * 512:(b + 1) * 512] = jnp.dot(xs, bim_ref[b], preferred_element_type=F32)
        _scan_tile(ure, uim, hre_ref, him_ref, coef_ref, car_re, car_im, ts, False)
        for b in range(nb):
            hr = hre_ref[:, b * 512:(b + 1) * 512].astype(BF16)
            hi = him_ref[:, b * 512:(b + 1) * 512].astype(BF16)
            y_ref[:, b * 128:(b + 1) * 128] = (jnp.dot(hr, cre_ref[b], preferred_element_type=F32)
                                               - jnp.dot(hi, cim_ref[b], preferred_element_type=F32))

    row = lambda wd: pl.BlockSpec((ts, wd), lambda i: (i, 0))
    return pl.pallas_call(
        body, name="s5_fwd", grid=(s // ts,),
        in_specs=[row(w), _whole(bb_re), _whole(bb_im), _whole(c_re), _whole(c_im), _whole(coef)],
        out_specs=[row(nl), row(nl), row(w)],
        out_shape=[jax.ShapeDtypeStruct((s, nl), F32), jax.ShapeDtypeStruct((s, nl), F32),
                   jax.ShapeDtypeStruct((s, w), F32)],
        scratch_shapes=[pltpu.VMEM((ts, nl), F32), pltpu.VMEM((ts, nl), F32), pltpu.VMEM((8, nl), F32),
                        pltpu.VMEM((8, nl), F32)],
        compiler_params=_cparams(("arbitrary",)))(xb, bb_re, bb_im, c_re, c_im, coef)


def _s5_bwd(dy, xb, h_re, h_im, bb_re, bb_im, c_re, c_im, coef, ts):
    s, w = xb.shape
    nb = bb_re.shape[0]
    nl = nb * 512
    nt = s // ts

    def body(dy_ref, x_ref, hre_ref, him_ref, halo_re, halo_im, bre_ref, bim_ref, cre_ref, cim_ref, coef_ref,
             dx_ref, dbre_ref, dbim_ref, dcre_ref, dcim_ref, dare_ref, daim_ref, gre, gim, car_re, car_im):
        i = pl.program_id(0)

        @pl.when(i == 0)
        def _():
            for r in (car_re, car_im, dbre_ref, dbim_ref, dcre_ref, dcim_ref, dare_ref, daim_ref):
                r[...] = jnp.zeros_like(r)

        for b in range(nb):
            dyb = dy_ref[:, b * 128:(b + 1) * 128].astype(BF16)
            gre[:, b * 512:(b + 1) * 512] = lax.dot_general(dyb, cre_ref[b], (((1,), (1,)), ((), ())),
                                                            preferred_element_type=F32)
            gim[:, b * 512:(b + 1) * 512] = -lax.dot_general(dyb, cim_ref[b], (((1,), (1,)), ((), ())),
                                                             preferred_element_type=F32)
            hr = hre_ref[:, b * 512:(b + 1) * 512].astype(BF16)
            hi = him_ref[:, b * 512:(b + 1) * 512].astype(BF16)
            dcre_ref[b] += lax.dot_general(hr, dyb, (((0,), (0,)), ((), ())), preferred_element_type=F32)
            dcim_ref[b] -= lax.dot_general(hi, dyb, (((0,), (0,)), ((), ())), preferred_element_type=F32)
        first = (i != nt - 1).astype(F32)
        _scan_tile(gre, gim, gre, gim, coef_ref, car_re, car_im, ts, True,
                   extra=(hre_ref, him_ref, halo_re, halo_im, first, dare_ref, daim_ref))
        for b in range(nb):
            gr = gre[:, b * 512:(b + 1) * 512].astype(BF16)
            gi = gim[:, b * 512:(b + 1) * 512].astype(BF16)
            xs = x_ref[:, b * 128:(b + 1) * 128].astype(BF16)
            dx_ref[:, b * 128:(b + 1) * 128] = (
                lax.dot_general(gr, bre_ref[b], (((1,), (1,)), ((), ())), preferred_element_type=F32)
                + lax.dot_general(gi, bim_ref[b], (((1,), (1,)), ((), ())), preferred_element_type=F32))
            dbre_ref[b] += lax.dot_general(xs, gr, (((0,), (0,)), ((), ())), preferred_element_type=F32)
            dbim_ref[b] += lax.dot_general(xs, gi, (((0,), (0,)), ((), ())), preferred_element_type=F32)

    row = lambda wd: pl.BlockSpec((ts, wd), lambda i: (nt - 1 - i, 0))
    halo = pl.BlockSpec((8, nl), lambda i: (jnp.maximum((nt - 1 - i) * (ts // 8) - 1, 0), 0))
    return pl.pallas_call(
        body, name="s5_bwd", grid=(nt,),
        in_specs=[row(w), row(w), row(nl), row(nl), halo, halo, _whole(bb_re), _whole(bb_im), _whole(c_re),
                  _whole(c_im), _whole(coef)],
        out_specs=[row(w), _whole(bb_re), _whole(bb_im), _whole(c_re), _whole(c_im),
                   pl.BlockSpec((8, nl), lambda i: (0, 0)), pl.BlockSpec((8, nl), lambda i: (0, 0))],
        out_shape=[jax.ShapeDtypeStruct((s, w), F32), jax.ShapeDtypeStruct(bb_re.shape, F32),
                   jax.ShapeDtypeStruct(bb_im.shape, F32), jax.ShapeDtypeStruct(c_re.shape, F32),
                   jax.ShapeDtypeStruct(c_im.shape, F32), jax.ShapeDtypeStruct((8, nl), F32),
                   jax.ShapeDtypeStruct((8, nl), F32)],
        scratch_shapes=[pltpu.VMEM((ts, nl), F32), pltpu.VMEM((ts, nl), F32), pltpu.VMEM((8, nl), F32),
                        pltpu.VMEM((8, nl), F32)],
        compiler_params=_cparams(("arbitrary",)))(dy, xb, h_re, h_im, h_re, h_im, bb_re, bb_im, c_re, c_im, coef)


def _final(x, mo, target, fg, ts):
    s, d = x.shape

    def f(x, mo, fg, tgt):
        y = _rms(x + mo, fg)[0]
        err = y - tgt
        return 0.5 * jnp.sum(jnp.mean(err * err, axis=-1, keepdims=True), axis=0, keepdims=True)

    def body(x_ref, mo_ref, t_ref, fg_ref, dh_ref, dfg_ref, loss_ref):
        @pl.when(pl.program_id(0) == 0)
        def _():
            dfg_ref[...] = jnp.zeros_like(dfg_ref)
            loss_ref[...] = jnp.zeros_like(loss_ref)

        loss, vjp = jax.vjp(f, x_ref[...], mo_ref[...], fg_ref[...], t_ref[...])
        _, dmo, dfg, _ = vjp(jnp.ones((1, 1), F32))
        dh_ref[...] = dmo
        dfg_ref[...] += dfg
        loss_ref[...] += jnp.broadcast_to(loss, loss_ref.shape)

    row = pl.BlockSpec((ts, d), lambda i: (i, 0))
    return pl.pallas_call(
        body, name="final", grid=(s // ts,), in_specs=[row, row, row, _whole(fg)],
        out_specs=[row, _whole(fg), pl.BlockSpec((8, 128), lambda i: (0, 0))],
        out_shape=[jax.ShapeDtypeStruct((s, d), F32), jax.ShapeDtypeStruct(fg.shape, F32),
                   jax.ShapeDtypeStruct((8, 128), F32)],
        compiler_params=_cparams(("arbitrary",)))(x, mo, target, fg)


N_CHIPS = 4


def _other_chips(x, y):
    return [((1 - x, y), 2 * (1 - x) + y), ((x, 1 - y), 2 * x + 1 - y), ((1 - x, 1 - y), 2 * (1 - x) + 1 - y)]


def _comm_call(body, name, srcs, out_shapes, n_sems):
    n = len(srcs)
    return pl.pallas_call(
        body, name=name, in_specs=[pl.BlockSpec(memory_space=pl.ANY)] * n,
        out_specs=[pl.BlockSpec(memory_space=pl.ANY)] * n, out_shape=out_shapes,
        scratch_shapes=[pltpu.SemaphoreType.DMA((n, n_sems)), pltpu.SemaphoreType.DMA((n, n_sems)),
                        pltpu.SemaphoreType.DMA((n,))],
        compiler_params=pltpu.CompilerParams(has_side_effects=True))(*srcs)


def _gather(srcs, name):
    n = len(srcs)

    def body(*refs):
        src, out = refs[:n], refs[n:2 * n]
        send_sems, recv_sems, local_sems = refs[2 * n:]
        x, y, c = lax.axis_index("x"), lax.axis_index("y"), lax.axis_index("c")
        me, sib_slot, sib = 4 * x + 2 * y + c, 4 * x + 2 * y + 1 - c, (x, y, 1 - c)
        chips = _other_chips(x, y)

        def cp(a, k, src_ref, slot, to):
            return pltpu.make_async_remote_copy(
                src_ref=src_ref, dst_ref=out[a].at[slot], send_sem=send_sems.at[a, k], recv_sem=recv_sems.at[a, k],
                device_id=to, device_id_type=pl.DeviceIdType.MESH)

        local = [pltpu.make_async_copy(src[a], out[a].at[me], local_sems.at[a]) for a in range(n)]
        first = [cp(a, 0, src[a], me, sib) for a in range(n)]
        first += [cp(a, 1 + j, src[a], me, (*chip, c)) for j, (chip, _) in enumerate(chips) for a in range(n)]
        for d in local + first:
            d.start()
        passed = []
        for j, (chip, q) in enumerate(chips):
            for a in range(n):
                cp(a, 1 + j, src[a], 2 * q + c, sib).wait_recv()
                fwd = cp(a, 4 + j, out[a].at[2 * q + c], 2 * q + c, sib)
                fwd.start()
                passed.append(fwd)
        for a in range(n):
            cp(a, 0, src[a], sib_slot, sib).wait_recv()
        for j, (chip, q) in enumerate(chips):
            for a in range(n):
                cp(a, 4 + j, src[a], 2 * q + 1 - c, sib).wait_recv()
        for d in first + passed:
            d.wait_send()
        for d in local:
            d.wait()

    return _comm_call(body, name, srcs, [jax.ShapeDtypeStruct((N_DEV,) + s.shape, s.dtype) for s in srcs], 7)


def _pair_scatter(gs, name):
    n = len(gs)

    def body(*refs):
        src, out = refs[:n], refs[n:2 * n]
        send_sems, recv_sems, _ = refs[2 * n:]
        x, y, c = lax.axis_index("x"), lax.axis_index("y"), lax.axis_index("c")
        sends = []
        for q in range(N_CHIPS):
            for a in range(n):
                d = pltpu.make_async_remote_copy(
                    src_ref=src[a].at[2 * q + 1 - c], dst_ref=out[a].at[q], send_sem=send_sems.at[a, q],
                    recv_sem=recv_sems.at[a, q], device_id=(x, y, 1 - c), device_id_type=pl.DeviceIdType.MESH)
                d.start()
                sends.append(d)
        for d in sends:
            d.wait_recv()
        for d in sends:
            d.wait_send()

    return _comm_call(body, name, gs, [jax.ShapeDtypeStruct((N_CHIPS,) + g.shape[1:], g.dtype) for g in gs], N_CHIPS)


def _cross_scatter(ps, name):
    n = len(ps)

    def body(*refs):
        src, out = refs[:n], refs[n:2 * n]
        send_sems, recv_sems, local_sems = refs[2 * n:]
        x, y, c = lax.axis_index("x"), lax.axis_index("y"), lax.axis_index("c")
        mine = 2 * x + y
        chips = _other_chips(x, y)
        local = [pltpu.make_async_copy(src[a].at[mine], out[a].at[mine], local_sems.at[a]) for a in range(n)]
        for d in local:
            d.start()
        sends = []
        for j, (chip, q) in enumerate(chips):
            for a in range(n):
                d = pltpu.make_async_remote_copy(
                    src_ref=src[a].at[q], dst_ref=out[a].at[mine], send_sem=send_sems.at[a, j],
                    recv_sem=recv_sems.at[a, j], device_id=(*chip, c), device_id_type=pl.DeviceIdType.MESH)
                d.start()
                sends.append(d)
        for j, (chip, q) in enumerate(chips):
            for a in range(n):
                pltpu.make_async_remote_copy(
                    src_ref=src[a].at[q], dst_ref=out[a].at[q], send_sem=send_sems.at[a, j],
                    recv_sem=recv_sems.at[a, j], device_id=(*chip, c), device_id_type=pl.DeviceIdType.MESH).wait_recv()
        for d in sends:
            d.wait_send()
        for d in local:
            d.wait()

    return _comm_call(body, name, ps, [jax.ShapeDtypeStruct(p.shape, p.dtype) for p in ps], 3)


def _pair_sum(g, got, c_idx, name):
    _, r, c = g.shape
    lanes = -(-c // 128) * 128
    tr = _pick_rows(r, max(8, (2 * 1024 * 1024) // (lanes * 4)))
    g4 = g.reshape((N_CHIPS, 2) + g.shape[1:])

    def body(c_ref, g_ref, got_ref, o_ref):
        o_ref[...] = g_ref[...] + got_ref[...]

    return pl.pallas_call(
        body, name=name,
        grid_spec=pltpu.PrefetchScalarGridSpec(
            num_scalar_prefetch=1, grid=(N_CHIPS, r // tr),
            in_specs=[pl.BlockSpec((None, None, tr, c), lambda q, i, cr: (q, cr[0], i, 0)),
                      pl.BlockSpec((None, tr, c), lambda q, i, cr: (q, i, 0))],
            out_specs=pl.BlockSpec((None, tr, c), lambda q, i, cr: (q, i, 0))),
        out_shape=jax.ShapeDtypeStruct(got.shape, F32),
        compiler_params=_cparams(("parallel", "parallel")))(c_idx, g4, got)


def _pick_rows(r, pref):
    t = (min(pref, r) // 8) * 8
    while t >= 8:
        if r % t == 0:
            return t
        t -= 8
    return r


def _w_in_from_shards(t, lo, hi):
    n, r, cs = t.shape
    tr = _pick_rows(r, 256)
    wm = n * cs - (hi - lo)

    def body(t_ref, m_ref, b_ref):
        full = jnp.concatenate([t_ref[j] for j in range(n)], axis=1)
        m_ref[...] = jnp.concatenate([full[:, :lo], full[:, hi:]], axis=1)
        b_ref[...] = jnp.concatenate([full[:, lo:hi], jnp.zeros((tr, 128 - (hi - lo)), full.dtype)], axis=1)

    return pl.pallas_call(
        body, name="w_in_layout", grid=(r // tr,), in_specs=[pl.BlockSpec((n, tr, cs), lambda i: (0, i, 0))],
        out_specs=[pl.BlockSpec((tr, wm), lambda i: (i, 0)), pl.BlockSpec((tr, 128), lambda i: (i, 0))],
        out_shape=[jax.ShapeDtypeStruct((r, wm), t.dtype), jax.ShapeDtypeStruct((r, 128), t.dtype)],
        compiler_params=_cparams(("parallel",)))(t)


def _w_in_to_shards(gm, gb, lo, hi):
    r, wm = gm.shape
    cs = (wm + hi - lo) // N_DEV
    tr = _pick_rows(r, 64)

    def body(m_ref, b_ref, o_ref):
        m = m_ref[...]
        full = jnp.concatenate([m[:, :lo], b_ref[:, :hi - lo], m[:, lo:]], axis=1)
        for j in range(N_DEV):
            o_ref[j] = full[:, j * cs:(j + 1) * cs]

    return pl.pallas_call(
        body, name="dw_in_layout", grid=(r // tr,),
        in_specs=[pl.BlockSpec((tr, wm), lambda i: (i, 0)), pl.BlockSpec((tr, 128), lambda i: (i, 0))],
        out_specs=pl.BlockSpec((N_DEV, tr, cs), lambda i: (0, i, 0)),
        out_shape=jax.ShapeDtypeStruct((N_DEV, r, cs), gm.dtype), compiler_params=_cparams(("parallel",)))(gm, gb)


def _pack(arrs, dtype, lead=()):
    nlead = len(lead)
    flat = jnp.concatenate([a.astype(dtype).reshape(lead + (-1,)) for a in arrs], axis=nlead)
    n = flat.shape[-1]
    unit = PACK_WIDTH * PACK_ROWS
    pad = (-n) % unit
    flat = jnp.pad(flat, [(0, 0)] * nlead + [(0, pad)])
    return flat.reshape(lead + ((n + pad) // PACK_WIDTH, PACK_WIDTH))


def _unpack(buf, shapes, lead=()):
    flat = buf.reshape(lead + (-1,))
    out, off = [], 0
    for shp in shapes:
        n = math.prod(shp)
        out.append(flat[..., off:off + n].reshape(lead + tuple(shp)))
        off += n
    return out


def _adam_math(w, g, m, v):
    m = ADAM_B1 * m + (1.0 - ADAM_B1) * g
    v = ADAM_B2 * v + (1.0 - ADAM_B2) * (g * g)
    m_hat = m / (1.0 - ADAM_B1 ** ADAM_STEP)
    v_hat = v / (1.0 - ADAM_B2 ** ADAM_STEP)
    delta = -ADAM_LR * (m_hat / (jnp.sqrt(v_hat) + ADAM_EPS) + ADAM_WD * w)
    return delta, m, v


def _sum_adam(parts, w, m, v, name):
    r, c = w.shape
    nparts = parts.shape[0]
    lanes = -(-c // 128) * 128
    tr = _pick_rows(r, max(8, (6 * 1024 * 1024) // (nparts * lanes * 4)))

    def body(p_ref, w_ref, m_ref, v_ref, g_ref, d_ref, nm_ref, nv_ref):
        g = p_ref[0]
        for j in range(1, nparts):
            g = g + p_ref[j]
        d, nm, nv = _adam_math(w_ref[...], g, m_ref[...], v_ref[...])
        g_ref[...] = g
        d_ref[...] = d
        nm_ref[...] = nm
        nv_ref[...] = nv

    row = pl.BlockSpec((tr, c), lambda i: (i, 0))
    return pl.pallas_call(
        body, name=name, grid=(r // tr,), in_specs=[pl.BlockSpec((nparts, tr, c), lambda i: (0, i, 0)), row, row, row],
        out_specs=[row] * 4, out_shape=[jax.ShapeDtypeStruct((r, c), F32)] * 4,
        compiler_params=_cparams(("parallel",)))(parts, w, m, v)


def _block_diag(t):
    nb, g, a, b = t.shape
    eye = jnp.eye(g, dtype=t.dtype)
    return jnp.einsum('ngab,gh->ngahb', t, eye).reshape(nb, g * a, g * b)


def _diag_blocks(t, a, b):
    nb = t.shape[0]
    g = S5_GROUPS_PER_BLOCK
    t = t.reshape(nb, g, a, g, b)
    return jnp.stack([t[:, j, :, j, :] for j in range(g)], axis=1)


def _local_step(x, mem, target, p):
    s, d = x.shape
    gw = d // 2
    nh = gw // GDN_HEAD_DIM
    ng = gw // S5_GROUP
    nb = ng // S5_GROUPS_PER_BLOCK
    nl = ng * S5_STATE
    ts = min(256, s)
    nt = s // ts
    grads = {}

    w_main, w_ba = p['w_main'], p['w_ba']
    CB_ZA, CB_XB, CB_ZB, CB_QC, CB_ZC, CB_G = 3, 4, 5, 6, 7, 8

    u = _tile_fwd(_rms, "rms_fwd", nt, [_rt(x, ts)], [p['norm_g']],
                  [((s, d), BF16, (ts, d), lambda i: (i, 0))])[0]
    proj = _mm(u, w_main, name="proj_main")
    pba = _mm(u, w_ba, name="proj_ba")

    conv_w = p['conv_w']
    col = lambda arr, cb: (arr, (s, GDN_HEAD_DIM), lambda i, cb=cb: (0, cb + i))
    qkv = []
    for j, mode in enumerate(('q', 'k', 'v')):
        off = j * nh
        qkv.append(_tile_fwd(
            _gdn_pre(mode), "gdn_pre_" + mode, nh, [col(proj, off), (conv_w, (CONV_WIDTH, GDN_HEAD_DIM), lambda i, off=off: (0, off + i))],
            [], [((s, gw), F32, (s, GDN_HEAD_DIM), lambda i: (0, i))])[0])
    q, k, v = qkv
    lane = jnp.arange(128)[:, None]
    colh = jnp.arange(gw)[None, :] // GDN_HEAD_DIM
    e_beta = (lane == colh).astype(F32)
    e_g = (lane == colh + nh).astype(F32)
    alog_row = jnp.pad(p['gdn_a_log'], ((0, 0), (nh, 128 - 2 * nh)))
    dtb_row = jnp.pad(p['gdn_dt_bias'], ((0, 0), (nh, 128 - 2 * nh)))
    row_gw = lambda: ((s, gw), F32, (ts, gw), lambda i: (i, 0))
    betab, gb = _tile_fwd(_gdn_gates, "gdn_gates", nt, [_rt(pba, ts)], [alog_row, dtb_row, e_beta, e_g],
                          [row_gw(), row_gw()])
    intra = _gdn_intra_fwd(q, k, v, gb, betab)
    o_raw, states = _gdn_inter_fwd(*intra, gb)
    ga = _tile_fwd(_gdn_post, "gdn_post", nt, [_rt(o_raw, ts), _rt(proj, ts, CB_ZA, gw)], [p['gdn_norm_g']],
                   [((s, gw), BF16, (ts, gw), lambda i: (i, 0))])[0]

    e_rep = (jnp.arange(S5_STATE)[:, None] == jnp.arange(S5_STATE * S5_GROUP)[None, :] // S5_GROUP).astype(F32)
    s5_in = [p['s5_lambda_re'], p['s5_lambda_im'], p['s5_log_dt'].reshape(ng, 1),
             p['s5_b_re'].reshape(ng, S5_STATE * S5_GROUP), p['s5_b_im'].reshape(ng, S5_STATE * S5_GROUP), e_rep]
    one = lambda shp: (shp, F32, shp, lambda i, n=len(shp): (0,) * n)
    ab_re, ab_im, bbr, bbi = _tile_fwd(_s5_params, "s5_params", 1, [], s5_in,
                                       [one((ng, S5_STATE)), one((ng, S5_STATE)), one((ng, S5_STATE * S5_GROUP)),
                                        one((ng, S5_STATE * S5_GROUP))])
    coef = _s5_coef(ab_re.reshape(1, nl), ab_im.reshape(1, nl))
    to_bd_b = lambda t: _block_diag(t.reshape(nb, S5_GROUPS_PER_BLOCK, S5_STATE, S5_GROUP).transpose(0, 1, 3, 2))
    to_bd_c = lambda t: _block_diag(t.reshape(nb, S5_GROUPS_PER_BLOCK, S5_GROUP, S5_STATE).transpose(0, 1, 3, 2))
    bbd_re, bbd_im = to_bd_b(bbr).astype(BF16), to_bd_b(bbi).astype(BF16)
    cbd_re, cbd_im = to_bd_c(p['s5_c_re']).astype(BF16), to_bd_c(p['s5_c_im']).astype(BF16)
    xb_arr = lax.slice_in_dim(proj, CB_XB * gw, (CB_XB + 1) * gw, axis=1)
    h_re, h_im, ylin = _s5_fwd(xb_arr, bbd_re, bbd_im, cbd_re, cbd_im, coef, ts)
    gl = _tile_fwd(_s5_post1, "s5_post1", nt, [_rt(ylin, ts), _rt(proj, ts, CB_XB, gw)], [p['s5_d']],
                   [((s, gw), BF16, (ts, gw), lambda i: (i, 0))])[0]
    tglu = _mm(gl, p['s5_w_glu'], b_shards=True, name="s5_glu")
    gbb = _tile_fwd(_s5_post2, "s5_post2", nt, [_rt(tglu, ts), _rt(proj, ts, CB_ZB, gw)], [],
                    [((s, gw), BF16, (ts, gw), lambda i: (i, 0))])[0]

    m_len = mem.shape[0]
    mem_n = _tile_fwd(_rms, "mem_rms", 1, [_rt(mem, m_len)], [p['mem_norm_g']],
                      [((m_len, d), BF16, (m_len, d), lambda i: (i, 0))])[0]
    kv = _mm(mem_n, p['w_kv_mem'], name="mem_kv")
    gcc = _tile_fwd(_attn, "attn", nt, [_rt(proj, ts, CB_QC, gw), _rt(proj, ts, CB_ZC, gw)], [kv],
                    [((s, gw), BF16, (ts, gw), lambda i: (i, 0))])[0]

    p_a = _mm(ga, p['w_br_a'], b_shards=True, name="br_a")
    p_b = _mm(gbb, p['w_br_b'], b_shards=True, name="br_b")
    p_c = _mm(gcc, p['w_br_c'], b_shards=True, name="br_c")
    gate_acts = [_rt(proj, ts, CB_G // 2 + j, d) for j in range(3)]
    merged = _tile_fwd(_merge, "merge", nt, gate_acts + [_rt(p_a, ts), _rt(p_b, ts), _rt(p_c, ts)], [],
                       [((s, d), BF16, (ts, d), lambda i: (i, 0))])[0]
    mo = _mm(merged, p['w_out'], name="out_proj")
    dh, dfg, loss = _final(x, mo, target, p['final_g'].reshape(1, d), ts)
    grads['final_g'] = dfg.reshape(d)

    dmerged = _mm(dh, p['w_out'], tb=True, name="d_merged")
    grads['w_out'] = _mm(merged, dh, ta=True, name="dw_out")
    row_d = lambda dt: ((s, d), dt, (ts, d), lambda i: (i, 0))
    dg0, dg1, dg2, dpa, dpb, dpc = _tile_bwd(
        _merge, "merge_bwd", nt, gate_acts + [_rt(p_a, ts), _rt(p_b, ts), _rt(p_c, ts)], [], [_rt(dmerged, ts)],
        [row_d(BF16)] * 6, [])
    dga = _mm(dpa, p['w_br_a'], tb=True, b_shards=True, name="d_ga")
    dgbb = _mm(dpb, p['w_br_b'], tb=True, b_shards=True, name="d_gb")
    dgcc = _mm(dpc, p['w_br_c'], tb=True, b_shards=True, name="d_gc")
    grads['w_br_a'] = _mm(ga, dpa, ta=True, out_shards=N_DEV, name="dw_br_a")
    grads['w_br_b'] = _mm(gbb, dpb, ta=True, out_shards=N_DEV, name="dw_br_b")
    grads['w_br_c'] = _mm(gcc, dpc, ta=True, out_shards=N_DEV, name="dw_br_c")
    row_h = lambda dt: ((s, gw), dt, (ts, gw), lambda i: (i, 0))

    dqc, dzc, dkv = _tile_bwd(_attn, "attn_bwd", nt, [_rt(proj, ts, CB_QC, gw), _rt(proj, ts, CB_ZC, gw)], [kv],
                              [_rt(dgcc, ts)], [row_h(BF16), row_h(BF16)], [True])
    grads['w_kv_mem'] = _mm(mem_n, dkv, ta=True, name="dw_kv")
    dmem_n = _mm(dkv, p['w_kv_mem'], tb=True, name="d_mem_n")
    grads['mem_norm_g'] = _tile_bwd(_rms, "mem_rms_bwd", 1, [_rt(mem, m_len)], [p['mem_norm_g']],
                                    [_rt(dmem_n, m_len)], [None], [True])[0]

    dtglu, dzb = _tile_bwd(_s5_post2, "s5_post2_bwd", nt, [_rt(tglu, ts), _rt(proj, ts, CB_ZB, gw)], [],
                           [_rt(dgbb, ts)], [((s, 2 * gw), BF16, (ts, 2 * gw), lambda i: (i, 0)), row_h(BF16)], [])
    grads['s5_w_glu'] = _mm(gl, dtglu, ta=True, out_shards=N_DEV, name="dw_glu")
    dgl = _mm(dtglu, p['s5_w_glu'], tb=True, b_shards=True, name="d_gl")
    dylin, dxb1, dd = _tile_bwd(_s5_post1, "s5_post1_bwd", nt, [_rt(ylin, ts), _rt(proj, ts, CB_XB, gw)],
                                [p['s5_d']], [_rt(dgl, ts)], [row_h(F32), row_h(F32)], [True])
    grads['s5_d'] = dd
    dxb2, dbbd_re, dbbd_im, dcbd_re, dcbd_im, da_re, da_im = _s5_bwd(dylin, xb_arr, h_re, h_im, bbd_re, bbd_im,
                                                                    cbd_re, cbd_im, coef, ts)
    from_bd_b = lambda t: _diag_blocks(t, S5_GROUP, S5_STATE).transpose(0, 1, 3, 2).reshape(ng, S5_STATE * S5_GROUP)
    from_bd_c = lambda t: _diag_blocks(t, S5_STATE, S5_GROUP).transpose(0, 1, 3, 2).reshape(1, ng, S5_GROUP, S5_STATE)
    grads['s5_c_re'], grads['s5_c_im'] = from_bd_c(dcbd_re), from_bd_c(dcbd_im)
    s5_cts = [jnp.sum(da_re, axis=0).reshape(ng, S5_STATE), jnp.sum(da_im, axis=0).reshape(ng, S5_STATE),
              from_bd_b(dbbd_re), from_bd_b(dbbd_im)]
    dlr, dli, dlogdt, dbr, dbi = _tile_bwd(_s5_params, "s5_params_bwd", 1, [], s5_in,
                                           [(c, c.shape, lambda i: (0, 0)) for c in s5_cts], [],
                                           [True, True, True, True, True, False])
    grads['s5_lambda_re'], grads['s5_lambda_im'] = dlr[None], dli[None]
    grads['s5_log_dt'] = dlogdt.reshape(1, ng)
    grads['s5_b_re'] = dbr.reshape(1, ng, S5_STATE, S5_GROUP)
    grads['s5_b_im'] = dbi.reshape(1, ng, S5_STATE, S5_GROUP)
    dxb = (dxb1 + dxb2).astype(BF16)

    do_raw, dza, dgng = _tile_bwd(_gdn_post, "gdn_post_bwd", nt, [_rt(o_raw, ts), _rt(proj, ts, CB_ZA, gw)],
                                  [p['gdn_norm_g']], [_rt(dga, ts)], [row_h(F32), row_h(BF16)], [True])
    grads['gdn_norm_g'] = dgng
    *intra_cts, dgb_inter = _gdn_inter_bwd(*intra, gb, states, do_raw)
    dq, dk, dv, dgb, dbetab = _gdn_intra_bwd(q, k, v, gb, betab, intra_cts, dgb_inter)
    dpba, dalog, ddtb = _tile_bwd(_gdn_gates, "gdn_gates_bwd", nt, [_rt(pba, ts)], [alog_row, dtb_row, e_beta, e_g],
                                  [_rt(dbetab, ts), _rt(dgb, ts)], [((s, 128), BF16, (ts, 128), lambda i: (i, 0))],
                                  [True, True, False, False])
    grads['gdn_a_log'] = dalog[:, nh:2 * nh]
    grads['gdn_dt_bias'] = ddtb[:, nh:2 * nh]
    dqkv, dconv = [], []
    for j, (mode, ct) in enumerate((('q', dq), ('k', dk), ('v', dv))):
        off = j * nh
        wspec = (conv_w, (CONV_WIDTH, GDN_HEAD_DIM), lambda i, off=off: (0, off + i))
        dxc, dwc = _tile_bwd(
            _gdn_pre(mode), "gdn_pre_bwd_" + mode, nh, [col(proj, off), wspec], [], [col(ct, 0)],
            [((s, gw), BF16, (s, GDN_HEAD_DIM), lambda i: (0, i)),
             ((CONV_WIDTH, gw), F32, (CONV_WIDTH, GDN_HEAD_DIM), lambda i: (0, i))], [])
        dqkv.append(dxc)
        dconv.append(dwc)
    grads['conv_w'] = jnp.concatenate(dconv, axis=1)

    dproj = jnp.concatenate(dqkv + [dza, dxb, dzb, dqc, dzc, dg0, dg1, dg2], axis=1)
    du = _mm(dpba, w_ba, tb=True, name="du_ba")
    du = _mm(dproj, w_main, tb=True, addend=du, name="du_main")
    grads['w_main'] = _mm(u, dproj, ta=True, name="dw_main")
    grads['w_ba'] = _mm(u, dpba, ta=True, name="dw_ba")
    dx, dng = _tile_bwd(_rms, "rms_bwd", nt, [_rt(x, ts)], [p['norm_g']], [_rt(du, ts)], [row_d(F32)], [True])
    grads['norm_g'] = dng
    return loss, dx + dh, grads


def _to_shards(name, g):
    if SHARDED[name] == 'row':
        return g.reshape((N_DEV, g.shape[0] // N_DEV) + g.shape[1:])
    r, c = g.shape
    return g.reshape(r, N_DEV, c // N_DEV).transpose(1, 0, 2)


def _from_shards(name, t):
    if SHARDED[name] == 'row':
        return t.reshape((t.shape[0] * t.shape[1],) + t.shape[2:])
    n, r, c = t.shape
    return t.transpose(1, 0, 2).reshape(r, n * c)


def _step(x, mem, target, w, m, v):
    sharded = list(SHARDED)
    shard_shapes = {n: tuple(w[n].shape[1:]) for n in sharded}
    d = x.shape[-1]
    ba_lo = 2 * d
    ba_hi = ba_lo + 2 * (d // 2 // GDN_HEAD_DIM)

    srcs = [w[n][0].astype(BF16) for n in GATHER_BF16] + [w['conv_w'][0]]
    got = _gather(srcs, "gather_weights")
    full = {n: t for n, t in zip(GATHER_BF16 + ['conv_w'], got)}
    full['w_main'], full['w_ba'] = _w_in_from_shards(full.pop('w_in'), ba_lo, ba_hi)
    for n in ('w_kv_mem', 'w_out', 'conv_w'):
        full[n] = _from_shards(n, full[n])
    for n in REPLICATED:
        full[n] = w[n]
    for n in ('s5_lambda_re', 's5_lambda_im', 's5_c_re', 's5_c_im'):
        full[n] = w[n][0]

    loss, grad_x, grads = _local_step(x[0], mem[0], target[0], full)

    grads['w_in'] = _w_in_to_shards(grads.pop('w_main'), grads.pop('w_ba'), ba_lo, ba_hi)
    for n in ('w_kv_mem', 'w_out', 'conv_w'):
        grads[n] = _to_shards(n, grads[n])
    gs = [grads[n] for n in sharded]
    c_idx = lax.axis_index("c").astype(jnp.int32).reshape(1)
    from_sibling = _pair_scatter(gs, "scatter_pair")
    chip_sums = [_pair_sum(g, got, c_idx, "pair_sum_" + n) for n, g, got in zip(sharded, gs, from_sibling)]
    parts = _cross_scatter(chip_sums, "scatter_cross")
    res = {}
    for n, part in zip(sharded, parts):
        outs = _sum_adam(part, w[n][0], m[n][0], v[n][0], name="adam_" + n)
        for kind, t in zip(('grad', 'delta', 'new_m', 'new_v'), outs):
            res[kind, n] = t[None]

    small = _pack([grads[n].reshape(w[n].shape) for n in REPLICATED] + [loss[:1, :1]], F32)
    allp = _gather([small], "gather_small")[0]
    zero = jnp.zeros((1, 1), F32)
    outs = _sum_adam(allp, *[_pack([t[n] for n in REPLICATED] + [zero], F32) for t in (w, m, v)], name="adam_small")
    shapes = [w[n].shape for n in REPLICATED] + [(1, 1)]
    for kind, buf in zip(('grad', 'delta', 'new_m', 'new_v'), outs):
        got = _unpack(buf, shapes)
        for n, t in zip(REPLICATED, got):
            res[kind, n] = t
        if kind == 'grad':
            total_loss = got[-1].reshape(())
    out = [total_loss, grad_x[None]]
    for kind in ('grad', 'delta', 'new_m', 'new_v'):
        out += [res[kind, n] for n in WEIGHTS]
    return tuple(out)


def kernel(x, mem, norm_g, w_in, conv_w, gdn_a_log, gdn_dt_bias, gdn_norm_g, s5_lambda_re, s5_lambda_im, s5_log_dt, s5_b_re, s5_b_im, s5_c_re, s5_c_im, s5_d, s5_w_glu, mem_norm_g, w_kv_mem, w_br_a, w_br_b, w_br_c, w_out, final_g, loss_target, m_norm_g, m_w_in, m_conv_w, m_gdn_a_log, m_gdn_dt_bias, m_gdn_norm_g, m_s5_lambda_re, m_s5_lambda_im, m_s5_log_dt, m_s5_b_re, m_s5_b_im, m_s5_c_re, m_s5_c_im, m_s5_d, m_s5_w_glu, m_mem_norm_g, m_w_kv_mem, m_w_br_a, m_w_br_b, m_w_br_c, m_w_out, m_final_g, v_norm_g, v_w_in, v_conv_w, v_gdn_a_log, v_gdn_dt_bias, v_gdn_norm_g, v_s5_lambda_re, v_s5_lambda_im, v_s5_log_dt, v_s5_b_re, v_s5_b_im, v_s5_c_re, v_s5_c_im, v_s5_d, v_s5_w_glu, v_mem_norm_g, v_w_kv_mem, v_w_br_a, v_w_br_b, v_w_br_c, v_w_out, v_final_g):
    a = dict(locals())
    w = {n: a[n] for n in WEIGHTS}
    m = {n: a['m_' + n] for n in WEIGHTS}
    v = {n: a['v_' + n] for n in WEIGHTS}
    return _step(x, mem, loss_target, w, m, v)
```

```python
import functools
import math

import jax
import jax.numpy as jnp
from jax import lax
from jax.experimental import pallas as pl
from jax.experimental.pallas import tpu as pltpu

F32 = jnp.float32
BF16 = jnp.bfloat16
HI = lax.Precision.HIGHEST

EPS = 1e-6
CHUNK = 64
GDN_HEAD_DIM = 128
CONV_WIDTH = 4
S5_GROUP = 16
S5_STATE = 64
S5_GROUPS_PER_BLOCK = 8
XA_HEADS = 4
N_DEV = 8
ADAM_LR, ADAM_B1, ADAM_B2, ADAM_EPS, ADAM_WD, ADAM_STEP = 0.001, 0.9, 0.999, 1e-08, 0.01, 10

VMEM_LIMIT_BYTES = 56 * 1024 * 1024
SCAN_LANES = 512
PACK_WIDTH = 512
PACK_ROWS = 256

WEIGHTS = ['norm_g', 'w_in', 'conv_w', 'gdn_a_log', 'gdn_dt_bias', 'gdn_norm_g', 's5_lambda_re', 's5_lambda_im',
           's5_log_dt', 's5_b_re', 's5_b_im', 's5_c_re', 's5_c_im', 's5_d', 's5_w_glu', 'mem_norm_g', 'w_kv_mem',
           'w_br_a', 'w_br_b', 'w_br_c', 'w_out', 'final_g']
SHARDED = {'w_in': 'col', 'conv_w': 'col', 's5_w_glu': 'col', 'w_kv_mem': 'row', 'w_br_a': 'col', 'w_br_b': 'col',
           'w_br_c': 'col', 'w_out': 'row'}
GATHER_BF16 = ['w_in', 's5_w_glu', 'w_kv_mem', 'w_br_a', 'w_br_b', 'w_br_c', 'w_out']
REPLICATED = [n for n in WEIGHTS if n not in SHARDED]


def _cparams(sem=None):
    return pltpu.CompilerParams(dimension_semantics=sem, vmem_limit_bytes=VMEM_LIMIT_BYTES)


def _pick(dim, pref):
    t = (min(pref, dim) // 128) * 128
    while t >= 128:
        if dim % t == 0:
            return t
        t -= 128
    return dim


def _make_dots(prep, precision):
    def raw(a, b, dims):
        return lax.dot_general(prep(a), prep(b), (dims, ((), ())), preferred_element_type=F32, precision=precision)

    @jax.custom_vjp
    def nn(a, b):
        return raw(a, b, ((1,), (0,)))

    @jax.custom_vjp
    def nt(a, b):
        return raw(a, b, ((1,), (1,)))

    @jax.custom_vjp
    def tn(a, b):
        return raw(a, b, ((0,), (0,)))

    nn.defvjp(lambda a, b: (nn(a, b), (a, b)), lambda r, ct: (nt(ct, r[1]), tn(r[0], ct)))
    nt.defvjp(lambda a, b: (nt(a, b), (a, b)), lambda r, ct: (nn(ct, r[1]), tn(ct, r[0])))
    tn.defvjp(lambda a, b: (tn(a, b), (a, b)), lambda r, ct: (nt(r[1], ct), nn(r[0], ct)))
    return nn, nt, tn


_bnn, _bnt, _btn = _make_dots(lambda a: a.astype(BF16), None)
_hnn, _hnt, _htn = _make_dots(lambda a: a.astype(F32), HI)
_mnn, _mnt, _mtn = _make_dots(lambda a: a.astype(F32), lax.Precision.HIGH)


def _mm(a, b, *, name, ta=False, tb=False, out_dtype=F32, addend=None, tm=512, tn=1024, tk=1024, b_shards=False,
        out_shards=0):
    m, k = (a.shape[1], a.shape[0]) if ta else a.shape
    brows, bcols = (b.shape[1], b.shape[0] * b.shape[2]) if b_shards else b.shape
    n = brows if tb else bcols
    assert (bcols if tb else brows) == k, (a.shape, b.shape, ta, tb)
    tm, tn, tk = _pick(m, tm), _pick(n, tn), _pick(k, tk)
    bcs = ocs = 0
    if b_shards:
        bcs = b.shape[2]
        assert bcs % 128 == 0 and (tk if tb else tn) % bcs == 0
    if out_shards:
        ocs = n // out_shards
        assert ocs % 128 == 0 and tn % ocs == 0
    nk = k // tk
    dims = ((0 if ta else 1,), (1 if tb else 0,))

    def body(*refs):
        if addend is None:
            a_ref, b_ref, o_ref, acc_ref = refs
        else:
            a_ref, b_ref, add_ref, o_ref, acc_ref = refs
        kk = pl.program_id(2)

        @pl.when(kk == 0)
        def _():
            acc_ref[...] = jnp.zeros_like(acc_ref)

        dot = lambda x, y: lax.dot_general(x.astype(BF16), y.astype(BF16), (dims, ((), ())), preferred_element_type=F32)
        if not b_shards:
            acc_ref[...] += dot(a_ref[...], b_ref[...])
        elif tb:
            for g in range(tk // bcs):
                acc_ref[...] += dot(a_ref[:, g * bcs:(g + 1) * bcs], b_ref[g])
        else:
            for g in range(tn // bcs):
                acc_ref[:, g * bcs:(g + 1) * bcs] += dot(a_ref[...], b_ref[g])

        @pl.when(kk == nk - 1)
        def _():
            r = acc_ref[...]
            if addend is not None:
                r = r + add_ref[...].astype(F32)
            if out_shards:
                for g in range(tn // ocs):
                    o_ref[g] = r[:, g * ocs:(g + 1) * ocs].astype(o_ref.dtype)
            else:
                o_ref[...] = r.astype(o_ref.dtype)

    a_spec = pl.BlockSpec((tk, tm), lambda i, j, kk: (kk, i)) if ta else pl.BlockSpec((tm, tk), lambda i, j, kk: (i, kk))
    if b_shards:
        b_spec = (pl.BlockSpec((tk // bcs, tn, bcs), lambda i, j, kk: (kk, j, 0)) if tb
                  else pl.BlockSpec((tn // bcs, tk, bcs), lambda i, j, kk: (j, kk, 0)))
    else:
        b_spec = (pl.BlockSpec((tn, tk), lambda i, j, kk: (j, kk)) if tb
                  else pl.BlockSpec((tk, tn), lambda i, j, kk: (kk, j)))
    if out_shards:
        o_spec = pl.BlockSpec((tn // ocs, tm, ocs), lambda i, j, kk: (j, i, 0))
        out_shape = jax.ShapeDtypeStruct((out_shards, m, ocs), out_dtype)
    else:
        o_spec = pl.BlockSpec((tm, tn), lambda i, j, kk: (i, j))
        out_shape = jax.ShapeDtypeStruct((m, n), out_dtype)
    in_specs = [a_spec, b_spec] + ([o_spec] if addend is not None else [])
    args = (a, b) + ((addend,) if addend is not None else ())
    return pl.pallas_call(
        body, name=name, grid=(m // tm, n // tn, nk), in_specs=in_specs, out_specs=o_spec,
        out_shape=out_shape, scratch_shapes=[pltpu.VMEM((tm, tn), F32)],
        compiler_params=_cparams(("parallel", "parallel", "arbitrary")))(*args)


def _rt(arr, ts, cb=0, w=None):
    w = arr.shape[1] if w is None else w
    return (arr, (ts, w), lambda i, cb=cb: (i, cb))


def _whole(p):
    return pl.BlockSpec(p.shape, lambda i, nd=p.ndim: (0,) * nd)


def _tile_fwd(f, name, n, acts, params, outs):
    na, npar = len(acts), len(params)

    def body(*refs):
        res = f(*[r[...] for r in refs[:na + npar]])
        for r, v in zip(refs[na + npar:], res):
            r[...] = v.astype(r.dtype)

    in_specs = [pl.BlockSpec(b, m) for _, b, m in acts] + [_whole(p) for p in params]
    out = pl.pallas_call(
        body, name=name, grid=(n,), in_specs=in_specs,
        out_specs=[pl.BlockSpec(b, m) for _, _, b, m in outs],
        out_shape=[jax.ShapeDtypeStruct(s, d) for s, d, _, _ in outs],
        compiler_params=_cparams(("parallel",)))(*[a for a, _, _ in acts], *params)
    return out


def _tile_bwd(f, name, n, acts, params, cts, agrads, pgrads):
    na, npar, nc = len(acts), len(params), len(cts)

    def body(*refs):
        i = pl.program_id(0)
        ins = [r[...] for r in refs[:na + npar]]
        outs, vjp = jax.vjp(f, *ins)
        g = vjp(tuple(c[...].astype(o.dtype) for c, o in zip(refs[na + npar:na + npar + nc], outs)))
        orefs = refs[na + npar + nc:]
        k = 0
        for j in range(na):
            if agrads[j] is not None:
                orefs[k][...] = g[j].astype(orefs[k].dtype)
                k += 1
        for j in range(npar):
            if pgrads[j]:
                o = orefs[k]

                @pl.when(i == 0)
                def _(o=o):
                    o[...] = jnp.zeros_like(o)

                o[...] += g[na + j].astype(F32)
                k += 1

    in_specs = ([pl.BlockSpec(b, m) for _, b, m in acts] + [_whole(p) for p in params]
                + [pl.BlockSpec(b, m) for _, b, m in cts])
    out_specs = [pl.BlockSpec(g[2], g[3]) for g in agrads if g is not None]
    out_shape = [jax.ShapeDtypeStruct(g[0], g[1]) for g in agrads if g is not None]
    for p, flag in zip(params, pgrads):
        if flag:
            out_specs.append(_whole(p))
            out_shape.append(jax.ShapeDtypeStruct(p.shape, F32))
    return pl.pallas_call(
        body, name=name, grid=(n,), in_specs=in_specs, out_specs=out_specs, out_shape=out_shape,
        compiler_params=_cparams(("arbitrary",)))(*[a for a, _, _ in acts], *params, *[c for c, _, _ in cts])


def _silu(x):
    return x * jax.nn.sigmoid(x)


def _rms(x, g):
    x = x.astype(F32)
    return (x * lax.rsqrt(jnp.mean(x * x, axis=-1, keepdims=True) + EPS) * g,)


def _shift_down(x, s):
    row = lax.broadcasted_iota(jnp.int32, x.shape, 0)
    return jnp.where(row >= s, pltpu.roll(x, s, 0), 0.0)


def _shift_up(x, s):
    n = x.shape[0]
    row = lax.broadcasted_iota(jnp.int32, x.shape, 0)
    return jnp.where(row < n - s, pltpu.roll(x, n - s, 0), 0.0)


@functools.partial(jax.custom_vjp, nondiff_argnums=(1,))
def _shift(x, s):
    return _shift_down(x, s)


_shift.defvjp(lambda x, s: (_shift_down(x, s), None), lambda s, _, ct: (_shift_up(ct, s),))


def _gdn_pre(mode):
    def f(x, w):
        y = x * w[CONV_WIDTH - 1:CONV_WIDTH, :]
        for j in range(CONV_WIDTH - 1):
            y = y + _shift(x, CONV_WIDTH - 1 - j) * w[j:j + 1, :]
        y = _silu(y)
        if mode != 'v':
            y = y * lax.rsqrt(jnp.sum(y * y, axis=-1, keepdims=True) + EPS)
        if mode == 'q':
            y = y * (GDN_HEAD_DIM ** -0.5)
        return (y,)
    return f


def _softplus(x):
    return jnp.maximum(x, 0.0) + jnp.log1p(jnp.exp(-jnp.abs(x)))


def _gdn_gates(ba, alog, dtb, e_beta, e_g):
    beta = jax.nn.sigmoid(ba)
    g = -jnp.exp(alog) * _softplus(ba + dtb)
    return _hnn(beta, lax.stop_gradient(e_beta)), _hnn(g, lax.stop_gradient(e_g))


def _gdn_intra(q, k, v, gb, bb):
    n, c = len(q), q[0].shape[0]
    ri = lax.broadcasted_iota(jnp.int32, (c, c), 0)
    ci = lax.broadcasted_iota(jnp.int32, (c, c), 1)
    incl, strict = ri >= ci, ri > ci
    tri = incl.astype(F32)
    eye = (ri == ci).astype(F32)
    each = range(n)
    gc = [_hnn(tri, gb[i]) for i in each]
    decay = [jnp.exp(jnp.where(incl, gc[i][:, :c] - gc[i].T[:c, :], -1e30)) for i in each]
    kb = [k[i] * bb[i] for i in each]
    kk = [_bnt(kb[i], k[i]) for i in each]
    qk = [_bnt(q[i], k[i]) for i in each]
    p = [jnp.where(strict, -(kk[i] * decay[i]), 0.0) for i in each]
    t = [eye + p[i] for i in each]
    for _ in range(int(math.log2(c)) - 1):
        p = [_mnn(p[i], p[i]) for i in each]
        tp = [_mnn(t[i], p[i]) for i in each]
        t = [t[i] + tp[i] for i in each]
    egc = [jnp.exp(gc[i]) for i in each]
    u_val = [_mnn(t[i], v[i] * bb[i]) for i in each]
    w_dec = [_mnn(t[i], kb[i] * egc[i]) for i in each]
    qk = [qk[i] * decay[i] for i in each]
    gl = [jnp.sum(gb[i], axis=0, keepdims=True) for i in each]
    return w_dec, u_val, qk, [q[i] * egc[i] for i in each], [k[i] * jnp.exp(gl[i] - gc[i]) for i in each]


def _gdn_inter(w_dec, u_val, qk, q_dec, k_dec, gb, state):
    each = range(len(state))
    ws = [_bnn(w_dec[i], state[i]) for i in each]
    qs = [_bnn(q_dec[i], state[i]) for i in each]
    v_new = [u_val[i] - ws[i] for i in each]
    qv = [_bnn(qk[i], v_new[i]) for i in each]
    kv = [_btn(k_dec[i], v_new[i]) for i in each]
    decayed = [state[i] * jnp.exp(jnp.sum(gb[i], axis=0, keepdims=True)) for i in each]
    return [qs[i] + qv[i] for i in each], [decayed[i] + kv[i] for i in each]


def _gdn_post(o, z, g):
    parts = []
    for h in range(o.shape[1] // GDN_HEAD_DIM):
        oh = o[:, h * GDN_HEAD_DIM:(h + 1) * GDN_HEAD_DIM]
        parts.append(oh * lax.rsqrt(jnp.mean(oh * oh, axis=-1, keepdims=True) + EPS) * g)
    y = parts[0] if len(parts) == 1 else jnp.concatenate(parts, axis=1)
    return (y * _silu(z),)


def _gelu(x):
    return 0.5 * x * (1.0 + jnp.tanh(0.7978845608028654 * (x + 0.044715 * x * x * x)))


def _s5_post1(ylin, xb, d):
    return (_gelu(ylin + d * xb),)


def _s5_post2(t, z):
    w = t.shape[1] // 2
    return (t[:, :w] * jax.nn.sigmoid(t[:, w:]) * _silu(z),)


def _attn(q, z, kv):
    w = q.shape[1]
    hd = w // XA_HEADS
    parts = []
    for h in range(XA_HEADS):
        s = _bnt(q[:, h * hd:(h + 1) * hd], kv[:, h * hd:(h + 1) * hd]) * (hd ** -0.5)
        s = s - jnp.max(s, axis=-1, keepdims=True)
        e = jnp.exp(s)
        p = e / jnp.sum(e, axis=-1, keepdims=True)
        parts.append(_bnn(p, kv[:, w + h * hd:w + (h + 1) * hd]))
    return (jnp.concatenate(parts, axis=1) * _silu(z),)


def _merge(g0, g1, g2, pa, pb, pc):
    return (jax.nn.sigmoid(g0) * pa + jax.nn.sigmoid(g1) * pb + jax.nn.sigmoid(g2) * pc,)


def _s5_params(lr, li, logdt, br, bi, e):
    dt = jnp.exp(logdt)
    mag = jnp.exp(lr * dt)
    ab_re, ab_im = mag * jnp.cos(li * dt), mag * jnp.sin(li * dt)
    den = lr * lr + li * li
    nr, ni = ab_re - 1.0, ab_im
    e = lax.stop_gradient(e)
    cre = _hnn((nr * lr + ni * li) / den, e)
    cim = _hnn((ni * lr - nr * li) / den, e)
    return ab_re, ab_im, cre * br - cim * bi, cre * bi + cim * br


def _gdn_blocks(s, w, per_step):
    nh, nc = w // GDN_HEAD_DIM, s // CHUNK
    cpb = math.gcd(per_step, nc)
    return nh, nc, cpb, nc // cpb, (cpb * CHUNK, w), (cpb * CHUNK, nh * CHUNK)


def _gdn_pairs(cpb, nh):
    wide, narrow = [], []
    for cb in range(cpb):
        rows = slice(cb * CHUNK, (cb + 1) * CHUNK)
        for h in range(nh):
            wide.append((rows, slice(h * GDN_HEAD_DIM, (h + 1) * GDN_HEAD_DIM)))
            narrow.append((rows, slice(h * CHUNK, (h + 1) * CHUNK)))
    return wide, narrow


def _gdn_intra_fwd(q, k, v, gb, bb, per_step=4):
    s, w = q.shape
    nh, nc, cpb, n, wide, narrow = _gdn_blocks(s, w, per_step)
    hd = GDN_HEAD_DIM

    def body(q_ref, k_ref, v_ref, g_ref, b_ref, wd_ref, uv_ref, qk_ref, qd_ref, kd_ref):
        wide, narrow = _gdn_pairs(cpb, nh)
        res = _gdn_intra(*[[r[ix] for ix in wide] for r in (q_ref, k_ref, v_ref, g_ref, b_ref)])
        for ref, vals, where in zip((wd_ref, uv_ref, qk_ref, qd_ref, kd_ref), res, (wide, wide, narrow, wide, wide)):
            for ix, val in zip(where, vals):
                ref[ix] = val

    bw = pl.BlockSpec(wide, lambda i: (i, 0))
    bn = pl.BlockSpec(narrow, lambda i: (i, 0))
    fw = jax.ShapeDtypeStruct((s, w), F32)
    return pl.pallas_call(
        body, name="gdn_intra", grid=(n,), in_specs=[bw] * 5, out_specs=[bw, bw, bn, bw, bw],
        out_shape=[fw, fw, jax.ShapeDtypeStruct((s, nh * CHUNK), F32), fw, fw],
        compiler_params=_cparams(("parallel",)))(q, k, v, gb, bb)


def _gdn_intra_bwd(q, k, v, gb, bb, cts, dgb_inter, per_step=2):
    s, w = q.shape
    nh, nc, cpb, n, wide, narrow = _gdn_blocks(s, w, per_step)
    hd = GDN_HEAD_DIM

    def body(q_ref, k_ref, v_ref, g_ref, b_ref, cwd, cuv, cqk, cqd, ckd, dgi, dq_ref, dk_ref, dv_ref, dg_ref, db_ref):
        wide, narrow = _gdn_pairs(cpb, nh)
        _, vjp = jax.vjp(_gdn_intra, *[[r[ix] for ix in wide] for r in (q_ref, k_ref, v_ref, g_ref, b_ref)])
        cts = tuple([r[ix] for ix in where] for r, where in zip((cwd, cuv, cqk, cqd, ckd),
                                                               (wide, wide, narrow, wide, wide)))
        dq, dk, dv, dg, db = vjp(cts)
        for j, ix in enumerate(wide):
            dq_ref[ix], dk_ref[ix], dv_ref[ix], db_ref[ix] = dq[j], dk[j], dv[j], db[j]
            dg_ref[ix] = dg[j] + dgi[ix]

    bw = pl.BlockSpec(wide, lambda i: (i, 0))
    bn = pl.BlockSpec(narrow, lambda i: (i, 0))
    return pl.pallas_call(
        body, name="gdn_intra_bwd", grid=(n,), in_specs=[bw] * 5 + [bw, bw, bn, bw, bw, bw], out_specs=[bw] * 5,
        out_shape=[jax.ShapeDtypeStruct((s, w), F32)] * 5,
        compiler_params=_cparams(("parallel",)))(q, k, v, gb, bb, *cts, dgb_inter)


def _gdn_inter_fwd(wd, uv, qk, qd, kd, gb, per_step=4):
    s, w = wd.shape
    nh, nc, cpb, n, wide, narrow = _gdn_blocks(s, w, per_step)
    hd = GDN_HEAD_DIM

    def body(wd_ref, uv_ref, qk_ref, qd_ref, kd_ref, g_ref, o_ref, st_ref, state):
        @pl.when(pl.program_id(0) == 0)
        def _():
            state[...] = jnp.zeros_like(state)

        wide, narrow = _gdn_pairs(cpb, nh)
        st = [state[h] for h in range(nh)]
        for cb in range(cpb):
            wi, na = wide[cb * nh:(cb + 1) * nh], narrow[cb * nh:(cb + 1) * nh]
            for h in range(nh):
                st_ref[cb, h] = st[h]
            o, st = _gdn_inter([wd_ref[ix] for ix in wi], [uv_ref[ix] for ix in wi], [qk_ref[ix] for ix in na],
                               [qd_ref[ix] for ix in wi], [kd_ref[ix] for ix in wi], [g_ref[ix] for ix in wi], st)
            for h in range(nh):
                o_ref[wi[h]] = o[h]
        for h in range(nh):
            state[h] = st[h]

    bw = pl.BlockSpec(wide, lambda i: (i, 0))
    bn = pl.BlockSpec(narrow, lambda i: (i, 0))
    return pl.pallas_call(
        body, name="gdn_inter", grid=(n,), in_specs=[bw, bw, bn, bw, bw, bw],
        out_specs=[bw, pl.BlockSpec((cpb, nh, hd, hd), lambda i: (i, 0, 0, 0))],
        out_shape=[jax.ShapeDtypeStruct((s, w), F32), jax.ShapeDtypeStruct((nc, nh, hd, hd), F32)],
        scratch_shapes=[pltpu.VMEM((nh, hd, hd), F32)],
        compiler_params=_cparams(("arbitrary",)))(wd, uv, qk, qd, kd, gb)


def _gdn_inter_bwd(wd, uv, qk, qd, kd, gb, states, do, per_step=4):
    s, w = wd.shape
    nh, nc, cpb, n, wide, narrow = _gdn_blocks(s, w, per_step)
    hd = GDN_HEAD_DIM

    def body(wd_ref, uv_ref, qk_ref, qd_ref, kd_ref, g_ref, st_ref, do_ref, cwd, cuv, cqk, cqd, ckd, dg_ref, dstate):
        @pl.when(pl.program_id(0) == 0)
        def _():
            dstate[...] = jnp.zeros_like(dstate)

        wide, narrow = _gdn_pairs(cpb, nh)
        dst = [dstate[h] for h in range(nh)]
        for cb in reversed(range(cpb)):
            wi, na = wide[cb * nh:(cb + 1) * nh], narrow[cb * nh:(cb + 1) * nh]
            _, vjp = jax.vjp(_gdn_inter, [wd_ref[ix] for ix in wi], [uv_ref[ix] for ix in wi],
                             [qk_ref[ix] for ix in na], [qd_ref[ix] for ix in wi], [kd_ref[ix] for ix in wi],
                             [g_ref[ix] for ix in wi], [st_ref[cb, h] for h in range(nh)])
            dwd, duv, dqk, dqd, dkd, dg, dst = vjp(([do_ref[ix] for ix in wi], dst))
            for h in range(nh):
                cwd[wi[h]], cuv[wi[h]], cqk[na[h]], cqd[wi[h]], ckd[wi[h]], dg_ref[wi[h]] = (
                    dwd[h], duv[h], dqk[h], dqd[h], dkd[h], dg[h])
        for h in range(nh):
            dstate[h] = dst[h]

    bw = pl.BlockSpec(wide, lambda i: (n - 1 - i, 0))
    bn = pl.BlockSpec(narrow, lambda i: (n - 1 - i, 0))
    fw = jax.ShapeDtypeStruct((s, w), F32)
    return pl.pallas_call(
        body, name="gdn_inter_bwd", grid=(n,),
        in_specs=[bw, bw, bn, bw, bw, bw, pl.BlockSpec((cpb, nh, hd, hd), lambda i: (n - 1 - i, 0, 0, 0)), bw],
        out_specs=[bw, bw, bn, bw, bw, bw],
        out_shape=[fw, fw, jax.ShapeDtypeStruct((s, nh * CHUNK), F32), fw, fw, fw],
        scratch_shapes=[pltpu.VMEM((nh, hd, hd), F32)],
        compiler_params=_cparams(("arbitrary",)))(wd, uv, qk, qd, kd, gb, states, do)


def _s5_coef(ar, ai):
    nl = ar.shape[1]

    def body(ar_ref, ai_ref, o_ref):
        row = lax.broadcasted_iota(jnp.int32, (8, nl), 0)
        for base, sign in ((0, 1.0), (8, -1.0)):
            pr = [jnp.broadcast_to(ar_ref[...], (8, nl))]
            pi = [jnp.broadcast_to(ai_ref[...], (8, nl)) * sign]
            for _ in range(7):
                pr.append(pr[-1] * pr[0] - pi[-1] * pi[0])
                pi.append(pr[-2] * pi[0] + pi[-1] * pr[0])
            for j, d in enumerate((1, 2, 4)):
                m = (row >= d) if base == 0 else (row <= 7 - d)
                o_ref[base + 2 * j] = jnp.where(m, pr[d - 1], 0.0)
                o_ref[base + 2 * j + 1] = jnp.where(m, pi[d - 1], 0.0)
            cr, ci = jnp.zeros((8, nl), F32), jnp.zeros((8, nl), F32)
            for t in range(8):
                e = t if base == 0 else 7 - t
                cr = jnp.where(row == t, pr[e], cr)
                ci = jnp.where(row == t, pi[e], ci)
            o_ref[base + 6] = cr
            o_ref[base + 7] = ci

    return pl.pallas_call(body, name="s5_coef", out_shape=jax.ShapeDtypeStruct((16, 8, nl), F32),
                          compiler_params=_cparams())(ar, ai)


def _scan_tile(src_re, src_im, dst_re, dst_im, coef_ref, carry_re, carry_im, ts, reverse, extra=None):
    nl = src_re.shape[1]
    base = 8 if reverse else 0
    ng = ts // 8
    for lc in range(nl // SCAN_LANES):
        ln = slice(lc * SCAN_LANES, (lc + 1) * SCAN_LANES)
        m = [coef_ref[base + j, :, ln] for j in range(8)]
        row = lax.broadcasted_iota(jnp.int32, (8, SCAN_LANES), 0)

        def step(r, carry, ln=ln, m=m, row=row):
            grp = (ng - 1 - r) if reverse else r
            rows = pl.ds(pl.multiple_of(grp * 8, 8), 8)
            xr, xi = src_re[rows, ln], src_im[rows, ln]
            for j, d in enumerate((1, 2, 4)):
                sh = 8 - d if reverse else d
                sr, si = pltpu.roll(xr, sh, 0), pltpu.roll(xi, sh, 0)
                mr, mi = m[2 * j], m[2 * j + 1]
                xr, xi = xr + mr * sr - mi * si, xi + mr * si + mi * sr
            cr, ci = carry[0], carry[1]
            hr = xr + m[6] * cr - m[7] * ci
            hi = xi + m[6] * ci + m[7] * cr
            dst_re[rows, ln] = hr
            dst_im[rows, ln] = hi
            edge = 0 if reverse else 7
            out = (jnp.broadcast_to(hr[edge:edge + 1, :], hr.shape), jnp.broadcast_to(hi[edge:edge + 1, :], hi.shape))
            if extra is not None:
                h_re, h_im, halo_re, halo_im, first, _, _ = extra
                prev = pl.ds(pl.multiple_of(jnp.maximum(grp - 1, 0) * 8, 8), 8)
                use_halo = grp == 0
                pr = jnp.where(use_halo, halo_re[:, ln] * first, h_re[prev, ln])
                pi = jnp.where(use_halo, halo_im[:, ln] * first, h_im[prev, ln])
                qr = jnp.where(row == 0, jnp.broadcast_to(pr[7:8, :], pr.shape), pltpu.roll(h_re[rows, ln], 1, 0))
                qi = jnp.where(row == 0, jnp.broadcast_to(pi[7:8, :], pi.shape), pltpu.roll(h_im[rows, ln], 1, 0))
                out = out + (carry[2] + hr * qr + hi * qi, carry[3] + hi * qr - hr * qi)
            return out

        init = (carry_re[:, ln], carry_im[:, ln])
        if extra is not None:
            init = init + (extra[5][:, ln], extra[6][:, ln])
        fin = lax.fori_loop(0, ng, step, init)
        carry_re[:, ln] = fin[0]
        carry_im[:, ln] = fin[1]
        if extra is not None:
            extra[5][:, ln] = fin[2]
            extra[6][:, ln] = fin[3]


def _s5_fwd(xb, bb_re, bb_im, c_re, c_im, coef, ts):
    s, w = xb.shape
    nb = bb_re.shape[0]
    nl = nb * 512

    def body(x_ref, bre_ref, bim_ref, cre_ref, cim_ref, coef_ref, hre_ref, him_ref, y_ref, ure, uim, car_re, car_im):
        @pl.when(pl.program_id(0) == 0)
        def _():
            car_re[...] = jnp.zeros_like(car_re)
            car_im[...] = jnp.zeros_like(car_im)

        for b in range(nb):
            xs = x_ref[:, b * 128:(b + 1) * 128].astype(BF16)
            ure[:, b * 512:(b + 1) * 512] = jnp.dot(xs, bre_ref[b], preferred_element_type=F32)
            uim[:, b * 512:(b + 1) * 512] = jnp.dot(xs, bim_ref[b], preferred_element_type=F32)
        _scan_tile(ure, uim, hre_ref, him_ref, coef_ref, car_re, car_im, ts, False)
        for b in range(nb):
            hr = hre_ref[:, b * 512:(b + 1) * 512].astype(BF16)
            hi = him_ref[:, b * 512:(b + 1) * 512].astype(BF16)
            y_ref[:, b * 128:(b + 1) * 128] = (jnp.dot(hr, cre_ref[b], preferred_element_type=F32)
                                               - jnp.dot(hi, cim_ref[b], preferred_element_type=F32))

    row = lambda wd: pl.BlockSpec((ts, wd), lambda i: (i, 0))
    return pl.pallas_call(
        body, name="s5_fwd", grid=(s // ts,),
        in_specs=[row(w), _whole(bb_re), _whole(bb_im), _whole(c_re), _whole(c_im), _whole(coef)],
        out_specs=[row(nl), row(nl), row(w)],
        out_shape=[jax.ShapeDtypeStruct((s, nl), F32), jax.ShapeDtypeStruct((s, nl), F32),
                   jax.ShapeDtypeStruct((s, w), F32)],
        scratch_shapes=[pltpu.VMEM((ts, nl), F32), pltpu.VMEM((ts, nl), F32), pltpu.VMEM((8, nl), F32),
                        pltpu.VMEM((8, nl), F32)],
        compiler_params=_cparams(("arbitrary",)))(xb, bb_re, bb_im, c_re, c_im, coef)


def _s5_bwd(dy, xb, h_re, h_im, bb_re, bb_im, c_re, c_im, coef, ts):
    s, w = xb.shape
    nb = bb_re.shape[0]
    nl = nb * 512
    nt = s // ts

    def body(dy_ref, x_ref, hre_ref, him_ref, halo_re, halo_im, bre_ref, bim_ref, cre_ref, cim_ref, coef_ref,
             dx_ref, dbre_ref, dbim_ref, dcre_ref, dcim_ref, dare_ref, daim_ref, gre, gim, car_re, car_im):
        i = pl.program_id(0)

        @pl.when(i == 0)
        def _():
            for r in (car_re, car_im, dbre_ref, dbim_ref, dcre_ref, dcim_ref, dare_ref, daim_ref):
                r[...] = jnp.zeros_like(r)

        for b in range(nb):
            dyb = dy_ref[:, b * 128:(b + 1) * 128].astype(BF16)
            gre[:, b * 512:(b + 1) * 512] = lax.dot_general(dyb, cre_ref[b], (((1,), (1,)), ((), ())),
                                                            preferred_element_type=F32)
            gim[:, b * 512:(b + 1) * 512] = -lax.dot_general(dyb, cim_ref[b], (((1,), (1,)), ((), ())),
                                                             preferred_element_type=F32)
            hr = hre_ref[:, b * 512:(b + 1) * 512].astype(BF16)
            hi = him_ref[:, b * 512:(b + 1) * 512].astype(BF16)
            dcre_ref[b] += lax.dot_general(hr, dyb, (((0,), (0,)), ((), ())), preferred_element_type=F32)
            dcim_ref[b] -= lax.dot_general(hi, dyb, (((0,), (0,)), ((), ())), preferred_element_type=F32)
        first = (i != nt - 1).astype(F32)
        _scan_tile(gre, gim, gre, gim, coef_ref, car_re, car_im, ts, True,
                   extra=(hre_ref, him_ref, halo_re, halo_im, first, dare_ref, daim_ref))
        for b in range(nb):
            gr = gre[:, b * 512:(b + 1) * 512].astype(BF16)
            gi = gim[:, b * 512:(b + 1) * 512].astype(BF16)
            xs = x_ref[:, b * 128:(b + 1) * 128].astype(BF16)
            dx_ref[:, b * 128:(b + 1) * 128] = (
                lax.dot_general(gr, bre_ref[b], (((1,), (1,)), ((), ())), preferred_element_type=F32)
                + lax.dot_general(gi, bim_ref[b], (((1,), (1,)), ((), ())), preferred_element_type=F32))
            dbre_ref[b] += lax.dot_general(xs, gr, (((0,), (0,)), ((), ())), preferred_element_type=F32)
            dbim_ref[b] += lax.dot_general(xs, gi, (((0,), (0,)), ((), ())), preferred_element_type=F32)

    row = lambda wd: pl.BlockSpec((ts, wd), lambda i: (nt - 1 - i, 0))
    halo = pl.BlockSpec((8, nl), lambda i: (jnp.maximum((nt - 1 - i) * (ts // 8) - 1, 0), 0))
    return pl.pallas_call(
        body, name="s5_bwd", grid=(nt,),
        in_specs=[row(w), row(w), row(nl), row(nl), halo, halo, _whole(bb_re), _whole(bb_im), _whole(c_re),
                  _whole(c_im), _whole(coef)],
        out_specs=[row(w), _whole(bb_re), _whole(bb_im), _whole(c_re), _whole(c_im),
                   pl.BlockSpec((8, nl), lambda i: (0, 0)), pl.BlockSpec((8, nl), lambda i: (0, 0))],
        out_shape=[jax.ShapeDtypeStruct((s, w), F32), jax.ShapeDtypeStruct(bb_re.shape, F32),
                   jax.ShapeDtypeStruct(bb_im.shape, F32), jax.ShapeDtypeStruct(c_re.shape, F32),
                   jax.ShapeDtypeStruct(c_im.shape, F32), jax.ShapeDtypeStruct((8, nl), F32),
                   jax.ShapeDtypeStruct((8, nl), F32)],
        scratch_shapes=[pltpu.VMEM((ts, nl), F32), pltpu.VMEM((ts, nl), F32), pltpu.VMEM((8, nl), F32),
                        pltpu.VMEM((8, nl), F32)],
        compiler_params=_cparams(("arbitrary",)))(dy, xb, h_re, h_im, h_re, h_im, bb_re, bb_im, c_re, c_im, coef)


def _final(x, mo, target, fg, ts):
    s, d = x.shape

    def f(x, mo, fg, tgt):
        y = _rms(x + mo, fg)[0]
        err = y - tgt
        return 0.5 * jnp.sum(jnp.mean(err * err, axis=-1, keepdims=True), axis=0, keepdims=True)

    def body(x_ref, mo_ref, t_ref, fg_ref, dh_ref, dfg_ref, loss_ref):
        @pl.when(pl.program_id(0) == 0)
        def _():
            dfg_ref[...] = jnp.zeros_like(dfg_ref)
            loss_ref[...] = jnp.zeros_like(loss_ref)

        loss, vjp = jax.vjp(f, x_ref[...], mo_ref[...], fg_ref[...], t_ref[...])
        _, dmo, dfg, _ = vjp(jnp.ones((1, 1), F32))
        dh_ref[...] = dmo
        dfg_ref[...] += dfg
        loss_ref[...] += jnp.broadcast_to(loss, loss_ref.shape)

    row = pl.BlockSpec((ts, d), lambda i: (i, 0))
    return pl.pallas_call(
        body, name="final", grid=(s // ts,), in_specs=[row, row, row, _whole(fg)],
        out_specs=[row, _whole(fg), pl.BlockSpec((8, 128), lambda i: (0, 0))],
        out_shape=[jax.ShapeDtypeStruct((s, d), F32), jax.ShapeDtypeStruct(fg.shape, F32),
                   jax.ShapeDtypeStruct((8, 128), F32)],
        compiler_params=_cparams(("arbitrary",)))(x, mo, target, fg)


N_CHIPS = 4


def _other_chips(x, y):
    return [((1 - x, y), 2 * (1 - x) + y), ((x, 1 - y), 2 * x + 1 - y), ((1 - x, 1 - y), 2 * (1 - x) + 1 - y)]


def _comm_call(body, name, srcs, out_shapes, n_sems):
    n = len(srcs)
    return pl.pallas_call(
        body, name=name, in_specs=[pl.BlockSpec(memory_space=pl.ANY)] * n,
        out_specs=[pl.BlockSpec(memory_space=pl.ANY)] * n, out_shape=out_shapes,
        scratch_shapes=[pltpu.SemaphoreType.DMA((n, n_sems)), pltpu.SemaphoreType.DMA((n, n_sems)),
                        pltpu.SemaphoreType.DMA((n,))],
        compiler_params=pltpu.CompilerParams(has_side_effects=True))(*srcs)


def _gather(srcs, name):
    n = len(srcs)

    def body(*refs):
        src, out = refs[:n], refs[n:2 * n]
        send_sems, recv_sems, local_sems = refs[2 * n:]
        x, y, c = lax.axis_index("x"), lax.axis_index("y"), lax.axis_index("c")
        me, sib_slot, sib = 4 * x + 2 * y + c, 4 * x + 2 * y + 1 - c, (x, y, 1 - c)
        chips = _other_chips(x, y)

        def cp(a, k, src_ref, slot, to):
            return pltpu.make_async_remote_copy(
                src_ref=src_ref, dst_ref=out[a].at[slot], send_sem=send_sems.at[a, k], recv_sem=recv_sems.at[a, k],
                device_id=to, device_id_type=pl.DeviceIdType.MESH)

        local = [pltpu.make_async_copy(src[a], out[a].at[me], local_sems.at[a]) for a in range(n)]
        first = [cp(a, 0, src[a], me, sib) for a in range(n)]
        first += [cp(a, 1 + j, src[a], me, (*chip, c)) for j, (chip, _) in enumerate(chips) for a in range(n)]
        for d in local + first:
            d.start()
        passed = []
        for j, (chip, q) in enumerate(chips):
            for a in range(n):
                cp(a, 1 + j, src[a], 2 * q + c, sib).wait_recv()
                fwd = cp(a, 4 + j, out[a].at[2 * q + c], 2 * q + c, sib)
                fwd.start()
                passed.append(fwd)
        for a in range(n):
            cp(a, 0, src[a], sib_slot, sib).wait_recv()
        for j, (chip, q) in enumerate(chips):
            for a in range(n):
                cp(a, 4 + j, src[a], 2 * q + 1 - c, sib).wait_recv()
        for d in first + passed:
            d.wait_send()
        for d in local:
            d.wait()

    return _comm_call(body, name, srcs, [jax.ShapeDtypeStruct((N_DEV,) + s.shape, s.dtype) for s in srcs], 7)


def _pair_scatter(gs, name):
    n = len(gs)

    def body(*refs):
        src, out = refs[:n], refs[n:2 * n]
        send_sems, recv_sems, _ = refs[2 * n:]
        x, y, c = lax.axis_index("x"), lax.axis_index("y"), lax.axis_index("c")
        sends = []
        for q in range(N_CHIPS):
            for a in range(n):
                d = pltpu.make_async_remote_copy(
                    src_ref=src[a].at[2 * q + 1 - c], dst_ref=out[a].at[q], send_sem=send_sems.at[a, q],
                    recv_sem=recv_sems.at[a, q], device_id=(x, y, 1 - c), device_id_type=pl.DeviceIdType.MESH)
                d.start()
                sends.append(d)
        for d in sends:
            d.wait_recv()
        for d in sends:
            d.wait_send()

    return _comm_call(body, name, gs, [jax.ShapeDtypeStruct((N_CHIPS,) + g.shape[1:], g.dtype) for g in gs], N_CHIPS)


def _cross_scatter(ps, name):
    n = len(ps)

    def body(*refs):
        src, out = refs[:n], refs[n:2 * n]
        send_sems, recv_sems, local_sems = refs[2 * n:]
        x, y, c = lax.axis_index("x"), lax.axis_index("y"), lax.axis_index("c")
        mine = 2 * x + y
        chips = _other_chips(x, y)
        local = [pltpu.make_async_copy(src[a].at[mine], out[a].at[mine], local_sems.at[a]) for a in range(n)]
        for d in local:
            d.start()
        sends = []
        for j, (chip, q) in enumerate(chips):
            for a in range(n):
                d = pltpu.make_async_remote_copy(
                    src_ref=src[a].at[q], dst_ref=out[a].at[mine], send_sem=send_sems.at[a, j],
                    recv_sem=recv_sems.at[a, j], device_id=(*chip, c), device_id_type=pl.DeviceIdType.MESH)
                d.start()
                sends.append(d)
        for j, (chip, q) in enumerate(chips):
            for a in range(n):
                pltpu.make_async_remote_copy(
                    src_ref=src[a].at[q], dst_ref=out[a].at[q], send_sem=send_sems.at[a, j],
                    recv_sem=recv_sems.at[a, j], device_id=(*chip, c), device_id_type=pl.DeviceIdType.MESH).wait_recv()
        for d in sends:
            d.wait_send()
        for d in local:
            d.wait()

    return _comm_call(body, name, ps, [jax.ShapeDtypeStruct(p.shape, p.dtype) for p in ps], 3)


def _pair_sum(g, got, c_idx, out_dtype, name):
    _, r, c = g.shape
    lanes = -(-c // 128) * 128
    tr = _pick_rows(r, max(8, (2 * 1024 * 1024) // (lanes * 4)))
    g4 = g.reshape((N_CHIPS, 2) + g.shape[1:])

    def body(c_ref, g_ref, got_ref, o_ref):
        o_ref[...] = (g_ref[...] + got_ref[...]).astype(o_ref.dtype)

    return pl.pallas_call(
        body, name=name,
        grid_spec=pltpu.PrefetchScalarGridSpec(
            num_scalar_prefetch=1, grid=(N_CHIPS, r // tr),
            in_specs=[pl.BlockSpec((None, None, tr, c), lambda q, i, cr: (q, cr[0], i, 0)),
                      pl.BlockSpec((None, tr, c), lambda q, i, cr: (q, i, 0))],
            out_specs=pl.BlockSpec((None, tr, c), lambda q, i, cr: (q, i, 0))),
        out_shape=jax.ShapeDtypeStruct(got.shape, out_dtype),
        compiler_params=_cparams(("parallel", "parallel")))(c_idx, g4, got)


def _pick_rows(r, pref):
    t = (min(pref, r) // 8) * 8
    while t >= 8:
        if r % t == 0:
            return t
        t -= 8
    return r


def _w_in_from_shards(t, lo, hi):
    n, r, cs = t.shape
    tr = _pick_rows(r, 256)
    wm = n * cs - (hi - lo)

    def body(t_ref, m_ref, b_ref):
        full = jnp.concatenate([t_ref[j] for j in range(n)], axis=1)
        m_ref[...] = jnp.concatenate([full[:, :lo], full[:, hi:]], axis=1)
        b_ref[...] = jnp.concatenate([full[:, lo:hi], jnp.zeros((tr, 128 - (hi - lo)), full.dtype)], axis=1)

    return pl.pallas_call(
        body, name="w_in_layout", grid=(r // tr,), in_specs=[pl.BlockSpec((n, tr, cs), lambda i: (0, i, 0))],
        out_specs=[pl.BlockSpec((tr, wm), lambda i: (i, 0)), pl.BlockSpec((tr, 128), lambda i: (i, 0))],
        out_shape=[jax.ShapeDtypeStruct((r, wm), t.dtype), jax.ShapeDtypeStruct((r, 128), t.dtype)],
        compiler_params=_cparams(("parallel",)))(t)


def _w_in_to_shards(gm, gb, lo, hi):
    r, wm = gm.shape
    cs = (wm + hi - lo) // N_DEV
    tr = _pick_rows(r, 64)

    def body(m_ref, b_ref, o_ref):
        m = m_ref[...]
        full = jnp.concatenate([m[:, :lo], b_ref[:, :hi - lo], m[:, lo:]], axis=1)
        for j in range(N_DEV):
            o_ref[j] = full[:, j * cs:(j + 1) * cs]

    return pl.pallas_call(
        body, name="dw_in_layout", grid=(r // tr,),
        in_specs=[pl.BlockSpec((tr, wm), lambda i: (i, 0)), pl.BlockSpec((tr, 128), lambda i: (i, 0))],
        out_specs=pl.BlockSpec((N_DEV, tr, cs), lambda i: (0, i, 0)),
        out_shape=jax.ShapeDtypeStruct((N_DEV, r, cs), gm.dtype), compiler_params=_cparams(("parallel",)))(gm, gb)


def _pack(arrs, dtype, lead=()):
    nlead = len(lead)
    flat = jnp.concatenate([a.astype(dtype).reshape(lead + (-1,)) for a in arrs], axis=nlead)
    n = flat.shape[-1]
    unit = PACK_WIDTH * PACK_ROWS
    pad = (-n) % unit
    flat = jnp.pad(flat, [(0, 0)] * nlead + [(0, pad)])
    return flat.reshape(lead + ((n + pad) // PACK_WIDTH, PACK_WIDTH))


def _unpack(buf, shapes, lead=()):
    flat = buf.reshape(lead + (-1,))
    out, off = [], 0
    for shp in shapes:
        n = math.prod(shp)
        out.append(flat[..., off:off + n].reshape(lead + tuple(shp)))
        off += n
    return out


def _adam_math(w, g, m, v):
    m = ADAM_B1 * m + (1.0 - ADAM_B1) * g
    v = ADAM_B2 * v + (1.0 - ADAM_B2) * (g * g)
    m_hat = m / (1.0 - ADAM_B1 ** ADAM_STEP)
    v_hat = v / (1.0 - ADAM_B2 ** ADAM_STEP)
    delta = -ADAM_LR * (m_hat / (jnp.sqrt(v_hat) + ADAM_EPS) + ADAM_WD * w)
    return delta, m, v


def _sum_adam(parts, w, m, v, name):
    r, c = w.shape
    nparts = parts.shape[0]
    lanes = -(-c // 128) * 128
    tr = _pick_rows(r, max(8, (6 * 1024 * 1024) // (nparts * lanes * 4)))

    def body(p_ref, w_ref, m_ref, v_ref, g_ref, d_ref, nm_ref, nv_ref):
        g = p_ref[0].astype(F32)
        for j in range(1, nparts):
            g = g + p_ref[j].astype(F32)
        d, nm, nv = _adam_math(w_ref[...], g, m_ref[...], v_ref[...])
        g_ref[...] = g
        d_ref[...] = d
        nm_ref[...] = nm
        nv_ref[...] = nv

    row = pl.BlockSpec((tr, c), lambda i: (i, 0))
    return pl.pallas_call(
        body, name=name, grid=(r // tr,), in_specs=[pl.BlockSpec((nparts, tr, c), lambda i: (0, i, 0)), row, row, row],
        out_specs=[row] * 4, out_shape=[jax.ShapeDtypeStruct((r, c), F32)] * 4,
        compiler_params=_cparams(("parallel",)))(parts, w, m, v)


def _block_diag(t):
    nb, g, a, b = t.shape
    eye = jnp.eye(g, dtype=t.dtype)
    return jnp.einsum('ngab,gh->ngahb', t, eye).reshape(nb, g * a, g * b)


def _diag_blocks(t, a, b):
    nb = t.shape[0]
    g = S5_GROUPS_PER_BLOCK
    t = t.reshape(nb, g, a, g, b)
    return jnp.stack([t[:, j, :, j, :] for j in range(g)], axis=1)


def _local_step(x, mem, target, p):
    s, d = x.shape
    gw = d // 2
    nh = gw // GDN_HEAD_DIM
    ng = gw // S5_GROUP
    nb = ng // S5_GROUPS_PER_BLOCK
    nl = ng * S5_STATE
    ts = min(256, s)
    nt = s // ts
    grads = {}

    w_main, w_ba = p['w_main'], p['w_ba']
    CB_ZA, CB_XB, CB_ZB, CB_QC, CB_ZC, CB_G = 3, 4, 5, 6, 7, 8

    u = _tile_fwd(_rms, "rms_fwd", nt, [_rt(x, ts)], [p['norm_g']],
                  [((s, d), BF16, (ts, d), lambda i: (i, 0))])[0]
    proj = _mm(u, w_main, tm=1024, tn=2048, tk=512, name="proj_main")
    pba = _mm(u, w_ba, name="proj_ba")

    conv_w = p['conv_w']
    col = lambda arr, cb: (arr, (s, GDN_HEAD_DIM), lambda i, cb=cb: (0, cb + i))
    qkv = []
    for j, mode in enumerate(('q', 'k', 'v')):
        off = j * nh
        qkv.append(_tile_fwd(
            _gdn_pre(mode), "gdn_pre_" + mode, nh, [col(proj, off), (conv_w, (CONV_WIDTH, GDN_HEAD_DIM), lambda i, off=off: (0, off + i))],
            [], [((s, gw), F32, (s, GDN_HEAD_DIM), lambda i: (0, i))])[0])
    q, k, v = qkv
    lane = jnp.arange(128)[:, None]
    colh = jnp.arange(gw)[None, :] // GDN_HEAD_DIM
    e_beta = (lane == colh).astype(F32)
    e_g = (lane == colh + nh).astype(F32)
    alog_row = jnp.pad(p['gdn_a_log'], ((0, 0), (nh, 128 - 2 * nh)))
    dtb_row = jnp.pad(p['gdn_dt_bias'], ((0, 0), (nh, 128 - 2 * nh)))
    row_gw = lambda: ((s, gw), F32, (ts, gw), lambda i: (i, 0))
    betab, gb = _tile_fwd(_gdn_gates, "gdn_gates", nt, [_rt(pba, ts)], [alog_row, dtb_row, e_beta, e_g],
                          [row_gw(), row_gw()])
    intra = _gdn_intra_fwd(q, k, v, gb, betab)
    o_raw, states = _gdn_inter_fwd(*intra, gb)
    ga = _tile_fwd(_gdn_post, "gdn_post", nt, [_rt(o_raw, ts), _rt(proj, ts, CB_ZA, gw)], [p['gdn_norm_g']],
                   [((s, gw), BF16, (ts, gw), lambda i: (i, 0))])[0]

    e_rep = (jnp.arange(S5_STATE)[:, None] == jnp.arange(S5_STATE * S5_GROUP)[None, :] // S5_GROUP).astype(F32)
    s5_in = [p['s5_lambda_re'], p['s5_lambda_im'], p['s5_log_dt'].reshape(ng, 1),
             p['s5_b_re'].reshape(ng, S5_STATE * S5_GROUP), p['s5_b_im'].reshape(ng, S5_STATE * S5_GROUP), e_rep]
    one = lambda shp: (shp, F32, shp, lambda i, n=len(shp): (0,) * n)
    ab_re, ab_im, bbr, bbi = _tile_fwd(_s5_params, "s5_params", 1, [], s5_in,
                                       [one((ng, S5_STATE)), one((ng, S5_STATE)), one((ng, S5_STATE * S5_GROUP)),
                                        one((ng, S5_STATE * S5_GROUP))])
    coef = _s5_coef(ab_re.reshape(1, nl), ab_im.reshape(1, nl))
    to_bd_b = lambda t: _block_diag(t.reshape(nb, S5_GROUPS_PER_BLOCK, S5_STATE, S5_GROUP).transpose(0, 1, 3, 2))
    to_bd_c = lambda t: _block_diag(t.reshape(nb, S5_GROUPS_PER_BLOCK, S5_GROUP, S5_STATE).transpose(0, 1, 3, 2))
    bbd_re, bbd_im = to_bd_b(bbr).astype(BF16), to_bd_b(bbi).astype(BF16)
    cbd_re, cbd_im = to_bd_c(p['s5_c_re']).astype(BF16), to_bd_c(p['s5_c_im']).astype(BF16)
    xb_arr = lax.slice_in_dim(proj, CB_XB * gw, (CB_XB + 1) * gw, axis=1)
    h_re, h_im, ylin = _s5_fwd(xb_arr, bbd_re, bbd_im, cbd_re, cbd_im, coef, ts)
    gl = _tile_fwd(_s5_post1, "s5_post1", nt, [_rt(ylin, ts), _rt(proj, ts, CB_XB, gw)], [p['s5_d']],
                   [((s, gw), BF16, (ts, gw), lambda i: (i, 0))])[0]
    tglu = _mm(gl, p['s5_w_glu'], b_shards=True, name="s5_glu")
    gbb = _tile_fwd(_s5_post2, "s5_post2", nt, [_rt(tglu, ts), _rt(proj, ts, CB_ZB, gw)], [],
                    [((s, gw), BF16, (ts, gw), lambda i: (i, 0))])[0]

    m_len = mem.shape[0]
    mem_n = _tile_fwd(_rms, "mem_rms", 1, [_rt(mem, m_len)], [p['mem_norm_g']],
                      [((m_len, d), BF16, (m_len, d), lambda i: (i, 0))])[0]
    kv = _mm(mem_n, p['w_kv_mem'], name="mem_kv")
    gcc = _tile_fwd(_attn, "attn", nt, [_rt(proj, ts, CB_QC, gw), _rt(proj, ts, CB_ZC, gw)], [kv],
                    [((s, gw), BF16, (ts, gw), lambda i: (i, 0))])[0]

    p_a = _mm(ga, p['w_br_a'], b_shards=True, name="br_a")
    p_b = _mm(gbb, p['w_br_b'], b_shards=True, name="br_b")
    p_c = _mm(gcc, p['w_br_c'], b_shards=True, name="br_c")
    gate_acts = [_rt(proj, ts, CB_G // 2 + j, d) for j in range(3)]
    merged = _tile_fwd(_merge, "merge", nt, gate_acts + [_rt(p_a, ts), _rt(p_b, ts), _rt(p_c, ts)], [],
                       [((s, d), BF16, (ts, d), lambda i: (i, 0))])[0]
    mo = _mm(merged, p['w_out'], name="out_proj")
    dh, dfg, loss = _final(x, mo, target, p['final_g'].reshape(1, d), ts)
    grads['final_g'] = dfg.reshape(d)

    dmerged = _mm(dh, p['w_out'], tb=True, name="d_merged")
    grads['w_out'] = _mm(merged, dh, ta=True, name="dw_out")
    row_d = lambda dt: ((s, d), dt, (ts, d), lambda i: (i, 0))
    dg0, dg1, dg2, dpa, dpb, dpc = _tile_bwd(
        _merge, "merge_bwd", nt, gate_acts + [_rt(p_a, ts), _rt(p_b, ts), _rt(p_c, ts)], [], [_rt(dmerged, ts)],
        [row_d(BF16)] * 6, [])
    dga = _mm(dpa, p['w_br_a'], tb=True, b_shards=True, name="d_ga")
    dgbb = _mm(dpb, p['w_br_b'], tb=True, b_shards=True, name="d_gb")
    dgcc = _mm(dpc, p['w_br_c'], tb=True, b_shards=True, name="d_gc")
    grads['w_br_a'] = _mm(ga, dpa, ta=True, out_shards=N_DEV, name="dw_br_a")
    grads['w_br_b'] = _mm(gbb, dpb, ta=True, out_shards=N_DEV, name="dw_br_b")
    grads['w_br_c'] = _mm(gcc, dpc, ta=True, out_shards=N_DEV, name="dw_br_c")
    row_h = lambda dt: ((s, gw), dt, (ts, gw), lambda i: (i, 0))

    dqc, dzc, dkv = _tile_bwd(_attn, "attn_bwd", nt, [_rt(proj, ts, CB_QC, gw), _rt(proj, ts, CB_ZC, gw)], [kv],
                              [_rt(dgcc, ts)], [row_h(BF16), row_h(BF16)], [True])
    grads['w_kv_mem'] = _mm(mem_n, dkv, ta=True, name="dw_kv")
    dmem_n = _mm(dkv, p['w_kv_mem'], tb=True, name="d_mem_n")
    grads['mem_norm_g'] = _tile_bwd(_rms, "mem_rms_bwd", 1, [_rt(mem, m_len)], [p['mem_norm_g']],
                                    [_rt(dmem_n, m_len)], [None], [True])[0]

    dtglu, dzb = _tile_bwd(_s5_post2, "s5_post2_bwd", nt, [_rt(tglu, ts), _rt(proj, ts, CB_ZB, gw)], [],
                           [_rt(dgbb, ts)], [((s, 2 * gw), BF16, (ts, 2 * gw), lambda i: (i, 0)), row_h(BF16)], [])
    grads['s5_w_glu'] = _mm(gl, dtglu, ta=True, out_shards=N_DEV, name="dw_glu")
    dgl = _mm(dtglu, p['s5_w_glu'], tb=True, b_shards=True, name="d_gl")
    dylin, dxb1, dd = _tile_bwd(_s5_post1, "s5_post1_bwd", nt, [_rt(ylin, ts), _rt(proj, ts, CB_XB, gw)],
                                [p['s5_d']], [_rt(dgl, ts)], [row_h(F32), row_h(F32)], [True])
    grads['s5_d'] = dd
    dxb2, dbbd_re, dbbd_im, dcbd_re, dcbd_im, da_re, da_im = _s5_bwd(dylin, xb_arr, h_re, h_im, bbd_re, bbd_im,
                                                                    cbd_re, cbd_im, coef, ts)
    from_bd_b = lambda t: _diag_blocks(t, S5_GROUP, S5_STATE).transpose(0, 1, 3, 2).reshape(ng, S5_STATE * S5_GROUP)
    from_bd_c = lambda t: _diag_blocks(t, S5_STATE, S5_GROUP).transpose(0, 1, 3, 2).reshape(1, ng, S5_GROUP, S5_STATE)
    grads['s5_c_re'], grads['s5_c_im'] = from_bd_c(dcbd_re), from_bd_c(dcbd_im)
    s5_cts = [jnp.sum(da_re, axis=0).reshape(ng, S5_STATE), jnp.sum(da_im, axis=0).reshape(ng, S5_STATE),
              from_bd_b(dbbd_re), from_bd_b(dbbd_im)]
    dlr, dli, dlogdt, dbr, dbi = _tile_bwd(_s5_params, "s5_params_bwd", 1, [], s5_in,
                                           [(c, c.shape, lambda i: (0, 0)) for c in s5_cts], [],
                                           [True, True, True, True, True, False])
    grads['s5_lambda_re'], grads['s5_lambda_im'] = dlr[None], dli[None]
    grads['s5_log_dt'] = dlogdt.reshape(1, ng)
    grads['s5_b_re'] = dbr.reshape(1, ng, S5_STATE, S5_GROUP)
    grads['s5_b_im'] = dbi.reshape(1, ng, S5_STATE, S5_GROUP)
    dxb = (dxb1 + dxb2).astype(BF16)

    do_raw, dza, dgng = _tile_bwd(_gdn_post, "gdn_post_bwd", nt, [_rt(o_raw, ts), _rt(proj, ts, CB_ZA, gw)],
                                  [p['gdn_norm_g']], [_rt(dga, ts)], [row_h(F32), row_h(BF16)], [True])
    grads['gdn_norm_g'] = dgng
    *intra_cts, dgb_inter = _gdn_inter_bwd(*intra, gb, states, do_raw)
    dq, dk, dv, dgb, dbetab = _gdn_intra_bwd(q, k, v, gb, betab, intra_cts, dgb_inter)
    dpba, dalog, ddtb = _tile_bwd(_gdn_gates, "gdn_gates_bwd", nt, [_rt(pba, ts)], [alog_row, dtb_row, e_beta, e_g],
                                  [_rt(dbetab, ts), _rt(dgb, ts)], [((s, 128), BF16, (ts, 128), lambda i: (i, 0))],
                                  [True, True, False, False])
    grads['gdn_a_log'] = dalog[:, nh:2 * nh]
    grads['gdn_dt_bias'] = ddtb[:, nh:2 * nh]
    dqkv, dconv = [], []
    for j, (mode, ct) in enumerate((('q', dq), ('k', dk), ('v', dv))):
        off = j * nh
        wspec = (conv_w, (CONV_WIDTH, GDN_HEAD_DIM), lambda i, off=off: (0, off + i))
        dxc, dwc = _tile_bwd(
            _gdn_pre(mode), "gdn_pre_bwd_" + mode, nh, [col(proj, off), wspec], [], [col(ct, 0)],
            [((s, gw), BF16, (s, GDN_HEAD_DIM), lambda i: (0, i)),
             ((CONV_WIDTH, gw), F32, (CONV_WIDTH, GDN_HEAD_DIM), lambda i: (0, i))], [])
        dqkv.append(dxc)
        dconv.append(dwc)
    grads['conv_w'] = jnp.concatenate(dconv, axis=1)

    dproj = jnp.concatenate(dqkv + [dza, dxb, dzb, dqc, dzc, dg0, dg1, dg2], axis=1)
    du = _mm(dpba, w_ba, tb=True, name="du_ba")
    du = _mm(dproj, w_main, tb=True, addend=du, tm=512, tn=2048, tk=1024, name="du_main")
    grads['w_main'] = _mm(u, dproj, ta=True, tm=1024, tn=2048, tk=512, name="dw_main")
    grads['w_ba'] = _mm(u, dpba, ta=True, name="dw_ba")
    dx, dng = _tile_bwd(_rms, "rms_bwd", nt, [_rt(x, ts)], [p['norm_g']], [_rt(du, ts)], [row_d(F32)], [True])
    grads['norm_g'] = dng
    return loss, dx + dh, grads


def _to_shards(name, g):
    if SHARDED[name] == 'row':
        return g.reshape((N_DEV, g.shape[0] // N_DEV) + g.shape[1:])
    r, c = g.shape
    return g.reshape(r, N_DEV, c // N_DEV).transpose(1, 0, 2)


def _from_shards(name, t):
    if SHARDED[name] == 'row':
        return t.reshape((t.shape[0] * t.shape[1],) + t.shape[2:])
    n, r, c = t.shape
    return t.transpose(1, 0, 2).reshape(r, n * c)


def _step(x, mem, target, w, m, v):
    sharded = list(SHARDED)
    shard_shapes = {n: tuple(w[n].shape[1:]) for n in sharded}
    d = x.shape[-1]
    ba_lo = 2 * d
    ba_hi = ba_lo + 2 * (d // 2 // GDN_HEAD_DIM)

    srcs = [w[n][0].astype(BF16) for n in GATHER_BF16] + [w['conv_w'][0]]
    got = _gather(srcs, "gather_weights")
    full = {n: t for n, t in zip(GATHER_BF16 + ['conv_w'], got)}
    full['w_main'], full['w_ba'] = _w_in_from_shards(full.pop('w_in'), ba_lo, ba_hi)
    for n in ('w_kv_mem', 'w_out', 'conv_w'):
        full[n] = _from_shards(n, full[n])
    for n in REPLICATED:
        full[n] = w[n]
    for n in ('s5_lambda_re', 's5_lambda_im', 's5_c_re', 's5_c_im'):
        full[n] = w[n][0]

    loss, grad_x, grads = _local_step(x[0], mem[0], target[0], full)

    grads['w_in'] = _w_in_to_shards(grads.pop('w_main'), grads.pop('w_ba'), ba_lo, ba_hi)
    for n in ('w_kv_mem', 'w_out', 'conv_w'):
        grads[n] = _to_shards(n, grads[n])
    gs = [grads[n] for n in sharded]
    c_idx = lax.axis_index("c").astype(jnp.int32).reshape(1)
    from_sibling = _pair_scatter(gs, "scatter_pair")
    chip_sums = [_pair_sum(g, got, c_idx, BF16 if n in GATHER_BF16 else F32, "pair_sum_" + n)
                 for n, g, got in zip(sharded, gs, from_sibling)]
    parts = _cross_scatter(chip_sums, "scatter_cross")
    res = {}
    for n, part in zip(sharded, parts):
        outs = _sum_adam(part, w[n][0], m[n][0], v[n][0], name="adam_" + n)
        for kind, t in zip(('grad', 'delta', 'new_m', 'new_v'), outs):
            res[kind, n] = t[None]

    small = _pack([grads[n].reshape(w[n].shape) for n in REPLICATED] + [loss[:1, :1]], F32)
    allp = _gather([small], "gather_small")[0]
    zero = jnp.zeros((1, 1), F32)
    outs = _sum_adam(allp, *[_pack([t[n] for n in REPLICATED] + [zero], F32) for t in (w, m, v)], name="adam_small")
    shapes = [w[n].shape for n in REPLICATED] + [(1, 1)]
    for kind, buf in zip(('grad', 'delta', 'new_m', 'new_v'), outs):
        got = _unpack(buf, shapes)
        for n, t in zip(REPLICATED, got):
            res[kind, n] = t
        if kind == 'grad':
            total_loss = got[-1].reshape(())
    out = [total_loss, grad_x[None]]
    for kind in ('grad', 'delta', 'new_m', 'new_v'):
        out += [res[kind, n] for n in WEIGHTS]
    return tuple(out)


def kernel(x, mem, norm_g, w_in, conv_w, gdn_a_log, gdn_dt_bias, gdn_norm_g, s5_lambda_re, s5_lambda_im, s5_log_dt, s5_b_re, s5_b_im, s5_c_re, s5_c_im, s5_d, s5_w_glu, mem_norm_g, w_kv_mem, w_br_a, w_br_b, w_br_c, w_out, final_g, loss_target, m_norm_g, m_w_in, m_conv_w, m_gdn_a_log, m_gdn_dt_bias, m_gdn_norm_g, m_s5_lambda_re, m_s5_lambda_im, m_s5_log_dt, m_s5_b_re, m_s5_b_im, m_s5_c_re, m_s5_c_im, m_s5_d, m_s5_w_glu, m_mem_norm_g, m_w_kv_mem, m_w_br_a, m_w_br_b, m_w_br_c, m_w_out, m_final_g, v_norm_g, v_w_in, v_conv_w, v_gdn_a_log, v_gdn_dt_bias, v_gdn_norm_g, v_s5_lambda_re, v_s5_lambda_im, v_s5_log_dt, v_s5_b_re, v_s5_b_im, v_s5_c_re, v_s5_c_im, v_s5_d, v_s5_w_glu, v_mem_norm_g, v_w_kv_mem, v_w_br_a, v_w_br_b, v_w_br_c, v_w_out, v_final_g):
    a = dict(locals())
    w = {n: a[n] for n in WEIGHTS}
    m = {n: a['m_' + n] for n in WEIGHTS}
    v = {n: a['v_' + n] for n in WEIGHTS}
    return _step(x, mem, loss_target, w, m, v)
```

```python
import functools
import math

import jax
import jax.numpy as jnp
from jax import lax
from jax.experimental import pallas as pl
from jax.experimental.pallas import tpu as pltpu

F32 = jnp.float32
BF16 = jnp.bfloat16
HI = lax.Precision.HIGHEST

EPS = 1e-6
CHUNK = 64
GDN_HEAD_DIM = 128
CONV_WIDTH = 4
S5_GROUP = 16
S5_STATE = 64
S5_GROUPS_PER_BLOCK = 8
XA_HEADS = 4
N_DEV = 8
ADAM_LR, ADAM_B1, ADAM_B2, ADAM_EPS, ADAM_WD, ADAM_STEP = 0.001, 0.9, 0.999, 1e-08, 0.01, 10

VMEM_LIMIT_BYTES = 56 * 1024 * 1024
SCAN_LANES = 512
PACK_WIDTH = 512
PACK_ROWS = 256

WEIGHTS = ['norm_g', 'w_in', 'conv_w', 'gdn_a_log', 'gdn_dt_bias', 'gdn_norm_g', 's5_lambda_re', 's5_lambda_im',
           's5_log_dt', 's5_b_re', 's5_b_im', 's5_c_re', 's5_c_im', 's5_d', 's5_w_glu', 'mem_norm_g', 'w_kv_mem',
           'w_br_a', 'w_br_b', 'w_br_c', 'w_out', 'final_g']
SHARDED = {'w_in': 'col', 'conv_w': 'col', 's5_w_glu': 'col', 'w_kv_mem': 'row', 'w_br_a': 'col', 'w_br_b': 'col',
           'w_br_c': 'col', 'w_out': 'row'}
GATHER_BF16 = ['w_in', 's5_w_glu', 'w_kv_mem', 'w_br_a', 'w_br_b', 'w_br_c', 'w_out']
REPLICATED = [n for n in WEIGHTS if n not in SHARDED]


def _cparams(sem=None):
    return pltpu.CompilerParams(dimension_semantics=sem, vmem_limit_bytes=VMEM_LIMIT_BYTES)


def _pick(dim, pref):
    t = (min(pref, dim) // 128) * 128
    while t >= 128:
        if dim % t == 0:
            return t
        t -= 128
    return dim


def _make_dots(prep, precision):
    def raw(a, b, dims):
        return lax.dot_general(prep(a), prep(b), (dims, ((), ())), preferred_element_type=F32, precision=precision)

    @jax.custom_vjp
    def nn(a, b):
        return raw(a, b, ((1,), (0,)))

    @jax.custom_vjp
    def nt(a, b):
        return raw(a, b, ((1,), (1,)))

    @jax.custom_vjp
    def tn(a, b):
        return raw(a, b, ((0,), (0,)))

    nn.defvjp(lambda a, b: (nn(a, b), (a, b)), lambda r, ct: (nt(ct, r[1]), tn(r[0], ct)))
    nt.defvjp(lambda a, b: (nt(a, b), (a, b)), lambda r, ct: (nn(ct, r[1]), tn(ct, r[0])))
    tn.defvjp(lambda a, b: (tn(a, b), (a, b)), lambda r, ct: (nt(r[1], ct), nn(r[0], ct)))
    return nn, nt, tn


_bnn, _bnt, _btn = _make_dots(lambda a: a.astype(BF16), None)
_hnn, _hnt, _htn = _make_dots(lambda a: a.astype(F32), HI)
_mnn, _mnt, _mtn = _make_dots(lambda a: a.astype(F32), lax.Precision.HIGH)


def _mm(a, b, *, name, ta=False, tb=False, out_dtype=F32, addend=None, tm=512, tn=1024, tk=1024, b_shards=False,
        out_shards=0):
    m, k = (a.shape[1], a.shape[0]) if ta else a.shape
    brows, bcols = (b.shape[1], b.shape[0] * b.shape[2]) if b_shards else b.shape
    n = brows if tb else bcols
    assert (bcols if tb else brows) == k, (a.shape, b.shape, ta, tb)
    tm, tn, tk = _pick(m, tm), _pick(n, tn), _pick(k, tk)
    bcs = ocs = 0
    if b_shards:
        bcs = b.shape[2]
        assert bcs % 128 == 0 and (tk if tb else tn) % bcs == 0
    if out_shards:
        ocs = n // out_shards
        assert ocs % 128 == 0 and tn % ocs == 0
    nk = k // tk
    dims = ((0 if ta else 1,), (1 if tb else 0,))

    def body(*refs):
        if addend is None:
            a_ref, b_ref, o_ref, acc_ref = refs
        else:
            a_ref, b_ref, add_ref, o_ref, acc_ref = refs
        kk = pl.program_id(2)

        @pl.when(kk == 0)
        def _():
            acc_ref[...] = jnp.zeros_like(acc_ref)

        dot = lambda x, y: lax.dot_general(x.astype(BF16), y.astype(BF16), (dims, ((), ())), preferred_element_type=F32)
        if not b_shards:
            acc_ref[...] += dot(a_ref[...], b_ref[...])
        elif tb:
            for g in range(tk // bcs):
                acc_ref[...] += dot(a_ref[:, g * bcs:(g + 1) * bcs], b_ref[g])
        else:
            for g in range(tn // bcs):
                acc_ref[:, g * bcs:(g + 1) * bcs] += dot(a_ref[...], b_ref[g])

        @pl.when(kk == nk - 1)
        def _():
            r = acc_ref[...]
            if addend is not None:
                r = r + add_ref[...].astype(F32)
            if out_shards:
                for g in range(tn // ocs):
                    o_ref[g] = r[:, g * ocs:(g + 1) * ocs].astype(o_ref.dtype)
            else:
                o_ref[...] = r.astype(o_ref.dtype)

    a_spec = pl.BlockSpec((tk, tm), lambda i, j, kk: (kk, i)) if ta else pl.BlockSpec((tm, tk), lambda i, j, kk: (i, kk))
    if b_shards:
        b_spec = (pl.BlockSpec((tk // bcs, tn, bcs), lambda i, j, kk: (kk, j, 0)) if tb
                  else pl.BlockSpec((tn // bcs, tk, bcs), lambda i, j, kk: (j, kk, 0)))
    else:
        b_spec = (pl.BlockSpec((tn, tk), lambda i, j, kk: (j, kk)) if tb
                  else pl.BlockSpec((tk, tn), lambda i, j, kk: (kk, j)))
    if out_shards:
        o_spec = pl.BlockSpec((tn // ocs, tm, ocs), lambda i, j, kk: (j, i, 0))
        out_shape = jax.ShapeDtypeStruct((out_shards, m, ocs), out_dtype)
    else:
        o_spec = pl.BlockSpec((tm, tn), lambda i, j, kk: (i, j))
        out_shape = jax.ShapeDtypeStruct((m, n), out_dtype)
    in_specs = [a_spec, b_spec] + ([o_spec] if addend is not None else [])
    args = (a, b) + ((addend,) if addend is not None else ())
    return pl.pallas_call(
        body, name=name, grid=(m // tm, n // tn, nk), in_specs=in_specs, out_specs=o_spec,
        out_shape=out_shape, scratch_shapes=[pltpu.VMEM((tm, tn), F32)],
        compiler_params=_cparams(("parallel", "parallel", "arbitrary")))(*args)


def _rt(arr, ts, cb=0, w=None):
    w = arr.shape[1] if w is None else w
    return (arr, (ts, w), lambda i, cb=cb: (i, cb))


def _whole(p):
    return pl.BlockSpec(p.shape, lambda i, nd=p.ndim: (0,) * nd)


def _tile_fwd(f, name, n, acts, params, outs):
    na, npar = len(acts), len(params)

    def body(*refs):
        res = f(*[r[...] for r in refs[:na + npar]])
        for r, v in zip(refs[na + npar:], res):
            r[...] = v.astype(r.dtype)

    in_specs = [pl.BlockSpec(b, m) for _, b, m in acts] + [_whole(p) for p in params]
    out = pl.pallas_call(
        body, name=name, grid=(n,), in_specs=in_specs,
        out_specs=[pl.BlockSpec(b, m) for _, _, b, m in outs],
        out_shape=[jax.ShapeDtypeStruct(s, d) for s, d, _, _ in outs],
        compiler_params=_cparams(("parallel",)))(*[a for a, _, _ in acts], *params)
    return out


def _tile_bwd(f, name, n, acts, params, cts, agrads, pgrads):
    na, npar, nc = len(acts), len(params), len(cts)

    def body(*refs):
        i = pl.program_id(0)
        ins = [r[...] for r in refs[:na + npar]]
        outs, vjp = jax.vjp(f, *ins)
        g = vjp(tuple(c[...].astype(o.dtype) for c, o in zip(refs[na + npar:na + npar + nc], outs)))
        orefs = refs[na + npar + nc:]
        k = 0
        for j in range(na):
            if agrads[j] is not None:
                orefs[k][...] = g[j].astype(orefs[k].dtype)
                k += 1
        for j in range(npar):
            if pgrads[j]:
                o = orefs[k]

                @pl.when(i == 0)
                def _(o=o):
                    o[...] = jnp.zeros_like(o)

                o[...] += g[na + j].astype(F32)
                k += 1

    in_specs = ([pl.BlockSpec(b, m) for _, b, m in acts] + [_whole(p) for p in params]
                + [pl.BlockSpec(b, m) for _, b, m in cts])
    out_specs = [pl.BlockSpec(g[2], g[3]) for g in agrads if g is not None]
    out_shape = [jax.ShapeDtypeStruct(g[0], g[1]) for g in agrads if g is not None]
    for p, flag in zip(params, pgrads):
        if flag:
            out_specs.append(_whole(p))
            out_shape.append(jax.ShapeDtypeStruct(p.shape, F32))
    return pl.pallas_call(
        body, name=name, grid=(n,), in_specs=in_specs, out_specs=out_specs, out_shape=out_shape,
        compiler_params=_cparams(("arbitrary",)))(*[a for a, _, _ in acts], *params, *[c for c, _, _ in cts])


def _silu(x):
    return x * jax.nn.sigmoid(x)


def _rms(x, g):
    x = x.astype(F32)
    return (x * lax.rsqrt(jnp.mean(x * x, axis=-1, keepdims=True) + EPS) * g,)


def _shift_down(x, s):
    row = lax.broadcasted_iota(jnp.int32, x.shape, 0)
    return jnp.where(row >= s, pltpu.roll(x, s, 0), 0.0)


def _shift_up(x, s):
    n = x.shape[0]
    row = lax.broadcasted_iota(jnp.int32, x.shape, 0)
    return jnp.where(row < n - s, pltpu.roll(x, n - s, 0), 0.0)


@functools.partial(jax.custom_vjp, nondiff_argnums=(1,))
def _shift(x, s):
    return _shift_down(x, s)


_shift.defvjp(lambda x, s: (_shift_down(x, s), None), lambda s, _, ct: (_shift_up(ct, s),))


def _gdn_pre(mode):
    def f(x, w):
        y = x * w[CONV_WIDTH - 1:CONV_WIDTH, :]
        for j in range(CONV_WIDTH - 1):
            y = y + _shift(x, CONV_WIDTH - 1 - j) * w[j:j + 1, :]
        y = _silu(y)
        if mode != 'v':
            y = y * lax.rsqrt(jnp.sum(y * y, axis=-1, keepdims=True) + EPS)
        if mode == 'q':
            y = y * (GDN_HEAD_DIM ** -0.5)
        return (y,)
    return f


def _softplus(x):
    return jnp.maximum(x, 0.0) + jnp.log1p(jnp.exp(-jnp.abs(x)))


def _gdn_gates(ba, alog, dtb, e_beta, e_g):
    beta = jax.nn.sigmoid(ba)
    g = -jnp.exp(alog) * _softplus(ba + dtb)
    return _hnn(beta, lax.stop_gradient(e_beta)), _hnn(g, lax.stop_gradient(e_g))


@jax.custom_vjp
def _inverse_known(neg, t):
    return t


_inverse_known.defvjp(lambda neg, t: (t, t), lambda t, ct: (_mtn(t, _mnt(ct, t)), jnp.zeros_like(t)))


def _gdn_intra(q, k, v, gb, bb, t_known=None):
    n, c = len(q), q[0].shape[0]
    ri = lax.broadcasted_iota(jnp.int32, (c, c), 0)
    ci = lax.broadcasted_iota(jnp.int32, (c, c), 1)
    incl, strict = ri >= ci, ri > ci
    tri = incl.astype(F32)
    eye = (ri == ci).astype(F32)
    each = range(n)
    gc = [_hnn(tri, gb[i]) for i in each]
    decay = [jnp.exp(jnp.where(incl, gc[i][:, :c] - gc[i].T[:c, :], -1e30)) for i in each]
    kb = [k[i] * bb[i] for i in each]
    kk = [_bnt(kb[i], k[i]) for i in each]
    qk = [_bnt(q[i], k[i]) for i in each]
    p = [jnp.where(strict, -(kk[i] * decay[i]), 0.0) for i in each]
    if t_known is None:
        t = [eye + p[i] for i in each]
        for _ in range(int(math.log2(c)) - 1):
            p = [_mnn(p[i], p[i]) for i in each]
            tp = [_mnn(t[i], p[i]) for i in each]
            t = [t[i] + tp[i] for i in each]
    else:
        t = [_inverse_known(p[i], t_known[i]) for i in each]
    egc = [jnp.exp(gc[i]) for i in each]
    u_val = [_mnn(t[i], v[i] * bb[i]) for i in each]
    w_dec = [_mnn(t[i], kb[i] * egc[i]) for i in each]
    qk = [qk[i] * decay[i] for i in each]
    gl = [jnp.sum(gb[i], axis=0, keepdims=True) for i in each]
    return w_dec, u_val, qk, [q[i] * egc[i] for i in each], [k[i] * jnp.exp(gl[i] - gc[i]) for i in each], t


def _gdn_inter(w_dec, u_val, qk, q_dec, k_dec, gb, state):
    each = range(len(state))
    ws = [_bnn(w_dec[i], state[i]) for i in each]
    qs = [_bnn(q_dec[i], state[i]) for i in each]
    v_new = [u_val[i] - ws[i] for i in each]
    qv = [_bnn(qk[i], v_new[i]) for i in each]
    kv = [_btn(k_dec[i], v_new[i]) for i in each]
    decayed = [state[i] * jnp.exp(jnp.sum(gb[i], axis=0, keepdims=True)) for i in each]
    return [qs[i] + qv[i] for i in each], [decayed[i] + kv[i] for i in each]


def _gdn_post(o, z, g):
    parts = []
    for h in range(o.shape[1] // GDN_HEAD_DIM):
        oh = o[:, h * GDN_HEAD_DIM:(h + 1) * GDN_HEAD_DIM]
        parts.append(oh * lax.rsqrt(jnp.mean(oh * oh, axis=-1, keepdims=True) + EPS) * g)
    y = parts[0] if len(parts) == 1 else jnp.concatenate(parts, axis=1)
    return (y * _silu(z),)


def _gelu(x):
    return 0.5 * x * (1.0 + jnp.tanh(0.7978845608028654 * (x + 0.044715 * x * x * x)))


def _s5_post1(ylin, xb, d):
    return (_gelu(ylin + d * xb),)


def _s5_post2(t, z):
    w = t.shape[1] // 2
    return (t[:, :w] * jax.nn.sigmoid(t[:, w:]) * _silu(z),)


def _attn(q, z, kv):
    w = q.shape[1]
    hd = w // XA_HEADS
    parts = []
    for h in range(XA_HEADS):
        s = _bnt(q[:, h * hd:(h + 1) * hd], kv[:, h * hd:(h + 1) * hd]) * (hd ** -0.5)
        s = s - jnp.max(s, axis=-1, keepdims=True)
        e = jnp.exp(s)
        p = e / jnp.sum(e, axis=-1, keepdims=True)
        parts.append(_bnn(p, kv[:, w + h * hd:w + (h + 1) * hd]))
    return (jnp.concatenate(parts, axis=1) * _silu(z),)


def _merge(g0, g1, g2, pa, pb, pc):
    return (jax.nn.sigmoid(g0) * pa + jax.nn.sigmoid(g1) * pb + jax.nn.sigmoid(g2) * pc,)


def _s5_params(lr, li, logdt, br, bi, e):
    dt = jnp.exp(logdt)
    mag = jnp.exp(lr * dt)
    ab_re, ab_im = mag * jnp.cos(li * dt), mag * jnp.sin(li * dt)
    den = lr * lr + li * li
    nr, ni = ab_re - 1.0, ab_im
    e = lax.stop_gradient(e)
    cre = _hnn((nr * lr + ni * li) / den, e)
    cim = _hnn((ni * lr - nr * li) / den, e)
    return ab_re, ab_im, cre * br - cim * bi, cre * bi + cim * br


def _gdn_blocks(s, w, per_step):
    nh, nc = w // GDN_HEAD_DIM, s // CHUNK
    cpb = math.gcd(per_step, nc)
    return nh, nc, cpb, nc // cpb, (cpb * CHUNK, w), (cpb * CHUNK, nh * CHUNK)


def _gdn_pairs(cpb, nh):
    wide, narrow = [], []
    for cb in range(cpb):
        rows = slice(cb * CHUNK, (cb + 1) * CHUNK)
        for h in range(nh):
            wide.append((rows, slice(h * GDN_HEAD_DIM, (h + 1) * GDN_HEAD_DIM)))
            narrow.append((rows, slice(h * CHUNK, (h + 1) * CHUNK)))
    return wide, narrow


def _gdn_intra_fwd(q, k, v, gb, bb, per_step=4):
    s, w = q.shape
    nh, nc, cpb, n, wide, narrow = _gdn_blocks(s, w, per_step)

    def body(q_ref, k_ref, v_ref, g_ref, b_ref, wd_ref, uv_ref, qk_ref, qd_ref, kd_ref, t_ref):
        wide, narrow = _gdn_pairs(cpb, nh)
        res = _gdn_intra(*[[r[ix] for ix in wide] for r in (q_ref, k_ref, v_ref, g_ref, b_ref)])
        for ref, vals, where in zip((wd_ref, uv_ref, qk_ref, qd_ref, kd_ref, t_ref), res,
                                    (wide, wide, narrow, wide, wide, narrow)):
            for ix, val in zip(where, vals):
                ref[ix] = val

    bw = pl.BlockSpec(wide, lambda i: (i, 0))
    bn = pl.BlockSpec(narrow, lambda i: (i, 0))
    fw = jax.ShapeDtypeStruct((s, w), F32)
    fn = jax.ShapeDtypeStruct((s, nh * CHUNK), F32)
    return pl.pallas_call(
        body, name="gdn_intra", grid=(n,), in_specs=[bw] * 5, out_specs=[bw, bw, bn, bw, bw, bn],
        out_shape=[fw, fw, fn, fw, fw, fn], compiler_params=_cparams(("parallel",)))(q, k, v, gb, bb)


def _gdn_intra_bwd(q, k, v, gb, bb, t, cts, dgb_inter, per_step=4):
    s, w = q.shape
    nh, nc, cpb, n, wide, narrow = _gdn_blocks(s, w, per_step)

    def body(q_ref, k_ref, v_ref, g_ref, b_ref, t_ref, cwd, cuv, cqk, cqd, ckd, dgi, dq_ref, dk_ref, dv_ref, dg_ref,
             db_ref):
        wide, narrow = _gdn_pairs(cpb, nh)
        t_known = [t_ref[ix] for ix in narrow]
        _, vjp = jax.vjp(lambda *a: _gdn_intra(*a, t_known=t_known)[:5],
                         *[[r[ix] for ix in wide] for r in (q_ref, k_ref, v_ref, g_ref, b_ref)])
        cts = tuple([r[ix] for ix in where] for r, where in zip((cwd, cuv, cqk, cqd, ckd),
                                                               (wide, wide, narrow, wide, wide)))
        dq, dk, dv, dg, db = vjp(cts)
        for j, ix in enumerate(wide):
            dq_ref[ix], dk_ref[ix], dv_ref[ix], db_ref[ix] = dq[j], dk[j], dv[j], db[j]
            dg_ref[ix] = dg[j] + dgi[ix]

    bw = pl.BlockSpec(wide, lambda i: (i, 0))
    bn = pl.BlockSpec(narrow, lambda i: (i, 0))
    return pl.pallas_call(
        body, name="gdn_intra_bwd", grid=(n,), in_specs=[bw] * 5 + [bn, bw, bw, bn, bw, bw, bw], out_specs=[bw] * 5,
        out_shape=[jax.ShapeDtypeStruct((s, w), F32)] * 5,
        compiler_params=_cparams(("parallel",)))(q, k, v, gb, bb, t, *cts, dgb_inter)


def _gdn_inter_fwd(wd, uv, qk, qd, kd, gb, per_step=4):
    s, w = wd.shape
    nh, nc, cpb, n, wide, narrow = _gdn_blocks(s, w, per_step)
    hd = GDN_HEAD_DIM

    def body(wd_ref, uv_ref, qk_ref, qd_ref, kd_ref, g_ref, o_ref, st_ref, state):
        @pl.when(pl.program_id(0) == 0)
        def _():
            state[...] = jnp.zeros_like(state)

        wide, narrow = _gdn_pairs(cpb, nh)
        st = [state[h] for h in range(nh)]
        for cb in range(cpb):
            wi, na = wide[cb * nh:(cb + 1) * nh], narrow[cb * nh:(cb + 1) * nh]
            for h in range(nh):
                st_ref[cb, h] = st[h]
            o, st = _gdn_inter([wd_ref[ix] for ix in wi], [uv_ref[ix] for ix in wi], [qk_ref[ix] for ix in na],
                               [qd_ref[ix] for ix in wi], [kd_ref[ix] for ix in wi], [g_ref[ix] for ix in wi], st)
            for h in range(nh):
                o_ref[wi[h]] = o[h]
        for h in range(nh):
            state[h] = st[h]

    bw = pl.BlockSpec(wide, lambda i: (i, 0))
    bn = pl.BlockSpec(narrow, lambda i: (i, 0))
    return pl.pallas_call(
        body, name="gdn_inter", grid=(n,), in_specs=[bw, bw, bn, bw, bw, bw],
        out_specs=[bw, pl.BlockSpec((cpb, nh, hd, hd), lambda i: (i, 0, 0, 0))],
        out_shape=[jax.ShapeDtypeStruct((s, w), F32), jax.ShapeDtypeStruct((nc, nh, hd, hd), F32)],
        scratch_shapes=[pltpu.VMEM((nh, hd, hd), F32)],
        compiler_params=_cparams(("arbitrary",)))(wd, uv, qk, qd, kd, gb)


def _gdn_inter_bwd(wd, uv, qk, qd, kd, gb, states, do, per_step=4):
    s, w = wd.shape
    nh, nc, cpb, n, wide, narrow = _gdn_blocks(s, w, per_step)
    hd = GDN_HEAD_DIM

    def body(wd_ref, uv_ref, qk_ref, qd_ref, kd_ref, g_ref, st_ref, do_ref, cwd, cuv, cqk, cqd, ckd, dg_ref, dstate):
        @pl.when(pl.program_id(0) == 0)
        def _():
            dstate[...] = jnp.zeros_like(dstate)

        wide, narrow = _gdn_pairs(cpb, nh)
        dst = [dstate[h] for h in range(nh)]
        for cb in reversed(range(cpb)):
            wi, na = wide[cb * nh:(cb + 1) * nh], narrow[cb * nh:(cb + 1) * nh]
            _, vjp = jax.vjp(_gdn_inter, [wd_ref[ix] for ix in wi], [uv_ref[ix] for ix in wi],
                             [qk_ref[ix] for ix in na], [qd_ref[ix] for ix in wi], [kd_ref[ix] for ix in wi],
                             [g_ref[ix] for ix in wi], [st_ref[cb, h] for h in range(nh)])
            dwd, duv, dqk, dqd, dkd, dg, dst = vjp(([do_ref[ix] for ix in wi], dst))
            for h in range(nh):
                cwd[wi[h]], cuv[wi[h]], cqk[na[h]], cqd[wi[h]], ckd[wi[h]], dg_ref[wi[h]] = (
                    dwd[h], duv[h], dqk[h], dqd[h], dkd[h], dg[h])
        for h in range(nh):
            dstate[h] = dst[h]

    bw = pl.BlockSpec(wide, lambda i: (n - 1 - i, 0))
    bn = pl.BlockSpec(narrow, lambda i: (n - 1 - i, 0))
    fw = jax.ShapeDtypeStruct((s, w), F32)
    return pl.pallas_call(
        body, name="gdn_inter_bwd", grid=(n,),
        in_specs=[bw, bw, bn, bw, bw, bw, pl.BlockSpec((cpb, nh, hd, hd), lambda i: (n - 1 - i, 0, 0, 0)), bw],
        out_specs=[bw, bw, bn, bw, bw, bw],
        out_shape=[fw, fw, jax.ShapeDtypeStruct((s, nh * CHUNK), F32), fw, fw, fw],
        scratch_shapes=[pltpu.VMEM((nh, hd, hd), F32)],
        compiler_params=_cparams(("arbitrary",)))(wd, uv, qk, qd, kd, gb, states, do)


def _s5_coef(ar, ai):
    nl = ar.shape[1]

    def body(ar_ref, ai_ref, o_ref):
        row = lax.broadcasted_iota(jnp.int32, (8, nl), 0)
        for base, sign in ((0, 1.0), (8, -1.0)):
            pr = [jnp.broadcast_to(ar_ref[...], (8, nl))]
            pi = [jnp.broadcast_to(ai_ref[...], (8, nl)) * sign]
            for _ in range(7):
                pr.append(pr[-1] * pr[0] - pi[-1] * pi[0])
                pi.append(pr[-2] * pi[0] + pi[-1] * pr[0])
            for j, d in enumerate((1, 2, 4)):
                m = (row >= d) if base == 0 else (row <= 7 - d)
                o_ref[base + 2 * j] = jnp.where(m, pr[d - 1], 0.0)
                o_ref[base + 2 * j + 1] = jnp.where(m, pi[d - 1], 0.0)
            cr, ci = jnp.zeros((8, nl), F32), jnp.zeros((8, nl), F32)
            for t in range(8):
                e = t if base == 0 else 7 - t
                cr = jnp.where(row == t, pr[e], cr)
                ci = jnp.where(row == t, pi[e], ci)
            o_ref[base + 6] = cr
            o_ref[base + 7] = ci

    return pl.pallas_call(body, name="s5_coef", out_shape=jax.ShapeDtypeStruct((16, 8, nl), F32),
                          compiler_params=_cparams())(ar, ai)


def _scan_tile(src_re, src_im, dst_re, dst_im, coef_ref, carry_re, carry_im, ts, reverse, extra=None):
    nl = src_re.shape[1]
    base = 8 if reverse else 0
    ng = ts // 8
    for lc in range(nl // SCAN_LANES):
        ln = slice(lc * SCAN_LANES, (lc + 1) * SCAN_LANES)
        m = [coef_ref[base + j, :, ln] for j in range(8)]
        row = lax.broadcasted_iota(jnp.int32, (8, SCAN_LANES), 0)

        def step(r, carry, ln=ln, m=m, row=row):
            grp = (ng - 1 - r) if reverse else r
            rows = pl.ds(pl.multiple_of(grp * 8, 8), 8)
            xr, xi = src_re[rows, ln], src_im[rows, ln]
            for j, d in enumerate((1, 2, 4)):
                sh = 8 - d if reverse else d
                sr, si = pltpu.roll(xr, sh, 0), pltpu.roll(xi, sh, 0)
                mr, mi = m[2 * j], m[2 * j + 1]
                xr, xi = xr + mr * sr - mi * si, xi + mr * si + mi * sr
            cr, ci = carry[0], carry[1]
            hr = xr + m[6] * cr - m[7] * ci
            hi = xi + m[6] * ci + m[7] * cr
            dst_re[rows, ln] = hr
            dst_im[rows, ln] = hi
            edge = 0 if reverse else 7
            out = (jnp.broadcast_to(hr[edge:edge + 1, :], hr.shape), jnp.broadcast_to(hi[edge:edge + 1, :], hi.shape))
            if extra is not None:
                h_re, h_im, halo_re, halo_im, first, _, _ = extra
                prev = pl.ds(pl.multiple_of(jnp.maximum(grp - 1, 0) * 8, 8), 8)
                use_halo = grp == 0
                pr = jnp.where(use_halo, halo_re[:, ln] * first, h_re[prev, ln])
                pi = jnp.where(use_halo, halo_im[:, ln] * first, h_im[prev, ln])
                qr = jnp.where(row == 0, jnp.broadcast_to(pr[7:8, :], pr.shape), pltpu.roll(h_re[rows, ln], 1, 0))
                qi = jnp.where(row == 0, jnp.broadcast_to(pi[7:8, :], pi.shape), pltpu.roll(h_im[rows, ln], 1, 0))
                out = out + (carry[2] + hr * qr + hi * qi, carry[3] + hi * qr - hr * qi)
            return out

        init = (carry_re[:, ln], carry_im[:, ln])
        if extra is not None:
            init = init + (extra[5][:, ln], extra[6][:, ln])
        fin = lax.fori_loop(0, ng, step, init)
        carry_re[:, ln] = fin[0]
        carry_im[:, ln] = fin[1]
        if extra is not None:
            extra[5][:, ln] = fin[2]
            extra[6][:, ln] = fin[3]


def _s5_fwd(xb, bb_re, bb_im, c_re, c_im, coef, ts):
    s, w = xb.shape
    nb = bb_re.shape[0]
    nl = nb * 512

    def body(x_ref, bre_ref, bim_ref, cre_ref, cim_ref, coef_ref, hre_ref, him_ref, y_ref, ure, uim, car_re, car_im):
        @pl.when(pl.program_id(0) == 0)
        def _():
            car_re[...] = jnp.zeros_like(car_re)
            car_im[...] = jnp.zeros_like(car_im)

        for b in range(nb):
            xs = x_ref[:, b * 128:(b + 1) * 128].astype(BF16)
            ure[:, b * 512:(b + 1) * 512] = jnp.dot(xs, bre_ref[b], preferred_element_type=F32)
            uim[:, b * 512:(b + 1) * 512] = jnp.dot(xs, bim_ref[b], preferred_element_type=F32)
        _scan_tile(ure, uim, hre_ref, him_ref, coef_ref, car_re, car_im, ts, False)
        for b in range(nb):
            hr = hre_ref[:, b * 512:(b + 1) * 512].astype(BF16)
            hi = him_ref[:, b * 512:(b + 1) * 512].astype(BF16)
            y_ref[:, b * 128:(b + 1) * 128] = (jnp.dot(hr, cre_ref[b], preferred_element_type=F32)
                                               - jnp.dot(hi, cim_ref[b], preferred_element_type=F32))

    row = lambda wd: pl.BlockSpec((ts, wd), lambda i: (i, 0))
    return pl.pallas_call(
        body, name="s5_fwd", grid=(s // ts,),
        in_specs=[row(w), _whole(bb_re), _whole(bb_im), _whole(c_re), _whole(c_im), _whole(coef)],
        out_specs=[row(nl), row(nl), row(w)],
        out_shape=[jax.ShapeDtypeStruct((s, nl), F32), jax.ShapeDtypeStruct((s, nl), F32),
                   jax.ShapeDtypeStruct((s, w), F32)],
        scratch_shapes=[pltpu.VMEM((ts, nl), F32), pltpu.VMEM((ts, nl), F32), pltpu.VMEM((8, nl), F32),
                        pltpu.VMEM((8, nl), F32)],
        compiler_params=_cparams(("arbitrary",)))(xb, bb_re, bb_im, c_re, c_im, coef)


def _s5_bwd(dy, xb, h_re, h_im, bb_re, bb_im, c_re, c_im, coef, ts):
    s, w = xb.shape
    nb = bb_re.shape[0]
    nl = nb * 512
    nt = s // ts

    def body(dy_ref, x_ref, hre_ref, him_ref, halo_re, halo_im, bre_ref, bim_ref, cre_ref, cim_ref, coef_ref,
             dx_ref, dbre_ref, dbim_ref, dcre_ref, dcim_ref, dare_ref, daim_ref, gre, gim, car_re, car_im):
        i = pl.program_id(0)

        @pl.when(i == 0)
        def _():
            for r in (car_re, car_im, dbre_ref, dbim_ref, dcre_ref, dcim_ref, dare_ref, daim_ref):
                r[...] = jnp.zeros_like(r)

        for b in range(nb):
            dyb = dy_ref[:, b * 128:(b + 1) * 128].astype(BF16)
            gre[:, b * 512:(b + 1) * 512] = lax.dot_general(dyb, cre_ref[b], (((1,), (1,)), ((), ())),
                                                            preferred_element_type=F32)
            gim[:, b * 512:(b + 1) * 512] = -lax.dot_general(dyb, cim_ref[b], (((1,), (1,)), ((), ())),
                                                             preferred_element_type=F32)
            hr = hre_ref[:, b * 512:(b + 1) * 512].astype(BF16)
            hi = him_ref[:, b * 512:(b + 1) * 512].astype(BF16)
            dcre_ref[b] += lax.dot_general(hr, dyb, (((0,), (0,)), ((), ())), preferred_element_type=F32)
            dcim_ref[b] -= lax.dot_general(hi, dyb, (((0,), (0,)), ((), ())), preferred_element_type=F32)
        first = (i != nt - 1).astype(F32)
        _scan_tile(gre, gim, gre, gim, coef_ref, car_re, car_im, ts, True,
                   extra=(hre_ref, him_ref, halo_re, halo_im, first, dare_ref, daim_ref))
        for b in range(nb):
            gr = gre[:, b * 512:(b + 1) * 512].astype(BF16)
            gi = gim[:, b * 512:(b + 1) * 512].astype(BF16)
            xs = x_ref[:, b * 128:(b + 1) * 128].astype(BF16)
            dx_ref[:, b * 128:(b + 1) * 128] = (
                lax.dot_general(gr, bre_ref[b], (((1,), (1,)), ((), ())), preferred_element_type=F32)
                + lax.dot_general(gi, bim_ref[b], (((1,), (1,)), ((), ())), preferred_element_type=F32))
            dbre_ref[b] += lax.dot_general(xs, gr, (((0,), (0,)), ((), ())), preferred_element_type=F32)
            dbim_ref[b] += lax.dot_general(xs, gi, (((0,), (0,)), ((), ())), preferred_element_type=F32)

    row = lambda wd: pl.BlockSpec((ts, wd), lambda i: (nt - 1 - i, 0))
    halo = pl.BlockSpec((8, nl), lambda i: (jnp.maximum((nt - 1 - i) * (ts // 8) - 1, 0), 0))
    return pl.pallas_call(
        body, name="s5_bwd", grid=(nt,),
        in_specs=[row(w), row(w), row(nl), row(nl), halo, halo, _whole(bb_re), _whole(bb_im), _whole(c_re),
                  _whole(c_im), _whole(coef)],
        out_specs=[row(w), _whole(bb_re), _whole(bb_im), _whole(c_re), _whole(c_im),
                   pl.BlockSpec((8, nl), lambda i: (0, 0)), pl.BlockSpec((8, nl), lambda i: (0, 0))],
        out_shape=[jax.ShapeDtypeStruct((s, w), F32), jax.ShapeDtypeStruct(bb_re.shape, F32),
                   jax.ShapeDtypeStruct(bb_im.shape, F32), jax.ShapeDtypeStruct(c_re.shape, F32),
                   jax.ShapeDtypeStruct(c_im.shape, F32), jax.ShapeDtypeStruct((8, nl), F32),
                   jax.ShapeDtypeStruct((8, nl), F32)],
        scratch_shapes=[pltpu.VMEM((ts, nl), F32), pltpu.VMEM((ts, nl), F32), pltpu.VMEM((8, nl), F32),
                        pltpu.VMEM((8, nl), F32)],
        compiler_params=_cparams(("arbitrary",)))(dy, xb, h_re, h_im, h_re, h_im, bb_re, bb_im, c_re, c_im, coef)


def _final(x, mo, target, fg, ts):
    s, d = x.shape

    def f(x, mo, fg, tgt):
        y = _rms(x + mo, fg)[0]
        err = y - tgt
        return 0.5 * jnp.sum(jnp.mean(err * err, axis=-1, keepdims=True), axis=0, keepdims=True)

    def body(x_ref, mo_ref, t_ref, fg_ref, dh_ref, dfg_ref, loss_ref):
        @pl.when(pl.program_id(0) == 0)
        def _():
            dfg_ref[...] = jnp.zeros_like(dfg_ref)
            loss_ref[...] = jnp.zeros_like(loss_ref)

        loss, vjp = jax.vjp(f, x_ref[...], mo_ref[...], fg_ref[...], t_ref[...])
        _, dmo, dfg, _ = vjp(jnp.ones((1, 1), F32))
        dh_ref[...] = dmo
        dfg_ref[...] += dfg
        loss_ref[...] += jnp.broadcast_to(loss, loss_ref.shape)

    row = pl.BlockSpec((ts, d), lambda i: (i, 0))
    return pl.pallas_call(
        body, name="final", grid=(s // ts,), in_specs=[row, row, row, _whole(fg)],
        out_specs=[row, _whole(fg), pl.BlockSpec((8, 128), lambda i: (0, 0))],
        out_shape=[jax.ShapeDtypeStruct((s, d), F32), jax.ShapeDtypeStruct(fg.shape, F32),
                   jax.ShapeDtypeStruct((8, 128), F32)],
        compiler_params=_cparams(("arbitrary",)))(x, mo, target, fg)


N_CHIPS = 4


def _other_chips(x, y):
    return [((1 - x, y), 2 * (1 - x) + y), ((x, 1 - y), 2 * x + 1 - y), ((1 - x, 1 - y), 2 * (1 - x) + 1 - y)]


def _comm_call(body, name, srcs, out_shapes, n_sems):
    n = len(srcs)
    return pl.pallas_call(
        body, name=name, in_specs=[pl.BlockSpec(memory_space=pl.ANY)] * n,
        out_specs=[pl.BlockSpec(memory_space=pl.ANY)] * n, out_shape=out_shapes,
        scratch_shapes=[pltpu.SemaphoreType.DMA((n, n_sems)), pltpu.SemaphoreType.DMA((n, n_sems)),
                        pltpu.SemaphoreType.DMA((n,))],
        compiler_params=pltpu.CompilerParams(has_side_effects=True))(*srcs)


def _gather(srcs, name):
    n = len(srcs)

    def body(*refs):
        src, out = refs[:n], refs[n:2 * n]
        send_sems, recv_sems, local_sems = refs[2 * n:]
        x, y, c = lax.axis_index("x"), lax.axis_index("y"), lax.axis_index("c")
        me, sib_slot, sib = 4 * x + 2 * y + c, 4 * x + 2 * y + 1 - c, (x, y, 1 - c)
        chips = _other_chips(x, y)

        def cp(a, k, src_ref, slot, to):
            return pltpu.make_async_remote_copy(
                src_ref=src_ref, dst_ref=out[a].at[slot], send_sem=send_sems.at[a, k], recv_sem=recv_sems.at[a, k],
                device_id=to, device_id_type=pl.DeviceIdType.MESH)

        local = [pltpu.make_async_copy(src[a], out[a].at[me], local_sems.at[a]) for a in range(n)]
        first = [cp(a, 0, src[a], me, sib) for a in range(n)]
        first += [cp(a, 1 + j, src[a], me, (*chip, c)) for j, (chip, _) in enumerate(chips) for a in range(n)]
        for d in local + first:
            d.start()
        passed = []
        for j, (chip, q) in enumerate(chips):
            for a in range(n):
                cp(a, 1 + j, src[a], 2 * q + c, sib).wait_recv()
                fwd = cp(a, 4 + j, out[a].at[2 * q + c], 2 * q + c, sib)
                fwd.start()
                passed.append(fwd)
        for a in range(n):
            cp(a, 0, src[a], sib_slot, sib).wait_recv()
        for j, (chip, q) in enumerate(chips):
            for a in range(n):
                cp(a, 4 + j, src[a], 2 * q + 1 - c, sib).wait_recv()
        for d in first + passed:
            d.wait_send()
        for d in local:
            d.wait()

    return _comm_call(body, name, srcs, [jax.ShapeDtypeStruct((N_DEV,) + s.shape, s.dtype) for s in srcs], 7)


def _pair_scatter(gs, name):
    n = len(gs)

    def body(*refs):
        src, out = refs[:n], refs[n:2 * n]
        send_sems, recv_sems, _ = refs[2 * n:]
        x, y, c = lax.axis_index("x"), lax.axis_index("y"), lax.axis_index("c")
        sends = []
        for q in range(N_CHIPS):
            for a in range(n):
                d = pltpu.make_async_remote_copy(
                    src_ref=src[a].at[2 * q + 1 - c], dst_ref=out[a].at[q], send_sem=send_sems.at[a, q],
                    recv_sem=recv_sems.at[a, q], device_id=(x, y, 1 - c), device_id_type=pl.DeviceIdType.MESH)
                d.start()
                sends.append(d)
        for d in sends:
            d.wait_recv()
        for d in sends:
            d.wait_send()

    return _comm_call(body, name, gs, [jax.ShapeDtypeStruct((N_CHIPS,) + g.shape[1:], g.dtype) for g in gs], N_CHIPS)


def _cross_scatter(ps, name):
    n = len(ps)

    def body(*refs):
        src, out = refs[:n], refs[n:2 * n]
        send_sems, recv_sems, local_sems = refs[2 * n:]
        x, y, c = lax.axis_index("x"), lax.axis_index("y"), lax.axis_index("c")
        mine = 2 * x + y
        chips = _other_chips(x, y)
        local = [pltpu.make_async_copy(src[a].at[mine], out[a].at[mine], local_sems.at[a]) for a in range(n)]
        for d in local:
            d.start()
        sends = []
        for j, (chip, q) in enumerate(chips):
            for a in range(n):
                d = pltpu.make_async_remote_copy(
                    src_ref=src[a].at[q], dst_ref=out[a].at[mine], send_sem=send_sems.at[a, j],
                    recv_sem=recv_sems.at[a, j], device_id=(*chip, c), device_id_type=pl.DeviceIdType.MESH)
                d.start()
                sends.append(d)
        for j, (chip, q) in enumerate(chips):
            for a in range(n):
                pltpu.make_async_remote_copy(
                    src_ref=src[a].at[q], dst_ref=out[a].at[q], send_sem=send_sems.at[a, j],
                    recv_sem=recv_sems.at[a, j], device_id=(*chip, c), device_id_type=pl.DeviceIdType.MESH).wait_recv()
        for d in sends:
            d.wait_send()
        for d in local:
            d.wait()

    return _comm_call(body, name, ps, [jax.ShapeDtypeStruct(p.shape, p.dtype) for p in ps], 3)


def _pair_sum(g, got, c_idx, out_dtype, name):
    _, r, c = g.shape
    lanes = -(-c // 128) * 128
    tr = _pick_rows(r, max(8, (2 * 1024 * 1024) // (lanes * 4)))
    g4 = g.reshape((N_CHIPS, 2) + g.shape[1:])

    def body(c_ref, g_ref, got_ref, o_ref):
        o_ref[...] = (g_ref[...] + got_ref[...]).astype(o_ref.dtype)

    return pl.pallas_call(
        body, name=name,
        grid_spec=pltpu.PrefetchScalarGridSpec(
            num_scalar_prefetch=1, grid=(N_CHIPS, r // tr),
            in_specs=[pl.BlockSpec((None, None, tr, c), lambda q, i, cr: (q, cr[0], i, 0)),
                      pl.BlockSpec((None, tr, c), lambda q, i, cr: (q, i, 0))],
            out_specs=pl.BlockSpec((None, tr, c), lambda q, i, cr: (q, i, 0))),
        out_shape=jax.ShapeDtypeStruct(got.shape, out_dtype),
        compiler_params=_cparams(("parallel", "parallel")))(c_idx, g4, got)


def _pick_rows(r, pref):
    t = (min(pref, r) // 8) * 8
    while t >= 8:
        if r % t == 0:
            return t
        t -= 8
    return r


def _w_in_from_shards(t, lo, hi):
    n, r, cs = t.shape
    tr = _pick_rows(r, 256)
    wm = n * cs - (hi - lo)

    def body(t_ref, m_ref, b_ref):
        full = jnp.concatenate([t_ref[j] for j in range(n)], axis=1)
        m_ref[...] = jnp.concatenate([full[:, :lo], full[:, hi:]], axis=1)
        b_ref[...] = jnp.concatenate([full[:, lo:hi], jnp.zeros((tr, 128 - (hi - lo)), full.dtype)], axis=1)

    return pl.pallas_call(
        body, name="w_in_layout", grid=(r // tr,), in_specs=[pl.BlockSpec((n, tr, cs), lambda i: (0, i, 0))],
        out_specs=[pl.BlockSpec((tr, wm), lambda i: (i, 0)), pl.BlockSpec((tr, 128), lambda i: (i, 0))],
        out_shape=[jax.ShapeDtypeStruct((r, wm), t.dtype), jax.ShapeDtypeStruct((r, 128), t.dtype)],
        compiler_params=_cparams(("parallel",)))(t)


def _w_in_to_shards(gm, gb, lo, hi):
    r, wm = gm.shape
    cs = (wm + hi - lo) // N_DEV
    tr = _pick_rows(r, 64)

    def body(m_ref, b_ref, o_ref):
        m = m_ref[...]
        full = jnp.concatenate([m[:, :lo], b_ref[:, :hi - lo], m[:, lo:]], axis=1)
        for j in range(N_DEV):
            o_ref[j] = full[:, j * cs:(j + 1) * cs]

    return pl.pallas_call(
        body, name="dw_in_layout", grid=(r // tr,),
        in_specs=[pl.BlockSpec((tr, wm), lambda i: (i, 0)), pl.BlockSpec((tr, 128), lambda i: (i, 0))],
        out_specs=pl.BlockSpec((N_DEV, tr, cs), lambda i: (0, i, 0)),
        out_shape=jax.ShapeDtypeStruct((N_DEV, r, cs), gm.dtype), compiler_params=_cparams(("parallel",)))(gm, gb)


def _pack(arrs, dtype, lead=()):
    nlead = len(lead)
    flat = jnp.concatenate([a.astype(dtype).reshape(lead + (-1,)) for a in arrs], axis=nlead)
    n = flat.shape[-1]
    unit = PACK_WIDTH * PACK_ROWS
    pad = (-n) % unit
    flat = jnp.pad(flat, [(0, 0)] * nlead + [(0, pad)])
    return flat.reshape(lead + ((n + pad) // PACK_WIDTH, PACK_WIDTH))


def _unpack(buf, shapes, lead=()):
    flat = buf.reshape(lead + (-1,))
    out, off = [], 0
    for shp in shapes:
        n = math.prod(shp)
        out.append(flat[..., off:off + n].reshape(lead + tuple(shp)))
        off += n
    return out


def _adam_math(w, g, m, v):
    m = ADAM_B1 * m + (1.0 - ADAM_B1) * g
    v = ADAM_B2 * v + (1.0 - ADAM_B2) * (g * g)
    m_hat = m / (1.0 - ADAM_B1 ** ADAM_STEP)
    v_hat = v / (1.0 - ADAM_B2 ** ADAM_STEP)
    delta = -ADAM_LR * (m_hat / (jnp.sqrt(v_hat) + ADAM_EPS) + ADAM_WD * w)
    return delta, m, v


def _sum_adam(parts, w, m, v, name):
    r, c = w.shape
    nparts = parts.shape[0]
    lanes = -(-c // 128) * 128
    tr = _pick_rows(r, max(8, (6 * 1024 * 1024) // (nparts * lanes * 4)))

    def body(p_ref, w_ref, m_ref, v_ref, g_ref, d_ref, nm_ref, nv_ref):
        g = p_ref[0].astype(F32)
        for j in range(1, nparts):
            g = g + p_ref[j].astype(F32)
        d, nm, nv = _adam_math(w_ref[...], g, m_ref[...], v_ref[...])
        g_ref[...] = g
        d_ref[...] = d
        nm_ref[...] = nm
        nv_ref[...] = nv

    row = pl.BlockSpec((tr, c), lambda i: (i, 0))
    return pl.pallas_call(
        body, name=name, grid=(r // tr,), in_specs=[pl.BlockSpec((nparts, tr, c), lambda i: (0, i, 0)), row, row, row],
        out_specs=[row] * 4, out_shape=[jax.ShapeDtypeStruct((r, c), F32)] * 4,
        compiler_params=_cparams(("parallel",)))(parts, w, m, v)


def _block_diag(t):
    nb, g, a, b = t.shape
    eye = jnp.eye(g, dtype=t.dtype)
    return jnp.einsum('ngab,gh->ngahb', t, eye).reshape(nb, g * a, g * b)


def _diag_blocks(t, a, b):
    nb = t.shape[0]
    g = S5_GROUPS_PER_BLOCK
    t = t.reshape(nb, g, a, g, b)
    return jnp.stack([t[:, j, :, j, :] for j in range(g)], axis=1)


def _local_step(x, mem, target, p):
    s, d = x.shape
    gw = d // 2
    nh = gw // GDN_HEAD_DIM
    ng = gw // S5_GROUP
    nb = ng // S5_GROUPS_PER_BLOCK
    nl = ng * S5_STATE
    ts = min(256, s)
    nt = s // ts
    grads = {}

    w_main, w_ba = p['w_main'], p['w_ba']
    CB_ZA, CB_XB, CB_ZB, CB_QC, CB_ZC, CB_G = 3, 4, 5, 6, 7, 8

    u = _tile_fwd(_rms, "rms_fwd", nt, [_rt(x, ts)], [p['norm_g']],
                  [((s, d), BF16, (ts, d), lambda i: (i, 0))])[0]
    proj = _mm(u, w_main, tm=1024, tn=2048, tk=512, name="proj_main")
    pba = _mm(u, w_ba, name="proj_ba")

    conv_w = p['conv_w']
    col = lambda arr, cb: (arr, (s, GDN_HEAD_DIM), lambda i, cb=cb: (0, cb + i))
    qkv = []
    for j, mode in enumerate(('q', 'k', 'v')):
        off = j * nh
        qkv.append(_tile_fwd(
            _gdn_pre(mode), "gdn_pre_" + mode, nh, [col(proj, off), (conv_w, (CONV_WIDTH, GDN_HEAD_DIM), lambda i, off=off: (0, off + i))],
            [], [((s, gw), F32, (s, GDN_HEAD_DIM), lambda i: (0, i))])[0])
    q, k, v = qkv
    lane = jnp.arange(128)[:, None]
    colh = jnp.arange(gw)[None, :] // GDN_HEAD_DIM
    e_beta = (lane == colh).astype(F32)
    e_g = (lane == colh + nh).astype(F32)
    alog_row = jnp.pad(p['gdn_a_log'], ((0, 0), (nh, 128 - 2 * nh)))
    dtb_row = jnp.pad(p['gdn_dt_bias'], ((0, 0), (nh, 128 - 2 * nh)))
    row_gw = lambda: ((s, gw), F32, (ts, gw), lambda i: (i, 0))
    betab, gb = _tile_fwd(_gdn_gates, "gdn_gates", nt, [_rt(pba, ts)], [alog_row, dtb_row, e_beta, e_g],
                          [row_gw(), row_gw()])
    *intra, t_inv = _gdn_intra_fwd(q, k, v, gb, betab)
    o_raw, states = _gdn_inter_fwd(*intra, gb)
    ga = _tile_fwd(_gdn_post, "gdn_post", nt, [_rt(o_raw, ts), _rt(proj, ts, CB_ZA, gw)], [p['gdn_norm_g']],
                   [((s, gw), BF16, (ts, gw), lambda i: (i, 0))])[0]

    e_rep = (jnp.arange(S5_STATE)[:, None] == jnp.arange(S5_STATE * S5_GROUP)[None, :] // S5_GROUP).astype(F32)
    s5_in = [p['s5_lambda_re'], p['s5_lambda_im'], p['s5_log_dt'].reshape(ng, 1),
             p['s5_b_re'].reshape(ng, S5_STATE * S5_GROUP), p['s5_b_im'].reshape(ng, S5_STATE * S5_GROUP), e_rep]
    one = lambda shp: (shp, F32, shp, lambda i, n=len(shp): (0,) * n)
    ab_re, ab_im, bbr, bbi = _tile_fwd(_s5_params, "s5_params", 1, [], s5_in,
                                       [one((ng, S5_STATE)), one((ng, S5_STATE)), one((ng, S5_STATE * S5_GROUP)),
                                        one((ng, S5_STATE * S5_GROUP))])
    coef = _s5_coef(ab_re.reshape(1, nl), ab_im.reshape(1, nl))
    to_bd_b = lambda t: _block_diag(t.reshape(nb, S5_GROUPS_PER_BLOCK, S5_STATE, S5_GROUP).transpose(0, 1, 3, 2))
    to_bd_c = lambda t: _block_diag(t.reshape(nb, S5_GROUPS_PER_BLOCK, S5_GROUP, S5_STATE).transpose(0, 1, 3, 2))
    bbd_re, bbd_im = to_bd_b(bbr).astype(BF16), to_bd_b(bbi).astype(BF16)
    cbd_re, cbd_im = to_bd_c(p['s5_c_re']).astype(BF16), to_bd_c(p['s5_c_im']).astype(BF16)
    xb_arr = lax.slice_in_dim(proj, CB_XB * gw, (CB_XB + 1) * gw, axis=1)
    h_re, h_im, ylin = _s5_fwd(xb_arr, bbd_re, bbd_im, cbd_re, cbd_im, coef, ts)
    gl = _tile_fwd(_s5_post1, "s5_post1", nt, [_rt(ylin, ts), _rt(proj, ts, CB_XB, gw)], [p['s5_d']],
                   [((s, gw), BF16, (ts, gw), lambda i: (i, 0))])[0]
    tglu = _mm(gl, p['s5_w_glu'], b_shards=True, name="s5_glu")
    gbb = _tile_fwd(_s5_post2, "s5_post2", nt, [_rt(tglu, ts), _rt(proj, ts, CB_ZB, gw)], [],
                    [((s, gw), BF16, (ts, gw), lambda i: (i, 0))])[0]

    m_len = mem.shape[0]
    mem_n = _tile_fwd(_rms, "mem_rms", 1, [_rt(mem, m_len)], [p['mem_norm_g']],
                      [((m_len, d), BF16, (m_len, d), lambda i: (i, 0))])[0]
    kv = _mm(mem_n, p['w_kv_mem'], name="mem_kv")
    gcc = _tile_fwd(_attn, "attn", nt, [_rt(proj, ts, CB_QC, gw), _rt(proj, ts, CB_ZC, gw)], [kv],
                    [((s, gw), BF16, (ts, gw), lambda i: (i, 0))])[0]

    p_a = _mm(ga, p['w_br_a'], b_shards=True, name="br_a")
    p_b = _mm(gbb, p['w_br_b'], b_shards=True, name="br_b")
    p_c = _mm(gcc, p['w_br_c'], b_shards=True, name="br_c")
    gate_acts = [_rt(proj, ts, CB_G // 2 + j, d) for j in range(3)]
    merged = _tile_fwd(_merge, "merge", nt, gate_acts + [_rt(p_a, ts), _rt(p_b, ts), _rt(p_c, ts)], [],
                       [((s, d), BF16, (ts, d), lambda i: (i, 0))])[0]
    mo = _mm(merged, p['w_out'], name="out_proj")
    dh, dfg, loss = _final(x, mo, target, p['final_g'].reshape(1, d), ts)
    grads['final_g'] = dfg.reshape(d)

    dmerged = _mm(dh, p['w_out'], tb=True, name="d_merged")
    grads['w_out'] = _mm(merged, dh, ta=True, name="dw_out")
    row_d = lambda dt: ((s, d), dt, (ts, d), lambda i: (i, 0))
    dg0, dg1, dg2, dpa, dpb, dpc = _tile_bwd(
        _merge, "merge_bwd", nt, gate_acts + [_rt(p_a, ts), _rt(p_b, ts), _rt(p_c, ts)], [], [_rt(dmerged, ts)],
        [row_d(BF16)] * 6, [])
    dga = _mm(dpa, p['w_br_a'], tb=True, b_shards=True, name="d_ga")
    dgbb = _mm(dpb, p['w_br_b'], tb=True, b_shards=True, name="d_gb")
    dgcc = _mm(dpc, p['w_br_c'], tb=True, b_shards=True, name="d_gc")
    grads['w_br_a'] = _mm(ga, dpa, ta=True, out_shards=N_DEV, name="dw_br_a")
    grads['w_br_b'] = _mm(gbb, dpb, ta=True, out_shards=N_DEV, name="dw_br_b")
    grads['w_br_c'] = _mm(gcc, dpc, ta=True, out_shards=N_DEV, name="dw_br_c")
    row_h = lambda dt: ((s, gw), dt, (ts, gw), lambda i: (i, 0))

    dqc, dzc, dkv = _tile_bwd(_attn, "attn_bwd", nt, [_rt(proj, ts, CB_QC, gw), _rt(proj, ts, CB_ZC, gw)], [kv],
                              [_rt(dgcc, ts)], [row_h(BF16), row_h(BF16)], [True])
    grads['w_kv_mem'] = _mm(mem_n, dkv, ta=True, name="dw_kv")
    dmem_n = _mm(dkv, p['w_kv_mem'], tb=True, name="d_mem_n")
    grads['mem_norm_g'] = _tile_bwd(_rms, "mem_rms_bwd", 1, [_rt(mem, m_len)], [p['mem_norm_g']],
                                    [_rt(dmem_n, m_len)], [None], [True])[0]

    dtglu, dzb = _tile_bwd(_s5_post2, "s5_post2_bwd", nt, [_rt(tglu, ts), _rt(proj, ts, CB_ZB, gw)], [],
                           [_rt(dgbb, ts)], [((s, 2 * gw), BF16, (ts, 2 * gw), lambda i: (i, 0)), row_h(BF16)], [])
    grads['s5_w_glu'] = _mm(gl, dtglu, ta=True, out_shards=N_DEV, name="dw_glu")
    dgl = _mm(dtglu, p['s5_w_glu'], tb=True, b_shards=True, name="d_gl")
    dylin, dxb1, dd = _tile_bwd(_s5_post1, "s5_post1_bwd", nt, [_rt(ylin, ts), _rt(proj, ts, CB_XB, gw)],
                                [p['s5_d']], [_rt(dgl, ts)], [row_h(F32), row_h(F32)], [True])
    grads['s5_d'] = dd
    dxb2, dbbd_re, dbbd_im, dcbd_re, dcbd_im, da_re, da_im = _s5_bwd(dylin, xb_arr, h_re, h_im, bbd_re, bbd_im,
                                                                    cbd_re, cbd_im, coef, ts)
    from_bd_b = lambda t: _diag_blocks(t, S5_GROUP, S5_STATE).transpose(0, 1, 3, 2).reshape(ng, S5_STATE * S5_GROUP)
    from_bd_c = lambda t: _diag_blocks(t, S5_STATE, S5_GROUP).transpose(0, 1, 3, 2).reshape(1, ng, S5_GROUP, S5_STATE)
    grads['s5_c_re'], grads['s5_c_im'] = from_bd_c(dcbd_re), from_bd_c(dcbd_im)
    s5_cts = [jnp.sum(da_re, axis=0).reshape(ng, S5_STATE), jnp.sum(da_im, axis=0).reshape(ng, S5_STATE),
              from_bd_b(dbbd_re), from_bd_b(dbbd_im)]
    dlr, dli, dlogdt, dbr, dbi = _tile_bwd(_s5_params, "s5_params_bwd", 1, [], s5_in,
                                           [(c, c.shape, lambda i: (0, 0)) for c in s5_cts], [],
                                           [True, True, True, True, True, False])
    grads['s5_lambda_re'], grads['s5_lambda_im'] = dlr[None], dli[None]
    grads['s5_log_dt'] = dlogdt.reshape(1, ng)
    grads['s5_b_re'] = dbr.reshape(1, ng, S5_STATE, S5_GROUP)
    grads['s5_b_im'] = dbi.reshape(1, ng, S5_STATE, S5_GROUP)
    dxb = (dxb1 + dxb2).astype(BF16)

    do_raw, dza, dgng = _tile_bwd(_gdn_post, "gdn_post_bwd", nt, [_rt(o_raw, ts), _rt(proj, ts, CB_ZA, gw)],
                                  [p['gdn_norm_g']], [_rt(dga, ts)], [row_h(F32), row_h(BF16)], [True])
    grads['gdn_norm_g'] = dgng
    *intra_cts, dgb_inter = _gdn_inter_bwd(*intra, gb, states, do_raw)
    dq, dk, dv, dgb, dbetab = _gdn_intra_bwd(q, k, v, gb, betab, t_inv, intra_cts, dgb_inter)
    dpba, dalog, ddtb = _tile_bwd(_gdn_gates, "gdn_gates_bwd", nt, [_rt(pba, ts)], [alog_row, dtb_row, e_beta, e_g],
                                  [_rt(dbetab, ts), _rt(dgb, ts)], [((s, 128), BF16, (ts, 128), lambda i: (i, 0))],
                                  [True, True, False, False])
    grads['gdn_a_log'] = dalog[:, nh:2 * nh]
    grads['gdn_dt_bias'] = ddtb[:, nh:2 * nh]
    dqkv, dconv = [], []
    for j, (mode, ct) in enumerate((('q', dq), ('k', dk), ('v', dv))):
        off = j * nh
        wspec = (conv_w, (CONV_WIDTH, GDN_HEAD_DIM), lambda i, off=off: (0, off + i))
        dxc, dwc = _tile_bwd(
            _gdn_pre(mode), "gdn_pre_bwd_" + mode, nh, [col(proj, off), wspec], [], [col(ct, 0)],
            [((s, gw), BF16, (s, GDN_HEAD_DIM), lambda i: (0, i)),
             ((CONV_WIDTH, gw), F32, (CONV_WIDTH, GDN_HEAD_DIM), lambda i: (0, i))], [])
        dqkv.append(dxc)
        dconv.append(dwc)
    grads['conv_w'] = jnp.concatenate(dconv, axis=1)

    dproj = jnp.concatenate(dqkv + [dza, dxb, dzb, dqc, dzc, dg0, dg1, dg2], axis=1)
    du = _mm(dpba, w_ba, tb=True, name="du_ba")
    du = _mm(dproj, w_main, tb=True, addend=du, tm=512, tn=2048, tk=1024, name="du_main")
    grads['w_main'] = _mm(u, dproj, ta=True, tm=1024, tn=2048, tk=512, name="dw_main")
    grads['w_ba'] = _mm(u, dpba, ta=True, name="dw_ba")
    dx, dng = _tile_bwd(_rms, "rms_bwd", nt, [_rt(x, ts)], [p['norm_g']], [_rt(du, ts)], [row_d(F32)], [True])
    grads['norm_g'] = dng
    return loss, dx + dh, grads


def _to_shards(name, g):
    if SHARDED[name] == 'row':
        return g.reshape((N_DEV, g.shape[0] // N_DEV) + g.shape[1:])
    r, c = g.shape
    return g.reshape(r, N_DEV, c // N_DEV).transpose(1, 0, 2)


def _from_shards(name, t):
    if SHARDED[name] == 'row':
        return t.reshape((t.shape[0] * t.shape[1],) + t.shape[2:])
    n, r, c = t.shape
    return t.transpose(1, 0, 2).reshape(r, n * c)


def _step(x, mem, target, w, m, v):
    sharded = list(SHARDED)
    shard_shapes = {n: tuple(w[n].shape[1:]) for n in sharded}
    d = x.shape[-1]
    ba_lo = 2 * d
    ba_hi = ba_lo + 2 * (d // 2 // GDN_HEAD_DIM)

    srcs = [w[n][0].astype(BF16) for n in GATHER_BF16] + [w['conv_w'][0]]
    got = _gather(srcs, "gather_weights")
    full = {n: t for n, t in zip(GATHER_BF16 + ['conv_w'], got)}
    full['w_main'], full['w_ba'] = _w_in_from_shards(full.pop('w_in'), ba_lo, ba_hi)
    for n in ('w_kv_mem', 'w_out', 'conv_w'):
        full[n] = _from_shards(n, full[n])
    for n in REPLICATED:
        full[n] = w[n]
    for n in ('s5_lambda_re', 's5_lambda_im', 's5_c_re', 's5_c_im'):
        full[n] = w[n][0]

    loss, grad_x, grads = _local_step(x[0], mem[0], target[0], full)

    grads['w_in'] = _w_in_to_shards(grads.pop('w_main'), grads.pop('w_ba'), ba_lo, ba_hi)
    for n in ('w_kv_mem', 'w_out', 'conv_w'):
        grads[n] = _to_shards(n, grads[n])
    gs = [grads[n] for n in sharded]
    c_idx = lax.axis_index("c").astype(jnp.int32).reshape(1)
    from_sibling = _pair_scatter(gs, "scatter_pair")
    chip_sums = [_pair_sum(g, got, c_idx, BF16 if n in GATHER_BF16 else F32, "pair_sum_" + n)
                 for n, g, got in zip(sharded, gs, from_sibling)]
    parts = _cross_scatter(chip_sums, "scatter_cross")
    res = {}
    for n, part in zip(sharded, parts):
        outs = _sum_adam(part, w[n][0], m[n][0], v[n][0], name="adam_" + n)
        for kind, t in zip(('grad', 'delta', 'new_m', 'new_v'), outs):
            res[kind, n] = t[None]

    small = _pack([grads[n].reshape(w[n].shape) for n in REPLICATED] + [loss[:1, :1]], F32)
    allp = _gather([small], "gather_small")[0]
    zero = jnp.zeros((1, 1), F32)
    outs = _sum_adam(allp, *[_pack([t[n] for n in REPLICATED] + [zero], F32) for t in (w, m, v)], name="adam_small")
    shapes = [w[n].shape for n in REPLICATED] + [(1, 1)]
    for kind, buf in zip(('grad', 'delta', 'new_m', 'new_v'), outs):
        got = _unpack(buf, shapes)
        for n, t in zip(REPLICATED, got):
            res[kind, n] = t
        if kind == 'grad':
            total_loss = got[-1].reshape(())
    out = [total_loss, grad_x[None]]
    for kind in ('grad', 'delta', 'new_m', 'new_v'):
        out += [res[kind, n] for n in WEIGHTS]
    return tuple(out)


def kernel(x, mem, norm_g, w_in, conv_w, gdn_a_log, gdn_dt_bias, gdn_norm_g, s5_lambda_re, s5_lambda_im, s5_log_dt, s5_b_re, s5_b_im, s5_c_re, s5_c_im, s5_d, s5_w_glu, mem_norm_g, w_kv_mem, w_br_a, w_br_b, w_br_c, w_out, final_g, loss_target, m_norm_g, m_w_in, m_conv_w, m_gdn_a_log, m_gdn_dt_bias, m_gdn_norm_g, m_s5_lambda_re, m_s5_lambda_im, m_s5_log_dt, m_s5_b_re, m_s5_b_im, m_s5_c_re, m_s5_c_im, m_s5_d, m_s5_w_glu, m_mem_norm_g, m_w_kv_mem, m_w_br_a, m_w_br_b, m_w_br_c, m_w_out, m_final_g, v_norm_g, v_w_in, v_conv_w, v_gdn_a_log, v_gdn_dt_bias, v_gdn_norm_g, v_s5_lambda_re, v_s5_lambda_im, v_s5_log_dt, v_s5_b_re, v_s5_b_im, v_s5_c_re, v_s5_c_im, v_s5_d, v_s5_w_glu, v_mem_norm_g, v_w_kv_mem, v_w_br_a, v_w_br_b, v_w_br_c, v_w_out, v_final_g):
    a = dict(locals())
    w = {n: a[n] for n in WEIGHTS}
    m = {n: a['m_' + n] for n in WEIGHTS}
    v = {n: a['v_' + n] for n in WEIGHTS}
    return _step(x, mem, loss_target, w, m, v)
```

```python
import functools
import math

import jax
import jax.numpy as jnp
from jax import lax
from jax.experimental import pallas as pl
from jax.experimental.pallas import tpu as pltpu

F32 = jnp.float32
BF16 = jnp.bfloat16
HI = lax.Precision.HIGHEST

EPS = 1e-6
CHUNK = 64
GDN_HEAD_DIM = 128
CONV_WIDTH = 4
S5_GROUP = 16
S5_STATE = 64
S5_GROUPS_PER_BLOCK = 8
XA_HEADS = 4
N_DEV = 8
ADAM_LR, ADAM_B1, ADAM_B2, ADAM_EPS, ADAM_WD, ADAM_STEP = 0.001, 0.9, 0.999, 1e-08, 0.01, 10

VMEM_LIMIT_BYTES = 56 * 1024 * 1024
SCAN_LANES = 512
PACK_WIDTH = 512
PACK_ROWS = 256

WEIGHTS = ['norm_g', 'w_in', 'conv_w', 'gdn_a_log', 'gdn_dt_bias', 'gdn_norm_g', 's5_lambda_re', 's5_lambda_im',
           's5_log_dt', 's5_b_re', 's5_b_im', 's5_c_re', 's5_c_im', 's5_d', 's5_w_glu', 'mem_norm_g', 'w_kv_mem',
           'w_br_a', 'w_br_b', 'w_br_c', 'w_out', 'final_g']
SHARDED = {'w_in': 'col', 'conv_w': 'col', 's5_w_glu': 'col', 'w_kv_mem': 'row', 'w_br_a': 'col', 'w_br_b': 'col',
           'w_br_c': 'col', 'w_out': 'row'}
GATHER_BF16 = ['w_in', 's5_w_glu', 'w_kv_mem', 'w_br_a', 'w_br_b', 'w_br_c', 'w_out']
REPLICATED = [n for n in WEIGHTS if n not in SHARDED]
OVERLAPPED = ['s5_w_glu', 'w_kv_mem', 'w_br_a', 'w_br_b', 'w_br_c', 'w_out']


def _cparams(sem=None):
    return pltpu.CompilerParams(dimension_semantics=sem, vmem_limit_bytes=VMEM_LIMIT_BYTES)


def _pick(dim, pref):
    t = (min(pref, dim) // 128) * 128
    while t >= 128:
        if dim % t == 0:
            return t
        t -= 128
    return dim


def _make_dots(prep, precision):
    def raw(a, b, dims):
        return lax.dot_general(prep(a), prep(b), (dims, ((), ())), preferred_element_type=F32, precision=precision)

    @jax.custom_vjp
    def nn(a, b):
        return raw(a, b, ((1,), (0,)))

    @jax.custom_vjp
    def nt(a, b):
        return raw(a, b, ((1,), (1,)))

    @jax.custom_vjp
    def tn(a, b):
        return raw(a, b, ((0,), (0,)))

    nn.defvjp(lambda a, b: (nn(a, b), (a, b)), lambda r, ct: (nt(ct, r[1]), tn(r[0], ct)))
    nt.defvjp(lambda a, b: (nt(a, b), (a, b)), lambda r, ct: (nn(ct, r[1]), tn(ct, r[0])))
    tn.defvjp(lambda a, b: (tn(a, b), (a, b)), lambda r, ct: (nt(r[1], ct), nn(r[0], ct)))
    return nn, nt, tn


_bnn, _bnt, _btn = _make_dots(lambda a: a.astype(BF16), None)
_hnn, _hnt, _htn = _make_dots(lambda a: a.astype(F32), HI)
_mnn, _mnt, _mtn = _make_dots(lambda a: a.astype(F32), lax.Precision.HIGH)


def _mm(a, b, *, name, ta=False, tb=False, out_dtype=F32, addend=None, tm=512, tn=1024, tk=1024, b_shards=False,
        out_shards=0):
    m, k = (a.shape[1], a.shape[0]) if ta else a.shape
    brows, bcols = (b.shape[1], b.shape[0] * b.shape[2]) if b_shards else b.shape
    n = brows if tb else bcols
    assert (bcols if tb else brows) == k, (a.shape, b.shape, ta, tb)
    tm, tn, tk = _pick(m, tm), _pick(n, tn), _pick(k, tk)
    bcs = ocs = 0
    if b_shards:
        bcs = b.shape[2]
        assert bcs % 128 == 0 and (tk if tb else tn) % bcs == 0
    if out_shards:
        ocs = n // out_shards
        assert ocs % 128 == 0 and tn % ocs == 0
    nk = k // tk
    dims = ((0 if ta else 1,), (1 if tb else 0,))

    def body(*refs):
        if addend is None:
            a_ref, b_ref, o_ref, acc_ref = refs
        else:
            a_ref, b_ref, add_ref, o_ref, acc_ref = refs
        kk = pl.program_id(2)

        @pl.when(kk == 0)
        def _():
            acc_ref[...] = jnp.zeros_like(acc_ref)

        dot = lambda x, y: lax.dot_general(x.astype(BF16), y.astype(BF16), (dims, ((), ())), preferred_element_type=F32)
        if not b_shards:
            acc_ref[...] += dot(a_ref[...], b_ref[...])
        elif tb:
            for g in range(tk // bcs):
                acc_ref[...] += dot(a_ref[:, g * bcs:(g + 1) * bcs], b_ref[g])
        else:
            for g in range(tn // bcs):
                acc_ref[:, g * bcs:(g + 1) * bcs] += dot(a_ref[...], b_ref[g])

        @pl.when(kk == nk - 1)
        def _():
            r = acc_ref[...]
            if addend is not None:
                r = r + add_ref[...].astype(F32)
            if out_shards:
                for g in range(tn // ocs):
                    o_ref[g] = r[:, g * ocs:(g + 1) * ocs].astype(o_ref.dtype)
            else:
                o_ref[...] = r.astype(o_ref.dtype)

    a_spec = pl.BlockSpec((tk, tm), lambda i, j, kk: (kk, i)) if ta else pl.BlockSpec((tm, tk), lambda i, j, kk: (i, kk))
    if b_shards:
        b_spec = (pl.BlockSpec((tk // bcs, tn, bcs), lambda i, j, kk: (kk, j, 0)) if tb
                  else pl.BlockSpec((tn // bcs, tk, bcs), lambda i, j, kk: (j, kk, 0)))
    else:
        b_spec = (pl.BlockSpec((tn, tk), lambda i, j, kk: (j, kk)) if tb
                  else pl.BlockSpec((tk, tn), lambda i, j, kk: (kk, j)))
    if out_shards:
        o_spec = pl.BlockSpec((tn // ocs, tm, ocs), lambda i, j, kk: (j, i, 0))
        out_shape = jax.ShapeDtypeStruct((out_shards, m, ocs), out_dtype)
    else:
        o_spec = pl.BlockSpec((tm, tn), lambda i, j, kk: (i, j))
        out_shape = jax.ShapeDtypeStruct((m, n), out_dtype)
    in_specs = [a_spec, b_spec] + ([o_spec] if addend is not None else [])
    args = (a, b) + ((addend,) if addend is not None else ())
    return pl.pallas_call(
        body, name=name, grid=(m // tm, n // tn, nk), in_specs=in_specs, out_specs=o_spec,
        out_shape=out_shape, scratch_shapes=[pltpu.VMEM((tm, tn), F32)],
        compiler_params=_cparams(("parallel", "parallel", "arbitrary")))(*args)


def _rt(arr, ts, cb=0, w=None):
    w = arr.shape[1] if w is None else w
    return (arr, (ts, w), lambda i, cb=cb: (i, cb))


def _whole(p):
    return pl.BlockSpec(p.shape, lambda i, nd=p.ndim: (0,) * nd)


def _tile_fwd(f, name, n, acts, params, outs):
    na, npar = len(acts), len(params)

    def body(*refs):
        res = f(*[r[...] for r in refs[:na + npar]])
        for r, v in zip(refs[na + npar:], res):
            r[...] = v.astype(r.dtype)

    in_specs = [pl.BlockSpec(b, m) for _, b, m in acts] + [_whole(p) for p in params]
    out = pl.pallas_call(
        body, name=name, grid=(n,), in_specs=in_specs,
        out_specs=[pl.BlockSpec(b, m) for _, _, b, m in outs],
        out_shape=[jax.ShapeDtypeStruct(s, d) for s, d, _, _ in outs],
        compiler_params=_cparams(("parallel",)))(*[a for a, _, _ in acts], *params)
    return out


def _tile_bwd(f, name, n, acts, params, cts, agrads, pgrads):
    na, npar, nc = len(acts), len(params), len(cts)

    def body(*refs):
        i = pl.program_id(0)
        ins = [r[...] for r in refs[:na + npar]]
        outs, vjp = jax.vjp(f, *ins)
        g = vjp(tuple(c[...].astype(o.dtype) for c, o in zip(refs[na + npar:na + npar + nc], outs)))
        orefs = refs[na + npar + nc:]
        k = 0
        for j in range(na):
            if agrads[j] is not None:
                orefs[k][...] = g[j].astype(orefs[k].dtype)
                k += 1
        for j in range(npar):
            if pgrads[j]:
                o = orefs[k]

                @pl.when(i == 0)
                def _(o=o):
                    o[...] = jnp.zeros_like(o)

                o[...] += g[na + j].astype(F32)
                k += 1

    in_specs = ([pl.BlockSpec(b, m) for _, b, m in acts] + [_whole(p) for p in params]
                + [pl.BlockSpec(b, m) for _, b, m in cts])
    out_specs = [pl.BlockSpec(g[2], g[3]) for g in agrads if g is not None]
    out_shape = [jax.ShapeDtypeStruct(g[0], g[1]) for g in agrads if g is not None]
    for p, flag in zip(params, pgrads):
        if flag:
            out_specs.append(_whole(p))
            out_shape.append(jax.ShapeDtypeStruct(p.shape, F32))
    return pl.pallas_call(
        body, name=name, grid=(n,), in_specs=in_specs, out_specs=out_specs, out_shape=out_shape,
        compiler_params=_cparams(("arbitrary",)))(*[a for a, _, _ in acts], *params, *[c for c, _, _ in cts])


def _silu(x):
    return x * jax.nn.sigmoid(x)


def _rms(x, g):
    x = x.astype(F32)
    return (x * lax.rsqrt(jnp.mean(x * x, axis=-1, keepdims=True) + EPS) * g,)


def _shift_down(x, s):
    row = lax.broadcasted_iota(jnp.int32, x.shape, 0)
    return jnp.where(row >= s, pltpu.roll(x, s, 0), 0.0)


def _shift_up(x, s):
    n = x.shape[0]
    row = lax.broadcasted_iota(jnp.int32, x.shape, 0)
    return jnp.where(row < n - s, pltpu.roll(x, n - s, 0), 0.0)


@functools.partial(jax.custom_vjp, nondiff_argnums=(1,))
def _shift(x, s):
    return _shift_down(x, s)


_shift.defvjp(lambda x, s: (_shift_down(x, s), None), lambda s, _, ct: (_shift_up(ct, s),))


def _gdn_pre(mode):
    def f(x, w):
        y = x * w[CONV_WIDTH - 1:CONV_WIDTH, :]
        for j in range(CONV_WIDTH - 1):
            y = y + _shift(x, CONV_WIDTH - 1 - j) * w[j:j + 1, :]
        y = _silu(y)
        if mode != 'v':
            y = y * lax.rsqrt(jnp.sum(y * y, axis=-1, keepdims=True) + EPS)
        if mode == 'q':
            y = y * (GDN_HEAD_DIM ** -0.5)
        return (y,)
    return f


def _softplus(x):
    return jnp.maximum(x, 0.0) + jnp.log1p(jnp.exp(-jnp.abs(x)))


def _gdn_gates(ba, alog, dtb, e_beta, e_g):
    beta = jax.nn.sigmoid(ba)
    g = -jnp.exp(alog) * _softplus(ba + dtb)
    return _hnn(beta, lax.stop_gradient(e_beta)), _hnn(g, lax.stop_gradient(e_g))


@jax.custom_vjp
def _inverse_known(neg, t):
    return t


_inverse_known.defvjp(lambda neg, t: (t, t), lambda t, ct: (_mtn(t, _mnt(ct, t)), jnp.zeros_like(t)))


def _gdn_intra(q, k, v, gb, bb, t_known=None):
    n, c = len(q), q[0].shape[0]
    ri = lax.broadcasted_iota(jnp.int32, (c, c), 0)
    ci = lax.broadcasted_iota(jnp.int32, (c, c), 1)
    incl, strict = ri >= ci, ri > ci
    tri = incl.astype(F32)
    eye = (ri == ci).astype(F32)
    each = range(n)
    gc = [_hnn(tri, gb[i]) for i in each]
    decay = [jnp.exp(jnp.where(incl, gc[i][:, :c] - gc[i].T[:c, :], -1e30)) for i in each]
    kb = [k[i] * bb[i] for i in each]
    kk = [_bnt(kb[i], k[i]) for i in each]
    qk = [_bnt(q[i], k[i]) for i in each]
    p = [jnp.where(strict, -(kk[i] * decay[i]), 0.0) for i in each]
    if t_known is None:
        t = [eye + p[i] for i in each]
        for _ in range(int(math.log2(c)) - 1):
            p = [_mnn(p[i], p[i]) for i in each]
            tp = [_mnn(t[i], p[i]) for i in each]
            t = [t[i] + tp[i] for i in each]
    else:
        t = [_inverse_known(p[i], t_known[i]) for i in each]
    egc = [jnp.exp(gc[i]) for i in each]
    u_val = [_mnn(t[i], v[i] * bb[i]) for i in each]
    w_dec = [_mnn(t[i], kb[i] * egc[i]) for i in each]
    qk = [qk[i] * decay[i] for i in each]
    gl = [jnp.sum(gb[i], axis=0, keepdims=True) for i in each]
    return w_dec, u_val, qk, [q[i] * egc[i] for i in each], [k[i] * jnp.exp(gl[i] - gc[i]) for i in each], t


def _gdn_inter(w_dec, u_val, qk, q_dec, k_dec, gb, state):
    each = range(len(state))
    ws = [_bnn(w_dec[i], state[i]) for i in each]
    qs = [_bnn(q_dec[i], state[i]) for i in each]
    v_new = [u_val[i] - ws[i] for i in each]
    qv = [_bnn(qk[i], v_new[i]) for i in each]
    kv = [_btn(k_dec[i], v_new[i]) for i in each]
    decayed = [state[i] * jnp.exp(jnp.sum(gb[i], axis=0, keepdims=True)) for i in each]
    return [qs[i] + qv[i] for i in each], [decayed[i] + kv[i] for i in each]


def _gdn_post(o, z, g):
    parts = []
    for h in range(o.shape[1] // GDN_HEAD_DIM):
        oh = o[:, h * GDN_HEAD_DIM:(h + 1) * GDN_HEAD_DIM]
        parts.append(oh * lax.rsqrt(jnp.mean(oh * oh, axis=-1, keepdims=True) + EPS) * g)
    y = parts[0] if len(parts) == 1 else jnp.concatenate(parts, axis=1)
    return (y * _silu(z),)


def _gelu(x):
    return 0.5 * x * (1.0 + jnp.tanh(0.7978845608028654 * (x + 0.044715 * x * x * x)))


def _s5_post1(ylin, xb, d):
    return (_gelu(ylin + d * xb),)


def _s5_post2(t, z):
    w = t.shape[1] // 2
    return (t[:, :w] * jax.nn.sigmoid(t[:, w:]) * _silu(z),)


def _attn(q, z, kv):
    w = q.shape[1]
    hd = w // XA_HEADS
    parts = []
    for h in range(XA_HEADS):
        s = _bnt(q[:, h * hd:(h + 1) * hd], kv[:, h * hd:(h + 1) * hd]) * (hd ** -0.5)
        s = s - jnp.max(s, axis=-1, keepdims=True)
        e = jnp.exp(s)
        p = e / jnp.sum(e, axis=-1, keepdims=True)
        parts.append(_bnn(p, kv[:, w + h * hd:w + (h + 1) * hd]))
    return (jnp.concatenate(parts, axis=1) * _silu(z),)


def _merge(g0, g1, g2, pa, pb, pc):
    return (jax.nn.sigmoid(g0) * pa + jax.nn.sigmoid(g1) * pb + jax.nn.sigmoid(g2) * pc,)


def _s5_params(lr, li, logdt, br, bi, e):
    dt = jnp.exp(logdt)
    mag = jnp.exp(lr * dt)
    ab_re, ab_im = mag * jnp.cos(li * dt), mag * jnp.sin(li * dt)
    den = lr * lr + li * li
    nr, ni = ab_re - 1.0, ab_im
    e = lax.stop_gradient(e)
    cre = _hnn((nr * lr + ni * li) / den, e)
    cim = _hnn((ni * lr - nr * li) / den, e)
    return ab_re, ab_im, cre * br - cim * bi, cre * bi + cim * br


def _gdn_blocks(s, w, per_step):
    nh, nc = w // GDN_HEAD_DIM, s // CHUNK
    cpb = math.gcd(per_step, nc)
    return nh, nc, cpb, nc // cpb, (cpb * CHUNK, w), (cpb * CHUNK, nh * CHUNK)


def _gdn_pairs(cpb, nh):
    wide, narrow = [], []
    for cb in range(cpb):
        rows = slice(cb * CHUNK, (cb + 1) * CHUNK)
        for h in range(nh):
            wide.append((rows, slice(h * GDN_HEAD_DIM, (h + 1) * GDN_HEAD_DIM)))
            narrow.append((rows, slice(h * CHUNK, (h + 1) * CHUNK)))
    return wide, narrow


def _gdn_intra_fwd(q, k, v, gb, bb, per_step=4):
    s, w = q.shape
    nh, nc, cpb, n, wide, narrow = _gdn_blocks(s, w, per_step)

    def body(q_ref, k_ref, v_ref, g_ref, b_ref, wd_ref, uv_ref, qk_ref, qd_ref, kd_ref, t_ref):
        wide, narrow = _gdn_pairs(cpb, nh)
        res = _gdn_intra(*[[r[ix] for ix in wide] for r in (q_ref, k_ref, v_ref, g_ref, b_ref)])
        for ref, vals, where in zip((wd_ref, uv_ref, qk_ref, qd_ref, kd_ref, t_ref), res,
                                    (wide, wide, narrow, wide, wide, narrow)):
            for ix, val in zip(where, vals):
                ref[ix] = val

    bw = pl.BlockSpec(wide, lambda i: (i, 0))
    bn = pl.BlockSpec(narrow, lambda i: (i, 0))
    fw = jax.ShapeDtypeStruct((s, w), F32)
    fn = jax.ShapeDtypeStruct((s, nh * CHUNK), F32)
    return pl.pallas_call(
        body, name="gdn_intra", grid=(n,), in_specs=[bw] * 5, out_specs=[bw, bw, bn, bw, bw, bn],
        out_shape=[fw, fw, fn, fw, fw, fn], compiler_params=_cparams(("parallel",)))(q, k, v, gb, bb)


def _gdn_intra_bwd(q, k, v, gb, bb, t, cts, dgb_inter, per_step=4):
    s, w = q.shape
    nh, nc, cpb, n, wide, narrow = _gdn_blocks(s, w, per_step)

    def body(q_ref, k_ref, v_ref, g_ref, b_ref, t_ref, cwd, cuv, cqk, cqd, ckd, dgi, dq_ref, dk_ref, dv_ref, dg_ref,
             db_ref):
        wide, narrow = _gdn_pairs(cpb, nh)
        t_known = [t_ref[ix] for ix in narrow]
        _, vjp = jax.vjp(lambda *a: _gdn_intra(*a, t_known=t_known)[:5],
                         *[[r[ix] for ix in wide] for r in (q_ref, k_ref, v_ref, g_ref, b_ref)])
        cts = tuple([r[ix] for ix in where] for r, where in zip((cwd, cuv, cqk, cqd, ckd),
                                                               (wide, wide, narrow, wide, wide)))
        dq, dk, dv, dg, db = vjp(cts)
        for j, ix in enumerate(wide):
            dq_ref[ix], dk_ref[ix], dv_ref[ix], db_ref[ix] = dq[j], dk[j], dv[j], db[j]
            dg_ref[ix] = dg[j] + dgi[ix]

    bw = pl.BlockSpec(wide, lambda i: (i, 0))
    bn = pl.BlockSpec(narrow, lambda i: (i, 0))
    return pl.pallas_call(
        body, name="gdn_intra_bwd", grid=(n,), in_specs=[bw] * 5 + [bn, bw, bw, bn, bw, bw, bw], out_specs=[bw] * 5,
        out_shape=[jax.ShapeDtypeStruct((s, w), F32)] * 5,
        compiler_params=_cparams(("parallel",)))(q, k, v, gb, bb, t, *cts, dgb_inter)


def _gdn_inter_fwd(wd, uv, qk, qd, kd, gb, per_step=4):
    s, w = wd.shape
    nh, nc, cpb, n, wide, narrow = _gdn_blocks(s, w, per_step)
    hd = GDN_HEAD_DIM

    def body(wd_ref, uv_ref, qk_ref, qd_ref, kd_ref, g_ref, o_ref, st_ref, state):
        @pl.when(pl.program_id(0) == 0)
        def _():
            state[...] = jnp.zeros_like(state)

        wide, narrow = _gdn_pairs(cpb, nh)
        st = [state[h] for h in range(nh)]
        for cb in range(cpb):
            wi, na = wide[cb * nh:(cb + 1) * nh], narrow[cb * nh:(cb + 1) * nh]
            for h in range(nh):
                st_ref[cb, h] = st[h]
            o, st = _gdn_inter([wd_ref[ix] for ix in wi], [uv_ref[ix] for ix in wi], [qk_ref[ix] for ix in na],
                               [qd_ref[ix] for ix in wi], [kd_ref[ix] for ix in wi], [g_ref[ix] for ix in wi], st)
            for h in range(nh):
                o_ref[wi[h]] = o[h]
        for h in range(nh):
            state[h] = st[h]

    bw = pl.BlockSpec(wide, lambda i: (i, 0))
    bn = pl.BlockSpec(narrow, lambda i: (i, 0))
    return pl.pallas_call(
        body, name="gdn_inter", grid=(n,), in_specs=[bw, bw, bn, bw, bw, bw],
        out_specs=[bw, pl.BlockSpec((cpb, nh, hd, hd), lambda i: (i, 0, 0, 0))],
        out_shape=[jax.ShapeDtypeStruct((s, w), F32), jax.ShapeDtypeStruct((nc, nh, hd, hd), F32)],
        scratch_shapes=[pltpu.VMEM((nh, hd, hd), F32)],
        compiler_params=_cparams(("arbitrary",)))(wd, uv, qk, qd, kd, gb)


def _gdn_inter_bwd(wd, uv, qk, qd, kd, gb, states, do, per_step=4):
    s, w = wd.shape
    nh, nc, cpb, n, wide, narrow = _gdn_blocks(s, w, per_step)
    hd = GDN_HEAD_DIM

    def body(wd_ref, uv_ref, qk_ref, qd_ref, kd_ref, g_ref, st_ref, do_ref, cwd, cuv, cqk, cqd, ckd, dg_ref, dstate):
        @pl.when(pl.program_id(0) == 0)
        def _():
            dstate[...] = jnp.zeros_like(dstate)

        wide, narrow = _gdn_pairs(cpb, nh)
        dst = [dstate[h] for h in range(nh)]
        for cb in reversed(range(cpb)):
            wi, na = wide[cb * nh:(cb + 1) * nh], narrow[cb * nh:(cb + 1) * nh]
            _, vjp = jax.vjp(_gdn_inter, [wd_ref[ix] for ix in wi], [uv_ref[ix] for ix in wi],
                             [qk_ref[ix] for ix in na], [qd_ref[ix] for ix in wi], [kd_ref[ix] for ix in wi],
                             [g_ref[ix] for ix in wi], [st_ref[cb, h] for h in range(nh)])
            dwd, duv, dqk, dqd, dkd, dg, dst = vjp(([do_ref[ix] for ix in wi], dst))
            for h in range(nh):
                cwd[wi[h]], cuv[wi[h]], cqk[na[h]], cqd[wi[h]], ckd[wi[h]], dg_ref[wi[h]] = (
                    dwd[h], duv[h], dqk[h], dqd[h], dkd[h], dg[h])
        for h in range(nh):
            dstate[h] = dst[h]

    bw = pl.BlockSpec(wide, lambda i: (n - 1 - i, 0))
    bn = pl.BlockSpec(narrow, lambda i: (n - 1 - i, 0))
    fw = jax.ShapeDtypeStruct((s, w), F32)
    return pl.pallas_call(
        body, name="gdn_inter_bwd", grid=(n,),
        in_specs=[bw, bw, bn, bw, bw, bw, pl.BlockSpec((cpb, nh, hd, hd), lambda i: (n - 1 - i, 0, 0, 0)), bw],
        out_specs=[bw, bw, bn, bw, bw, bw],
        out_shape=[fw, fw, jax.ShapeDtypeStruct((s, nh * CHUNK), F32), fw, fw, fw],
        scratch_shapes=[pltpu.VMEM((nh, hd, hd), F32)],
        compiler_params=_cparams(("arbitrary",)))(wd, uv, qk, qd, kd, gb, states, do)


def _s5_coef(ar, ai):
    nl = ar.shape[1]

    def body(ar_ref, ai_ref, o_ref):
        row = lax.broadcasted_iota(jnp.int32, (8, nl), 0)
        for base, sign in ((0, 1.0), (8, -1.0)):
            pr = [jnp.broadcast_to(ar_ref[...], (8, nl))]
            pi = [jnp.broadcast_to(ai_ref[...], (8, nl)) * sign]
            for _ in range(7):
                pr.append(pr[-1] * pr[0] - pi[-1] * pi[0])
                pi.append(pr[-2] * pi[0] + pi[-1] * pr[0])
            for j, d in enumerate((1, 2, 4)):
                m = (row >= d) if base == 0 else (row <= 7 - d)
                o_ref[base + 2 * j] = jnp.where(m, pr[d - 1], 0.0)
                o_ref[base + 2 * j + 1] = jnp.where(m, pi[d - 1], 0.0)
            cr, ci = jnp.zeros((8, nl), F32), jnp.zeros((8, nl), F32)
            for t in range(8):
                e = t if base == 0 else 7 - t
                cr = jnp.where(row == t, pr[e], cr)
                ci = jnp.where(row == t, pi[e], ci)
            o_ref[base + 6] = cr
            o_ref[base + 7] = ci

    return pl.pallas_call(body, name="s5_coef", out_shape=jax.ShapeDtypeStruct((16, 8, nl), F32),
                          compiler_params=_cparams())(ar, ai)


def _scan_tile(src_re, src_im, dst_re, dst_im, coef_ref, carry_re, carry_im, ts, reverse, extra=None):
    nl = src_re.shape[1]
    base = 8 if reverse else 0
    ng = ts // 8
    for lc in range(nl // SCAN_LANES):
        ln = slice(lc * SCAN_LANES, (lc + 1) * SCAN_LANES)
        m = [coef_ref[base + j, :, ln] for j in range(8)]
        row = lax.broadcasted_iota(jnp.int32, (8, SCAN_LANES), 0)

        def step(r, carry, ln=ln, m=m, row=row):
            grp = (ng - 1 - r) if reverse else r
            rows = pl.ds(pl.multiple_of(grp * 8, 8), 8)
            xr, xi = src_re[rows, ln], src_im[rows, ln]
            for j, d in enumerate((1, 2, 4)):
                sh = 8 - d if reverse else d
                sr, si = pltpu.roll(xr, sh, 0), pltpu.roll(xi, sh, 0)
                mr, mi = m[2 * j], m[2 * j + 1]
                xr, xi = xr + mr * sr - mi * si, xi + mr * si + mi * sr
            cr, ci = carry[0], carry[1]
            hr = xr + m[6] * cr - m[7] * ci
            hi = xi + m[6] * ci + m[7] * cr
            dst_re[rows, ln] = hr
            dst_im[rows, ln] = hi
            edge = 0 if reverse else 7
            out = (jnp.broadcast_to(hr[edge:edge + 1, :], hr.shape), jnp.broadcast_to(hi[edge:edge + 1, :], hi.shape))
            if extra is not None:
                h_re, h_im, halo_re, halo_im, first, _, _ = extra
                prev = pl.ds(pl.multiple_of(jnp.maximum(grp - 1, 0) * 8, 8), 8)
                use_halo = grp == 0
                pr = jnp.where(use_halo, halo_re[:, ln] * first, h_re[prev, ln])
                pi = jnp.where(use_halo, halo_im[:, ln] * first, h_im[prev, ln])
                qr = jnp.where(row == 0, jnp.broadcast_to(pr[7:8, :], pr.shape), pltpu.roll(h_re[rows, ln], 1, 0))
                qi = jnp.where(row == 0, jnp.broadcast_to(pi[7:8, :], pi.shape), pltpu.roll(h_im[rows, ln], 1, 0))
                out = out + (carry[2] + hr * qr + hi * qi, carry[3] + hi * qr - hr * qi)
            return out

        init = (carry_re[:, ln], carry_im[:, ln])
        if extra is not None:
            init = init + (extra[5][:, ln], extra[6][:, ln])
        fin = lax.fori_loop(0, ng, step, init)
        carry_re[:, ln] = fin[0]
        carry_im[:, ln] = fin[1]
        if extra is not None:
            extra[5][:, ln] = fin[2]
            extra[6][:, ln] = fin[3]


def _s5_fwd(xb, bb_re, bb_im, c_re, c_im, coef, ts):
    s, w = xb.shape
    nb = bb_re.shape[0]
    nl = nb * 512

    def body(x_ref, bre_ref, bim_ref, cre_ref, cim_ref, coef_ref, hre_ref, him_ref, y_ref, ure, uim, car_re, car_im):
        @pl.when(pl.program_id(0) == 0)
        def _():
            car_re[...] = jnp.zeros_like(car_re)
            car_im[...] = jnp.zeros_like(car_im)

        for b in range(nb):
            xs = x_ref[:, b * 128:(b + 1) * 128].astype(BF16)
            ure[:, b * 512:(b + 1) * 512] = jnp.dot(xs, bre_ref[b], preferred_element_type=F32)
            uim[:, b * 512:(b + 1) * 512] = jnp.dot(xs, bim_ref[b], preferred_element_type=F32)
        _scan_tile(ure, uim, hre_ref, him_ref, coef_ref, car_re, car_im, ts, False)
        for b in range(nb):
            hr = hre_ref[:, b * 512:(b + 1) * 512].astype(BF16)
            hi = him_ref[:, b * 512:(b + 1) * 512].astype(BF16)
            y_ref[:, b * 128:(b + 1) * 128] = (jnp.dot(hr, cre_ref[b], preferred_element_type=F32)
                                               - jnp.dot(hi, cim_ref[b], preferred_element_type=F32))

    row = lambda wd: pl.BlockSpec((ts, wd), lambda i: (i, 0))
    return pl.pallas_call(
        body, name="s5_fwd", grid=(s // ts,),
        in_specs=[row(w), _whole(bb_re), _whole(bb_im), _whole(c_re), _whole(c_im), _whole(coef)],
        out_specs=[row(nl), row(nl), row(w)],
        out_shape=[jax.ShapeDtypeStruct((s, nl), F32), jax.ShapeDtypeStruct((s, nl), F32),
                   jax.ShapeDtypeStruct((s, w), F32)],
        scratch_shapes=[pltpu.VMEM((ts, nl), F32), pltpu.VMEM((ts, nl), F32), pltpu.VMEM((8, nl), F32),
                        pltpu.VMEM((8, nl), F32)],
        compiler_params=_cparams(("arbitrary",)))(xb, bb_re, bb_im, c_re, c_im, coef)


def _s5_bwd(dy, xb, h_re, h_im, bb_re, bb_im, c_re, c_im, coef, ts):
    s, w = xb.shape
    nb = bb_re.shape[0]
    nl = nb * 512
    nt = s // ts

    def body(dy_ref, x_ref, hre_ref, him_ref, halo_re, halo_im, bre_ref, bim_ref, cre_ref, cim_ref, coef_ref,
             dx_ref, dbre_ref, dbim_ref, dcre_ref, dcim_ref, dare_ref, daim_ref, gre, gim, car_re, car_im):
        i = pl.program_id(0)

        @pl.when(i == 0)
        def _():
            for r in (car_re, car_im, dbre_ref, dbim_ref, dcre_ref, dcim_ref, dare_ref, daim_ref):
                r[...] = jnp.zeros_like(r)

        for b in range(nb):
            dyb = dy_ref[:, b * 128:(b + 1) * 128].astype(BF16)
            gre[:, b * 512:(b + 1) * 512] = lax.dot_general(dyb, cre_ref[b], (((1,), (1,)), ((), ())),
                                                            preferred_element_type=F32)
            gim[:, b * 512:(b + 1) * 512] = -lax.dot_general(dyb, cim_ref[b], (((1,), (1,)), ((), ())),
                                                             preferred_element_type=F32)
            hr = hre_ref[:, b * 512:(b + 1) * 512].astype(BF16)
            hi = him_ref[:, b * 512:(b + 1) * 512].astype(BF16)
            dcre_ref[b] += lax.dot_general(hr, dyb, (((0,), (0,)), ((), ())), preferred_element_type=F32)
            dcim_ref[b] -= lax.dot_general(hi, dyb, (((0,), (0,)), ((), ())), preferred_element_type=F32)
        first = (i != nt - 1).astype(F32)
        _scan_tile(gre, gim, gre, gim, coef_ref, car_re, car_im, ts, True,
                   extra=(hre_ref, him_ref, halo_re, halo_im, first, dare_ref, daim_ref))
        for b in range(nb):
            gr = gre[:, b * 512:(b + 1) * 512].astype(BF16)
            gi = gim[:, b * 512:(b + 1) * 512].astype(BF16)
            xs = x_ref[:, b * 128:(b + 1) * 128].astype(BF16)
            dx_ref[:, b * 128:(b + 1) * 128] = (
                lax.dot_general(gr, bre_ref[b], (((1,), (1,)), ((), ())), preferred_element_type=F32)
                + lax.dot_general(gi, bim_ref[b], (((1,), (1,)), ((), ())), preferred_element_type=F32))
            dbre_ref[b] += lax.dot_general(xs, gr, (((0,), (0,)), ((), ())), preferred_element_type=F32)
            dbim_ref[b] += lax.dot_general(xs, gi, (((0,), (0,)), ((), ())), preferred_element_type=F32)

    row = lambda wd: pl.BlockSpec((ts, wd), lambda i: (nt - 1 - i, 0))
    halo = pl.BlockSpec((8, nl), lambda i: (jnp.maximum((nt - 1 - i) * (ts // 8) - 1, 0), 0))
    return pl.pallas_call(
        body, name="s5_bwd", grid=(nt,),
        in_specs=[row(w), row(w), row(nl), row(nl), halo, halo, _whole(bb_re), _whole(bb_im), _whole(c_re),
                  _whole(c_im), _whole(coef)],
        out_specs=[row(w), _whole(bb_re), _whole(bb_im), _whole(c_re), _whole(c_im),
                   pl.BlockSpec((8, nl), lambda i: (0, 0)), pl.BlockSpec((8, nl), lambda i: (0, 0))],
        out_shape=[jax.ShapeDtypeStruct((s, w), F32), jax.ShapeDtypeStruct(bb_re.shape, F32),
                   jax.ShapeDtypeStruct(bb_im.shape, F32), jax.ShapeDtypeStruct(c_re.shape, F32),
                   jax.ShapeDtypeStruct(c_im.shape, F32), jax.ShapeDtypeStruct((8, nl), F32),
                   jax.ShapeDtypeStruct((8, nl), F32)],
        scratch_shapes=[pltpu.VMEM((ts, nl), F32), pltpu.VMEM((ts, nl), F32), pltpu.VMEM((8, nl), F32),
                        pltpu.VMEM((8, nl), F32)],
        compiler_params=_cparams(("arbitrary",)))(dy, xb, h_re, h_im, h_re, h_im, bb_re, bb_im, c_re, c_im, coef)


def _final(x, mo, target, fg, ts):
    s, d = x.shape

    def f(x, mo, fg, tgt):
        y = _rms(x + mo, fg)[0]
        err = y - tgt
        return 0.5 * jnp.sum(jnp.mean(err * err, axis=-1, keepdims=True), axis=0, keepdims=True)

    def body(x_ref, mo_ref, t_ref, fg_ref, dh_ref, dfg_ref, loss_ref):
        @pl.when(pl.program_id(0) == 0)
        def _():
            dfg_ref[...] = jnp.zeros_like(dfg_ref)
            loss_ref[...] = jnp.zeros_like(loss_ref)

        loss, vjp = jax.vjp(f, x_ref[...], mo_ref[...], fg_ref[...], t_ref[...])
        _, dmo, dfg, _ = vjp(jnp.ones((1, 1), F32))
        dh_ref[...] = dmo
        dfg_ref[...] += dfg
        loss_ref[...] += jnp.broadcast_to(loss, loss_ref.shape)

    row = pl.BlockSpec((ts, d), lambda i: (i, 0))
    return pl.pallas_call(
        body, name="final", grid=(s // ts,), in_specs=[row, row, row, _whole(fg)],
        out_specs=[row, _whole(fg), pl.BlockSpec((8, 128), lambda i: (0, 0))],
        out_shape=[jax.ShapeDtypeStruct((s, d), F32), jax.ShapeDtypeStruct(fg.shape, F32),
                   jax.ShapeDtypeStruct((8, 128), F32)],
        compiler_params=_cparams(("arbitrary",)))(x, mo, target, fg)


N_CHIPS = 4


def _other_chips(x, y):
    return [((1 - x, y), 2 * (1 - x) + y), ((x, 1 - y), 2 * x + 1 - y), ((1 - x, 1 - y), 2 * (1 - x) + 1 - y)]


def _comm_call(body, name, srcs, out_shapes, n_sems):
    n = len(srcs)
    return pl.pallas_call(
        body, name=name, in_specs=[pl.BlockSpec(memory_space=pl.ANY)] * n,
        out_specs=[pl.BlockSpec(memory_space=pl.ANY)] * n, out_shape=out_shapes,
        scratch_shapes=[pltpu.SemaphoreType.DMA((n, n_sems)), pltpu.SemaphoreType.DMA((n, n_sems)),
                        pltpu.SemaphoreType.DMA((n,))],
        compiler_params=pltpu.CompilerParams(has_side_effects=True))(*srcs)


def _gather(srcs, name):
    n = len(srcs)

    def body(*refs):
        src, out = refs[:n], refs[n:2 * n]
        send_sems, recv_sems, local_sems = refs[2 * n:]
        x, y, c = lax.axis_index("x"), lax.axis_index("y"), lax.axis_index("c")
        me, sib_slot, sib = 4 * x + 2 * y + c, 4 * x + 2 * y + 1 - c, (x, y, 1 - c)
        chips = _other_chips(x, y)

        def cp(a, k, src_ref, slot, to):
            return pltpu.make_async_remote_copy(
                src_ref=src_ref, dst_ref=out[a].at[slot], send_sem=send_sems.at[a, k], recv_sem=recv_sems.at[a, k],
                device_id=to, device_id_type=pl.DeviceIdType.MESH)

        local = [pltpu.make_async_copy(src[a], out[a].at[me], local_sems.at[a]) for a in range(n)]
        first = [cp(a, 0, src[a], me, sib) for a in range(n)]
        first += [cp(a, 1 + j, src[a], me, (*chip, c)) for j, (chip, _) in enumerate(chips) for a in range(n)]
        for d in local + first:
            d.start()
        passed = []
        for j, (chip, q) in enumerate(chips):
            for a in range(n):
                cp(a, 1 + j, src[a], 2 * q + c, sib).wait_recv()
                fwd = cp(a, 4 + j, out[a].at[2 * q + c], 2 * q + c, sib)
                fwd.start()
                passed.append(fwd)
        for a in range(n):
            cp(a, 0, src[a], sib_slot, sib).wait_recv()
        for j, (chip, q) in enumerate(chips):
            for a in range(n):
                cp(a, 4 + j, src[a], 2 * q + 1 - c, sib).wait_recv()
        for d in first + passed:
            d.wait_send()
        for d in local:
            d.wait()

    return _comm_call(body, name, srcs, [jax.ShapeDtypeStruct((N_DEV,) + s.shape, s.dtype) for s in srcs], 7)


def _all_peers(x, y, c):
    out = []
    for k in range(1, N_DEV):
        px = 1 - x if k & 4 else x
        py = 1 - y if k & 2 else y
        pc = 1 - c if k & 1 else c
        out.append(((px, py, pc), 4 * px + 2 * py + pc))
    return out


_HBM = pl.BlockSpec(memory_space=pltpu.HBM)
_SEM = pl.BlockSpec(memory_space=pltpu.SEMAPHORE)
_DATAFLOW = pltpu.SideEffectType.DATAFLOW_SIDE_EFFECTING


def _send_whole(ref, slot):
    return ref


def _send_slot(ref, slot):
    return ref.at[slot]


def _direct_copies(src, land, send_sems, recv_sems, picks, arriving):
    x, y, c = lax.axis_index("x"), lax.axis_index("y"), lax.axis_index("c")
    me = 4 * x + 2 * y + c
    out = []
    for k, (pos, slot) in enumerate(_all_peers(x, y, c)):
        for a in range(len(src)):
            sem = a * (N_DEV - 1) + k
            out.append(pltpu.make_async_remote_copy(
                src_ref=picks[a](src[a], slot), dst_ref=land[a].at[slot if arriving else me],
                send_sem=send_sems.at[sem], recv_sem=recv_sems.at[sem], device_id=pos,
                device_id_type=pl.DeviceIdType.MESH))
    return out


def _direct_start(srcs, lands, picks, name):
    n = len(srcs)

    def body(*refs):
        src, land = refs[:n], refs[n:2 * n]
        send_sems, recv_sems = refs[2 * n], refs[2 * n + 1]
        for push in _direct_copies(src, land, send_sems, recv_sems, picks, False):
            push.start()
        refs[-1][...] = jnp.zeros_like(refs[-1])

    arrays = [pltpu.with_memory_space_constraint(t, pltpu.HBM) for t in list(srcs) + list(lands)]
    outs = pl.pallas_call(
        body, name=name, in_specs=[_HBM] * (2 * n),
        out_specs=(_SEM, _SEM, *[_HBM] * (2 * n), pl.BlockSpec(memory_space=pltpu.VMEM)),
        out_shape=(pltpu.SemaphoreType.DMA((n * (N_DEV - 1),)), pltpu.SemaphoreType.DMA((n * (N_DEV - 1),)),
                   *[pltpu.HBM(t.shape, t.dtype) for t in arrays], jax.ShapeDtypeStruct((8, 128), F32)),
        input_output_aliases={i: 2 + i for i in range(2 * n)},
        compiler_params=pltpu.CompilerParams(has_side_effects=_DATAFLOW))(*arrays)
    return outs[:-1], outs[-1]


def _direct_wait(started, picks, after, name):
    send_sems, recv_sems, *thru = started
    n = len(thru) // 2

    def body(*refs):
        src, land = refs[:n], refs[n:2 * n]
        for arrive in _direct_copies(src, land, refs[2 * n], refs[2 * n + 1], picks, True):
            arrive.wait_send()
            arrive.wait_recv()

    outs = pl.pallas_call(
        body, name=name, in_specs=[_HBM] * (2 * n) + [_SEM, _SEM, pl.BlockSpec(memory_space=pl.ANY)],
        out_specs=[_HBM] * (2 * n), out_shape=[pltpu.HBM(t.shape, t.dtype) for t in thru],
        input_output_aliases={i: i for i in range(2 * n)},
        compiler_params=pltpu.CompilerParams(has_side_effects=_DATAFLOW))(*thru, send_sems, recv_sems, after)
    return outs[n:]


def _pair_scatter(gs, name):
    n = len(gs)

    def body(*refs):
        src, out = refs[:n], refs[n:2 * n]
        send_sems, recv_sems, _ = refs[2 * n:]
        x, y, c = lax.axis_index("x"), lax.axis_index("y"), lax.axis_index("c")
        sends = []
        for q in range(N_CHIPS):
            for a in range(n):
                d = pltpu.make_async_remote_copy(
                    src_ref=src[a].at[2 * q + 1 - c], dst_ref=out[a].at[q], send_sem=send_sems.at[a, q],
                    recv_sem=recv_sems.at[a, q], device_id=(x, y, 1 - c), device_id_type=pl.DeviceIdType.MESH)
                d.start()
                sends.append(d)
        for d in sends:
            d.wait_recv()
        for d in sends:
            d.wait_send()

    return _comm_call(body, name, gs, [jax.ShapeDtypeStruct((N_CHIPS,) + g.shape[1:], g.dtype) for g in gs], N_CHIPS)


def _cross_scatter(ps, name):
    n = len(ps)

    def body(*refs):
        src, out = refs[:n], refs[n:2 * n]
        send_sems, recv_sems, local_sems = refs[2 * n:]
        x, y, c = lax.axis_index("x"), lax.axis_index("y"), lax.axis_index("c")
        mine = 2 * x + y
        chips = _other_chips(x, y)
        local = [pltpu.make_async_copy(src[a].at[mine], out[a].at[mine], local_sems.at[a]) for a in range(n)]
        for d in local:
            d.start()
        sends = []
        for j, (chip, q) in enumerate(chips):
            for a in range(n):
                d = pltpu.make_async_remote_copy(
                    src_ref=src[a].at[q], dst_ref=out[a].at[mine], send_sem=send_sems.at[a, j],
                    recv_sem=recv_sems.at[a, j], device_id=(*chip, c), device_id_type=pl.DeviceIdType.MESH)
                d.start()
                sends.append(d)
        for j, (chip, q) in enumerate(chips):
            for a in range(n):
                pltpu.make_async_remote_copy(
                    src_ref=src[a].at[q], dst_ref=out[a].at[q], send_sem=send_sems.at[a, j],
                    recv_sem=recv_sems.at[a, j], device_id=(*chip, c), device_id_type=pl.DeviceIdType.MESH).wait_recv()
        for d in sends:
            d.wait_send()
        for d in local:
            d.wait()

    return _comm_call(body, name, ps, [jax.ShapeDtypeStruct(p.shape, p.dtype) for p in ps], 3)


def _pair_sum(g, got, c_idx, out_dtype, name):
    _, r, c = g.shape
    lanes = -(-c // 128) * 128
    tr = _pick_rows(r, max(8, (2 * 1024 * 1024) // (lanes * 4)))
    g4 = g.reshape((N_CHIPS, 2) + g.shape[1:])

    def body(c_ref, g_ref, got_ref, o_ref):
        o_ref[...] = (g_ref[...] + got_ref[...]).astype(o_ref.dtype)

    return pl.pallas_call(
        body, name=name,
        grid_spec=pltpu.PrefetchScalarGridSpec(
            num_scalar_prefetch=1, grid=(N_CHIPS, r // tr),
            in_specs=[pl.BlockSpec((None, None, tr, c), lambda q, i, cr: (q, cr[0], i, 0)),
                      pl.BlockSpec((None, tr, c), lambda q, i, cr: (q, i, 0))],
            out_specs=pl.BlockSpec((None, tr, c), lambda q, i, cr: (q, i, 0))),
        out_shape=jax.ShapeDtypeStruct(got.shape, out_dtype),
        compiler_params=_cparams(("parallel", "parallel")))(c_idx, g4, got)


def _pick_rows(r, pref):
    t = (min(pref, r) // 8) * 8
    while t >= 8:
        if r % t == 0:
            return t
        t -= 8
    return r


def _w_in_from_shards(t, lo, hi):
    n, r, cs = t.shape
    tr = _pick_rows(r, 256)
    wm = n * cs - (hi - lo)

    def body(t_ref, m_ref, b_ref):
        full = jnp.concatenate([t_ref[j] for j in range(n)], axis=1)
        m_ref[...] = jnp.concatenate([full[:, :lo], full[:, hi:]], axis=1)
        b_ref[...] = jnp.concatenate([full[:, lo:hi], jnp.zeros((tr, 128 - (hi - lo)), full.dtype)], axis=1)

    return pl.pallas_call(
        body, name="w_in_layout", grid=(r // tr,), in_specs=[pl.BlockSpec((n, tr, cs), lambda i: (0, i, 0))],
        out_specs=[pl.BlockSpec((tr, wm), lambda i: (i, 0)), pl.BlockSpec((tr, 128), lambda i: (i, 0))],
        out_shape=[jax.ShapeDtypeStruct((r, wm), t.dtype), jax.ShapeDtypeStruct((r, 128), t.dtype)],
        compiler_params=_cparams(("parallel",)))(t)


def _w_in_to_shards(gm, gb, lo, hi):
    r, wm = gm.shape
    cs = (wm + hi - lo) // N_DEV
    tr = _pick_rows(r, 64)

    def body(m_ref, b_ref, o_ref):
        m = m_ref[...]
        full = jnp.concatenate([m[:, :lo], b_ref[:, :hi - lo], m[:, lo:]], axis=1)
        for j in range(N_DEV):
            o_ref[j] = full[:, j * cs:(j + 1) * cs]

    return pl.pallas_call(
        body, name="dw_in_layout", grid=(r // tr,),
        in_specs=[pl.BlockSpec((tr, wm), lambda i: (i, 0)), pl.BlockSpec((tr, 128), lambda i: (i, 0))],
        out_specs=pl.BlockSpec((N_DEV, tr, cs), lambda i: (0, i, 0)),
        out_shape=jax.ShapeDtypeStruct((N_DEV, r, cs), gm.dtype), compiler_params=_cparams(("parallel",)))(gm, gb)


def _pack(arrs, dtype, lead=()):
    nlead = len(lead)
    flat = jnp.concatenate([a.astype(dtype).reshape(lead + (-1,)) for a in arrs], axis=nlead)
    n = flat.shape[-1]
    unit = PACK_WIDTH * PACK_ROWS
    pad = (-n) % unit
    flat = jnp.pad(flat, [(0, 0)] * nlead + [(0, pad)])
    return flat.reshape(lead + ((n + pad) // PACK_WIDTH, PACK_WIDTH))


def _unpack(buf, shapes, lead=()):
    flat = buf.reshape(lead + (-1,))
    out, off = [], 0
    for shp in shapes:
        n = math.prod(shp)
        out.append(flat[..., off:off + n].reshape(lead + tuple(shp)))
        off += n
    return out


def _adam_math(w, g, m, v):
    m = ADAM_B1 * m + (1.0 - ADAM_B1) * g
    v = ADAM_B2 * v + (1.0 - ADAM_B2) * (g * g)
    m_hat = m / (1.0 - ADAM_B1 ** ADAM_STEP)
    v_hat = v / (1.0 - ADAM_B2 ** ADAM_STEP)
    delta = -ADAM_LR * (m_hat / (jnp.sqrt(v_hat) + ADAM_EPS) + ADAM_WD * w)
    return delta, m, v


def _sum_adam(parts, w, m, v, name):
    r, c = w.shape
    nparts = parts.shape[0]
    lanes = -(-c // 128) * 128
    tr = _pick_rows(r, max(8, (6 * 1024 * 1024) // (nparts * lanes * 4)))

    def body(p_ref, w_ref, m_ref, v_ref, g_ref, d_ref, nm_ref, nv_ref):
        g = p_ref[0].astype(F32)
        for j in range(1, nparts):
            g = g + p_ref[j].astype(F32)
        d, nm, nv = _adam_math(w_ref[...], g, m_ref[...], v_ref[...])
        g_ref[...] = g
        d_ref[...] = d
        nm_ref[...] = nm
        nv_ref[...] = nv

    row = pl.BlockSpec((tr, c), lambda i: (i, 0))
    return pl.pallas_call(
        body, name=name, grid=(r // tr,), in_specs=[pl.BlockSpec((nparts, tr, c), lambda i: (0, i, 0)), row, row, row],
        out_specs=[row] * 4, out_shape=[jax.ShapeDtypeStruct((r, c), F32)] * 4,
        compiler_params=_cparams(("parallel",)))(parts, w, m, v)


def _block_diag(t):
    nb, g, a, b = t.shape
    eye = jnp.eye(g, dtype=t.dtype)
    return jnp.einsum('ngab,gh->ngahb', t, eye).reshape(nb, g * a, g * b)


def _diag_blocks(t, a, b):
    nb = t.shape[0]
    g = S5_GROUPS_PER_BLOCK
    t = t.reshape(nb, g, a, g, b)
    return jnp.stack([t[:, j, :, j, :] for j in range(g)], axis=1)


def _local_step(x, mem, target, p, late_weights=None, early_grads=None):
    s, d = x.shape
    gw = d // 2
    nh = gw // GDN_HEAD_DIM
    ng = gw // S5_GROUP
    nb = ng // S5_GROUPS_PER_BLOCK
    nl = ng * S5_STATE
    ts = min(256, s)
    nt = s // ts
    grads = {}

    w_main, w_ba = p['w_main'], p['w_ba']
    CB_ZA, CB_XB, CB_ZB, CB_QC, CB_ZC, CB_G = 3, 4, 5, 6, 7, 8

    u = _tile_fwd(_rms, "rms_fwd", nt, [_rt(x, ts)], [p['norm_g']],
                  [((s, d), BF16, (ts, d), lambda i: (i, 0))])[0]
    proj = _mm(u, w_main, tm=1024, tn=2048, tk=512, name="proj_main")
    pba = _mm(u, w_ba, name="proj_ba")
    if late_weights is not None:
        p = {**p, **late_weights(proj)}

    conv_w = p['conv_w']
    col = lambda arr, cb: (arr, (s, GDN_HEAD_DIM), lambda i, cb=cb: (0, cb + i))
    qkv = []
    for j, mode in enumerate(('q', 'k', 'v')):
        off = j * nh
        qkv.append(_tile_fwd(
            _gdn_pre(mode), "gdn_pre_" + mode, nh, [col(proj, off), (conv_w, (CONV_WIDTH, GDN_HEAD_DIM), lambda i, off=off: (0, off + i))],
            [], [((s, gw), F32, (s, GDN_HEAD_DIM), lambda i: (0, i))])[0])
    q, k, v = qkv
    lane = jnp.arange(128)[:, None]
    colh = jnp.arange(gw)[None, :] // GDN_HEAD_DIM
    e_beta = (lane == colh).astype(F32)
    e_g = (lane == colh + nh).astype(F32)
    alog_row = jnp.pad(p['gdn_a_log'], ((0, 0), (nh, 128 - 2 * nh)))
    dtb_row = jnp.pad(p['gdn_dt_bias'], ((0, 0), (nh, 128 - 2 * nh)))
    row_gw = lambda: ((s, gw), F32, (ts, gw), lambda i: (i, 0))
    betab, gb = _tile_fwd(_gdn_gates, "gdn_gates", nt, [_rt(pba, ts)], [alog_row, dtb_row, e_beta, e_g],
                          [row_gw(), row_gw()])
    *intra, t_inv = _gdn_intra_fwd(q, k, v, gb, betab)
    o_raw, states = _gdn_inter_fwd(*intra, gb)
    ga = _tile_fwd(_gdn_post, "gdn_post", nt, [_rt(o_raw, ts), _rt(proj, ts, CB_ZA, gw)], [p['gdn_norm_g']],
                   [((s, gw), BF16, (ts, gw), lambda i: (i, 0))])[0]

    e_rep = (jnp.arange(S5_STATE)[:, None] == jnp.arange(S5_STATE * S5_GROUP)[None, :] // S5_GROUP).astype(F32)
    s5_in = [p['s5_lambda_re'], p['s5_lambda_im'], p['s5_log_dt'].reshape(ng, 1),
             p['s5_b_re'].reshape(ng, S5_STATE * S5_GROUP), p['s5_b_im'].reshape(ng, S5_STATE * S5_GROUP), e_rep]
    one = lambda shp: (shp, F32, shp, lambda i, n=len(shp): (0,) * n)
    ab_re, ab_im, bbr, bbi = _tile_fwd(_s5_params, "s5_params", 1, [], s5_in,
                                       [one((ng, S5_STATE)), one((ng, S5_STATE)), one((ng, S5_STATE * S5_GROUP)),
                                        one((ng, S5_STATE * S5_GROUP))])
    coef = _s5_coef(ab_re.reshape(1, nl), ab_im.reshape(1, nl))
    to_bd_b = lambda t: _block_diag(t.reshape(nb, S5_GROUPS_PER_BLOCK, S5_STATE, S5_GROUP).transpose(0, 1, 3, 2))
    to_bd_c = lambda t: _block_diag(t.reshape(nb, S5_GROUPS_PER_BLOCK, S5_GROUP, S5_STATE).transpose(0, 1, 3, 2))
    bbd_re, bbd_im = to_bd_b(bbr).astype(BF16), to_bd_b(bbi).astype(BF16)
    cbd_re, cbd_im = to_bd_c(p['s5_c_re']).astype(BF16), to_bd_c(p['s5_c_im']).astype(BF16)
    xb_arr = lax.slice_in_dim(proj, CB_XB * gw, (CB_XB + 1) * gw, axis=1)
    h_re, h_im, ylin = _s5_fwd(xb_arr, bbd_re, bbd_im, cbd_re, cbd_im, coef, ts)
    gl = _tile_fwd(_s5_post1, "s5_post1", nt, [_rt(ylin, ts), _rt(proj, ts, CB_XB, gw)], [p['s5_d']],
                   [((s, gw), BF16, (ts, gw), lambda i: (i, 0))])[0]
    tglu = _mm(gl, p['s5_w_glu'], b_shards=True, name="s5_glu")
    gbb = _tile_fwd(_s5_post2, "s5_post2", nt, [_rt(tglu, ts), _rt(proj, ts, CB_ZB, gw)], [],
                    [((s, gw), BF16, (ts, gw), lambda i: (i, 0))])[0]

    m_len = mem.shape[0]
    mem_n = _tile_fwd(_rms, "mem_rms", 1, [_rt(mem, m_len)], [p['mem_norm_g']],
                      [((m_len, d), BF16, (m_len, d), lambda i: (i, 0))])[0]
    kv = _mm(mem_n, p['w_kv_mem'], name="mem_kv")
    gcc = _tile_fwd(_attn, "attn", nt, [_rt(proj, ts, CB_QC, gw), _rt(proj, ts, CB_ZC, gw)], [kv],
                    [((s, gw), BF16, (ts, gw), lambda i: (i, 0))])[0]

    p_a = _mm(ga, p['w_br_a'], b_shards=True, name="br_a")
    p_b = _mm(gbb, p['w_br_b'], b_shards=True, name="br_b")
    p_c = _mm(gcc, p['w_br_c'], b_shards=True, name="br_c")
    gate_acts = [_rt(proj, ts, CB_G // 2 + j, d) for j in range(3)]
    merged = _tile_fwd(_merge, "merge", nt, gate_acts + [_rt(p_a, ts), _rt(p_b, ts), _rt(p_c, ts)], [],
                       [((s, d), BF16, (ts, d), lambda i: (i, 0))])[0]
    mo = _mm(merged, p['w_out'], name="out_proj")
    dh, dfg, loss = _final(x, mo, target, p['final_g'].reshape(1, d), ts)
    grads['final_g'] = dfg.reshape(d)

    dmerged = _mm(dh, p['w_out'], tb=True, name="d_merged")
    grads['w_out'] = _mm(merged, dh, ta=True, name="dw_out")
    row_d = lambda dt: ((s, d), dt, (ts, d), lambda i: (i, 0))
    dg0, dg1, dg2, dpa, dpb, dpc = _tile_bwd(
        _merge, "merge_bwd", nt, gate_acts + [_rt(p_a, ts), _rt(p_b, ts), _rt(p_c, ts)], [], [_rt(dmerged, ts)],
        [row_d(BF16)] * 6, [])
    dga = _mm(dpa, p['w_br_a'], tb=True, b_shards=True, name="d_ga")
    dgbb = _mm(dpb, p['w_br_b'], tb=True, b_shards=True, name="d_gb")
    dgcc = _mm(dpc, p['w_br_c'], tb=True, b_shards=True, name="d_gc")
    grads['w_br_a'] = _mm(ga, dpa, ta=True, out_shards=N_DEV, name="dw_br_a")
    grads['w_br_b'] = _mm(gbb, dpb, ta=True, out_shards=N_DEV, name="dw_br_b")
    grads['w_br_c'] = _mm(gcc, dpc, ta=True, out_shards=N_DEV, name="dw_br_c")
    row_h = lambda dt: ((s, gw), dt, (ts, gw), lambda i: (i, 0))

    dqc, dzc, dkv = _tile_bwd(_attn, "attn_bwd", nt, [_rt(proj, ts, CB_QC, gw), _rt(proj, ts, CB_ZC, gw)], [kv],
                              [_rt(dgcc, ts)], [row_h(BF16), row_h(BF16)], [True])
    grads['w_kv_mem'] = _mm(mem_n, dkv, ta=True, name="dw_kv")
    dmem_n = _mm(dkv, p['w_kv_mem'], tb=True, name="d_mem_n")
    grads['mem_norm_g'] = _tile_bwd(_rms, "mem_rms_bwd", 1, [_rt(mem, m_len)], [p['mem_norm_g']],
                                    [_rt(dmem_n, m_len)], [None], [True])[0]

    dtglu, dzb = _tile_bwd(_s5_post2, "s5_post2_bwd", nt, [_rt(tglu, ts), _rt(proj, ts, CB_ZB, gw)], [],
                           [_rt(dgbb, ts)], [((s, 2 * gw), BF16, (ts, 2 * gw), lambda i: (i, 0)), row_h(BF16)], [])
    grads['s5_w_glu'] = _mm(gl, dtglu, ta=True, out_shards=N_DEV, name="dw_glu")
    s5_d = p['s5_d']
    if early_grads is not None:
        s5_d = s5_d + early_grads(grads)[:1, :1]
    dgl = _mm(dtglu, p['s5_w_glu'], tb=True, b_shards=True, name="d_gl")
    dylin, dxb1, dd = _tile_bwd(_s5_post1, "s5_post1_bwd", nt, [_rt(ylin, ts), _rt(proj, ts, CB_XB, gw)],
                                [s5_d], [_rt(dgl, ts)], [row_h(F32), row_h(F32)], [True])
    grads['s5_d'] = dd
    dxb2, dbbd_re, dbbd_im, dcbd_re, dcbd_im, da_re, da_im = _s5_bwd(dylin, xb_arr, h_re, h_im, bbd_re, bbd_im,
                                                                    cbd_re, cbd_im, coef, ts)
    from_bd_b = lambda t: _diag_blocks(t, S5_GROUP, S5_STATE).transpose(0, 1, 3, 2).reshape(ng, S5_STATE * S5_GROUP)
    from_bd_c = lambda t: _diag_blocks(t, S5_STATE, S5_GROUP).transpose(0, 1, 3, 2).reshape(1, ng, S5_GROUP, S5_STATE)
    grads['s5_c_re'], grads['s5_c_im'] = from_bd_c(dcbd_re), from_bd_c(dcbd_im)
    s5_cts = [jnp.sum(da_re, axis=0).reshape(ng, S5_STATE), jnp.sum(da_im, axis=0).reshape(ng, S5_STATE),
              from_bd_b(dbbd_re), from_bd_b(dbbd_im)]
    dlr, dli, dlogdt, dbr, dbi = _tile_bwd(_s5_params, "s5_params_bwd", 1, [], s5_in,
                                           [(c, c.shape, lambda i: (0, 0)) for c in s5_cts], [],
                                           [True, True, True, True, True, False])
    grads['s5_lambda_re'], grads['s5_lambda_im'] = dlr[None], dli[None]
    grads['s5_log_dt'] = dlogdt.reshape(1, ng)
    grads['s5_b_re'] = dbr.reshape(1, ng, S5_STATE, S5_GROUP)
    grads['s5_b_im'] = dbi.reshape(1, ng, S5_STATE, S5_GROUP)
    dxb = (dxb1 + dxb2).astype(BF16)

    do_raw, dza, dgng = _tile_bwd(_gdn_post, "gdn_post_bwd", nt, [_rt(o_raw, ts), _rt(proj, ts, CB_ZA, gw)],
                                  [p['gdn_norm_g']], [_rt(dga, ts)], [row_h(F32), row_h(BF16)], [True])
    grads['gdn_norm_g'] = dgng
    *intra_cts, dgb_inter = _gdn_inter_bwd(*intra, gb, states, do_raw)
    dq, dk, dv, dgb, dbetab = _gdn_intra_bwd(q, k, v, gb, betab, t_inv, intra_cts, dgb_inter)
    dpba, dalog, ddtb = _tile_bwd(_gdn_gates, "gdn_gates_bwd", nt, [_rt(pba, ts)], [alog_row, dtb_row, e_beta, e_g],
                                  [_rt(dbetab, ts), _rt(dgb, ts)], [((s, 128), BF16, (ts, 128), lambda i: (i, 0))],
                                  [True, True, False, False])
    grads['gdn_a_log'] = dalog[:, nh:2 * nh]
    grads['gdn_dt_bias'] = ddtb[:, nh:2 * nh]
    dqkv, dconv = [], []
    for j, (mode, ct) in enumerate((('q', dq), ('k', dk), ('v', dv))):
        off = j * nh
        wspec = (conv_w, (CONV_WIDTH, GDN_HEAD_DIM), lambda i, off=off: (0, off + i))
        dxc, dwc = _tile_bwd(
            _gdn_pre(mode), "gdn_pre_bwd_" + mode, nh, [col(proj, off), wspec], [], [col(ct, 0)],
            [((s, gw), BF16, (s, GDN_HEAD_DIM), lambda i: (0, i)),
             ((CONV_WIDTH, gw), F32, (CONV_WIDTH, GDN_HEAD_DIM), lambda i: (0, i))], [])
        dqkv.append(dxc)
        dconv.append(dwc)
    grads['conv_w'] = jnp.concatenate(dconv, axis=1)

    dproj = jnp.concatenate(dqkv + [dza, dxb, dzb, dqc, dzc, dg0, dg1, dg2], axis=1)
    du = _mm(dpba, w_ba, tb=True, name="du_ba")
    du = _mm(dproj, w_main, tb=True, addend=du, tm=512, tn=2048, tk=1024, name="du_main")
    grads['w_main'] = _mm(u, dproj, ta=True, tm=1024, tn=2048, tk=512, name="dw_main")
    grads['w_ba'] = _mm(u, dpba, ta=True, name="dw_ba")
    dx, dng = _tile_bwd(_rms, "rms_bwd", nt, [_rt(x, ts)], [p['norm_g']], [_rt(du, ts)], [row_d(F32)], [True])
    grads['norm_g'] = dng
    return loss, dx + dh, grads


def _to_shards(name, g):
    if SHARDED[name] == 'row':
        return g.reshape((N_DEV, g.shape[0] // N_DEV) + g.shape[1:])
    r, c = g.shape
    return g.reshape(r, N_DEV, c // N_DEV).transpose(1, 0, 2)


def _from_shards(name, t):
    if SHARDED[name] == 'row':
        return t.reshape((t.shape[0] * t.shape[1],) + t.shape[2:])
    n, r, c = t.shape
    return t.transpose(1, 0, 2).reshape(r, n * c)


def _step(x, mem, target, w, m, v):
    sharded = list(SHARDED)
    shard_shapes = {n: tuple(w[n].shape[1:]) for n in sharded}
    d = x.shape[-1]
    ba_lo = 2 * d
    ba_hi = ba_lo + 2 * (d // 2 // GDN_HEAD_DIM)

    w_in_all = _gather([w['w_in'][0].astype(BF16)], "gather_w_in")[0]
    late = [w[n][0].astype(BF16) for n in OVERLAPPED] + [w['conv_w'][0]]
    late, w_in_all = lax.optimization_barrier((late, w_in_all))
    every = [_send_whole] * len(late)
    lands = [jnp.broadcast_to(t[None], (N_DEV,) + t.shape) for t in late]
    gather_started, token = _direct_start(late, lands, every, "gather_rest_start")
    full = {}
    full['w_main'], full['w_ba'] = _w_in_from_shards(w_in_all, ba_lo, ba_hi)
    for n in REPLICATED:
        full[n] = w[n]
    for n in ('s5_lambda_re', 's5_lambda_im', 's5_c_re', 's5_c_im'):
        full[n] = w[n][0]
    full['norm_g'] = w['norm_g'] + token[:1, :1]

    def late_weights(proj):
        got = dict(zip(OVERLAPPED + ['conv_w'], _direct_wait(gather_started, every, proj, "gather_rest_wait")))
        for n in ('w_kv_mem', 'w_out', 'conv_w'):
            got[n] = _from_shards(n, got[n])
        return got

    slots = [_send_slot] * len(OVERLAPPED)
    scatter_started = []

    def early_grads(grads):
        gs = [_to_shards(n, grads[n]) if SHARDED[n] == 'row' else grads[n] for n in OVERLAPPED]
        started, tok = _direct_start(gs, gs, slots, "scatter_early_start")
        scatter_started.append(started)
        return tok

    loss, grad_x, grads = _local_step(x[0], mem[0], target[0], full, late_weights, early_grads)
    res = {}
    parts = _direct_wait(scatter_started[0], slots, grad_x, "scatter_early_wait")
    for n, part in zip(OVERLAPPED, parts):
        outs = _sum_adam(part, w[n][0], m[n][0], v[n][0], name="adam_" + n)
        for kind, t in zip(('grad', 'delta', 'new_m', 'new_v'), outs):
            res[kind, n] = t[None]

    grads['w_in'] = _w_in_to_shards(grads.pop('w_main'), grads.pop('w_ba'), ba_lo, ba_hi)
    grads['conv_w'] = _to_shards('conv_w', grads['conv_w'])
    last = ['w_in', 'conv_w']
    gs = [grads[n] for n in last]
    c_idx = lax.axis_index("c").astype(jnp.int32).reshape(1)
    from_sibling = _pair_scatter(gs, "scatter_pair")
    chip_sums = [_pair_sum(g, got, c_idx, BF16 if n in GATHER_BF16 else F32, "pair_sum_" + n)
                 for n, g, got in zip(last, gs, from_sibling)]
    parts = _cross_scatter(chip_sums, "scatter_cross")
    for n, part in zip(last, parts):
        outs = _sum_adam(part, w[n][0], m[n][0], v[n][0], name="adam_" + n)
        for kind, t in zip(('grad', 'delta', 'new_m', 'new_v'), outs):
            res[kind, n] = t[None]

    small = _pack([grads[n].reshape(w[n].shape) for n in REPLICATED] + [loss[:1, :1]], F32)
    allp = _gather([small], "gather_small")[0]
    zero = jnp.zeros((1, 1), F32)
    outs = _sum_adam(allp, *[_pack([t[n] for n in REPLICATED] + [zero], F32) for t in (w, m, v)], name="adam_small")
    shapes = [w[n].shape for n in REPLICATED] + [(1, 1)]
    for kind, buf in zip(('grad', 'delta', 'new_m', 'new_v'), outs):
        got = _unpack(buf, shapes)
        for n, t in zip(REPLICATED, got):
            res[kind, n] = t
        if kind == 'grad':
            total_loss = got[-1].reshape(())
    out = [total_loss, grad_x[None]]
    for kind in ('grad', 'delta', 'new_m', 'new_v'):
        out += [res[kind, n] for n in WEIGHTS]
    return tuple(out)


def kernel(x, mem, norm_g, w_in, conv_w, gdn_a_log, gdn_dt_bias, gdn_norm_g, s5_lambda_re, s5_lambda_im, s5_log_dt, s5_b_re, s5_b_im, s5_c_re, s5_c_im, s5_d, s5_w_glu, mem_norm_g, w_kv_mem, w_br_a, w_br_b, w_br_c, w_out, final_g, loss_target, m_norm_g, m_w_in, m_conv_w, m_gdn_a_log, m_gdn_dt_bias, m_gdn_norm_g, m_s5_lambda_re, m_s5_lambda_im, m_s5_log_dt, m_s5_b_re, m_s5_b_im, m_s5_c_re, m_s5_c_im, m_s5_d, m_s5_w_glu, m_mem_norm_g, m_w_kv_mem, m_w_br_a, m_w_br_b, m_w_br_c, m_w_out, m_final_g, v_norm_g, v_w_in, v_conv_w, v_gdn_a_log, v_gdn_dt_bias, v_gdn_norm_g, v_s5_lambda_re, v_s5_lambda_im, v_s5_log_dt, v_s5_b_re, v_s5_b_im, v_s5_c_re, v_s5_c_im, v_s5_d, v_s5_w_glu, v_mem_norm_g, v_w_kv_mem, v_w_br_a, v_w_br_b, v_w_br_c, v_w_out, v_final_g):
    a = dict(locals())
    w = {n: a[n] for n in WEIGHTS}
    m = {n: a['m_' + n] for n in WEIGHTS}
    v = {n: a['v_' + n] for n in WEIGHTS}
    return _step(x, mem, loss_target, w, m, v)
```

```python
import functools
import math

import jax
import jax.numpy as jnp
from jax import lax
from jax.experimental import pallas as pl
from jax.experimental.pallas import tpu as pltpu

F32 = jnp.float32
BF16 = jnp.bfloat16
HI = lax.Precision.HIGHEST

EPS = 1e-6
CHUNK = 64
GDN_HEAD_DIM = 128
CONV_WIDTH = 4
S5_GROUP = 16
S5_STATE = 64
S5_GROUPS_PER_BLOCK = 8
XA_HEADS = 4
N_DEV = 8
ADAM_LR, ADAM_B1, ADAM_B2, ADAM_EPS, ADAM_WD, ADAM_STEP = 0.001, 0.9, 0.999, 1e-08, 0.01, 10

VMEM_LIMIT_BYTES = 56 * 1024 * 1024
SCAN_LANES = 512
PACK_WIDTH = 512
PACK_ROWS = 256

WEIGHTS = ['norm_g', 'w_in', 'conv_w', 'gdn_a_log', 'gdn_dt_bias', 'gdn_norm_g', 's5_lambda_re', 's5_lambda_im',
           's5_log_dt', 's5_b_re', 's5_b_im', 's5_c_re', 's5_c_im', 's5_d', 's5_w_glu', 'mem_norm_g', 'w_kv_mem',
           'w_br_a', 'w_br_b', 'w_br_c', 'w_out', 'final_g']
SHARDED = {'w_in': 'col', 'conv_w': 'col', 's5_w_glu': 'col', 'w_kv_mem': 'row', 'w_br_a': 'col', 'w_br_b': 'col',
           'w_br_c': 'col', 'w_out': 'row'}
GATHER_BF16 = ['w_in', 's5_w_glu', 'w_kv_mem', 'w_br_a', 'w_br_b', 'w_br_c', 'w_out']
REPLICATED = [n for n in WEIGHTS if n not in SHARDED]
OVERLAPPED = ['s5_w_glu', 'w_kv_mem', 'w_br_a', 'w_br_b', 'w_br_c', 'w_out']


def _cparams(sem=None):
    return pltpu.CompilerParams(dimension_semantics=sem, vmem_limit_bytes=VMEM_LIMIT_BYTES)


def _pick(dim, pref):
    t = (min(pref, dim) // 128) * 128
    while t >= 128:
        if dim % t == 0:
            return t
        t -= 128
    return dim


def _make_dots(prep, precision):
    def raw(a, b, dims):
        return lax.dot_general(prep(a), prep(b), (dims, ((), ())), preferred_element_type=F32, precision=precision)

    @jax.custom_vjp
    def nn(a, b):
        return raw(a, b, ((1,), (0,)))

    @jax.custom_vjp
    def nt(a, b):
        return raw(a, b, ((1,), (1,)))

    @jax.custom_vjp
    def tn(a, b):
        return raw(a, b, ((0,), (0,)))

    nn.defvjp(lambda a, b: (nn(a, b), (a, b)), lambda r, ct: (nt(ct, r[1]), tn(r[0], ct)))
    nt.defvjp(lambda a, b: (nt(a, b), (a, b)), lambda r, ct: (nn(ct, r[1]), tn(ct, r[0])))
    tn.defvjp(lambda a, b: (tn(a, b), (a, b)), lambda r, ct: (nt(r[1], ct), nn(r[0], ct)))
    return nn, nt, tn


_bnn, _bnt, _btn = _make_dots(lambda a: a.astype(BF16), None)
_hnn, _hnt, _htn = _make_dots(lambda a: a.astype(F32), HI)
_mnn, _mnt, _mtn = _make_dots(lambda a: a.astype(F32), lax.Precision.HIGH)


def _mm(a, b, *, name, ta=False, tb=False, out_dtype=F32, addend=None, tm=512, tn=1024, tk=1024, b_shards=False,
        out_shards=0):
    m, k = (a.shape[1], a.shape[0]) if ta else a.shape
    brows, bcols = (b.shape[1], b.shape[0] * b.shape[2]) if b_shards else b.shape
    n = brows if tb else bcols
    assert (bcols if tb else brows) == k, (a.shape, b.shape, ta, tb)
    tm, tn, tk = _pick(m, tm), _pick(n, tn), _pick(k, tk)
    bcs = ocs = 0
    if b_shards:
        bcs = b.shape[2]
        assert bcs % 128 == 0 and (tk if tb else tn) % bcs == 0
    if out_shards:
        ocs = n // out_shards
        assert ocs % 128 == 0 and tn % ocs == 0
    nk = k // tk
    dims = ((0 if ta else 1,), (1 if tb else 0,))

    def body(*refs):
        if addend is None:
            a_ref, b_ref, o_ref, acc_ref = refs
        else:
            a_ref, b_ref, add_ref, o_ref, acc_ref = refs
        kk = pl.program_id(2)

        @pl.when(kk == 0)
        def _():
            acc_ref[...] = jnp.zeros_like(acc_ref)

        dot = lambda x, y: lax.dot_general(x.astype(BF16), y.astype(BF16), (dims, ((), ())), preferred_element_type=F32)
        if not b_shards:
            acc_ref[...] += dot(a_ref[...], b_ref[...])
        elif tb:
            for g in range(tk // bcs):
                acc_ref[...] += dot(a_ref[:, g * bcs:(g + 1) * bcs], b_ref[g])
        else:
            for g in range(tn // bcs):
                acc_ref[:, g * bcs:(g + 1) * bcs] += dot(a_ref[...], b_ref[g])

        @pl.when(kk == nk - 1)
        def _():
            r = acc_ref[...]
            if addend is not None:
                r = r + add_ref[...].astype(F32)
            if out_shards:
                for g in range(tn // ocs):
                    o_ref[g] = r[:, g * ocs:(g + 1) * ocs].astype(o_ref.dtype)
            else:
                o_ref[...] = r.astype(o_ref.dtype)

    a_spec = pl.BlockSpec((tk, tm), lambda i, j, kk: (kk, i)) if ta else pl.BlockSpec((tm, tk), lambda i, j, kk: (i, kk))
    if b_shards:
        b_spec = (pl.BlockSpec((tk // bcs, tn, bcs), lambda i, j, kk: (kk, j, 0)) if tb
                  else pl.BlockSpec((tn // bcs, tk, bcs), lambda i, j, kk: (j, kk, 0)))
    else:
        b_spec = (pl.BlockSpec((tn, tk), lambda i, j, kk: (j, kk)) if tb
                  else pl.BlockSpec((tk, tn), lambda i, j, kk: (kk, j)))
    if out_shards:
        o_spec = pl.BlockSpec((tn // ocs, tm, ocs), lambda i, j, kk: (j, i, 0))
        out_shape = jax.ShapeDtypeStruct((out_shards, m, ocs), out_dtype)
    else:
        o_spec = pl.BlockSpec((tm, tn), lambda i, j, kk: (i, j))
        out_shape = jax.ShapeDtypeStruct((m, n), out_dtype)
    in_specs = [a_spec, b_spec] + ([o_spec] if addend is not None else [])
    args = (a, b) + ((addend,) if addend is not None else ())
    return pl.pallas_call(
        body, name=name, grid=(m // tm, n // tn, nk), in_specs=in_specs, out_specs=o_spec,
        out_shape=out_shape, scratch_shapes=[pltpu.VMEM((tm, tn), F32)],
        compiler_params=_cparams(("parallel", "parallel", "arbitrary")))(*args)


def _rt(arr, ts, cb=0, w=None):
    w = arr.shape[1] if w is None else w
    return (arr, (ts, w), lambda i, cb=cb: (i, cb))


def _whole(p):
    return pl.BlockSpec(p.shape, lambda i, nd=p.ndim: (0,) * nd)


def _tile_fwd(f, name, n, acts, params, outs):
    na, npar = len(acts), len(params)

    def body(*refs):
        res = f(*[r[...] for r in refs[:na + npar]])
        for r, v in zip(refs[na + npar:], res):
            r[...] = v.astype(r.dtype)

    in_specs = [pl.BlockSpec(b, m) for _, b, m in acts] + [_whole(p) for p in params]
    out = pl.pallas_call(
        body, name=name, grid=(n,), in_specs=in_specs,
        out_specs=[pl.BlockSpec(b, m) for _, _, b, m in outs],
        out_shape=[jax.ShapeDtypeStruct(s, d) for s, d, _, _ in outs],
        compiler_params=_cparams(("parallel",)))(*[a for a, _, _ in acts], *params)
    return out


def _tile_bwd(f, name, n, acts, params, cts, agrads, pgrads):
    na, npar, nc = len(acts), len(params), len(cts)

    def body(*refs):
        i = pl.program_id(0)
        ins = [r[...] for r in refs[:na + npar]]
        outs, vjp = jax.vjp(f, *ins)
        g = vjp(tuple(c[...].astype(o.dtype) for c, o in zip(refs[na + npar:na + npar + nc], outs)))
        orefs = refs[na + npar + nc:]
        k = 0
        for j in range(na):
            if agrads[j] is not None:
                orefs[k][...] = g[j].astype(orefs[k].dtype)
                k += 1
        for j in range(npar):
            if pgrads[j]:
                o = orefs[k]

                @pl.when(i == 0)
                def _(o=o):
                    o[...] = jnp.zeros_like(o)

                o[...] += g[na + j].astype(F32)
                k += 1

    in_specs = ([pl.BlockSpec(b, m) for _, b, m in acts] + [_whole(p) for p in params]
                + [pl.BlockSpec(b, m) for _, b, m in cts])
    out_specs = [pl.BlockSpec(g[2], g[3]) for g in agrads if g is not None]
    out_shape = [jax.ShapeDtypeStruct(g[0], g[1]) for g in agrads if g is not None]
    for p, flag in zip(params, pgrads):
        if flag:
            out_specs.append(_whole(p))
            out_shape.append(jax.ShapeDtypeStruct(p.shape, F32))
    return pl.pallas_call(
        body, name=name, grid=(n,), in_specs=in_specs, out_specs=out_specs, out_shape=out_shape,
        compiler_params=_cparams(("arbitrary",)))(*[a for a, _, _ in acts], *params, *[c for c, _, _ in cts])


def _silu(x):
    return x * jax.nn.sigmoid(x)


def _rms(x, g):
    x = x.astype(F32)
    return (x * lax.rsqrt(jnp.mean(x * x, axis=-1, keepdims=True) + EPS) * g,)


def _shift_down(x, s):
    row = lax.broadcasted_iota(jnp.int32, x.shape, 0)
    return jnp.where(row >= s, pltpu.roll(x, s, 0), 0.0)


def _shift_up(x, s):
    n = x.shape[0]
    row = lax.broadcasted_iota(jnp.int32, x.shape, 0)
    return jnp.where(row < n - s, pltpu.roll(x, n - s, 0), 0.0)


@functools.partial(jax.custom_vjp, nondiff_argnums=(1,))
def _shift(x, s):
    return _shift_down(x, s)


_shift.defvjp(lambda x, s: (_shift_down(x, s), None), lambda s, _, ct: (_shift_up(ct, s),))


def _gdn_pre(mode):
    def f(x, w):
        y = x * w[CONV_WIDTH - 1:CONV_WIDTH, :]
        for j in range(CONV_WIDTH - 1):
            y = y + _shift(x, CONV_WIDTH - 1 - j) * w[j:j + 1, :]
        y = _silu(y)
        if mode != 'v':
            y = y * lax.rsqrt(jnp.sum(y * y, axis=-1, keepdims=True) + EPS)
        if mode == 'q':
            y = y * (GDN_HEAD_DIM ** -0.5)
        return (y,)
    return f


def _softplus(x):
    return jnp.maximum(x, 0.0) + jnp.log1p(jnp.exp(-jnp.abs(x)))


def _gdn_gates(ba, alog, dtb, e_beta, e_g):
    beta = jax.nn.sigmoid(ba)
    g = -jnp.exp(alog) * _softplus(ba + dtb)
    return _hnn(beta, lax.stop_gradient(e_beta)), _hnn(g, lax.stop_gradient(e_g))


@jax.custom_vjp
def _inverse_known(neg, t):
    return t


_inverse_known.defvjp(lambda neg, t: (t, t), lambda t, ct: (_mtn(t, _mnt(ct, t)), jnp.zeros_like(t)))


def _gdn_intra(q, k, v, gb, bb, t_known=None):
    n, c = len(q), q[0].shape[0]
    ri = lax.broadcasted_iota(jnp.int32, (c, c), 0)
    ci = lax.broadcasted_iota(jnp.int32, (c, c), 1)
    incl, strict = ri >= ci, ri > ci
    tri = incl.astype(F32)
    eye = (ri == ci).astype(F32)
    each = range(n)
    gc = [_hnn(tri, gb[i]) for i in each]
    decay = [jnp.exp(jnp.where(incl, gc[i][:, :c] - gc[i].T[:c, :], -1e30)) for i in each]
    kb = [k[i] * bb[i] for i in each]
    kk = [_bnt(kb[i], k[i]) for i in each]
    qk = [_bnt(q[i], k[i]) for i in each]
    p = [jnp.where(strict, -(kk[i] * decay[i]), 0.0) for i in each]
    if t_known is None:
        t = [eye + p[i] for i in each]
        for _ in range(int(math.log2(c)) - 1):
            p = [_mnn(p[i], p[i]) for i in each]
            tp = [_mnn(t[i], p[i]) for i in each]
            t = [t[i] + tp[i] for i in each]
    else:
        t = [_inverse_known(p[i], t_known[i]) for i in each]
    egc = [jnp.exp(gc[i]) for i in each]
    u_val = [_mnn(t[i], v[i] * bb[i]) for i in each]
    w_dec = [_mnn(t[i], kb[i] * egc[i]) for i in each]
    qk = [qk[i] * decay[i] for i in each]
    gl = [jnp.sum(gb[i], axis=0, keepdims=True) for i in each]
    return w_dec, u_val, qk, [q[i] * egc[i] for i in each], [k[i] * jnp.exp(gl[i] - gc[i]) for i in each], t


def _gdn_inter(w_dec, u_val, qk, q_dec, k_dec, gb, state):
    each = range(len(state))
    ws = [_bnn(w_dec[i], state[i]) for i in each]
    qs = [_bnn(q_dec[i], state[i]) for i in each]
    v_new = [u_val[i] - ws[i] for i in each]
    qv = [_bnn(qk[i], v_new[i]) for i in each]
    kv = [_btn(k_dec[i], v_new[i]) for i in each]
    decayed = [state[i] * jnp.exp(jnp.sum(gb[i], axis=0, keepdims=True)) for i in each]
    return [qs[i] + qv[i] for i in each], [decayed[i] + kv[i] for i in each]


def _gdn_post(o, z, g):
    parts = []
    for h in range(o.shape[1] // GDN_HEAD_DIM):
        oh = o[:, h * GDN_HEAD_DIM:(h + 1) * GDN_HEAD_DIM]
        parts.append(oh * lax.rsqrt(jnp.mean(oh * oh, axis=-1, keepdims=True) + EPS) * g)
    y = parts[0] if len(parts) == 1 else jnp.concatenate(parts, axis=1)
    return (y * _silu(z),)


def _gelu(x):
    return 0.5 * x * (1.0 + jnp.tanh(0.7978845608028654 * (x + 0.044715 * x * x * x)))


def _s5_post1(ylin, xb, d):
    return (_gelu(ylin + d * xb),)


def _s5_post2(t, z):
    w = t.shape[1] // 2
    return (t[:, :w] * jax.nn.sigmoid(t[:, w:]) * _silu(z),)


def _attn(q, z, kv):
    w = q.shape[1]
    hd = w // XA_HEADS
    parts = []
    for h in range(XA_HEADS):
        s = _bnt(q[:, h * hd:(h + 1) * hd], kv[:, h * hd:(h + 1) * hd]) * (hd ** -0.5)
        s = s - jnp.max(s, axis=-1, keepdims=True)
        e = jnp.exp(s)
        p = e / jnp.sum(e, axis=-1, keepdims=True)
        parts.append(_bnn(p, kv[:, w + h * hd:w + (h + 1) * hd]))
    return (jnp.concatenate(parts, axis=1) * _silu(z),)


def _merge(g0, g1, g2, pa, pb, pc):
    return (jax.nn.sigmoid(g0) * pa + jax.nn.sigmoid(g1) * pb + jax.nn.sigmoid(g2) * pc,)


def _s5_params(lr, li, logdt, br, bi, e):
    dt = jnp.exp(logdt)
    mag = jnp.exp(lr * dt)
    ab_re, ab_im = mag * jnp.cos(li * dt), mag * jnp.sin(li * dt)
    den = lr * lr + li * li
    nr, ni = ab_re - 1.0, ab_im
    e = lax.stop_gradient(e)
    cre = _hnn((nr * lr + ni * li) / den, e)
    cim = _hnn((ni * lr - nr * li) / den, e)
    return ab_re, ab_im, cre * br - cim * bi, cre * bi + cim * br


def _gdn_blocks(s, w, per_step):
    nh, nc = w // GDN_HEAD_DIM, s // CHUNK
    cpb = math.gcd(per_step, nc)
    return nh, nc, cpb, nc // cpb, (cpb * CHUNK, w), (cpb * CHUNK, nh * CHUNK)


def _gdn_pairs(cpb, nh):
    wide, narrow = [], []
    for cb in range(cpb):
        rows = slice(cb * CHUNK, (cb + 1) * CHUNK)
        for h in range(nh):
            wide.append((rows, slice(h * GDN_HEAD_DIM, (h + 1) * GDN_HEAD_DIM)))
            narrow.append((rows, slice(h * CHUNK, (h + 1) * CHUNK)))
    return wide, narrow


def _gdn_intra_fwd(q, k, v, gb, bb, per_step=4):
    s, w = q.shape
    nh, nc, cpb, n, wide, narrow = _gdn_blocks(s, w, per_step)

    def body(q_ref, k_ref, v_ref, g_ref, b_ref, wd_ref, uv_ref, qk_ref, qd_ref, kd_ref, t_ref):
        wide, narrow = _gdn_pairs(cpb, nh)
        res = _gdn_intra(*[[r[ix] for ix in wide] for r in (q_ref, k_ref, v_ref, g_ref, b_ref)])
        for ref, vals, where in zip((wd_ref, uv_ref, qk_ref, qd_ref, kd_ref, t_ref), res,
                                    (wide, wide, narrow, wide, wide, narrow)):
            for ix, val in zip(where, vals):
                ref[ix] = val

    bw = pl.BlockSpec(wide, lambda i: (i, 0))
    bn = pl.BlockSpec(narrow, lambda i: (i, 0))
    fw = jax.ShapeDtypeStruct((s, w), F32)
    fn = jax.ShapeDtypeStruct((s, nh * CHUNK), F32)
    return pl.pallas_call(
        body, name="gdn_intra", grid=(n,), in_specs=[bw] * 5, out_specs=[bw, bw, bn, bw, bw, bn],
        out_shape=[fw, fw, fn, fw, fw, fn], compiler_params=_cparams(("parallel",)))(q, k, v, gb, bb)


def _gdn_intra_bwd(q, k, v, gb, bb, t, cts, dgb_inter, per_step=4):
    s, w = q.shape
    nh, nc, cpb, n, wide, narrow = _gdn_blocks(s, w, per_step)

    def body(q_ref, k_ref, v_ref, g_ref, b_ref, t_ref, cwd, cuv, cqk, cqd, ckd, dgi, dq_ref, dk_ref, dv_ref, dg_ref,
             db_ref):
        wide, narrow = _gdn_pairs(cpb, nh)
        t_known = [t_ref[ix] for ix in narrow]
        _, vjp = jax.vjp(lambda *a: _gdn_intra(*a, t_known=t_known)[:5],
                         *[[r[ix] for ix in wide] for r in (q_ref, k_ref, v_ref, g_ref, b_ref)])
        cts = tuple([r[ix] for ix in where] for r, where in zip((cwd, cuv, cqk, cqd, ckd),
                                                               (wide, wide, narrow, wide, wide)))
        dq, dk, dv, dg, db = vjp(cts)
        for j, ix in enumerate(wide):
            dq_ref[ix], dk_ref[ix], dv_ref[ix], db_ref[ix] = dq[j], dk[j], dv[j], db[j]
            dg_ref[ix] = dg[j] + dgi[ix]

    bw = pl.BlockSpec(wide, lambda i: (i, 0))
    bn = pl.BlockSpec(narrow, lambda i: (i, 0))
    return pl.pallas_call(
        body, name="gdn_intra_bwd", grid=(n,), in_specs=[bw] * 5 + [bn, bw, bw, bn, bw, bw, bw], out_specs=[bw] * 5,
        out_shape=[jax.ShapeDtypeStruct((s, w), F32)] * 5,
        compiler_params=_cparams(("parallel",)))(q, k, v, gb, bb, t, *cts, dgb_inter)


def _gdn_inter_fwd(wd, uv, qk, qd, kd, gb, per_step=4):
    s, w = wd.shape
    nh, nc, cpb, n, wide, narrow = _gdn_blocks(s, w, per_step)
    hd = GDN_HEAD_DIM

    def body(wd_ref, uv_ref, qk_ref, qd_ref, kd_ref, g_ref, o_ref, st_ref, state):
        @pl.when(pl.program_id(0) == 0)
        def _():
            state[...] = jnp.zeros_like(state)

        wide, narrow = _gdn_pairs(cpb, nh)
        st = [state[h] for h in range(nh)]
        for cb in range(cpb):
            wi, na = wide[cb * nh:(cb + 1) * nh], narrow[cb * nh:(cb + 1) * nh]
            for h in range(nh):
                st_ref[cb, h] = st[h]
            o, st = _gdn_inter([wd_ref[ix] for ix in wi], [uv_ref[ix] for ix in wi], [qk_ref[ix] for ix in na],
                               [qd_ref[ix] for ix in wi], [kd_ref[ix] for ix in wi], [g_ref[ix] for ix in wi], st)
            for h in range(nh):
                o_ref[wi[h]] = o[h]
        for h in range(nh):
            state[h] = st[h]

    bw = pl.BlockSpec(wide, lambda i: (i, 0))
    bn = pl.BlockSpec(narrow, lambda i: (i, 0))
    return pl.pallas_call(
        body, name="gdn_inter", grid=(n,), in_specs=[bw, bw, bn, bw, bw, bw],
        out_specs=[bw, pl.BlockSpec((cpb, nh, hd, hd), lambda i: (i, 0, 0, 0))],
        out_shape=[jax.ShapeDtypeStruct((s, w), F32), jax.ShapeDtypeStruct((nc, nh, hd, hd), F32)],
        scratch_shapes=[pltpu.VMEM((nh, hd, hd), F32)],
        compiler_params=_cparams(("arbitrary",)))(wd, uv, qk, qd, kd, gb)


def _gdn_inter_bwd(wd, uv, qk, qd, kd, gb, states, do, per_step=4):
    s, w = wd.shape
    nh, nc, cpb, n, wide, narrow = _gdn_blocks(s, w, per_step)
    hd = GDN_HEAD_DIM

    def body(wd_ref, uv_ref, qk_ref, qd_ref, kd_ref, g_ref, st_ref, do_ref, cwd, cuv, cqk, cqd, ckd, dg_ref, dstate):
        @pl.when(pl.program_id(0) == 0)
        def _():
            dstate[...] = jnp.zeros_like(dstate)

        wide, narrow = _gdn_pairs(cpb, nh)
        dst = [dstate[h] for h in range(nh)]
        for cb in reversed(range(cpb)):
            wi, na = wide[cb * nh:(cb + 1) * nh], narrow[cb * nh:(cb + 1) * nh]
            _, vjp = jax.vjp(_gdn_inter, [wd_ref[ix] for ix in wi], [uv_ref[ix] for ix in wi],
                             [qk_ref[ix] for ix in na], [qd_ref[ix] for ix in wi], [kd_ref[ix] for ix in wi],
                             [g_ref[ix] for ix in wi], [st_ref[cb, h] for h in range(nh)])
            dwd, duv, dqk, dqd, dkd, dg, dst = vjp(([do_ref[ix] for ix in wi], dst))
            for h in range(nh):
                cwd[wi[h]], cuv[wi[h]], cqk[na[h]], cqd[wi[h]], ckd[wi[h]], dg_ref[wi[h]] = (
                    dwd[h], duv[h], dqk[h], dqd[h], dkd[h], dg[h])
        for h in range(nh):
            dstate[h] = dst[h]

    bw = pl.BlockSpec(wide, lambda i: (n - 1 - i, 0))
    bn = pl.BlockSpec(narrow, lambda i: (n - 1 - i, 0))
    fw = jax.ShapeDtypeStruct((s, w), F32)
    return pl.pallas_call(
        body, name="gdn_inter_bwd", grid=(n,),
        in_specs=[bw, bw, bn, bw, bw, bw, pl.BlockSpec((cpb, nh, hd, hd), lambda i: (n - 1 - i, 0, 0, 0)), bw],
        out_specs=[bw, bw, bn, bw, bw, bw],
        out_shape=[fw, fw, jax.ShapeDtypeStruct((s, nh * CHUNK), F32), fw, fw, fw],
        scratch_shapes=[pltpu.VMEM((nh, hd, hd), F32)],
        compiler_params=_cparams(("arbitrary",)))(wd, uv, qk, qd, kd, gb, states, do)


def _s5_coef(ar, ai):
    nl = ar.shape[1]

    def body(ar_ref, ai_ref, o_ref):
        row = lax.broadcasted_iota(jnp.int32, (8, nl), 0)
        for base, sign in ((0, 1.0), (8, -1.0)):
            pr = [jnp.broadcast_to(ar_ref[...], (8, nl))]
            pi = [jnp.broadcast_to(ai_ref[...], (8, nl)) * sign]
            for _ in range(7):
                pr.append(pr[-1] * pr[0] - pi[-1] * pi[0])
                pi.append(pr[-2] * pi[0] + pi[-1] * pr[0])
            for j, d in enumerate((1, 2, 4)):
                m = (row >= d) if base == 0 else (row <= 7 - d)
                o_ref[base + 2 * j] = jnp.where(m, pr[d - 1], 0.0)
                o_ref[base + 2 * j + 1] = jnp.where(m, pi[d - 1], 0.0)
            cr, ci = jnp.zeros((8, nl), F32), jnp.zeros((8, nl), F32)
            for t in range(8):
                e = t if base == 0 else 7 - t
                cr = jnp.where(row == t, pr[e], cr)
                ci = jnp.where(row == t, pi[e], ci)
            o_ref[base + 6] = cr
            o_ref[base + 7] = ci

    return pl.pallas_call(body, name="s5_coef", out_shape=jax.ShapeDtypeStruct((16, 8, nl), F32),
                          compiler_params=_cparams())(ar, ai)


def _scan_tile(src_re, src_im, dst_re, dst_im, coef_ref, carry_re, carry_im, ts, reverse, extra=None):
    nl = src_re.shape[1]
    base = 8 if reverse else 0
    ng = ts // 8
    for lc in range(nl // SCAN_LANES):
        ln = slice(lc * SCAN_LANES, (lc + 1) * SCAN_LANES)
        m = [coef_ref[base + j, :, ln] for j in range(8)]
        row = lax.broadcasted_iota(jnp.int32, (8, SCAN_LANES), 0)

        def step(r, carry, ln=ln, m=m, row=row):
            grp = (ng - 1 - r) if reverse else r
            rows = pl.ds(pl.multiple_of(grp * 8, 8), 8)
            xr, xi = src_re[rows, ln], src_im[rows, ln]
            for j, d in enumerate((1, 2, 4)):
                sh = 8 - d if reverse else d
                sr, si = pltpu.roll(xr, sh, 0), pltpu.roll(xi, sh, 0)
                mr, mi = m[2 * j], m[2 * j + 1]
                xr, xi = xr + mr * sr - mi * si, xi + mr * si + mi * sr
            cr, ci = carry[0], carry[1]
            hr = xr + m[6] * cr - m[7] * ci
            hi = xi + m[6] * ci + m[7] * cr
            dst_re[rows, ln] = hr
            dst_im[rows, ln] = hi
            edge = 0 if reverse else 7
            out = (jnp.broadcast_to(hr[edge:edge + 1, :], hr.shape), jnp.broadcast_to(hi[edge:edge + 1, :], hi.shape))
            if extra is not None:
                h_re, h_im, halo_re, halo_im, first, _, _ = extra
                prev = pl.ds(pl.multiple_of(jnp.maximum(grp - 1, 0) * 8, 8), 8)
                use_halo = grp == 0
                pr = jnp.where(use_halo, halo_re[:, ln] * first, h_re[prev, ln])
                pi = jnp.where(use_halo, halo_im[:, ln] * first, h_im[prev, ln])
                qr = jnp.where(row == 0, jnp.broadcast_to(pr[7:8, :], pr.shape), pltpu.roll(h_re[rows, ln], 1, 0))
                qi = jnp.where(row == 0, jnp.broadcast_to(pi[7:8, :], pi.shape), pltpu.roll(h_im[rows, ln], 1, 0))
                out = out + (carry[2] + hr * qr + hi * qi, carry[3] + hi * qr - hr * qi)
            return out

        init = (carry_re[:, ln], carry_im[:, ln])
        if extra is not None:
            init = init + (extra[5][:, ln], extra[6][:, ln])
        fin = lax.fori_loop(0, ng, step, init)
        carry_re[:, ln] = fin[0]
        carry_im[:, ln] = fin[1]
        if extra is not None:
            extra[5][:, ln] = fin[2]
            extra[6][:, ln] = fin[3]


def _s5_fwd(xb, bb_re, bb_im, c_re, c_im, coef, ts):
    s, w = xb.shape
    nb = bb_re.shape[0]
    nl = nb * 512

    def body(x_ref, bre_ref, bim_ref, cre_ref, cim_ref, coef_ref, hre_ref, him_ref, y_ref, ure, uim, car_re, car_im):
        @pl.when(pl.program_id(0) == 0)
        def _():
            car_re[...] = jnp.zeros_like(car_re)
            car_im[...] = jnp.zeros_like(car_im)

        for b in range(nb):
            xs = x_ref[:, b * 128:(b + 1) * 128].astype(BF16)
            ure[:, b * 512:(b + 1) * 512] = jnp.dot(xs, bre_ref[b], preferred_element_type=F32)
            uim[:, b * 512:(b + 1) * 512] = jnp.dot(xs, bim_ref[b], preferred_element_type=F32)
        _scan_tile(ure, uim, hre_ref, him_ref, coef_ref, car_re, car_im, ts, False)
        for b in range(nb):
            hr = hre_ref[:, b * 512:(b + 1) * 512].astype(BF16)
            hi = him_ref[:, b * 512:(b + 1) * 512].astype(BF16)
            y_ref[:, b * 128:(b + 1) * 128] = (jnp.dot(hr, cre_ref[b], preferred_element_type=F32)
                                               - jnp.dot(hi, cim_ref[b], preferred_element_type=F32))

    row = lambda wd: pl.BlockSpec((ts, wd), lambda i: (i, 0))
    return pl.pallas_call(
        body, name="s5_fwd", grid=(s // ts,),
        in_specs=[row(w), _whole(bb_re), _whole(bb_im), _whole(c_re), _whole(c_im), _whole(coef)],
        out_specs=[row(nl), row(nl), row(w)],
        out_shape=[jax.ShapeDtypeStruct((s, nl), F32), jax.ShapeDtypeStruct((s, nl), F32),
                   jax.ShapeDtypeStruct((s, w), F32)],
        scratch_shapes=[pltpu.VMEM((ts, nl), F32), pltpu.VMEM((ts, nl), F32), pltpu.VMEM((8, nl), F32),
                        pltpu.VMEM((8, nl), F32)],
        compiler_params=_cparams(("arbitrary",)))(xb, bb_re, bb_im, c_re, c_im, coef)


def _s5_bwd(dy, xb, h_re, h_im, bb_re, bb_im, c_re, c_im, coef, ts):
    s, w = xb.shape
    nb = bb_re.shape[0]
    nl = nb * 512
    nt = s // ts

    def body(dy_ref, x_ref, hre_ref, him_ref, halo_re, halo_im, bre_ref, bim_ref, cre_ref, cim_ref, coef_ref,
             dx_ref, dbre_ref, dbim_ref, dcre_ref, dcim_ref, dare_ref, daim_ref, gre, gim, car_re, car_im):
        i = pl.program_id(0)

        @pl.when(i == 0)
        def _():
            for r in (car_re, car_im, dbre_ref, dbim_ref, dcre_ref, dcim_ref, dare_ref, daim_ref):
                r[...] = jnp.zeros_like(r)

        for b in range(nb):
            dyb = dy_ref[:, b * 128:(b + 1) * 128].astype(BF16)
            gre[:, b * 512:(b + 1) * 512] = lax.dot_general(dyb, cre_ref[b], (((1,), (1,)), ((), ())),
                                                            preferred_element_type=F32)
            gim[:, b * 512:(b + 1) * 512] = -lax.dot_general(dyb, cim_ref[b], (((1,), (1,)), ((), ())),
                                                             preferred_element_type=F32)
            hr = hre_ref[:, b * 512:(b + 1) * 512].astype(BF16)
            hi = him_ref[:, b * 512:(b + 1) * 512].astype(BF16)
            dcre_ref[b] += lax.dot_general(hr, dyb, (((0,), (0,)), ((), ())), preferred_element_type=F32)
            dcim_ref[b] -= lax.dot_general(hi, dyb, (((0,), (0,)), ((), ())), preferred_element_type=F32)
        first = (i != nt - 1).astype(F32)
        _scan_tile(gre, gim, gre, gim, coef_ref, car_re, car_im, ts, True,
                   extra=(hre_ref, him_ref, halo_re, halo_im, first, dare_ref, daim_ref))
        for b in range(nb):
            gr = gre[:, b * 512:(b + 1) * 512].astype(BF16)
            gi = gim[:, b * 512:(b + 1) * 512].astype(BF16)
            xs = x_ref[:, b * 128:(b + 1) * 128].astype(BF16)
            dx_ref[:, b * 128:(b + 1) * 128] = (
                lax.dot_general(gr, bre_ref[b], (((1,), (1,)), ((), ())), preferred_element_type=F32)
                + lax.dot_general(gi, bim_ref[b], (((1,), (1,)), ((), ())), preferred_element_type=F32))
            dbre_ref[b] += lax.dot_general(xs, gr, (((0,), (0,)), ((), ())), preferred_element_type=F32)
            dbim_ref[b] += lax.dot_general(xs, gi, (((0,), (0,)), ((), ())), preferred_element_type=F32)

    row = lambda wd: pl.BlockSpec((ts, wd), lambda i: (nt - 1 - i, 0))
    halo = pl.BlockSpec((8, nl), lambda i: (jnp.maximum((nt - 1 - i) * (ts // 8) - 1, 0), 0))
    return pl.pallas_call(
        body, name="s5_bwd", grid=(nt,),
        in_specs=[row(w), row(w), row(nl), row(nl), halo, halo, _whole(bb_re), _whole(bb_im), _whole(c_re),
                  _whole(c_im), _whole(coef)],
        out_specs=[row(w), _whole(bb_re), _whole(bb_im), _whole(c_re), _whole(c_im),
                   pl.BlockSpec((8, nl), lambda i: (0, 0)), pl.BlockSpec((8, nl), lambda i: (0, 0))],
        out_shape=[jax.ShapeDtypeStruct((s, w), F32), jax.ShapeDtypeStruct(bb_re.shape, F32),
                   jax.ShapeDtypeStruct(bb_im.shape, F32), jax.ShapeDtypeStruct(c_re.shape, F32),
                   jax.ShapeDtypeStruct(c_im.shape, F32), jax.ShapeDtypeStruct((8, nl), F32),
                   jax.ShapeDtypeStruct((8, nl), F32)],
        scratch_shapes=[pltpu.VMEM((ts, nl), F32), pltpu.VMEM((ts, nl), F32), pltpu.VMEM((8, nl), F32),
                        pltpu.VMEM((8, nl), F32)],
        compiler_params=_cparams(("arbitrary",)))(dy, xb, h_re, h_im, h_re, h_im, bb_re, bb_im, c_re, c_im, coef)


def _final(x, mo, target, fg, ts):
    s, d = x.shape

    def f(x, mo, fg, tgt):
        y = _rms(x + mo, fg)[0]
        err = y - tgt
        return 0.5 * jnp.sum(jnp.mean(err * err, axis=-1, keepdims=True), axis=0, keepdims=True)

    def body(x_ref, mo_ref, t_ref, fg_ref, dh_ref, dfg_ref, loss_ref):
        @pl.when(pl.program_id(0) == 0)
        def _():
            dfg_ref[...] = jnp.zeros_like(dfg_ref)
            loss_ref[...] = jnp.zeros_like(loss_ref)

        loss, vjp = jax.vjp(f, x_ref[...], mo_ref[...], fg_ref[...], t_ref[...])
        _, dmo, dfg, _ = vjp(jnp.ones((1, 1), F32))
        dh_ref[...] = dmo
        dfg_ref[...] += dfg
        loss_ref[...] += jnp.broadcast_to(loss, loss_ref.shape)

    row = pl.BlockSpec((ts, d), lambda i: (i, 0))
    return pl.pallas_call(
        body, name="final", grid=(s // ts,), in_specs=[row, row, row, _whole(fg)],
        out_specs=[row, _whole(fg), pl.BlockSpec((8, 128), lambda i: (0, 0))],
        out_shape=[jax.ShapeDtypeStruct((s, d), F32), jax.ShapeDtypeStruct(fg.shape, F32),
                   jax.ShapeDtypeStruct((8, 128), F32)],
        compiler_params=_cparams(("arbitrary",)))(x, mo, target, fg)


N_CHIPS = 4


def _other_chips(x, y):
    return [((1 - x, y), 2 * (1 - x) + y), ((x, 1 - y), 2 * x + 1 - y), ((1 - x, 1 - y), 2 * (1 - x) + 1 - y)]


def _comm_call(body, name, srcs, out_shapes, n_sems):
    n = len(srcs)
    return pl.pallas_call(
        body, name=name, in_specs=[pl.BlockSpec(memory_space=pl.ANY)] * n,
        out_specs=[pl.BlockSpec(memory_space=pl.ANY)] * n, out_shape=out_shapes,
        scratch_shapes=[pltpu.SemaphoreType.DMA((n, n_sems)), pltpu.SemaphoreType.DMA((n, n_sems)),
                        pltpu.SemaphoreType.DMA((n,))],
        compiler_params=pltpu.CompilerParams(has_side_effects=True))(*srcs)


def _gather(srcs, name):
    n = len(srcs)

    def body(*refs):
        src, out = refs[:n], refs[n:2 * n]
        send_sems, recv_sems, local_sems = refs[2 * n:]
        x, y, c = lax.axis_index("x"), lax.axis_index("y"), lax.axis_index("c")
        me, sib_slot, sib = 4 * x + 2 * y + c, 4 * x + 2 * y + 1 - c, (x, y, 1 - c)
        chips = _other_chips(x, y)

        def cp(a, k, src_ref, slot, to):
            return pltpu.make_async_remote_copy(
                src_ref=src_ref, dst_ref=out[a].at[slot], send_sem=send_sems.at[a, k], recv_sem=recv_sems.at[a, k],
                device_id=to, device_id_type=pl.DeviceIdType.MESH)

        local = [pltpu.make_async_copy(src[a], out[a].at[me], local_sems.at[a]) for a in range(n)]
        first = [cp(a, 0, src[a], me, sib) for a in range(n)]
        first += [cp(a, 1 + j, src[a], me, (*chip, c)) for j, (chip, _) in enumerate(chips) for a in range(n)]
        for d in local + first:
            d.start()
        passed = []
        for j, (chip, q) in enumerate(chips):
            for a in range(n):
                cp(a, 1 + j, src[a], 2 * q + c, sib).wait_recv()
                fwd = cp(a, 4 + j, out[a].at[2 * q + c], 2 * q + c, sib)
                fwd.start()
                passed.append(fwd)
        for a in range(n):
            cp(a, 0, src[a], sib_slot, sib).wait_recv()
        for j, (chip, q) in enumerate(chips):
            for a in range(n):
                cp(a, 4 + j, src[a], 2 * q + 1 - c, sib).wait_recv()
        for d in first + passed:
            d.wait_send()
        for d in local:
            d.wait()

    return _comm_call(body, name, srcs, [jax.ShapeDtypeStruct((N_DEV,) + s.shape, s.dtype) for s in srcs], 7)


def _all_peers(x, y, c):
    out = []
    for k in range(1, N_DEV):
        px = 1 - x if k & 4 else x
        py = 1 - y if k & 2 else y
        pc = 1 - c if k & 1 else c
        out.append(((px, py, pc), 4 * px + 2 * py + pc))
    return out


_HBM = pl.BlockSpec(memory_space=pltpu.HBM)
_SEM = pl.BlockSpec(memory_space=pltpu.SEMAPHORE)
_DATAFLOW = pltpu.SideEffectType.DATAFLOW_SIDE_EFFECTING


def _send_whole(ref, slot):
    return ref


def _send_slot(ref, slot):
    return ref.at[slot]


def _direct_copies(src, land, send_sems, recv_sems, picks, arriving):
    x, y, c = lax.axis_index("x"), lax.axis_index("y"), lax.axis_index("c")
    me = 4 * x + 2 * y + c
    out = []
    for k, (pos, slot) in enumerate(_all_peers(x, y, c)):
        for a in range(len(src)):
            sem = a * (N_DEV - 1) + k
            out.append(pltpu.make_async_remote_copy(
                src_ref=picks[a](src[a], slot), dst_ref=land[a].at[slot if arriving else me],
                send_sem=send_sems.at[sem], recv_sem=recv_sems.at[sem], device_id=pos,
                device_id_type=pl.DeviceIdType.MESH))
    return out


def _direct_start(srcs, lands, picks, name):
    n = len(srcs)

    def body(*refs):
        src, land = refs[:n], refs[n:2 * n]
        send_sems, recv_sems = refs[2 * n], refs[2 * n + 1]
        for push in _direct_copies(src, land, send_sems, recv_sems, picks, False):
            push.start()
        refs[-1][...] = jnp.zeros_like(refs[-1])

    arrays = [pltpu.with_memory_space_constraint(t, pltpu.HBM) for t in list(srcs) + list(lands)]
    outs = pl.pallas_call(
        body, name=name, in_specs=[_HBM] * (2 * n),
        out_specs=(_SEM, _SEM, *[_HBM] * (2 * n), pl.BlockSpec(memory_space=pltpu.VMEM)),
        out_shape=(pltpu.SemaphoreType.DMA((n * (N_DEV - 1),)), pltpu.SemaphoreType.DMA((n * (N_DEV - 1),)),
                   *[pltpu.HBM(t.shape, t.dtype) for t in arrays], jax.ShapeDtypeStruct((8, 128), F32)),
        input_output_aliases={i: 2 + i for i in range(2 * n)},
        compiler_params=pltpu.CompilerParams(has_side_effects=_DATAFLOW))(*arrays)
    return outs[:-1], outs[-1]


def _direct_wait(started, picks, after, name):
    send_sems, recv_sems, *thru = started
    n = len(thru) // 2

    def body(*refs):
        src, land = refs[:n], refs[n:2 * n]
        for arrive in _direct_copies(src, land, refs[2 * n], refs[2 * n + 1], picks, True):
            arrive.wait_send()
            arrive.wait_recv()

    outs = pl.pallas_call(
        body, name=name, in_specs=[_HBM] * (2 * n) + [_SEM, _SEM, pl.BlockSpec(memory_space=pl.ANY)],
        out_specs=[_HBM] * (2 * n), out_shape=[pltpu.HBM(t.shape, t.dtype) for t in thru],
        input_output_aliases={i: i for i in range(2 * n)},
        compiler_params=pltpu.CompilerParams(has_side_effects=_DATAFLOW))(*thru, send_sems, recv_sems, after)
    return outs[n:]


def _pair_scatter(gs, name):
    n = len(gs)

    def body(*refs):
        src, out = refs[:n], refs[n:2 * n]
        send_sems, recv_sems, _ = refs[2 * n:]
        x, y, c = lax.axis_index("x"), lax.axis_index("y"), lax.axis_index("c")
        sends = []
        for q in range(N_CHIPS):
            for a in range(n):
                d = pltpu.make_async_remote_copy(
                    src_ref=src[a].at[2 * q + 1 - c], dst_ref=out[a].at[q], send_sem=send_sems.at[a, q],
                    recv_sem=recv_sems.at[a, q], device_id=(x, y, 1 - c), device_id_type=pl.DeviceIdType.MESH)
                d.start()
                sends.append(d)
        for d in sends:
            d.wait_recv()
        for d in sends:
            d.wait_send()

    return _comm_call(body, name, gs, [jax.ShapeDtypeStruct((N_CHIPS,) + g.shape[1:], g.dtype) for g in gs], N_CHIPS)


def _cross_scatter(ps, name):
    n = len(ps)

    def body(*refs):
        src, out = refs[:n], refs[n:2 * n]
        send_sems, recv_sems, local_sems = refs[2 * n:]
        x, y, c = lax.axis_index("x"), lax.axis_index("y"), lax.axis_index("c")
        mine = 2 * x + y
        chips = _other_chips(x, y)
        local = [pltpu.make_async_copy(src[a].at[mine], out[a].at[mine], local_sems.at[a]) for a in range(n)]
        for d in local:
            d.start()
        sends = []
        for j, (chip, q) in enumerate(chips):
            for a in range(n):
                d = pltpu.make_async_remote_copy(
                    src_ref=src[a].at[q], dst_ref=out[a].at[mine], send_sem=send_sems.at[a, j],
                    recv_sem=recv_sems.at[a, j], device_id=(*chip, c), device_id_type=pl.DeviceIdType.MESH)
                d.start()
                sends.append(d)
        for j, (chip, q) in enumerate(chips):
            for a in range(n):
                pltpu.make_async_remote_copy(
                    src_ref=src[a].at[q], dst_ref=out[a].at[q], send_sem=send_sems.at[a, j],
                    recv_sem=recv_sems.at[a, j], device_id=(*chip, c), device_id_type=pl.DeviceIdType.MESH).wait_recv()
        for d in sends:
            d.wait_send()
        for d in local:
            d.wait()

    return _comm_call(body, name, ps, [jax.ShapeDtypeStruct(p.shape, p.dtype) for p in ps], 3)


def _pair_sum(g, got, c_idx, out_dtype, name):
    _, r, c = g.shape
    lanes = -(-c // 128) * 128
    tr = _pick_rows(r, max(8, (2 * 1024 * 1024) // (lanes * 4)))
    g4 = g.reshape((N_CHIPS, 2) + g.shape[1:])

    def body(c_ref, g_ref, got_ref, o_ref):
        o_ref[...] = (g_ref[...] + got_ref[...]).astype(o_ref.dtype)

    return pl.pallas_call(
        body, name=name,
        grid_spec=pltpu.PrefetchScalarGridSpec(
            num_scalar_prefetch=1, grid=(N_CHIPS, r // tr),
            in_specs=[pl.BlockSpec((None, None, tr, c), lambda q, i, cr: (q, cr[0], i, 0)),
                      pl.BlockSpec((None, tr, c), lambda q, i, cr: (q, i, 0))],
            out_specs=pl.BlockSpec((None, tr, c), lambda q, i, cr: (q, i, 0))),
        out_shape=jax.ShapeDtypeStruct(got.shape, out_dtype),
        compiler_params=_cparams(("parallel", "parallel")))(c_idx, g4, got)


def _pick_rows(r, pref):
    t = (min(pref, r) // 8) * 8
    while t >= 8:
        if r % t == 0:
            return t
        t -= 8
    return r


def _w_in_from_shards(t, lo, hi):
    n, r, cs = t.shape
    tr = _pick_rows(r, 256)
    wm = n * cs - (hi - lo)

    def body(t_ref, m_ref, b_ref):
        full = jnp.concatenate([t_ref[j] for j in range(n)], axis=1)
        m_ref[...] = jnp.concatenate([full[:, :lo], full[:, hi:]], axis=1)
        b_ref[...] = jnp.concatenate([full[:, lo:hi], jnp.zeros((tr, 128 - (hi - lo)), full.dtype)], axis=1)

    return pl.pallas_call(
        body, name="w_in_layout", grid=(r // tr,), in_specs=[pl.BlockSpec((n, tr, cs), lambda i: (0, i, 0))],
        out_specs=[pl.BlockSpec((tr, wm), lambda i: (i, 0)), pl.BlockSpec((tr, 128), lambda i: (i, 0))],
        out_shape=[jax.ShapeDtypeStruct((r, wm), t.dtype), jax.ShapeDtypeStruct((r, 128), t.dtype)],
        compiler_params=_cparams(("parallel",)))(t)


def _w_in_to_shards(gm, gb, lo, hi, dtype):
    r, wm = gm.shape
    cs = (wm + hi - lo) // N_DEV
    tr = _pick_rows(r, 64)

    def body(m_ref, b_ref, o_ref):
        m = m_ref[...]
        full = jnp.concatenate([m[:, :lo], b_ref[:, :hi - lo], m[:, lo:]], axis=1)
        for j in range(N_DEV):
            o_ref[j] = full[:, j * cs:(j + 1) * cs].astype(o_ref.dtype)

    return pl.pallas_call(
        body, name="dw_in_layout", grid=(r // tr,),
        in_specs=[pl.BlockSpec((tr, wm), lambda i: (i, 0)), pl.BlockSpec((tr, 128), lambda i: (i, 0))],
        out_specs=pl.BlockSpec((N_DEV, tr, cs), lambda i: (0, i, 0)),
        out_shape=jax.ShapeDtypeStruct((N_DEV, r, cs), dtype), compiler_params=_cparams(("parallel",)))(gm, gb)


def _pack(arrs, dtype, lead=()):
    nlead = len(lead)
    flat = jnp.concatenate([a.astype(dtype).reshape(lead + (-1,)) for a in arrs], axis=nlead)
    n = flat.shape[-1]
    unit = PACK_WIDTH * PACK_ROWS
    pad = (-n) % unit
    flat = jnp.pad(flat, [(0, 0)] * nlead + [(0, pad)])
    return flat.reshape(lead + ((n + pad) // PACK_WIDTH, PACK_WIDTH))


def _unpack(buf, shapes, lead=()):
    flat = buf.reshape(lead + (-1,))
    out, off = [], 0
    for shp in shapes:
        n = math.prod(shp)
        out.append(flat[..., off:off + n].reshape(lead + tuple(shp)))
        off += n
    return out


def _adam_math(w, g, m, v):
    m = ADAM_B1 * m + (1.0 - ADAM_B1) * g
    v = ADAM_B2 * v + (1.0 - ADAM_B2) * (g * g)
    m_hat = m / (1.0 - ADAM_B1 ** ADAM_STEP)
    v_hat = v / (1.0 - ADAM_B2 ** ADAM_STEP)
    delta = -ADAM_LR * (m_hat / (jnp.sqrt(v_hat) + ADAM_EPS) + ADAM_WD * w)
    return delta, m, v


def _sum_adam(parts, w, m, v, name):
    r, c = w.shape
    nparts = parts.shape[0]
    lanes = -(-c // 128) * 128
    tr = _pick_rows(r, max(8, (6 * 1024 * 1024) // (nparts * lanes * 4)))

    def body(p_ref, w_ref, m_ref, v_ref, g_ref, d_ref, nm_ref, nv_ref):
        g = p_ref[0].astype(F32)
        for j in range(1, nparts):
            g = g + p_ref[j].astype(F32)
        d, nm, nv = _adam_math(w_ref[...], g, m_ref[...], v_ref[...])
        g_ref[...] = g
        d_ref[...] = d
        nm_ref[...] = nm
        nv_ref[...] = nv

    row = pl.BlockSpec((tr, c), lambda i: (i, 0))
    return pl.pallas_call(
        body, name=name, grid=(r // tr,), in_specs=[pl.BlockSpec((nparts, tr, c), lambda i: (0, i, 0)), row, row, row],
        out_specs=[row] * 4, out_shape=[jax.ShapeDtypeStruct((r, c), F32)] * 4,
        compiler_params=_cparams(("parallel",)))(parts, w, m, v)


def _block_diag(t):
    nb, g, a, b = t.shape
    eye = jnp.eye(g, dtype=t.dtype)
    return jnp.einsum('ngab,gh->ngahb', t, eye).reshape(nb, g * a, g * b)


def _diag_blocks(t, a, b):
    nb = t.shape[0]
    g = S5_GROUPS_PER_BLOCK
    t = t.reshape(nb, g, a, g, b)
    return jnp.stack([t[:, j, :, j, :] for j in range(g)], axis=1)


def _local_step(x, mem, target, p, late_weights=None, early_grads=None, last_grads=None):
    s, d = x.shape
    gw = d // 2
    nh = gw // GDN_HEAD_DIM
    ng = gw // S5_GROUP
    nb = ng // S5_GROUPS_PER_BLOCK
    nl = ng * S5_STATE
    ts = min(256, s)
    nt = s // ts
    grads = {}

    w_main, w_ba = p['w_main'], p['w_ba']
    CB_ZA, CB_XB, CB_ZB, CB_QC, CB_ZC, CB_G = 3, 4, 5, 6, 7, 8

    u = _tile_fwd(_rms, "rms_fwd", nt, [_rt(x, ts)], [p['norm_g']],
                  [((s, d), BF16, (ts, d), lambda i: (i, 0))])[0]
    proj = _mm(u, w_main, tm=1024, tn=2048, tk=512, name="proj_main")
    pba = _mm(u, w_ba, name="proj_ba")
    if late_weights is not None:
        p = {**p, **late_weights(proj)}

    conv_w = p['conv_w']
    col = lambda arr, cb: (arr, (s, GDN_HEAD_DIM), lambda i, cb=cb: (0, cb + i))
    qkv = []
    for j, mode in enumerate(('q', 'k', 'v')):
        off = j * nh
        qkv.append(_tile_fwd(
            _gdn_pre(mode), "gdn_pre_" + mode, nh, [col(proj, off), (conv_w, (CONV_WIDTH, GDN_HEAD_DIM), lambda i, off=off: (0, off + i))],
            [], [((s, gw), F32, (s, GDN_HEAD_DIM), lambda i: (0, i))])[0])
    q, k, v = qkv
    lane = jnp.arange(128)[:, None]
    colh = jnp.arange(gw)[None, :] // GDN_HEAD_DIM
    e_beta = (lane == colh).astype(F32)
    e_g = (lane == colh + nh).astype(F32)
    alog_row = jnp.pad(p['gdn_a_log'], ((0, 0), (nh, 128 - 2 * nh)))
    dtb_row = jnp.pad(p['gdn_dt_bias'], ((0, 0), (nh, 128 - 2 * nh)))
    row_gw = lambda: ((s, gw), F32, (ts, gw), lambda i: (i, 0))
    betab, gb = _tile_fwd(_gdn_gates, "gdn_gates", nt, [_rt(pba, ts)], [alog_row, dtb_row, e_beta, e_g],
                          [row_gw(), row_gw()])
    *intra, t_inv = _gdn_intra_fwd(q, k, v, gb, betab)
    o_raw, states = _gdn_inter_fwd(*intra, gb)
    ga = _tile_fwd(_gdn_post, "gdn_post", nt, [_rt(o_raw, ts), _rt(proj, ts, CB_ZA, gw)], [p['gdn_norm_g']],
                   [((s, gw), BF16, (ts, gw), lambda i: (i, 0))])[0]

    e_rep = (jnp.arange(S5_STATE)[:, None] == jnp.arange(S5_STATE * S5_GROUP)[None, :] // S5_GROUP).astype(F32)
    s5_in = [p['s5_lambda_re'], p['s5_lambda_im'], p['s5_log_dt'].reshape(ng, 1),
             p['s5_b_re'].reshape(ng, S5_STATE * S5_GROUP), p['s5_b_im'].reshape(ng, S5_STATE * S5_GROUP), e_rep]
    one = lambda shp: (shp, F32, shp, lambda i, n=len(shp): (0,) * n)
    ab_re, ab_im, bbr, bbi = _tile_fwd(_s5_params, "s5_params", 1, [], s5_in,
                                       [one((ng, S5_STATE)), one((ng, S5_STATE)), one((ng, S5_STATE * S5_GROUP)),
                                        one((ng, S5_STATE * S5_GROUP))])
    coef = _s5_coef(ab_re.reshape(1, nl), ab_im.reshape(1, nl))
    to_bd_b = lambda t: _block_diag(t.reshape(nb, S5_GROUPS_PER_BLOCK, S5_STATE, S5_GROUP).transpose(0, 1, 3, 2))
    to_bd_c = lambda t: _block_diag(t.reshape(nb, S5_GROUPS_PER_BLOCK, S5_GROUP, S5_STATE).transpose(0, 1, 3, 2))
    bbd_re, bbd_im = to_bd_b(bbr).astype(BF16), to_bd_b(bbi).astype(BF16)
    cbd_re, cbd_im = to_bd_c(p['s5_c_re']).astype(BF16), to_bd_c(p['s5_c_im']).astype(BF16)
    xb_arr = lax.slice_in_dim(proj, CB_XB * gw, (CB_XB + 1) * gw, axis=1)
    h_re, h_im, ylin = _s5_fwd(xb_arr, bbd_re, bbd_im, cbd_re, cbd_im, coef, ts)
    gl = _tile_fwd(_s5_post1, "s5_post1", nt, [_rt(ylin, ts), _rt(proj, ts, CB_XB, gw)], [p['s5_d']],
                   [((s, gw), BF16, (ts, gw), lambda i: (i, 0))])[0]
    tglu = _mm(gl, p['s5_w_glu'], b_shards=True, name="s5_glu")
    gbb = _tile_fwd(_s5_post2, "s5_post2", nt, [_rt(tglu, ts), _rt(proj, ts, CB_ZB, gw)], [],
                    [((s, gw), BF16, (ts, gw), lambda i: (i, 0))])[0]

    m_len = mem.shape[0]
    mem_n = _tile_fwd(_rms, "mem_rms", 1, [_rt(mem, m_len)], [p['mem_norm_g']],
                      [((m_len, d), BF16, (m_len, d), lambda i: (i, 0))])[0]
    kv = _mm(mem_n, p['w_kv_mem'], name="mem_kv")
    gcc = _tile_fwd(_attn, "attn", nt, [_rt(proj, ts, CB_QC, gw), _rt(proj, ts, CB_ZC, gw)], [kv],
                    [((s, gw), BF16, (ts, gw), lambda i: (i, 0))])[0]

    p_a = _mm(ga, p['w_br_a'], b_shards=True, name="br_a")
    p_b = _mm(gbb, p['w_br_b'], b_shards=True, name="br_b")
    p_c = _mm(gcc, p['w_br_c'], b_shards=True, name="br_c")
    gate_acts = [_rt(proj, ts, CB_G // 2 + j, d) for j in range(3)]
    merged = _tile_fwd(_merge, "merge", nt, gate_acts + [_rt(p_a, ts), _rt(p_b, ts), _rt(p_c, ts)], [],
                       [((s, d), BF16, (ts, d), lambda i: (i, 0))])[0]
    mo = _mm(merged, p['w_out'], name="out_proj")
    dh, dfg, loss = _final(x, mo, target, p['final_g'].reshape(1, d), ts)
    grads['final_g'] = dfg.reshape(d)

    dmerged = _mm(dh, p['w_out'], tb=True, name="d_merged")
    grads['w_out'] = _mm(merged, dh, ta=True, name="dw_out")
    row_d = lambda dt: ((s, d), dt, (ts, d), lambda i: (i, 0))
    dg0, dg1, dg2, dpa, dpb, dpc = _tile_bwd(
        _merge, "merge_bwd", nt, gate_acts + [_rt(p_a, ts), _rt(p_b, ts), _rt(p_c, ts)], [], [_rt(dmerged, ts)],
        [row_d(BF16)] * 6, [])
    dga = _mm(dpa, p['w_br_a'], tb=True, b_shards=True, name="d_ga")
    dgbb = _mm(dpb, p['w_br_b'], tb=True, b_shards=True, name="d_gb")
    dgcc = _mm(dpc, p['w_br_c'], tb=True, b_shards=True, name="d_gc")
    grads['w_br_a'] = _mm(ga, dpa, ta=True, out_shards=N_DEV, name="dw_br_a")
    grads['w_br_b'] = _mm(gbb, dpb, ta=True, out_shards=N_DEV, name="dw_br_b")
    grads['w_br_c'] = _mm(gcc, dpc, ta=True, out_shards=N_DEV, name="dw_br_c")
    row_h = lambda dt: ((s, gw), dt, (ts, gw), lambda i: (i, 0))

    dqc, dzc, dkv = _tile_bwd(_attn, "attn_bwd", nt, [_rt(proj, ts, CB_QC, gw), _rt(proj, ts, CB_ZC, gw)], [kv],
                              [_rt(dgcc, ts)], [row_h(BF16), row_h(BF16)], [True])
    grads['w_kv_mem'] = _mm(mem_n, dkv, ta=True, name="dw_kv")
    dmem_n = _mm(dkv, p['w_kv_mem'], tb=True, name="d_mem_n")
    grads['mem_norm_g'] = _tile_bwd(_rms, "mem_rms_bwd", 1, [_rt(mem, m_len)], [p['mem_norm_g']],
                                    [_rt(dmem_n, m_len)], [None], [True])[0]

    dtglu, dzb = _tile_bwd(_s5_post2, "s5_post2_bwd", nt, [_rt(tglu, ts), _rt(proj, ts, CB_ZB, gw)], [],
                           [_rt(dgbb, ts)], [((s, 2 * gw), BF16, (ts, 2 * gw), lambda i: (i, 0)), row_h(BF16)], [])
    grads['s5_w_glu'] = _mm(gl, dtglu, ta=True, out_shards=N_DEV, name="dw_glu")
    s5_d = p['s5_d']
    if early_grads is not None:
        s5_d = s5_d + early_grads(grads)[:1, :1]
    dgl = _mm(dtglu, p['s5_w_glu'], tb=True, b_shards=True, name="d_gl")
    dylin, dxb1, dd = _tile_bwd(_s5_post1, "s5_post1_bwd", nt, [_rt(ylin, ts), _rt(proj, ts, CB_XB, gw)],
                                [s5_d], [_rt(dgl, ts)], [row_h(F32), row_h(F32)], [True])
    grads['s5_d'] = dd
    dxb2, dbbd_re, dbbd_im, dcbd_re, dcbd_im, da_re, da_im = _s5_bwd(dylin, xb_arr, h_re, h_im, bbd_re, bbd_im,
                                                                    cbd_re, cbd_im, coef, ts)
    from_bd_b = lambda t: _diag_blocks(t, S5_GROUP, S5_STATE).transpose(0, 1, 3, 2).reshape(ng, S5_STATE * S5_GROUP)
    from_bd_c = lambda t: _diag_blocks(t, S5_STATE, S5_GROUP).transpose(0, 1, 3, 2).reshape(1, ng, S5_GROUP, S5_STATE)
    grads['s5_c_re'], grads['s5_c_im'] = from_bd_c(dcbd_re), from_bd_c(dcbd_im)
    s5_cts = [jnp.sum(da_re, axis=0).reshape(ng, S5_STATE), jnp.sum(da_im, axis=0).reshape(ng, S5_STATE),
              from_bd_b(dbbd_re), from_bd_b(dbbd_im)]
    dlr, dli, dlogdt, dbr, dbi = _tile_bwd(_s5_params, "s5_params_bwd", 1, [], s5_in,
                                           [(c, c.shape, lambda i: (0, 0)) for c in s5_cts], [],
                                           [True, True, True, True, True, False])
    grads['s5_lambda_re'], grads['s5_lambda_im'] = dlr[None], dli[None]
    grads['s5_log_dt'] = dlogdt.reshape(1, ng)
    grads['s5_b_re'] = dbr.reshape(1, ng, S5_STATE, S5_GROUP)
    grads['s5_b_im'] = dbi.reshape(1, ng, S5_STATE, S5_GROUP)
    dxb = (dxb1 + dxb2).astype(BF16)

    do_raw, dza, dgng = _tile_bwd(_gdn_post, "gdn_post_bwd", nt, [_rt(o_raw, ts), _rt(proj, ts, CB_ZA, gw)],
                                  [p['gdn_norm_g']], [_rt(dga, ts)], [row_h(F32), row_h(BF16)], [True])
    grads['gdn_norm_g'] = dgng
    *intra_cts, dgb_inter = _gdn_inter_bwd(*intra, gb, states, do_raw)
    dq, dk, dv, dgb, dbetab = _gdn_intra_bwd(q, k, v, gb, betab, t_inv, intra_cts, dgb_inter)
    dpba, dalog, ddtb = _tile_bwd(_gdn_gates, "gdn_gates_bwd", nt, [_rt(pba, ts)], [alog_row, dtb_row, e_beta, e_g],
                                  [_rt(dbetab, ts), _rt(dgb, ts)], [((s, 128), BF16, (ts, 128), lambda i: (i, 0))],
                                  [True, True, False, False])
    grads['gdn_a_log'] = dalog[:, nh:2 * nh]
    grads['gdn_dt_bias'] = ddtb[:, nh:2 * nh]
    dqkv, dconv = [], []
    for j, (mode, ct) in enumerate((('q', dq), ('k', dk), ('v', dv))):
        off = j * nh
        wspec = (conv_w, (CONV_WIDTH, GDN_HEAD_DIM), lambda i, off=off: (0, off + i))
        dxc, dwc = _tile_bwd(
            _gdn_pre(mode), "gdn_pre_bwd_" + mode, nh, [col(proj, off), wspec], [], [col(ct, 0)],
            [((s, gw), BF16, (s, GDN_HEAD_DIM), lambda i: (0, i)),
             ((CONV_WIDTH, gw), F32, (CONV_WIDTH, GDN_HEAD_DIM), lambda i: (0, i))], [])
        dqkv.append(dxc)
        dconv.append(dwc)
    grads['conv_w'] = jnp.concatenate(dconv, axis=1)

    dproj = jnp.concatenate(dqkv + [dza, dxb, dzb, dqc, dzc, dg0, dg1, dg2], axis=1)
    grads['w_main'] = _mm(u, dproj, ta=True, tm=1024, tn=2048, tk=512, name="dw_main")
    grads['w_ba'] = _mm(u, dpba, ta=True, name="dw_ba")
    if last_grads is not None:
        dpba = dpba + last_grads(grads)[:1, :1].astype(BF16)
    du = _mm(dpba, w_ba, tb=True, name="du_ba")
    du = _mm(dproj, w_main, tb=True, addend=du, tm=512, tn=2048, tk=1024, name="du_main")
    dx, dng = _tile_bwd(_rms, "rms_bwd", nt, [_rt(x, ts)], [p['norm_g']], [_rt(du, ts)], [row_d(F32)], [True])
    grads['norm_g'] = dng
    return loss, dx + dh, grads


def _to_shards(name, g):
    if SHARDED[name] == 'row':
        return g.reshape((N_DEV, g.shape[0] // N_DEV) + g.shape[1:])
    r, c = g.shape
    return g.reshape(r, N_DEV, c // N_DEV).transpose(1, 0, 2)


def _from_shards(name, t):
    if SHARDED[name] == 'row':
        return t.reshape((t.shape[0] * t.shape[1],) + t.shape[2:])
    n, r, c = t.shape
    return t.transpose(1, 0, 2).reshape(r, n * c)


def _step(x, mem, target, w, m, v):
    sharded = list(SHARDED)
    shard_shapes = {n: tuple(w[n].shape[1:]) for n in sharded}
    d = x.shape[-1]
    ba_lo = 2 * d
    ba_hi = ba_lo + 2 * (d // 2 // GDN_HEAD_DIM)

    w_in_all = _gather([w['w_in'][0].astype(BF16)], "gather_w_in")[0]
    late = [w[n][0].astype(BF16) for n in OVERLAPPED] + [w['conv_w'][0]]
    late, w_in_all = lax.optimization_barrier((late, w_in_all))
    every = [_send_whole] * len(late)
    lands = [jnp.broadcast_to(t[None], (N_DEV,) + t.shape) for t in late]
    gather_started, token = _direct_start(late, lands, every, "gather_rest_start")
    full = {}
    full['w_main'], full['w_ba'] = _w_in_from_shards(w_in_all, ba_lo, ba_hi)
    for n in REPLICATED:
        full[n] = w[n]
    for n in ('s5_lambda_re', 's5_lambda_im', 's5_c_re', 's5_c_im'):
        full[n] = w[n][0]
    full['norm_g'] = w['norm_g'] + token[:1, :1]

    def late_weights(proj):
        got = dict(zip(OVERLAPPED + ['conv_w'], _direct_wait(gather_started, every, proj, "gather_rest_wait")))
        for n in ('w_kv_mem', 'w_out', 'conv_w'):
            got[n] = _from_shards(n, got[n])
        return got

    slots = [_send_slot] * len(OVERLAPPED)
    scatter_started = []

    def early_grads(grads):
        gs = [_to_shards(n, grads[n]) if SHARDED[n] == 'row' else grads[n] for n in OVERLAPPED]
        started, tok = _direct_start(gs, gs, slots, "scatter_early_start")
        scatter_started.append(started)
        return tok

    def last_grads(grads):
        gs = [_w_in_to_shards(grads['w_main'], grads['w_ba'], ba_lo, ba_hi, BF16), _to_shards('conv_w', grads['conv_w'])]
        started, tok = _direct_start(gs, gs, slots[:2], "scatter_last_start")
        scatter_started.append(started)
        return tok

    loss, grad_x, grads = _local_step(x[0], mem[0], target[0], full, late_weights, early_grads, last_grads)
    res = {}

    def update(names, parts):
        for n, part in zip(names, parts):
            outs = _sum_adam(part, w[n][0], m[n][0], v[n][0], name="adam_" + n)
            for kind, t in zip(('grad', 'delta', 'new_m', 'new_v'), outs):
                res[kind, n] = t[None]

    update(OVERLAPPED, _direct_wait(scatter_started[0], slots, grad_x, "scatter_early_wait"))

    small = _pack([grads[n].reshape(w[n].shape) for n in REPLICATED] + [loss[:1, :1]], F32)
    allp = _gather([small], "gather_small")[0]
    zero = jnp.zeros((1, 1), F32)
    outs = _sum_adam(allp, *[_pack([t[n] for n in REPLICATED] + [zero], F32) for t in (w, m, v)], name="adam_small")
    shapes = [w[n].shape for n in REPLICATED] + [(1, 1)]
    for kind, buf in zip(('grad', 'delta', 'new_m', 'new_v'), outs):
        got = _unpack(buf, shapes)
        for n, t in zip(REPLICATED, got):
            res[kind, n] = t
        if kind == 'grad':
            total_loss = got[-1].reshape(())
    update(['w_in', 'conv_w'], _direct_wait(scatter_started[1], slots[:2], outs[0], "scatter_last_wait"))
    out = [total_loss, grad_x[None]]
    for kind in ('grad', 'delta', 'new_m', 'new_v'):
        out += [res[kind, n] for n in WEIGHTS]
    return tuple(out)


def kernel(x, mem, norm_g, w_in, conv_w, gdn_a_log, gdn_dt_bias, gdn_norm_g, s5_lambda_re, s5_lambda_im, s5_log_dt, s5_b_re, s5_b_im, s5_c_re, s5_c_im, s5_d, s5_w_glu, mem_norm_g, w_kv_mem, w_br_a, w_br_b, w_br_c, w_out, final_g, loss_target, m_norm_g, m_w_in, m_conv_w, m_gdn_a_log, m_gdn_dt_bias, m_gdn_norm_g, m_s5_lambda_re, m_s5_lambda_im, m_s5_log_dt, m_s5_b_re, m_s5_b_im, m_s5_c_re, m_s5_c_im, m_s5_d, m_s5_w_glu, m_mem_norm_g, m_w_kv_mem, m_w_br_a, m_w_br_b, m_w_br_c, m_w_out, m_final_g, v_norm_g, v_w_in, v_conv_w, v_gdn_a_log, v_gdn_dt_bias, v_gdn_norm_g, v_s5_lambda_re, v_s5_lambda_im, v_s5_log_dt, v_s5_b_re, v_s5_b_im, v_s5_c_re, v_s5_c_im, v_s5_d, v_s5_w_glu, v_mem_norm_g, v_w_kv_mem, v_w_br_a, v_w_br_b, v_w_br_c, v_w_out, v_final_g):
    a = dict(locals())
    w = {n: a[n] for n in WEIGHTS}
    m = {n: a['m_' + n] for n in WEIGHTS}
    v = {n: a['v_' + n] for n in WEIGHTS}
    return _step(x, mem, loss_target, w, m, v)
```

```python
import functools
import math

import jax
import jax.numpy as jnp
from jax import lax
from jax.experimental import pallas as pl
from jax.experimental.pallas import tpu as pltpu

F32 = jnp.float32
BF16 = jnp.bfloat16
HI = lax.Precision.HIGHEST

EPS = 1e-6
CHUNK = 64
GDN_HEAD_DIM = 128
CONV_WIDTH = 4
S5_GROUP = 16
S5_STATE = 64
S5_GROUPS_PER_BLOCK = 8
XA_HEADS = 4
N_DEV = 8
ADAM_LR, ADAM_B1, ADAM_B2, ADAM_EPS, ADAM_WD, ADAM_STEP = 0.001, 0.9, 0.999, 1e-08, 0.01, 10

VMEM_LIMIT_BYTES = 56 * 1024 * 1024
SCAN_LANES = 512
PACK_WIDTH = 512
PACK_ROWS = 256

WEIGHTS = ['norm_g', 'w_in', 'conv_w', 'gdn_a_log', 'gdn_dt_bias', 'gdn_norm_g', 's5_lambda_re', 's5_lambda_im',
           's5_log_dt', 's5_b_re', 's5_b_im', 's5_c_re', 's5_c_im', 's5_d', 's5_w_glu', 'mem_norm_g', 'w_kv_mem',
           'w_br_a', 'w_br_b', 'w_br_c', 'w_out', 'final_g']
SHARDED = {'w_in': 'col', 'conv_w': 'col', 's5_w_glu': 'col', 'w_kv_mem': 'row', 'w_br_a': 'col', 'w_br_b': 'col',
           'w_br_c': 'col', 'w_out': 'row'}
GATHER_BF16 = ['w_in', 's5_w_glu', 'w_kv_mem', 'w_br_a', 'w_br_b', 'w_br_c', 'w_out']
REPLICATED = [n for n in WEIGHTS if n not in SHARDED]
OVERLAPPED = ['s5_w_glu', 'w_kv_mem', 'w_br_a', 'w_br_b', 'w_br_c', 'w_out']


def _cparams(sem=None):
    return pltpu.CompilerParams(dimension_semantics=sem, vmem_limit_bytes=VMEM_LIMIT_BYTES)


def _pick(dim, pref):
    t = (min(pref, dim) // 128) * 128
    while t >= 128:
        if dim % t == 0:
            return t
        t -= 128
    return dim


def _make_dots(prep, precision):
    def raw(a, b, dims):
        return lax.dot_general(prep(a), prep(b), (dims, ((), ())), preferred_element_type=F32, precision=precision)

    @jax.custom_vjp
    def nn(a, b):
        return raw(a, b, ((1,), (0,)))

    @jax.custom_vjp
    def nt(a, b):
        return raw(a, b, ((1,), (1,)))

    @jax.custom_vjp
    def tn(a, b):
        return raw(a, b, ((0,), (0,)))

    nn.defvjp(lambda a, b: (nn(a, b), (a, b)), lambda r, ct: (nt(ct, r[1]), tn(r[0], ct)))
    nt.defvjp(lambda a, b: (nt(a, b), (a, b)), lambda r, ct: (nn(ct, r[1]), tn(ct, r[0])))
    tn.defvjp(lambda a, b: (tn(a, b), (a, b)), lambda r, ct: (nt(r[1], ct), nn(r[0], ct)))
    return nn, nt, tn


_bnn, _bnt, _btn = _make_dots(lambda a: a.astype(BF16), None)
_hnn, _hnt, _htn = _make_dots(lambda a: a.astype(F32), HI)
_mnn, _mnt, _mtn = _make_dots(lambda a: a.astype(F32), lax.Precision.HIGH)


def _mm(a, b, *, name, ta=False, tb=False, out_dtype=F32, addend=None, tm=512, tn=1024, tk=1024, b_shards=False,
        out_shards=0):
    m, k = (a.shape[1], a.shape[0]) if ta else a.shape
    brows, bcols = (b.shape[1], b.shape[0] * b.shape[2]) if b_shards else b.shape
    n = brows if tb else bcols
    assert (bcols if tb else brows) == k, (a.shape, b.shape, ta, tb)
    tm, tn, tk = _pick(m, tm), _pick(n, tn), _pick(k, tk)
    bcs = ocs = 0
    if b_shards:
        bcs = b.shape[2]
        assert bcs % 128 == 0 and (tk if tb else tn) % bcs == 0
    if out_shards:
        ocs = n // out_shards
        assert ocs % 128 == 0 and tn % ocs == 0
    nk = k // tk
    dims = ((0 if ta else 1,), (1 if tb else 0,))

    def body(*refs):
        if addend is None:
            a_ref, b_ref, o_ref, acc_ref = refs
        else:
            a_ref, b_ref, add_ref, o_ref, acc_ref = refs
        kk = pl.program_id(2)

        @pl.when(kk == 0)
        def _():
            acc_ref[...] = jnp.zeros_like(acc_ref)

        dot = lambda x, y: lax.dot_general(x.astype(BF16), y.astype(BF16), (dims, ((), ())), preferred_element_type=F32)
        if not b_shards:
            acc_ref[...] += dot(a_ref[...], b_ref[...])
        elif tb:
            for g in range(tk // bcs):
                acc_ref[...] += dot(a_ref[:, g * bcs:(g + 1) * bcs], b_ref[g])
        else:
            for g in range(tn // bcs):
                acc_ref[:, g * bcs:(g + 1) * bcs] += dot(a_ref[...], b_ref[g])

        @pl.when(kk == nk - 1)
        def _():
            r = acc_ref[...]
            if addend is not None:
                r = r + add_ref[...].astype(F32)
            if out_shards:
                for g in range(tn // ocs):
                    o_ref[g] = r[:, g * ocs:(g + 1) * ocs].astype(o_ref.dtype)
            else:
                o_ref[...] = r.astype(o_ref.dtype)

    a_spec = pl.BlockSpec((tk, tm), lambda i, j, kk: (kk, i)) if ta else pl.BlockSpec((tm, tk), lambda i, j, kk: (i, kk))
    if b_shards:
        b_spec = (pl.BlockSpec((tk // bcs, tn, bcs), lambda i, j, kk: (kk, j, 0)) if tb
                  else pl.BlockSpec((tn // bcs, tk, bcs), lambda i, j, kk: (j, kk, 0)))
    else:
        b_spec = (pl.BlockSpec((tn, tk), lambda i, j, kk: (j, kk)) if tb
                  else pl.BlockSpec((tk, tn), lambda i, j, kk: (kk, j)))
    if out_shards:
        o_spec = pl.BlockSpec((tn // ocs, tm, ocs), lambda i, j, kk: (j, i, 0))
        out_shape = jax.ShapeDtypeStruct((out_shards, m, ocs), out_dtype)
    else:
        o_spec = pl.BlockSpec((tm, tn), lambda i, j, kk: (i, j))
        out_shape = jax.ShapeDtypeStruct((m, n), out_dtype)
    in_specs = [a_spec, b_spec] + ([o_spec] if addend is not None else [])
    args = (a, b) + ((addend,) if addend is not None else ())
    return pl.pallas_call(
        body, name=name, grid=(m // tm, n // tn, nk), in_specs=in_specs, out_specs=o_spec,
        out_shape=out_shape, scratch_shapes=[pltpu.VMEM((tm, tn), F32)],
        compiler_params=_cparams(("parallel", "parallel", "arbitrary")))(*args)


def _rt(arr, ts, cb=0, w=None):
    w = arr.shape[1] if w is None else w
    return (arr, (ts, w), lambda i, cb=cb: (i, cb))


def _whole(p):
    return pl.BlockSpec(p.shape, lambda i, nd=p.ndim: (0,) * nd)


def _tile_fwd(f, name, n, acts, params, outs):
    na, npar = len(acts), len(params)

    def body(*refs):
        res = f(*[r[...] for r in refs[:na + npar]])
        for r, v in zip(refs[na + npar:], res):
            r[...] = v.astype(r.dtype)

    in_specs = [pl.BlockSpec(b, m) for _, b, m in acts] + [_whole(p) for p in params]
    out = pl.pallas_call(
        body, name=name, grid=(n,), in_specs=in_specs,
        out_specs=[pl.BlockSpec(b, m) for _, _, b, m in outs],
        out_shape=[jax.ShapeDtypeStruct(s, d) for s, d, _, _ in outs],
        compiler_params=_cparams(("parallel",)))(*[a for a, _, _ in acts], *params)
    return out


def _tile_bwd(f, name, n, acts, params, cts, agrads, pgrads):
    na, npar, nc = len(acts), len(params), len(cts)
    adds = [g[4] for g in agrads if g is not None and len(g) == 5]

    def body(*refs):
        i = pl.program_id(0)
        ins = [r[...] for r in refs[:na + npar]]
        outs, vjp = jax.vjp(f, *ins)
        g = vjp(tuple(c[...].astype(o.dtype) for c, o in zip(refs[na + npar:na + npar + nc], outs)))
        add_refs = refs[na + npar + nc:na + npar + nc + len(adds)]
        orefs = refs[na + npar + nc + len(adds):]
        k = 0
        for j in range(na):
            if agrads[j] is not None:
                val = g[j]
                if len(agrads[j]) == 5:
                    val = val + add_refs[[id(t) for t in adds].index(id(agrads[j][4]))][...]
                orefs[k][...] = val.astype(orefs[k].dtype)
                k += 1
        for j in range(npar):
            if pgrads[j]:
                o = orefs[k]

                @pl.when(i == 0)
                def _(o=o):
                    o[...] = jnp.zeros_like(o)

                o[...] += g[na + j].astype(F32)
                k += 1

    in_specs = ([pl.BlockSpec(b, m) for _, b, m in acts] + [_whole(p) for p in params]
                + [pl.BlockSpec(b, m) for _, b, m in cts]
                + [pl.BlockSpec(g[2], g[3]) for g in agrads if g is not None and len(g) == 5])
    out_specs = [pl.BlockSpec(g[2], g[3]) for g in agrads if g is not None]
    out_shape = [jax.ShapeDtypeStruct(g[0], g[1]) for g in agrads if g is not None]
    for p, flag in zip(params, pgrads):
        if flag:
            out_specs.append(_whole(p))
            out_shape.append(jax.ShapeDtypeStruct(p.shape, F32))
    return pl.pallas_call(
        body, name=name, grid=(n,), in_specs=in_specs, out_specs=out_specs, out_shape=out_shape,
        compiler_params=_cparams(("arbitrary",)))(*[a for a, _, _ in acts], *params, *[c for c, _, _ in cts], *adds)


def _silu(x):
    return x * jax.nn.sigmoid(x)


def _rms(x, g):
    x = x.astype(F32)
    return (x * lax.rsqrt(jnp.mean(x * x, axis=-1, keepdims=True) + EPS) * g,)


def _shift_down(x, s):
    row = lax.broadcasted_iota(jnp.int32, x.shape, 0)
    return jnp.where(row >= s, pltpu.roll(x, s, 0), 0.0)


def _shift_up(x, s):
    n = x.shape[0]
    row = lax.broadcasted_iota(jnp.int32, x.shape, 0)
    return jnp.where(row < n - s, pltpu.roll(x, n - s, 0), 0.0)


@functools.partial(jax.custom_vjp, nondiff_argnums=(1,))
def _shift(x, s):
    return _shift_down(x, s)


_shift.defvjp(lambda x, s: (_shift_down(x, s), None), lambda s, _, ct: (_shift_up(ct, s),))


def _gdn_pre(mode):
    def f(x, w):
        y = x * w[CONV_WIDTH - 1:CONV_WIDTH, :]
        for j in range(CONV_WIDTH - 1):
            y = y + _shift(x, CONV_WIDTH - 1 - j) * w[j:j + 1, :]
        y = _silu(y)
        if mode != 'v':
            y = y * lax.rsqrt(jnp.sum(y * y, axis=-1, keepdims=True) + EPS)
        if mode == 'q':
            y = y * (GDN_HEAD_DIM ** -0.5)
        return (y,)
    return f


def _softplus(x):
    return jnp.maximum(x, 0.0) + jnp.log1p(jnp.exp(-jnp.abs(x)))


def _gdn_gates(ba, alog, dtb, e_beta, e_g):
    beta = jax.nn.sigmoid(ba)
    g = -jnp.exp(alog) * _softplus(ba + dtb)
    return _hnn(beta, lax.stop_gradient(e_beta)), _hnn(g, lax.stop_gradient(e_g))


@jax.custom_vjp
def _inverse_known(neg, t):
    return t


_inverse_known.defvjp(lambda neg, t: (t, t), lambda t, ct: (_mtn(t, _mnt(ct, t)), jnp.zeros_like(t)))


def _gdn_intra(q, k, v, gb, bb, t_known=None):
    n, c = len(q), q[0].shape[0]
    ri = lax.broadcasted_iota(jnp.int32, (c, c), 0)
    ci = lax.broadcasted_iota(jnp.int32, (c, c), 1)
    incl, strict = ri >= ci, ri > ci
    tri = incl.astype(F32)
    eye = (ri == ci).astype(F32)
    each = range(n)
    gc = [_hnn(tri, gb[i]) for i in each]
    decay = [jnp.exp(jnp.where(incl, gc[i][:, :c] - gc[i].T[:c, :], -1e30)) for i in each]
    kb = [k[i] * bb[i] for i in each]
    kk = [_bnt(kb[i], k[i]) for i in each]
    qk = [_bnt(q[i], k[i]) for i in each]
    p = [jnp.where(strict, -(kk[i] * decay[i]), 0.0) for i in each]
    if t_known is None:
        t = [eye + p[i] for i in each]
        for _ in range(int(math.log2(c)) - 1):
            p = [_mnn(p[i], p[i]) for i in each]
            tp = [_mnn(t[i], p[i]) for i in each]
            t = [t[i] + tp[i] for i in each]
    else:
        t = [_inverse_known(p[i], t_known[i]) for i in each]
    egc = [jnp.exp(gc[i]) for i in each]
    u_val = [_mnn(t[i], v[i] * bb[i]) for i in each]
    w_dec = [_mnn(t[i], kb[i] * egc[i]) for i in each]
    qk = [qk[i] * decay[i] for i in each]
    gl = [jnp.sum(gb[i], axis=0, keepdims=True) for i in each]
    return w_dec, u_val, qk, [q[i] * egc[i] for i in each], [k[i] * jnp.exp(gl[i] - gc[i]) for i in each], t


def _gdn_inter(w_dec, u_val, qk, q_dec, k_dec, gb, state):
    each = range(len(state))
    ws = [_bnn(w_dec[i], state[i]) for i in each]
    qs = [_bnn(q_dec[i], state[i]) for i in each]
    v_new = [u_val[i] - ws[i] for i in each]
    qv = [_bnn(qk[i], v_new[i]) for i in each]
    kv = [_btn(k_dec[i], v_new[i]) for i in each]
    decayed = [state[i] * jnp.exp(jnp.sum(gb[i], axis=0, keepdims=True)) for i in each]
    return [qs[i] + qv[i] for i in each], [decayed[i] + kv[i] for i in each]


def _gdn_post(o, z, g):
    parts = []
    for h in range(o.shape[1] // GDN_HEAD_DIM):
        oh = o[:, h * GDN_HEAD_DIM:(h + 1) * GDN_HEAD_DIM]
        parts.append(oh * lax.rsqrt(jnp.mean(oh * oh, axis=-1, keepdims=True) + EPS) * g)
    y = parts[0] if len(parts) == 1 else jnp.concatenate(parts, axis=1)
    return (y * _silu(z),)


def _gelu(x):
    return 0.5 * x * (1.0 + jnp.tanh(0.7978845608028654 * (x + 0.044715 * x * x * x)))


def _s5_post1(ylin, xb, d):
    return (_gelu(ylin + d * xb),)


def _s5_post2(t, z):
    w = t.shape[1] // 2
    return (t[:, :w] * jax.nn.sigmoid(t[:, w:]) * _silu(z),)


def _attn(q, z, kv):
    w = q.shape[1]
    hd = w // XA_HEADS
    parts = []
    for h in range(XA_HEADS):
        s = _bnt(q[:, h * hd:(h + 1) * hd], kv[:, h * hd:(h + 1) * hd]) * (hd ** -0.5)
        s = s - jnp.max(s, axis=-1, keepdims=True)
        e = jnp.exp(s)
        p = e / jnp.sum(e, axis=-1, keepdims=True)
        parts.append(_bnn(p, kv[:, w + h * hd:w + (h + 1) * hd]))
    return (jnp.concatenate(parts, axis=1) * _silu(z),)


def _merge(g0, g1, g2, pa, pb, pc):
    return (jax.nn.sigmoid(g0) * pa + jax.nn.sigmoid(g1) * pb + jax.nn.sigmoid(g2) * pc,)


def _s5_params(lr, li, logdt, br, bi, e):
    dt = jnp.exp(logdt)
    mag = jnp.exp(lr * dt)
    ab_re, ab_im = mag * jnp.cos(li * dt), mag * jnp.sin(li * dt)
    den = lr * lr + li * li
    nr, ni = ab_re - 1.0, ab_im
    e = lax.stop_gradient(e)
    cre = _hnn((nr * lr + ni * li) / den, e)
    cim = _hnn((ni * lr - nr * li) / den, e)
    return ab_re, ab_im, cre * br - cim * bi, cre * bi + cim * br


def _gdn_blocks(s, w, per_step):
    nh, nc = w // GDN_HEAD_DIM, s // CHUNK
    cpb = math.gcd(per_step, nc)
    return nh, nc, cpb, nc // cpb, (cpb * CHUNK, w), (cpb * CHUNK, nh * CHUNK)


def _gdn_pairs(cpb, nh):
    wide, narrow = [], []
    for cb in range(cpb):
        rows = slice(cb * CHUNK, (cb + 1) * CHUNK)
        for h in range(nh):
            wide.append((rows, slice(h * GDN_HEAD_DIM, (h + 1) * GDN_HEAD_DIM)))
            narrow.append((rows, slice(h * CHUNK, (h + 1) * CHUNK)))
    return wide, narrow


def _gdn_intra_fwd(q, k, v, gb, bb, per_step=4):
    s, w = q.shape
    nh, nc, cpb, n, wide, narrow = _gdn_blocks(s, w, per_step)

    def body(q_ref, k_ref, v_ref, g_ref, b_ref, wd_ref, uv_ref, qk_ref, qd_ref, kd_ref, t_ref):
        wide, narrow = _gdn_pairs(cpb, nh)
        res = _gdn_intra(*[[r[ix] for ix in wide] for r in (q_ref, k_ref, v_ref, g_ref, b_ref)])
        for ref, vals, where in zip((wd_ref, uv_ref, qk_ref, qd_ref, kd_ref, t_ref), res,
                                    (wide, wide, narrow, wide, wide, narrow)):
            for ix, val in zip(where, vals):
                ref[ix] = val

    bw = pl.BlockSpec(wide, lambda i: (i, 0))
    bn = pl.BlockSpec(narrow, lambda i: (i, 0))
    fw = jax.ShapeDtypeStruct((s, w), F32)
    fn = jax.ShapeDtypeStruct((s, nh * CHUNK), F32)
    return pl.pallas_call(
        body, name="gdn_intra", grid=(n,), in_specs=[bw] * 5, out_specs=[bw, bw, bn, bw, bw, bn],
        out_shape=[fw, fw, fn, fw, fw, fn], compiler_params=_cparams(("parallel",)))(q, k, v, gb, bb)


def _gdn_intra_bwd(q, k, v, gb, bb, t, cts, dgb_inter, per_step=4):
    s, w = q.shape
    nh, nc, cpb, n, wide, narrow = _gdn_blocks(s, w, per_step)

    def body(q_ref, k_ref, v_ref, g_ref, b_ref, t_ref, cwd, cuv, cqk, cqd, ckd, dgi, dq_ref, dk_ref, dv_ref, dg_ref,
             db_ref):
        wide, narrow = _gdn_pairs(cpb, nh)
        t_known = [t_ref[ix] for ix in narrow]
        _, vjp = jax.vjp(lambda *a: _gdn_intra(*a, t_known=t_known)[:5],
                         *[[r[ix] for ix in wide] for r in (q_ref, k_ref, v_ref, g_ref, b_ref)])
        cts = tuple([r[ix] for ix in where] for r, where in zip((cwd, cuv, cqk, cqd, ckd),
                                                               (wide, wide, narrow, wide, wide)))
        dq, dk, dv, dg, db = vjp(cts)
        for j, ix in enumerate(wide):
            dq_ref[ix], dk_ref[ix], dv_ref[ix], db_ref[ix] = dq[j], dk[j], dv[j], db[j]
            dg_ref[ix] = dg[j] + dgi[ix]

    bw = pl.BlockSpec(wide, lambda i: (i, 0))
    bn = pl.BlockSpec(narrow, lambda i: (i, 0))
    return pl.pallas_call(
        body, name="gdn_intra_bwd", grid=(n,), in_specs=[bw] * 5 + [bn, bw, bw, bn, bw, bw, bw], out_specs=[bw] * 5,
        out_shape=[jax.ShapeDtypeStruct((s, w), F32)] * 5,
        compiler_params=_cparams(("parallel",)))(q, k, v, gb, bb, t, *cts, dgb_inter)


def _gdn_inter_fwd(wd, uv, qk, qd, kd, gb, per_step=4):
    s, w = wd.shape
    nh, nc, cpb, n, wide, narrow = _gdn_blocks(s, w, per_step)
    hd = GDN_HEAD_DIM

    def body(wd_ref, uv_ref, qk_ref, qd_ref, kd_ref, g_ref, o_ref, st_ref, state):
        @pl.when(pl.program_id(0) == 0)
        def _():
            state[...] = jnp.zeros_like(state)

        wide, narrow = _gdn_pairs(cpb, nh)
        st = [state[h] for h in range(nh)]
        for cb in range(cpb):
            wi, na = wide[cb * nh:(cb + 1) * nh], narrow[cb * nh:(cb + 1) * nh]
            for h in range(nh):
                st_ref[cb, h] = st[h]
            o, st = _gdn_inter([wd_ref[ix] for ix in wi], [uv_ref[ix] for ix in wi], [qk_ref[ix] for ix in na],
                               [qd_ref[ix] for ix in wi], [kd_ref[ix] for ix in wi], [g_ref[ix] for ix in wi], st)
            for h in range(nh):
                o_ref[wi[h]] = o[h]
        for h in range(nh):
            state[h] = st[h]

    bw = pl.BlockSpec(wide, lambda i: (i, 0))
    bn = pl.BlockSpec(narrow, lambda i: (i, 0))
    return pl.pallas_call(
        body, name="gdn_inter", grid=(n,), in_specs=[bw, bw, bn, bw, bw, bw],
        out_specs=[bw, pl.BlockSpec((cpb, nh, hd, hd), lambda i: (i, 0, 0, 0))],
        out_shape=[jax.ShapeDtypeStruct((s, w), F32), jax.ShapeDtypeStruct((nc, nh, hd, hd), F32)],
        scratch_shapes=[pltpu.VMEM((nh, hd, hd), F32)],
        compiler_params=_cparams(("arbitrary",)))(wd, uv, qk, qd, kd, gb)


def _gdn_inter_bwd(wd, uv, qk, qd, kd, gb, states, do, per_step=4):
    s, w = wd.shape
    nh, nc, cpb, n, wide, narrow = _gdn_blocks(s, w, per_step)
    hd = GDN_HEAD_DIM

    def body(wd_ref, uv_ref, qk_ref, qd_ref, kd_ref, g_ref, st_ref, do_ref, cwd, cuv, cqk, cqd, ckd, dg_ref, dstate):
        @pl.when(pl.program_id(0) == 0)
        def _():
            dstate[...] = jnp.zeros_like(dstate)

        wide, narrow = _gdn_pairs(cpb, nh)
        dst = [dstate[h] for h in range(nh)]
        for cb in reversed(range(cpb)):
            wi, na = wide[cb * nh:(cb + 1) * nh], narrow[cb * nh:(cb + 1) * nh]
            _, vjp = jax.vjp(_gdn_inter, [wd_ref[ix] for ix in wi], [uv_ref[ix] for ix in wi],
                             [qk_ref[ix] for ix in na], [qd_ref[ix] for ix in wi], [kd_ref[ix] for ix in wi],
                             [g_ref[ix] for ix in wi], [st_ref[cb, h] for h in range(nh)])
            dwd, duv, dqk, dqd, dkd, dg, dst = vjp(([do_ref[ix] for ix in wi], dst))
            for h in range(nh):
                cwd[wi[h]], cuv[wi[h]], cqk[na[h]], cqd[wi[h]], ckd[wi[h]], dg_ref[wi[h]] = (
                    dwd[h], duv[h], dqk[h], dqd[h], dkd[h], dg[h])
        for h in range(nh):
            dstate[h] = dst[h]

    bw = pl.BlockSpec(wide, lambda i: (n - 1 - i, 0))
    bn = pl.BlockSpec(narrow, lambda i: (n - 1 - i, 0))
    fw = jax.ShapeDtypeStruct((s, w), F32)
    return pl.pallas_call(
        body, name="gdn_inter_bwd", grid=(n,),
        in_specs=[bw, bw, bn, bw, bw, bw, pl.BlockSpec((cpb, nh, hd, hd), lambda i: (n - 1 - i, 0, 0, 0)), bw],
        out_specs=[bw, bw, bn, bw, bw, bw],
        out_shape=[fw, fw, jax.ShapeDtypeStruct((s, nh * CHUNK), F32), fw, fw, fw],
        scratch_shapes=[pltpu.VMEM((nh, hd, hd), F32)],
        compiler_params=_cparams(("arbitrary",)))(wd, uv, qk, qd, kd, gb, states, do)


def _s5_coef(ar, ai):
    nl = ar.shape[1]

    def body(ar_ref, ai_ref, o_ref):
        row = lax.broadcasted_iota(jnp.int32, (8, nl), 0)
        for base, sign in ((0, 1.0), (8, -1.0)):
            pr = [jnp.broadcast_to(ar_ref[...], (8, nl))]
            pi = [jnp.broadcast_to(ai_ref[...], (8, nl)) * sign]
            for _ in range(7):
                pr.append(pr[-1] * pr[0] - pi[-1] * pi[0])
                pi.append(pr[-2] * pi[0] + pi[-1] * pr[0])
            for j, d in enumerate((1, 2, 4)):
                m = (row >= d) if base == 0 else (row <= 7 - d)
                o_ref[base + 2 * j] = jnp.where(m, pr[d - 1], 0.0)
                o_ref[base + 2 * j + 1] = jnp.where(m, pi[d - 1], 0.0)
            cr, ci = jnp.zeros((8, nl), F32), jnp.zeros((8, nl), F32)
            for t in range(8):
                e = t if base == 0 else 7 - t
                cr = jnp.where(row == t, pr[e], cr)
                ci = jnp.where(row == t, pi[e], ci)
            o_ref[base + 6] = cr
            o_ref[base + 7] = ci

    return pl.pallas_call(body, name="s5_coef", out_shape=jax.ShapeDtypeStruct((16, 8, nl), F32),
                          compiler_params=_cparams())(ar, ai)


def _scan_tile(src_re, src_im, dst_re, dst_im, coef_ref, carry_re, carry_im, ts, reverse, extra=None):
    nl = src_re.shape[1]
    base = 8 if reverse else 0
    ng = ts // 8
    for lc in range(nl // SCAN_LANES):
        ln = slice(lc * SCAN_LANES, (lc + 1) * SCAN_LANES)
        m = [coef_ref[base + j, :, ln] for j in range(8)]
        row = lax.broadcasted_iota(jnp.int32, (8, SCAN_LANES), 0)

        def step(r, carry, ln=ln, m=m, row=row):
            grp = (ng - 1 - r) if reverse else r
            rows = pl.ds(pl.multiple_of(grp * 8, 8), 8)
            xr, xi = src_re[rows, ln], src_im[rows, ln]
            for j, d in enumerate((1, 2, 4)):
                sh = 8 - d if reverse else d
                sr, si = pltpu.roll(xr, sh, 0), pltpu.roll(xi, sh, 0)
                mr, mi = m[2 * j], m[2 * j + 1]
                xr, xi = xr + mr * sr - mi * si, xi + mr * si + mi * sr
            cr, ci = carry[0], carry[1]
            hr = xr + m[6] * cr - m[7] * ci
            hi = xi + m[6] * ci + m[7] * cr
            dst_re[rows, ln] = hr
            dst_im[rows, ln] = hi
            edge = 0 if reverse else 7
            out = (jnp.broadcast_to(hr[edge:edge + 1, :], hr.shape), jnp.broadcast_to(hi[edge:edge + 1, :], hi.shape))
            if extra is not None:
                h_re, h_im, halo_re, halo_im, first, _, _ = extra
                prev = pl.ds(pl.multiple_of(jnp.maximum(grp - 1, 0) * 8, 8), 8)
                use_halo = grp == 0
                pr = jnp.where(use_halo, halo_re[:, ln] * first, h_re[prev, ln])
                pi = jnp.where(use_halo, halo_im[:, ln] * first, h_im[prev, ln])
                qr = jnp.where(row == 0, jnp.broadcast_to(pr[7:8, :], pr.shape), pltpu.roll(h_re[rows, ln], 1, 0))
                qi = jnp.where(row == 0, jnp.broadcast_to(pi[7:8, :], pi.shape), pltpu.roll(h_im[rows, ln], 1, 0))
                out = out + (carry[2] + hr * qr + hi * qi, carry[3] + hi * qr - hr * qi)
            return out

        init = (carry_re[:, ln], carry_im[:, ln])
        if extra is not None:
            init = init + (extra[5][:, ln], extra[6][:, ln])
        fin = lax.fori_loop(0, ng, step, init)
        carry_re[:, ln] = fin[0]
        carry_im[:, ln] = fin[1]
        if extra is not None:
            extra[5][:, ln] = fin[2]
            extra[6][:, ln] = fin[3]


def _s5_fwd(xb, bb_re, bb_im, c_re, c_im, coef, ts):
    s, w = xb.shape
    nb = bb_re.shape[0]
    nl = nb * 512

    def body(x_ref, bre_ref, bim_ref, cre_ref, cim_ref, coef_ref, hre_ref, him_ref, y_ref, ure, uim, car_re, car_im):
        @pl.when(pl.program_id(0) == 0)
        def _():
            car_re[...] = jnp.zeros_like(car_re)
            car_im[...] = jnp.zeros_like(car_im)

        for b in range(nb):
            xs = x_ref[:, b * 128:(b + 1) * 128].astype(BF16)
            ure[:, b * 512:(b + 1) * 512] = jnp.dot(xs, bre_ref[b], preferred_element_type=F32)
            uim[:, b * 512:(b + 1) * 512] = jnp.dot(xs, bim_ref[b], preferred_element_type=F32)
        _scan_tile(ure, uim, hre_ref, him_ref, coef_ref, car_re, car_im, ts, False)
        for b in range(nb):
            hr = hre_ref[:, b * 512:(b + 1) * 512].astype(BF16)
            hi = him_ref[:, b * 512:(b + 1) * 512].astype(BF16)
            y_ref[:, b * 128:(b + 1) * 128] = (jnp.dot(hr, cre_ref[b], preferred_element_type=F32)
                                               - jnp.dot(hi, cim_ref[b], preferred_element_type=F32))

    row = lambda wd: pl.BlockSpec((ts, wd), lambda i: (i, 0))
    return pl.pallas_call(
        body, name="s5_fwd", grid=(s // ts,),
        in_specs=[row(w), _whole(bb_re), _whole(bb_im), _whole(c_re), _whole(c_im), _whole(coef)],
        out_specs=[row(nl), row(nl), row(w)],
        out_shape=[jax.ShapeDtypeStruct((s, nl), F32), jax.ShapeDtypeStruct((s, nl), F32),
                   jax.ShapeDtypeStruct((s, w), F32)],
        scratch_shapes=[pltpu.VMEM((ts, nl), F32), pltpu.VMEM((ts, nl), F32), pltpu.VMEM((8, nl), F32),
                        pltpu.VMEM((8, nl), F32)],
        compiler_params=_cparams(("arbitrary",)))(xb, bb_re, bb_im, c_re, c_im, coef)


def _s5_bwd(dy, xb, h_re, h_im, bb_re, bb_im, c_re, c_im, coef, ts):
    s, w = xb.shape
    nb = bb_re.shape[0]
    nl = nb * 512
    nt = s // ts

    def body(dy_ref, x_ref, hre_ref, him_ref, halo_re, halo_im, bre_ref, bim_ref, cre_ref, cim_ref, coef_ref,
             dx_ref, dbre_ref, dbim_ref, dcre_ref, dcim_ref, dare_ref, daim_ref, gre, gim, car_re, car_im):
        i = pl.program_id(0)

        @pl.when(i == 0)
        def _():
            for r in (car_re, car_im, dbre_ref, dbim_ref, dcre_ref, dcim_ref, dare_ref, daim_ref):
                r[...] = jnp.zeros_like(r)

        for b in range(nb):
            dyb = dy_ref[:, b * 128:(b + 1) * 128].astype(BF16)
            gre[:, b * 512:(b + 1) * 512] = lax.dot_general(dyb, cre_ref[b], (((1,), (1,)), ((), ())),
                                                            preferred_element_type=F32)
            gim[:, b * 512:(b + 1) * 512] = -lax.dot_general(dyb, cim_ref[b], (((1,), (1,)), ((), ())),
                                                             preferred_element_type=F32)
            hr = hre_ref[:, b * 512:(b + 1) * 512].astype(BF16)
            hi = him_ref[:, b * 512:(b + 1) * 512].astype(BF16)
            dcre_ref[b] += lax.dot_general(hr, dyb, (((0,), (0,)), ((), ())), preferred_element_type=F32)
            dcim_ref[b] -= lax.dot_general(hi, dyb, (((0,), (0,)), ((), ())), preferred_element_type=F32)
        first = (i != nt - 1).astype(F32)
        _scan_tile(gre, gim, gre, gim, coef_ref, car_re, car_im, ts, True,
                   extra=(hre_ref, him_ref, halo_re, halo_im, first, dare_ref, daim_ref))
        for b in range(nb):
            gr = gre[:, b * 512:(b + 1) * 512].astype(BF16)
            gi = gim[:, b * 512:(b + 1) * 512].astype(BF16)
            xs = x_ref[:, b * 128:(b + 1) * 128].astype(BF16)
            dx_ref[:, b * 128:(b + 1) * 128] = (
                lax.dot_general(gr, bre_ref[b], (((1,), (1,)), ((), ())), preferred_element_type=F32)
                + lax.dot_general(gi, bim_ref[b], (((1,), (1,)), ((), ())), preferred_element_type=F32))
            dbre_ref[b] += lax.dot_general(xs, gr, (((0,), (0,)), ((), ())), preferred_element_type=F32)
            dbim_ref[b] += lax.dot_general(xs, gi, (((0,), (0,)), ((), ())), preferred_element_type=F32)

    row = lambda wd: pl.BlockSpec((ts, wd), lambda i: (nt - 1 - i, 0))
    halo = pl.BlockSpec((8, nl), lambda i: (jnp.maximum((nt - 1 - i) * (ts // 8) - 1, 0), 0))
    return pl.pallas_call(
        body, name="s5_bwd", grid=(nt,),
        in_specs=[row(w), row(w), row(nl), row(nl), halo, halo, _whole(bb_re), _whole(bb_im), _whole(c_re),
                  _whole(c_im), _whole(coef)],
        out_specs=[row(w), _whole(bb_re), _whole(bb_im), _whole(c_re), _whole(c_im),
                   pl.BlockSpec((8, nl), lambda i: (0, 0)), pl.BlockSpec((8, nl), lambda i: (0, 0))],
        out_shape=[jax.ShapeDtypeStruct((s, w), F32), jax.ShapeDtypeStruct(bb_re.shape, F32),
                   jax.ShapeDtypeStruct(bb_im.shape, F32), jax.ShapeDtypeStruct(c_re.shape, F32),
                   jax.ShapeDtypeStruct(c_im.shape, F32), jax.ShapeDtypeStruct((8, nl), F32),
                   jax.ShapeDtypeStruct((8, nl), F32)],
        scratch_shapes=[pltpu.VMEM((ts, nl), F32), pltpu.VMEM((ts, nl), F32), pltpu.VMEM((8, nl), F32),
                        pltpu.VMEM((8, nl), F32)],
        compiler_params=_cparams(("arbitrary",)))(dy, xb, h_re, h_im, h_re, h_im, bb_re, bb_im, c_re, c_im, coef)


def _final(x, mo, target, fg, ts):
    s, d = x.shape

    def f(x, mo, fg, tgt):
        y = _rms(x + mo, fg)[0]
        err = y - tgt
        return 0.5 * jnp.sum(jnp.mean(err * err, axis=-1, keepdims=True), axis=0, keepdims=True)

    def body(x_ref, mo_ref, t_ref, fg_ref, dh_ref, dfg_ref, loss_ref):
        @pl.when(pl.program_id(0) == 0)
        def _():
            dfg_ref[...] = jnp.zeros_like(dfg_ref)
            loss_ref[...] = jnp.zeros_like(loss_ref)

        loss, vjp = jax.vjp(f, x_ref[...], mo_ref[...], fg_ref[...], t_ref[...])
        _, dmo, dfg, _ = vjp(jnp.ones((1, 1), F32))
        dh_ref[...] = dmo
        dfg_ref[...] += dfg
        loss_ref[...] += jnp.broadcast_to(loss, loss_ref.shape)

    row = pl.BlockSpec((ts, d), lambda i: (i, 0))
    return pl.pallas_call(
        body, name="final", grid=(s // ts,), in_specs=[row, row, row, _whole(fg)],
        out_specs=[row, _whole(fg), pl.BlockSpec((8, 128), lambda i: (0, 0))],
        out_shape=[jax.ShapeDtypeStruct((s, d), F32), jax.ShapeDtypeStruct(fg.shape, F32),
                   jax.ShapeDtypeStruct((8, 128), F32)],
        compiler_params=_cparams(("arbitrary",)))(x, mo, target, fg)


N_CHIPS = 4


def _other_chips(x, y):
    return [((1 - x, y), 2 * (1 - x) + y), ((x, 1 - y), 2 * x + 1 - y), ((1 - x, 1 - y), 2 * (1 - x) + 1 - y)]


def _comm_call(body, name, srcs, out_shapes, n_sems):
    n = len(srcs)
    return pl.pallas_call(
        body, name=name, in_specs=[pl.BlockSpec(memory_space=pl.ANY)] * n,
        out_specs=[pl.BlockSpec(memory_space=pl.ANY)] * n, out_shape=out_shapes,
        scratch_shapes=[pltpu.SemaphoreType.DMA((n, n_sems)), pltpu.SemaphoreType.DMA((n, n_sems)),
                        pltpu.SemaphoreType.DMA((n,))],
        compiler_params=pltpu.CompilerParams(has_side_effects=True))(*srcs)


def _gather(srcs, name):
    n = len(srcs)

    def body(*refs):
        src, out = refs[:n], refs[n:2 * n]
        send_sems, recv_sems, local_sems = refs[2 * n:]
        x, y, c = lax.axis_index("x"), lax.axis_index("y"), lax.axis_index("c")
        me, sib_slot, sib = 4 * x + 2 * y + c, 4 * x + 2 * y + 1 - c, (x, y, 1 - c)
        chips = _other_chips(x, y)

        def cp(a, k, src_ref, slot, to):
            return pltpu.make_async_remote_copy(
                src_ref=src_ref, dst_ref=out[a].at[slot], send_sem=send_sems.at[a, k], recv_sem=recv_sems.at[a, k],
                device_id=to, device_id_type=pl.DeviceIdType.MESH)

        local = [pltpu.make_async_copy(src[a], out[a].at[me], local_sems.at[a]) for a in range(n)]
        first = [cp(a, 0, src[a], me, sib) for a in range(n)]
        first += [cp(a, 1 + j, src[a], me, (*chip, c)) for j, (chip, _) in enumerate(chips) for a in range(n)]
        for d in local + first:
            d.start()
        passed = []
        for j, (chip, q) in enumerate(chips):
            for a in range(n):
                cp(a, 1 + j, src[a], 2 * q + c, sib).wait_recv()
                fwd = cp(a, 4 + j, out[a].at[2 * q + c], 2 * q + c, sib)
                fwd.start()
                passed.append(fwd)
        for a in range(n):
            cp(a, 0, src[a], sib_slot, sib).wait_recv()
        for j, (chip, q) in enumerate(chips):
            for a in range(n):
                cp(a, 4 + j, src[a], 2 * q + 1 - c, sib).wait_recv()
        for d in first + passed:
            d.wait_send()
        for d in local:
            d.wait()

    return _comm_call(body, name, srcs, [jax.ShapeDtypeStruct((N_DEV,) + s.shape, s.dtype) for s in srcs], 7)


def _all_peers(x, y, c):
    out = []
    for k in range(1, N_DEV):
        px = 1 - x if k & 4 else x
        py = 1 - y if k & 2 else y
        pc = 1 - c if k & 1 else c
        out.append(((px, py, pc), 4 * px + 2 * py + pc))
    return out


_HBM = pl.BlockSpec(memory_space=pltpu.HBM)
_SEM = pl.BlockSpec(memory_space=pltpu.SEMAPHORE)
_DATAFLOW = pltpu.SideEffectType.DATAFLOW_SIDE_EFFECTING


def _send_whole(ref, slot):
    return ref


def _send_slot(ref, slot):
    return ref.at[slot]


def _direct_copies(src, land, send_sems, recv_sems, picks, arriving):
    x, y, c = lax.axis_index("x"), lax.axis_index("y"), lax.axis_index("c")
    me = 4 * x + 2 * y + c
    out = []
    for k, (pos, slot) in enumerate(_all_peers(x, y, c)):
        for a in range(len(src)):
            sem = a * (N_DEV - 1) + k
            out.append(pltpu.make_async_remote_copy(
                src_ref=picks[a](src[a], slot), dst_ref=land[a].at[slot if arriving else me],
                send_sem=send_sems.at[sem], recv_sem=recv_sems.at[sem], device_id=pos,
                device_id_type=pl.DeviceIdType.MESH))
    return out


def _direct_start(srcs, lands, picks, name):
    n = len(srcs)

    def body(*refs):
        src, land = refs[:n], refs[n:2 * n]
        send_sems, recv_sems = refs[2 * n], refs[2 * n + 1]
        for push in _direct_copies(src, land, send_sems, recv_sems, picks, False):
            push.start()
        refs[-1][...] = jnp.zeros_like(refs[-1])

    arrays = [pltpu.with_memory_space_constraint(t, pltpu.HBM) for t in list(srcs) + list(lands)]
    outs = pl.pallas_call(
        body, name=name, in_specs=[_HBM] * (2 * n),
        out_specs=(_SEM, _SEM, *[_HBM] * (2 * n), pl.BlockSpec(memory_space=pltpu.VMEM)),
        out_shape=(pltpu.SemaphoreType.DMA((n * (N_DEV - 1),)), pltpu.SemaphoreType.DMA((n * (N_DEV - 1),)),
                   *[pltpu.HBM(t.shape, t.dtype) for t in arrays], jax.ShapeDtypeStruct((8, 128), F32)),
        input_output_aliases={i: 2 + i for i in range(2 * n)},
        compiler_params=pltpu.CompilerParams(has_side_effects=_DATAFLOW))(*arrays)
    return outs[:-1], outs[-1]


def _direct_wait(started, picks, after, name):
    send_sems, recv_sems, *thru = started
    n = len(thru) // 2

    def body(*refs):
        src, land = refs[:n], refs[n:2 * n]
        for arrive in _direct_copies(src, land, refs[2 * n], refs[2 * n + 1], picks, True):
            arrive.wait_send()
            arrive.wait_recv()

    outs = pl.pallas_call(
        body, name=name, in_specs=[_HBM] * (2 * n) + [_SEM, _SEM, pl.BlockSpec(memory_space=pl.ANY)],
        out_specs=[_HBM] * (2 * n), out_shape=[pltpu.HBM(t.shape, t.dtype) for t in thru],
        input_output_aliases={i: i for i in range(2 * n)},
        compiler_params=pltpu.CompilerParams(has_side_effects=_DATAFLOW))(*thru, send_sems, recv_sems, after)
    return outs[:n], outs[n:]


def _pair_scatter(gs, name):
    n = len(gs)

    def body(*refs):
        src, out = refs[:n], refs[n:2 * n]
        send_sems, recv_sems, _ = refs[2 * n:]
        x, y, c = lax.axis_index("x"), lax.axis_index("y"), lax.axis_index("c")
        sends = []
        for q in range(N_CHIPS):
            for a in range(n):
                d = pltpu.make_async_remote_copy(
                    src_ref=src[a].at[2 * q + 1 - c], dst_ref=out[a].at[q], send_sem=send_sems.at[a, q],
                    recv_sem=recv_sems.at[a, q], device_id=(x, y, 1 - c), device_id_type=pl.DeviceIdType.MESH)
                d.start()
                sends.append(d)
        for d in sends:
            d.wait_recv()
        for d in sends:
            d.wait_send()

    return _comm_call(body, name, gs, [jax.ShapeDtypeStruct((N_CHIPS,) + g.shape[1:], g.dtype) for g in gs], N_CHIPS)


def _cross_scatter(ps, name):
    n = len(ps)

    def body(*refs):
        src, out = refs[:n], refs[n:2 * n]
        send_sems, recv_sems, local_sems = refs[2 * n:]
        x, y, c = lax.axis_index("x"), lax.axis_index("y"), lax.axis_index("c")
        mine = 2 * x + y
        chips = _other_chips(x, y)
        local = [pltpu.make_async_copy(src[a].at[mine], out[a].at[mine], local_sems.at[a]) for a in range(n)]
        for d in local:
            d.start()
        sends = []
        for j, (chip, q) in enumerate(chips):
            for a in range(n):
                d = pltpu.make_async_remote_copy(
                    src_ref=src[a].at[q], dst_ref=out[a].at[mine], send_sem=send_sems.at[a, j],
                    recv_sem=recv_sems.at[a, j], device_id=(*chip, c), device_id_type=pl.DeviceIdType.MESH)
                d.start()
                sends.append(d)
        for j, (chip, q) in enumerate(chips):
            for a in range(n):
                pltpu.make_async_remote_copy(
                    src_ref=src[a].at[q], dst_ref=out[a].at[q], send_sem=send_sems.at[a, j],
                    recv_sem=recv_sems.at[a, j], device_id=(*chip, c), device_id_type=pl.DeviceIdType.MESH).wait_recv()
        for d in sends:
            d.wait_send()
        for d in local:
            d.wait()

    return _comm_call(body, name, ps, [jax.ShapeDtypeStruct(p.shape, p.dtype) for p in ps], 3)


def _pair_sum(g, got, c_idx, out_dtype, name):
    _, r, c = g.shape
    lanes = -(-c // 128) * 128
    tr = _pick_rows(r, max(8, (2 * 1024 * 1024) // (lanes * 4)))
    g4 = g.reshape((N_CHIPS, 2) + g.shape[1:])

    def body(c_ref, g_ref, got_ref, o_ref):
        o_ref[...] = (g_ref[...] + got_ref[...]).astype(o_ref.dtype)

    return pl.pallas_call(
        body, name=name,
        grid_spec=pltpu.PrefetchScalarGridSpec(
            num_scalar_prefetch=1, grid=(N_CHIPS, r // tr),
            in_specs=[pl.BlockSpec((None, None, tr, c), lambda q, i, cr: (q, cr[0], i, 0)),
                      pl.BlockSpec((None, tr, c), lambda q, i, cr: (q, i, 0))],
            out_specs=pl.BlockSpec((None, tr, c), lambda q, i, cr: (q, i, 0))),
        out_shape=jax.ShapeDtypeStruct(got.shape, out_dtype),
        compiler_params=_cparams(("parallel", "parallel")))(c_idx, g4, got)


def _pick_rows(r, pref):
    t = (min(pref, r) // 8) * 8
    while t >= 8:
        if r % t == 0:
            return t
        t -= 8
    return r


def _w_in_from_shards(t, lo, hi):
    n, r, cs = t.shape
    tr = _pick_rows(r, 256)
    wm = n * cs - (hi - lo)

    def body(t_ref, m_ref, b_ref):
        full = jnp.concatenate([t_ref[j] for j in range(n)], axis=1)
        m_ref[...] = jnp.concatenate([full[:, :lo], full[:, hi:]], axis=1)
        b_ref[...] = jnp.concatenate([full[:, lo:hi], jnp.zeros((tr, 128 - (hi - lo)), full.dtype)], axis=1)

    return pl.pallas_call(
        body, name="w_in_layout", grid=(r // tr,), in_specs=[pl.BlockSpec((n, tr, cs), lambda i: (0, i, 0))],
        out_specs=[pl.BlockSpec((tr, wm), lambda i: (i, 0)), pl.BlockSpec((tr, 128), lambda i: (i, 0))],
        out_shape=[jax.ShapeDtypeStruct((r, wm), t.dtype), jax.ShapeDtypeStruct((r, 128), t.dtype)],
        compiler_params=_cparams(("parallel",)))(t)


def _w_in_to_shards(gm, gb, lo, hi, dtype):
    r, wm = gm.shape
    cs = (wm + hi - lo) // N_DEV
    tr = _pick_rows(r, 64)

    def body(m_ref, b_ref, o_ref):
        m = m_ref[...]
        full = jnp.concatenate([m[:, :lo], b_ref[:, :hi - lo], m[:, lo:]], axis=1)
        for j in range(N_DEV):
            o_ref[j] = full[:, j * cs:(j + 1) * cs].astype(o_ref.dtype)

    return pl.pallas_call(
        body, name="dw_in_layout", grid=(r // tr,),
        in_specs=[pl.BlockSpec((tr, wm), lambda i: (i, 0)), pl.BlockSpec((tr, 128), lambda i: (i, 0))],
        out_specs=pl.BlockSpec((N_DEV, tr, cs), lambda i: (0, i, 0)),
        out_shape=jax.ShapeDtypeStruct((N_DEV, r, cs), dtype), compiler_params=_cparams(("parallel",)))(gm, gb)


def _pack(arrs, dtype, lead=()):
    nlead = len(lead)
    flat = jnp.concatenate([a.astype(dtype).reshape(lead + (-1,)) for a in arrs], axis=nlead)
    n = flat.shape[-1]
    unit = PACK_WIDTH * PACK_ROWS
    pad = (-n) % unit
    flat = jnp.pad(flat, [(0, 0)] * nlead + [(0, pad)])
    return flat.reshape(lead + ((n + pad) // PACK_WIDTH, PACK_WIDTH))


def _unpack(buf, shapes, lead=()):
    flat = buf.reshape(lead + (-1,))
    out, off = [], 0
    for shp in shapes:
        n = math.prod(shp)
        out.append(flat[..., off:off + n].reshape(lead + tuple(shp)))
        off += n
    return out


def _adam_math(w, g, m, v):
    m = ADAM_B1 * m + (1.0 - ADAM_B1) * g
    v = ADAM_B2 * v + (1.0 - ADAM_B2) * (g * g)
    m_hat = m / (1.0 - ADAM_B1 ** ADAM_STEP)
    v_hat = v / (1.0 - ADAM_B2 ** ADAM_STEP)
    delta = -ADAM_LR * (m_hat / (jnp.sqrt(v_hat) + ADAM_EPS) + ADAM_WD * w)
    return delta, m, v


def _sum_adam(parts, w, m, v, name, own=None, me=None):
    r, c = w.shape
    nparts = parts.shape[0]
    lanes = -(-c // 128) * 128
    tr = _pick_rows(r, max(8, (6 * 1024 * 1024) // (nparts * lanes * 4)))

    def finish(g, w_ref, m_ref, v_ref, g_ref, d_ref, nm_ref, nv_ref):
        d, nm, nv = _adam_math(w_ref[...], g, m_ref[...], v_ref[...])
        g_ref[...] = g
        d_ref[...] = d
        nm_ref[...] = nm
        nv_ref[...] = nv

    out_shape = [jax.ShapeDtypeStruct((r, c), F32)] * 4
    if own is None:
        def body(p_ref, *rest):
            g = p_ref[0].astype(F32)
            for j in range(1, nparts):
                g = g + p_ref[j].astype(F32)
            finish(g, *rest)

        row = pl.BlockSpec((tr, c), lambda i: (i, 0))
        return pl.pallas_call(
            body, name=name, grid=(r // tr,),
            in_specs=[pl.BlockSpec((nparts, tr, c), lambda i: (0, i, 0)), row, row, row],
            out_specs=[row] * 4, out_shape=out_shape, compiler_params=_cparams(("parallel",)))(parts, w, m, v)

    def body(me_ref, p_ref, own_ref, *rest):
        mine = own_ref[...].astype(F32)
        g = jnp.where(me_ref[0] == 0, mine, p_ref[0].astype(F32))
        for j in range(1, nparts):
            g = g + jnp.where(me_ref[0] == j, mine, p_ref[j].astype(F32))
        finish(g, *rest)

    row = pl.BlockSpec((tr, c), lambda i, me_ref: (i, 0))
    return pl.pallas_call(
        body, name=name,
        grid_spec=pltpu.PrefetchScalarGridSpec(
            num_scalar_prefetch=1, grid=(r // tr,),
            in_specs=[pl.BlockSpec((nparts, tr, c), lambda i, me_ref: (0, i, 0)),
                      pl.BlockSpec((None, tr, c), lambda i, me_ref: (me_ref[0], i, 0)), row, row, row],
            out_specs=[row] * 4),
        out_shape=out_shape, compiler_params=_cparams(("parallel",)))(me, parts, own, w, m, v)


def _block_diag(t):
    nb, g, a, b = t.shape
    eye = jnp.eye(g, dtype=t.dtype)
    return jnp.einsum('ngab,gh->ngahb', t, eye).reshape(nb, g * a, g * b)


def _diag_blocks(t, a, b):
    nb = t.shape[0]
    g = S5_GROUPS_PER_BLOCK
    t = t.reshape(nb, g, a, g, b)
    return jnp.stack([t[:, j, :, j, :] for j in range(g)], axis=1)


def _local_step(x, mem, target, p, late_weights=None, early_grads=None, last_grads=None):
    s, d = x.shape
    gw = d // 2
    nh = gw // GDN_HEAD_DIM
    ng = gw // S5_GROUP
    nb = ng // S5_GROUPS_PER_BLOCK
    nl = ng * S5_STATE
    ts = min(256, s)
    nt = s // ts
    grads = {}

    w_main, w_ba = p['w_main'], p['w_ba']
    CB_ZA, CB_XB, CB_ZB, CB_QC, CB_ZC, CB_G = 3, 4, 5, 6, 7, 8

    u = _tile_fwd(_rms, "rms_fwd", nt, [_rt(x, ts)], [p['norm_g']],
                  [((s, d), BF16, (ts, d), lambda i: (i, 0))])[0]
    proj = _mm(u, w_main, tm=1024, tn=2048, tk=512, name="proj_main")
    pba = _mm(u, w_ba, name="proj_ba")
    if late_weights is not None:
        p = {**p, **late_weights(proj)}

    conv_w = p['conv_w']
    col = lambda arr, cb: (arr, (s, GDN_HEAD_DIM), lambda i, cb=cb: (0, cb + i))
    qkv = []
    for j, mode in enumerate(('q', 'k', 'v')):
        off = j * nh
        qkv.append(_tile_fwd(
            _gdn_pre(mode), "gdn_pre_" + mode, nh, [col(proj, off), (conv_w, (CONV_WIDTH, GDN_HEAD_DIM), lambda i, off=off: (0, off + i))],
            [], [((s, gw), F32, (s, GDN_HEAD_DIM), lambda i: (0, i))])[0])
    q, k, v = qkv
    lane = jnp.arange(128)[:, None]
    colh = jnp.arange(gw)[None, :] // GDN_HEAD_DIM
    e_beta = (lane == colh).astype(F32)
    e_g = (lane == colh + nh).astype(F32)
    alog_row = jnp.pad(p['gdn_a_log'], ((0, 0), (nh, 128 - 2 * nh)))
    dtb_row = jnp.pad(p['gdn_dt_bias'], ((0, 0), (nh, 128 - 2 * nh)))
    row_gw = lambda: ((s, gw), F32, (ts, gw), lambda i: (i, 0))
    betab, gb = _tile_fwd(_gdn_gates, "gdn_gates", nt, [_rt(pba, ts)], [alog_row, dtb_row, e_beta, e_g],
                          [row_gw(), row_gw()])
    *intra, t_inv = _gdn_intra_fwd(q, k, v, gb, betab)
    o_raw, states = _gdn_inter_fwd(*intra, gb)
    ga = _tile_fwd(_gdn_post, "gdn_post", nt, [_rt(o_raw, ts), _rt(proj, ts, CB_ZA, gw)], [p['gdn_norm_g']],
                   [((s, gw), BF16, (ts, gw), lambda i: (i, 0))])[0]

    e_rep = (jnp.arange(S5_STATE)[:, None] == jnp.arange(S5_STATE * S5_GROUP)[None, :] // S5_GROUP).astype(F32)
    s5_in = [p['s5_lambda_re'], p['s5_lambda_im'], p['s5_log_dt'].reshape(ng, 1),
             p['s5_b_re'].reshape(ng, S5_STATE * S5_GROUP), p['s5_b_im'].reshape(ng, S5_STATE * S5_GROUP), e_rep]
    one = lambda shp: (shp, F32, shp, lambda i, n=len(shp): (0,) * n)
    ab_re, ab_im, bbr, bbi = _tile_fwd(_s5_params, "s5_params", 1, [], s5_in,
                                       [one((ng, S5_STATE)), one((ng, S5_STATE)), one((ng, S5_STATE * S5_GROUP)),
                                        one((ng, S5_STATE * S5_GROUP))])
    coef = _s5_coef(ab_re.reshape(1, nl), ab_im.reshape(1, nl))
    to_bd_b = lambda t: _block_diag(t.reshape(nb, S5_GROUPS_PER_BLOCK, S5_STATE, S5_GROUP).transpose(0, 1, 3, 2))
    to_bd_c = lambda t: _block_diag(t.reshape(nb, S5_GROUPS_PER_BLOCK, S5_GROUP, S5_STATE).transpose(0, 1, 3, 2))
    bbd_re, bbd_im = to_bd_b(bbr).astype(BF16), to_bd_b(bbi).astype(BF16)
    cbd_re, cbd_im = to_bd_c(p['s5_c_re']).astype(BF16), to_bd_c(p['s5_c_im']).astype(BF16)
    xb_arr = lax.slice_in_dim(proj, CB_XB * gw, (CB_XB + 1) * gw, axis=1)
    h_re, h_im, ylin = _s5_fwd(xb_arr, bbd_re, bbd_im, cbd_re, cbd_im, coef, ts)
    gl = _tile_fwd(_s5_post1, "s5_post1", nt, [_rt(ylin, ts), _rt(proj, ts, CB_XB, gw)], [p['s5_d']],
                   [((s, gw), BF16, (ts, gw), lambda i: (i, 0))])[0]
    tglu = _mm(gl, p['s5_w_glu'], b_shards=True, name="s5_glu")
    gbb = _tile_fwd(_s5_post2, "s5_post2", nt, [_rt(tglu, ts), _rt(proj, ts, CB_ZB, gw)], [],
                    [((s, gw), BF16, (ts, gw), lambda i: (i, 0))])[0]

    m_len = mem.shape[0]
    mem_n = _tile_fwd(_rms, "mem_rms", 1, [_rt(mem, m_len)], [p['mem_norm_g']],
                      [((m_len, d), BF16, (m_len, d), lambda i: (i, 0))])[0]
    kv = _mm(mem_n, p['w_kv_mem'], name="mem_kv")
    gcc = _tile_fwd(_attn, "attn", nt, [_rt(proj, ts, CB_QC, gw), _rt(proj, ts, CB_ZC, gw)], [kv],
                    [((s, gw), BF16, (ts, gw), lambda i: (i, 0))])[0]

    p_a = _mm(ga, p['w_br_a'], b_shards=True, name="br_a")
    p_b = _mm(gbb, p['w_br_b'], b_shards=True, name="br_b")
    p_c = _mm(gcc, p['w_br_c'], b_shards=True, name="br_c")
    gate_acts = [_rt(proj, ts, CB_G // 2 + j, d) for j in range(3)]
    merged = _tile_fwd(_merge, "merge", nt, gate_acts + [_rt(p_a, ts), _rt(p_b, ts), _rt(p_c, ts)], [],
                       [((s, d), BF16, (ts, d), lambda i: (i, 0))])[0]
    mo = _mm(merged, p['w_out'], name="out_proj")
    dh, dfg, loss = _final(x, mo, target, p['final_g'].reshape(1, d), ts)
    grads['final_g'] = dfg.reshape(d)

    dmerged = _mm(dh, p['w_out'], tb=True, name="d_merged")
    grads['w_out'] = _mm(merged, dh, ta=True, name="dw_out")
    row_d = lambda dt: ((s, d), dt, (ts, d), lambda i: (i, 0))
    dg0, dg1, dg2, dpa, dpb, dpc = _tile_bwd(
        _merge, "merge_bwd", nt, gate_acts + [_rt(p_a, ts), _rt(p_b, ts), _rt(p_c, ts)], [], [_rt(dmerged, ts)],
        [row_d(BF16)] * 6, [])
    dga = _mm(dpa, p['w_br_a'], tb=True, b_shards=True, name="d_ga")
    dgbb = _mm(dpb, p['w_br_b'], tb=True, b_shards=True, name="d_gb")
    dgcc = _mm(dpc, p['w_br_c'], tb=True, b_shards=True, name="d_gc")
    grads['w_br_a'] = _mm(ga, dpa, ta=True, out_shards=N_DEV, name="dw_br_a")
    grads['w_br_b'] = _mm(gbb, dpb, ta=True, out_shards=N_DEV, name="dw_br_b")
    grads['w_br_c'] = _mm(gcc, dpc, ta=True, out_shards=N_DEV, name="dw_br_c")
    row_h = lambda dt: ((s, gw), dt, (ts, gw), lambda i: (i, 0))

    dqc, dzc, dkv = _tile_bwd(_attn, "attn_bwd", nt, [_rt(proj, ts, CB_QC, gw), _rt(proj, ts, CB_ZC, gw)], [kv],
                              [_rt(dgcc, ts)], [row_h(BF16), row_h(BF16)], [True])
    grads['w_kv_mem'] = _mm(mem_n, dkv, ta=True, name="dw_kv")
    dmem_n = _mm(dkv, p['w_kv_mem'], tb=True, name="d_mem_n")
    grads['mem_norm_g'] = _tile_bwd(_rms, "mem_rms_bwd", 1, [_rt(mem, m_len)], [p['mem_norm_g']],
                                    [_rt(dmem_n, m_len)], [None], [True])[0]

    dtglu, dzb = _tile_bwd(_s5_post2, "s5_post2_bwd", nt, [_rt(tglu, ts), _rt(proj, ts, CB_ZB, gw)], [],
                           [_rt(dgbb, ts)], [((s, 2 * gw), BF16, (ts, 2 * gw), lambda i: (i, 0)), row_h(BF16)], [])
    grads['s5_w_glu'] = _mm(gl, dtglu, ta=True, out_shards=N_DEV, name="dw_glu")
    s5_d = p['s5_d']
    if early_grads is not None:
        s5_d = s5_d + early_grads(grads)[:1, :1]
    dgl = _mm(dtglu, p['s5_w_glu'], tb=True, b_shards=True, name="d_gl")
    dylin, dxb1, dd = _tile_bwd(_s5_post1, "s5_post1_bwd", nt, [_rt(ylin, ts), _rt(proj, ts, CB_XB, gw)],
                                [s5_d], [_rt(dgl, ts)], [row_h(F32), row_h(F32)], [True])
    grads['s5_d'] = dd
    dxb2, dbbd_re, dbbd_im, dcbd_re, dcbd_im, da_re, da_im = _s5_bwd(dylin, xb_arr, h_re, h_im, bbd_re, bbd_im,
                                                                    cbd_re, cbd_im, coef, ts)
    from_bd_b = lambda t: _diag_blocks(t, S5_GROUP, S5_STATE).transpose(0, 1, 3, 2).reshape(ng, S5_STATE * S5_GROUP)
    from_bd_c = lambda t: _diag_blocks(t, S5_STATE, S5_GROUP).transpose(0, 1, 3, 2).reshape(1, ng, S5_GROUP, S5_STATE)
    grads['s5_c_re'], grads['s5_c_im'] = from_bd_c(dcbd_re), from_bd_c(dcbd_im)
    s5_cts = [jnp.sum(da_re, axis=0).reshape(ng, S5_STATE), jnp.sum(da_im, axis=0).reshape(ng, S5_STATE),
              from_bd_b(dbbd_re), from_bd_b(dbbd_im)]
    dlr, dli, dlogdt, dbr, dbi = _tile_bwd(_s5_params, "s5_params_bwd", 1, [], s5_in,
                                           [(c, c.shape, lambda i: (0, 0)) for c in s5_cts], [],
                                           [True, True, True, True, True, False])
    grads['s5_lambda_re'], grads['s5_lambda_im'] = dlr[None], dli[None]
    grads['s5_log_dt'] = dlogdt.reshape(1, ng)
    grads['s5_b_re'] = dbr.reshape(1, ng, S5_STATE, S5_GROUP)
    grads['s5_b_im'] = dbi.reshape(1, ng, S5_STATE, S5_GROUP)
    dxb = (dxb1 + dxb2).astype(BF16)

    do_raw, dza, dgng = _tile_bwd(_gdn_post, "gdn_post_bwd", nt, [_rt(o_raw, ts), _rt(proj, ts, CB_ZA, gw)],
                                  [p['gdn_norm_g']], [_rt(dga, ts)], [row_h(F32), row_h(BF16)], [True])
    grads['gdn_norm_g'] = dgng
    *intra_cts, dgb_inter = _gdn_inter_bwd(*intra, gb, states, do_raw)
    dq, dk, dv, dgb, dbetab = _gdn_intra_bwd(q, k, v, gb, betab, t_inv, intra_cts, dgb_inter)
    dpba, dalog, ddtb = _tile_bwd(_gdn_gates, "gdn_gates_bwd", nt, [_rt(pba, ts)], [alog_row, dtb_row, e_beta, e_g],
                                  [_rt(dbetab, ts), _rt(dgb, ts)], [((s, 128), BF16, (ts, 128), lambda i: (i, 0))],
                                  [True, True, False, False])
    grads['gdn_a_log'] = dalog[:, nh:2 * nh]
    grads['gdn_dt_bias'] = ddtb[:, nh:2 * nh]
    dqkv, dconv = [], []
    for j, (mode, ct) in enumerate((('q', dq), ('k', dk), ('v', dv))):
        off = j * nh
        wspec = (conv_w, (CONV_WIDTH, GDN_HEAD_DIM), lambda i, off=off: (0, off + i))
        dxc, dwc = _tile_bwd(
            _gdn_pre(mode), "gdn_pre_bwd_" + mode, nh, [col(proj, off), wspec], [], [col(ct, 0)],
            [((s, gw), BF16, (s, GDN_HEAD_DIM), lambda i: (0, i)),
             ((CONV_WIDTH, gw), F32, (CONV_WIDTH, GDN_HEAD_DIM), lambda i: (0, i))], [])
        dqkv.append(dxc)
        dconv.append(dwc)
    grads['conv_w'] = jnp.concatenate(dconv, axis=1)

    dproj = jnp.concatenate(dqkv + [dza, dxb, dzb, dqc, dzc, dg0, dg1, dg2], axis=1)
    grads['w_main'] = _mm(u, dproj, ta=True, tm=1024, tn=2048, tk=512, name="dw_main")
    grads['w_ba'] = _mm(u, dpba, ta=True, name="dw_ba")
    if last_grads is not None:
        dpba = dpba + last_grads(grads)[:1, :1].astype(BF16)
    du = _mm(dpba, w_ba, tb=True, name="du_ba")
    du = _mm(dproj, w_main, tb=True, addend=du, tm=512, tn=2048, tk=1024, name="du_main")
    grad_x, dng = _tile_bwd(_rms, "rms_bwd", nt, [_rt(x, ts)], [p['norm_g']], [_rt(du, ts)],
                            [row_d(F32) + (dh,)], [True])
    grads['norm_g'] = dng
    return loss, grad_x, grads


def _to_shards(name, g):
    if SHARDED[name] == 'row':
        return g.reshape((N_DEV, g.shape[0] // N_DEV) + g.shape[1:])
    r, c = g.shape
    return g.reshape(r, N_DEV, c // N_DEV).transpose(1, 0, 2)


def _from_shards(name, t):
    if SHARDED[name] == 'row':
        return t.reshape((t.shape[0] * t.shape[1],) + t.shape[2:])
    n, r, c = t.shape
    return t.transpose(1, 0, 2).reshape(r, n * c)


def _step(x, mem, target, w, m, v):
    sharded = list(SHARDED)
    shard_shapes = {n: tuple(w[n].shape[1:]) for n in sharded}
    d = x.shape[-1]
    ba_lo = 2 * d
    ba_hi = ba_lo + 2 * (d // 2 // GDN_HEAD_DIM)

    w_in_all = _gather([w['w_in'][0].astype(BF16)], "gather_w_in")[0]
    late = [w[n][0].astype(BF16) for n in OVERLAPPED] + [w['conv_w'][0]]
    late, w_in_all = lax.optimization_barrier((late, w_in_all))
    every = [_send_whole] * len(late)
    lands = [jnp.broadcast_to(t[None], (N_DEV,) + t.shape) for t in late]
    gather_started, token = _direct_start(late, lands, every, "gather_rest_start")
    full = {}
    full['w_main'], full['w_ba'] = _w_in_from_shards(w_in_all, ba_lo, ba_hi)
    for n in REPLICATED:
        full[n] = w[n]
    for n in ('s5_lambda_re', 's5_lambda_im', 's5_c_re', 's5_c_im'):
        full[n] = w[n][0]
    full['norm_g'] = w['norm_g'] + token[:1, :1]

    def late_weights(proj):
        got = dict(zip(OVERLAPPED + ['conv_w'], _direct_wait(gather_started, every, proj, "gather_rest_wait")[1]))
        for n in ('w_kv_mem', 'w_out', 'conv_w'):
            got[n] = _from_shards(n, got[n])
        return got

    slots = [_send_slot] * len(OVERLAPPED)
    scatter_started = []

    def early_grads(grads):
        gs = [_to_shards(n, grads[n]) if SHARDED[n] == 'row' else grads[n] for n in OVERLAPPED]
        started, tok = _direct_start(gs, [lax.empty(g.shape, g.dtype) for g in gs], slots, "scatter_early_start")
        scatter_started.append(started)
        return tok

    def last_grads(grads):
        gs = [_w_in_to_shards(grads['w_main'], grads['w_ba'], ba_lo, ba_hi, BF16), _to_shards('conv_w', grads['conv_w'])]
        started, tok = _direct_start(gs, [lax.empty(g.shape, g.dtype) for g in gs], slots[:2], "scatter_last_start")
        scatter_started.append(started)
        return tok

    loss, grad_x, grads = _local_step(x[0], mem[0], target[0], full, late_weights, early_grads, last_grads)
    res = {}
    me = (4 * lax.axis_index("x") + 2 * lax.axis_index("y") + lax.axis_index("c")).astype(jnp.int32).reshape(1)

    def update(names, exchanged):
        for n, own, part in zip(names, *exchanged):
            outs = _sum_adam(part, w[n][0], m[n][0], v[n][0], name="adam_" + n, own=own, me=me)
            for kind, t in zip(('grad', 'delta', 'new_m', 'new_v'), outs):
                res[kind, n] = t[None]

    update(OVERLAPPED, _direct_wait(scatter_started[0], slots, grad_x, "scatter_early_wait"))

    small = _pack([grads[n].reshape(w[n].shape) for n in REPLICATED] + [loss[:1, :1]], F32)
    allp = _gather([small], "gather_small")[0]
    zero = jnp.zeros((1, 1), F32)
    outs = _sum_adam(allp, *[_pack([t[n] for n in REPLICATED] + [zero], F32) for t in (w, m, v)], name="adam_small")
    shapes = [w[n].shape for n in REPLICATED] + [(1, 1)]
    for kind, buf in zip(('grad', 'delta', 'new_m', 'new_v'), outs):
        got = _unpack(buf, shapes)
        for n, t in zip(REPLICATED, got):
            res[kind, n] = t
        if kind == 'grad':
            total_loss = got[-1].reshape(())
    update(['w_in', 'conv_w'], _direct_wait(scatter_started[1], slots[:2], outs[0], "scatter_last_wait"))
    out = [total_loss, grad_x[None]]
    for kind in ('grad', 'delta', 'new_m', 'new_v'):
        out += [res[kind, n] for n in WEIGHTS]
    return tuple(out)


def kernel(x, mem, norm_g, w_in, conv_w, gdn_a_log, gdn_dt_bias, gdn_norm_g, s5_lambda_re, s5_lambda_im, s5_log_dt, s5_b_re, s5_b_im, s5_c_re, s5_c_im, s5_d, s5_w_glu, mem_norm_g, w_kv_mem, w_br_a, w_br_b, w_br_c, w_out, final_g, loss_target, m_norm_g, m_w_in, m_conv_w, m_gdn_a_log, m_gdn_dt_bias, m_gdn_norm_g, m_s5_lambda_re, m_s5_lambda_im, m_s5_log_dt, m_s5_b_re, m_s5_b_im, m_s5_c_re, m_s5_c_im, m_s5_d, m_s5_w_glu, m_mem_norm_g, m_w_kv_mem, m_w_br_a, m_w_br_b, m_w_br_c, m_w_out, m_final_g, v_norm_g, v_w_in, v_conv_w, v_gdn_a_log, v_gdn_dt_bias, v_gdn_norm_g, v_s5_lambda_re, v_s5_lambda_im, v_s5_log_dt, v_s5_b_re, v_s5_b_im, v_s5_c_re, v_s5_c_im, v_s5_d, v_s5_w_glu, v_mem_norm_g, v_w_kv_mem, v_w_br_a, v_w_br_b, v_w_br_c, v_w_out, v_final_g):
    a = dict(locals())
    w = {n: a[n] for n in WEIGHTS}
    m = {n: a['m_' + n] for n in WEIGHTS}
    v = {n: a['v_' + n] for n in WEIGHTS}
    return _step(x, mem, loss_target, w, m, v)
```

```python
import functools
import math

import jax
import jax.numpy as jnp
from jax import lax
from jax.experimental import pallas as pl
from jax.experimental.pallas import tpu as pltpu

F32 = jnp.float32
BF16 = jnp.bfloat16
HI = lax.Precision.HIGHEST

EPS = 1e-6
CHUNK = 64
GDN_HEAD_DIM = 128
CONV_WIDTH = 4
S5_GROUP = 16
S5_STATE = 64
S5_GROUPS_PER_BLOCK = 8
XA_HEADS = 4
N_DEV = 8
ADAM_LR, ADAM_B1, ADAM_B2, ADAM_EPS, ADAM_WD, ADAM_STEP = 0.001, 0.9, 0.999, 1e-08, 0.01, 10

VMEM_LIMIT_BYTES = 56 * 1024 * 1024
SCAN_LANES = 512
PACK_WIDTH = 512
PACK_ROWS = 256

WEIGHTS = ['norm_g', 'w_in', 'conv_w', 'gdn_a_log', 'gdn_dt_bias', 'gdn_norm_g', 's5_lambda_re', 's5_lambda_im',
           's5_log_dt', 's5_b_re', 's5_b_im', 's5_c_re', 's5_c_im', 's5_d', 's5_w_glu', 'mem_norm_g', 'w_kv_mem',
           'w_br_a', 'w_br_b', 'w_br_c', 'w_out', 'final_g']
SHARDED = {'w_in': 'col', 'conv_w': 'col', 's5_w_glu': 'col', 'w_kv_mem': 'row', 'w_br_a': 'col', 'w_br_b': 'col',
           'w_br_c': 'col', 'w_out': 'row'}
GATHER_BF16 = ['w_in', 's5_w_glu', 'w_kv_mem', 'w_br_a', 'w_br_b', 'w_br_c', 'w_out']
REPLICATED = [n for n in WEIGHTS if n not in SHARDED]
OVERLAPPED = ['s5_w_glu', 'w_kv_mem', 'w_br_a', 'w_br_b', 'w_br_c', 'w_out']


def _cparams(sem=None):
    return pltpu.CompilerParams(dimension_semantics=sem, vmem_limit_bytes=VMEM_LIMIT_BYTES)


def _pick(dim, pref):
    t = (min(pref, dim) // 128) * 128
    while t >= 128:
        if dim % t == 0:
            return t
        t -= 128
    return dim


def _make_dots(prep, precision):
    def raw(a, b, dims):
        return lax.dot_general(prep(a), prep(b), (dims, ((), ())), preferred_element_type=F32, precision=precision)

    @jax.custom_vjp
    def nn(a, b):
        return raw(a, b, ((1,), (0,)))

    @jax.custom_vjp
    def nt(a, b):
        return raw(a, b, ((1,), (1,)))

    @jax.custom_vjp
    def tn(a, b):
        return raw(a, b, ((0,), (0,)))

    nn.defvjp(lambda a, b: (nn(a, b), (a, b)), lambda r, ct: (nt(ct, r[1]), tn(r[0], ct)))
    nt.defvjp(lambda a, b: (nt(a, b), (a, b)), lambda r, ct: (nn(ct, r[1]), tn(ct, r[0])))
    tn.defvjp(lambda a, b: (tn(a, b), (a, b)), lambda r, ct: (nt(r[1], ct), nn(r[0], ct)))
    return nn, nt, tn


_bnn, _bnt, _btn = _make_dots(lambda a: a.astype(BF16), None)
_hnn, _hnt, _htn = _make_dots(lambda a: a.astype(F32), HI)
_mnn, _mnt, _mtn = _make_dots(lambda a: a.astype(F32), lax.Precision.HIGH)


def _mm(a, b, *, name, ta=False, tb=False, out_dtype=F32, addend=None, tm=512, tn=1024, tk=1024, b_shards=False,
        out_shards=0, rows=None):
    m, k = (a.shape[1], a.shape[0]) if ta else a.shape
    brows, bcols = (b.shape[1], b.shape[0] * b.shape[2]) if b_shards else b.shape
    n = brows if tb else bcols
    assert (bcols if tb else brows) == k, (a.shape, b.shape, ta, tb)
    first_row = 0
    if rows is not None:
        first_row, m = rows
    tm, tn, tk = _pick(m, tm), _pick(n, tn), _pick(k, tk)
    assert first_row % tm == 0
    r0 = first_row // tm
    bcs = ocs = 0
    if b_shards:
        bcs = b.shape[2]
        assert bcs % 128 == 0 and (tk if tb else tn) % bcs == 0
    if out_shards:
        ocs = n // out_shards
        assert ocs % 128 == 0 and tn % ocs == 0
    nk = k // tk
    dims = ((0 if ta else 1,), (1 if tb else 0,))

    def body(*refs):
        if addend is None:
            a_ref, b_ref, o_ref, acc_ref = refs
        else:
            a_ref, b_ref, add_ref, o_ref, acc_ref = refs
        kk = pl.program_id(2)

        @pl.when(kk == 0)
        def _():
            acc_ref[...] = jnp.zeros_like(acc_ref)

        dot = lambda x, y: lax.dot_general(x.astype(BF16), y.astype(BF16), (dims, ((), ())), preferred_element_type=F32)
        if not b_shards:
            acc_ref[...] += dot(a_ref[...], b_ref[...])
        elif tb:
            for g in range(tk // bcs):
                acc_ref[...] += dot(a_ref[:, g * bcs:(g + 1) * bcs], b_ref[g])
        else:
            for g in range(tn // bcs):
                acc_ref[:, g * bcs:(g + 1) * bcs] += dot(a_ref[...], b_ref[g])

        @pl.when(kk == nk - 1)
        def _():
            r = acc_ref[...]
            if addend is not None:
                r = r + add_ref[...].astype(F32)
            if out_shards:
                for g in range(tn // ocs):
                    o_ref[g] = r[:, g * ocs:(g + 1) * ocs].astype(o_ref.dtype)
            else:
                o_ref[...] = r.astype(o_ref.dtype)

    a_spec = (pl.BlockSpec((tk, tm), lambda i, j, kk: (kk, i + r0)) if ta
              else pl.BlockSpec((tm, tk), lambda i, j, kk: (i + r0, kk)))
    if b_shards:
        b_spec = (pl.BlockSpec((tk // bcs, tn, bcs), lambda i, j, kk: (kk, j, 0)) if tb
                  else pl.BlockSpec((tn // bcs, tk, bcs), lambda i, j, kk: (j, kk, 0)))
    else:
        b_spec = (pl.BlockSpec((tn, tk), lambda i, j, kk: (j, kk)) if tb
                  else pl.BlockSpec((tk, tn), lambda i, j, kk: (kk, j)))
    if out_shards:
        o_spec = pl.BlockSpec((tn // ocs, tm, ocs), lambda i, j, kk: (j, i, 0))
        out_shape = jax.ShapeDtypeStruct((out_shards, m, ocs), out_dtype)
    else:
        o_spec = pl.BlockSpec((tm, tn), lambda i, j, kk: (i, j))
        out_shape = jax.ShapeDtypeStruct((m, n), out_dtype)
    in_specs = [a_spec, b_spec] + ([o_spec] if addend is not None else [])
    args = (a, b) + ((addend,) if addend is not None else ())
    return pl.pallas_call(
        body, name=name, grid=(m // tm, n // tn, nk), in_specs=in_specs, out_specs=o_spec,
        out_shape=out_shape, scratch_shapes=[pltpu.VMEM((tm, tn), F32)],
        compiler_params=_cparams(("parallel", "parallel", "arbitrary")))(*args)


def _rt(arr, ts, cb=0, w=None):
    w = arr.shape[1] if w is None else w
    return (arr, (ts, w), lambda i, cb=cb: (i, cb))


def _whole(p):
    return pl.BlockSpec(p.shape, lambda i, nd=p.ndim: (0,) * nd)


def _tile_fwd(f, name, n, acts, params, outs):
    na, npar = len(acts), len(params)

    def body(*refs):
        res = f(*[r[...] for r in refs[:na + npar]])
        for r, v in zip(refs[na + npar:], res):
            r[...] = v.astype(r.dtype)

    in_specs = [pl.BlockSpec(b, m) for _, b, m in acts] + [_whole(p) for p in params]
    out = pl.pallas_call(
        body, name=name, grid=(n,), in_specs=in_specs,
        out_specs=[pl.BlockSpec(b, m) for _, _, b, m in outs],
        out_shape=[jax.ShapeDtypeStruct(s, d) for s, d, _, _ in outs],
        compiler_params=_cparams(("parallel",)))(*[a for a, _, _ in acts], *params)
    return out


def _tile_bwd(f, name, n, acts, params, cts, agrads, pgrads):
    na, npar, nc = len(acts), len(params), len(cts)
    adds = [g[4] for g in agrads if g is not None and len(g) == 5]

    def body(*refs):
        i = pl.program_id(0)
        ins = [r[...] for r in refs[:na + npar]]
        outs, vjp = jax.vjp(f, *ins)
        g = vjp(tuple(c[...].astype(o.dtype) for c, o in zip(refs[na + npar:na + npar + nc], outs)))
        add_refs = refs[na + npar + nc:na + npar + nc + len(adds)]
        orefs = refs[na + npar + nc + len(adds):]
        k = 0
        for j in range(na):
            if agrads[j] is not None:
                val = g[j]
                if len(agrads[j]) == 5:
                    val = val + add_refs[[id(t) for t in adds].index(id(agrads[j][4]))][...]
                orefs[k][...] = val.astype(orefs[k].dtype)
                k += 1
        for j in range(npar):
            if pgrads[j]:
                o = orefs[k]

                @pl.when(i == 0)
                def _(o=o):
                    o[...] = jnp.zeros_like(o)

                o[...] += g[na + j].astype(F32)
                k += 1

    in_specs = ([pl.BlockSpec(b, m) for _, b, m in acts] + [_whole(p) for p in params]
                + [pl.BlockSpec(b, m) for _, b, m in cts]
                + [pl.BlockSpec(g[2], g[3]) for g in agrads if g is not None and len(g) == 5])
    out_specs = [pl.BlockSpec(g[2], g[3]) for g in agrads if g is not None]
    out_shape = [jax.ShapeDtypeStruct(g[0], g[1]) for g in agrads if g is not None]
    for p, flag in zip(params, pgrads):
        if flag:
            out_specs.append(_whole(p))
            out_shape.append(jax.ShapeDtypeStruct(p.shape, F32))
    return pl.pallas_call(
        body, name=name, grid=(n,), in_specs=in_specs, out_specs=out_specs, out_shape=out_shape,
        compiler_params=_cparams(("arbitrary",)))(*[a for a, _, _ in acts], *params, *[c for c, _, _ in cts], *adds)


def _silu(x):
    return x * jax.nn.sigmoid(x)


def _rms(x, g):
    x = x.astype(F32)
    return (x * lax.rsqrt(jnp.mean(x * x, axis=-1, keepdims=True) + EPS) * g,)


def _shift_down(x, s):
    row = lax.broadcasted_iota(jnp.int32, x.shape, 0)
    return jnp.where(row >= s, pltpu.roll(x, s, 0), 0.0)


def _shift_up(x, s):
    n = x.shape[0]
    row = lax.broadcasted_iota(jnp.int32, x.shape, 0)
    return jnp.where(row < n - s, pltpu.roll(x, n - s, 0), 0.0)


@functools.partial(jax.custom_vjp, nondiff_argnums=(1,))
def _shift(x, s):
    return _shift_down(x, s)


_shift.defvjp(lambda x, s: (_shift_down(x, s), None), lambda s, _, ct: (_shift_up(ct, s),))


def _gdn_pre(mode):
    def f(x, w):
        y = x * w[CONV_WIDTH - 1:CONV_WIDTH, :]
        for j in range(CONV_WIDTH - 1):
            y = y + _shift(x, CONV_WIDTH - 1 - j) * w[j:j + 1, :]
        y = _silu(y)
        if mode != 'v':
            y = y * lax.rsqrt(jnp.sum(y * y, axis=-1, keepdims=True) + EPS)
        if mode == 'q':
            y = y * (GDN_HEAD_DIM ** -0.5)
        return (y,)
    return f


def _softplus(x):
    return jnp.maximum(x, 0.0) + jnp.log1p(jnp.exp(-jnp.abs(x)))


def _gdn_gates(ba, alog, dtb, e_beta, e_g):
    beta = jax.nn.sigmoid(ba)
    g = -jnp.exp(alog) * _softplus(ba + dtb)
    return _hnn(beta, lax.stop_gradient(e_beta)), _hnn(g, lax.stop_gradient(e_g))


@jax.custom_vjp
def _inverse_known(neg, t):
    return t


_inverse_known.defvjp(lambda neg, t: (t, t), lambda t, ct: (_mtn(t, _mnt(ct, t)), jnp.zeros_like(t)))


def _gdn_intra(q, k, v, gb, bb, t_known=None):
    n, c = len(q), q[0].shape[0]
    ri = lax.broadcasted_iota(jnp.int32, (c, c), 0)
    ci = lax.broadcasted_iota(jnp.int32, (c, c), 1)
    incl, strict = ri >= ci, ri > ci
    tri = incl.astype(F32)
    eye = (ri == ci).astype(F32)
    each = range(n)
    gc = [_hnn(tri, gb[i]) for i in each]
    decay = [jnp.exp(jnp.where(incl, gc[i][:, :c] - gc[i].T[:c, :], -1e30)) for i in each]
    kb = [k[i] * bb[i] for i in each]
    kk = [_bnt(kb[i], k[i]) for i in each]
    qk = [_bnt(q[i], k[i]) for i in each]
    p = [jnp.where(strict, -(kk[i] * decay[i]), 0.0) for i in each]
    if t_known is None:
        t = [eye + p[i] for i in each]
        for _ in range(int(math.log2(c)) - 1):
            p = [_mnn(p[i], p[i]) for i in each]
            tp = [_mnn(t[i], p[i]) for i in each]
            t = [t[i] + tp[i] for i in each]
    else:
        t = [_inverse_known(p[i], t_known[i]) for i in each]
    egc = [jnp.exp(gc[i]) for i in each]
    u_val = [_mnn(t[i], v[i] * bb[i]) for i in each]
    w_dec = [_mnn(t[i], kb[i] * egc[i]) for i in each]
    qk = [qk[i] * decay[i] for i in each]
    gl = [jnp.sum(gb[i], axis=0, keepdims=True) for i in each]
    return w_dec, u_val, qk, [q[i] * egc[i] for i in each], [k[i] * jnp.exp(gl[i] - gc[i]) for i in each], t


def _gdn_inter(w_dec, u_val, qk, q_dec, k_dec, gb, state):
    each = range(len(state))
    ws = [_bnn(w_dec[i], state[i]) for i in each]
    qs = [_bnn(q_dec[i], state[i]) for i in each]
    v_new = [u_val[i] - ws[i] for i in each]
    qv = [_bnn(qk[i], v_new[i]) for i in each]
    kv = [_btn(k_dec[i], v_new[i]) for i in each]
    decayed = [state[i] * jnp.exp(jnp.sum(gb[i], axis=0, keepdims=True)) for i in each]
    return [qs[i] + qv[i] for i in each], [decayed[i] + kv[i] for i in each]


def _gdn_post(o, z, g):
    parts = []
    for h in range(o.shape[1] // GDN_HEAD_DIM):
        oh = o[:, h * GDN_HEAD_DIM:(h + 1) * GDN_HEAD_DIM]
        parts.append(oh * lax.rsqrt(jnp.mean(oh * oh, axis=-1, keepdims=True) + EPS) * g)
    y = parts[0] if len(parts) == 1 else jnp.concatenate(parts, axis=1)
    return (y * _silu(z),)


def _gelu(x):
    return 0.5 * x * (1.0 + jnp.tanh(0.7978845608028654 * (x + 0.044715 * x * x * x)))


def _s5_post1(ylin, xb, d):
    return (_gelu(ylin + d * xb),)


def _s5_post2(t, z):
    w = t.shape[1] // 2
    return (t[:, :w] * jax.nn.sigmoid(t[:, w:]) * _silu(z),)


def _attn(q, z, kv):
    w = q.shape[1]
    hd = w // XA_HEADS
    parts = []
    for h in range(XA_HEADS):
        s = _bnt(q[:, h * hd:(h + 1) * hd], kv[:, h * hd:(h + 1) * hd]) * (hd ** -0.5)
        s = s - jnp.max(s, axis=-1, keepdims=True)
        e = jnp.exp(s)
        p = e / jnp.sum(e, axis=-1, keepdims=True)
        parts.append(_bnn(p, kv[:, w + h * hd:w + (h + 1) * hd]))
    return (jnp.concatenate(parts, axis=1) * _silu(z),)


def _merge(g0, g1, g2, pa, pb, pc):
    return (jax.nn.sigmoid(g0) * pa + jax.nn.sigmoid(g1) * pb + jax.nn.sigmoid(g2) * pc,)


def _s5_params(lr, li, logdt, br, bi, e):
    dt = jnp.exp(logdt)
    mag = jnp.exp(lr * dt)
    ab_re, ab_im = mag * jnp.cos(li * dt), mag * jnp.sin(li * dt)
    den = lr * lr + li * li
    nr, ni = ab_re - 1.0, ab_im
    e = lax.stop_gradient(e)
    cre = _hnn((nr * lr + ni * li) / den, e)
    cim = _hnn((ni * lr - nr * li) / den, e)
    return ab_re, ab_im, cre * br - cim * bi, cre * bi + cim * br


def _gdn_blocks(s, w, per_step):
    nh, nc = w // GDN_HEAD_DIM, s // CHUNK
    cpb = math.gcd(per_step, nc)
    return nh, nc, cpb, nc // cpb, (cpb * CHUNK, w), (cpb * CHUNK, nh * CHUNK)


def _gdn_pairs(cpb, nh):
    wide, narrow = [], []
    for cb in range(cpb):
        rows = slice(cb * CHUNK, (cb + 1) * CHUNK)
        for h in range(nh):
            wide.append((rows, slice(h * GDN_HEAD_DIM, (h + 1) * GDN_HEAD_DIM)))
            narrow.append((rows, slice(h * CHUNK, (h + 1) * CHUNK)))
    return wide, narrow


def _gdn_intra_fwd(q, k, v, gb, bb, per_step=4):
    s, w = q.shape
    nh, nc, cpb, n, wide, narrow = _gdn_blocks(s, w, per_step)

    def body(q_ref, k_ref, v_ref, g_ref, b_ref, wd_ref, uv_ref, qk_ref, qd_ref, kd_ref, t_ref):
        wide, narrow = _gdn_pairs(cpb, nh)
        res = _gdn_intra(*[[r[ix] for ix in wide] for r in (q_ref, k_ref, v_ref, g_ref, b_ref)])
        for ref, vals, where in zip((wd_ref, uv_ref, qk_ref, qd_ref, kd_ref, t_ref), res,
                                    (wide, wide, narrow, wide, wide, narrow)):
            for ix, val in zip(where, vals):
                ref[ix] = val

    bw = pl.BlockSpec(wide, lambda i: (i, 0))
    bn = pl.BlockSpec(narrow, lambda i: (i, 0))
    fw = jax.ShapeDtypeStruct((s, w), F32)
    fn = jax.ShapeDtypeStruct((s, nh * CHUNK), F32)
    return pl.pallas_call(
        body, name="gdn_intra", grid=(n,), in_specs=[bw] * 5, out_specs=[bw, bw, bn, bw, bw, bn],
        out_shape=[fw, fw, fn, fw, fw, fn], compiler_params=_cparams(("parallel",)))(q, k, v, gb, bb)


def _gdn_intra_bwd(q, k, v, gb, bb, t, cts, dgb_inter, per_step=4):
    s, w = q.shape
    nh, nc, cpb, n, wide, narrow = _gdn_blocks(s, w, per_step)

    def body(q_ref, k_ref, v_ref, g_ref, b_ref, t_ref, cwd, cuv, cqk, cqd, ckd, dgi, dq_ref, dk_ref, dv_ref, dg_ref,
             db_ref):
        wide, narrow = _gdn_pairs(cpb, nh)
        t_known = [t_ref[ix] for ix in narrow]
        _, vjp = jax.vjp(lambda *a: _gdn_intra(*a, t_known=t_known)[:5],
                         *[[r[ix] for ix in wide] for r in (q_ref, k_ref, v_ref, g_ref, b_ref)])
        cts = tuple([r[ix] for ix in where] for r, where in zip((cwd, cuv, cqk, cqd, ckd),
                                                               (wide, wide, narrow, wide, wide)))
        dq, dk, dv, dg, db = vjp(cts)
        for j, ix in enumerate(wide):
            dq_ref[ix], dk_ref[ix], dv_ref[ix], db_ref[ix] = dq[j], dk[j], dv[j], db[j]
            dg_ref[ix] = dg[j] + dgi[ix]

    bw = pl.BlockSpec(wide, lambda i: (i, 0))
    bn = pl.BlockSpec(narrow, lambda i: (i, 0))
    return pl.pallas_call(
        body, name="gdn_intra_bwd", grid=(n,), in_specs=[bw] * 5 + [bn, bw, bw, bn, bw, bw, bw], out_specs=[bw] * 5,
        out_shape=[jax.ShapeDtypeStruct((s, w), F32)] * 5,
        compiler_params=_cparams(("parallel",)))(q, k, v, gb, bb, t, *cts, dgb_inter)


def _gdn_inter_fwd(wd, uv, qk, qd, kd, gb, per_step=4):
    s, w = wd.shape
    nh, nc, cpb, n, wide, narrow = _gdn_blocks(s, w, per_step)
    hd = GDN_HEAD_DIM

    def body(wd_ref, uv_ref, qk_ref, qd_ref, kd_ref, g_ref, o_ref, st_ref, state):
        @pl.when(pl.program_id(0) == 0)
        def _():
            state[...] = jnp.zeros_like(state)

        wide, narrow = _gdn_pairs(cpb, nh)
        st = [state[h] for h in range(nh)]
        for cb in range(cpb):
            wi, na = wide[cb * nh:(cb + 1) * nh], narrow[cb * nh:(cb + 1) * nh]
            for h in range(nh):
                st_ref[cb, h] = st[h]
            o, st = _gdn_inter([wd_ref[ix] for ix in wi], [uv_ref[ix] for ix in wi], [qk_ref[ix] for ix in na],
                               [qd_ref[ix] for ix in wi], [kd_ref[ix] for ix in wi], [g_ref[ix] for ix in wi], st)
            for h in range(nh):
                o_ref[wi[h]] = o[h]
        for h in range(nh):
            state[h] = st[h]

    bw = pl.BlockSpec(wide, lambda i: (i, 0))
    bn = pl.BlockSpec(narrow, lambda i: (i, 0))
    return pl.pallas_call(
        body, name="gdn_inter", grid=(n,), in_specs=[bw, bw, bn, bw, bw, bw],
        out_specs=[bw, pl.BlockSpec((cpb, nh, hd, hd), lambda i: (i, 0, 0, 0))],
        out_shape=[jax.ShapeDtypeStruct((s, w), F32), jax.ShapeDtypeStruct((nc, nh, hd, hd), F32)],
        scratch_shapes=[pltpu.VMEM((nh, hd, hd), F32)],
        compiler_params=_cparams(("arbitrary",)))(wd, uv, qk, qd, kd, gb)


def _gdn_inter_bwd(wd, uv, qk, qd, kd, gb, states, do, per_step=4):
    s, w = wd.shape
    nh, nc, cpb, n, wide, narrow = _gdn_blocks(s, w, per_step)
    hd = GDN_HEAD_DIM

    def body(wd_ref, uv_ref, qk_ref, qd_ref, kd_ref, g_ref, st_ref, do_ref, cwd, cuv, cqk, cqd, ckd, dg_ref, dstate):
        @pl.when(pl.program_id(0) == 0)
        def _():
            dstate[...] = jnp.zeros_like(dstate)

        wide, narrow = _gdn_pairs(cpb, nh)
        dst = [dstate[h] for h in range(nh)]
        for cb in reversed(range(cpb)):
            wi, na = wide[cb * nh:(cb + 1) * nh], narrow[cb * nh:(cb + 1) * nh]
            _, vjp = jax.vjp(_gdn_inter, [wd_ref[ix] for ix in wi], [uv_ref[ix] for ix in wi],
                             [qk_ref[ix] for ix in na], [qd_ref[ix] for ix in wi], [kd_ref[ix] for ix in wi],
                             [g_ref[ix] for ix in wi], [st_ref[cb, h] for h in range(nh)])
            dwd, duv, dqk, dqd, dkd, dg, dst = vjp(([do_ref[ix] for ix in wi], dst))
            for h in range(nh):
                cwd[wi[h]], cuv[wi[h]], cqk[na[h]], cqd[wi[h]], ckd[wi[h]], dg_ref[wi[h]] = (
                    dwd[h], duv[h], dqk[h], dqd[h], dkd[h], dg[h])
        for h in range(nh):
            dstate[h] = dst[h]

    bw = pl.BlockSpec(wide, lambda i: (n - 1 - i, 0))
    bn = pl.BlockSpec(narrow, lambda i: (n - 1 - i, 0))
    fw = jax.ShapeDtypeStruct((s, w), F32)
    return pl.pallas_call(
        body, name="gdn_inter_bwd", grid=(n,),
        in_specs=[bw, bw, bn, bw, bw, bw, pl.BlockSpec((cpb, nh, hd, hd), lambda i: (n - 1 - i, 0, 0, 0)), bw],
        out_specs=[bw, bw, bn, bw, bw, bw],
        out_shape=[fw, fw, jax.ShapeDtypeStruct((s, nh * CHUNK), F32), fw, fw, fw],
        scratch_shapes=[pltpu.VMEM((nh, hd, hd), F32)],
        compiler_params=_cparams(("arbitrary",)))(wd, uv, qk, qd, kd, gb, states, do)


def _s5_coef(ar, ai):
    nl = ar.shape[1]

    def body(ar_ref, ai_ref, o_ref):
        row = lax.broadcasted_iota(jnp.int32, (8, nl), 0)
        for base, sign in ((0, 1.0), (8, -1.0)):
            pr = [jnp.broadcast_to(ar_ref[...], (8, nl))]
            pi = [jnp.broadcast_to(ai_ref[...], (8, nl)) * sign]
            for _ in range(7):
                pr.append(pr[-1] * pr[0] - pi[-1] * pi[0])
                pi.append(pr[-2] * pi[0] + pi[-1] * pr[0])
            for j, d in enumerate((1, 2, 4)):
                m = (row >= d) if base == 0 else (row <= 7 - d)
                o_ref[base + 2 * j] = jnp.where(m, pr[d - 1], 0.0)
                o_ref[base + 2 * j + 1] = jnp.where(m, pi[d - 1], 0.0)
            cr, ci = jnp.zeros((8, nl), F32), jnp.zeros((8, nl), F32)
            for t in range(8):
                e = t if base == 0 else 7 - t
                cr = jnp.where(row == t, pr[e], cr)
                ci = jnp.where(row == t, pi[e], ci)
            o_ref[base + 6] = cr
            o_ref[base + 7] = ci

    return pl.pallas_call(body, name="s5_coef", out_shape=jax.ShapeDtypeStruct((16, 8, nl), F32),
                          compiler_params=_cparams())(ar, ai)


def _scan_tile(src_re, src_im, dst_re, dst_im, coef_ref, carry_re, carry_im, ts, reverse, extra=None):
    nl = src_re.shape[1]
    base = 8 if reverse else 0
    ng = ts // 8
    for lc in range(nl // SCAN_LANES):
        ln = slice(lc * SCAN_LANES, (lc + 1) * SCAN_LANES)
        m = [coef_ref[base + j, :, ln] for j in range(8)]
        row = lax.broadcasted_iota(jnp.int32, (8, SCAN_LANES), 0)

        def step(r, carry, ln=ln, m=m, row=row):
            grp = (ng - 1 - r) if reverse else r
            rows = pl.ds(pl.multiple_of(grp * 8, 8), 8)
            xr, xi = src_re[rows, ln], src_im[rows, ln]
            for j, d in enumerate((1, 2, 4)):
                sh = 8 - d if reverse else d
                sr, si = pltpu.roll(xr, sh, 0), pltpu.roll(xi, sh, 0)
                mr, mi = m[2 * j], m[2 * j + 1]
                xr, xi = xr + mr * sr - mi * si, xi + mr * si + mi * sr
            cr, ci = carry[0], carry[1]
            hr = xr + m[6] * cr - m[7] * ci
            hi = xi + m[6] * ci + m[7] * cr
            dst_re[rows, ln] = hr
            dst_im[rows, ln] = hi
            edge = 0 if reverse else 7
            out = (jnp.broadcast_to(hr[edge:edge + 1, :], hr.shape), jnp.broadcast_to(hi[edge:edge + 1, :], hi.shape))
            if extra is not None:
                h_re, h_im, halo_re, halo_im, first, _, _ = extra
                prev = pl.ds(pl.multiple_of(jnp.maximum(grp - 1, 0) * 8, 8), 8)
                use_halo = grp == 0
                pr = jnp.where(use_halo, halo_re[:, ln] * first, h_re[prev, ln])
                pi = jnp.where(use_halo, halo_im[:, ln] * first, h_im[prev, ln])
                qr = jnp.where(row == 0, jnp.broadcast_to(pr[7:8, :], pr.shape), pltpu.roll(h_re[rows, ln], 1, 0))
                qi = jnp.where(row == 0, jnp.broadcast_to(pi[7:8, :], pi.shape), pltpu.roll(h_im[rows, ln], 1, 0))
                out = out + (carry[2] + hr * qr + hi * qi, carry[3] + hi * qr - hr * qi)
            return out

        init = (carry_re[:, ln], carry_im[:, ln])
        if extra is not None:
            init = init + (extra[5][:, ln], extra[6][:, ln])
        fin = lax.fori_loop(0, ng, step, init)
        carry_re[:, ln] = fin[0]
        carry_im[:, ln] = fin[1]
        if extra is not None:
            extra[5][:, ln] = fin[2]
            extra[6][:, ln] = fin[3]


def _s5_fwd(xb, bb_re, bb_im, c_re, c_im, coef, ts):
    s, w = xb.shape
    nb = bb_re.shape[0]
    nl = nb * 512

    def body(x_ref, bre_ref, bim_ref, cre_ref, cim_ref, coef_ref, hre_ref, him_ref, y_ref, ure, uim, car_re, car_im):
        @pl.when(pl.program_id(0) == 0)
        def _():
            car_re[...] = jnp.zeros_like(car_re)
            car_im[...] = jnp.zeros_like(car_im)

        for b in range(nb):
            xs = x_ref[:, b * 128:(b + 1) * 128].astype(BF16)
            ure[:, b * 512:(b + 1) * 512] = jnp.dot(xs, bre_ref[b], preferred_element_type=F32)
            uim[:, b * 512:(b + 1) * 512] = jnp.dot(xs, bim_ref[b], preferred_element_type=F32)
        _scan_tile(ure, uim, hre_ref, him_ref, coef_ref, car_re, car_im, ts, False)
        for b in range(nb):
            hr = hre_ref[:, b * 512:(b + 1) * 512].astype(BF16)
            hi = him_ref[:, b * 512:(b + 1) * 512].astype(BF16)
            y_ref[:, b * 128:(b + 1) * 128] = (jnp.dot(hr, cre_ref[b], preferred_element_type=F32)
                                               - jnp.dot(hi, cim_ref[b], preferred_element_type=F32))

    row = lambda wd: pl.BlockSpec((ts, wd), lambda i: (i, 0))
    return pl.pallas_call(
        body, name="s5_fwd", grid=(s // ts,),
        in_specs=[row(w), _whole(bb_re), _whole(bb_im), _whole(c_re), _whole(c_im), _whole(coef)],
        out_specs=[row(nl), row(nl), row(w)],
        out_shape=[jax.ShapeDtypeStruct((s, nl), F32), jax.ShapeDtypeStruct((s, nl), F32),
                   jax.ShapeDtypeStruct((s, w), F32)],
        scratch_shapes=[pltpu.VMEM((ts, nl), F32), pltpu.VMEM((ts, nl), F32), pltpu.VMEM((8, nl), F32),
                        pltpu.VMEM((8, nl), F32)],
        compiler_params=_cparams(("arbitrary",)))(xb, bb_re, bb_im, c_re, c_im, coef)


def _s5_bwd(dy, xb, h_re, h_im, bb_re, bb_im, c_re, c_im, coef, ts):
    s, w = xb.shape
    nb = bb_re.shape[0]
    nl = nb * 512
    nt = s // ts

    def body(dy_ref, x_ref, hre_ref, him_ref, halo_re, halo_im, bre_ref, bim_ref, cre_ref, cim_ref, coef_ref,
             dx_ref, dbre_ref, dbim_ref, dcre_ref, dcim_ref, dare_ref, daim_ref, gre, gim, car_re, car_im):
        i = pl.program_id(0)

        @pl.when(i == 0)
        def _():
            for r in (car_re, car_im, dbre_ref, dbim_ref, dcre_ref, dcim_ref, dare_ref, daim_ref):
                r[...] = jnp.zeros_like(r)

        for b in range(nb):
            dyb = dy_ref[:, b * 128:(b + 1) * 128].astype(BF16)
            gre[:, b * 512:(b + 1) * 512] = lax.dot_general(dyb, cre_ref[b], (((1,), (1,)), ((), ())),
                                                            preferred_element_type=F32)
            gim[:, b * 512:(b + 1) * 512] = -lax.dot_general(dyb, cim_ref[b], (((1,), (1,)), ((), ())),
                                                             preferred_element_type=F32)
            hr = hre_ref[:, b * 512:(b + 1) * 512].astype(BF16)
            hi = him_ref[:, b * 512:(b + 1) * 512].astype(BF16)
            dcre_ref[b] += lax.dot_general(hr, dyb, (((0,), (0,)), ((), ())), preferred_element_type=F32)
            dcim_ref[b] -= lax.dot_general(hi, dyb, (((0,), (0,)), ((), ())), preferred_element_type=F32)
        first = (i != nt - 1).astype(F32)
        _scan_tile(gre, gim, gre, gim, coef_ref, car_re, car_im, ts, True,
                   extra=(hre_ref, him_ref, halo_re, halo_im, first, dare_ref, daim_ref))
        for b in range(nb):
            gr = gre[:, b * 512:(b + 1) * 512].astype(BF16)
            gi = gim[:, b * 512:(b + 1) * 512].astype(BF16)
            xs = x_ref[:, b * 128:(b + 1) * 128].astype(BF16)
            dx_ref[:, b * 128:(b + 1) * 128] = (
                lax.dot_general(gr, bre_ref[b], (((1,), (1,)), ((), ())), preferred_element_type=F32)
                + lax.dot_general(gi, bim_ref[b], (((1,), (1,)), ((), ())), preferred_element_type=F32))
            dbre_ref[b] += lax.dot_general(xs, gr, (((0,), (0,)), ((), ())), preferred_element_type=F32)
            dbim_ref[b] += lax.dot_general(xs, gi, (((0,), (0,)), ((), ())), preferred_element_type=F32)

    row = lambda wd: pl.BlockSpec((ts, wd), lambda i: (nt - 1 - i, 0))
    halo = pl.BlockSpec((8, nl), lambda i: (jnp.maximum((nt - 1 - i) * (ts // 8) - 1, 0), 0))
    return pl.pallas_call(
        body, name="s5_bwd", grid=(nt,),
        in_specs=[row(w), row(w), row(nl), row(nl), halo, halo, _whole(bb_re), _whole(bb_im), _whole(c_re),
                  _whole(c_im), _whole(coef)],
        out_specs=[row(w), _whole(bb_re), _whole(bb_im), _whole(c_re), _whole(c_im),
                   pl.BlockSpec((8, nl), lambda i: (0, 0)), pl.BlockSpec((8, nl), lambda i: (0, 0))],
        out_shape=[jax.ShapeDtypeStruct((s, w), F32), jax.ShapeDtypeStruct(bb_re.shape, F32),
                   jax.ShapeDtypeStruct(bb_im.shape, F32), jax.ShapeDtypeStruct(c_re.shape, F32),
                   jax.ShapeDtypeStruct(c_im.shape, F32), jax.ShapeDtypeStruct((8, nl), F32),
                   jax.ShapeDtypeStruct((8, nl), F32)],
        scratch_shapes=[pltpu.VMEM((ts, nl), F32), pltpu.VMEM((ts, nl), F32), pltpu.VMEM((8, nl), F32),
                        pltpu.VMEM((8, nl), F32)],
        compiler_params=_cparams(("arbitrary",)))(dy, xb, h_re, h_im, h_re, h_im, bb_re, bb_im, c_re, c_im, coef)


def _final(x, mo, target, fg, ts):
    s, d = x.shape

    def f(x, mo, fg, tgt):
        y = _rms(x + mo, fg)[0]
        err = y - tgt
        return 0.5 * jnp.sum(jnp.mean(err * err, axis=-1, keepdims=True), axis=0, keepdims=True)

    def body(x_ref, mo_ref, t_ref, fg_ref, dh_ref, dfg_ref, loss_ref):
        @pl.when(pl.program_id(0) == 0)
        def _():
            dfg_ref[...] = jnp.zeros_like(dfg_ref)
            loss_ref[...] = jnp.zeros_like(loss_ref)

        loss, vjp = jax.vjp(f, x_ref[...], mo_ref[...], fg_ref[...], t_ref[...])
        _, dmo, dfg, _ = vjp(jnp.ones((1, 1), F32))
        dh_ref[...] = dmo
        dfg_ref[...] += dfg
        loss_ref[...] += jnp.broadcast_to(loss, loss_ref.shape)

    row = pl.BlockSpec((ts, d), lambda i: (i, 0))
    return pl.pallas_call(
        body, name="final", grid=(s // ts,), in_specs=[row, row, row, _whole(fg)],
        out_specs=[row, _whole(fg), pl.BlockSpec((8, 128), lambda i: (0, 0))],
        out_shape=[jax.ShapeDtypeStruct((s, d), F32), jax.ShapeDtypeStruct(fg.shape, F32),
                   jax.ShapeDtypeStruct((8, 128), F32)],
        compiler_params=_cparams(("arbitrary",)))(x, mo, target, fg)


N_CHIPS = 4


def _other_chips(x, y):
    return [((1 - x, y), 2 * (1 - x) + y), ((x, 1 - y), 2 * x + 1 - y), ((1 - x, 1 - y), 2 * (1 - x) + 1 - y)]


def _comm_call(body, name, srcs, out_shapes, n_sems):
    n = len(srcs)
    return pl.pallas_call(
        body, name=name, in_specs=[pl.BlockSpec(memory_space=pl.ANY)] * n,
        out_specs=[pl.BlockSpec(memory_space=pl.ANY)] * n, out_shape=out_shapes,
        scratch_shapes=[pltpu.SemaphoreType.DMA((n, n_sems)), pltpu.SemaphoreType.DMA((n, n_sems)),
                        pltpu.SemaphoreType.DMA((n,))],
        compiler_params=pltpu.CompilerParams(has_side_effects=True))(*srcs)


def _gather(srcs, name):
    n = len(srcs)

    def body(*refs):
        src, out = refs[:n], refs[n:2 * n]
        send_sems, recv_sems, local_sems = refs[2 * n:]
        x, y, c = lax.axis_index("x"), lax.axis_index("y"), lax.axis_index("c")
        me, sib_slot, sib = 4 * x + 2 * y + c, 4 * x + 2 * y + 1 - c, (x, y, 1 - c)
        chips = _other_chips(x, y)

        def cp(a, k, src_ref, slot, to):
            return pltpu.make_async_remote_copy(
                src_ref=src_ref, dst_ref=out[a].at[slot], send_sem=send_sems.at[a, k], recv_sem=recv_sems.at[a, k],
                device_id=to, device_id_type=pl.DeviceIdType.MESH)

        local = [pltpu.make_async_copy(src[a], out[a].at[me], local_sems.at[a]) for a in range(n)]
        first = [cp(a, 0, src[a], me, sib) for a in range(n)]
        first += [cp(a, 1 + j, src[a], me, (*chip, c)) for j, (chip, _) in enumerate(chips) for a in range(n)]
        for d in local + first:
            d.start()
        passed = []
        for j, (chip, q) in enumerate(chips):
            for a in range(n):
                cp(a, 1 + j, src[a], 2 * q + c, sib).wait_recv()
                fwd = cp(a, 4 + j, out[a].at[2 * q + c], 2 * q + c, sib)
                fwd.start()
                passed.append(fwd)
        for a in range(n):
            cp(a, 0, src[a], sib_slot, sib).wait_recv()
        for j, (chip, q) in enumerate(chips):
            for a in range(n):
                cp(a, 4 + j, src[a], 2 * q + 1 - c, sib).wait_recv()
        for d in first + passed:
            d.wait_send()
        for d in local:
            d.wait()

    return _comm_call(body, name, srcs, [jax.ShapeDtypeStruct((N_DEV,) + s.shape, s.dtype) for s in srcs], 7)


def _all_peers(x, y, c):
    out = []
    for k in range(1, N_DEV):
        px = 1 - x if k & 4 else x
        py = 1 - y if k & 2 else y
        pc = 1 - c if k & 1 else c
        out.append(((px, py, pc), 4 * px + 2 * py + pc))
    return out


_HBM = pl.BlockSpec(memory_space=pltpu.HBM)
_SEM = pl.BlockSpec(memory_space=pltpu.SEMAPHORE)
_DATAFLOW = pltpu.SideEffectType.DATAFLOW_SIDE_EFFECTING


def _send_whole(ref, slot):
    return ref


def _send_slot(ref, slot):
    return ref.at[slot]


def _direct_copies(src, land, send_sems, recv_sems, picks, arriving):
    x, y, c = lax.axis_index("x"), lax.axis_index("y"), lax.axis_index("c")
    me = 4 * x + 2 * y + c
    out = []
    for k, (pos, slot) in enumerate(_all_peers(x, y, c)):
        for a in range(len(src)):
            sem = a * (N_DEV - 1) + k
            out.append(pltpu.make_async_remote_copy(
                src_ref=picks[a](src[a], slot), dst_ref=land[a].at[slot if arriving else me],
                send_sem=send_sems.at[sem], recv_sem=recv_sems.at[sem], device_id=pos,
                device_id_type=pl.DeviceIdType.MESH))
    return out


def _direct_start(srcs, lands, picks, name):
    n = len(srcs)

    def body(*refs):
        src, land = refs[:n], refs[n:2 * n]
        send_sems, recv_sems = refs[2 * n], refs[2 * n + 1]
        for push in _direct_copies(src, land, send_sems, recv_sems, picks, False):
            push.start()
        refs[-1][...] = jnp.zeros_like(refs[-1])

    arrays = [pltpu.with_memory_space_constraint(t, pltpu.HBM) for t in list(srcs) + list(lands)]
    outs = pl.pallas_call(
        body, name=name, in_specs=[_HBM] * (2 * n),
        out_specs=(_SEM, _SEM, *[_HBM] * (2 * n), pl.BlockSpec(memory_space=pltpu.VMEM)),
        out_shape=(pltpu.SemaphoreType.DMA((n * (N_DEV - 1),)), pltpu.SemaphoreType.DMA((n * (N_DEV - 1),)),
                   *[pltpu.HBM(t.shape, t.dtype) for t in arrays], jax.ShapeDtypeStruct((8, 128), F32)),
        input_output_aliases={i: 2 + i for i in range(2 * n)},
        compiler_params=pltpu.CompilerParams(has_side_effects=_DATAFLOW))(*arrays)
    return outs[:-1], outs[-1]


def _direct_wait(started, picks, after, name):
    send_sems, recv_sems, *thru = started
    n = len(thru) // 2

    def body(*refs):
        src, land = refs[:n], refs[n:2 * n]
        for arrive in _direct_copies(src, land, refs[2 * n], refs[2 * n + 1], picks, True):
            arrive.wait_send()
            arrive.wait_recv()

    outs = pl.pallas_call(
        body, name=name, in_specs=[_HBM] * (2 * n) + [_SEM, _SEM, pl.BlockSpec(memory_space=pl.ANY)],
        out_specs=[_HBM] * (2 * n), out_shape=[pltpu.HBM(t.shape, t.dtype) for t in thru],
        input_output_aliases={i: i for i in range(2 * n)},
        compiler_params=pltpu.CompilerParams(has_side_effects=_DATAFLOW))(*thru, send_sems, recv_sems, after)
    return outs[:n], outs[n:]


def _pair_scatter(gs, name):
    n = len(gs)

    def body(*refs):
        src, out = refs[:n], refs[n:2 * n]
        send_sems, recv_sems, _ = refs[2 * n:]
        x, y, c = lax.axis_index("x"), lax.axis_index("y"), lax.axis_index("c")
        sends = []
        for q in range(N_CHIPS):
            for a in range(n):
                d = pltpu.make_async_remote_copy(
                    src_ref=src[a].at[2 * q + 1 - c], dst_ref=out[a].at[q], send_sem=send_sems.at[a, q],
                    recv_sem=recv_sems.at[a, q], device_id=(x, y, 1 - c), device_id_type=pl.DeviceIdType.MESH)
                d.start()
                sends.append(d)
        for d in sends:
            d.wait_recv()
        for d in sends:
            d.wait_send()

    return _comm_call(body, name, gs, [jax.ShapeDtypeStruct((N_CHIPS,) + g.shape[1:], g.dtype) for g in gs], N_CHIPS)


def _cross_scatter(ps, name):
    n = len(ps)

    def body(*refs):
        src, out = refs[:n], refs[n:2 * n]
        send_sems, recv_sems, local_sems = refs[2 * n:]
        x, y, c = lax.axis_index("x"), lax.axis_index("y"), lax.axis_index("c")
        mine = 2 * x + y
        chips = _other_chips(x, y)
        local = [pltpu.make_async_copy(src[a].at[mine], out[a].at[mine], local_sems.at[a]) for a in range(n)]
        for d in local:
            d.start()
        sends = []
        for j, (chip, q) in enumerate(chips):
            for a in range(n):
                d = pltpu.make_async_remote_copy(
                    src_ref=src[a].at[q], dst_ref=out[a].at[mine], send_sem=send_sems.at[a, j],
                    recv_sem=recv_sems.at[a, j], device_id=(*chip, c), device_id_type=pl.DeviceIdType.MESH)
                d.start()
                sends.append(d)
        for j, (chip, q) in enumerate(chips):
            for a in range(n):
                pltpu.make_async_remote_copy(
                    src_ref=src[a].at[q], dst_ref=out[a].at[q], send_sem=send_sems.at[a, j],
                    recv_sem=recv_sems.at[a, j], device_id=(*chip, c), device_id_type=pl.DeviceIdType.MESH).wait_recv()
        for d in sends:
            d.wait_send()
        for d in local:
            d.wait()

    return _comm_call(body, name, ps, [jax.ShapeDtypeStruct(p.shape, p.dtype) for p in ps], 3)


def _pair_sum(g, got, c_idx, out_dtype, name):
    _, r, c = g.shape
    lanes = -(-c // 128) * 128
    tr = _pick_rows(r, max(8, (2 * 1024 * 1024) // (lanes * 4)))
    g4 = g.reshape((N_CHIPS, 2) + g.shape[1:])

    def body(c_ref, g_ref, got_ref, o_ref):
        o_ref[...] = (g_ref[...] + got_ref[...]).astype(o_ref.dtype)

    return pl.pallas_call(
        body, name=name,
        grid_spec=pltpu.PrefetchScalarGridSpec(
            num_scalar_prefetch=1, grid=(N_CHIPS, r // tr),
            in_specs=[pl.BlockSpec((None, None, tr, c), lambda q, i, cr: (q, cr[0], i, 0)),
                      pl.BlockSpec((None, tr, c), lambda q, i, cr: (q, i, 0))],
            out_specs=pl.BlockSpec((None, tr, c), lambda q, i, cr: (q, i, 0))),
        out_shape=jax.ShapeDtypeStruct(got.shape, out_dtype),
        compiler_params=_cparams(("parallel", "parallel")))(c_idx, g4, got)


def _pick_rows(r, pref):
    t = (min(pref, r) // 8) * 8
    while t >= 8:
        if r % t == 0:
            return t
        t -= 8
    return r


def _w_in_from_shards(t, lo, hi):
    n, r, cs = t.shape
    tr = _pick_rows(r, 256)
    wm = n * cs - (hi - lo)

    def body(t_ref, m_ref, b_ref):
        full = jnp.concatenate([t_ref[j] for j in range(n)], axis=1)
        m_ref[...] = jnp.concatenate([full[:, :lo], full[:, hi:]], axis=1)
        b_ref[...] = jnp.concatenate([full[:, lo:hi], jnp.zeros((tr, 128 - (hi - lo)), full.dtype)], axis=1)

    return pl.pallas_call(
        body, name="w_in_layout", grid=(r // tr,), in_specs=[pl.BlockSpec((n, tr, cs), lambda i: (0, i, 0))],
        out_specs=[pl.BlockSpec((tr, wm), lambda i: (i, 0)), pl.BlockSpec((tr, 128), lambda i: (i, 0))],
        out_shape=[jax.ShapeDtypeStruct((r, wm), t.dtype), jax.ShapeDtypeStruct((r, 128), t.dtype)],
        compiler_params=_cparams(("parallel",)))(t)


def _w_in_to_shards(gm, gb, lo, hi, dtype, first_row=0, name="dw_in_layout"):
    r, wm = gm.shape
    cs = (wm + hi - lo) // N_DEV
    tr = _pick_rows(r, 64)
    assert first_row % tr == 0
    b0 = first_row // tr

    def body(m_ref, b_ref, o_ref):
        m = m_ref[...]
        full = jnp.concatenate([m[:, :lo], b_ref[:, :hi - lo], m[:, lo:]], axis=1)
        for j in range(N_DEV):
            o_ref[j] = full[:, j * cs:(j + 1) * cs].astype(o_ref.dtype)

    return pl.pallas_call(
        body, name=name, grid=(r // tr,),
        in_specs=[pl.BlockSpec((tr, wm), lambda i: (i, 0)), pl.BlockSpec((tr, 128), lambda i: (i + b0, 0))],
        out_specs=pl.BlockSpec((N_DEV, tr, cs), lambda i: (0, i, 0)),
        out_shape=jax.ShapeDtypeStruct((N_DEV, r, cs), dtype), compiler_params=_cparams(("parallel",)))(gm, gb)


def _pack(arrs, dtype, lead=()):
    nlead = len(lead)
    flat = jnp.concatenate([a.astype(dtype).reshape(lead + (-1,)) for a in arrs], axis=nlead)
    n = flat.shape[-1]
    unit = PACK_WIDTH * PACK_ROWS
    pad = (-n) % unit
    flat = jnp.pad(flat, [(0, 0)] * nlead + [(0, pad)])
    return flat.reshape(lead + ((n + pad) // PACK_WIDTH, PACK_WIDTH))


def _unpack(buf, shapes, lead=()):
    flat = buf.reshape(lead + (-1,))
    out, off = [], 0
    for shp in shapes:
        n = math.prod(shp)
        out.append(flat[..., off:off + n].reshape(lead + tuple(shp)))
        off += n
    return out


def _adam_math(w, g, m, v):
    m = ADAM_B1 * m + (1.0 - ADAM_B1) * g
    v = ADAM_B2 * v + (1.0 - ADAM_B2) * (g * g)
    m_hat = m / (1.0 - ADAM_B1 ** ADAM_STEP)
    v_hat = v / (1.0 - ADAM_B2 ** ADAM_STEP)
    delta = -ADAM_LR * (m_hat / (jnp.sqrt(v_hat) + ADAM_EPS) + ADAM_WD * w)
    return delta, m, v


def _sum_adam(parts, w, m, v, name, own=None, me=None):
    r, c = w.shape
    parts_list = list(parts) if isinstance(parts, (list, tuple)) else [parts]
    own_list = list(own) if isinstance(own, (list, tuple)) else [own]
    nparts = parts_list[0].shape[0]
    lanes = -(-c // 128) * 128
    tr = _pick_rows(r // len(parts_list), max(8, (6 * 1024 * 1024) // (nparts * lanes * 4)))

    def finish(g, w_ref, m_ref, v_ref, g_ref, d_ref, nm_ref, nv_ref):
        d, nm, nv = _adam_math(w_ref[...], g, m_ref[...], v_ref[...])
        g_ref[...] = g
        d_ref[...] = d
        nm_ref[...] = nm
        nv_ref[...] = nv

    out_shape = [jax.ShapeDtypeStruct((r, c), F32)] * 4
    if own is None:
        def body(p_ref, *rest):
            g = p_ref[0].astype(F32)
            for j in range(1, nparts):
                g = g + p_ref[j].astype(F32)
            finish(g, *rest)

        row = pl.BlockSpec((tr, c), lambda i: (i, 0))
        return pl.pallas_call(
            body, name=name, grid=(r // tr,),
            in_specs=[pl.BlockSpec((nparts, tr, c), lambda i: (0, i, 0)), row, row, row],
            out_specs=[row] * 4, out_shape=out_shape, compiler_params=_cparams(("parallel",)))(parts, w, m, v)

    nch = len(parts_list)
    tpc = r // nch // tr

    def body(me_ref, *refs):
        p_refs, o_refs, rest = refs[:nch], refs[nch:2 * nch], refs[2 * nch:]
        i = pl.program_id(0)

        def of_chunk(vals):
            out = vals[0]
            for q in range(1, nch):
                out = jnp.where(i >= q * tpc, vals[q], out)
            return out

        mine = of_chunk([o[...].astype(F32) for o in o_refs])
        g = None
        for j in range(nparts):
            t = jnp.where(me_ref[0] == j, mine, of_chunk([p[j].astype(F32) for p in p_refs]))
            g = t if g is None else g + t
        finish(g, *rest)

    row = pl.BlockSpec((tr, c), lambda i, me_ref: (i, 0))
    step = lambda i, q: jnp.clip(i - q * tpc, 0, tpc - 1)
    return pl.pallas_call(
        body, name=name,
        grid_spec=pltpu.PrefetchScalarGridSpec(
            num_scalar_prefetch=1, grid=(r // tr,),
            in_specs=[pl.BlockSpec((nparts, tr, c), lambda i, me_ref, q=q: (0, step(i, q), 0)) for q in range(nch)]
            + [pl.BlockSpec((None, tr, c), lambda i, me_ref, q=q: (me_ref[0], step(i, q), 0)) for q in range(nch)]
            + [row, row, row],
            out_specs=[row] * 4),
        out_shape=out_shape, compiler_params=_cparams(("parallel",)))(me, *parts_list, *own_list, w, m, v)


def _block_diag(t):
    nb, g, a, b = t.shape
    eye = jnp.eye(g, dtype=t.dtype)
    return jnp.einsum('ngab,gh->ngahb', t, eye).reshape(nb, g * a, g * b)


def _diag_blocks(t, a, b):
    nb = t.shape[0]
    g = S5_GROUPS_PER_BLOCK
    t = t.reshape(nb, g, a, g, b)
    return jnp.stack([t[:, j, :, j, :] for j in range(g)], axis=1)


def _local_step(x, mem, target, p, late_weights=None, early_grads=None, last_grads=None):
    s, d = x.shape
    gw = d // 2
    nh = gw // GDN_HEAD_DIM
    ng = gw // S5_GROUP
    nb = ng // S5_GROUPS_PER_BLOCK
    nl = ng * S5_STATE
    ts = min(256, s)
    nt = s // ts
    grads = {}

    w_main, w_ba = p['w_main'], p['w_ba']
    CB_ZA, CB_XB, CB_ZB, CB_QC, CB_ZC, CB_G = 3, 4, 5, 6, 7, 8

    u = _tile_fwd(_rms, "rms_fwd", nt, [_rt(x, ts)], [p['norm_g']],
                  [((s, d), BF16, (ts, d), lambda i: (i, 0))])[0]
    proj = _mm(u, w_main, tm=1024, tn=2048, tk=512, name="proj_main")
    pba = _mm(u, w_ba, name="proj_ba")
    if late_weights is not None:
        p = {**p, **late_weights(proj)}

    conv_w = p['conv_w']
    col = lambda arr, cb: (arr, (s, GDN_HEAD_DIM), lambda i, cb=cb: (0, cb + i))
    qkv = []
    for j, mode in enumerate(('q', 'k', 'v')):
        off = j * nh
        qkv.append(_tile_fwd(
            _gdn_pre(mode), "gdn_pre_" + mode, nh, [col(proj, off), (conv_w, (CONV_WIDTH, GDN_HEAD_DIM), lambda i, off=off: (0, off + i))],
            [], [((s, gw), F32, (s, GDN_HEAD_DIM), lambda i: (0, i))])[0])
    q, k, v = qkv
    lane = jnp.arange(128)[:, None]
    colh = jnp.arange(gw)[None, :] // GDN_HEAD_DIM
    e_beta = (lane == colh).astype(F32)
    e_g = (lane == colh + nh).astype(F32)
    alog_row = jnp.pad(p['gdn_a_log'], ((0, 0), (nh, 128 - 2 * nh)))
    dtb_row = jnp.pad(p['gdn_dt_bias'], ((0, 0), (nh, 128 - 2 * nh)))
    row_gw = lambda: ((s, gw), F32, (ts, gw), lambda i: (i, 0))
    betab, gb = _tile_fwd(_gdn_gates, "gdn_gates", nt, [_rt(pba, ts)], [alog_row, dtb_row, e_beta, e_g],
                          [row_gw(), row_gw()])
    *intra, t_inv = _gdn_intra_fwd(q, k, v, gb, betab)
    o_raw, states = _gdn_inter_fwd(*intra, gb)
    ga = _tile_fwd(_gdn_post, "gdn_post", nt, [_rt(o_raw, ts), _rt(proj, ts, CB_ZA, gw)], [p['gdn_norm_g']],
                   [((s, gw), BF16, (ts, gw), lambda i: (i, 0))])[0]

    e_rep = (jnp.arange(S5_STATE)[:, None] == jnp.arange(S5_STATE * S5_GROUP)[None, :] // S5_GROUP).astype(F32)
    s5_in = [p['s5_lambda_re'], p['s5_lambda_im'], p['s5_log_dt'].reshape(ng, 1),
             p['s5_b_re'].reshape(ng, S5_STATE * S5_GROUP), p['s5_b_im'].reshape(ng, S5_STATE * S5_GROUP), e_rep]
    one = lambda shp: (shp, F32, shp, lambda i, n=len(shp): (0,) * n)
    ab_re, ab_im, bbr, bbi = _tile_fwd(_s5_params, "s5_params", 1, [], s5_in,
                                       [one((ng, S5_STATE)), one((ng, S5_STATE)), one((ng, S5_STATE * S5_GROUP)),
                                        one((ng, S5_STATE * S5_GROUP))])
    coef = _s5_coef(ab_re.reshape(1, nl), ab_im.reshape(1, nl))
    to_bd_b = lambda t: _block_diag(t.reshape(nb, S5_GROUPS_PER_BLOCK, S5_STATE, S5_GROUP).transpose(0, 1, 3, 2))
    to_bd_c = lambda t: _block_diag(t.reshape(nb, S5_GROUPS_PER_BLOCK, S5_GROUP, S5_STATE).transpose(0, 1, 3, 2))
    bbd_re, bbd_im = to_bd_b(bbr).astype(BF16), to_bd_b(bbi).astype(BF16)
    cbd_re, cbd_im = to_bd_c(p['s5_c_re']).astype(BF16), to_bd_c(p['s5_c_im']).astype(BF16)
    xb_arr = lax.slice_in_dim(proj, CB_XB * gw, (CB_XB + 1) * gw, axis=1)
    h_re, h_im, ylin = _s5_fwd(xb_arr, bbd_re, bbd_im, cbd_re, cbd_im, coef, ts)
    gl = _tile_fwd(_s5_post1, "s5_post1", nt, [_rt(ylin, ts), _rt(proj, ts, CB_XB, gw)], [p['s5_d']],
                   [((s, gw), BF16, (ts, gw), lambda i: (i, 0))])[0]
    tglu = _mm(gl, p['s5_w_glu'], b_shards=True, name="s5_glu")
    gbb = _tile_fwd(_s5_post2, "s5_post2", nt, [_rt(tglu, ts), _rt(proj, ts, CB_ZB, gw)], [],
                    [((s, gw), BF16, (ts, gw), lambda i: (i, 0))])[0]

    m_len = mem.shape[0]
    mem_n = _tile_fwd(_rms, "mem_rms", 1, [_rt(mem, m_len)], [p['mem_norm_g']],
                      [((m_len, d), BF16, (m_len, d), lambda i: (i, 0))])[0]
    kv = _mm(mem_n, p['w_kv_mem'], name="mem_kv")
    gcc = _tile_fwd(_attn, "attn", nt, [_rt(proj, ts, CB_QC, gw), _rt(proj, ts, CB_ZC, gw)], [kv],
                    [((s, gw), BF16, (ts, gw), lambda i: (i, 0))])[0]

    p_a = _mm(ga, p['w_br_a'], b_shards=True, name="br_a")
    p_b = _mm(gbb, p['w_br_b'], b_shards=True, name="br_b")
    p_c = _mm(gcc, p['w_br_c'], b_shards=True, name="br_c")
    gate_acts = [_rt(proj, ts, CB_G // 2 + j, d) for j in range(3)]
    merged = _tile_fwd(_merge, "merge", nt, gate_acts + [_rt(p_a, ts), _rt(p_b, ts), _rt(p_c, ts)], [],
                       [((s, d), BF16, (ts, d), lambda i: (i, 0))])[0]
    mo = _mm(merged, p['w_out'], name="out_proj")
    dh, dfg, loss = _final(x, mo, target, p['final_g'].reshape(1, d), ts)
    grads['final_g'] = dfg.reshape(d)

    dmerged = _mm(dh, p['w_out'], tb=True, name="d_merged")
    grads['w_out'] = _mm(merged, dh, ta=True, name="dw_out")
    row_d = lambda dt: ((s, d), dt, (ts, d), lambda i: (i, 0))
    dg0, dg1, dg2, dpa, dpb, dpc = _tile_bwd(
        _merge, "merge_bwd", nt, gate_acts + [_rt(p_a, ts), _rt(p_b, ts), _rt(p_c, ts)], [], [_rt(dmerged, ts)],
        [row_d(BF16)] * 6, [])
    dga = _mm(dpa, p['w_br_a'], tb=True, b_shards=True, name="d_ga")
    dgbb = _mm(dpb, p['w_br_b'], tb=True, b_shards=True, name="d_gb")
    dgcc = _mm(dpc, p['w_br_c'], tb=True, b_shards=True, name="d_gc")
    grads['w_br_a'] = _mm(ga, dpa, ta=True, out_shards=N_DEV, name="dw_br_a")
    grads['w_br_b'] = _mm(gbb, dpb, ta=True, out_shards=N_DEV, name="dw_br_b")
    grads['w_br_c'] = _mm(gcc, dpc, ta=True, out_shards=N_DEV, name="dw_br_c")
    row_h = lambda dt: ((s, gw), dt, (ts, gw), lambda i: (i, 0))

    dqc, dzc, dkv = _tile_bwd(_attn, "attn_bwd", nt, [_rt(proj, ts, CB_QC, gw), _rt(proj, ts, CB_ZC, gw)], [kv],
                              [_rt(dgcc, ts)], [row_h(BF16), row_h(BF16)], [True])
    grads['w_kv_mem'] = _mm(mem_n, dkv, ta=True, name="dw_kv")
    dmem_n = _mm(dkv, p['w_kv_mem'], tb=True, name="d_mem_n")
    grads['mem_norm_g'] = _tile_bwd(_rms, "mem_rms_bwd", 1, [_rt(mem, m_len)], [p['mem_norm_g']],
                                    [_rt(dmem_n, m_len)], [None], [True])[0]

    dtglu, dzb = _tile_bwd(_s5_post2, "s5_post2_bwd", nt, [_rt(tglu, ts), _rt(proj, ts, CB_ZB, gw)], [],
                           [_rt(dgbb, ts)], [((s, 2 * gw), BF16, (ts, 2 * gw), lambda i: (i, 0)), row_h(BF16)], [])
    grads['s5_w_glu'] = _mm(gl, dtglu, ta=True, out_shards=N_DEV, name="dw_glu")
    s5_d = p['s5_d']
    if early_grads is not None:
        s5_d = s5_d + early_grads(grads)[:1, :1]
    dgl = _mm(dtglu, p['s5_w_glu'], tb=True, b_shards=True, name="d_gl")
    dylin, dxb1, dd = _tile_bwd(_s5_post1, "s5_post1_bwd", nt, [_rt(ylin, ts), _rt(proj, ts, CB_XB, gw)],
                                [s5_d], [_rt(dgl, ts)], [row_h(F32), row_h(F32)], [True])
    grads['s5_d'] = dd
    dxb2, dbbd_re, dbbd_im, dcbd_re, dcbd_im, da_re, da_im = _s5_bwd(dylin, xb_arr, h_re, h_im, bbd_re, bbd_im,
                                                                    cbd_re, cbd_im, coef, ts)
    from_bd_b = lambda t: _diag_blocks(t, S5_GROUP, S5_STATE).transpose(0, 1, 3, 2).reshape(ng, S5_STATE * S5_GROUP)
    from_bd_c = lambda t: _diag_blocks(t, S5_STATE, S5_GROUP).transpose(0, 1, 3, 2).reshape(1, ng, S5_GROUP, S5_STATE)
    grads['s5_c_re'], grads['s5_c_im'] = from_bd_c(dcbd_re), from_bd_c(dcbd_im)
    s5_cts = [jnp.sum(da_re, axis=0).reshape(ng, S5_STATE), jnp.sum(da_im, axis=0).reshape(ng, S5_STATE),
              from_bd_b(dbbd_re), from_bd_b(dbbd_im)]
    dlr, dli, dlogdt, dbr, dbi = _tile_bwd(_s5_params, "s5_params_bwd", 1, [], s5_in,
                                           [(c, c.shape, lambda i: (0, 0)) for c in s5_cts], [],
                                           [True, True, True, True, True, False])
    grads['s5_lambda_re'], grads['s5_lambda_im'] = dlr[None], dli[None]
    grads['s5_log_dt'] = dlogdt.reshape(1, ng)
    grads['s5_b_re'] = dbr.reshape(1, ng, S5_STATE, S5_GROUP)
    grads['s5_b_im'] = dbi.reshape(1, ng, S5_STATE, S5_GROUP)
    dxb = (dxb1 + dxb2).astype(BF16)

    do_raw, dza, dgng = _tile_bwd(_gdn_post, "gdn_post_bwd", nt, [_rt(o_raw, ts), _rt(proj, ts, CB_ZA, gw)],
                                  [p['gdn_norm_g']], [_rt(dga, ts)], [row_h(F32), row_h(BF16)], [True])
    grads['gdn_norm_g'] = dgng
    *intra_cts, dgb_inter = _gdn_inter_bwd(*intra, gb, states, do_raw)
    dq, dk, dv, dgb, dbetab = _gdn_intra_bwd(q, k, v, gb, betab, t_inv, intra_cts, dgb_inter)
    dpba, dalog, ddtb = _tile_bwd(_gdn_gates, "gdn_gates_bwd", nt, [_rt(pba, ts)], [alog_row, dtb_row, e_beta, e_g],
                                  [_rt(dbetab, ts), _rt(dgb, ts)], [((s, 128), BF16, (ts, 128), lambda i: (i, 0))],
                                  [True, True, False, False])
    grads['gdn_a_log'] = dalog[:, nh:2 * nh]
    grads['gdn_dt_bias'] = ddtb[:, nh:2 * nh]
    dqkv, dconv = [], []
    for j, (mode, ct) in enumerate((('q', dq), ('k', dk), ('v', dv))):
        off = j * nh
        wspec = (conv_w, (CONV_WIDTH, GDN_HEAD_DIM), lambda i, off=off: (0, off + i))
        dxc, dwc = _tile_bwd(
            _gdn_pre(mode), "gdn_pre_bwd_" + mode, nh, [col(proj, off), wspec], [], [col(ct, 0)],
            [((s, gw), BF16, (s, GDN_HEAD_DIM), lambda i: (0, i)),
             ((CONV_WIDTH, gw), F32, (CONV_WIDTH, GDN_HEAD_DIM), lambda i: (0, i))], [])
        dqkv.append(dxc)
        dconv.append(dwc)
    grads['conv_w'] = jnp.concatenate(dconv, axis=1)

    dproj = jnp.concatenate(dqkv + [dza, dxb, dzb, dqc, dzc, dg0, dg1, dg2], axis=1)
    grads['w_ba'] = _mm(u, dpba, ta=True, name="dw_ba")
    if last_grads is None:
        grads['w_main'] = _mm(u, dproj, ta=True, tm=1024, tn=2048, tk=512, name="dw_main")
    else:
        uu = u
        for h in range(2):
            gm = _mm(uu, dproj, ta=True, tm=1024, tn=2048, tk=512, rows=(h * (d // 2), d // 2), name=f"dw_main_{h}")
            tok = last_grads(h, gm, grads)
            uu, dpba, _ = lax.optimization_barrier((uu, dpba, tok))
    du = _mm(dpba, w_ba, tb=True, name="du_ba")
    du = _mm(dproj, w_main, tb=True, addend=du, tm=512, tn=2048, tk=1024, name="du_main")
    grad_x, dng = _tile_bwd(_rms, "rms_bwd", nt, [_rt(x, ts)], [p['norm_g']], [_rt(du, ts)],
                            [row_d(F32) + (dh,)], [True])
    grads['norm_g'] = dng
    return loss, grad_x, grads


def _to_shards(name, g):
    if SHARDED[name] == 'row':
        return g.reshape((N_DEV, g.shape[0] // N_DEV) + g.shape[1:])
    r, c = g.shape
    return g.reshape(r, N_DEV, c // N_DEV).transpose(1, 0, 2)


def _from_shards(name, t):
    if SHARDED[name] == 'row':
        return t.reshape((t.shape[0] * t.shape[1],) + t.shape[2:])
    n, r, c = t.shape
    return t.transpose(1, 0, 2).reshape(r, n * c)


def _step(x, mem, target, w, m, v):
    sharded = list(SHARDED)
    shard_shapes = {n: tuple(w[n].shape[1:]) for n in sharded}
    d = x.shape[-1]
    ba_lo = 2 * d
    ba_hi = ba_lo + 2 * (d // 2 // GDN_HEAD_DIM)

    w_in_all = _gather([w['w_in'][0].astype(BF16)], "gather_w_in")[0]
    late = [w[n][0].astype(BF16) for n in OVERLAPPED] + [w['conv_w'][0]]
    late, w_in_all = lax.optimization_barrier((late, w_in_all))
    every = [_send_whole] * len(late)
    lands = [jnp.broadcast_to(t[None], (N_DEV,) + t.shape) for t in late]
    gather_started, token = _direct_start(late, lands, every, "gather_rest_start")
    full = {}
    full['w_main'], full['w_ba'] = _w_in_from_shards(w_in_all, ba_lo, ba_hi)
    for n in REPLICATED:
        full[n] = w[n]
    for n in ('s5_lambda_re', 's5_lambda_im', 's5_c_re', 's5_c_im'):
        full[n] = w[n][0]
    full['norm_g'] = w['norm_g'] + token[:1, :1]

    def late_weights(proj):
        got = dict(zip(OVERLAPPED + ['conv_w'], _direct_wait(gather_started, every, proj, "gather_rest_wait")[1]))
        for n in ('w_kv_mem', 'w_out', 'conv_w'):
            got[n] = _from_shards(n, got[n])
        return got

    slots = [_send_slot] * len(OVERLAPPED)
    scatter_started = []

    def early_grads(grads):
        gs = [_to_shards(n, grads[n]) if SHARDED[n] == 'row' else grads[n] for n in OVERLAPPED]
        started, tok = _direct_start(gs, [lax.empty(g.shape, g.dtype) for g in gs], slots, "scatter_early_start")
        scatter_started.append(started)
        return tok

    def last_grads(h, gm, grads):
        gs = [_w_in_to_shards(gm, grads['w_ba'], ba_lo, ba_hi, BF16, h * gm.shape[0], f"dw_in_layout_{h}")]
        if h == 0:
            gs.append(_to_shards('conv_w', grads['conv_w']))
        started, tok = _direct_start(gs, [lax.empty(g.shape, g.dtype) for g in gs], slots[:len(gs)],
                                     f"scatter_last_start_{h}")
        scatter_started.append(started)
        return tok

    loss, grad_x, grads = _local_step(x[0], mem[0], target[0], full, late_weights, early_grads, last_grads)
    res = {}
    me = (4 * lax.axis_index("x") + 2 * lax.axis_index("y") + lax.axis_index("c")).astype(jnp.int32).reshape(1)

    def update(names, exchanged):
        for n, own, part in zip(names, *exchanged):
            outs = _sum_adam(part, w[n][0], m[n][0], v[n][0], name="adam_" + n, own=own, me=me)
            for kind, t in zip(('grad', 'delta', 'new_m', 'new_v'), outs):
                res[kind, n] = t[None]

    update(OVERLAPPED, _direct_wait(scatter_started[0], slots, grad_x, "scatter_early_wait"))

    small = _pack([grads[n].reshape(w[n].shape) for n in REPLICATED] + [loss[:1, :1]], F32)
    allp = _gather([small], "gather_small")[0]
    zero = jnp.zeros((1, 1), F32)
    outs = _sum_adam(allp, *[_pack([t[n] for n in REPLICATED] + [zero], F32) for t in (w, m, v)], name="adam_small")
    shapes = [w[n].shape for n in REPLICATED] + [(1, 1)]
    for kind, buf in zip(('grad', 'delta', 'new_m', 'new_v'), outs):
        got = _unpack(buf, shapes)
        for n, t in zip(REPLICATED, got):
            res[kind, n] = t
        if kind == 'grad':
            total_loss = got[-1].reshape(())
    (own0, own_conv), (got0, got_conv) = _direct_wait(scatter_started[1], slots[:2], outs[0], "scatter_last_wait_0")
    (own1,), (got1,) = _direct_wait(scatter_started[2], slots[:1], outs[0], "scatter_last_wait_1")
    update(['w_in', 'conv_w'], ([[own0, own1], own_conv], [[got0, got1], got_conv]))
    out = [total_loss, grad_x[None]]
    for kind in ('grad', 'delta', 'new_m', 'new_v'):
        out += [res[kind, n] for n in WEIGHTS]
    return tuple(out)


def kernel(x, mem, norm_g, w_in, conv_w, gdn_a_log, gdn_dt_bias, gdn_norm_g, s5_lambda_re, s5_lambda_im, s5_log_dt, s5_b_re, s5_b_im, s5_c_re, s5_c_im, s5_d, s5_w_glu, mem_norm_g, w_kv_mem, w_br_a, w_br_b, w_br_c, w_out, final_g, loss_target, m_norm_g, m_w_in, m_conv_w, m_gdn_a_log, m_gdn_dt_bias, m_gdn_norm_g, m_s5_lambda_re, m_s5_lambda_im, m_s5_log_dt, m_s5_b_re, m_s5_b_im, m_s5_c_re, m_s5_c_im, m_s5_d, m_s5_w_glu, m_mem_norm_g, m_w_kv_mem, m_w_br_a, m_w_br_b, m_w_br_c, m_w_out, m_final_g, v_norm_g, v_w_in, v_conv_w, v_gdn_a_log, v_gdn_dt_bias, v_gdn_norm_g, v_s5_lambda_re, v_s5_lambda_im, v_s5_log_dt, v_s5_b_re, v_s5_b_im, v_s5_c_re, v_s5_c_im, v_s5_d, v_s5_w_glu, v_mem_norm_g, v_w_kv_mem, v_w_br_a, v_w_br_b, v_w_br_c, v_w_out, v_final_g):
    a = dict(locals())
    w = {n: a[n] for n in WEIGHTS}
    m = {n: a['m_' + n] for n in WEIGHTS}
    v = {n: a['v_' + n] for n in WEIGHTS}
    return _step(x, mem, loss_target, w, m, v)
```

```python
import functools
import math

import jax
import jax.numpy as jnp
from jax import lax
from jax.experimental import pallas as pl
from jax.experimental.pallas import tpu as pltpu

F32 = jnp.float32
BF16 = jnp.bfloat16
HI = lax.Precision.HIGHEST

EPS = 1e-6
CHUNK = 64
GDN_HEAD_DIM = 128
CONV_WIDTH = 4
S5_GROUP = 16
S5_STATE = 64
S5_GROUPS_PER_BLOCK = 8
XA_HEADS = 4
N_DEV = 8
ADAM_LR, ADAM_B1, ADAM_B2, ADAM_EPS, ADAM_WD, ADAM_STEP = 0.001, 0.9, 0.999, 1e-08, 0.01, 10

VMEM_LIMIT_BYTES = 56 * 1024 * 1024
SCAN_LANES = 512
PACK_WIDTH = 512
PACK_ROWS = 256

WEIGHTS = ['norm_g', 'w_in', 'conv_w', 'gdn_a_log', 'gdn_dt_bias', 'gdn_norm_g', 's5_lambda_re', 's5_lambda_im',
           's5_log_dt', 's5_b_re', 's5_b_im', 's5_c_re', 's5_c_im', 's5_d', 's5_w_glu', 'mem_norm_g', 'w_kv_mem',
           'w_br_a', 'w_br_b', 'w_br_c', 'w_out', 'final_g']
SHARDED = {'w_in': 'col', 'conv_w': 'col', 's5_w_glu': 'col', 'w_kv_mem': 'row', 'w_br_a': 'col', 'w_br_b': 'col',
           'w_br_c': 'col', 'w_out': 'row'}
GATHER_BF16 = ['w_in', 's5_w_glu', 'w_kv_mem', 'w_br_a', 'w_br_b', 'w_br_c', 'w_out']
REPLICATED = [n for n in WEIGHTS if n not in SHARDED]
OVERLAPPED = ['s5_w_glu', 'w_kv_mem', 'w_br_a', 'w_br_b', 'w_br_c', 'w_out']


def _cparams(sem=None):
    return pltpu.CompilerParams(dimension_semantics=sem, vmem_limit_bytes=VMEM_LIMIT_BYTES)


def _pick(dim, pref):
    t = (min(pref, dim) // 128) * 128
    while t >= 128:
        if dim % t == 0:
            return t
        t -= 128
    return dim


def _make_dots(prep, precision):
    def raw(a, b, dims):
        return lax.dot_general(prep(a), prep(b), (dims, ((), ())), preferred_element_type=F32, precision=precision)

    @jax.custom_vjp
    def nn(a, b):
        return raw(a, b, ((1,), (0,)))

    @jax.custom_vjp
    def nt(a, b):
        return raw(a, b, ((1,), (1,)))

    @jax.custom_vjp
    def tn(a, b):
        return raw(a, b, ((0,), (0,)))

    nn.defvjp(lambda a, b: (nn(a, b), (a, b)), lambda r, ct: (nt(ct, r[1]), tn(r[0], ct)))
    nt.defvjp(lambda a, b: (nt(a, b), (a, b)), lambda r, ct: (nn(ct, r[1]), tn(ct, r[0])))
    tn.defvjp(lambda a, b: (tn(a, b), (a, b)), lambda r, ct: (nt(r[1], ct), nn(r[0], ct)))
    return nn, nt, tn


_bnn, _bnt, _btn = _make_dots(lambda a: a.astype(BF16), None)
_hnn, _hnt, _htn = _make_dots(lambda a: a.astype(F32), HI)
_mnn, _mnt, _mtn = _make_dots(lambda a: a.astype(F32), lax.Precision.HIGH)


def _mm(a, b, *, name, ta=False, tb=False, out_dtype=F32, addend=None, tm=512, tn=1024, tk=1024, b_shards=False,
        out_shards=0, rows=None):
    m, k = (a.shape[1], a.shape[0]) if ta else a.shape
    brows, bcols = (b.shape[1], b.shape[0] * b.shape[2]) if b_shards else b.shape
    n = brows if tb else bcols
    assert (bcols if tb else brows) == k, (a.shape, b.shape, ta, tb)
    first_row = 0
    if rows is not None:
        first_row, m = rows
    tm, tn, tk = _pick(m, tm), _pick(n, tn), _pick(k, tk)
    assert first_row % tm == 0
    r0 = first_row // tm
    bcs = ocs = 0
    if b_shards:
        bcs = b.shape[2]
        assert bcs % 128 == 0 and (tk if tb else tn) % bcs == 0
    if out_shards:
        ocs = n // out_shards
        assert ocs % 128 == 0 and tn % ocs == 0
    nk = k // tk
    dims = ((0 if ta else 1,), (1 if tb else 0,))

    def body(*refs):
        if addend is None:
            a_ref, b_ref, o_ref, acc_ref = refs
        else:
            a_ref, b_ref, add_ref, o_ref, acc_ref = refs
        kk = pl.program_id(2)

        @pl.when(kk == 0)
        def _():
            acc_ref[...] = jnp.zeros_like(acc_ref)

        dot = lambda x, y: lax.dot_general(x.astype(BF16), y.astype(BF16), (dims, ((), ())), preferred_element_type=F32)
        if not b_shards:
            acc_ref[...] += dot(a_ref[...], b_ref[...])
        elif tb:
            for g in range(tk // bcs):
                acc_ref[...] += dot(a_ref[:, g * bcs:(g + 1) * bcs], b_ref[g])
        else:
            for g in range(tn // bcs):
                acc_ref[:, g * bcs:(g + 1) * bcs] += dot(a_ref[...], b_ref[g])

        @pl.when(kk == nk - 1)
        def _():
            r = acc_ref[...]
            if addend is not None:
                r = r + add_ref[...].astype(F32)
            if out_shards:
                for g in range(tn // ocs):
                    o_ref[g] = r[:, g * ocs:(g + 1) * ocs].astype(o_ref.dtype)
            else:
                o_ref[...] = r.astype(o_ref.dtype)

    a_spec = (pl.BlockSpec((tk, tm), lambda i, j, kk: (kk, i + r0)) if ta
              else pl.BlockSpec((tm, tk), lambda i, j, kk: (i + r0, kk)))
    if b_shards:
        b_spec = (pl.BlockSpec((tk // bcs, tn, bcs), lambda i, j, kk: (kk, j, 0)) if tb
                  else pl.BlockSpec((tn // bcs, tk, bcs), lambda i, j, kk: (j, kk, 0)))
    else:
        b_spec = (pl.BlockSpec((tn, tk), lambda i, j, kk: (j, kk)) if tb
                  else pl.BlockSpec((tk, tn), lambda i, j, kk: (kk, j)))
    if out_shards:
        o_spec = pl.BlockSpec((tn // ocs, tm, ocs), lambda i, j, kk: (j, i, 0))
        out_shape = jax.ShapeDtypeStruct((out_shards, m, ocs), out_dtype)
    else:
        o_spec = pl.BlockSpec((tm, tn), lambda i, j, kk: (i, j))
        out_shape = jax.ShapeDtypeStruct((m, n), out_dtype)
    in_specs = [a_spec, b_spec] + ([o_spec] if addend is not None else [])
    args = (a, b) + ((addend,) if addend is not None else ())
    return pl.pallas_call(
        body, name=name, grid=(m // tm, n // tn, nk), in_specs=in_specs, out_specs=o_spec,
        out_shape=out_shape, scratch_shapes=[pltpu.VMEM((tm, tn), F32)],
        compiler_params=_cparams(("parallel", "parallel", "arbitrary")))(*args)


def _rt(arr, ts, cb=0, w=None):
    w = arr.shape[1] if w is None else w
    return (arr, (ts, w), lambda i, cb=cb: (i, cb))


def _whole(p):
    return pl.BlockSpec(p.shape, lambda i, nd=p.ndim: (0,) * nd)


def _tile_fwd(f, name, n, acts, params, outs):
    na, npar = len(acts), len(params)

    def body(*refs):
        res = f(*[r[...] for r in refs[:na + npar]])
        for r, v in zip(refs[na + npar:], res):
            r[...] = v.astype(r.dtype)

    in_specs = [pl.BlockSpec(b, m) for _, b, m in acts] + [_whole(p) for p in params]
    out = pl.pallas_call(
        body, name=name, grid=(n,), in_specs=in_specs,
        out_specs=[pl.BlockSpec(b, m) for _, _, b, m in outs],
        out_shape=[jax.ShapeDtypeStruct(s, d) for s, d, _, _ in outs],
        compiler_params=_cparams(("parallel",)))(*[a for a, _, _ in acts], *params)
    return out


def _tile_bwd(f, name, n, acts, params, cts, agrads, pgrads):
    na, npar, nc = len(acts), len(params), len(cts)
    adds = [g[4] for g in agrads if g is not None and len(g) == 5]

    def body(*refs):
        i = pl.program_id(0)
        ins = [r[...] for r in refs[:na + npar]]
        outs, vjp = jax.vjp(f, *ins)
        g = vjp(tuple(c[...].astype(o.dtype) for c, o in zip(refs[na + npar:na + npar + nc], outs)))
        add_refs = refs[na + npar + nc:na + npar + nc + len(adds)]
        orefs = refs[na + npar + nc + len(adds):]
        k = 0
        for j in range(na):
            if agrads[j] is not None:
                val = g[j]
                if len(agrads[j]) == 5:
                    val = val + add_refs[[id(t) for t in adds].index(id(agrads[j][4]))][...]
                orefs[k][...] = val.astype(orefs[k].dtype)
                k += 1
        for j in range(npar):
            if pgrads[j]:
                o = orefs[k]

                @pl.when(i == 0)
                def _(o=o):
                    o[...] = jnp.zeros_like(o)

                o[...] += g[na + j].astype(F32)
                k += 1

    in_specs = ([pl.BlockSpec(b, m) for _, b, m in acts] + [_whole(p) for p in params]
                + [pl.BlockSpec(b, m) for _, b, m in cts]
                + [pl.BlockSpec(g[2], g[3]) for g in agrads if g is not None and len(g) == 5])
    out_specs = [pl.BlockSpec(g[2], g[3]) for g in agrads if g is not None]
    out_shape = [jax.ShapeDtypeStruct(g[0], g[1]) for g in agrads if g is not None]
    for p, flag in zip(params, pgrads):
        if flag:
            out_specs.append(_whole(p))
            out_shape.append(jax.ShapeDtypeStruct(p.shape, F32))
    return pl.pallas_call(
        body, name=name, grid=(n,), in_specs=in_specs, out_specs=out_specs, out_shape=out_shape,
        compiler_params=_cparams(("arbitrary",)))(*[a for a, _, _ in acts], *params, *[c for c, _, _ in cts], *adds)


def _silu(x):
    return x * jax.nn.sigmoid(x)


def _rms(x, g):
    x = x.astype(F32)
    return (x * lax.rsqrt(jnp.mean(x * x, axis=-1, keepdims=True) + EPS) * g,)


def _shift_down(x, s):
    row = lax.broadcasted_iota(jnp.int32, x.shape, 0)
    return jnp.where(row >= s, pltpu.roll(x, s, 0), 0.0)


def _shift_up(x, s):
    n = x.shape[0]
    row = lax.broadcasted_iota(jnp.int32, x.shape, 0)
    return jnp.where(row < n - s, pltpu.roll(x, n - s, 0), 0.0)


@functools.partial(jax.custom_vjp, nondiff_argnums=(1,))
def _shift(x, s):
    return _shift_down(x, s)


_shift.defvjp(lambda x, s: (_shift_down(x, s), None), lambda s, _, ct: (_shift_up(ct, s),))


def _gdn_pre(mode):
    def f(x, w):
        y = x * w[CONV_WIDTH - 1:CONV_WIDTH, :]
        for j in range(CONV_WIDTH - 1):
            y = y + _shift(x, CONV_WIDTH - 1 - j) * w[j:j + 1, :]
        y = _silu(y)
        if mode != 'v':
            y = y * lax.rsqrt(jnp.sum(y * y, axis=-1, keepdims=True) + EPS)
        if mode == 'q':
            y = y * (GDN_HEAD_DIM ** -0.5)
        return (y,)
    return f


def _softplus(x):
    return jnp.maximum(x, 0.0) + jnp.log1p(jnp.exp(-jnp.abs(x)))


def _gdn_gates(ba, alog, dtb, e_beta, e_g):
    beta = jax.nn.sigmoid(ba)
    g = -jnp.exp(alog) * _softplus(ba + dtb)
    return _hnn(beta, lax.stop_gradient(e_beta)), _hnn(g, lax.stop_gradient(e_g))


@jax.custom_vjp
def _inverse_known(neg, t):
    return t


_inverse_known.defvjp(lambda neg, t: (t, t), lambda t, ct: (_mtn(t, _mnt(ct, t)), jnp.zeros_like(t)))


def _gdn_intra(q, k, v, gb, bb, t_known=None):
    n, c = len(q), q[0].shape[0]
    ri = lax.broadcasted_iota(jnp.int32, (c, c), 0)
    ci = lax.broadcasted_iota(jnp.int32, (c, c), 1)
    incl, strict = ri >= ci, ri > ci
    tri = incl.astype(F32)
    eye = (ri == ci).astype(F32)
    each = range(n)
    gc = [_hnn(tri, gb[i]) for i in each]
    decay = [jnp.exp(jnp.where(incl, gc[i][:, :c] - gc[i].T[:c, :], -1e30)) for i in each]
    kb = [k[i] * bb[i] for i in each]
    kk = [_bnt(kb[i], k[i]) for i in each]
    qk = [_bnt(q[i], k[i]) for i in each]
    p = [jnp.where(strict, -(kk[i] * decay[i]), 0.0) for i in each]
    if t_known is None:
        t = [eye + p[i] for i in each]
        for _ in range(int(math.log2(c)) - 1):
            p = [_mnn(p[i], p[i]) for i in each]
            tp = [_mnn(t[i], p[i]) for i in each]
            t = [t[i] + tp[i] for i in each]
    else:
        t = [_inverse_known(p[i], t_known[i]) for i in each]
    egc = [jnp.exp(gc[i]) for i in each]
    u_val = [_mnn(t[i], v[i] * bb[i]) for i in each]
    w_dec = [_mnn(t[i], kb[i] * egc[i]) for i in each]
    qk = [qk[i] * decay[i] for i in each]
    gl = [jnp.sum(gb[i], axis=0, keepdims=True) for i in each]
    return w_dec, u_val, qk, [q[i] * egc[i] for i in each], [k[i] * jnp.exp(gl[i] - gc[i]) for i in each], t


def _gdn_inter(w_dec, u_val, qk, q_dec, k_dec, gb, state):
    each = range(len(state))
    ws = [_bnn(w_dec[i], state[i]) for i in each]
    qs = [_bnn(q_dec[i], state[i]) for i in each]
    v_new = [u_val[i] - ws[i] for i in each]
    qv = [_bnn(qk[i], v_new[i]) for i in each]
    kv = [_btn(k_dec[i], v_new[i]) for i in each]
    decayed = [state[i] * jnp.exp(jnp.sum(gb[i], axis=0, keepdims=True)) for i in each]
    return [qs[i] + qv[i] for i in each], [decayed[i] + kv[i] for i in each]


def _gdn_post(o, z, g):
    parts = []
    for h in range(o.shape[1] // GDN_HEAD_DIM):
        oh = o[:, h * GDN_HEAD_DIM:(h + 1) * GDN_HEAD_DIM]
        parts.append(oh * lax.rsqrt(jnp.mean(oh * oh, axis=-1, keepdims=True) + EPS) * g)
    y = parts[0] if len(parts) == 1 else jnp.concatenate(parts, axis=1)
    return (y * _silu(z),)


def _gelu(x):
    return 0.5 * x * (1.0 + jnp.tanh(0.7978845608028654 * (x + 0.044715 * x * x * x)))


def _s5_post1(ylin, xb, d):
    return (_gelu(ylin + d * xb),)


def _s5_post2(t, z):
    w = t.shape[1] // 2
    return (t[:, :w] * jax.nn.sigmoid(t[:, w:]) * _silu(z),)


def _attn(q, z, kv):
    w = q.shape[1]
    hd = w // XA_HEADS
    parts = []
    for h in range(XA_HEADS):
        s = _bnt(q[:, h * hd:(h + 1) * hd], kv[:, h * hd:(h + 1) * hd]) * (hd ** -0.5)
        s = s - jnp.max(s, axis=-1, keepdims=True)
        e = jnp.exp(s)
        p = e / jnp.sum(e, axis=-1, keepdims=True)
        parts.append(_bnn(p, kv[:, w + h * hd:w + (h + 1) * hd]))
    return (jnp.concatenate(parts, axis=1) * _silu(z),)


def _merge(g0, g1, g2, pa, pb, pc):
    return (jax.nn.sigmoid(g0) * pa + jax.nn.sigmoid(g1) * pb + jax.nn.sigmoid(g2) * pc,)


def _s5_params(lr, li, logdt, br, bi, e):
    dt = jnp.exp(logdt)
    mag = jnp.exp(lr * dt)
    ab_re, ab_im = mag * jnp.cos(li * dt), mag * jnp.sin(li * dt)
    den = lr * lr + li * li
    nr, ni = ab_re - 1.0, ab_im
    e = lax.stop_gradient(e)
    cre = _hnn((nr * lr + ni * li) / den, e)
    cim = _hnn((ni * lr - nr * li) / den, e)
    return ab_re, ab_im, cre * br - cim * bi, cre * bi + cim * br


def _gdn_blocks(s, w, per_step):
    nh, nc = w // GDN_HEAD_DIM, s // CHUNK
    cpb = math.gcd(per_step, nc)
    return nh, nc, cpb, nc // cpb, (cpb * CHUNK, w), (cpb * CHUNK, nh * CHUNK)


def _gdn_pairs(cpb, nh):
    wide, narrow = [], []
    for cb in range(cpb):
        rows = slice(cb * CHUNK, (cb + 1) * CHUNK)
        for h in range(nh):
            wide.append((rows, slice(h * GDN_HEAD_DIM, (h + 1) * GDN_HEAD_DIM)))
            narrow.append((rows, slice(h * CHUNK, (h + 1) * CHUNK)))
    return wide, narrow


def _gdn_intra_fwd(q, k, v, gb, bb, per_step=4):
    s, w = q.shape
    nh, nc, cpb, n, wide, narrow = _gdn_blocks(s, w, per_step)

    def body(q_ref, k_ref, v_ref, g_ref, b_ref, wd_ref, uv_ref, qk_ref, qd_ref, kd_ref, t_ref):
        wide, narrow = _gdn_pairs(cpb, nh)
        res = _gdn_intra(*[[r[ix] for ix in wide] for r in (q_ref, k_ref, v_ref, g_ref, b_ref)])
        for ref, vals, where in zip((wd_ref, uv_ref, qk_ref, qd_ref, kd_ref, t_ref), res,
                                    (wide, wide, narrow, wide, wide, narrow)):
            for ix, val in zip(where, vals):
                ref[ix] = val

    bw = pl.BlockSpec(wide, lambda i: (i, 0))
    bn = pl.BlockSpec(narrow, lambda i: (i, 0))
    fw = jax.ShapeDtypeStruct((s, w), F32)
    fn = jax.ShapeDtypeStruct((s, nh * CHUNK), F32)
    return pl.pallas_call(
        body, name="gdn_intra", grid=(n,), in_specs=[bw] * 5, out_specs=[bw, bw, bn, bw, bw, bn],
        out_shape=[fw, fw, fn, fw, fw, fn], compiler_params=_cparams(("parallel",)))(q, k, v, gb, bb)


def _gdn_intra_bwd(q, k, v, gb, bb, t, cts, dgb_inter, per_step=4):
    s, w = q.shape
    nh, nc, cpb, n, wide, narrow = _gdn_blocks(s, w, per_step)

    def body(q_ref, k_ref, v_ref, g_ref, b_ref, t_ref, cwd, cuv, cqk, cqd, ckd, dgi, dq_ref, dk_ref, dv_ref, dg_ref,
             db_ref):
        wide, narrow = _gdn_pairs(cpb, nh)
        t_known = [t_ref[ix] for ix in narrow]
        _, vjp = jax.vjp(lambda *a: _gdn_intra(*a, t_known=t_known)[:5],
                         *[[r[ix] for ix in wide] for r in (q_ref, k_ref, v_ref, g_ref, b_ref)])
        cts = tuple([r[ix] for ix in where] for r, where in zip((cwd, cuv, cqk, cqd, ckd),
                                                               (wide, wide, narrow, wide, wide)))
        dq, dk, dv, dg, db = vjp(cts)
        for j, ix in enumerate(wide):
            dq_ref[ix], dk_ref[ix], dv_ref[ix], db_ref[ix] = dq[j], dk[j], dv[j], db[j]
            dg_ref[ix] = dg[j] + dgi[ix]

    bw = pl.BlockSpec(wide, lambda i: (i, 0))
    bn = pl.BlockSpec(narrow, lambda i: (i, 0))
    return pl.pallas_call(
        body, name="gdn_intra_bwd", grid=(n,), in_specs=[bw] * 5 + [bn, bw, bw, bn, bw, bw, bw], out_specs=[bw] * 5,
        out_shape=[jax.ShapeDtypeStruct((s, w), F32)] * 5,
        compiler_params=_cparams(("parallel",)))(q, k, v, gb, bb, t, *cts, dgb_inter)


def _gdn_inter_fwd(wd, uv, qk, qd, kd, gb, per_step=4):
    s, w = wd.shape
    nh, nc, cpb, n, wide, narrow = _gdn_blocks(s, w, per_step)
    hd = GDN_HEAD_DIM

    def body(wd_ref, uv_ref, qk_ref, qd_ref, kd_ref, g_ref, o_ref, st_ref, state):
        @pl.when(pl.program_id(0) == 0)
        def _():
            state[...] = jnp.zeros_like(state)

        wide, narrow = _gdn_pairs(cpb, nh)
        st = [state[h] for h in range(nh)]
        for cb in range(cpb):
            wi, na = wide[cb * nh:(cb + 1) * nh], narrow[cb * nh:(cb + 1) * nh]
            for h in range(nh):
                st_ref[cb, h] = st[h]
            o, st = _gdn_inter([wd_ref[ix] for ix in wi], [uv_ref[ix] for ix in wi], [qk_ref[ix] for ix in na],
                               [qd_ref[ix] for ix in wi], [kd_ref[ix] for ix in wi], [g_ref[ix] for ix in wi], st)
            for h in range(nh):
                o_ref[wi[h]] = o[h]
        for h in range(nh):
            state[h] = st[h]

    bw = pl.BlockSpec(wide, lambda i: (i, 0))
    bn = pl.BlockSpec(narrow, lambda i: (i, 0))
    return pl.pallas_call(
        body, name="gdn_inter", grid=(n,), in_specs=[bw, bw, bn, bw, bw, bw],
        out_specs=[bw, pl.BlockSpec((cpb, nh, hd, hd), lambda i: (i, 0, 0, 0))],
        out_shape=[jax.ShapeDtypeStruct((s, w), F32), jax.ShapeDtypeStruct((nc, nh, hd, hd), F32)],
        scratch_shapes=[pltpu.VMEM((nh, hd, hd), F32)],
        compiler_params=_cparams(("arbitrary",)))(wd, uv, qk, qd, kd, gb)


def _gdn_inter_bwd(wd, uv, qk, qd, kd, gb, states, do, per_step=4):
    s, w = wd.shape
    nh, nc, cpb, n, wide, narrow = _gdn_blocks(s, w, per_step)
    hd = GDN_HEAD_DIM

    def body(wd_ref, uv_ref, qk_ref, qd_ref, kd_ref, g_ref, st_ref, do_ref, cwd, cuv, cqk, cqd, ckd, dg_ref, dstate):
        @pl.when(pl.program_id(0) == 0)
        def _():
            dstate[...] = jnp.zeros_like(dstate)

        wide, narrow = _gdn_pairs(cpb, nh)
        dst = [dstate[h] for h in range(nh)]
        for cb in reversed(range(cpb)):
            wi, na = wide[cb * nh:(cb + 1) * nh], narrow[cb * nh:(cb + 1) * nh]
            _, vjp = jax.vjp(_gdn_inter, [wd_ref[ix] for ix in wi], [uv_ref[ix] for ix in wi],
                             [qk_ref[ix] for ix in na], [qd_ref[ix] for ix in wi], [kd_ref[ix] for ix in wi],
                             [g_ref[ix] for ix in wi], [st_ref[cb, h] for h in range(nh)])
            dwd, duv, dqk, dqd, dkd, dg, dst = vjp(([do_ref[ix] for ix in wi], dst))
            for h in range(nh):
                cwd[wi[h]], cuv[wi[h]], cqk[na[h]], cqd[wi[h]], ckd[wi[h]], dg_ref[wi[h]] = (
                    dwd[h], duv[h], dqk[h], dqd[h], dkd[h], dg[h])
        for h in range(nh):
            dstate[h] = dst[h]

    bw = pl.BlockSpec(wide, lambda i: (n - 1 - i, 0))
    bn = pl.BlockSpec(narrow, lambda i: (n - 1 - i, 0))
    fw = jax.ShapeDtypeStruct((s, w), F32)
    return pl.pallas_call(
        body, name="gdn_inter_bwd", grid=(n,),
        in_specs=[bw, bw, bn, bw, bw, bw, pl.BlockSpec((cpb, nh, hd, hd), lambda i: (n - 1 - i, 0, 0, 0)), bw],
        out_specs=[bw, bw, bn, bw, bw, bw],
        out_shape=[fw, fw, jax.ShapeDtypeStruct((s, nh * CHUNK), F32), fw, fw, fw],
        scratch_shapes=[pltpu.VMEM((nh, hd, hd), F32)],
        compiler_params=_cparams(("arbitrary",)))(wd, uv, qk, qd, kd, gb, states, do)


def _s5_coef(ar, ai):
    nl = ar.shape[1]

    def body(ar_ref, ai_ref, o_ref):
        row = lax.broadcasted_iota(jnp.int32, (8, nl), 0)
        for base, sign in ((0, 1.0), (8, -1.0)):
            pr = [jnp.broadcast_to(ar_ref[...], (8, nl))]
            pi = [jnp.broadcast_to(ai_ref[...], (8, nl)) * sign]
            for _ in range(7):
                pr.append(pr[-1] * pr[0] - pi[-1] * pi[0])
                pi.append(pr[-2] * pi[0] + pi[-1] * pr[0])
            for j, d in enumerate((1, 2, 4)):
                m = (row >= d) if base == 0 else (row <= 7 - d)
                o_ref[base + 2 * j] = jnp.where(m, pr[d - 1], 0.0)
                o_ref[base + 2 * j + 1] = jnp.where(m, pi[d - 1], 0.0)
            cr, ci = jnp.zeros((8, nl), F32), jnp.zeros((8, nl), F32)
            for t in range(8):
                e = t if base == 0 else 7 - t
                cr = jnp.where(row == t, pr[e], cr)
                ci = jnp.where(row == t, pi[e], ci)
            o_ref[base + 6] = cr
            o_ref[base + 7] = ci

    return pl.pallas_call(body, name="s5_coef", out_shape=jax.ShapeDtypeStruct((16, 8, nl), F32),
                          compiler_params=_cparams())(ar, ai)


def _scan_tile(src_re, src_im, dst_re, dst_im, coef_ref, carry_re, carry_im, ts, reverse, extra=None):
    nl = src_re.shape[1]
    base = 8 if reverse else 0
    ng = ts // 8
    for lc in range(nl // SCAN_LANES):
        ln = slice(lc * SCAN_LANES, (lc + 1) * SCAN_LANES)
        m = [coef_ref[base + j, :, ln] for j in range(8)]
        row = lax.broadcasted_iota(jnp.int32, (8, SCAN_LANES), 0)

        def step(r, carry, ln=ln, m=m, row=row):
            grp = (ng - 1 - r) if reverse else r
            rows = pl.ds(pl.multiple_of(grp * 8, 8), 8)
            xr, xi = src_re[rows, ln], src_im[rows, ln]
            for j, d in enumerate((1, 2, 4)):
                sh = 8 - d if reverse else d
                sr, si = pltpu.roll(xr, sh, 0), pltpu.roll(xi, sh, 0)
                mr, mi = m[2 * j], m[2 * j + 1]
                xr, xi = xr + mr * sr - mi * si, xi + mr * si + mi * sr
            cr, ci = carry[0], carry[1]
            hr = xr + m[6] * cr - m[7] * ci
            hi = xi + m[6] * ci + m[7] * cr
            dst_re[rows, ln] = hr
            dst_im[rows, ln] = hi
            edge = 0 if reverse else 7
            out = (jnp.broadcast_to(hr[edge:edge + 1, :], hr.shape), jnp.broadcast_to(hi[edge:edge + 1, :], hi.shape))
            if extra is not None:
                h_re, h_im, halo_re, halo_im, first, _, _ = extra
                prev = pl.ds(pl.multiple_of(jnp.maximum(grp - 1, 0) * 8, 8), 8)
                use_halo = grp == 0
                pr = jnp.where(use_halo, halo_re[:, ln] * first, h_re[prev, ln])
                pi = jnp.where(use_halo, halo_im[:, ln] * first, h_im[prev, ln])
                qr = jnp.where(row == 0, jnp.broadcast_to(pr[7:8, :], pr.shape), pltpu.roll(h_re[rows, ln], 1, 0))
                qi = jnp.where(row == 0, jnp.broadcast_to(pi[7:8, :], pi.shape), pltpu.roll(h_im[rows, ln], 1, 0))
                out = out + (carry[2] + hr * qr + hi * qi, carry[3] + hi * qr - hr * qi)
            return out

        init = (carry_re[:, ln], carry_im[:, ln])
        if extra is not None:
            init = init + (extra[5][:, ln], extra[6][:, ln])
        fin = lax.fori_loop(0, ng, step, init)
        carry_re[:, ln] = fin[0]
        carry_im[:, ln] = fin[1]
        if extra is not None:
            extra[5][:, ln] = fin[2]
            extra[6][:, ln] = fin[3]


def _s5_fwd(xb, bb_re, bb_im, c_re, c_im, coef, ts):
    s, w = xb.shape
    nb = bb_re.shape[0]
    nl = nb * 512

    def body(x_ref, bre_ref, bim_ref, cre_ref, cim_ref, coef_ref, hre_ref, him_ref, y_ref, ure, uim, car_re, car_im):
        @pl.when(pl.program_id(0) == 0)
        def _():
            car_re[...] = jnp.zeros_like(car_re)
            car_im[...] = jnp.zeros_like(car_im)

        for b in range(nb):
            xs = x_ref[:, b * 128:(b + 1) * 128].astype(BF16)
            ure[:, b * 512:(b + 1) * 512] = jnp.dot(xs, bre_ref[b], preferred_element_type=F32)
            uim[:, b * 512:(b + 1) * 512] = jnp.dot(xs, bim_ref[b], preferred_element_type=F32)
        _scan_tile(ure, uim, hre_ref, him_ref, coef_ref, car_re, car_im, ts, False)
        for b in range(nb):
            hr = hre_ref[:, b * 512:(b + 1) * 512].astype(BF16)
            hi = him_ref[:, b * 512:(b + 1) * 512].astype(BF16)
            y_ref[:, b * 128:(b + 1) * 128] = (jnp.dot(hr, cre_ref[b], preferred_element_type=F32)
                                               - jnp.dot(hi, cim_ref[b], preferred_element_type=F32))

    row = lambda wd: pl.BlockSpec((ts, wd), lambda i: (i, 0))
    return pl.pallas_call(
        body, name="s5_fwd", grid=(s // ts,),
        in_specs=[row(w), _whole(bb_re), _whole(bb_im), _whole(c_re), _whole(c_im), _whole(coef)],
        out_specs=[row(nl), row(nl), row(w)],
        out_shape=[jax.ShapeDtypeStruct((s, nl), F32), jax.ShapeDtypeStruct((s, nl), F32),
                   jax.ShapeDtypeStruct((s, w), F32)],
        scratch_shapes=[pltpu.VMEM((ts, nl), F32), pltpu.VMEM((ts, nl), F32), pltpu.VMEM((8, nl), F32),
                        pltpu.VMEM((8, nl), F32)],
        compiler_params=_cparams(("arbitrary",)))(xb, bb_re, bb_im, c_re, c_im, coef)


def _s5_bwd(dy, xb, h_re, h_im, bb_re, bb_im, c_re, c_im, coef, ts):
    s, w = xb.shape
    nb = bb_re.shape[0]
    nl = nb * 512
    nt = s // ts

    def body(dy_ref, x_ref, hre_ref, him_ref, halo_re, halo_im, bre_ref, bim_ref, cre_ref, cim_ref, coef_ref,
             dx_ref, dbre_ref, dbim_ref, dcre_ref, dcim_ref, dare_ref, daim_ref, gre, gim, car_re, car_im):
        i = pl.program_id(0)

        @pl.when(i == 0)
        def _():
            for r in (car_re, car_im, dbre_ref, dbim_ref, dcre_ref, dcim_ref, dare_ref, daim_ref):
                r[...] = jnp.zeros_like(r)

        for b in range(nb):
            dyb = dy_ref[:, b * 128:(b + 1) * 128].astype(BF16)
            gre[:, b * 512:(b + 1) * 512] = lax.dot_general(dyb, cre_ref[b], (((1,), (1,)), ((), ())),
                                                            preferred_element_type=F32)
            gim[:, b * 512:(b + 1) * 512] = -lax.dot_general(dyb, cim_ref[b], (((1,), (1,)), ((), ())),
                                                             preferred_element_type=F32)
            hr = hre_ref[:, b * 512:(b + 1) * 512].astype(BF16)
            hi = him_ref[:, b * 512:(b + 1) * 512].astype(BF16)
            dcre_ref[b] += lax.dot_general(hr, dyb, (((0,), (0,)), ((), ())), preferred_element_type=F32)
            dcim_ref[b] -= lax.dot_general(hi, dyb, (((0,), (0,)), ((), ())), preferred_element_type=F32)
        first = (i != nt - 1).astype(F32)
        _scan_tile(gre, gim, gre, gim, coef_ref, car_re, car_im, ts, True,
                   extra=(hre_ref, him_ref, halo_re, halo_im, first, dare_ref, daim_ref))
        for b in range(nb):
            gr = gre[:, b * 512:(b + 1) * 512].astype(BF16)
            gi = gim[:, b * 512:(b + 1) * 512].astype(BF16)
            xs = x_ref[:, b * 128:(b + 1) * 128].astype(BF16)
            dx_ref[:, b * 128:(b + 1) * 128] = (
                lax.dot_general(gr, bre_ref[b], (((1,), (1,)), ((), ())), preferred_element_type=F32)
                + lax.dot_general(gi, bim_ref[b], (((1,), (1,)), ((), ())), preferred_element_type=F32))
            dbre_ref[b] += lax.dot_general(xs, gr, (((0,), (0,)), ((), ())), preferred_element_type=F32)
            dbim_ref[b] += lax.dot_general(xs, gi, (((0,), (0,)), ((), ())), preferred_element_type=F32)

    row = lambda wd: pl.BlockSpec((ts, wd), lambda i: (nt - 1 - i, 0))
    halo = pl.BlockSpec((8, nl), lambda i: (jnp.maximum((nt - 1 - i) * (ts // 8) - 1, 0), 0))
    return pl.pallas_call(
        body, name="s5_bwd", grid=(nt,),
        in_specs=[row(w), row(w), row(nl), row(nl), halo, halo, _whole(bb_re), _whole(bb_im), _whole(c_re),
                  _whole(c_im), _whole(coef)],
        out_specs=[row(w), _whole(bb_re), _whole(bb_im), _whole(c_re), _whole(c_im),
                   pl.BlockSpec((8, nl), lambda i: (0, 0)), pl.BlockSpec((8, nl), lambda i: (0, 0))],
        out_shape=[jax.ShapeDtypeStruct((s, w), F32), jax.ShapeDtypeStruct(bb_re.shape, F32),
                   jax.ShapeDtypeStruct(bb_im.shape, F32), jax.ShapeDtypeStruct(c_re.shape, F32),
                   jax.ShapeDtypeStruct(c_im.shape, F32), jax.ShapeDtypeStruct((8, nl), F32),
                   jax.ShapeDtypeStruct((8, nl), F32)],
        scratch_shapes=[pltpu.VMEM((ts, nl), F32), pltpu.VMEM((ts, nl), F32), pltpu.VMEM((8, nl), F32),
                        pltpu.VMEM((8, nl), F32)],
        compiler_params=_cparams(("arbitrary",)))(dy, xb, h_re, h_im, h_re, h_im, bb_re, bb_im, c_re, c_im, coef)


def _final(x, mo, target, fg, ts):
    s, d = x.shape

    def f(x, mo, fg, tgt):
        y = _rms(x + mo, fg)[0]
        err = y - tgt
        return 0.5 * jnp.sum(jnp.mean(err * err, axis=-1, keepdims=True), axis=0, keepdims=True)

    def body(x_ref, mo_ref, t_ref, fg_ref, dh_ref, dfg_ref, loss_ref):
        @pl.when(pl.program_id(0) == 0)
        def _():
            dfg_ref[...] = jnp.zeros_like(dfg_ref)
            loss_ref[...] = jnp.zeros_like(loss_ref)

        loss, vjp = jax.vjp(f, x_ref[...], mo_ref[...], fg_ref[...], t_ref[...])
        _, dmo, dfg, _ = vjp(jnp.ones((1, 1), F32))
        dh_ref[...] = dmo
        dfg_ref[...] += dfg
        loss_ref[...] += jnp.broadcast_to(loss, loss_ref.shape)

    row = pl.BlockSpec((ts, d), lambda i: (i, 0))
    return pl.pallas_call(
        body, name="final", grid=(s // ts,), in_specs=[row, row, row, _whole(fg)],
        out_specs=[row, _whole(fg), pl.BlockSpec((8, 128), lambda i: (0, 0))],
        out_shape=[jax.ShapeDtypeStruct((s, d), F32), jax.ShapeDtypeStruct(fg.shape, F32),
                   jax.ShapeDtypeStruct((8, 128), F32)],
        compiler_params=_cparams(("arbitrary",)))(x, mo, target, fg)


N_CHIPS = 4


def _other_chips(x, y):
    return [((1 - x, y), 2 * (1 - x) + y), ((x, 1 - y), 2 * x + 1 - y), ((1 - x, 1 - y), 2 * (1 - x) + 1 - y)]


def _comm_call(body, name, srcs, out_shapes, n_sems):
    n = len(srcs)
    return pl.pallas_call(
        body, name=name, in_specs=[pl.BlockSpec(memory_space=pl.ANY)] * n,
        out_specs=[pl.BlockSpec(memory_space=pl.ANY)] * n, out_shape=out_shapes,
        scratch_shapes=[pltpu.SemaphoreType.DMA((n, n_sems)), pltpu.SemaphoreType.DMA((n, n_sems)),
                        pltpu.SemaphoreType.DMA((n,))],
        compiler_params=pltpu.CompilerParams(has_side_effects=True))(*srcs)


def _gather(srcs, name):
    n = len(srcs)

    def body(*refs):
        src, out = refs[:n], refs[n:2 * n]
        send_sems, recv_sems, local_sems = refs[2 * n:]
        x, y, c = lax.axis_index("x"), lax.axis_index("y"), lax.axis_index("c")
        me, sib_slot, sib = 4 * x + 2 * y + c, 4 * x + 2 * y + 1 - c, (x, y, 1 - c)
        chips = _other_chips(x, y)

        def cp(a, k, src_ref, slot, to):
            return pltpu.make_async_remote_copy(
                src_ref=src_ref, dst_ref=out[a].at[slot], send_sem=send_sems.at[a, k], recv_sem=recv_sems.at[a, k],
                device_id=to, device_id_type=pl.DeviceIdType.MESH)

        local = [pltpu.make_async_copy(src[a], out[a].at[me], local_sems.at[a]) for a in range(n)]
        first = [cp(a, 0, src[a], me, sib) for a in range(n)]
        first += [cp(a, 1 + j, src[a], me, (*chip, c)) for j, (chip, _) in enumerate(chips) for a in range(n)]
        for d in local + first:
            d.start()
        passed = []
        for j, (chip, q) in enumerate(chips):
            for a in range(n):
                cp(a, 1 + j, src[a], 2 * q + c, sib).wait_recv()
                fwd = cp(a, 4 + j, out[a].at[2 * q + c], 2 * q + c, sib)
                fwd.start()
                passed.append(fwd)
        for a in range(n):
            cp(a, 0, src[a], sib_slot, sib).wait_recv()
        for j, (chip, q) in enumerate(chips):
            for a in range(n):
                cp(a, 4 + j, src[a], 2 * q + 1 - c, sib).wait_recv()
        for d in first + passed:
            d.wait_send()
        for d in local:
            d.wait()

    return _comm_call(body, name, srcs, [jax.ShapeDtypeStruct((N_DEV,) + s.shape, s.dtype) for s in srcs], 7)


def _all_peers(x, y, c):
    out = []
    for k in range(1, N_DEV):
        px = 1 - x if k & 4 else x
        py = 1 - y if k & 2 else y
        pc = 1 - c if k & 1 else c
        out.append(((px, py, pc), 4 * px + 2 * py + pc))
    return out


_HBM = pl.BlockSpec(memory_space=pltpu.HBM)
_SEM = pl.BlockSpec(memory_space=pltpu.SEMAPHORE)
_DATAFLOW = pltpu.SideEffectType.DATAFLOW_SIDE_EFFECTING


def _send_whole(ref, slot):
    return ref


def _send_slot(ref, slot):
    return ref.at[slot]


def _direct_copies(src, land, send_sems, recv_sems, picks, arriving):
    x, y, c = lax.axis_index("x"), lax.axis_index("y"), lax.axis_index("c")
    me = 4 * x + 2 * y + c
    out = []
    for k, (pos, slot) in enumerate(_all_peers(x, y, c)):
        for a in range(len(src)):
            sem = a * (N_DEV - 1) + k
            out.append(pltpu.make_async_remote_copy(
                src_ref=picks[a](src[a], slot), dst_ref=land[a].at[slot if arriving else me],
                send_sem=send_sems.at[sem], recv_sem=recv_sems.at[sem], device_id=pos,
                device_id_type=pl.DeviceIdType.MESH))
    return out


def _direct_start(srcs, lands, picks, name):
    n = len(srcs)

    def body(*refs):
        src, land = refs[:n], refs[n:2 * n]
        send_sems, recv_sems = refs[2 * n], refs[2 * n + 1]
        for push in _direct_copies(src, land, send_sems, recv_sems, picks, False):
            push.start()
        refs[-1][...] = jnp.zeros_like(refs[-1])

    arrays = [pltpu.with_memory_space_constraint(t, pltpu.HBM) for t in list(srcs) + list(lands)]
    outs = pl.pallas_call(
        body, name=name, in_specs=[_HBM] * (2 * n),
        out_specs=(_SEM, _SEM, *[_HBM] * (2 * n), pl.BlockSpec(memory_space=pltpu.VMEM)),
        out_shape=(pltpu.SemaphoreType.DMA((n * (N_DEV - 1),)), pltpu.SemaphoreType.DMA((n * (N_DEV - 1),)),
                   *[pltpu.HBM(t.shape, t.dtype) for t in arrays], jax.ShapeDtypeStruct((8, 128), F32)),
        input_output_aliases={i: 2 + i for i in range(2 * n)},
        compiler_params=pltpu.CompilerParams(has_side_effects=_DATAFLOW))(*arrays)
    return outs[:-1], outs[-1]


def _direct_wait(started, picks, after, name):
    send_sems, recv_sems, *thru = started
    n = len(thru) // 2

    def body(*refs):
        src, land = refs[:n], refs[n:2 * n]
        for arrive in _direct_copies(src, land, refs[2 * n], refs[2 * n + 1], picks, True):
            arrive.wait_send()
            arrive.wait_recv()

    outs = pl.pallas_call(
        body, name=name, in_specs=[_HBM] * (2 * n) + [_SEM, _SEM, pl.BlockSpec(memory_space=pl.ANY)],
        out_specs=[_HBM] * (2 * n), out_shape=[pltpu.HBM(t.shape, t.dtype) for t in thru],
        input_output_aliases={i: i for i in range(2 * n)},
        compiler_params=pltpu.CompilerParams(has_side_effects=_DATAFLOW))(*thru, send_sems, recv_sems, after)
    return outs[:n], outs[n:]


def _pair_scatter(gs, name):
    n = len(gs)

    def body(*refs):
        src, out = refs[:n], refs[n:2 * n]
        send_sems, recv_sems, _ = refs[2 * n:]
        x, y, c = lax.axis_index("x"), lax.axis_index("y"), lax.axis_index("c")
        sends = []
        for q in range(N_CHIPS):
            for a in range(n):
                d = pltpu.make_async_remote_copy(
                    src_ref=src[a].at[2 * q + 1 - c], dst_ref=out[a].at[q], send_sem=send_sems.at[a, q],
                    recv_sem=recv_sems.at[a, q], device_id=(x, y, 1 - c), device_id_type=pl.DeviceIdType.MESH)
                d.start()
                sends.append(d)
        for d in sends:
            d.wait_recv()
        for d in sends:
            d.wait_send()

    return _comm_call(body, name, gs, [jax.ShapeDtypeStruct((N_CHIPS,) + g.shape[1:], g.dtype) for g in gs], N_CHIPS)


def _cross_scatter(ps, name):
    n = len(ps)

    def body(*refs):
        src, out = refs[:n], refs[n:2 * n]
        send_sems, recv_sems, local_sems = refs[2 * n:]
        x, y, c = lax.axis_index("x"), lax.axis_index("y"), lax.axis_index("c")
        mine = 2 * x + y
        chips = _other_chips(x, y)
        local = [pltpu.make_async_copy(src[a].at[mine], out[a].at[mine], local_sems.at[a]) for a in range(n)]
        for d in local:
            d.start()
        sends = []
        for j, (chip, q) in enumerate(chips):
            for a in range(n):
                d = pltpu.make_async_remote_copy(
                    src_ref=src[a].at[q], dst_ref=out[a].at[mine], send_sem=send_sems.at[a, j],
                    recv_sem=recv_sems.at[a, j], device_id=(*chip, c), device_id_type=pl.DeviceIdType.MESH)
                d.start()
                sends.append(d)
        for j, (chip, q) in enumerate(chips):
            for a in range(n):
                pltpu.make_async_remote_copy(
                    src_ref=src[a].at[q], dst_ref=out[a].at[q], send_sem=send_sems.at[a, j],
                    recv_sem=recv_sems.at[a, j], device_id=(*chip, c), device_id_type=pl.DeviceIdType.MESH).wait_recv()
        for d in sends:
            d.wait_send()
        for d in local:
            d.wait()

    return _comm_call(body, name, ps, [jax.ShapeDtypeStruct(p.shape, p.dtype) for p in ps], 3)


def _pair_sum(g, got, c_idx, out_dtype, name):
    _, r, c = g.shape
    lanes = -(-c // 128) * 128
    tr = _pick_rows(r, max(8, (2 * 1024 * 1024) // (lanes * 4)))
    g4 = g.reshape((N_CHIPS, 2) + g.shape[1:])

    def body(c_ref, g_ref, got_ref, o_ref):
        o_ref[...] = (g_ref[...] + got_ref[...]).astype(o_ref.dtype)

    return pl.pallas_call(
        body, name=name,
        grid_spec=pltpu.PrefetchScalarGridSpec(
            num_scalar_prefetch=1, grid=(N_CHIPS, r // tr),
            in_specs=[pl.BlockSpec((None, None, tr, c), lambda q, i, cr: (q, cr[0], i, 0)),
                      pl.BlockSpec((None, tr, c), lambda q, i, cr: (q, i, 0))],
            out_specs=pl.BlockSpec((None, tr, c), lambda q, i, cr: (q, i, 0))),
        out_shape=jax.ShapeDtypeStruct(got.shape, out_dtype),
        compiler_params=_cparams(("parallel", "parallel")))(c_idx, g4, got)


def _pick_rows(r, pref):
    t = (min(pref, r) // 8) * 8
    while t >= 8:
        if r % t == 0:
            return t
        t -= 8
    return r


def _w_in_from_shards(t, lo, hi):
    n, r, cs = t.shape
    tr = _pick_rows(r, 256)
    wm = n * cs - (hi - lo)

    def body(t_ref, m_ref, b_ref):
        full = jnp.concatenate([t_ref[j] for j in range(n)], axis=1)
        m_ref[...] = jnp.concatenate([full[:, :lo], full[:, hi:]], axis=1)
        b_ref[...] = jnp.concatenate([full[:, lo:hi], jnp.zeros((tr, 128 - (hi - lo)), full.dtype)], axis=1)

    return pl.pallas_call(
        body, name="w_in_layout", grid=(r // tr,), in_specs=[pl.BlockSpec((n, tr, cs), lambda i: (0, i, 0))],
        out_specs=[pl.BlockSpec((tr, wm), lambda i: (i, 0)), pl.BlockSpec((tr, 128), lambda i: (i, 0))],
        out_shape=[jax.ShapeDtypeStruct((r, wm), t.dtype), jax.ShapeDtypeStruct((r, 128), t.dtype)],
        compiler_params=_cparams(("parallel",)))(t)


def _w_in_to_shards(gm, gb, lo, hi, dtype, first_row=0, name="dw_in_layout"):
    r, wm = gm.shape
    cs = (wm + hi - lo) // N_DEV
    tr = _pick_rows(r, 64)
    assert first_row % tr == 0
    b0 = first_row // tr

    def body(m_ref, b_ref, o_ref):
        m = m_ref[...]
        full = jnp.concatenate([m[:, :lo], b_ref[:, :hi - lo], m[:, lo:]], axis=1)
        for j in range(N_DEV):
            o_ref[j] = full[:, j * cs:(j + 1) * cs].astype(o_ref.dtype)

    return pl.pallas_call(
        body, name=name, grid=(r // tr,),
        in_specs=[pl.BlockSpec((tr, wm), lambda i: (i, 0)), pl.BlockSpec((tr, 128), lambda i: (i + b0, 0))],
        out_specs=pl.BlockSpec((N_DEV, tr, cs), lambda i: (0, i, 0)),
        out_shape=jax.ShapeDtypeStruct((N_DEV, r, cs), dtype), compiler_params=_cparams(("parallel",)))(gm, gb)


def _pack(arrs, dtype, lead=()):
    nlead = len(lead)
    flat = jnp.concatenate([a.astype(dtype).reshape(lead + (-1,)) for a in arrs], axis=nlead)
    n = flat.shape[-1]
    unit = PACK_WIDTH * PACK_ROWS
    pad = (-n) % unit
    flat = jnp.pad(flat, [(0, 0)] * nlead + [(0, pad)])
    return flat.reshape(lead + ((n + pad) // PACK_WIDTH, PACK_WIDTH))


def _unpack(buf, shapes, lead=()):
    flat = buf.reshape(lead + (-1,))
    out, off = [], 0
    for shp in shapes:
        n = math.prod(shp)
        out.append(flat[..., off:off + n].reshape(lead + tuple(shp)))
        off += n
    return out


def _adam_math(w, g, m, v):
    m = ADAM_B1 * m + (1.0 - ADAM_B1) * g
    v = ADAM_B2 * v + (1.0 - ADAM_B2) * (g * g)
    m_hat = m / (1.0 - ADAM_B1 ** ADAM_STEP)
    v_hat = v / (1.0 - ADAM_B2 ** ADAM_STEP)
    delta = -ADAM_LR * (m_hat / (jnp.sqrt(v_hat) + ADAM_EPS) + ADAM_WD * w)
    return delta, m, v


def _sum_adam(parts, w, m, v, name, own=None, me=None):
    r, c = w.shape
    parts_list = list(parts) if isinstance(parts, (list, tuple)) else [parts]
    own_list = list(own) if isinstance(own, (list, tuple)) else [own]
    nparts = parts_list[0].shape[0]
    lanes = -(-c // 128) * 128
    tr = _pick_rows(r // len(parts_list), max(8, (6 * 1024 * 1024) // (nparts * lanes * 4)))

    def finish(g, w_ref, m_ref, v_ref, g_ref, d_ref, nm_ref, nv_ref):
        d, nm, nv = _adam_math(w_ref[...], g, m_ref[...], v_ref[...])
        g_ref[...] = g
        d_ref[...] = d
        nm_ref[...] = nm
        nv_ref[...] = nv

    out_shape = [jax.ShapeDtypeStruct((r, c), F32)] * 4
    if own is None:
        def body(p_ref, *rest):
            g = p_ref[0].astype(F32)
            for j in range(1, nparts):
                g = g + p_ref[j].astype(F32)
            finish(g, *rest)

        row = pl.BlockSpec((tr, c), lambda i: (i, 0))
        return pl.pallas_call(
            body, name=name, grid=(r // tr,),
            in_specs=[pl.BlockSpec((nparts, tr, c), lambda i: (0, i, 0)), row, row, row],
            out_specs=[row] * 4, out_shape=out_shape, compiler_params=_cparams(("parallel",)))(parts, w, m, v)

    nch = len(parts_list)
    tpc = r // nch // tr

    def body(me_ref, *refs):
        p_refs, o_refs, rest = refs[:nch], refs[nch:2 * nch], refs[2 * nch:]
        i = pl.program_id(0)

        def of_chunk(vals):
            out = vals[0]
            for q in range(1, nch):
                out = jnp.where(i >= q * tpc, vals[q], out)
            return out

        mine = of_chunk([o[...].astype(F32) for o in o_refs])
        g = None
        for j in range(nparts):
            t = jnp.where(me_ref[0] == j, mine, of_chunk([p[j].astype(F32) for p in p_refs]))
            g = t if g is None else g + t
        finish(g, *rest)

    row = pl.BlockSpec((tr, c), lambda i, me_ref: (i, 0))
    step = lambda i, q: jnp.clip(i - q * tpc, 0, tpc - 1)
    return pl.pallas_call(
        body, name=name,
        grid_spec=pltpu.PrefetchScalarGridSpec(
            num_scalar_prefetch=1, grid=(r // tr,),
            in_specs=[pl.BlockSpec((nparts, tr, c), lambda i, me_ref, q=q: (0, step(i, q), 0)) for q in range(nch)]
            + [pl.BlockSpec((None, tr, c), lambda i, me_ref, q=q: (me_ref[0], step(i, q), 0)) for q in range(nch)]
            + [row, row, row],
            out_specs=[row] * 4),
        out_shape=out_shape, compiler_params=_cparams(("parallel",)))(me, *parts_list, *own_list, w, m, v)


def _block_diag(t):
    nb, g, a, b = t.shape
    eye = jnp.eye(g, dtype=t.dtype)
    return jnp.einsum('ngab,gh->ngahb', t, eye).reshape(nb, g * a, g * b)


def _diag_blocks(t, a, b):
    nb = t.shape[0]
    g = S5_GROUPS_PER_BLOCK
    t = t.reshape(nb, g, a, g, b)
    return jnp.stack([t[:, j, :, j, :] for j in range(g)], axis=1)


def _local_step(x, mem, target, p, late_weights=None, early_grads=None, last_grads=None):
    s, d = x.shape
    gw = d // 2
    nh = gw // GDN_HEAD_DIM
    ng = gw // S5_GROUP
    nb = ng // S5_GROUPS_PER_BLOCK
    nl = ng * S5_STATE
    ts = min(256, s)
    nt = s // ts
    grads = {}

    w_main, w_ba = p['w_main'], p['w_ba']
    CB_ZA, CB_XB, CB_ZB, CB_QC, CB_ZC, CB_G = 3, 4, 5, 6, 7, 8

    u = _tile_fwd(_rms, "rms_fwd", nt, [_rt(x, ts)], [p['norm_g']],
                  [((s, d), BF16, (ts, d), lambda i: (i, 0))])[0]
    proj = _mm(u, w_main, tm=1024, tn=2048, tk=512, name="proj_main")
    pba = _mm(u, w_ba, name="proj_ba")
    if late_weights is not None:
        p = {**p, **late_weights(proj)}

    conv_w = p['conv_w']
    col = lambda arr, cb: (arr, (s, GDN_HEAD_DIM), lambda i, cb=cb: (0, cb + i))
    qkv = []
    for j, mode in enumerate(('q', 'k', 'v')):
        off = j * nh
        qkv.append(_tile_fwd(
            _gdn_pre(mode), "gdn_pre_" + mode, nh, [col(proj, off), (conv_w, (CONV_WIDTH, GDN_HEAD_DIM), lambda i, off=off: (0, off + i))],
            [], [((s, gw), F32, (s, GDN_HEAD_DIM), lambda i: (0, i))])[0])
    q, k, v = qkv
    lane = jnp.arange(128)[:, None]
    colh = jnp.arange(gw)[None, :] // GDN_HEAD_DIM
    e_beta = (lane == colh).astype(F32)
    e_g = (lane == colh + nh).astype(F32)
    alog_row = jnp.pad(p['gdn_a_log'], ((0, 0), (nh, 128 - 2 * nh)))
    dtb_row = jnp.pad(p['gdn_dt_bias'], ((0, 0), (nh, 128 - 2 * nh)))
    row_gw = lambda: ((s, gw), F32, (ts, gw), lambda i: (i, 0))
    betab, gb = _tile_fwd(_gdn_gates, "gdn_gates", nt, [_rt(pba, ts)], [alog_row, dtb_row, e_beta, e_g],
                          [row_gw(), row_gw()])
    *intra, t_inv = _gdn_intra_fwd(q, k, v, gb, betab)
    o_raw, states = _gdn_inter_fwd(*intra, gb)
    ga = _tile_fwd(_gdn_post, "gdn_post", nt, [_rt(o_raw, ts), _rt(proj, ts, CB_ZA, gw)], [p['gdn_norm_g']],
                   [((s, gw), BF16, (ts, gw), lambda i: (i, 0))])[0]

    e_rep = (jnp.arange(S5_STATE)[:, None] == jnp.arange(S5_STATE * S5_GROUP)[None, :] // S5_GROUP).astype(F32)
    s5_in = [p['s5_lambda_re'], p['s5_lambda_im'], p['s5_log_dt'].reshape(ng, 1),
             p['s5_b_re'].reshape(ng, S5_STATE * S5_GROUP), p['s5_b_im'].reshape(ng, S5_STATE * S5_GROUP), e_rep]
    one = lambda shp: (shp, F32, shp, lambda i, n=len(shp): (0,) * n)
    ab_re, ab_im, bbr, bbi = _tile_fwd(_s5_params, "s5_params", 1, [], s5_in,
                                       [one((ng, S5_STATE)), one((ng, S5_STATE)), one((ng, S5_STATE * S5_GROUP)),
                                        one((ng, S5_STATE * S5_GROUP))])
    coef = _s5_coef(ab_re.reshape(1, nl), ab_im.reshape(1, nl))
    to_bd_b = lambda t: _block_diag(t.reshape(nb, S5_GROUPS_PER_BLOCK, S5_STATE, S5_GROUP).transpose(0, 1, 3, 2))
    to_bd_c = lambda t: _block_diag(t.reshape(nb, S5_GROUPS_PER_BLOCK, S5_GROUP, S5_STATE).transpose(0, 1, 3, 2))
    bbd_re, bbd_im = to_bd_b(bbr).astype(BF16), to_bd_b(bbi).astype(BF16)
    cbd_re, cbd_im = to_bd_c(p['s5_c_re']).astype(BF16), to_bd_c(p['s5_c_im']).astype(BF16)
    xb_arr = lax.slice_in_dim(proj, CB_XB * gw, (CB_XB + 1) * gw, axis=1)
    h_re, h_im, ylin = _s5_fwd(xb_arr, bbd_re, bbd_im, cbd_re, cbd_im, coef, ts)
    gl = _tile_fwd(_s5_post1, "s5_post1", nt, [_rt(ylin, ts), _rt(proj, ts, CB_XB, gw)], [p['s5_d']],
                   [((s, gw), BF16, (ts, gw), lambda i: (i, 0))])[0]
    tglu = _mm(gl, p['s5_w_glu'], b_shards=True, name="s5_glu")
    gbb = _tile_fwd(_s5_post2, "s5_post2", nt, [_rt(tglu, ts), _rt(proj, ts, CB_ZB, gw)], [],
                    [((s, gw), BF16, (ts, gw), lambda i: (i, 0))])[0]

    m_len = mem.shape[0]
    mem_n = _tile_fwd(_rms, "mem_rms", 1, [_rt(mem, m_len)], [p['mem_norm_g']],
                      [((m_len, d), BF16, (m_len, d), lambda i: (i, 0))])[0]
    kv = _mm(mem_n, p['w_kv_mem'], name="mem_kv")
    gcc = _tile_fwd(_attn, "attn", nt, [_rt(proj, ts, CB_QC, gw), _rt(proj, ts, CB_ZC, gw)], [kv],
                    [((s, gw), BF16, (ts, gw), lambda i: (i, 0))])[0]

    p_a = _mm(ga, p['w_br_a'], b_shards=True, name="br_a")
    p_b = _mm(gbb, p['w_br_b'], b_shards=True, name="br_b")
    p_c = _mm(gcc, p['w_br_c'], b_shards=True, name="br_c")
    gate_acts = [_rt(proj, ts, CB_G // 2 + j, d) for j in range(3)]
    merged = _tile_fwd(_merge, "merge", nt, gate_acts + [_rt(p_a, ts), _rt(p_b, ts), _rt(p_c, ts)], [],
                       [((s, d), BF16, (ts, d), lambda i: (i, 0))])[0]
    mo = _mm(merged, p['w_out'], name="out_proj")
    dh, dfg, loss = _final(x, mo, target, p['final_g'].reshape(1, d), ts)
    grads['final_g'] = dfg.reshape(d)

    dmerged = _mm(dh, p['w_out'], tb=True, name="d_merged")
    grads['w_out'] = _mm(merged, dh, ta=True, name="dw_out")
    row_d = lambda dt: ((s, d), dt, (ts, d), lambda i: (i, 0))
    dg0, dg1, dg2, dpa, dpb, dpc = _tile_bwd(
        _merge, "merge_bwd", nt, gate_acts + [_rt(p_a, ts), _rt(p_b, ts), _rt(p_c, ts)], [], [_rt(dmerged, ts)],
        [row_d(BF16)] * 6, [])
    dga = _mm(dpa, p['w_br_a'], tb=True, b_shards=True, name="d_ga")
    dgbb = _mm(dpb, p['w_br_b'], tb=True, b_shards=True, name="d_gb")
    dgcc = _mm(dpc, p['w_br_c'], tb=True, b_shards=True, name="d_gc")
    grads['w_br_a'] = _mm(ga, dpa, ta=True, out_shards=N_DEV, name="dw_br_a")
    grads['w_br_b'] = _mm(gbb, dpb, ta=True, out_shards=N_DEV, name="dw_br_b")
    grads['w_br_c'] = _mm(gcc, dpc, ta=True, out_shards=N_DEV, name="dw_br_c")
    row_h = lambda dt: ((s, gw), dt, (ts, gw), lambda i: (i, 0))

    dqc, dzc, dkv = _tile_bwd(_attn, "attn_bwd", nt, [_rt(proj, ts, CB_QC, gw), _rt(proj, ts, CB_ZC, gw)], [kv],
                              [_rt(dgcc, ts)], [row_h(BF16), row_h(BF16)], [True])
    grads['w_kv_mem'] = _mm(mem_n, dkv, ta=True, name="dw_kv")
    dmem_n = _mm(dkv, p['w_kv_mem'], tb=True, name="d_mem_n")
    grads['mem_norm_g'] = _tile_bwd(_rms, "mem_rms_bwd", 1, [_rt(mem, m_len)], [p['mem_norm_g']],
                                    [_rt(dmem_n, m_len)], [None], [True])[0]

    dtglu, dzb = _tile_bwd(_s5_post2, "s5_post2_bwd", nt, [_rt(tglu, ts), _rt(proj, ts, CB_ZB, gw)], [],
                           [_rt(dgbb, ts)], [((s, 2 * gw), BF16, (ts, 2 * gw), lambda i: (i, 0)), row_h(BF16)], [])
    grads['s5_w_glu'] = _mm(gl, dtglu, ta=True, out_shards=N_DEV, name="dw_glu")
    s5_d = p['s5_d']
    if early_grads is not None:
        s5_d = s5_d + early_grads(grads)[:1, :1]
    dgl = _mm(dtglu, p['s5_w_glu'], tb=True, b_shards=True, name="d_gl")
    dylin, dxb1, dd = _tile_bwd(_s5_post1, "s5_post1_bwd", nt, [_rt(ylin, ts), _rt(proj, ts, CB_XB, gw)],
                                [s5_d], [_rt(dgl, ts)], [row_h(F32), row_h(F32)], [True])
    grads['s5_d'] = dd
    dxb2, dbbd_re, dbbd_im, dcbd_re, dcbd_im, da_re, da_im = _s5_bwd(dylin, xb_arr, h_re, h_im, bbd_re, bbd_im,
                                                                    cbd_re, cbd_im, coef, ts)
    from_bd_b = lambda t: _diag_blocks(t, S5_GROUP, S5_STATE).transpose(0, 1, 3, 2).reshape(ng, S5_STATE * S5_GROUP)
    from_bd_c = lambda t: _diag_blocks(t, S5_STATE, S5_GROUP).transpose(0, 1, 3, 2).reshape(1, ng, S5_GROUP, S5_STATE)
    grads['s5_c_re'], grads['s5_c_im'] = from_bd_c(dcbd_re), from_bd_c(dcbd_im)
    s5_cts = [jnp.sum(da_re, axis=0).reshape(ng, S5_STATE), jnp.sum(da_im, axis=0).reshape(ng, S5_STATE),
              from_bd_b(dbbd_re), from_bd_b(dbbd_im)]
    dlr, dli, dlogdt, dbr, dbi = _tile_bwd(_s5_params, "s5_params_bwd", 1, [], s5_in,
                                           [(c, c.shape, lambda i: (0, 0)) for c in s5_cts], [],
                                           [True, True, True, True, True, False])
    grads['s5_lambda_re'], grads['s5_lambda_im'] = dlr[None], dli[None]
    grads['s5_log_dt'] = dlogdt.reshape(1, ng)
    grads['s5_b_re'] = dbr.reshape(1, ng, S5_STATE, S5_GROUP)
    grads['s5_b_im'] = dbi.reshape(1, ng, S5_STATE, S5_GROUP)
    dxb = (dxb1 + dxb2).astype(BF16)

    do_raw, dza, dgng = _tile_bwd(_gdn_post, "gdn_post_bwd", nt, [_rt(o_raw, ts), _rt(proj, ts, CB_ZA, gw)],
                                  [p['gdn_norm_g']], [_rt(dga, ts)], [row_h(F32), row_h(BF16)], [True])
    grads['gdn_norm_g'] = dgng
    *intra_cts, dgb_inter = _gdn_inter_bwd(*intra, gb, states, do_raw)
    dq, dk, dv, dgb, dbetab = _gdn_intra_bwd(q, k, v, gb, betab, t_inv, intra_cts, dgb_inter)
    dpba, dalog, ddtb = _tile_bwd(_gdn_gates, "gdn_gates_bwd", nt, [_rt(pba, ts)], [alog_row, dtb_row, e_beta, e_g],
                                  [_rt(dbetab, ts), _rt(dgb, ts)], [((s, 128), BF16, (ts, 128), lambda i: (i, 0))],
                                  [True, True, False, False])
    grads['gdn_a_log'] = dalog[:, nh:2 * nh]
    grads['gdn_dt_bias'] = ddtb[:, nh:2 * nh]
    dqkv, dconv = [], []
    for j, (mode, ct) in enumerate((('q', dq), ('k', dk), ('v', dv))):
        off = j * nh
        wspec = (conv_w, (CONV_WIDTH, GDN_HEAD_DIM), lambda i, off=off: (0, off + i))
        dxc, dwc = _tile_bwd(
            _gdn_pre(mode), "gdn_pre_bwd_" + mode, nh, [col(proj, off), wspec], [], [col(ct, 0)],
            [((s, gw), BF16, (s, GDN_HEAD_DIM), lambda i: (0, i)),
             ((CONV_WIDTH, gw), F32, (CONV_WIDTH, GDN_HEAD_DIM), lambda i: (0, i))], [])
        dqkv.append(dxc)
        dconv.append(dwc)
    grads['conv_w'] = jnp.concatenate(dconv, axis=1)

    dproj = jnp.concatenate(dqkv + [dza, dxb, dzb, dqc, dzc, dg0, dg1, dg2], axis=1)
    grads['w_ba'] = _mm(u, dpba, ta=True, name="dw_ba")
    if last_grads is None:
        grads['w_main'] = _mm(u, dproj, ta=True, tm=1024, tn=2048, tk=512, name="dw_main")
    else:
        uu = u
        for h in range(2):
            gm = _mm(uu, dproj, ta=True, tm=1024, tn=2048, tk=512, rows=(h * (d // 2), d // 2), name=f"dw_main_{h}")
            tok = last_grads(h, gm, grads)[:1, :1].astype(BF16)
            if h == 0:
                uu = uu + tok
            else:
                dpba = dpba + tok
    du = _mm(dpba, w_ba, tb=True, name="du_ba")
    du = _mm(dproj, w_main, tb=True, addend=du, tm=512, tn=2048, tk=1024, name="du_main")
    grad_x, dng = _tile_bwd(_rms, "rms_bwd", nt, [_rt(x, ts)], [p['norm_g']], [_rt(du, ts)],
                            [row_d(F32) + (dh,)], [True])
    grads['norm_g'] = dng
    return loss, grad_x, grads


def _to_shards(name, g):
    if SHARDED[name] == 'row':
        return g.reshape((N_DEV, g.shape[0] // N_DEV) + g.shape[1:])
    r, c = g.shape
    return g.reshape(r, N_DEV, c // N_DEV).transpose(1, 0, 2)


def _from_shards(name, t):
    if SHARDED[name] == 'row':
        return t.reshape((t.shape[0] * t.shape[1],) + t.shape[2:])
    n, r, c = t.shape
    return t.transpose(1, 0, 2).reshape(r, n * c)


def _step(x, mem, target, w, m, v):
    sharded = list(SHARDED)
    shard_shapes = {n: tuple(w[n].shape[1:]) for n in sharded}
    d = x.shape[-1]
    ba_lo = 2 * d
    ba_hi = ba_lo + 2 * (d // 2 // GDN_HEAD_DIM)

    w_in_all = _gather([w['w_in'][0].astype(BF16)], "gather_w_in")[0]
    late = [w[n][0].astype(BF16) for n in OVERLAPPED] + [w['conv_w'][0]]
    every = [_send_whole] * len(late)
    lands = [jnp.broadcast_to(t[None], (N_DEV,) + t.shape) for t in late]
    gather_started, token = _direct_start(late, lands, every, "gather_rest_start")
    full = {}
    full['w_main'], full['w_ba'] = _w_in_from_shards(w_in_all, ba_lo, ba_hi)
    for n in REPLICATED:
        full[n] = w[n]
    for n in ('s5_lambda_re', 's5_lambda_im', 's5_c_re', 's5_c_im'):
        full[n] = w[n][0]
    full['norm_g'] = w['norm_g'] + token[:1, :1]

    def late_weights(proj):
        got = dict(zip(OVERLAPPED + ['conv_w'], _direct_wait(gather_started, every, proj, "gather_rest_wait")[1]))
        for n in ('w_kv_mem', 'w_out', 'conv_w'):
            got[n] = _from_shards(n, got[n])
        return got

    slots = [_send_slot] * len(OVERLAPPED)
    scatter_started = []

    def early_grads(grads):
        gs = [_to_shards(n, grads[n]) if SHARDED[n] == 'row' else grads[n] for n in OVERLAPPED]
        started, tok = _direct_start(gs, [lax.empty(g.shape, g.dtype) for g in gs], slots, "scatter_early_start")
        scatter_started.append(started)
        return tok

    def last_grads(h, gm, grads):
        gs = [_w_in_to_shards(gm, grads['w_ba'], ba_lo, ba_hi, BF16, h * gm.shape[0], f"dw_in_layout_{h}")]
        if h == 0:
            gs.append(_to_shards('conv_w', grads['conv_w']))
        started, tok = _direct_start(gs, [lax.empty(g.shape, g.dtype) for g in gs], slots[:len(gs)],
                                     f"scatter_last_start_{h}")
        scatter_started.append(started)
        return tok

    loss, grad_x, grads = _local_step(x[0], mem[0], target[0], full, late_weights, early_grads, last_grads)
    res = {}
    me = (4 * lax.axis_index("x") + 2 * lax.axis_index("y") + lax.axis_index("c")).astype(jnp.int32).reshape(1)

    def update(names, exchanged):
        for n, own, part in zip(names, *exchanged):
            outs = _sum_adam(part, w[n][0], m[n][0], v[n][0], name="adam_" + n, own=own, me=me)
            for kind, t in zip(('grad', 'delta', 'new_m', 'new_v'), outs):
                res[kind, n] = t[None]

    update(OVERLAPPED, _direct_wait(scatter_started[0], slots, grad_x, "scatter_early_wait"))

    small = _pack([grads[n].reshape(w[n].shape) for n in REPLICATED] + [loss[:1, :1]], F32)
    allp = _gather([small], "gather_small")[0]
    zero = jnp.zeros((1, 1), F32)
    outs = _sum_adam(allp, *[_pack([t[n] for n in REPLICATED] + [zero], F32) for t in (w, m, v)], name="adam_small")
    shapes = [w[n].shape for n in REPLICATED] + [(1, 1)]
    for kind, buf in zip(('grad', 'delta', 'new_m', 'new_v'), outs):
        got = _unpack(buf, shapes)
        for n, t in zip(REPLICATED, got):
            res[kind, n] = t
        if kind == 'grad':
            total_loss = got[-1].reshape(())
    (own0, own_conv), (got0, got_conv) = _direct_wait(scatter_started[1], slots[:2], outs[0], "scatter_last_wait_0")
    (own1,), (got1,) = _direct_wait(scatter_started[2], slots[:1], outs[0], "scatter_last_wait_1")
    update(['w_in', 'conv_w'], ([[own0, own1], own_conv], [[got0, got1], got_conv]))
    out = [total_loss, grad_x[None]]
    for kind in ('grad', 'delta', 'new_m', 'new_v'):
        out += [res[kind, n] for n in WEIGHTS]
    return tuple(out)


def kernel(x, mem, norm_g, w_in, conv_w, gdn_a_log, gdn_dt_bias, gdn_norm_g, s5_lambda_re, s5_lambda_im, s5_log_dt, s5_b_re, s5_b_im, s5_c_re, s5_c_im, s5_d, s5_w_glu, mem_norm_g, w_kv_mem, w_br_a, w_br_b, w_br_c, w_out, final_g, loss_target, m_norm_g, m_w_in, m_conv_w, m_gdn_a_log, m_gdn_dt_bias, m_gdn_norm_g, m_s5_lambda_re, m_s5_lambda_im, m_s5_log_dt, m_s5_b_re, m_s5_b_im, m_s5_c_re, m_s5_c_im, m_s5_d, m_s5_w_glu, m_mem_norm_g, m_w_kv_mem, m_w_br_a, m_w_br_b, m_w_br_c, m_w_out, m_final_g, v_norm_g, v_w_in, v_conv_w, v_gdn_a_log, v_gdn_dt_bias, v_gdn_norm_g, v_s5_lambda_re, v_s5_lambda_im, v_s5_log_dt, v_s5_b_re, v_s5_b_im, v_s5_c_re, v_s5_c_im, v_s5_d, v_s5_w_glu, v_mem_norm_g, v_w_kv_mem, v_w_br_a, v_w_br_b, v_w_br_c, v_w_out, v_final_g):
    a = dict(locals())
    w = {n: a[n] for n in WEIGHTS}
    m = {n: a['m_' + n] for n in WEIGHTS}
    v = {n: a['v_' + n] for n in WEIGHTS}
    return _step(x, mem, loss_target, w, m, v)
```

```python
import functools
import math

import jax
import jax.numpy as jnp
from jax import lax
from jax.experimental import pallas as pl
from jax.experimental.pallas import tpu as pltpu

F32 = jnp.float32
BF16 = jnp.bfloat16
HI = lax.Precision.HIGHEST

EPS = 1e-6
CHUNK = 64
GDN_HEAD_DIM = 128
CONV_WIDTH = 4
S5_GROUP = 16
S5_STATE = 64
S5_GROUPS_PER_BLOCK = 8
XA_HEADS = 4
N_DEV = 8
ADAM_LR, ADAM_B1, ADAM_B2, ADAM_EPS, ADAM_WD, ADAM_STEP = 0.001, 0.9, 0.999, 1e-08, 0.01, 10

VMEM_LIMIT_BYTES = 56 * 1024 * 1024
SCAN_LANES = 512
PACK_WIDTH = 512
PACK_ROWS = 256

WEIGHTS = ['norm_g', 'w_in', 'conv_w', 'gdn_a_log', 'gdn_dt_bias', 'gdn_norm_g', 's5_lambda_re', 's5_lambda_im',
           's5_log_dt', 's5_b_re', 's5_b_im', 's5_c_re', 's5_c_im', 's5_d', 's5_w_glu', 'mem_norm_g', 'w_kv_mem',
           'w_br_a', 'w_br_b', 'w_br_c', 'w_out', 'final_g']
SHARDED = {'w_in': 'col', 'conv_w': 'col', 's5_w_glu': 'col', 'w_kv_mem': 'row', 'w_br_a': 'col', 'w_br_b': 'col',
           'w_br_c': 'col', 'w_out': 'row'}
GATHER_BF16 = ['w_in', 's5_w_glu', 'w_kv_mem', 'w_br_a', 'w_br_b', 'w_br_c', 'w_out']
REPLICATED = [n for n in WEIGHTS if n not in SHARDED]
OVERLAPPED = ['s5_w_glu', 'w_kv_mem', 'w_br_a', 'w_br_b', 'w_br_c', 'w_out']


def _cparams(sem=None):
    return pltpu.CompilerParams(dimension_semantics=sem, vmem_limit_bytes=VMEM_LIMIT_BYTES)


def _pick(dim, pref):
    t = (min(pref, dim) // 128) * 128
    while t >= 128:
        if dim % t == 0:
            return t
        t -= 128
    return dim


def _make_dots(prep, precision):
    def raw(a, b, dims):
        return lax.dot_general(prep(a), prep(b), (dims, ((), ())), preferred_element_type=F32, precision=precision)

    @jax.custom_vjp
    def nn(a, b):
        return raw(a, b, ((1,), (0,)))

    @jax.custom_vjp
    def nt(a, b):
        return raw(a, b, ((1,), (1,)))

    @jax.custom_vjp
    def tn(a, b):
        return raw(a, b, ((0,), (0,)))

    nn.defvjp(lambda a, b: (nn(a, b), (a, b)), lambda r, ct: (nt(ct, r[1]), tn(r[0], ct)))
    nt.defvjp(lambda a, b: (nt(a, b), (a, b)), lambda r, ct: (nn(ct, r[1]), tn(ct, r[0])))
    tn.defvjp(lambda a, b: (tn(a, b), (a, b)), lambda r, ct: (nt(r[1], ct), nn(r[0], ct)))
    return nn, nt, tn


_bnn, _bnt, _btn = _make_dots(lambda a: a.astype(BF16), None)
_hnn, _hnt, _htn = _make_dots(lambda a: a.astype(F32), HI)
_mnn, _mnt, _mtn = _make_dots(lambda a: a.astype(F32), lax.Precision.HIGH)


def _mm(a, b, *, name, ta=False, tb=False, out_dtype=F32, addend=None, tm=512, tn=1024, tk=1024, b_shards=False,
        out_shards=0, rows=None):
    m, k = (a.shape[1], a.shape[0]) if ta else a.shape
    brows, bcols = (b.shape[1], b.shape[0] * b.shape[2]) if b_shards else b.shape
    n = brows if tb else bcols
    assert (bcols if tb else brows) == k, (a.shape, b.shape, ta, tb)
    first_row = 0
    if rows is not None:
        first_row, m = rows
    tm, tn, tk = _pick(m, tm), _pick(n, tn), _pick(k, tk)
    assert first_row % tm == 0
    r0 = first_row // tm
    bcs = ocs = 0
    if b_shards:
        bcs = b.shape[2]
        assert bcs % 128 == 0 and (tk if tb else tn) % bcs == 0
    if out_shards:
        ocs = n // out_shards
        assert ocs % 128 == 0 and tn % ocs == 0
    nk = k // tk
    dims = ((0 if ta else 1,), (1 if tb else 0,))

    def body(*refs):
        if addend is None:
            a_ref, b_ref, o_ref, acc_ref = refs
        else:
            a_ref, b_ref, add_ref, o_ref, acc_ref = refs
        kk = pl.program_id(2)

        @pl.when(kk == 0)
        def _():
            acc_ref[...] = jnp.zeros_like(acc_ref)

        dot = lambda x, y: lax.dot_general(x.astype(BF16), y.astype(BF16), (dims, ((), ())), preferred_element_type=F32)
        if not b_shards:
            acc_ref[...] += dot(a_ref[...], b_ref[...])
        elif tb:
            for g in range(tk // bcs):
                acc_ref[...] += dot(a_ref[:, g * bcs:(g + 1) * bcs], b_ref[g])
        else:
            for g in range(tn // bcs):
                acc_ref[:, g * bcs:(g + 1) * bcs] += dot(a_ref[...], b_ref[g])

        @pl.when(kk == nk - 1)
        def _():
            r = acc_ref[...]
            if addend is not None:
                r = r + add_ref[...].astype(F32)
            if out_shards:
                for g in range(tn // ocs):
                    o_ref[g] = r[:, g * ocs:(g + 1) * ocs].astype(o_ref.dtype)
            else:
                o_ref[...] = r.astype(o_ref.dtype)

    a_spec = (pl.BlockSpec((tk, tm), lambda i, j, kk: (kk, i + r0)) if ta
              else pl.BlockSpec((tm, tk), lambda i, j, kk: (i + r0, kk)))
    if b_shards:
        b_spec = (pl.BlockSpec((tk // bcs, tn, bcs), lambda i, j, kk: (kk, j, 0)) if tb
                  else pl.BlockSpec((tn // bcs, tk, bcs), lambda i, j, kk: (j, kk, 0)))
    else:
        b_spec = (pl.BlockSpec((tn, tk), lambda i, j, kk: (j, kk)) if tb
                  else pl.BlockSpec((tk, tn), lambda i, j, kk: (kk, j)))
    if out_shards:
        o_spec = pl.BlockSpec((tn // ocs, tm, ocs), lambda i, j, kk: (j, i, 0))
        out_shape = jax.ShapeDtypeStruct((out_shards, m, ocs), out_dtype)
    else:
        o_spec = pl.BlockSpec((tm, tn), lambda i, j, kk: (i, j))
        out_shape = jax.ShapeDtypeStruct((m, n), out_dtype)
    in_specs = [a_spec, b_spec] + ([o_spec] if addend is not None else [])
    args = (a, b) + ((addend,) if addend is not None else ())
    return pl.pallas_call(
        body, name=name, grid=(m // tm, n // tn, nk), in_specs=in_specs, out_specs=o_spec,
        out_shape=out_shape, scratch_shapes=[pltpu.VMEM((tm, tn), F32)],
        compiler_params=_cparams(("parallel", "parallel", "arbitrary")))(*args)


def _rt(arr, ts, cb=0, w=None):
    w = arr.shape[1] if w is None else w
    return (arr, (ts, w), lambda i, cb=cb: (i, cb))


def _whole(p):
    return pl.BlockSpec(p.shape, lambda i, nd=p.ndim: (0,) * nd)


def _tile_fwd(f, name, n, acts, params, outs):
    na, npar = len(acts), len(params)

    def body(*refs):
        res = f(*[r[...] for r in refs[:na + npar]])
        for r, v in zip(refs[na + npar:], res):
            r[...] = v.astype(r.dtype)

    in_specs = [pl.BlockSpec(b, m) for _, b, m in acts] + [_whole(p) for p in params]
    out = pl.pallas_call(
        body, name=name, grid=(n,), in_specs=in_specs,
        out_specs=[pl.BlockSpec(b, m) for _, _, b, m in outs],
        out_shape=[jax.ShapeDtypeStruct(s, d) for s, d, _, _ in outs],
        compiler_params=_cparams(("parallel",)))(*[a for a, _, _ in acts], *params)
    return out


def _tile_bwd(f, name, n, acts, params, cts, agrads, pgrads):
    na, npar, nc = len(acts), len(params), len(cts)
    adds = [g[4] for g in agrads if g is not None and len(g) == 5]

    def body(*refs):
        i = pl.program_id(0)
        ins = [r[...] for r in refs[:na + npar]]
        outs, vjp = jax.vjp(f, *ins)
        g = vjp(tuple(c[...].astype(o.dtype) for c, o in zip(refs[na + npar:na + npar + nc], outs)))
        add_refs = refs[na + npar + nc:na + npar + nc + len(adds)]
        orefs = refs[na + npar + nc + len(adds):]
        k = 0
        for j in range(na):
            if agrads[j] is not None:
                val = g[j]
                if len(agrads[j]) == 5:
                    val = val + add_refs[[id(t) for t in adds].index(id(agrads[j][4]))][...]
                orefs[k][...] = val.astype(orefs[k].dtype)
                k += 1
        for j in range(npar):
            if pgrads[j]:
                o = orefs[k]

                @pl.when(i == 0)
                def _(o=o):
                    o[...] = jnp.zeros_like(o)

                o[...] += g[na + j].astype(F32)
                k += 1

    in_specs = ([pl.BlockSpec(b, m) for _, b, m in acts] + [_whole(p) for p in params]
                + [pl.BlockSpec(b, m) for _, b, m in cts]
                + [pl.BlockSpec(g[2], g[3]) for g in agrads if g is not None and len(g) == 5])
    out_specs = [pl.BlockSpec(g[2], g[3]) for g in agrads if g is not None]
    out_shape = [jax.ShapeDtypeStruct(g[0], g[1]) for g in agrads if g is not None]
    for p, flag in zip(params, pgrads):
        if flag:
            out_specs.append(_whole(p))
            out_shape.append(jax.ShapeDtypeStruct(p.shape, F32))
    return pl.pallas_call(
        body, name=name, grid=(n,), in_specs=in_specs, out_specs=out_specs, out_shape=out_shape,
        compiler_params=_cparams(("arbitrary",)))(*[a for a, _, _ in acts], *params, *[c for c, _, _ in cts], *adds)


def _silu(x):
    return x * jax.nn.sigmoid(x)


def _rms(x, g):
    x = x.astype(F32)
    return (x * lax.rsqrt(jnp.mean(x * x, axis=-1, keepdims=True) + EPS) * g,)


def _shift_down(x, s):
    row = lax.broadcasted_iota(jnp.int32, x.shape, 0)
    return jnp.where(row >= s, pltpu.roll(x, s, 0), 0.0)


def _shift_up(x, s):
    n = x.shape[0]
    row = lax.broadcasted_iota(jnp.int32, x.shape, 0)
    return jnp.where(row < n - s, pltpu.roll(x, n - s, 0), 0.0)


@functools.partial(jax.custom_vjp, nondiff_argnums=(1,))
def _shift(x, s):
    return _shift_down(x, s)


_shift.defvjp(lambda x, s: (_shift_down(x, s), None), lambda s, _, ct: (_shift_up(ct, s),))


def _gdn_pre(mode):
    def f(x, w):
        y = x * w[CONV_WIDTH - 1:CONV_WIDTH, :]
        for j in range(CONV_WIDTH - 1):
            y = y + _shift(x, CONV_WIDTH - 1 - j) * w[j:j + 1, :]
        y = _silu(y)
        if mode != 'v':
            y = y * lax.rsqrt(jnp.sum(y * y, axis=-1, keepdims=True) + EPS)
        if mode == 'q':
            y = y * (GDN_HEAD_DIM ** -0.5)
        return (y,)
    return f


def _softplus(x):
    return jnp.maximum(x, 0.0) + jnp.log1p(jnp.exp(-jnp.abs(x)))


def _gdn_gates(ba, alog, dtb, e_beta, e_g):
    beta = jax.nn.sigmoid(ba)
    g = -jnp.exp(alog) * _softplus(ba + dtb)
    return _hnn(beta, lax.stop_gradient(e_beta)), _hnn(g, lax.stop_gradient(e_g))


@jax.custom_vjp
def _inverse_known(neg, t):
    return t


_inverse_known.defvjp(lambda neg, t: (t, t), lambda t, ct: (_mtn(t, _mnt(ct, t)), jnp.zeros_like(t)))


def _gdn_intra(q, k, v, gb, bb, t_known=None):
    n, c = len(q), q[0].shape[0]
    ri = lax.broadcasted_iota(jnp.int32, (c, c), 0)
    ci = lax.broadcasted_iota(jnp.int32, (c, c), 1)
    incl, strict = ri >= ci, ri > ci
    tri = incl.astype(F32)
    eye = (ri == ci).astype(F32)
    each = range(n)
    gc = [_hnn(tri, gb[i]) for i in each]
    decay = [jnp.exp(jnp.where(incl, gc[i][:, :c] - gc[i].T[:c, :], -1e30)) for i in each]
    kb = [k[i] * bb[i] for i in each]
    kk = [_bnt(kb[i], k[i]) for i in each]
    qk = [_bnt(q[i], k[i]) for i in each]
    p = [jnp.where(strict, -(kk[i] * decay[i]), 0.0) for i in each]
    if t_known is None:
        t = [eye + p[i] for i in each]
        for _ in range(int(math.log2(c)) - 1):
            p = [_mnn(p[i], p[i]) for i in each]
            tp = [_mnn(t[i], p[i]) for i in each]
            t = [t[i] + tp[i] for i in each]
    else:
        t = [_inverse_known(p[i], t_known[i]) for i in each]
    egc = [jnp.exp(gc[i]) for i in each]
    u_val = [_mnn(t[i], v[i] * bb[i]) for i in each]
    w_dec = [_mnn(t[i], kb[i] * egc[i]) for i in each]
    qk = [qk[i] * decay[i] for i in each]
    gl = [jnp.sum(gb[i], axis=0, keepdims=True) for i in each]
    return w_dec, u_val, qk, [q[i] * egc[i] for i in each], [k[i] * jnp.exp(gl[i] - gc[i]) for i in each], t


def _gdn_inter(w_dec, u_val, qk, q_dec, k_dec, gb, state):
    each = range(len(state))
    ws = [_bnn(w_dec[i], state[i]) for i in each]
    qs = [_bnn(q_dec[i], state[i]) for i in each]
    v_new = [u_val[i] - ws[i] for i in each]
    qv = [_bnn(qk[i], v_new[i]) for i in each]
    kv = [_btn(k_dec[i], v_new[i]) for i in each]
    decayed = [state[i] * jnp.exp(jnp.sum(gb[i], axis=0, keepdims=True)) for i in each]
    return [qs[i] + qv[i] for i in each], [decayed[i] + kv[i] for i in each]


def _gdn_post(o, z, g):
    parts = []
    for h in range(o.shape[1] // GDN_HEAD_DIM):
        oh = o[:, h * GDN_HEAD_DIM:(h + 1) * GDN_HEAD_DIM]
        parts.append(oh * lax.rsqrt(jnp.mean(oh * oh, axis=-1, keepdims=True) + EPS) * g)
    y = parts[0] if len(parts) == 1 else jnp.concatenate(parts, axis=1)
    return (y * _silu(z),)


def _gelu(x):
    return 0.5 * x * (1.0 + jnp.tanh(0.7978845608028654 * (x + 0.044715 * x * x * x)))


def _s5_post1(ylin, xb, d):
    return (_gelu(ylin + d * xb),)


def _s5_post2(t, z):
    w = t.shape[1] // 2
    return (t[:, :w] * jax.nn.sigmoid(t[:, w:]) * _silu(z),)


def _attn(q, z, kv):
    w = q.shape[1]
    hd = w // XA_HEADS
    parts = []
    for h in range(XA_HEADS):
        s = _bnt(q[:, h * hd:(h + 1) * hd], kv[:, h * hd:(h + 1) * hd]) * (hd ** -0.5)
        s = s - jnp.max(s, axis=-1, keepdims=True)
        e = jnp.exp(s)
        p = e / jnp.sum(e, axis=-1, keepdims=True)
        parts.append(_bnn(p, kv[:, w + h * hd:w + (h + 1) * hd]))
    return (jnp.concatenate(parts, axis=1) * _silu(z),)


def _merge(g0, g1, g2, pa, pb, pc):
    return (jax.nn.sigmoid(g0) * pa + jax.nn.sigmoid(g1) * pb + jax.nn.sigmoid(g2) * pc,)


def _s5_params(lr, li, logdt, br, bi, e):
    dt = jnp.exp(logdt)
    mag = jnp.exp(lr * dt)
    ab_re, ab_im = mag * jnp.cos(li * dt), mag * jnp.sin(li * dt)
    den = lr * lr + li * li
    nr, ni = ab_re - 1.0, ab_im
    e = lax.stop_gradient(e)
    cre = _hnn((nr * lr + ni * li) / den, e)
    cim = _hnn((ni * lr - nr * li) / den, e)
    return ab_re, ab_im, cre * br - cim * bi, cre * bi + cim * br


def _gdn_blocks(s, w, per_step):
    nh, nc = w // GDN_HEAD_DIM, s // CHUNK
    cpb = math.gcd(per_step, nc)
    return nh, nc, cpb, nc // cpb, (cpb * CHUNK, w), (cpb * CHUNK, nh * CHUNK)


def _gdn_pairs(cpb, nh):
    wide, narrow = [], []
    for cb in range(cpb):
        rows = slice(cb * CHUNK, (cb + 1) * CHUNK)
        for h in range(nh):
            wide.append((rows, slice(h * GDN_HEAD_DIM, (h + 1) * GDN_HEAD_DIM)))
            narrow.append((rows, slice(h * CHUNK, (h + 1) * CHUNK)))
    return wide, narrow


def _gdn_intra_fwd(q, k, v, gb, bb, per_step=4):
    s, w = q.shape
    nh, nc, cpb, n, wide, narrow = _gdn_blocks(s, w, per_step)

    def body(q_ref, k_ref, v_ref, g_ref, b_ref, wd_ref, uv_ref, qk_ref, qd_ref, kd_ref, t_ref):
        wide, narrow = _gdn_pairs(cpb, nh)
        res = _gdn_intra(*[[r[ix] for ix in wide] for r in (q_ref, k_ref, v_ref, g_ref, b_ref)])
        for ref, vals, where in zip((wd_ref, uv_ref, qk_ref, qd_ref, kd_ref, t_ref), res,
                                    (wide, wide, narrow, wide, wide, narrow)):
            for ix, val in zip(where, vals):
                ref[ix] = val

    bw = pl.BlockSpec(wide, lambda i: (i, 0))
    bn = pl.BlockSpec(narrow, lambda i: (i, 0))
    fw = jax.ShapeDtypeStruct((s, w), F32)
    fn = jax.ShapeDtypeStruct((s, nh * CHUNK), F32)
    return pl.pallas_call(
        body, name="gdn_intra", grid=(n,), in_specs=[bw] * 5, out_specs=[bw, bw, bn, bw, bw, bn],
        out_shape=[fw, fw, fn, fw, fw, fn], compiler_params=_cparams(("parallel",)))(q, k, v, gb, bb)


def _gdn_intra_bwd(q, k, v, gb, bb, t, cts, dgb_inter, per_step=4):
    s, w = q.shape
    nh, nc, cpb, n, wide, narrow = _gdn_blocks(s, w, per_step)

    def body(q_ref, k_ref, v_ref, g_ref, b_ref, t_ref, cwd, cuv, cqk, cqd, ckd, dgi, dq_ref, dk_ref, dv_ref, dg_ref,
             db_ref):
        wide, narrow = _gdn_pairs(cpb, nh)
        t_known = [t_ref[ix] for ix in narrow]
        _, vjp = jax.vjp(lambda *a: _gdn_intra(*a, t_known=t_known)[:5],
                         *[[r[ix] for ix in wide] for r in (q_ref, k_ref, v_ref, g_ref, b_ref)])
        cts = tuple([r[ix] for ix in where] for r, where in zip((cwd, cuv, cqk, cqd, ckd),
                                                               (wide, wide, narrow, wide, wide)))
        dq, dk, dv, dg, db = vjp(cts)
        for j, ix in enumerate(wide):
            dq_ref[ix], dk_ref[ix], dv_ref[ix], db_ref[ix] = dq[j], dk[j], dv[j], db[j]
            dg_ref[ix] = dg[j] + dgi[ix]

    bw = pl.BlockSpec(wide, lambda i: (i, 0))
    bn = pl.BlockSpec(narrow, lambda i: (i, 0))
    return pl.pallas_call(
        body, name="gdn_intra_bwd", grid=(n,), in_specs=[bw] * 5 + [bn, bw, bw, bn, bw, bw, bw], out_specs=[bw] * 5,
        out_shape=[jax.ShapeDtypeStruct((s, w), F32)] * 5,
        compiler_params=_cparams(("parallel",)))(q, k, v, gb, bb, t, *cts, dgb_inter)


def _gdn_inter_fwd(wd, uv, qk, qd, kd, gb, per_step=4):
    s, w = wd.shape
    nh, nc, cpb, n, wide, narrow = _gdn_blocks(s, w, per_step)
    hd = GDN_HEAD_DIM

    def body(wd_ref, uv_ref, qk_ref, qd_ref, kd_ref, g_ref, o_ref, st_ref, state):
        @pl.when(pl.program_id(0) == 0)
        def _():
            state[...] = jnp.zeros_like(state)

        wide, narrow = _gdn_pairs(cpb, nh)
        st = [state[h] for h in range(nh)]
        for cb in range(cpb):
            wi, na = wide[cb * nh:(cb + 1) * nh], narrow[cb * nh:(cb + 1) * nh]
            for h in range(nh):
                st_ref[cb, h] = st[h]
            o, st = _gdn_inter([wd_ref[ix] for ix in wi], [uv_ref[ix] for ix in wi], [qk_ref[ix] for ix in na],
                               [qd_ref[ix] for ix in wi], [kd_ref[ix] for ix in wi], [g_ref[ix] for ix in wi], st)
            for h in range(nh):
                o_ref[wi[h]] = o[h]
        for h in range(nh):
            state[h] = st[h]

    bw = pl.BlockSpec(wide, lambda i: (i, 0))
    bn = pl.BlockSpec(narrow, lambda i: (i, 0))
    return pl.pallas_call(
        body, name="gdn_inter", grid=(n,), in_specs=[bw, bw, bn, bw, bw, bw],
        out_specs=[bw, pl.BlockSpec((cpb, nh, hd, hd), lambda i: (i, 0, 0, 0))],
        out_shape=[jax.ShapeDtypeStruct((s, w), F32), jax.ShapeDtypeStruct((nc, nh, hd, hd), F32)],
        scratch_shapes=[pltpu.VMEM((nh, hd, hd), F32)],
        compiler_params=_cparams(("arbitrary",)))(wd, uv, qk, qd, kd, gb)


def _gdn_inter_bwd(wd, uv, qk, qd, kd, gb, states, do, per_step=4):
    s, w = wd.shape
    nh, nc, cpb, n, wide, narrow = _gdn_blocks(s, w, per_step)
    hd = GDN_HEAD_DIM

    def body(wd_ref, uv_ref, qk_ref, qd_ref, kd_ref, g_ref, st_ref, do_ref, cwd, cuv, cqk, cqd, ckd, dg_ref, dstate):
        @pl.when(pl.program_id(0) == 0)
        def _():
            dstate[...] = jnp.zeros_like(dstate)

        wide, narrow = _gdn_pairs(cpb, nh)
        dst = [dstate[h] for h in range(nh)]
        for cb in reversed(range(cpb)):
            wi, na = wide[cb * nh:(cb + 1) * nh], narrow[cb * nh:(cb + 1) * nh]
            _, vjp = jax.vjp(_gdn_inter, [wd_ref[ix] for ix in wi], [uv_ref[ix] for ix in wi],
                             [qk_ref[ix] for ix in na], [qd_ref[ix] for ix in wi], [kd_ref[ix] for ix in wi],
                             [g_ref[ix] for ix in wi], [st_ref[cb, h] for h in range(nh)])
            dwd, duv, dqk, dqd, dkd, dg, dst = vjp(([do_ref[ix] for ix in wi], dst))
            for h in range(nh):
                cwd[wi[h]], cuv[wi[h]], cqk[na[h]], cqd[wi[h]], ckd[wi[h]], dg_ref[wi[h]] = (
                    dwd[h], duv[h], dqk[h], dqd[h], dkd[h], dg[h])
        for h in range(nh):
            dstate[h] = dst[h]

    bw = pl.BlockSpec(wide, lambda i: (n - 1 - i, 0))
    bn = pl.BlockSpec(narrow, lambda i: (n - 1 - i, 0))
    fw = jax.ShapeDtypeStruct((s, w), F32)
    return pl.pallas_call(
        body, name="gdn_inter_bwd", grid=(n,),
        in_specs=[bw, bw, bn, bw, bw, bw, pl.BlockSpec((cpb, nh, hd, hd), lambda i: (n - 1 - i, 0, 0, 0)), bw],
        out_specs=[bw, bw, bn, bw, bw, bw],
        out_shape=[fw, fw, jax.ShapeDtypeStruct((s, nh * CHUNK), F32), fw, fw, fw],
        scratch_shapes=[pltpu.VMEM((nh, hd, hd), F32)],
        compiler_params=_cparams(("arbitrary",)))(wd, uv, qk, qd, kd, gb, states, do)


def _s5_coef(ar, ai):
    nl = ar.shape[1]

    def body(ar_ref, ai_ref, o_ref):
        row = lax.broadcasted_iota(jnp.int32, (8, nl), 0)
        for base, sign in ((0, 1.0), (8, -1.0)):
            pr = [jnp.broadcast_to(ar_ref[...], (8, nl))]
            pi = [jnp.broadcast_to(ai_ref[...], (8, nl)) * sign]
            for _ in range(7):
                pr.append(pr[-1] * pr[0] - pi[-1] * pi[0])
                pi.append(pr[-2] * pi[0] + pi[-1] * pr[0])
            for j, d in enumerate((1, 2, 4)):
                m = (row >= d) if base == 0 else (row <= 7 - d)
                o_ref[base + 2 * j] = jnp.where(m, pr[d - 1], 0.0)
                o_ref[base + 2 * j + 1] = jnp.where(m, pi[d - 1], 0.0)
            cr, ci = jnp.zeros((8, nl), F32), jnp.zeros((8, nl), F32)
            for t in range(8):
                e = t if base == 0 else 7 - t
                cr = jnp.where(row == t, pr[e], cr)
                ci = jnp.where(row == t, pi[e], ci)
            o_ref[base + 6] = cr
            o_ref[base + 7] = ci

    return pl.pallas_call(body, name="s5_coef", out_shape=jax.ShapeDtypeStruct((16, 8, nl), F32),
                          compiler_params=_cparams())(ar, ai)


def _scan_tile(src_re, src_im, dst_re, dst_im, coef_ref, carry_re, carry_im, ts, reverse, extra=None):
    nl = src_re.shape[1]
    base = 8 if reverse else 0
    ng = ts // 8
    for lc in range(nl // SCAN_LANES):
        ln = slice(lc * SCAN_LANES, (lc + 1) * SCAN_LANES)
        m = [coef_ref[base + j, :, ln] for j in range(8)]
        row = lax.broadcasted_iota(jnp.int32, (8, SCAN_LANES), 0)

        def step(r, carry, ln=ln, m=m, row=row):
            grp = (ng - 1 - r) if reverse else r
            rows = pl.ds(pl.multiple_of(grp * 8, 8), 8)
            xr, xi = src_re[rows, ln], src_im[rows, ln]
            for j, d in enumerate((1, 2, 4)):
                sh = 8 - d if reverse else d
                sr, si = pltpu.roll(xr, sh, 0), pltpu.roll(xi, sh, 0)
                mr, mi = m[2 * j], m[2 * j + 1]
                xr, xi = xr + mr * sr - mi * si, xi + mr * si + mi * sr
            cr, ci = carry[0], carry[1]
            hr = xr + m[6] * cr - m[7] * ci
            hi = xi + m[6] * ci + m[7] * cr
            dst_re[rows, ln] = hr
            dst_im[rows, ln] = hi
            edge = 0 if reverse else 7
            out = (jnp.broadcast_to(hr[edge:edge + 1, :], hr.shape), jnp.broadcast_to(hi[edge:edge + 1, :], hi.shape))
            if extra is not None:
                h_re, h_im, halo_re, halo_im, first, _, _ = extra
                prev = pl.ds(pl.multiple_of(jnp.maximum(grp - 1, 0) * 8, 8), 8)
                use_halo = grp == 0
                pr = jnp.where(use_halo, halo_re[:, ln] * first, h_re[prev, ln])
                pi = jnp.where(use_halo, halo_im[:, ln] * first, h_im[prev, ln])
                qr = jnp.where(row == 0, jnp.broadcast_to(pr[7:8, :], pr.shape), pltpu.roll(h_re[rows, ln], 1, 0))
                qi = jnp.where(row == 0, jnp.broadcast_to(pi[7:8, :], pi.shape), pltpu.roll(h_im[rows, ln], 1, 0))
                out = out + (carry[2] + hr * qr + hi * qi, carry[3] + hi * qr - hr * qi)
            return out

        init = (carry_re[:, ln], carry_im[:, ln])
        if extra is not None:
            init = init + (extra[5][:, ln], extra[6][:, ln])
        fin = lax.fori_loop(0, ng, step, init)
        carry_re[:, ln] = fin[0]
        carry_im[:, ln] = fin[1]
        if extra is not None:
            extra[5][:, ln] = fin[2]
            extra[6][:, ln] = fin[3]


def _s5_fwd(xb, bb_re, bb_im, c_re, c_im, coef, ts):
    s, w = xb.shape
    nb = bb_re.shape[0]
    nl = nb * 512

    def body(x_ref, bre_ref, bim_ref, cre_ref, cim_ref, coef_ref, hre_ref, him_ref, y_ref, ure, uim, car_re, car_im):
        @pl.when(pl.program_id(0) == 0)
        def _():
            car_re[...] = jnp.zeros_like(car_re)
            car_im[...] = jnp.zeros_like(car_im)

        for b in range(nb):
            xs = x_ref[:, b * 128:(b + 1) * 128].astype(BF16)
            ure[:, b * 512:(b + 1) * 512] = jnp.dot(xs, bre_ref[b], preferred_element_type=F32)
            uim[:, b * 512:(b + 1) * 512] = jnp.dot(xs, bim_ref[b], preferred_element_type=F32)
        _scan_tile(ure, uim, hre_ref, him_ref, coef_ref, car_re, car_im, ts, False)
        for b in range(nb):
            hr = hre_ref[:, b * 512:(b + 1) * 512].astype(BF16)
            hi = him_ref[:, b * 512:(b + 1) * 512].astype(BF16)
            y_ref[:, b * 128:(b + 1) * 128] = (jnp.dot(hr, cre_ref[b], preferred_element_type=F32)
                                               - jnp.dot(hi, cim_ref[b], preferred_element_type=F32))

    row = lambda wd: pl.BlockSpec((ts, wd), lambda i: (i, 0))
    return pl.pallas_call(
        body, name="s5_fwd", grid=(s // ts,),
        in_specs=[row(w), _whole(bb_re), _whole(bb_im), _whole(c_re), _whole(c_im), _whole(coef)],
        out_specs=[row(nl), row(nl), row(w)],
        out_shape=[jax.ShapeDtypeStruct((s, nl), F32), jax.ShapeDtypeStruct((s, nl), F32),
                   jax.ShapeDtypeStruct((s, w), F32)],
        scratch_shapes=[pltpu.VMEM((ts, nl), F32), pltpu.VMEM((ts, nl), F32), pltpu.VMEM((8, nl), F32),
                        pltpu.VMEM((8, nl), F32)],
        compiler_params=_cparams(("arbitrary",)))(xb, bb_re, bb_im, c_re, c_im, coef)


def _s5_bwd(dy, xb, h_re, h_im, bb_re, bb_im, c_re, c_im, coef, ts):
    s, w = xb.shape
    nb = bb_re.shape[0]
    nl = nb * 512
    nt = s // ts

    def body(dy_ref, x_ref, hre_ref, him_ref, halo_re, halo_im, bre_ref, bim_ref, cre_ref, cim_ref, coef_ref,
             dx_ref, dbre_ref, dbim_ref, dcre_ref, dcim_ref, dare_ref, daim_ref, gre, gim, car_re, car_im):
        i = pl.program_id(0)

        @pl.when(i == 0)
        def _():
            for r in (car_re, car_im, dbre_ref, dbim_ref, dcre_ref, dcim_ref, dare_ref, daim_ref):
                r[...] = jnp.zeros_like(r)

        for b in range(nb):
            dyb = dy_ref[:, b * 128:(b + 1) * 128].astype(BF16)
            gre[:, b * 512:(b + 1) * 512] = lax.dot_general(dyb, cre_ref[b], (((1,), (1,)), ((), ())),
                                                            preferred_element_type=F32)
            gim[:, b * 512:(b + 1) * 512] = -lax.dot_general(dyb, cim_ref[b], (((1,), (1,)), ((), ())),
                                                             preferred_element_type=F32)
            hr = hre_ref[:, b * 512:(b + 1) * 512].astype(BF16)
            hi = him_ref[:, b * 512:(b + 1) * 512].astype(BF16)
            dcre_ref[b] += lax.dot_general(hr, dyb, (((0,), (0,)), ((), ())), preferred_element_type=F32)
            dcim_ref[b] -= lax.dot_general(hi, dyb, (((0,), (0,)), ((), ())), preferred_element_type=F32)
        first = (i != nt - 1).astype(F32)
        _scan_tile(gre, gim, gre, gim, coef_ref, car_re, car_im, ts, True,
                   extra=(hre_ref, him_ref, halo_re, halo_im, first, dare_ref, daim_ref))
        for b in range(nb):
            gr = gre[:, b * 512:(b + 1) * 512].astype(BF16)
            gi = gim[:, b * 512:(b + 1) * 512].astype(BF16)
            xs = x_ref[:, b * 128:(b + 1) * 128].astype(BF16)
            dx_ref[:, b * 128:(b + 1) * 128] = (
                lax.dot_general(gr, bre_ref[b], (((1,), (1,)), ((), ())), preferred_element_type=F32)
                + lax.dot_general(gi, bim_ref[b], (((1,), (1,)), ((), ())), preferred_element_type=F32))
            dbre_ref[b] += lax.dot_general(xs, gr, (((0,), (0,)), ((), ())), preferred_element_type=F32)
            dbim_ref[b] += lax.dot_general(xs, gi, (((0,), (0,)), ((), ())), preferred_element_type=F32)

    row = lambda wd: pl.BlockSpec((ts, wd), lambda i: (nt - 1 - i, 0))
    halo = pl.BlockSpec((8, nl), lambda i: (jnp.maximum((nt - 1 - i) * (ts // 8) - 1, 0), 0))
    return pl.pallas_call(
        body, name="s5_bwd", grid=(nt,),
        in_specs=[row(w), row(w), row(nl), row(nl), halo, halo, _whole(bb_re), _whole(bb_im), _whole(c_re),
                  _whole(c_im), _whole(coef)],
        out_specs=[row(w), _whole(bb_re), _whole(bb_im), _whole(c_re), _whole(c_im),
                   pl.BlockSpec((8, nl), lambda i: (0, 0)), pl.BlockSpec((8, nl), lambda i: (0, 0))],
        out_shape=[jax.ShapeDtypeStruct((s, w), F32), jax.ShapeDtypeStruct(bb_re.shape, F32),
                   jax.ShapeDtypeStruct(bb_im.shape, F32), jax.ShapeDtypeStruct(c_re.shape, F32),
                   jax.ShapeDtypeStruct(c_im.shape, F32), jax.ShapeDtypeStruct((8, nl), F32),
                   jax.ShapeDtypeStruct((8, nl), F32)],
        scratch_shapes=[pltpu.VMEM((ts, nl), F32), pltpu.VMEM((ts, nl), F32), pltpu.VMEM((8, nl), F32),
                        pltpu.VMEM((8, nl), F32)],
        compiler_params=_cparams(("arbitrary",)))(dy, xb, h_re, h_im, h_re, h_im, bb_re, bb_im, c_re, c_im, coef)


def _final(x, mo, target, fg, ts):
    s, d = x.shape

    def f(x, mo, fg, tgt):
        y = _rms(x + mo, fg)[0]
        err = y - tgt
        return 0.5 * jnp.sum(jnp.mean(err * err, axis=-1, keepdims=True), axis=0, keepdims=True)

    def body(x_ref, mo_ref, t_ref, fg_ref, dh_ref, dfg_ref, loss_ref):
        @pl.when(pl.program_id(0) == 0)
        def _():
            dfg_ref[...] = jnp.zeros_like(dfg_ref)
            loss_ref[...] = jnp.zeros_like(loss_ref)

        loss, vjp = jax.vjp(f, x_ref[...], mo_ref[...], fg_ref[...], t_ref[...])
        _, dmo, dfg, _ = vjp(jnp.ones((1, 1), F32))
        dh_ref[...] = dmo
        dfg_ref[...] += dfg
        loss_ref[...] += jnp.broadcast_to(loss, loss_ref.shape)

    row = pl.BlockSpec((ts, d), lambda i: (i, 0))
    return pl.pallas_call(
        body, name="final", grid=(s // ts,), in_specs=[row, row, row, _whole(fg)],
        out_specs=[row, _whole(fg), pl.BlockSpec((8, 128), lambda i: (0, 0))],
        out_shape=[jax.ShapeDtypeStruct((s, d), F32), jax.ShapeDtypeStruct(fg.shape, F32),
                   jax.ShapeDtypeStruct((8, 128), F32)],
        compiler_params=_cparams(("arbitrary",)))(x, mo, target, fg)


N_CHIPS = 4


def _other_chips(x, y):
    return [((1 - x, y), 2 * (1 - x) + y), ((x, 1 - y), 2 * x + 1 - y), ((1 - x, 1 - y), 2 * (1 - x) + 1 - y)]


def _comm_call(body, name, srcs, out_shapes, n_sems):
    n = len(srcs)
    return pl.pallas_call(
        body, name=name, in_specs=[pl.BlockSpec(memory_space=pl.ANY)] * n,
        out_specs=[pl.BlockSpec(memory_space=pl.ANY)] * n, out_shape=out_shapes,
        scratch_shapes=[pltpu.SemaphoreType.DMA((n, n_sems)), pltpu.SemaphoreType.DMA((n, n_sems)),
                        pltpu.SemaphoreType.DMA((n,))],
        compiler_params=pltpu.CompilerParams(has_side_effects=True))(*srcs)


def _gather(srcs, name):
    n = len(srcs)

    def body(*refs):
        src, out = refs[:n], refs[n:2 * n]
        send_sems, recv_sems, local_sems = refs[2 * n:]
        x, y, c = lax.axis_index("x"), lax.axis_index("y"), lax.axis_index("c")
        me, sib_slot, sib = 4 * x + 2 * y + c, 4 * x + 2 * y + 1 - c, (x, y, 1 - c)
        chips = _other_chips(x, y)

        def cp(a, k, src_ref, slot, to):
            return pltpu.make_async_remote_copy(
                src_ref=src_ref, dst_ref=out[a].at[slot], send_sem=send_sems.at[a, k], recv_sem=recv_sems.at[a, k],
                device_id=to, device_id_type=pl.DeviceIdType.MESH)

        local = [pltpu.make_async_copy(src[a], out[a].at[me], local_sems.at[a]) for a in range(n)]
        first = [cp(a, 0, src[a], me, sib) for a in range(n)]
        first += [cp(a, 1 + j, src[a], me, (*chip, c)) for j, (chip, _) in enumerate(chips) for a in range(n)]
        for d in local + first:
            d.start()
        passed = []
        for j, (chip, q) in enumerate(chips):
            for a in range(n):
                cp(a, 1 + j, src[a], 2 * q + c, sib).wait_recv()
                fwd = cp(a, 4 + j, out[a].at[2 * q + c], 2 * q + c, sib)
                fwd.start()
                passed.append(fwd)
        for a in range(n):
            cp(a, 0, src[a], sib_slot, sib).wait_recv()
        for j, (chip, q) in enumerate(chips):
            for a in range(n):
                cp(a, 4 + j, src[a], 2 * q + 1 - c, sib).wait_recv()
        for d in first + passed:
            d.wait_send()
        for d in local:
            d.wait()

    return _comm_call(body, name, srcs, [jax.ShapeDtypeStruct((N_DEV,) + s.shape, s.dtype) for s in srcs], 7)


def _all_peers(x, y, c):
    out = []
    for k in range(1, N_DEV):
        px = 1 - x if k & 4 else x
        py = 1 - y if k & 2 else y
        pc = 1 - c if k & 1 else c
        out.append(((px, py, pc), 4 * px + 2 * py + pc))
    return out


_HBM = pl.BlockSpec(memory_space=pltpu.HBM)
_SEM = pl.BlockSpec(memory_space=pltpu.SEMAPHORE)
_DATAFLOW = pltpu.SideEffectType.DATAFLOW_SIDE_EFFECTING


def _send_whole(ref, slot):
    return ref


def _send_slot(ref, slot):
    return ref.at[slot]


def _direct_copies(src, land, send_sems, recv_sems, picks, arriving):
    x, y, c = lax.axis_index("x"), lax.axis_index("y"), lax.axis_index("c")
    me = 4 * x + 2 * y + c
    out = []
    for k, (pos, slot) in enumerate(_all_peers(x, y, c)):
        for a in range(len(src)):
            sem = a * (N_DEV - 1) + k
            out.append(pltpu.make_async_remote_copy(
                src_ref=picks[a](src[a], slot), dst_ref=land[a].at[slot if arriving else me],
                send_sem=send_sems.at[sem], recv_sem=recv_sems.at[sem], device_id=pos,
                device_id_type=pl.DeviceIdType.MESH))
    return out


def _direct_start(srcs, lands, picks, name):
    n = len(srcs)

    def body(*refs):
        src, land = refs[:n], refs[n:2 * n]
        send_sems, recv_sems = refs[2 * n], refs[2 * n + 1]
        for push in _direct_copies(src, land, send_sems, recv_sems, picks, False):
            push.start()
        refs[-1][...] = jnp.zeros_like(refs[-1])

    arrays = [pltpu.with_memory_space_constraint(t, pltpu.HBM) for t in list(srcs) + list(lands)]
    outs = pl.pallas_call(
        body, name=name, in_specs=[_HBM] * (2 * n),
        out_specs=(_SEM, _SEM, *[_HBM] * (2 * n), pl.BlockSpec(memory_space=pltpu.VMEM)),
        out_shape=(pltpu.SemaphoreType.DMA((n * (N_DEV - 1),)), pltpu.SemaphoreType.DMA((n * (N_DEV - 1),)),
                   *[pltpu.HBM(t.shape, t.dtype) for t in arrays], jax.ShapeDtypeStruct((8, 128), F32)),
        input_output_aliases={i: 2 + i for i in range(2 * n)},
        compiler_params=pltpu.CompilerParams(has_side_effects=_DATAFLOW))(*arrays)
    return outs[:-1], outs[-1]


def _direct_wait(started, picks, after, name):
    send_sems, recv_sems, *thru = started
    n = len(thru) // 2

    def body(*refs):
        src, land = refs[:n], refs[n:2 * n]
        for arrive in _direct_copies(src, land, refs[2 * n], refs[2 * n + 1], picks, True):
            arrive.wait_send()
            arrive.wait_recv()

    outs = pl.pallas_call(
        body, name=name, in_specs=[_HBM] * (2 * n) + [_SEM, _SEM, pl.BlockSpec(memory_space=pl.ANY)],
        out_specs=[_HBM] * (2 * n), out_shape=[pltpu.HBM(t.shape, t.dtype) for t in thru],
        input_output_aliases={i: i for i in range(2 * n)},
        compiler_params=pltpu.CompilerParams(has_side_effects=_DATAFLOW))(*thru, send_sems, recv_sems, after)
    return outs[:n], outs[n:]


def _pair_scatter(gs, name):
    n = len(gs)

    def body(*refs):
        src, out = refs[:n], refs[n:2 * n]
        send_sems, recv_sems, _ = refs[2 * n:]
        x, y, c = lax.axis_index("x"), lax.axis_index("y"), lax.axis_index("c")
        sends = []
        for q in range(N_CHIPS):
            for a in range(n):
                d = pltpu.make_async_remote_copy(
                    src_ref=src[a].at[2 * q + 1 - c], dst_ref=out[a].at[q], send_sem=send_sems.at[a, q],
                    recv_sem=recv_sems.at[a, q], device_id=(x, y, 1 - c), device_id_type=pl.DeviceIdType.MESH)
                d.start()
                sends.append(d)
        for d in sends:
            d.wait_recv()
        for d in sends:
            d.wait_send()

    return _comm_call(body, name, gs, [jax.ShapeDtypeStruct((N_CHIPS,) + g.shape[1:], g.dtype) for g in gs], N_CHIPS)


def _cross_scatter(ps, name):
    n = len(ps)

    def body(*refs):
        src, out = refs[:n], refs[n:2 * n]
        send_sems, recv_sems, local_sems = refs[2 * n:]
        x, y, c = lax.axis_index("x"), lax.axis_index("y"), lax.axis_index("c")
        mine = 2 * x + y
        chips = _other_chips(x, y)
        local = [pltpu.make_async_copy(src[a].at[mine], out[a].at[mine], local_sems.at[a]) for a in range(n)]
        for d in local:
            d.start()
        sends = []
        for j, (chip, q) in enumerate(chips):
            for a in range(n):
                d = pltpu.make_async_remote_copy(
                    src_ref=src[a].at[q], dst_ref=out[a].at[mine], send_sem=send_sems.at[a, j],
                    recv_sem=recv_sems.at[a, j], device_id=(*chip, c), device_id_type=pl.DeviceIdType.MESH)
                d.start()
                sends.append(d)
        for j, (chip, q) in enumerate(chips):
            for a in range(n):
                pltpu.make_async_remote_copy(
                    src_ref=src[a].at[q], dst_ref=out[a].at[q], send_sem=send_sems.at[a, j],
                    recv_sem=recv_sems.at[a, j], device_id=(*chip, c), device_id_type=pl.DeviceIdType.MESH).wait_recv()
        for d in sends:
            d.wait_send()
        for d in local:
            d.wait()

    return _comm_call(body, name, ps, [jax.ShapeDtypeStruct(p.shape, p.dtype) for p in ps], 3)


def _pair_sum(g, got, c_idx, out_dtype, name):
    _, r, c = g.shape
    lanes = -(-c // 128) * 128
    tr = _pick_rows(r, max(8, (2 * 1024 * 1024) // (lanes * 4)))
    g4 = g.reshape((N_CHIPS, 2) + g.shape[1:])

    def body(c_ref, g_ref, got_ref, o_ref):
        o_ref[...] = (g_ref[...] + got_ref[...]).astype(o_ref.dtype)

    return pl.pallas_call(
        body, name=name,
        grid_spec=pltpu.PrefetchScalarGridSpec(
            num_scalar_prefetch=1, grid=(N_CHIPS, r // tr),
            in_specs=[pl.BlockSpec((None, None, tr, c), lambda q, i, cr: (q, cr[0], i, 0)),
                      pl.BlockSpec((None, tr, c), lambda q, i, cr: (q, i, 0))],
            out_specs=pl.BlockSpec((None, tr, c), lambda q, i, cr: (q, i, 0))),
        out_shape=jax.ShapeDtypeStruct(got.shape, out_dtype),
        compiler_params=_cparams(("parallel", "parallel")))(c_idx, g4, got)


def _pick_rows(r, pref):
    t = (min(pref, r) // 8) * 8
    while t >= 8:
        if r % t == 0:
            return t
        t -= 8
    return r


def _w_in_from_shards(t, lo, hi):
    n, r, cs = t.shape
    tr = _pick_rows(r, 256)
    wm = n * cs - (hi - lo)

    def body(t_ref, m_ref, b_ref):
        full = jnp.concatenate([t_ref[j] for j in range(n)], axis=1)
        m_ref[...] = jnp.concatenate([full[:, :lo], full[:, hi:]], axis=1)
        b_ref[...] = jnp.concatenate([full[:, lo:hi], jnp.zeros((tr, 128 - (hi - lo)), full.dtype)], axis=1)

    return pl.pallas_call(
        body, name="w_in_layout", grid=(r // tr,), in_specs=[pl.BlockSpec((n, tr, cs), lambda i: (0, i, 0))],
        out_specs=[pl.BlockSpec((tr, wm), lambda i: (i, 0)), pl.BlockSpec((tr, 128), lambda i: (i, 0))],
        out_shape=[jax.ShapeDtypeStruct((r, wm), t.dtype), jax.ShapeDtypeStruct((r, 128), t.dtype)],
        compiler_params=_cparams(("parallel",)))(t)


def _w_in_to_shards(gm, gb, lo, hi, dtype, first_row=0, name="dw_in_layout"):
    r, wm = gm.shape
    cs = (wm + hi - lo) // N_DEV
    tr = _pick_rows(r, 64)
    assert first_row % tr == 0
    b0 = first_row // tr

    def body(m_ref, b_ref, o_ref):
        m = m_ref[...]
        full = jnp.concatenate([m[:, :lo], b_ref[:, :hi - lo], m[:, lo:]], axis=1)
        for j in range(N_DEV):
            o_ref[j] = full[:, j * cs:(j + 1) * cs].astype(o_ref.dtype)

    return pl.pallas_call(
        body, name=name, grid=(r // tr,),
        in_specs=[pl.BlockSpec((tr, wm), lambda i: (i, 0)), pl.BlockSpec((tr, 128), lambda i: (i + b0, 0))],
        out_specs=pl.BlockSpec((N_DEV, tr, cs), lambda i: (0, i, 0)),
        out_shape=jax.ShapeDtypeStruct((N_DEV, r, cs), dtype), compiler_params=_cparams(("parallel",)))(gm, gb)


def _pack(arrs, dtype, lead=()):
    nlead = len(lead)
    flat = jnp.concatenate([a.astype(dtype).reshape(lead + (-1,)) for a in arrs], axis=nlead)
    n = flat.shape[-1]
    unit = PACK_WIDTH * PACK_ROWS
    pad = (-n) % unit
    flat = jnp.pad(flat, [(0, 0)] * nlead + [(0, pad)])
    return flat.reshape(lead + ((n + pad) // PACK_WIDTH, PACK_WIDTH))


def _unpack(buf, shapes, lead=()):
    flat = buf.reshape(lead + (-1,))
    out, off = [], 0
    for shp in shapes:
        n = math.prod(shp)
        out.append(flat[..., off:off + n].reshape(lead + tuple(shp)))
        off += n
    return out


def _adam_math(w, g, m, v):
    m = ADAM_B1 * m + (1.0 - ADAM_B1) * g
    v = ADAM_B2 * v + (1.0 - ADAM_B2) * (g * g)
    m_hat = m / (1.0 - ADAM_B1 ** ADAM_STEP)
    v_hat = v / (1.0 - ADAM_B2 ** ADAM_STEP)
    delta = -ADAM_LR * (m_hat / (jnp.sqrt(v_hat) + ADAM_EPS) + ADAM_WD * w)
    return delta, m, v


def _sum_adam(parts, w, m, v, name, own=None, me=None):
    r, c = w.shape
    parts_list = list(parts) if isinstance(parts, (list, tuple)) else [parts]
    own_list = list(own) if isinstance(own, (list, tuple)) else [own]
    nparts = parts_list[0].shape[0]
    lanes = -(-c // 128) * 128
    tr = _pick_rows(r // len(parts_list), max(8, (6 * 1024 * 1024) // (nparts * lanes * 4)))

    def finish(g, w_ref, m_ref, v_ref, g_ref, d_ref, nm_ref, nv_ref):
        d, nm, nv = _adam_math(w_ref[...], g, m_ref[...], v_ref[...])
        g_ref[...] = g
        d_ref[...] = d
        nm_ref[...] = nm
        nv_ref[...] = nv

    out_shape = [jax.ShapeDtypeStruct((r, c), F32)] * 4
    if own is None:
        def body(p_ref, *rest):
            g = p_ref[0].astype(F32)
            for j in range(1, nparts):
                g = g + p_ref[j].astype(F32)
            finish(g, *rest)

        row = pl.BlockSpec((tr, c), lambda i: (i, 0))
        return pl.pallas_call(
            body, name=name, grid=(r // tr,),
            in_specs=[pl.BlockSpec((nparts, tr, c), lambda i: (0, i, 0)), row, row, row],
            out_specs=[row] * 4, out_shape=out_shape, compiler_params=_cparams(("parallel",)))(parts, w, m, v)

    nch = len(parts_list)
    tpc = r // nch // tr

    def body(me_ref, *refs):
        p_refs, o_refs, rest = refs[:nch], refs[nch:2 * nch], refs[2 * nch:]
        i = pl.program_id(0)

        def of_chunk(vals):
            out = vals[0]
            for q in range(1, nch):
                out = jnp.where(i >= q * tpc, vals[q], out)
            return out

        mine = of_chunk([o[...].astype(F32) for o in o_refs])
        g = None
        for j in range(nparts):
            t = jnp.where(me_ref[0] == j, mine, of_chunk([p[j].astype(F32) for p in p_refs]))
            g = t if g is None else g + t
        finish(g, *rest)

    row = pl.BlockSpec((tr, c), lambda i, me_ref: (i, 0))
    step = lambda i, q: jnp.clip(i - q * tpc, 0, tpc - 1)
    return pl.pallas_call(
        body, name=name,
        grid_spec=pltpu.PrefetchScalarGridSpec(
            num_scalar_prefetch=1, grid=(r // tr,),
            in_specs=[pl.BlockSpec((nparts, tr, c), lambda i, me_ref, q=q: (0, step(i, q), 0)) for q in range(nch)]
            + [pl.BlockSpec((None, tr, c), lambda i, me_ref, q=q: (me_ref[0], step(i, q), 0)) for q in range(nch)]
            + [row, row, row],
            out_specs=[row] * 4),
        out_shape=out_shape, compiler_params=_cparams(("parallel",)))(me, *parts_list, *own_list, w, m, v)


def _block_diag(t):
    nb, g, a, b = t.shape
    eye = jnp.eye(g, dtype=t.dtype)
    return jnp.einsum('ngab,gh->ngahb', t, eye).reshape(nb, g * a, g * b)


def _diag_blocks(t, a, b):
    nb = t.shape[0]
    g = S5_GROUPS_PER_BLOCK
    t = t.reshape(nb, g, a, g, b)
    return jnp.stack([t[:, j, :, j, :] for j in range(g)], axis=1)


def _local_step(x, mem, target, p, late_weights=None, early_grads=None, last_grads=None):
    s, d = x.shape
    gw = d // 2
    nh = gw // GDN_HEAD_DIM
    ng = gw // S5_GROUP
    nb = ng // S5_GROUPS_PER_BLOCK
    nl = ng * S5_STATE
    ts = min(256, s)
    nt = s // ts
    grads = {}

    w_main, w_ba = p['w_main'], p['w_ba']
    CB_ZA, CB_XB, CB_ZB, CB_QC, CB_ZC, CB_G = 3, 4, 5, 6, 7, 8

    u = _tile_fwd(_rms, "rms_fwd", nt, [_rt(x, ts)], [p['norm_g']],
                  [((s, d), BF16, (ts, d), lambda i: (i, 0))])[0]
    proj = _mm(u, w_main, tm=1024, tn=2048, tk=512, name="proj_main")
    pba = _mm(u, w_ba, name="proj_ba")
    if late_weights is not None:
        p = {**p, **late_weights(proj)}

    conv_w = p['conv_w']
    col = lambda arr, cb: (arr, (s, GDN_HEAD_DIM), lambda i, cb=cb: (0, cb + i))
    qkv = []
    for j, mode in enumerate(('q', 'k', 'v')):
        off = j * nh
        qkv.append(_tile_fwd(
            _gdn_pre(mode), "gdn_pre_" + mode, nh, [col(proj, off), (conv_w, (CONV_WIDTH, GDN_HEAD_DIM), lambda i, off=off: (0, off + i))],
            [], [((s, gw), F32, (s, GDN_HEAD_DIM), lambda i: (0, i))])[0])
    q, k, v = qkv
    lane = jnp.arange(128)[:, None]
    colh = jnp.arange(gw)[None, :] // GDN_HEAD_DIM
    e_beta = (lane == colh).astype(F32)
    e_g = (lane == colh + nh).astype(F32)
    alog_row = jnp.pad(p['gdn_a_log'], ((0, 0), (nh, 128 - 2 * nh)))
    dtb_row = jnp.pad(p['gdn_dt_bias'], ((0, 0), (nh, 128 - 2 * nh)))
    row_gw = lambda: ((s, gw), F32, (ts, gw), lambda i: (i, 0))
    betab, gb = _tile_fwd(_gdn_gates, "gdn_gates", nt, [_rt(pba, ts)], [alog_row, dtb_row, e_beta, e_g],
                          [row_gw(), row_gw()])
    *intra, t_inv = _gdn_intra_fwd(q, k, v, gb, betab)
    o_raw, states = _gdn_inter_fwd(*intra, gb)
    ga = _tile_fwd(_gdn_post, "gdn_post", nt, [_rt(o_raw, ts), _rt(proj, ts, CB_ZA, gw)], [p['gdn_norm_g']],
                   [((s, gw), BF16, (ts, gw), lambda i: (i, 0))])[0]

    e_rep = (jnp.arange(S5_STATE)[:, None] == jnp.arange(S5_STATE * S5_GROUP)[None, :] // S5_GROUP).astype(F32)
    s5_in = [p['s5_lambda_re'], p['s5_lambda_im'], p['s5_log_dt'].reshape(ng, 1),
             p['s5_b_re'].reshape(ng, S5_STATE * S5_GROUP), p['s5_b_im'].reshape(ng, S5_STATE * S5_GROUP), e_rep]
    one = lambda shp: (shp, F32, shp, lambda i, n=len(shp): (0,) * n)
    ab_re, ab_im, bbr, bbi = _tile_fwd(_s5_params, "s5_params", 1, [], s5_in,
                                       [one((ng, S5_STATE)), one((ng, S5_STATE)), one((ng, S5_STATE * S5_GROUP)),
                                        one((ng, S5_STATE * S5_GROUP))])
    coef = _s5_coef(ab_re.reshape(1, nl), ab_im.reshape(1, nl))
    to_bd_b = lambda t: _block_diag(t.reshape(nb, S5_GROUPS_PER_BLOCK, S5_STATE, S5_GROUP).transpose(0, 1, 3, 2))
    to_bd_c = lambda t: _block_diag(t.reshape(nb, S5_GROUPS_PER_BLOCK, S5_GROUP, S5_STATE).transpose(0, 1, 3, 2))
    bbd_re, bbd_im = to_bd_b(bbr).astype(BF16), to_bd_b(bbi).astype(BF16)
    cbd_re, cbd_im = to_bd_c(p['s5_c_re']).astype(BF16), to_bd_c(p['s5_c_im']).astype(BF16)
    xb_arr = lax.slice_in_dim(proj, CB_XB * gw, (CB_XB + 1) * gw, axis=1)
    h_re, h_im, ylin = _s5_fwd(xb_arr, bbd_re, bbd_im, cbd_re, cbd_im, coef, ts)
    gl = _tile_fwd(_s5_post1, "s5_post1", nt, [_rt(ylin, ts), _rt(proj, ts, CB_XB, gw)], [p['s5_d']],
                   [((s, gw), BF16, (ts, gw), lambda i: (i, 0))])[0]
    tglu = _mm(gl, p['s5_w_glu'], b_shards=True, name="s5_glu")
    gbb = _tile_fwd(_s5_post2, "s5_post2", nt, [_rt(tglu, ts), _rt(proj, ts, CB_ZB, gw)], [],
                    [((s, gw), BF16, (ts, gw), lambda i: (i, 0))])[0]

    m_len = mem.shape[0]
    mem_n = _tile_fwd(_rms, "mem_rms", 1, [_rt(mem, m_len)], [p['mem_norm_g']],
                      [((m_len, d), BF16, (m_len, d), lambda i: (i, 0))])[0]
    kv = _mm(mem_n, p['w_kv_mem'], name="mem_kv")
    gcc = _tile_fwd(_attn, "attn", nt, [_rt(proj, ts, CB_QC, gw), _rt(proj, ts, CB_ZC, gw)], [kv],
                    [((s, gw), BF16, (ts, gw), lambda i: (i, 0))])[0]

    p_a = _mm(ga, p['w_br_a'], b_shards=True, name="br_a")
    p_b = _mm(gbb, p['w_br_b'], b_shards=True, name="br_b")
    p_c = _mm(gcc, p['w_br_c'], b_shards=True, name="br_c")
    gate_acts = [_rt(proj, ts, CB_G // 2 + j, d) for j in range(3)]
    merged = _tile_fwd(_merge, "merge", nt, gate_acts + [_rt(p_a, ts), _rt(p_b, ts), _rt(p_c, ts)], [],
                       [((s, d), BF16, (ts, d), lambda i: (i, 0))])[0]
    mo = _mm(merged, p['w_out'], name="out_proj")
    dh, dfg, loss = _final(x, mo, target, p['final_g'].reshape(1, d), ts)
    grads['final_g'] = dfg.reshape(d)

    dmerged = _mm(dh, p['w_out'], tb=True, name="d_merged")
    grads['w_out'] = _mm(merged, dh, ta=True, name="dw_out")
    row_d = lambda dt: ((s, d), dt, (ts, d), lambda i: (i, 0))
    dg0, dg1, dg2, dpa, dpb, dpc = _tile_bwd(
        _merge, "merge_bwd", nt, gate_acts + [_rt(p_a, ts), _rt(p_b, ts), _rt(p_c, ts)], [], [_rt(dmerged, ts)],
        [row_d(BF16)] * 6, [])
    dga = _mm(dpa, p['w_br_a'], tb=True, b_shards=True, name="d_ga")
    dgbb = _mm(dpb, p['w_br_b'], tb=True, b_shards=True, name="d_gb")
    dgcc = _mm(dpc, p['w_br_c'], tb=True, b_shards=True, name="d_gc")
    grads['w_br_a'] = _mm(ga, dpa, ta=True, out_shards=N_DEV, name="dw_br_a")
    grads['w_br_b'] = _mm(gbb, dpb, ta=True, out_shards=N_DEV, name="dw_br_b")
    grads['w_br_c'] = _mm(gcc, dpc, ta=True, out_shards=N_DEV, name="dw_br_c")
    row_h = lambda dt: ((s, gw), dt, (ts, gw), lambda i: (i, 0))

    dqc, dzc, dkv = _tile_bwd(_attn, "attn_bwd", nt, [_rt(proj, ts, CB_QC, gw), _rt(proj, ts, CB_ZC, gw)], [kv],
                              [_rt(dgcc, ts)], [row_h(BF16), row_h(BF16)], [True])
    grads['w_kv_mem'] = _mm(mem_n, dkv, ta=True, name="dw_kv")
    dmem_n = _mm(dkv, p['w_kv_mem'], tb=True, name="d_mem_n")
    grads['mem_norm_g'] = _tile_bwd(_rms, "mem_rms_bwd", 1, [_rt(mem, m_len)], [p['mem_norm_g']],
                                    [_rt(dmem_n, m_len)], [None], [True])[0]

    dtglu, dzb = _tile_bwd(_s5_post2, "s5_post2_bwd", nt, [_rt(tglu, ts), _rt(proj, ts, CB_ZB, gw)], [],
                           [_rt(dgbb, ts)], [((s, 2 * gw), BF16, (ts, 2 * gw), lambda i: (i, 0)), row_h(BF16)], [])
    grads['s5_w_glu'] = _mm(gl, dtglu, ta=True, out_shards=N_DEV, name="dw_glu")
    s5_d = p['s5_d']
    if early_grads is not None:
        s5_d = s5_d + early_grads(grads)[:1, :1]
    dgl = _mm(dtglu, p['s5_w_glu'], tb=True, b_shards=True, name="d_gl")
    dylin, dxb1, dd = _tile_bwd(_s5_post1, "s5_post1_bwd", nt, [_rt(ylin, ts), _rt(proj, ts, CB_XB, gw)],
                                [s5_d], [_rt(dgl, ts)], [row_h(F32), row_h(F32)], [True])
    grads['s5_d'] = dd
    dxb2, dbbd_re, dbbd_im, dcbd_re, dcbd_im, da_re, da_im = _s5_bwd(dylin, xb_arr, h_re, h_im, bbd_re, bbd_im,
                                                                    cbd_re, cbd_im, coef, ts)
    from_bd_b = lambda t: _diag_blocks(t, S5_GROUP, S5_STATE).transpose(0, 1, 3, 2).reshape(ng, S5_STATE * S5_GROUP)
    from_bd_c = lambda t: _diag_blocks(t, S5_STATE, S5_GROUP).transpose(0, 1, 3, 2).reshape(1, ng, S5_GROUP, S5_STATE)
    grads['s5_c_re'], grads['s5_c_im'] = from_bd_c(dcbd_re), from_bd_c(dcbd_im)
    s5_cts = [jnp.sum(da_re, axis=0).reshape(ng, S5_STATE), jnp.sum(da_im, axis=0).reshape(ng, S5_STATE),
              from_bd_b(dbbd_re), from_bd_b(dbbd_im)]
    dlr, dli, dlogdt, dbr, dbi = _tile_bwd(_s5_params, "s5_params_bwd", 1, [], s5_in,
                                           [(c, c.shape, lambda i: (0, 0)) for c in s5_cts], [],
                                           [True, True, True, True, True, False])
    grads['s5_lambda_re'], grads['s5_lambda_im'] = dlr[None], dli[None]
    grads['s5_log_dt'] = dlogdt.reshape(1, ng)
    grads['s5_b_re'] = dbr.reshape(1, ng, S5_STATE, S5_GROUP)
    grads['s5_b_im'] = dbi.reshape(1, ng, S5_STATE, S5_GROUP)
    dxb = (dxb1 + dxb2).astype(BF16)

    do_raw, dza, dgng = _tile_bwd(_gdn_post, "gdn_post_bwd", nt, [_rt(o_raw, ts), _rt(proj, ts, CB_ZA, gw)],
                                  [p['gdn_norm_g']], [_rt(dga, ts)], [row_h(F32), row_h(BF16)], [True])
    grads['gdn_norm_g'] = dgng
    *intra_cts, dgb_inter = _gdn_inter_bwd(*intra, gb, states, do_raw)
    dq, dk, dv, dgb, dbetab = _gdn_intra_bwd(q, k, v, gb, betab, t_inv, intra_cts, dgb_inter)
    dpba, dalog, ddtb = _tile_bwd(_gdn_gates, "gdn_gates_bwd", nt, [_rt(pba, ts)], [alog_row, dtb_row, e_beta, e_g],
                                  [_rt(dbetab, ts), _rt(dgb, ts)], [((s, 128), BF16, (ts, 128), lambda i: (i, 0))],
                                  [True, True, False, False])
    grads['gdn_a_log'] = dalog[:, nh:2 * nh]
    grads['gdn_dt_bias'] = ddtb[:, nh:2 * nh]
    dqkv, dconv = [], []
    for j, (mode, ct) in enumerate((('q', dq), ('k', dk), ('v', dv))):
        off = j * nh
        wspec = (conv_w, (CONV_WIDTH, GDN_HEAD_DIM), lambda i, off=off: (0, off + i))
        dxc, dwc = _tile_bwd(
            _gdn_pre(mode), "gdn_pre_bwd_" + mode, nh, [col(proj, off), wspec], [], [col(ct, 0)],
            [((s, gw), BF16, (s, GDN_HEAD_DIM), lambda i: (0, i)),
             ((CONV_WIDTH, gw), F32, (CONV_WIDTH, GDN_HEAD_DIM), lambda i: (0, i))], [])
        dqkv.append(dxc)
        dconv.append(dwc)
    grads['conv_w'] = jnp.concatenate(dconv, axis=1)

    dproj = jnp.concatenate(dqkv + [dza, dxb, dzb, dqc, dzc, dg0, dg1, dg2], axis=1)
    grads['w_ba'] = _mm(u, dpba, ta=True, name="dw_ba")
    if last_grads is None:
        grads['w_main'] = _mm(u, dproj, ta=True, tm=1024, tn=2048, tk=512, name="dw_main")
    else:
        uu = u
        for h in range(2):
            gm = _mm(uu, dproj, ta=True, tm=1024, tn=2048, tk=512, rows=(h * (d // 2), d // 2), name=f"dw_main_{h}")
            tok = last_grads(h, gm, grads)[:1, :1].astype(BF16)
            if h == 0:
                uu = uu + tok
            else:
                dpba = dpba + tok
    du = _mm(dpba, w_ba, tb=True, name="du_ba")
    du = _mm(dproj, w_main, tb=True, addend=du, tm=512, tn=2048, tk=1024, name="du_main")
    grad_x, dng = _tile_bwd(_rms, "rms_bwd", nt, [_rt(x, ts)], [p['norm_g']], [_rt(du, ts)],
                            [row_d(F32) + (dh,)], [True])
    grads['norm_g'] = dng
    return loss, grad_x, grads


def _to_shards(name, g):
    if SHARDED[name] == 'row':
        return g.reshape((N_DEV, g.shape[0] // N_DEV) + g.shape[1:])
    r, c = g.shape
    return g.reshape(r, N_DEV, c // N_DEV).transpose(1, 0, 2)


def _from_shards(name, t):
    if SHARDED[name] == 'row':
        return t.reshape((t.shape[0] * t.shape[1],) + t.shape[2:])
    n, r, c = t.shape
    return t.transpose(1, 0, 2).reshape(r, n * c)


def _step(x, mem, target, w, m, v):
    sharded = list(SHARDED)
    shard_shapes = {n: tuple(w[n].shape[1:]) for n in sharded}
    d = x.shape[-1]
    ba_lo = 2 * d
    ba_hi = ba_lo + 2 * (d // 2 // GDN_HEAD_DIM)

    w_in_all = _gather([w['w_in'][0].astype(BF16)], "gather_w_in")[0]
    after_w_in = (w_in_all[0, :1, :1] * 0).astype(F32)
    late = [w[n][0].astype(BF16) for n in OVERLAPPED] + [w['conv_w'][0] + after_w_in]
    every = [_send_whole] * len(late)
    lands = [jnp.broadcast_to(t[None], (N_DEV,) + t.shape) for t in late]
    gather_started, token = _direct_start(late, lands, every, "gather_rest_start")
    full = {}
    full['w_main'], full['w_ba'] = _w_in_from_shards(w_in_all, ba_lo, ba_hi)
    for n in REPLICATED:
        full[n] = w[n]
    for n in ('s5_lambda_re', 's5_lambda_im', 's5_c_re', 's5_c_im'):
        full[n] = w[n][0]
    full['norm_g'] = w['norm_g'] + token[:1, :1]

    def late_weights(proj):
        got = dict(zip(OVERLAPPED + ['conv_w'], _direct_wait(gather_started, every, proj, "gather_rest_wait")[1]))
        for n in ('w_kv_mem', 'w_out', 'conv_w'):
            got[n] = _from_shards(n, got[n])
        return got

    slots = [_send_slot] * len(OVERLAPPED)
    scatter_started = []

    def early_grads(grads):
        gs = [_to_shards(n, grads[n]) if SHARDED[n] == 'row' else grads[n] for n in OVERLAPPED]
        started, tok = _direct_start(gs, [lax.empty(g.shape, g.dtype) for g in gs], slots, "scatter_early_start")
        scatter_started.append(started)
        return tok

    def last_grads(h, gm, grads):
        gs = [_w_in_to_shards(gm, grads['w_ba'], ba_lo, ba_hi, BF16, h * gm.shape[0], f"dw_in_layout_{h}")]
        if h == 0:
            gs.append(_to_shards('conv_w', grads['conv_w']))
        started, tok = _direct_start(gs, [lax.empty(g.shape, g.dtype) for g in gs], slots[:len(gs)],
                                     f"scatter_last_start_{h}")
        scatter_started.append(started)
        return tok

    loss, grad_x, grads = _local_step(x[0], mem[0], target[0], full, late_weights, early_grads, last_grads)
    res = {}
    me = (4 * lax.axis_index("x") + 2 * lax.axis_index("y") + lax.axis_index("c")).astype(jnp.int32).reshape(1)

    def update(names, exchanged):
        for n, own, part in zip(names, *exchanged):
            outs = _sum_adam(part, w[n][0], m[n][0], v[n][0], name="adam_" + n, own=own, me=me)
            for kind, t in zip(('grad', 'delta', 'new_m', 'new_v'), outs):
                res[kind, n] = t[None]

    update(OVERLAPPED, _direct_wait(scatter_started[0], slots, grad_x, "scatter_early_wait"))

    small = _pack([grads[n].reshape(w[n].shape) for n in REPLICATED] + [loss[:1, :1]], F32)
    allp = _gather([small], "gather_small")[0]
    zero = jnp.zeros((1, 1), F32)
    outs = _sum_adam(allp, *[_pack([t[n] for n in REPLICATED] + [zero], F32) for t in (w, m, v)], name="adam_small")
    shapes = [w[n].shape for n in REPLICATED] + [(1, 1)]
    for kind, buf in zip(('grad', 'delta', 'new_m', 'new_v'), outs):
        got = _unpack(buf, shapes)
        for n, t in zip(REPLICATED, got):
            res[kind, n] = t
        if kind == 'grad':
            total_loss = got[-1].reshape(())
    (own0, own_conv), (got0, got_conv) = _direct_wait(scatter_started[1], slots[:2], outs[0], "scatter_last_wait_0")
    (own1,), (got1,) = _direct_wait(scatter_started[2], slots[:1], outs[0], "scatter_last_wait_1")
    update(['w_in', 'conv_w'], ([[own0, own1], own_conv], [[got0, got1], got_conv]))
    out = [total_loss, grad_x[None]]
    for kind in ('grad', 'delta', 'new_m', 'new_v'):
        out += [res[kind, n] for n in WEIGHTS]
    return tuple(out)


def kernel(x, mem, norm_g, w_in, conv_w, gdn_a_log, gdn_dt_bias, gdn_norm_g, s5_lambda_re, s5_lambda_im, s5_log_dt, s5_b_re, s5_b_im, s5_c_re, s5_c_im, s5_d, s5_w_glu, mem_norm_g, w_kv_mem, w_br_a, w_br_b, w_br_c, w_out, final_g, loss_target, m_norm_g, m_w_in, m_conv_w, m_gdn_a_log, m_gdn_dt_bias, m_gdn_norm_g, m_s5_lambda_re, m_s5_lambda_im, m_s5_log_dt, m_s5_b_re, m_s5_b_im, m_s5_c_re, m_s5_c_im, m_s5_d, m_s5_w_glu, m_mem_norm_g, m_w_kv_mem, m_w_br_a, m_w_br_b, m_w_br_c, m_w_out, m_final_g, v_norm_g, v_w_in, v_conv_w, v_gdn_a_log, v_gdn_dt_bias, v_gdn_norm_g, v_s5_lambda_re, v_s5_lambda_im, v_s5_log_dt, v_s5_b_re, v_s5_b_im, v_s5_c_re, v_s5_c_im, v_s5_d, v_s5_w_glu, v_mem_norm_g, v_w_kv_mem, v_w_br_a, v_w_br_b, v_w_br_c, v_w_out, v_final_g):
    a = dict(locals())
    w = {n: a[n] for n in WEIGHTS}
    m = {n: a['m_' + n] for n in WEIGHTS}
    v = {n: a['v_' + n] for n in WEIGHTS}
    return _step(x, mem, loss_target, w, m, v)
```

```python
import functools
import math

import jax
import jax.numpy as jnp
from jax import lax
from jax.experimental import pallas as pl
from jax.experimental.pallas import tpu as pltpu

F32 = jnp.float32
BF16 = jnp.bfloat16
HI = lax.Precision.HIGHEST

EPS = 1e-6
CHUNK = 64
GDN_HEAD_DIM = 128
CONV_WIDTH = 4
S5_GROUP = 16
S5_STATE = 64
S5_GROUPS_PER_BLOCK = 8
XA_HEADS = 4
N_DEV = 8
ADAM_LR, ADAM_B1, ADAM_B2, ADAM_EPS, ADAM_WD, ADAM_STEP = 0.001, 0.9, 0.999, 1e-08, 0.01, 10

VMEM_LIMIT_BYTES = 56 * 1024 * 1024
SCAN_LANES = 512
PACK_WIDTH = 512
PACK_ROWS = 256

WEIGHTS = ['norm_g', 'w_in', 'conv_w', 'gdn_a_log', 'gdn_dt_bias', 'gdn_norm_g', 's5_lambda_re', 's5_lambda_im',
           's5_log_dt', 's5_b_re', 's5_b_im', 's5_c_re', 's5_c_im', 's5_d', 's5_w_glu', 'mem_norm_g', 'w_kv_mem',
           'w_br_a', 'w_br_b', 'w_br_c', 'w_out', 'final_g']
SHARDED = {'w_in': 'col', 'conv_w': 'col', 's5_w_glu': 'col', 'w_kv_mem': 'row', 'w_br_a': 'col', 'w_br_b': 'col',
           'w_br_c': 'col', 'w_out': 'row'}
GATHER_BF16 = ['w_in', 's5_w_glu', 'w_kv_mem', 'w_br_a', 'w_br_b', 'w_br_c', 'w_out']
REPLICATED = [n for n in WEIGHTS if n not in SHARDED]
OVERLAPPED = ['s5_w_glu', 'w_kv_mem', 'w_br_a', 'w_br_b', 'w_br_c', 'w_out']


def _cparams(sem=None):
    return pltpu.CompilerParams(dimension_semantics=sem, vmem_limit_bytes=VMEM_LIMIT_BYTES)


def _pick(dim, pref):
    t = (min(pref, dim) // 128) * 128
    while t >= 128:
        if dim % t == 0:
            return t
        t -= 128
    return dim


def _make_dots(prep, precision):
    def raw(a, b, dims):
        return lax.dot_general(prep(a), prep(b), (dims, ((), ())), preferred_element_type=F32, precision=precision)

    @jax.custom_vjp
    def nn(a, b):
        return raw(a, b, ((1,), (0,)))

    @jax.custom_vjp
    def nt(a, b):
        return raw(a, b, ((1,), (1,)))

    @jax.custom_vjp
    def tn(a, b):
        return raw(a, b, ((0,), (0,)))

    nn.defvjp(lambda a, b: (nn(a, b), (a, b)), lambda r, ct: (nt(ct, r[1]), tn(r[0], ct)))
    nt.defvjp(lambda a, b: (nt(a, b), (a, b)), lambda r, ct: (nn(ct, r[1]), tn(ct, r[0])))
    tn.defvjp(lambda a, b: (tn(a, b), (a, b)), lambda r, ct: (nt(r[1], ct), nn(r[0], ct)))
    return nn, nt, tn


_bnn, _bnt, _btn = _make_dots(lambda a: a.astype(BF16), None)
_hnn, _hnt, _htn = _make_dots(lambda a: a.astype(F32), HI)
_mnn, _mnt, _mtn = _make_dots(lambda a: a.astype(F32), lax.Precision.HIGH)


def _mm(a, b, *, name, ta=False, tb=False, out_dtype=F32, addend=None, tm=512, tn=1024, tk=1024, b_shards=False,
        out_shards=0, rows=None):
    m, k = (a.shape[1], a.shape[0]) if ta else a.shape
    brows, bcols = (b.shape[1], b.shape[0] * b.shape[2]) if b_shards else b.shape
    n = brows if tb else bcols
    assert (bcols if tb else brows) == k, (a.shape, b.shape, ta, tb)
    first_row = 0
    if rows is not None:
        first_row, m = rows
    tm, tn, tk = _pick(m, tm), _pick(n, tn), _pick(k, tk)
    assert first_row % tm == 0
    r0 = first_row // tm
    bcs = ocs = 0
    if b_shards:
        bcs = b.shape[2]
        assert bcs % 128 == 0 and (tk if tb else tn) % bcs == 0
    if out_shards:
        ocs = n // out_shards
        assert ocs % 128 == 0 and tn % ocs == 0
    nk = k // tk
    dims = ((0 if ta else 1,), (1 if tb else 0,))

    def body(*refs):
        if addend is None:
            a_ref, b_ref, o_ref, acc_ref = refs
        else:
            a_ref, b_ref, add_ref, o_ref, acc_ref = refs
        kk = pl.program_id(2)

        @pl.when(kk == 0)
        def _():
            acc_ref[...] = jnp.zeros_like(acc_ref)

        dot = lambda x, y: lax.dot_general(x.astype(BF16), y.astype(BF16), (dims, ((), ())), preferred_element_type=F32)
        if not b_shards:
            acc_ref[...] += dot(a_ref[...], b_ref[...])
        elif tb:
            for g in range(tk // bcs):
                acc_ref[...] += dot(a_ref[:, g * bcs:(g + 1) * bcs], b_ref[g])
        else:
            for g in range(tn // bcs):
                acc_ref[:, g * bcs:(g + 1) * bcs] += dot(a_ref[...], b_ref[g])

        @pl.when(kk == nk - 1)
        def _():
            r = acc_ref[...]
            if addend is not None:
                r = r + add_ref[...].astype(F32)
            if out_shards:
                for g in range(tn // ocs):
                    o_ref[g] = r[:, g * ocs:(g + 1) * ocs].astype(o_ref.dtype)
            else:
                o_ref[...] = r.astype(o_ref.dtype)

    a_spec = (pl.BlockSpec((tk, tm), lambda i, j, kk: (kk, i + r0)) if ta
              else pl.BlockSpec((tm, tk), lambda i, j, kk: (i + r0, kk)))
    if b_shards:
        b_spec = (pl.BlockSpec((tk // bcs, tn, bcs), lambda i, j, kk: (kk, j, 0)) if tb
                  else pl.BlockSpec((tn // bcs, tk, bcs), lambda i, j, kk: (j, kk, 0)))
    else:
        b_spec = (pl.BlockSpec((tn, tk), lambda i, j, kk: (j, kk)) if tb
                  else pl.BlockSpec((tk, tn), lambda i, j, kk: (kk, j)))
    if out_shards:
        o_spec = pl.BlockSpec((tn // ocs, tm, ocs), lambda i, j, kk: (j, i, 0))
        out_shape = jax.ShapeDtypeStruct((out_shards, m, ocs), out_dtype)
    else:
        o_spec = pl.BlockSpec((tm, tn), lambda i, j, kk: (i, j))
        out_shape = jax.ShapeDtypeStruct((m, n), out_dtype)
    in_specs = [a_spec, b_spec] + ([o_spec] if addend is not None else [])
    args = (a, b) + ((addend,) if addend is not None else ())
    return pl.pallas_call(
        body, name=name, grid=(m // tm, n // tn, nk), in_specs=in_specs, out_specs=o_spec,
        out_shape=out_shape, scratch_shapes=[pltpu.VMEM((tm, tn), F32)],
        compiler_params=_cparams(("parallel", "parallel", "arbitrary")))(*args)


def _rt(arr, ts, cb=0, w=None):
    w = arr.shape[1] if w is None else w
    return (arr, (ts, w), lambda i, cb=cb: (i, cb))


def _whole(p):
    return pl.BlockSpec(p.shape, lambda i, nd=p.ndim: (0,) * nd)


def _tile_fwd(f, name, n, acts, params, outs):
    na, npar = len(acts), len(params)

    def body(*refs):
        res = f(*[r[...] for r in refs[:na + npar]])
        for r, v in zip(refs[na + npar:], res):
            r[...] = v.astype(r.dtype)

    in_specs = [pl.BlockSpec(b, m) for _, b, m in acts] + [_whole(p) for p in params]
    out = pl.pallas_call(
        body, name=name, grid=(n,), in_specs=in_specs,
        out_specs=[pl.BlockSpec(b, m) for _, _, b, m in outs],
        out_shape=[jax.ShapeDtypeStruct(s, d) for s, d, _, _ in outs],
        compiler_params=_cparams(("parallel",)))(*[a for a, _, _ in acts], *params)
    return out


def _tile_bwd(f, name, n, acts, params, cts, agrads, pgrads):
    na, npar, nc = len(acts), len(params), len(cts)
    adds = [g[4] for g in agrads if g is not None and len(g) == 5]

    def body(*refs):
        i = pl.program_id(0)
        ins = [r[...] for r in refs[:na + npar]]
        outs, vjp = jax.vjp(f, *ins)
        g = vjp(tuple(c[...].astype(o.dtype) for c, o in zip(refs[na + npar:na + npar + nc], outs)))
        add_refs = refs[na + npar + nc:na + npar + nc + len(adds)]
        orefs = refs[na + npar + nc + len(adds):]
        k = 0
        for j in range(na):
            if agrads[j] is not None:
                val = g[j]
                if len(agrads[j]) == 5:
                    val = val + add_refs[[id(t) for t in adds].index(id(agrads[j][4]))][...]
                orefs[k][...] = val.astype(orefs[k].dtype)
                k += 1
        for j in range(npar):
            if pgrads[j]:
                o = orefs[k]

                @pl.when(i == 0)
                def _(o=o):
                    o[...] = jnp.zeros_like(o)

                o[...] += g[na + j].astype(F32)
                k += 1

    in_specs = ([pl.BlockSpec(b, m) for _, b, m in acts] + [_whole(p) for p in params]
                + [pl.BlockSpec(b, m) for _, b, m in cts]
                + [pl.BlockSpec(g[2], g[3]) for g in agrads if g is not None and len(g) == 5])
    out_specs = [pl.BlockSpec(g[2], g[3]) for g in agrads if g is not None]
    out_shape = [jax.ShapeDtypeStruct(g[0], g[1]) for g in agrads if g is not None]
    for p, flag in zip(params, pgrads):
        if flag:
            out_specs.append(_whole(p))
            out_shape.append(jax.ShapeDtypeStruct(p.shape, F32))
    return pl.pallas_call(
        body, name=name, grid=(n,), in_specs=in_specs, out_specs=out_specs, out_shape=out_shape,
        compiler_params=_cparams(("arbitrary",)))(*[a for a, _, _ in acts], *params, *[c for c, _, _ in cts], *adds)


def _silu(x):
    return x * jax.nn.sigmoid(x)


def _rms(x, g):
    x = x.astype(F32)
    return (x * lax.rsqrt(jnp.mean(x * x, axis=-1, keepdims=True) + EPS) * g,)


def _shift_down(x, s):
    row = lax.broadcasted_iota(jnp.int32, x.shape, 0)
    return jnp.where(row >= s, pltpu.roll(x, s, 0), 0.0)


def _shift_up(x, s):
    n = x.shape[0]
    row = lax.broadcasted_iota(jnp.int32, x.shape, 0)
    return jnp.where(row < n - s, pltpu.roll(x, n - s, 0), 0.0)


@functools.partial(jax.custom_vjp, nondiff_argnums=(1,))
def _shift(x, s):
    return _shift_down(x, s)


_shift.defvjp(lambda x, s: (_shift_down(x, s), None), lambda s, _, ct: (_shift_up(ct, s),))


def _gdn_pre(mode):
    def f(x, w):
        x = x.astype(F32)
        y = x * w[CONV_WIDTH - 1:CONV_WIDTH, :]
        for j in range(CONV_WIDTH - 1):
            y = y + _shift(x, CONV_WIDTH - 1 - j) * w[j:j + 1, :]
        y = _silu(y)
        if mode != 'v':
            y = y * lax.rsqrt(jnp.sum(y * y, axis=-1, keepdims=True) + EPS)
        if mode == 'q':
            y = y * (GDN_HEAD_DIM ** -0.5)
        return (y,)
    return f


def _softplus(x):
    return jnp.maximum(x, 0.0) + jnp.log1p(jnp.exp(-jnp.abs(x)))


def _gdn_gates(ba, alog, dtb, e_beta, e_g):
    beta = jax.nn.sigmoid(ba)
    g = -jnp.exp(alog) * _softplus(ba + dtb)
    return _hnn(beta, lax.stop_gradient(e_beta)), _hnn(g, lax.stop_gradient(e_g))


@jax.custom_vjp
def _inverse_known(neg, t):
    return t


_inverse_known.defvjp(lambda neg, t: (t, t), lambda t, ct: (_mtn(t, _mnt(ct, t)), jnp.zeros_like(t)))


def _gdn_intra(q, k, v, gb, bb, t_known=None):
    n, c = len(q), q[0].shape[0]
    ri = lax.broadcasted_iota(jnp.int32, (c, c), 0)
    ci = lax.broadcasted_iota(jnp.int32, (c, c), 1)
    incl, strict = ri >= ci, ri > ci
    tri = incl.astype(F32)
    eye = (ri == ci).astype(F32)
    each = range(n)
    gc = [_hnn(tri, gb[i]) for i in each]
    decay = [jnp.exp(jnp.where(incl, gc[i][:, :c] - gc[i].T[:c, :], -1e30)) for i in each]
    kb = [k[i] * bb[i] for i in each]
    kk = [_bnt(kb[i], k[i]) for i in each]
    qk = [_bnt(q[i], k[i]) for i in each]
    p = [jnp.where(strict, -(kk[i] * decay[i]), 0.0) for i in each]
    if t_known is None:
        t = [eye + p[i] for i in each]
        for _ in range(int(math.log2(c)) - 1):
            p = [_mnn(p[i], p[i]) for i in each]
            tp = [_mnn(t[i], p[i]) for i in each]
            t = [t[i] + tp[i] for i in each]
    else:
        t = [_inverse_known(p[i], t_known[i]) for i in each]
    egc = [jnp.exp(gc[i]) for i in each]
    u_val = [_mnn(t[i], v[i] * bb[i]) for i in each]
    w_dec = [_mnn(t[i], kb[i] * egc[i]) for i in each]
    qk = [qk[i] * decay[i] for i in each]
    gl = [jnp.sum(gb[i], axis=0, keepdims=True) for i in each]
    return w_dec, u_val, qk, [q[i] * egc[i] for i in each], [k[i] * jnp.exp(gl[i] - gc[i]) for i in each], t


def _gdn_inter(w_dec, u_val, qk, q_dec, k_dec, gb, state):
    each = range(len(state))
    ws = [_bnn(w_dec[i], state[i]) for i in each]
    qs = [_bnn(q_dec[i], state[i]) for i in each]
    v_new = [u_val[i] - ws[i] for i in each]
    qv = [_bnn(qk[i], v_new[i]) for i in each]
    kv = [_btn(k_dec[i], v_new[i]) for i in each]
    decayed = [state[i] * jnp.exp(jnp.sum(gb[i], axis=0, keepdims=True)) for i in each]
    return [qs[i] + qv[i] for i in each], [decayed[i] + kv[i] for i in each]


def _gdn_post(o, z, g):
    parts = []
    for h in range(o.shape[1] // GDN_HEAD_DIM):
        oh = o[:, h * GDN_HEAD_DIM:(h + 1) * GDN_HEAD_DIM]
        parts.append(oh * lax.rsqrt(jnp.mean(oh * oh, axis=-1, keepdims=True) + EPS) * g)
    y = parts[0] if len(parts) == 1 else jnp.concatenate(parts, axis=1)
    return (y * _silu(z.astype(F32)),)


def _gelu(x):
    return 0.5 * x * (1.0 + jnp.tanh(0.7978845608028654 * (x + 0.044715 * x * x * x)))


def _s5_post1(ylin, xb, d):
    return (_gelu(ylin + d * xb.astype(F32)),)


def _s5_post2(t, z):
    w = t.shape[1] // 2
    return (t[:, :w] * jax.nn.sigmoid(t[:, w:]) * _silu(z.astype(F32)),)


def _attn(q, z, kv):
    w = q.shape[1]
    hd = w // XA_HEADS
    parts = []
    for h in range(XA_HEADS):
        s = _bnt(q[:, h * hd:(h + 1) * hd], kv[:, h * hd:(h + 1) * hd]) * (hd ** -0.5)
        s = s - jnp.max(s, axis=-1, keepdims=True)
        e = jnp.exp(s)
        p = e / jnp.sum(e, axis=-1, keepdims=True)
        parts.append(_bnn(p, kv[:, w + h * hd:w + (h + 1) * hd]))
    return (jnp.concatenate(parts, axis=1) * _silu(z.astype(F32)),)


def _merge(g0, g1, g2, pa, pb, pc):
    s0, s1, s2 = (jax.nn.sigmoid(t.astype(F32)) for t in (g0, g1, g2))
    return (s0 * pa + s1 * pb + s2 * pc,)


def _s5_params(lr, li, logdt, br, bi, e):
    dt = jnp.exp(logdt)
    mag = jnp.exp(lr * dt)
    ab_re, ab_im = mag * jnp.cos(li * dt), mag * jnp.sin(li * dt)
    den = lr * lr + li * li
    nr, ni = ab_re - 1.0, ab_im
    e = lax.stop_gradient(e)
    cre = _hnn((nr * lr + ni * li) / den, e)
    cim = _hnn((ni * lr - nr * li) / den, e)
    return ab_re, ab_im, cre * br - cim * bi, cre * bi + cim * br


def _gdn_blocks(s, w, per_step):
    nh, nc = w // GDN_HEAD_DIM, s // CHUNK
    cpb = math.gcd(per_step, nc)
    return nh, nc, cpb, nc // cpb, (cpb * CHUNK, w), (cpb * CHUNK, nh * CHUNK)


def _gdn_pairs(cpb, nh):
    wide, narrow = [], []
    for cb in range(cpb):
        rows = slice(cb * CHUNK, (cb + 1) * CHUNK)
        for h in range(nh):
            wide.append((rows, slice(h * GDN_HEAD_DIM, (h + 1) * GDN_HEAD_DIM)))
            narrow.append((rows, slice(h * CHUNK, (h + 1) * CHUNK)))
    return wide, narrow


def _gdn_intra_fwd(q, k, v, gb, bb, per_step=4):
    s, w = q.shape
    nh, nc, cpb, n, wide, narrow = _gdn_blocks(s, w, per_step)

    def body(q_ref, k_ref, v_ref, g_ref, b_ref, wd_ref, uv_ref, qk_ref, qd_ref, kd_ref, t_ref):
        wide, narrow = _gdn_pairs(cpb, nh)
        res = _gdn_intra(*[[r[ix] for ix in wide] for r in (q_ref, k_ref, v_ref, g_ref, b_ref)])
        for ref, vals, where in zip((wd_ref, uv_ref, qk_ref, qd_ref, kd_ref, t_ref), res,
                                    (wide, wide, narrow, wide, wide, narrow)):
            for ix, val in zip(where, vals):
                ref[ix] = val

    bw = pl.BlockSpec(wide, lambda i: (i, 0))
    bn = pl.BlockSpec(narrow, lambda i: (i, 0))
    fw = jax.ShapeDtypeStruct((s, w), F32)
    fn = jax.ShapeDtypeStruct((s, nh * CHUNK), F32)
    return pl.pallas_call(
        body, name="gdn_intra", grid=(n,), in_specs=[bw] * 5, out_specs=[bw, bw, bn, bw, bw, bn],
        out_shape=[fw, fw, fn, fw, fw, fn], compiler_params=_cparams(("parallel",)))(q, k, v, gb, bb)


def _gdn_intra_bwd(q, k, v, gb, bb, t, cts, dgb_inter, per_step=4):
    s, w = q.shape
    nh, nc, cpb, n, wide, narrow = _gdn_blocks(s, w, per_step)

    def body(q_ref, k_ref, v_ref, g_ref, b_ref, t_ref, cwd, cuv, cqk, cqd, ckd, dgi, dq_ref, dk_ref, dv_ref, dg_ref,
             db_ref):
        wide, narrow = _gdn_pairs(cpb, nh)
        t_known = [t_ref[ix] for ix in narrow]
        _, vjp = jax.vjp(lambda *a: _gdn_intra(*a, t_known=t_known)[:5],
                         *[[r[ix] for ix in wide] for r in (q_ref, k_ref, v_ref, g_ref, b_ref)])
        cts = tuple([r[ix] for ix in where] for r, where in zip((cwd, cuv, cqk, cqd, ckd),
                                                               (wide, wide, narrow, wide, wide)))
        dq, dk, dv, dg, db = vjp(cts)
        for j, ix in enumerate(wide):
            dq_ref[ix], dk_ref[ix], dv_ref[ix], db_ref[ix] = dq[j], dk[j], dv[j], db[j]
            dg_ref[ix] = dg[j] + dgi[ix]

    bw = pl.BlockSpec(wide, lambda i: (i, 0))
    bn = pl.BlockSpec(narrow, lambda i: (i, 0))
    return pl.pallas_call(
        body, name="gdn_intra_bwd", grid=(n,), in_specs=[bw] * 5 + [bn, bw, bw, bn, bw, bw, bw], out_specs=[bw] * 5,
        out_shape=[jax.ShapeDtypeStruct((s, w), F32)] * 5,
        compiler_params=_cparams(("parallel",)))(q, k, v, gb, bb, t, *cts, dgb_inter)


def _gdn_inter_fwd(wd, uv, qk, qd, kd, gb, per_step=4):
    s, w = wd.shape
    nh, nc, cpb, n, wide, narrow = _gdn_blocks(s, w, per_step)
    hd = GDN_HEAD_DIM

    def body(wd_ref, uv_ref, qk_ref, qd_ref, kd_ref, g_ref, o_ref, st_ref, state):
        @pl.when(pl.program_id(0) == 0)
        def _():
            state[...] = jnp.zeros_like(state)

        wide, narrow = _gdn_pairs(cpb, nh)
        st = [state[h] for h in range(nh)]
        for cb in range(cpb):
            wi, na = wide[cb * nh:(cb + 1) * nh], narrow[cb * nh:(cb + 1) * nh]
            for h in range(nh):
                st_ref[cb, h] = st[h]
            o, st = _gdn_inter([wd_ref[ix] for ix in wi], [uv_ref[ix] for ix in wi], [qk_ref[ix] for ix in na],
                               [qd_ref[ix] for ix in wi], [kd_ref[ix] for ix in wi], [g_ref[ix] for ix in wi], st)
            for h in range(nh):
                o_ref[wi[h]] = o[h]
        for h in range(nh):
            state[h] = st[h]

    bw = pl.BlockSpec(wide, lambda i: (i, 0))
    bn = pl.BlockSpec(narrow, lambda i: (i, 0))
    return pl.pallas_call(
        body, name="gdn_inter", grid=(n,), in_specs=[bw, bw, bn, bw, bw, bw],
        out_specs=[bw, pl.BlockSpec((cpb, nh, hd, hd), lambda i: (i, 0, 0, 0))],
        out_shape=[jax.ShapeDtypeStruct((s, w), F32), jax.ShapeDtypeStruct((nc, nh, hd, hd), F32)],
        scratch_shapes=[pltpu.VMEM((nh, hd, hd), F32)],
        compiler_params=_cparams(("arbitrary",)))(wd, uv, qk, qd, kd, gb)


def _gdn_inter_bwd(wd, uv, qk, qd, kd, gb, states, do, per_step=4):
    s, w = wd.shape
    nh, nc, cpb, n, wide, narrow = _gdn_blocks(s, w, per_step)
    hd = GDN_HEAD_DIM

    def body(wd_ref, uv_ref, qk_ref, qd_ref, kd_ref, g_ref, st_ref, do_ref, cwd, cuv, cqk, cqd, ckd, dg_ref, dstate):
        @pl.when(pl.program_id(0) == 0)
        def _():
            dstate[...] = jnp.zeros_like(dstate)

        wide, narrow = _gdn_pairs(cpb, nh)
        dst = [dstate[h] for h in range(nh)]
        for cb in reversed(range(cpb)):
            wi, na = wide[cb * nh:(cb + 1) * nh], narrow[cb * nh:(cb + 1) * nh]
            _, vjp = jax.vjp(_gdn_inter, [wd_ref[ix] for ix in wi], [uv_ref[ix] for ix in wi],
                             [qk_ref[ix] for ix in na], [qd_ref[ix] for ix in wi], [kd_ref[ix] for ix in wi],
                             [g_ref[ix] for ix in wi], [st_ref[cb, h] for h in range(nh)])
            dwd, duv, dqk, dqd, dkd, dg, dst = vjp(([do_ref[ix] for ix in wi], dst))
            for h in range(nh):
                cwd[wi[h]], cuv[wi[h]], cqk[na[h]], cqd[wi[h]], ckd[wi[h]], dg_ref[wi[h]] = (
                    dwd[h], duv[h], dqk[h], dqd[h], dkd[h], dg[h])
        for h in range(nh):
            dstate[h] = dst[h]

    bw = pl.BlockSpec(wide, lambda i: (n - 1 - i, 0))
    bn = pl.BlockSpec(narrow, lambda i: (n - 1 - i, 0))
    fw = jax.ShapeDtypeStruct((s, w), F32)
    return pl.pallas_call(
        body, name="gdn_inter_bwd", grid=(n,),
        in_specs=[bw, bw, bn, bw, bw, bw, pl.BlockSpec((cpb, nh, hd, hd), lambda i: (n - 1 - i, 0, 0, 0)), bw],
        out_specs=[bw, bw, bn, bw, bw, bw],
        out_shape=[fw, fw, jax.ShapeDtypeStruct((s, nh * CHUNK), F32), fw, fw, fw],
        scratch_shapes=[pltpu.VMEM((nh, hd, hd), F32)],
        compiler_params=_cparams(("arbitrary",)))(wd, uv, qk, qd, kd, gb, states, do)


def _s5_coef(ar, ai):
    nl = ar.shape[1]

    def body(ar_ref, ai_ref, o_ref):
        row = lax.broadcasted_iota(jnp.int32, (8, nl), 0)
        for base, sign in ((0, 1.0), (8, -1.0)):
            pr = [jnp.broadcast_to(ar_ref[...], (8, nl))]
            pi = [jnp.broadcast_to(ai_ref[...], (8, nl)) * sign]
            for _ in range(7):
                pr.append(pr[-1] * pr[0] - pi[-1] * pi[0])
                pi.append(pr[-2] * pi[0] + pi[-1] * pr[0])
            for j, d in enumerate((1, 2, 4)):
                m = (row >= d) if base == 0 else (row <= 7 - d)
                o_ref[base + 2 * j] = jnp.where(m, pr[d - 1], 0.0)
                o_ref[base + 2 * j + 1] = jnp.where(m, pi[d - 1], 0.0)
            cr, ci = jnp.zeros((8, nl), F32), jnp.zeros((8, nl), F32)
            for t in range(8):
                e = t if base == 0 else 7 - t
                cr = jnp.where(row == t, pr[e], cr)
                ci = jnp.where(row == t, pi[e], ci)
            o_ref[base + 6] = cr
            o_ref[base + 7] = ci

    return pl.pallas_call(body, name="s5_coef", out_shape=jax.ShapeDtypeStruct((16, 8, nl), F32),
                          compiler_params=_cparams())(ar, ai)


def _scan_tile(src_re, src_im, dst_re, dst_im, coef_ref, carry_re, carry_im, ts, reverse, extra=None):
    nl = src_re.shape[1]
    base = 8 if reverse else 0
    ng = ts // 8
    for lc in range(nl // SCAN_LANES):
        ln = slice(lc * SCAN_LANES, (lc + 1) * SCAN_LANES)
        m = [coef_ref[base + j, :, ln] for j in range(8)]
        row = lax.broadcasted_iota(jnp.int32, (8, SCAN_LANES), 0)

        def step(r, carry, ln=ln, m=m, row=row):
            grp = (ng - 1 - r) if reverse else r
            rows = pl.ds(pl.multiple_of(grp * 8, 8), 8)
            xr, xi = src_re[rows, ln], src_im[rows, ln]
            for j, d in enumerate((1, 2, 4)):
                sh = 8 - d if reverse else d
                sr, si = pltpu.roll(xr, sh, 0), pltpu.roll(xi, sh, 0)
                mr, mi = m[2 * j], m[2 * j + 1]
                xr, xi = xr + mr * sr - mi * si, xi + mr * si + mi * sr
            cr, ci = carry[0], carry[1]
            hr = xr + m[6] * cr - m[7] * ci
            hi = xi + m[6] * ci + m[7] * cr
            dst_re[rows, ln] = hr
            dst_im[rows, ln] = hi
            edge = 0 if reverse else 7
            out = (jnp.broadcast_to(hr[edge:edge + 1, :], hr.shape), jnp.broadcast_to(hi[edge:edge + 1, :], hi.shape))
            if extra is not None:
                h_re, h_im, halo_re, halo_im, first, _, _ = extra
                prev = pl.ds(pl.multiple_of(jnp.maximum(grp - 1, 0) * 8, 8), 8)
                use_halo = grp == 0
                pr = jnp.where(use_halo, halo_re[:, ln] * first, h_re[prev, ln])
                pi = jnp.where(use_halo, halo_im[:, ln] * first, h_im[prev, ln])
                qr = jnp.where(row == 0, jnp.broadcast_to(pr[7:8, :], pr.shape), pltpu.roll(h_re[rows, ln], 1, 0))
                qi = jnp.where(row == 0, jnp.broadcast_to(pi[7:8, :], pi.shape), pltpu.roll(h_im[rows, ln], 1, 0))
                out = out + (carry[2] + hr * qr + hi * qi, carry[3] + hi * qr - hr * qi)
            return out

        init = (carry_re[:, ln], carry_im[:, ln])
        if extra is not None:
            init = init + (extra[5][:, ln], extra[6][:, ln])
        fin = lax.fori_loop(0, ng, step, init)
        carry_re[:, ln] = fin[0]
        carry_im[:, ln] = fin[1]
        if extra is not None:
            extra[5][:, ln] = fin[2]
            extra[6][:, ln] = fin[3]


def _s5_fwd(xb, bb_re, bb_im, c_re, c_im, coef, ts):
    s, w = xb.shape
    nb = bb_re.shape[0]
    nl = nb * 512

    def body(x_ref, bre_ref, bim_ref, cre_ref, cim_ref, coef_ref, hre_ref, him_ref, y_ref, ure, uim, car_re, car_im):
        @pl.when(pl.program_id(0) == 0)
        def _():
            car_re[...] = jnp.zeros_like(car_re)
            car_im[...] = jnp.zeros_like(car_im)

        for b in range(nb):
            xs = x_ref[:, b * 128:(b + 1) * 128].astype(BF16)
            ure[:, b * 512:(b + 1) * 512] = jnp.dot(xs, bre_ref[b], preferred_element_type=F32)
            uim[:, b * 512:(b + 1) * 512] = jnp.dot(xs, bim_ref[b], preferred_element_type=F32)
        _scan_tile(ure, uim, hre_ref, him_ref, coef_ref, car_re, car_im, ts, False)
        for b in range(nb):
            hr = hre_ref[:, b * 512:(b + 1) * 512].astype(BF16)
            hi = him_ref[:, b * 512:(b + 1) * 512].astype(BF16)
            y_ref[:, b * 128:(b + 1) * 128] = (jnp.dot(hr, cre_ref[b], preferred_element_type=F32)
                                               - jnp.dot(hi, cim_ref[b], preferred_element_type=F32))

    row = lambda wd: pl.BlockSpec((ts, wd), lambda i: (i, 0))
    return pl.pallas_call(
        body, name="s5_fwd", grid=(s // ts,),
        in_specs=[row(w), _whole(bb_re), _whole(bb_im), _whole(c_re), _whole(c_im), _whole(coef)],
        out_specs=[row(nl), row(nl), row(w)],
        out_shape=[jax.ShapeDtypeStruct((s, nl), F32), jax.ShapeDtypeStruct((s, nl), F32),
                   jax.ShapeDtypeStruct((s, w), F32)],
        scratch_shapes=[pltpu.VMEM((ts, nl), F32), pltpu.VMEM((ts, nl), F32), pltpu.VMEM((8, nl), F32),
                        pltpu.VMEM((8, nl), F32)],
        compiler_params=_cparams(("arbitrary",)))(xb, bb_re, bb_im, c_re, c_im, coef)


def _s5_bwd(dy, xb, h_re, h_im, bb_re, bb_im, c_re, c_im, coef, ts):
    s, w = xb.shape
    nb = bb_re.shape[0]
    nl = nb * 512
    nt = s // ts

    def body(dy_ref, x_ref, hre_ref, him_ref, halo_re, halo_im, bre_ref, bim_ref, cre_ref, cim_ref, coef_ref,
             dx_ref, dbre_ref, dbim_ref, dcre_ref, dcim_ref, dare_ref, daim_ref, gre, gim, car_re, car_im):
        i = pl.program_id(0)

        @pl.when(i == 0)
        def _():
            for r in (car_re, car_im, dbre_ref, dbim_ref, dcre_ref, dcim_ref, dare_ref, daim_ref):
                r[...] = jnp.zeros_like(r)

        for b in range(nb):
            dyb = dy_ref[:, b * 128:(b + 1) * 128].astype(BF16)
            gre[:, b * 512:(b + 1) * 512] = lax.dot_general(dyb, cre_ref[b], (((1,), (1,)), ((), ())),
                                                            preferred_element_type=F32)
            gim[:, b * 512:(b + 1) * 512] = -lax.dot_general(dyb, cim_ref[b], (((1,), (1,)), ((), ())),
                                                             preferred_element_type=F32)
            hr = hre_ref[:, b * 512:(b + 1) * 512].astype(BF16)
            hi = him_ref[:, b * 512:(b + 1) * 512].astype(BF16)
            dcre_ref[b] += lax.dot_general(hr, dyb, (((0,), (0,)), ((), ())), preferred_element_type=F32)
            dcim_ref[b] -= lax.dot_general(hi, dyb, (((0,), (0,)), ((), ())), preferred_element_type=F32)
        first = (i != nt - 1).astype(F32)
        _scan_tile(gre, gim, gre, gim, coef_ref, car_re, car_im, ts, True,
                   extra=(hre_ref, him_ref, halo_re, halo_im, first, dare_ref, daim_ref))
        for b in range(nb):
            gr = gre[:, b * 512:(b + 1) * 512].astype(BF16)
            gi = gim[:, b * 512:(b + 1) * 512].astype(BF16)
            xs = x_ref[:, b * 128:(b + 1) * 128].astype(BF16)
            dx_ref[:, b * 128:(b + 1) * 128] = (
                lax.dot_general(gr, bre_ref[b], (((1,), (1,)), ((), ())), preferred_element_type=F32)
                + lax.dot_general(gi, bim_ref[b], (((1,), (1,)), ((), ())), preferred_element_type=F32))
            dbre_ref[b] += lax.dot_general(xs, gr, (((0,), (0,)), ((), ())), preferred_element_type=F32)
            dbim_ref[b] += lax.dot_general(xs, gi, (((0,), (0,)), ((), ())), preferred_element_type=F32)

    row = lambda wd: pl.BlockSpec((ts, wd), lambda i: (nt - 1 - i, 0))
    halo = pl.BlockSpec((8, nl), lambda i: (jnp.maximum((nt - 1 - i) * (ts // 8) - 1, 0), 0))
    return pl.pallas_call(
        body, name="s5_bwd", grid=(nt,),
        in_specs=[row(w), row(w), row(nl), row(nl), halo, halo, _whole(bb_re), _whole(bb_im), _whole(c_re),
                  _whole(c_im), _whole(coef)],
        out_specs=[row(w), _whole(bb_re), _whole(bb_im), _whole(c_re), _whole(c_im),
                   pl.BlockSpec((8, nl), lambda i: (0, 0)), pl.BlockSpec((8, nl), lambda i: (0, 0))],
        out_shape=[jax.ShapeDtypeStruct((s, w), F32), jax.ShapeDtypeStruct(bb_re.shape, F32),
                   jax.ShapeDtypeStruct(bb_im.shape, F32), jax.ShapeDtypeStruct(c_re.shape, F32),
                   jax.ShapeDtypeStruct(c_im.shape, F32), jax.ShapeDtypeStruct((8, nl), F32),
                   jax.ShapeDtypeStruct((8, nl), F32)],
        scratch_shapes=[pltpu.VMEM((ts, nl), F32), pltpu.VMEM((ts, nl), F32), pltpu.VMEM((8, nl), F32),
                        pltpu.VMEM((8, nl), F32)],
        compiler_params=_cparams(("arbitrary",)))(dy, xb, h_re, h_im, h_re, h_im, bb_re, bb_im, c_re, c_im, coef)


def _final(x, mo, target, fg, ts):
    s, d = x.shape

    def f(x, mo, fg, tgt):
        y = _rms(x + mo, fg)[0]
        err = y - tgt
        return 0.5 * jnp.sum(jnp.mean(err * err, axis=-1, keepdims=True), axis=0, keepdims=True)

    def body(x_ref, mo_ref, t_ref, fg_ref, dh_ref, dfg_ref, loss_ref):
        @pl.when(pl.program_id(0) == 0)
        def _():
            dfg_ref[...] = jnp.zeros_like(dfg_ref)
            loss_ref[...] = jnp.zeros_like(loss_ref)

        loss, vjp = jax.vjp(f, x_ref[...], mo_ref[...], fg_ref[...], t_ref[...])
        _, dmo, dfg, _ = vjp(jnp.ones((1, 1), F32))
        dh_ref[...] = dmo
        dfg_ref[...] += dfg
        loss_ref[...] += jnp.broadcast_to(loss, loss_ref.shape)

    row = pl.BlockSpec((ts, d), lambda i: (i, 0))
    return pl.pallas_call(
        body, name="final", grid=(s // ts,), in_specs=[row, row, row, _whole(fg)],
        out_specs=[row, _whole(fg), pl.BlockSpec((8, 128), lambda i: (0, 0))],
        out_shape=[jax.ShapeDtypeStruct((s, d), F32), jax.ShapeDtypeStruct(fg.shape, F32),
                   jax.ShapeDtypeStruct((8, 128), F32)],
        compiler_params=_cparams(("arbitrary",)))(x, mo, target, fg)


N_CHIPS = 4


def _other_chips(x, y):
    return [((1 - x, y), 2 * (1 - x) + y), ((x, 1 - y), 2 * x + 1 - y), ((1 - x, 1 - y), 2 * (1 - x) + 1 - y)]


def _comm_call(body, name, srcs, out_shapes, n_sems):
    n = len(srcs)
    return pl.pallas_call(
        body, name=name, in_specs=[pl.BlockSpec(memory_space=pl.ANY)] * n,
        out_specs=[pl.BlockSpec(memory_space=pl.ANY)] * n, out_shape=out_shapes,
        scratch_shapes=[pltpu.SemaphoreType.DMA((n, n_sems)), pltpu.SemaphoreType.DMA((n, n_sems)),
                        pltpu.SemaphoreType.DMA((n,))],
        compiler_params=pltpu.CompilerParams(has_side_effects=True))(*srcs)


def _gather(srcs, name):
    n = len(srcs)

    def body(*refs):
        src, out = refs[:n], refs[n:2 * n]
        send_sems, recv_sems, local_sems = refs[2 * n:]
        x, y, c = lax.axis_index("x"), lax.axis_index("y"), lax.axis_index("c")
        me, sib_slot, sib = 4 * x + 2 * y + c, 4 * x + 2 * y + 1 - c, (x, y, 1 - c)
        chips = _other_chips(x, y)

        def cp(a, k, src_ref, slot, to):
            return pltpu.make_async_remote_copy(
                src_ref=src_ref, dst_ref=out[a].at[slot], send_sem=send_sems.at[a, k], recv_sem=recv_sems.at[a, k],
                device_id=to, device_id_type=pl.DeviceIdType.MESH)

        local = [pltpu.make_async_copy(src[a], out[a].at[me], local_sems.at[a]) for a in range(n)]
        first = [cp(a, 0, src[a], me, sib) for a in range(n)]
        first += [cp(a, 1 + j, src[a], me, (*chip, c)) for j, (chip, _) in enumerate(chips) for a in range(n)]
        for d in local + first:
            d.start()
        passed = []
        for j, (chip, q) in enumerate(chips):
            for a in range(n):
                cp(a, 1 + j, src[a], 2 * q + c, sib).wait_recv()
                fwd = cp(a, 4 + j, out[a].at[2 * q + c], 2 * q + c, sib)
                fwd.start()
                passed.append(fwd)
        for a in range(n):
            cp(a, 0, src[a], sib_slot, sib).wait_recv()
        for j, (chip, q) in enumerate(chips):
            for a in range(n):
                cp(a, 4 + j, src[a], 2 * q + 1 - c, sib).wait_recv()
        for d in first + passed:
            d.wait_send()
        for d in local:
            d.wait()

    return _comm_call(body, name, srcs, [jax.ShapeDtypeStruct((N_DEV,) + s.shape, s.dtype) for s in srcs], 7)


def _all_peers(x, y, c):
    out = []
    for k in range(1, N_DEV):
        px = 1 - x if k & 4 else x
        py = 1 - y if k & 2 else y
        pc = 1 - c if k & 1 else c
        out.append(((px, py, pc), 4 * px + 2 * py + pc))
    return out


_HBM = pl.BlockSpec(memory_space=pltpu.HBM)
_SEM = pl.BlockSpec(memory_space=pltpu.SEMAPHORE)
_DATAFLOW = pltpu.SideEffectType.DATAFLOW_SIDE_EFFECTING


def _send_whole(ref, slot):
    return ref


def _send_slot(ref, slot):
    return ref.at[slot]


def _direct_copies(src, land, send_sems, recv_sems, picks, arriving):
    x, y, c = lax.axis_index("x"), lax.axis_index("y"), lax.axis_index("c")
    me = 4 * x + 2 * y + c
    out = []
    for k, (pos, slot) in enumerate(_all_peers(x, y, c)):
        for a in range(len(src)):
            sem = a * (N_DEV - 1) + k
            out.append(pltpu.make_async_remote_copy(
                src_ref=picks[a](src[a], slot), dst_ref=land[a].at[slot if arriving else me],
                send_sem=send_sems.at[sem], recv_sem=recv_sems.at[sem], device_id=pos,
                device_id_type=pl.DeviceIdType.MESH))
    return out


def _direct_start(srcs, lands, picks, name):
    n = len(srcs)

    def body(*refs):
        src, land = refs[:n], refs[n:2 * n]
        send_sems, recv_sems = refs[2 * n], refs[2 * n + 1]
        for push in _direct_copies(src, land, send_sems, recv_sems, picks, False):
            push.start()
        refs[-1][...] = jnp.zeros_like(refs[-1])

    arrays = [pltpu.with_memory_space_constraint(t, pltpu.HBM) for t in list(srcs) + list(lands)]
    outs = pl.pallas_call(
        body, name=name, in_specs=[_HBM] * (2 * n),
        out_specs=(_SEM, _SEM, *[_HBM] * (2 * n), pl.BlockSpec(memory_space=pltpu.VMEM)),
        out_shape=(pltpu.SemaphoreType.DMA((n * (N_DEV - 1),)), pltpu.SemaphoreType.DMA((n * (N_DEV - 1),)),
                   *[pltpu.HBM(t.shape, t.dtype) for t in arrays], jax.ShapeDtypeStruct((8, 128), F32)),
        input_output_aliases={i: 2 + i for i in range(2 * n)},
        compiler_params=pltpu.CompilerParams(has_side_effects=_DATAFLOW))(*arrays)
    return outs[:-1], outs[-1]


def _direct_wait(started, picks, after, name):
    send_sems, recv_sems, *thru = started
    n = len(thru) // 2

    def body(*refs):
        src, land = refs[:n], refs[n:2 * n]
        for arrive in _direct_copies(src, land, refs[2 * n], refs[2 * n + 1], picks, True):
            arrive.wait_send()
            arrive.wait_recv()

    outs = pl.pallas_call(
        body, name=name, in_specs=[_HBM] * (2 * n) + [_SEM, _SEM, pl.BlockSpec(memory_space=pl.ANY)],
        out_specs=[_HBM] * (2 * n), out_shape=[pltpu.HBM(t.shape, t.dtype) for t in thru],
        input_output_aliases={i: i for i in range(2 * n)},
        compiler_params=pltpu.CompilerParams(has_side_effects=_DATAFLOW))(*thru, send_sems, recv_sems, after)
    return outs[:n], outs[n:]


def _pair_scatter(gs, name):
    n = len(gs)

    def body(*refs):
        src, out = refs[:n], refs[n:2 * n]
        send_sems, recv_sems, _ = refs[2 * n:]
        x, y, c = lax.axis_index("x"), lax.axis_index("y"), lax.axis_index("c")
        sends = []
        for q in range(N_CHIPS):
            for a in range(n):
                d = pltpu.make_async_remote_copy(
                    src_ref=src[a].at[2 * q + 1 - c], dst_ref=out[a].at[q], send_sem=send_sems.at[a, q],
                    recv_sem=recv_sems.at[a, q], device_id=(x, y, 1 - c), device_id_type=pl.DeviceIdType.MESH)
                d.start()
                sends.append(d)
        for d in sends:
            d.wait_recv()
        for d in sends:
            d.wait_send()

    return _comm_call(body, name, gs, [jax.ShapeDtypeStruct((N_CHIPS,) + g.shape[1:], g.dtype) for g in gs], N_CHIPS)


def _cross_scatter(ps, name):
    n = len(ps)

    def body(*refs):
        src, out = refs[:n], refs[n:2 * n]
        send_sems, recv_sems, local_sems = refs[2 * n:]
        x, y, c = lax.axis_index("x"), lax.axis_index("y"), lax.axis_index("c")
        mine = 2 * x + y
        chips = _other_chips(x, y)
        local = [pltpu.make_async_copy(src[a].at[mine], out[a].at[mine], local_sems.at[a]) for a in range(n)]
        for d in local:
            d.start()
        sends = []
        for j, (chip, q) in enumerate(chips):
            for a in range(n):
                d = pltpu.make_async_remote_copy(
                    src_ref=src[a].at[q], dst_ref=out[a].at[mine], send_sem=send_sems.at[a, j],
                    recv_sem=recv_sems.at[a, j], device_id=(*chip, c), device_id_type=pl.DeviceIdType.MESH)
                d.start()
                sends.append(d)
        for j, (chip, q) in enumerate(chips):
            for a in range(n):
                pltpu.make_async_remote_copy(
                    src_ref=src[a].at[q], dst_ref=out[a].at[q], send_sem=send_sems.at[a, j],
                    recv_sem=recv_sems.at[a, j], device_id=(*chip, c), device_id_type=pl.DeviceIdType.MESH).wait_recv()
        for d in sends:
            d.wait_send()
        for d in local:
            d.wait()

    return _comm_call(body, name, ps, [jax.ShapeDtypeStruct(p.shape, p.dtype) for p in ps], 3)


def _pair_sum(g, got, c_idx, out_dtype, name):
    _, r, c = g.shape
    lanes = -(-c // 128) * 128
    tr = _pick_rows(r, max(8, (2 * 1024 * 1024) // (lanes * 4)))
    g4 = g.reshape((N_CHIPS, 2) + g.shape[1:])

    def body(c_ref, g_ref, got_ref, o_ref):
        o_ref[...] = (g_ref[...] + got_ref[...]).astype(o_ref.dtype)

    return pl.pallas_call(
        body, name=name,
        grid_spec=pltpu.PrefetchScalarGridSpec(
            num_scalar_prefetch=1, grid=(N_CHIPS, r // tr),
            in_specs=[pl.BlockSpec((None, None, tr, c), lambda q, i, cr: (q, cr[0], i, 0)),
                      pl.BlockSpec((None, tr, c), lambda q, i, cr: (q, i, 0))],
            out_specs=pl.BlockSpec((None, tr, c), lambda q, i, cr: (q, i, 0))),
        out_shape=jax.ShapeDtypeStruct(got.shape, out_dtype),
        compiler_params=_cparams(("parallel", "parallel")))(c_idx, g4, got)


def _pick_rows(r, pref):
    t = (min(pref, r) // 8) * 8
    while t >= 8:
        if r % t == 0:
            return t
        t -= 8
    return r


def _w_in_from_shards(t, lo, hi):
    n, r, cs = t.shape
    tr = _pick_rows(r, 256)
    wm = n * cs - (hi - lo)

    def body(t_ref, m_ref, b_ref):
        full = jnp.concatenate([t_ref[j] for j in range(n)], axis=1)
        m_ref[...] = jnp.concatenate([full[:, :lo], full[:, hi:]], axis=1)
        b_ref[...] = jnp.concatenate([full[:, lo:hi], jnp.zeros((tr, 128 - (hi - lo)), full.dtype)], axis=1)

    return pl.pallas_call(
        body, name="w_in_layout", grid=(r // tr,), in_specs=[pl.BlockSpec((n, tr, cs), lambda i: (0, i, 0))],
        out_specs=[pl.BlockSpec((tr, wm), lambda i: (i, 0)), pl.BlockSpec((tr, 128), lambda i: (i, 0))],
        out_shape=[jax.ShapeDtypeStruct((r, wm), t.dtype), jax.ShapeDtypeStruct((r, 128), t.dtype)],
        compiler_params=_cparams(("parallel",)))(t)


def _w_in_to_shards(gm, gb, lo, hi, dtype, first_row=0, name="dw_in_layout"):
    r, wm = gm.shape
    cs = (wm + hi - lo) // N_DEV
    tr = _pick_rows(r, 64)
    assert first_row % tr == 0
    b0 = first_row // tr

    def body(m_ref, b_ref, o_ref):
        m = m_ref[...]
        full = jnp.concatenate([m[:, :lo], b_ref[:, :hi - lo], m[:, lo:]], axis=1)
        for j in range(N_DEV):
            o_ref[j] = full[:, j * cs:(j + 1) * cs].astype(o_ref.dtype)

    return pl.pallas_call(
        body, name=name, grid=(r // tr,),
        in_specs=[pl.BlockSpec((tr, wm), lambda i: (i, 0)), pl.BlockSpec((tr, 128), lambda i: (i + b0, 0))],
        out_specs=pl.BlockSpec((N_DEV, tr, cs), lambda i: (0, i, 0)),
        out_shape=jax.ShapeDtypeStruct((N_DEV, r, cs), dtype), compiler_params=_cparams(("parallel",)))(gm, gb)


def _pack(arrs, dtype, lead=()):
    nlead = len(lead)
    flat = jnp.concatenate([a.astype(dtype).reshape(lead + (-1,)) for a in arrs], axis=nlead)
    n = flat.shape[-1]
    unit = PACK_WIDTH * PACK_ROWS
    pad = (-n) % unit
    flat = jnp.pad(flat, [(0, 0)] * nlead + [(0, pad)])
    return flat.reshape(lead + ((n + pad) // PACK_WIDTH, PACK_WIDTH))


def _unpack(buf, shapes, lead=()):
    flat = buf.reshape(lead + (-1,))
    out, off = [], 0
    for shp in shapes:
        n = math.prod(shp)
        out.append(flat[..., off:off + n].reshape(lead + tuple(shp)))
        off += n
    return out


def _adam_math(w, g, m, v):
    m = ADAM_B1 * m + (1.0 - ADAM_B1) * g
    v = ADAM_B2 * v + (1.0 - ADAM_B2) * (g * g)
    m_hat = m / (1.0 - ADAM_B1 ** ADAM_STEP)
    v_hat = v / (1.0 - ADAM_B2 ** ADAM_STEP)
    delta = -ADAM_LR * (m_hat / (jnp.sqrt(v_hat) + ADAM_EPS) + ADAM_WD * w)
    return delta, m, v


def _sum_adam(parts, w, m, v, name, own=None, me=None):
    r, c = w.shape
    parts_list = list(parts) if isinstance(parts, (list, tuple)) else [parts]
    own_list = list(own) if isinstance(own, (list, tuple)) else [own]
    nparts = parts_list[0].shape[0]
    lanes = -(-c // 128) * 128
    tr = _pick_rows(r // len(parts_list), max(8, (6 * 1024 * 1024) // (nparts * lanes * 4)))

    def finish(g, w_ref, m_ref, v_ref, g_ref, d_ref, nm_ref, nv_ref):
        d, nm, nv = _adam_math(w_ref[...], g, m_ref[...], v_ref[...])
        g_ref[...] = g
        d_ref[...] = d
        nm_ref[...] = nm
        nv_ref[...] = nv

    out_shape = [jax.ShapeDtypeStruct((r, c), F32)] * 4
    if own is None:
        def body(p_ref, *rest):
            g = p_ref[0].astype(F32)
            for j in range(1, nparts):
                g = g + p_ref[j].astype(F32)
            finish(g, *rest)

        row = pl.BlockSpec((tr, c), lambda i: (i, 0))
        return pl.pallas_call(
            body, name=name, grid=(r // tr,),
            in_specs=[pl.BlockSpec((nparts, tr, c), lambda i: (0, i, 0)), row, row, row],
            out_specs=[row] * 4, out_shape=out_shape, compiler_params=_cparams(("parallel",)))(parts, w, m, v)

    nch = len(parts_list)
    tpc = r // nch // tr

    def body(me_ref, *refs):
        p_refs, o_refs, rest = refs[:nch], refs[nch:2 * nch], refs[2 * nch:]
        i = pl.program_id(0)

        def of_chunk(vals):
            out = vals[0]
            for q in range(1, nch):
                out = jnp.where(i >= q * tpc, vals[q], out)
            return out

        mine = of_chunk([o[...].astype(F32) for o in o_refs])
        g = None
        for j in range(nparts):
            t = jnp.where(me_ref[0] == j, mine, of_chunk([p[j].astype(F32) for p in p_refs]))
            g = t if g is None else g + t
        finish(g, *rest)

    row = pl.BlockSpec((tr, c), lambda i, me_ref: (i, 0))
    step = lambda i, q: jnp.clip(i - q * tpc, 0, tpc - 1)
    return pl.pallas_call(
        body, name=name,
        grid_spec=pltpu.PrefetchScalarGridSpec(
            num_scalar_prefetch=1, grid=(r // tr,),
            in_specs=[pl.BlockSpec((nparts, tr, c), lambda i, me_ref, q=q: (0, step(i, q), 0)) for q in range(nch)]
            + [pl.BlockSpec((None, tr, c), lambda i, me_ref, q=q: (me_ref[0], step(i, q), 0)) for q in range(nch)]
            + [row, row, row],
            out_specs=[row] * 4),
        out_shape=out_shape, compiler_params=_cparams(("parallel",)))(me, *parts_list, *own_list, w, m, v)


def _block_diag(t):
    nb, g, a, b = t.shape
    eye = jnp.eye(g, dtype=t.dtype)
    return jnp.einsum('ngab,gh->ngahb', t, eye).reshape(nb, g * a, g * b)


def _diag_blocks(t, a, b):
    nb = t.shape[0]
    g = S5_GROUPS_PER_BLOCK
    t = t.reshape(nb, g, a, g, b)
    return jnp.stack([t[:, j, :, j, :] for j in range(g)], axis=1)


def _local_step(x, mem, target, p, late_weights=None, early_grads=None, last_grads=None):
    s, d = x.shape
    gw = d // 2
    nh = gw // GDN_HEAD_DIM
    ng = gw // S5_GROUP
    nb = ng // S5_GROUPS_PER_BLOCK
    nl = ng * S5_STATE
    ts = min(256, s)
    nt = s // ts
    grads = {}

    w_main, w_ba = p['w_main'], p['w_ba']
    CB_ZA, CB_XB, CB_ZB, CB_QC, CB_ZC, CB_G = 3, 4, 5, 6, 7, 8

    u = _tile_fwd(_rms, "rms_fwd", nt, [_rt(x, ts)], [p['norm_g']],
                  [((s, d), BF16, (ts, d), lambda i: (i, 0))])[0]
    proj = _mm(u, w_main, tm=1024, tn=2048, tk=512, out_dtype=BF16, name="proj_main")
    pba = _mm(u, w_ba, name="proj_ba")
    if late_weights is not None:
        p = {**p, **late_weights(proj)}

    conv_w = p['conv_w']
    col = lambda arr, cb: (arr, (s, GDN_HEAD_DIM), lambda i, cb=cb: (0, cb + i))
    qkv = []
    for j, mode in enumerate(('q', 'k', 'v')):
        off = j * nh
        qkv.append(_tile_fwd(
            _gdn_pre(mode), "gdn_pre_" + mode, nh, [col(proj, off), (conv_w, (CONV_WIDTH, GDN_HEAD_DIM), lambda i, off=off: (0, off + i))],
            [], [((s, gw), F32, (s, GDN_HEAD_DIM), lambda i: (0, i))])[0])
    q, k, v = qkv
    lane = jnp.arange(128)[:, None]
    colh = jnp.arange(gw)[None, :] // GDN_HEAD_DIM
    e_beta = (lane == colh).astype(F32)
    e_g = (lane == colh + nh).astype(F32)
    alog_row = jnp.pad(p['gdn_a_log'], ((0, 0), (nh, 128 - 2 * nh)))
    dtb_row = jnp.pad(p['gdn_dt_bias'], ((0, 0), (nh, 128 - 2 * nh)))
    row_gw = lambda: ((s, gw), F32, (ts, gw), lambda i: (i, 0))
    betab, gb = _tile_fwd(_gdn_gates, "gdn_gates", nt, [_rt(pba, ts)], [alog_row, dtb_row, e_beta, e_g],
                          [row_gw(), row_gw()])
    *intra, t_inv = _gdn_intra_fwd(q, k, v, gb, betab)
    o_raw, states = _gdn_inter_fwd(*intra, gb)
    ga = _tile_fwd(_gdn_post, "gdn_post", nt, [_rt(o_raw, ts), _rt(proj, ts, CB_ZA, gw)], [p['gdn_norm_g']],
                   [((s, gw), BF16, (ts, gw), lambda i: (i, 0))])[0]

    e_rep = (jnp.arange(S5_STATE)[:, None] == jnp.arange(S5_STATE * S5_GROUP)[None, :] // S5_GROUP).astype(F32)
    s5_in = [p['s5_lambda_re'], p['s5_lambda_im'], p['s5_log_dt'].reshape(ng, 1),
             p['s5_b_re'].reshape(ng, S5_STATE * S5_GROUP), p['s5_b_im'].reshape(ng, S5_STATE * S5_GROUP), e_rep]
    one = lambda shp: (shp, F32, shp, lambda i, n=len(shp): (0,) * n)
    ab_re, ab_im, bbr, bbi = _tile_fwd(_s5_params, "s5_params", 1, [], s5_in,
                                       [one((ng, S5_STATE)), one((ng, S5_STATE)), one((ng, S5_STATE * S5_GROUP)),
                                        one((ng, S5_STATE * S5_GROUP))])
    coef = _s5_coef(ab_re.reshape(1, nl), ab_im.reshape(1, nl))
    to_bd_b = lambda t: _block_diag(t.reshape(nb, S5_GROUPS_PER_BLOCK, S5_STATE, S5_GROUP).transpose(0, 1, 3, 2))
    to_bd_c = lambda t: _block_diag(t.reshape(nb, S5_GROUPS_PER_BLOCK, S5_GROUP, S5_STATE).transpose(0, 1, 3, 2))
    bbd_re, bbd_im = to_bd_b(bbr).astype(BF16), to_bd_b(bbi).astype(BF16)
    cbd_re, cbd_im = to_bd_c(p['s5_c_re']).astype(BF16), to_bd_c(p['s5_c_im']).astype(BF16)
    xb_arr = lax.slice_in_dim(proj, CB_XB * gw, (CB_XB + 1) * gw, axis=1)
    h_re, h_im, ylin = _s5_fwd(xb_arr, bbd_re, bbd_im, cbd_re, cbd_im, coef, ts)
    gl = _tile_fwd(_s5_post1, "s5_post1", nt, [_rt(ylin, ts), _rt(proj, ts, CB_XB, gw)], [p['s5_d']],
                   [((s, gw), BF16, (ts, gw), lambda i: (i, 0))])[0]
    tglu = _mm(gl, p['s5_w_glu'], b_shards=True, name="s5_glu")
    gbb = _tile_fwd(_s5_post2, "s5_post2", nt, [_rt(tglu, ts), _rt(proj, ts, CB_ZB, gw)], [],
                    [((s, gw), BF16, (ts, gw), lambda i: (i, 0))])[0]

    m_len = mem.shape[0]
    mem_n = _tile_fwd(_rms, "mem_rms", 1, [_rt(mem, m_len)], [p['mem_norm_g']],
                      [((m_len, d), BF16, (m_len, d), lambda i: (i, 0))])[0]
    kv = _mm(mem_n, p['w_kv_mem'], name="mem_kv")
    gcc = _tile_fwd(_attn, "attn", nt, [_rt(proj, ts, CB_QC, gw), _rt(proj, ts, CB_ZC, gw)], [kv],
                    [((s, gw), BF16, (ts, gw), lambda i: (i, 0))])[0]

    p_a = _mm(ga, p['w_br_a'], b_shards=True, name="br_a")
    p_b = _mm(gbb, p['w_br_b'], b_shards=True, name="br_b")
    p_c = _mm(gcc, p['w_br_c'], b_shards=True, name="br_c")
    gate_acts = [_rt(proj, ts, CB_G // 2 + j, d) for j in range(3)]
    merged = _tile_fwd(_merge, "merge", nt, gate_acts + [_rt(p_a, ts), _rt(p_b, ts), _rt(p_c, ts)], [],
                       [((s, d), BF16, (ts, d), lambda i: (i, 0))])[0]
    mo = _mm(merged, p['w_out'], name="out_proj")
    dh, dfg, loss = _final(x, mo, target, p['final_g'].reshape(1, d), ts)
    grads['final_g'] = dfg.reshape(d)

    dmerged = _mm(dh, p['w_out'], tb=True, name="d_merged")
    grads['w_out'] = _mm(merged, dh, ta=True, name="dw_out")
    row_d = lambda dt: ((s, d), dt, (ts, d), lambda i: (i, 0))
    dg0, dg1, dg2, dpa, dpb, dpc = _tile_bwd(
        _merge, "merge_bwd", nt, gate_acts + [_rt(p_a, ts), _rt(p_b, ts), _rt(p_c, ts)], [], [_rt(dmerged, ts)],
        [row_d(BF16)] * 6, [])
    dga = _mm(dpa, p['w_br_a'], tb=True, b_shards=True, name="d_ga")
    dgbb = _mm(dpb, p['w_br_b'], tb=True, b_shards=True, name="d_gb")
    dgcc = _mm(dpc, p['w_br_c'], tb=True, b_shards=True, name="d_gc")
    grads['w_br_a'] = _mm(ga, dpa, ta=True, out_shards=N_DEV, name="dw_br_a")
    grads['w_br_b'] = _mm(gbb, dpb, ta=True, out_shards=N_DEV, name="dw_br_b")
    grads['w_br_c'] = _mm(gcc, dpc, ta=True, out_shards=N_DEV, name="dw_br_c")
    row_h = lambda dt: ((s, gw), dt, (ts, gw), lambda i: (i, 0))

    dqc, dzc, dkv = _tile_bwd(_attn, "attn_bwd", nt, [_rt(proj, ts, CB_QC, gw), _rt(proj, ts, CB_ZC, gw)], [kv],
                              [_rt(dgcc, ts)], [row_h(BF16), row_h(BF16)], [True])
    grads['w_kv_mem'] = _mm(mem_n, dkv, ta=True, name="dw_kv")
    dmem_n = _mm(dkv, p['w_kv_mem'], tb=True, name="d_mem_n")
    grads['mem_norm_g'] = _tile_bwd(_rms, "mem_rms_bwd", 1, [_rt(mem, m_len)], [p['mem_norm_g']],
                                    [_rt(dmem_n, m_len)], [None], [True])[0]

    dtglu, dzb = _tile_bwd(_s5_post2, "s5_post2_bwd", nt, [_rt(tglu, ts), _rt(proj, ts, CB_ZB, gw)], [],
                           [_rt(dgbb, ts)], [((s, 2 * gw), BF16, (ts, 2 * gw), lambda i: (i, 0)), row_h(BF16)], [])
    grads['s5_w_glu'] = _mm(gl, dtglu, ta=True, out_shards=N_DEV, name="dw_glu")
    s5_d = p['s5_d']
    if early_grads is not None:
        s5_d = s5_d + early_grads(grads)[:1, :1]
    dgl = _mm(dtglu, p['s5_w_glu'], tb=True, b_shards=True, name="d_gl")
    dylin, dxb1, dd = _tile_bwd(_s5_post1, "s5_post1_bwd", nt, [_rt(ylin, ts), _rt(proj, ts, CB_XB, gw)],
                                [s5_d], [_rt(dgl, ts)], [row_h(F32), row_h(F32)], [True])
    grads['s5_d'] = dd
    dxb2, dbbd_re, dbbd_im, dcbd_re, dcbd_im, da_re, da_im = _s5_bwd(dylin, xb_arr, h_re, h_im, bbd_re, bbd_im,
                                                                    cbd_re, cbd_im, coef, ts)
    from_bd_b = lambda t: _diag_blocks(t, S5_GROUP, S5_STATE).transpose(0, 1, 3, 2).reshape(ng, S5_STATE * S5_GROUP)
    from_bd_c = lambda t: _diag_blocks(t, S5_STATE, S5_GROUP).transpose(0, 1, 3, 2).reshape(1, ng, S5_GROUP, S5_STATE)
    grads['s5_c_re'], grads['s5_c_im'] = from_bd_c(dcbd_re), from_bd_c(dcbd_im)
    s5_cts = [jnp.sum(da_re, axis=0).reshape(ng, S5_STATE), jnp.sum(da_im, axis=0).reshape(ng, S5_STATE),
              from_bd_b(dbbd_re), from_bd_b(dbbd_im)]
    dlr, dli, dlogdt, dbr, dbi = _tile_bwd(_s5_params, "s5_params_bwd", 1, [], s5_in,
                                           [(c, c.shape, lambda i: (0, 0)) for c in s5_cts], [],
                                           [True, True, True, True, True, False])
    grads['s5_lambda_re'], grads['s5_lambda_im'] = dlr[None], dli[None]
    grads['s5_log_dt'] = dlogdt.reshape(1, ng)
    grads['s5_b_re'] = dbr.reshape(1, ng, S5_STATE, S5_GROUP)
    grads['s5_b_im'] = dbi.reshape(1, ng, S5_STATE, S5_GROUP)
    dxb = (dxb1 + dxb2).astype(BF16)

    do_raw, dza, dgng = _tile_bwd(_gdn_post, "gdn_post_bwd", nt, [_rt(o_raw, ts), _rt(proj, ts, CB_ZA, gw)],
                                  [p['gdn_norm_g']], [_rt(dga, ts)], [row_h(F32), row_h(BF16)], [True])
    grads['gdn_norm_g'] = dgng
    *intra_cts, dgb_inter = _gdn_inter_bwd(*intra, gb, states, do_raw)
    dq, dk, dv, dgb, dbetab = _gdn_intra_bwd(q, k, v, gb, betab, t_inv, intra_cts, dgb_inter)
    dpba, dalog, ddtb = _tile_bwd(_gdn_gates, "gdn_gates_bwd", nt, [_rt(pba, ts)], [alog_row, dtb_row, e_beta, e_g],
                                  [_rt(dbetab, ts), _rt(dgb, ts)], [((s, 128), BF16, (ts, 128), lambda i: (i, 0))],
                                  [True, True, False, False])
    grads['gdn_a_log'] = dalog[:, nh:2 * nh]
    grads['gdn_dt_bias'] = ddtb[:, nh:2 * nh]
    dqkv, dconv = [], []
    for j, (mode, ct) in enumerate((('q', dq), ('k', dk), ('v', dv))):
        off = j * nh
        wspec = (conv_w, (CONV_WIDTH, GDN_HEAD_DIM), lambda i, off=off: (0, off + i))
        dxc, dwc = _tile_bwd(
            _gdn_pre(mode), "gdn_pre_bwd_" + mode, nh, [col(proj, off), wspec], [], [col(ct, 0)],
            [((s, gw), BF16, (s, GDN_HEAD_DIM), lambda i: (0, i)),
             ((CONV_WIDTH, gw), F32, (CONV_WIDTH, GDN_HEAD_DIM), lambda i: (0, i))], [])
        dqkv.append(dxc)
        dconv.append(dwc)
    grads['conv_w'] = jnp.concatenate(dconv, axis=1)

    dproj = jnp.concatenate(dqkv + [dza, dxb, dzb, dqc, dzc, dg0, dg1, dg2], axis=1)
    grads['w_ba'] = _mm(u, dpba, ta=True, name="dw_ba")
    if last_grads is None:
        grads['w_main'] = _mm(u, dproj, ta=True, tm=1024, tn=2048, tk=512, name="dw_main")
    else:
        uu = u
        for h in range(2):
            gm = _mm(uu, dproj, ta=True, tm=1024, tn=2048, tk=512, rows=(h * (d // 2), d // 2), name=f"dw_main_{h}")
            tok = last_grads(h, gm, grads)[:1, :1].astype(BF16)
            if h == 0:
                uu = uu + tok
            else:
                dpba = dpba + tok
    du = _mm(dpba, w_ba, tb=True, name="du_ba")
    du = _mm(dproj, w_main, tb=True, addend=du, tm=512, tn=2048, tk=1024, name="du_main")
    grad_x, dng = _tile_bwd(_rms, "rms_bwd", nt, [_rt(x, ts)], [p['norm_g']], [_rt(du, ts)],
                            [row_d(F32) + (dh,)], [True])
    grads['norm_g'] = dng
    return loss, grad_x, grads


def _to_shards(name, g):
    if SHARDED[name] == 'row':
        return g.reshape((N_DEV, g.shape[0] // N_DEV) + g.shape[1:])
    r, c = g.shape
    return g.reshape(r, N_DEV, c // N_DEV).transpose(1, 0, 2)


def _from_shards(name, t):
    if SHARDED[name] == 'row':
        return t.reshape((t.shape[0] * t.shape[1],) + t.shape[2:])
    n, r, c = t.shape
    return t.transpose(1, 0, 2).reshape(r, n * c)


def _step(x, mem, target, w, m, v):
    sharded = list(SHARDED)
    shard_shapes = {n: tuple(w[n].shape[1:]) for n in sharded}
    d = x.shape[-1]
    ba_lo = 2 * d
    ba_hi = ba_lo + 2 * (d // 2 // GDN_HEAD_DIM)

    w_in_all = _gather([w['w_in'][0].astype(BF16)], "gather_w_in")[0]
    after_w_in = (w_in_all[0, :1, :1] * 0).astype(F32)
    late = [w[n][0].astype(BF16) for n in OVERLAPPED] + [w['conv_w'][0] + after_w_in]
    every = [_send_whole] * len(late)
    lands = [jnp.broadcast_to(t[None], (N_DEV,) + t.shape) for t in late]
    gather_started, token = _direct_start(late, lands, every, "gather_rest_start")
    full = {}
    full['w_main'], full['w_ba'] = _w_in_from_shards(w_in_all, ba_lo, ba_hi)
    for n in REPLICATED:
        full[n] = w[n]
    for n in ('s5_lambda_re', 's5_lambda_im', 's5_c_re', 's5_c_im'):
        full[n] = w[n][0]
    full['norm_g'] = w['norm_g'] + token[:1, :1]

    def late_weights(proj):
        got = dict(zip(OVERLAPPED + ['conv_w'], _direct_wait(gather_started, every, proj, "gather_rest_wait")[1]))
        for n in ('w_kv_mem', 'w_out', 'conv_w'):
            got[n] = _from_shards(n, got[n])
        return got

    slots = [_send_slot] * len(OVERLAPPED)
    scatter_started = []

    def early_grads(grads):
        gs = [_to_shards(n, grads[n]) if SHARDED[n] == 'row' else grads[n] for n in OVERLAPPED]
        started, tok = _direct_start(gs, [lax.empty(g.shape, g.dtype) for g in gs], slots, "scatter_early_start")
        scatter_started.append(started)
        return tok

    def last_grads(h, gm, grads):
        gs = [_w_in_to_shards(gm, grads['w_ba'], ba_lo, ba_hi, BF16, h * gm.shape[0], f"dw_in_layout_{h}")]
        if h == 0:
            gs.append(_to_shards('conv_w', grads['conv_w']))
        started, tok = _direct_start(gs, [lax.empty(g.shape, g.dtype) for g in gs], slots[:len(gs)],
                                     f"scatter_last_start_{h}")
        scatter_started.append(started)
        return tok

    loss, grad_x, grads = _local_step(x[0], mem[0], target[0], full, late_weights, early_grads, last_grads)
    res = {}
    me = (4 * lax.axis_index("x") + 2 * lax.axis_index("y") + lax.axis_index("c")).astype(jnp.int32).reshape(1)

    def update(names, exchanged):
        for n, own, part in zip(names, *exchanged):
            outs = _sum_adam(part, w[n][0], m[n][0], v[n][0], name="adam_" + n, own=own, me=me)
            for kind, t in zip(('grad', 'delta', 'new_m', 'new_v'), outs):
                res[kind, n] = t[None]

    update(OVERLAPPED, _direct_wait(scatter_started[0], slots, grad_x, "scatter_early_wait"))

    small = _pack([grads[n].reshape(w[n].shape) for n in REPLICATED] + [loss[:1, :1]], F32)
    allp = _gather([small], "gather_small")[0]
    zero = jnp.zeros((1, 1), F32)
    outs = _sum_adam(allp, *[_pack([t[n] for n in REPLICATED] + [zero], F32) for t in (w, m, v)], name="adam_small")
    shapes = [w[n].shape for n in REPLICATED] + [(1, 1)]
    for kind, buf in zip(('grad', 'delta', 'new_m', 'new_v'), outs):
        got = _unpack(buf, shapes)
        for n, t in zip(REPLICATED, got):
            res[kind, n] = t
        if kind == 'grad':
            total_loss = got[-1].reshape(())
    (own0, own_conv), (got0, got_conv) = _direct_wait(scatter_started[1], slots[:2], outs[0], "scatter_last_wait_0")
    (own1,), (got1,) = _direct_wait(scatter_started[2], slots[:1], outs[0], "scatter_last_wait_1")
    update(['w_in', 'conv_w'], ([[own0, own1], own_conv], [[got0, got1], got_conv]))
    out = [total_loss, grad_x[None]]
    for kind in ('grad', 'delta', 'new_m', 'new_v'):
        out += [res[kind, n] for n in WEIGHTS]
    return tuple(out)


def kernel(x, mem, norm_g, w_in, conv_w, gdn_a_log, gdn_dt_bias, gdn_norm_g, s5_lambda_re, s5_lambda_im, s5_log_dt, s5_b_re, s5_b_im, s5_c_re, s5_c_im, s5_d, s5_w_glu, mem_norm_g, w_kv_mem, w_br_a, w_br_b, w_br_c, w_out, final_g, loss_target, m_norm_g, m_w_in, m_conv_w, m_gdn_a_log, m_gdn_dt_bias, m_gdn_norm_g, m_s5_lambda_re, m_s5_lambda_im, m_s5_log_dt, m_s5_b_re, m_s5_b_im, m_s5_c_re, m_s5_c_im, m_s5_d, m_s5_w_glu, m_mem_norm_g, m_w_kv_mem, m_w_br_a, m_w_br_b, m_w_br_c, m_w_out, m_final_g, v_norm_g, v_w_in, v_conv_w, v_gdn_a_log, v_gdn_dt_bias, v_gdn_norm_g, v_s5_lambda_re, v_s5_lambda_im, v_s5_log_dt, v_s5_b_re, v_s5_b_im, v_s5_c_re, v_s5_c_im, v_s5_d, v_s5_w_glu, v_mem_norm_g, v_w_kv_mem, v_w_br_a, v_w_br_b, v_w_br_c, v_w_out, v_final_g):
    a = dict(locals())
    w = {n: a[n] for n in WEIGHTS}
    m = {n: a['m_' + n] for n in WEIGHTS}
    v = {n: a['v_' + n] for n in WEIGHTS}
    return _step(x, mem, loss_target, w, m, v)
```

```python
import functools
import math

import jax
import jax.numpy as jnp
from jax import lax
from jax.experimental import pallas as pl
from jax.experimental.pallas import tpu as pltpu

F32 = jnp.float32
BF16 = jnp.bfloat16
HI = lax.Precision.HIGHEST

EPS = 1e-6
CHUNK = 64
GDN_HEAD_DIM = 128
CONV_WIDTH = 4
S5_GROUP = 16
S5_STATE = 64
S5_GROUPS_PER_BLOCK = 8
XA_HEADS = 4
N_DEV = 8
ADAM_LR, ADAM_B1, ADAM_B2, ADAM_EPS, ADAM_WD, ADAM_STEP = 0.001, 0.9, 0.999, 1e-08, 0.01, 10

VMEM_LIMIT_BYTES = 56 * 1024 * 1024
SCAN_LANES = 512
PACK_WIDTH = 512
PACK_ROWS = 256

WEIGHTS = ['norm_g', 'w_in', 'conv_w', 'gdn_a_log', 'gdn_dt_bias', 'gdn_norm_g', 's5_lambda_re', 's5_lambda_im',
           's5_log_dt', 's5_b_re', 's5_b_im', 's5_c_re', 's5_c_im', 's5_d', 's5_w_glu', 'mem_norm_g', 'w_kv_mem',
           'w_br_a', 'w_br_b', 'w_br_c', 'w_out', 'final_g']
SHARDED = {'w_in': 'col', 'conv_w': 'col', 's5_w_glu': 'col', 'w_kv_mem': 'row', 'w_br_a': 'col', 'w_br_b': 'col',
           'w_br_c': 'col', 'w_out': 'row'}
GATHER_BF16 = ['w_in', 's5_w_glu', 'w_kv_mem', 'w_br_a', 'w_br_b', 'w_br_c', 'w_out']
REPLICATED = [n for n in WEIGHTS if n not in SHARDED]
OVERLAPPED = ['s5_w_glu', 'w_kv_mem', 'w_br_a', 'w_br_b', 'w_br_c', 'w_out']


def _cparams(sem=None):
    return pltpu.CompilerParams(dimension_semantics=sem, vmem_limit_bytes=VMEM_LIMIT_BYTES)


def _pick(dim, pref):
    t = (min(pref, dim) // 128) * 128
    while t >= 128:
        if dim % t == 0:
            return t
        t -= 128
    return dim


def _make_dots(prep, precision):
    def raw(a, b, dims):
        return lax.dot_general(prep(a), prep(b), (dims, ((), ())), preferred_element_type=F32, precision=precision)

    @jax.custom_vjp
    def nn(a, b):
        return raw(a, b, ((1,), (0,)))

    @jax.custom_vjp
    def nt(a, b):
        return raw(a, b, ((1,), (1,)))

    @jax.custom_vjp
    def tn(a, b):
        return raw(a, b, ((0,), (0,)))

    nn.defvjp(lambda a, b: (nn(a, b), (a, b)), lambda r, ct: (nt(ct, r[1]), tn(r[0], ct)))
    nt.defvjp(lambda a, b: (nt(a, b), (a, b)), lambda r, ct: (nn(ct, r[1]), tn(ct, r[0])))
    tn.defvjp(lambda a, b: (tn(a, b), (a, b)), lambda r, ct: (nt(r[1], ct), nn(r[0], ct)))
    return nn, nt, tn


_bnn, _bnt, _btn = _make_dots(lambda a: a.astype(BF16), None)
_hnn, _hnt, _htn = _make_dots(lambda a: a.astype(F32), HI)
_mnn, _mnt, _mtn = _make_dots(lambda a: a.astype(F32), lax.Precision.HIGH)


def _mm(a, b, *, name, ta=False, tb=False, out_dtype=F32, addend=None, tm=512, tn=1024, tk=1024, b_shards=False,
        out_shards=0, rows=None):
    m, k = (a.shape[1], a.shape[0]) if ta else a.shape
    brows, bcols = (b.shape[1], b.shape[0] * b.shape[2]) if b_shards else b.shape
    n = brows if tb else bcols
    assert (bcols if tb else brows) == k, (a.shape, b.shape, ta, tb)
    first_row = 0
    if rows is not None:
        first_row, m = rows
    tm, tn, tk = _pick(m, tm), _pick(n, tn), _pick(k, tk)
    assert first_row % tm == 0
    r0 = first_row // tm
    bcs = ocs = 0
    if b_shards:
        bcs = b.shape[2]
        assert bcs % 128 == 0 and (tk if tb else tn) % bcs == 0
    if out_shards:
        ocs = n // out_shards
        assert ocs % 128 == 0 and tn % ocs == 0
    nk = k // tk
    dims = ((0 if ta else 1,), (1 if tb else 0,))

    def body(*refs):
        if addend is None:
            a_ref, b_ref, o_ref, acc_ref = refs
        else:
            a_ref, b_ref, add_ref, o_ref, acc_ref = refs
        kk = pl.program_id(2)

        @pl.when(kk == 0)
        def _():
            acc_ref[...] = jnp.zeros_like(acc_ref)

        dot = lambda x, y: lax.dot_general(x.astype(BF16), y.astype(BF16), (dims, ((), ())), preferred_element_type=F32)
        if not b_shards:
            acc_ref[...] += dot(a_ref[...], b_ref[...])
        elif tb:
            for g in range(tk // bcs):
                acc_ref[...] += dot(a_ref[:, g * bcs:(g + 1) * bcs], b_ref[g])
        else:
            for g in range(tn // bcs):
                acc_ref[:, g * bcs:(g + 1) * bcs] += dot(a_ref[...], b_ref[g])

        @pl.when(kk == nk - 1)
        def _():
            r = acc_ref[...]
            if addend is not None:
                r = r + add_ref[...].astype(F32)
            if out_shards:
                for g in range(tn // ocs):
                    o_ref[g] = r[:, g * ocs:(g + 1) * ocs].astype(o_ref.dtype)
            else:
                o_ref[...] = r.astype(o_ref.dtype)

    a_spec = (pl.BlockSpec((tk, tm), lambda i, j, kk: (kk, i + r0)) if ta
              else pl.BlockSpec((tm, tk), lambda i, j, kk: (i + r0, kk)))
    if b_shards:
        b_spec = (pl.BlockSpec((tk // bcs, tn, bcs), lambda i, j, kk: (kk, j, 0)) if tb
                  else pl.BlockSpec((tn // bcs, tk, bcs), lambda i, j, kk: (j, kk, 0)))
    else:
        b_spec = (pl.BlockSpec((tn, tk), lambda i, j, kk: (j, kk)) if tb
                  else pl.BlockSpec((tk, tn), lambda i, j, kk: (kk, j)))
    if out_shards:
        o_spec = pl.BlockSpec((tn // ocs, tm, ocs), lambda i, j, kk: (j, i, 0))
        out_shape = jax.ShapeDtypeStruct((out_shards, m, ocs), out_dtype)
    else:
        o_spec = pl.BlockSpec((tm, tn), lambda i, j, kk: (i, j))
        out_shape = jax.ShapeDtypeStruct((m, n), out_dtype)
    in_specs = [a_spec, b_spec] + ([o_spec] if addend is not None else [])
    args = (a, b) + ((addend,) if addend is not None else ())
    return pl.pallas_call(
        body, name=name, grid=(m // tm, n // tn, nk), in_specs=in_specs, out_specs=o_spec,
        out_shape=out_shape, scratch_shapes=[pltpu.VMEM((tm, tn), F32)],
        compiler_params=_cparams(("parallel", "parallel", "arbitrary")))(*args)


def _rt(arr, ts, cb=0, w=None):
    w = arr.shape[1] if w is None else w
    return (arr, (ts, w), lambda i, cb=cb: (i, cb))


def _whole(p):
    return pl.BlockSpec(p.shape, lambda i, nd=p.ndim: (0,) * nd)


def _tile_fwd(f, name, n, acts, params, outs):
    na, npar = len(acts), len(params)

    def body(*refs):
        res = f(*[r[...] for r in refs[:na + npar]])
        for r, v in zip(refs[na + npar:], res):
            r[...] = v.astype(r.dtype)

    in_specs = [pl.BlockSpec(b, m) for _, b, m in acts] + [_whole(p) for p in params]
    out = pl.pallas_call(
        body, name=name, grid=(n,), in_specs=in_specs,
        out_specs=[pl.BlockSpec(b, m) for _, _, b, m in outs],
        out_shape=[jax.ShapeDtypeStruct(s, d) for s, d, _, _ in outs],
        compiler_params=_cparams(("parallel",)))(*[a for a, _, _ in acts], *params)
    return out


def _tile_bwd(f, name, n, acts, params, cts, agrads, pgrads):
    na, npar, nc = len(acts), len(params), len(cts)
    adds = [g[4] for g in agrads if g is not None and len(g) == 5]

    def body(*refs):
        i = pl.program_id(0)
        ins = [r[...] for r in refs[:na + npar]]
        outs, vjp = jax.vjp(f, *ins)
        g = vjp(tuple(c[...].astype(o.dtype) for c, o in zip(refs[na + npar:na + npar + nc], outs)))
        add_refs = refs[na + npar + nc:na + npar + nc + len(adds)]
        orefs = refs[na + npar + nc + len(adds):]
        k = 0
        for j in range(na):
            if agrads[j] is not None:
                val = g[j]
                if len(agrads[j]) == 5:
                    val = val + add_refs[[id(t) for t in adds].index(id(agrads[j][4]))][...]
                orefs[k][...] = val.astype(orefs[k].dtype)
                k += 1
        for j in range(npar):
            if pgrads[j]:
                o = orefs[k]

                @pl.when(i == 0)
                def _(o=o):
                    o[...] = jnp.zeros_like(o)

                o[...] += g[na + j].astype(F32)
                k += 1

    in_specs = ([pl.BlockSpec(b, m) for _, b, m in acts] + [_whole(p) for p in params]
                + [pl.BlockSpec(b, m) for _, b, m in cts]
                + [pl.BlockSpec(g[2], g[3]) for g in agrads if g is not None and len(g) == 5])
    out_specs = [pl.BlockSpec(g[2], g[3]) for g in agrads if g is not None]
    out_shape = [jax.ShapeDtypeStruct(g[0], g[1]) for g in agrads if g is not None]
    for p, flag in zip(params, pgrads):
        if flag:
            out_specs.append(_whole(p))
            out_shape.append(jax.ShapeDtypeStruct(p.shape, F32))
    return pl.pallas_call(
        body, name=name, grid=(n,), in_specs=in_specs, out_specs=out_specs, out_shape=out_shape,
        compiler_params=_cparams(("arbitrary",)))(*[a for a, _, _ in acts], *params, *[c for c, _, _ in cts], *adds)


def _silu(x):
    return x * jax.nn.sigmoid(x)


def _rms(x, g):
    x = x.astype(F32)
    return (x * lax.rsqrt(jnp.mean(x * x, axis=-1, keepdims=True) + EPS) * g,)


def _shift_down(x, s):
    row = lax.broadcasted_iota(jnp.int32, x.shape, 0)
    return jnp.where(row >= s, pltpu.roll(x, s, 0), 0.0)


def _shift_up(x, s):
    n = x.shape[0]
    row = lax.broadcasted_iota(jnp.int32, x.shape, 0)
    return jnp.where(row < n - s, pltpu.roll(x, n - s, 0), 0.0)


@functools.partial(jax.custom_vjp, nondiff_argnums=(1,))
def _shift(x, s):
    return _shift_down(x, s)


_shift.defvjp(lambda x, s: (_shift_down(x, s), None), lambda s, _, ct: (_shift_up(ct, s),))


def _gdn_pre(mode):
    def f(x, w):
        x = x.astype(F32)
        y = x * w[CONV_WIDTH - 1:CONV_WIDTH, :]
        for j in range(CONV_WIDTH - 1):
            y = y + _shift(x, CONV_WIDTH - 1 - j) * w[j:j + 1, :]
        y = _silu(y)
        if mode != 'v':
            y = y * lax.rsqrt(jnp.sum(y * y, axis=-1, keepdims=True) + EPS)
        if mode == 'q':
            y = y * (GDN_HEAD_DIM ** -0.5)
        return (y,)
    return f


def _softplus(x):
    return jnp.maximum(x, 0.0) + jnp.log1p(jnp.exp(-jnp.abs(x)))


def _gdn_gates(ba, alog, dtb, e_beta, e_g):
    beta = jax.nn.sigmoid(ba)
    g = -jnp.exp(alog) * _softplus(ba + dtb)
    return _hnn(beta, lax.stop_gradient(e_beta)), _hnn(g, lax.stop_gradient(e_g))


@jax.custom_vjp
def _inverse_known(neg, t):
    return t


_inverse_known.defvjp(lambda neg, t: (t, t), lambda t, ct: (_mtn(t, _mnt(ct, t)), jnp.zeros_like(t)))


def _gdn_intra(q, k, v, gb, bb, t_known=None):
    n, c = len(q), q[0].shape[0]
    ri = lax.broadcasted_iota(jnp.int32, (c, c), 0)
    ci = lax.broadcasted_iota(jnp.int32, (c, c), 1)
    incl, strict = ri >= ci, ri > ci
    tri = incl.astype(F32)
    eye = (ri == ci).astype(F32)
    each = range(n)
    gc = [_hnn(tri, gb[i]) for i in each]
    decay = [jnp.exp(jnp.where(incl, gc[i][:, :c] - gc[i].T[:c, :], -1e30)) for i in each]
    kb = [k[i] * bb[i] for i in each]
    kk = [_bnt(kb[i], k[i]) for i in each]
    qk = [_bnt(q[i], k[i]) for i in each]
    p = [jnp.where(strict, -(kk[i] * decay[i]), 0.0) for i in each]
    if t_known is None:
        t = [eye + p[i] for i in each]
        for _ in range(int(math.log2(c)) - 1):
            p = [_mnn(p[i], p[i]) for i in each]
            tp = [_mnn(t[i], p[i]) for i in each]
            t = [t[i] + tp[i] for i in each]
    else:
        t = [_inverse_known(p[i], t_known[i]) for i in each]
    egc = [jnp.exp(gc[i]) for i in each]
    u_val = [_mnn(t[i], v[i] * bb[i]) for i in each]
    w_dec = [_mnn(t[i], kb[i] * egc[i]) for i in each]
    qk = [qk[i] * decay[i] for i in each]
    gl = [jnp.sum(gb[i], axis=0, keepdims=True) for i in each]
    return w_dec, u_val, qk, [q[i] * egc[i] for i in each], [k[i] * jnp.exp(gl[i] - gc[i]) for i in each], t


def _gdn_inter(w_dec, u_val, qk, q_dec, k_dec, gb, state):
    each = range(len(state))
    ws = [_bnn(w_dec[i], state[i]) for i in each]
    qs = [_bnn(q_dec[i], state[i]) for i in each]
    v_new = [u_val[i] - ws[i] for i in each]
    qv = [_bnn(qk[i], v_new[i]) for i in each]
    kv = [_btn(k_dec[i], v_new[i]) for i in each]
    decayed = [state[i] * jnp.exp(jnp.sum(gb[i], axis=0, keepdims=True)) for i in each]
    return [qs[i] + qv[i] for i in each], [decayed[i] + kv[i] for i in each]


def _gdn_post(o, z, g):
    parts = []
    for h in range(o.shape[1] // GDN_HEAD_DIM):
        oh = o[:, h * GDN_HEAD_DIM:(h + 1) * GDN_HEAD_DIM]
        parts.append(oh * lax.rsqrt(jnp.mean(oh * oh, axis=-1, keepdims=True) + EPS) * g)
    y = parts[0] if len(parts) == 1 else jnp.concatenate(parts, axis=1)
    return (y * _silu(z.astype(F32)),)


def _gelu(x):
    return 0.5 * x * (1.0 + jnp.tanh(0.7978845608028654 * (x + 0.044715 * x * x * x)))


def _s5_post1(ylin, xb, d):
    return (_gelu(ylin + d * xb.astype(F32)),)


def _s5_post2(t, z):
    w = t.shape[1] // 2
    return (t[:, :w] * jax.nn.sigmoid(t[:, w:]) * _silu(z.astype(F32)),)


def _attn(q, z, kv):
    w = q.shape[1]
    hd = w // XA_HEADS
    parts = []
    for h in range(XA_HEADS):
        s = _bnt(q[:, h * hd:(h + 1) * hd], kv[:, h * hd:(h + 1) * hd]) * (hd ** -0.5)
        s = s - jnp.max(s, axis=-1, keepdims=True)
        e = jnp.exp(s)
        p = e / jnp.sum(e, axis=-1, keepdims=True)
        parts.append(_bnn(p, kv[:, w + h * hd:w + (h + 1) * hd]))
    return (jnp.concatenate(parts, axis=1) * _silu(z.astype(F32)),)


def _merge(g0, g1, g2, pa, pb, pc):
    s0, s1, s2 = (jax.nn.sigmoid(t.astype(F32)) for t in (g0, g1, g2))
    return (s0 * pa + s1 * pb + s2 * pc,)


def _s5_params(lr, li, logdt, br, bi, e):
    dt = jnp.exp(logdt)
    mag = jnp.exp(lr * dt)
    ab_re, ab_im = mag * jnp.cos(li * dt), mag * jnp.sin(li * dt)
    den = lr * lr + li * li
    nr, ni = ab_re - 1.0, ab_im
    e = lax.stop_gradient(e)
    cre = _hnn((nr * lr + ni * li) / den, e)
    cim = _hnn((ni * lr - nr * li) / den, e)
    return ab_re, ab_im, cre * br - cim * bi, cre * bi + cim * br


def _gdn_blocks(s, w, per_step):
    nh, nc = w // GDN_HEAD_DIM, s // CHUNK
    cpb = math.gcd(per_step, nc)
    return nh, nc, cpb, nc // cpb, (cpb * CHUNK, w), (cpb * CHUNK, nh * CHUNK)


def _gdn_pairs(cpb, nh):
    wide, narrow = [], []
    for cb in range(cpb):
        rows = slice(cb * CHUNK, (cb + 1) * CHUNK)
        for h in range(nh):
            wide.append((rows, slice(h * GDN_HEAD_DIM, (h + 1) * GDN_HEAD_DIM)))
            narrow.append((rows, slice(h * CHUNK, (h + 1) * CHUNK)))
    return wide, narrow


def _gdn_intra_fwd(q, k, v, gb, bb, per_step=4):
    s, w = q.shape
    nh, nc, cpb, n, wide, narrow = _gdn_blocks(s, w, per_step)

    def body(q_ref, k_ref, v_ref, g_ref, b_ref, wd_ref, uv_ref, qk_ref, qd_ref, kd_ref, t_ref):
        wide, narrow = _gdn_pairs(cpb, nh)
        res = _gdn_intra(*[[r[ix] for ix in wide] for r in (q_ref, k_ref, v_ref, g_ref, b_ref)])
        for ref, vals, where in zip((wd_ref, uv_ref, qk_ref, qd_ref, kd_ref, t_ref), res,
                                    (wide, wide, narrow, wide, wide, narrow)):
            for ix, val in zip(where, vals):
                ref[ix] = val.astype(ref.dtype)

    bw = pl.BlockSpec(wide, lambda i: (i, 0))
    bn = pl.BlockSpec(narrow, lambda i: (i, 0))
    shape = lambda dt, narrow_one: jax.ShapeDtypeStruct((s, nh * CHUNK if narrow_one else w), dt)
    return pl.pallas_call(
        body, name="gdn_intra", grid=(n,), in_specs=[bw] * 5, out_specs=[bw, bw, bn, bw, bw, bn],
        out_shape=[shape(BF16, False), shape(F32, False), shape(BF16, True), shape(BF16, False), shape(BF16, False),
                   shape(F32, True)],
        compiler_params=_cparams(("parallel",)))(q, k, v, gb, bb)


def _gdn_intra_bwd(q, k, v, gb, bb, t, cts, dgb_inter, per_step=4):
    s, w = q.shape
    nh, nc, cpb, n, wide, narrow = _gdn_blocks(s, w, per_step)

    def body(q_ref, k_ref, v_ref, g_ref, b_ref, t_ref, cwd, cuv, cqk, cqd, ckd, dgi, dq_ref, dk_ref, dv_ref, dg_ref,
             db_ref):
        wide, narrow = _gdn_pairs(cpb, nh)
        t_known = [t_ref[ix] for ix in narrow]
        _, vjp = jax.vjp(lambda *a: _gdn_intra(*a, t_known=t_known)[:5],
                         *[[r[ix] for ix in wide] for r in (q_ref, k_ref, v_ref, g_ref, b_ref)])
        cts = tuple([r[ix] for ix in where] for r, where in zip((cwd, cuv, cqk, cqd, ckd),
                                                               (wide, wide, narrow, wide, wide)))
        dq, dk, dv, dg, db = vjp(cts)
        for j, ix in enumerate(wide):
            dq_ref[ix], dk_ref[ix], dv_ref[ix], db_ref[ix] = dq[j], dk[j], dv[j], db[j]
            dg_ref[ix] = dg[j] + dgi[ix]

    bw = pl.BlockSpec(wide, lambda i: (i, 0))
    bn = pl.BlockSpec(narrow, lambda i: (i, 0))
    return pl.pallas_call(
        body, name="gdn_intra_bwd", grid=(n,), in_specs=[bw] * 5 + [bn, bw, bw, bn, bw, bw, bw], out_specs=[bw] * 5,
        out_shape=[jax.ShapeDtypeStruct((s, w), F32)] * 5,
        compiler_params=_cparams(("parallel",)))(q, k, v, gb, bb, t, *cts, dgb_inter)


def _gdn_inter_fwd(wd, uv, qk, qd, kd, gb, per_step=4):
    s, w = wd.shape
    nh, nc, cpb, n, wide, narrow = _gdn_blocks(s, w, per_step)
    hd = GDN_HEAD_DIM

    def body(wd_ref, uv_ref, qk_ref, qd_ref, kd_ref, g_ref, o_ref, st_ref, state):
        @pl.when(pl.program_id(0) == 0)
        def _():
            state[...] = jnp.zeros_like(state)

        wide, narrow = _gdn_pairs(cpb, nh)
        st = [state[h] for h in range(nh)]
        for cb in range(cpb):
            wi, na = wide[cb * nh:(cb + 1) * nh], narrow[cb * nh:(cb + 1) * nh]
            for h in range(nh):
                st_ref[cb, h] = st[h]
            o, st = _gdn_inter([wd_ref[ix] for ix in wi], [uv_ref[ix] for ix in wi], [qk_ref[ix] for ix in na],
                               [qd_ref[ix] for ix in wi], [kd_ref[ix] for ix in wi], [g_ref[ix] for ix in wi], st)
            for h in range(nh):
                o_ref[wi[h]] = o[h]
        for h in range(nh):
            state[h] = st[h]

    bw = pl.BlockSpec(wide, lambda i: (i, 0))
    bn = pl.BlockSpec(narrow, lambda i: (i, 0))
    return pl.pallas_call(
        body, name="gdn_inter", grid=(n,), in_specs=[bw, bw, bn, bw, bw, bw],
        out_specs=[bw, pl.BlockSpec((cpb, nh, hd, hd), lambda i: (i, 0, 0, 0))],
        out_shape=[jax.ShapeDtypeStruct((s, w), F32), jax.ShapeDtypeStruct((nc, nh, hd, hd), F32)],
        scratch_shapes=[pltpu.VMEM((nh, hd, hd), F32)],
        compiler_params=_cparams(("arbitrary",)))(wd, uv, qk, qd, kd, gb)


def _gdn_inter_bwd(wd, uv, qk, qd, kd, gb, states, do, per_step=4):
    s, w = wd.shape
    nh, nc, cpb, n, wide, narrow = _gdn_blocks(s, w, per_step)
    hd = GDN_HEAD_DIM

    def body(wd_ref, uv_ref, qk_ref, qd_ref, kd_ref, g_ref, st_ref, do_ref, cwd, cuv, cqk, cqd, ckd, dg_ref, dstate):
        @pl.when(pl.program_id(0) == 0)
        def _():
            dstate[...] = jnp.zeros_like(dstate)

        wide, narrow = _gdn_pairs(cpb, nh)
        dst = [dstate[h] for h in range(nh)]
        for cb in reversed(range(cpb)):
            wi, na = wide[cb * nh:(cb + 1) * nh], narrow[cb * nh:(cb + 1) * nh]
            f32 = lambda ref, where: [ref[ix].astype(F32) for ix in where]
            _, vjp = jax.vjp(_gdn_inter, f32(wd_ref, wi), f32(uv_ref, wi), f32(qk_ref, na), f32(qd_ref, wi),
                             f32(kd_ref, wi), [g_ref[ix] for ix in wi], [st_ref[cb, h] for h in range(nh)])
            dwd, duv, dqk, dqd, dkd, dg, dst = vjp(([do_ref[ix] for ix in wi], dst))
            for h in range(nh):
                cwd[wi[h]], cuv[wi[h]], cqk[na[h]], cqd[wi[h]], ckd[wi[h]], dg_ref[wi[h]] = (
                    dwd[h], duv[h], dqk[h], dqd[h], dkd[h], dg[h])
        for h in range(nh):
            dstate[h] = dst[h]

    bw = pl.BlockSpec(wide, lambda i: (n - 1 - i, 0))
    bn = pl.BlockSpec(narrow, lambda i: (n - 1 - i, 0))
    fw = jax.ShapeDtypeStruct((s, w), F32)
    return pl.pallas_call(
        body, name="gdn_inter_bwd", grid=(n,),
        in_specs=[bw, bw, bn, bw, bw, bw, pl.BlockSpec((cpb, nh, hd, hd), lambda i: (n - 1 - i, 0, 0, 0)), bw],
        out_specs=[bw, bw, bn, bw, bw, bw],
        out_shape=[fw, fw, jax.ShapeDtypeStruct((s, nh * CHUNK), F32), fw, fw, fw],
        scratch_shapes=[pltpu.VMEM((nh, hd, hd), F32)],
        compiler_params=_cparams(("arbitrary",)))(wd, uv, qk, qd, kd, gb, states, do)


def _s5_coef(ar, ai):
    nl = ar.shape[1]

    def body(ar_ref, ai_ref, o_ref):
        row = lax.broadcasted_iota(jnp.int32, (8, nl), 0)
        for base, sign in ((0, 1.0), (8, -1.0)):
            pr = [jnp.broadcast_to(ar_ref[...], (8, nl))]
            pi = [jnp.broadcast_to(ai_ref[...], (8, nl)) * sign]
            for _ in range(7):
                pr.append(pr[-1] * pr[0] - pi[-1] * pi[0])
                pi.append(pr[-2] * pi[0] + pi[-1] * pr[0])
            for j, d in enumerate((1, 2, 4)):
                m = (row >= d) if base == 0 else (row <= 7 - d)
                o_ref[base + 2 * j] = jnp.where(m, pr[d - 1], 0.0)
                o_ref[base + 2 * j + 1] = jnp.where(m, pi[d - 1], 0.0)
            cr, ci = jnp.zeros((8, nl), F32), jnp.zeros((8, nl), F32)
            for t in range(8):
                e = t if base == 0 else 7 - t
                cr = jnp.where(row == t, pr[e], cr)
                ci = jnp.where(row == t, pi[e], ci)
            o_ref[base + 6] = cr
            o_ref[base + 7] = ci

    return pl.pallas_call(body, name="s5_coef", out_shape=jax.ShapeDtypeStruct((16, 8, nl), F32),
                          compiler_params=_cparams())(ar, ai)


def _scan_tile(src_re, src_im, dst_re, dst_im, coef_ref, carry_re, carry_im, ts, reverse, extra=None):
    nl = src_re.shape[1]
    base = 8 if reverse else 0
    ng = ts // 8
    for lc in range(nl // SCAN_LANES):
        ln = slice(lc * SCAN_LANES, (lc + 1) * SCAN_LANES)
        m = [coef_ref[base + j, :, ln] for j in range(8)]
        row = lax.broadcasted_iota(jnp.int32, (8, SCAN_LANES), 0)

        def step(r, carry, ln=ln, m=m, row=row):
            grp = (ng - 1 - r) if reverse else r
            rows = pl.ds(pl.multiple_of(grp * 8, 8), 8)
            xr, xi = src_re[rows, ln], src_im[rows, ln]
            for j, d in enumerate((1, 2, 4)):
                sh = 8 - d if reverse else d
                sr, si = pltpu.roll(xr, sh, 0), pltpu.roll(xi, sh, 0)
                mr, mi = m[2 * j], m[2 * j + 1]
                xr, xi = xr + mr * sr - mi * si, xi + mr * si + mi * sr
            cr, ci = carry[0], carry[1]
            hr = xr + m[6] * cr - m[7] * ci
            hi = xi + m[6] * ci + m[7] * cr
            dst_re[rows, ln] = hr
            dst_im[rows, ln] = hi
            edge = 0 if reverse else 7
            out = (jnp.broadcast_to(hr[edge:edge + 1, :], hr.shape), jnp.broadcast_to(hi[edge:edge + 1, :], hi.shape))
            if extra is not None:
                h_re, h_im, halo_re, halo_im, first, _, _ = extra
                prev = pl.ds(pl.multiple_of(jnp.maximum(grp - 1, 0) * 8, 8), 8)
                use_halo = grp == 0
                pr = jnp.where(use_halo, halo_re[:, ln] * first, h_re[prev, ln])
                pi = jnp.where(use_halo, halo_im[:, ln] * first, h_im[prev, ln])
                qr = jnp.where(row == 0, jnp.broadcast_to(pr[7:8, :], pr.shape), pltpu.roll(h_re[rows, ln], 1, 0))
                qi = jnp.where(row == 0, jnp.broadcast_to(pi[7:8, :], pi.shape), pltpu.roll(h_im[rows, ln], 1, 0))
                out = out + (carry[2] + hr * qr + hi * qi, carry[3] + hi * qr - hr * qi)
            return out

        init = (carry_re[:, ln], carry_im[:, ln])
        if extra is not None:
            init = init + (extra[5][:, ln], extra[6][:, ln])
        fin = lax.fori_loop(0, ng, step, init)
        carry_re[:, ln] = fin[0]
        carry_im[:, ln] = fin[1]
        if extra is not None:
            extra[5][:, ln] = fin[2]
            extra[6][:, ln] = fin[3]


def _s5_fwd(xb, bb_re, bb_im, c_re, c_im, coef, ts):
    s, w = xb.shape
    nb = bb_re.shape[0]
    nl = nb * 512

    def body(x_ref, bre_ref, bim_ref, cre_ref, cim_ref, coef_ref, hre_ref, him_ref, y_ref, ure, uim, car_re, car_im):
        @pl.when(pl.program_id(0) == 0)
        def _():
            car_re[...] = jnp.zeros_like(car_re)
            car_im[...] = jnp.zeros_like(car_im)

        for b in range(nb):
            xs = x_ref[:, b * 128:(b + 1) * 128].astype(BF16)
            ure[:, b * 512:(b + 1) * 512] = jnp.dot(xs, bre_ref[b], preferred_element_type=F32)
            uim[:, b * 512:(b + 1) * 512] = jnp.dot(xs, bim_ref[b], preferred_element_type=F32)
        _scan_tile(ure, uim, hre_ref, him_ref, coef_ref, car_re, car_im, ts, False)
        for b in range(nb):
            hr = hre_ref[:, b * 512:(b + 1) * 512].astype(BF16)
            hi = him_ref[:, b * 512:(b + 1) * 512].astype(BF16)
            y_ref[:, b * 128:(b + 1) * 128] = (jnp.dot(hr, cre_ref[b], preferred_element_type=F32)
                                               - jnp.dot(hi, cim_ref[b], preferred_element_type=F32))

    row = lambda wd: pl.BlockSpec((ts, wd), lambda i: (i, 0))
    return pl.pallas_call(
        body, name="s5_fwd", grid=(s // ts,),
        in_specs=[row(w), _whole(bb_re), _whole(bb_im), _whole(c_re), _whole(c_im), _whole(coef)],
        out_specs=[row(nl), row(nl), row(w)],
        out_shape=[jax.ShapeDtypeStruct((s, nl), F32), jax.ShapeDtypeStruct((s, nl), F32),
                   jax.ShapeDtypeStruct((s, w), F32)],
        scratch_shapes=[pltpu.VMEM((ts, nl), F32), pltpu.VMEM((ts, nl), F32), pltpu.VMEM((8, nl), F32),
                        pltpu.VMEM((8, nl), F32)],
        compiler_params=_cparams(("arbitrary",)))(xb, bb_re, bb_im, c_re, c_im, coef)


def _s5_bwd(dy, xb, h_re, h_im, bb_re, bb_im, c_re, c_im, coef, ts):
    s, w = xb.shape
    nb = bb_re.shape[0]
    nl = nb * 512
    nt = s // ts

    def body(dy_ref, x_ref, hre_ref, him_ref, halo_re, halo_im, bre_ref, bim_ref, cre_ref, cim_ref, coef_ref,
             dx_ref, dbre_ref, dbim_ref, dcre_ref, dcim_ref, dare_ref, daim_ref, gre, gim, car_re, car_im):
        i = pl.program_id(0)

        @pl.when(i == 0)
        def _():
            for r in (car_re, car_im, dbre_ref, dbim_ref, dcre_ref, dcim_ref, dare_ref, daim_ref):
                r[...] = jnp.zeros_like(r)

        for b in range(nb):
            dyb = dy_ref[:, b * 128:(b + 1) * 128].astype(BF16)
            gre[:, b * 512:(b + 1) * 512] = lax.dot_general(dyb, cre_ref[b], (((1,), (1,)), ((), ())),
                                                            preferred_element_type=F32)
            gim[:, b * 512:(b + 1) * 512] = -lax.dot_general(dyb, cim_ref[b], (((1,), (1,)), ((), ())),
                                                             preferred_element_type=F32)
            hr = hre_ref[:, b * 512:(b + 1) * 512].astype(BF16)
            hi = him_ref[:, b * 512:(b + 1) * 512].astype(BF16)
            dcre_ref[b] += lax.dot_general(hr, dyb, (((0,), (0,)), ((), ())), preferred_element_type=F32)
            dcim_ref[b] -= lax.dot_general(hi, dyb, (((0,), (0,)), ((), ())), preferred_element_type=F32)
        first = (i != nt - 1).astype(F32)
        _scan_tile(gre, gim, gre, gim, coef_ref, car_re, car_im, ts, True,
                   extra=(hre_ref, him_ref, halo_re, halo_im, first, dare_ref, daim_ref))
        for b in range(nb):
            gr = gre[:, b * 512:(b + 1) * 512].astype(BF16)
            gi = gim[:, b * 512:(b + 1) * 512].astype(BF16)
            xs = x_ref[:, b * 128:(b + 1) * 128].astype(BF16)
            dx_ref[:, b * 128:(b + 1) * 128] = (
                lax.dot_general(gr, bre_ref[b], (((1,), (1,)), ((), ())), preferred_element_type=F32)
                + lax.dot_general(gi, bim_ref[b], (((1,), (1,)), ((), ())), preferred_element_type=F32))
            dbre_ref[b] += lax.dot_general(xs, gr, (((0,), (0,)), ((), ())), preferred_element_type=F32)
            dbim_ref[b] += lax.dot_general(xs, gi, (((0,), (0,)), ((), ())), preferred_element_type=F32)

    row = lambda wd: pl.BlockSpec((ts, wd), lambda i: (nt - 1 - i, 0))
    halo = pl.BlockSpec((8, nl), lambda i: (jnp.maximum((nt - 1 - i) * (ts // 8) - 1, 0), 0))
    return pl.pallas_call(
        body, name="s5_bwd", grid=(nt,),
        in_specs=[row(w), row(w), row(nl), row(nl), halo, halo, _whole(bb_re), _whole(bb_im), _whole(c_re),
                  _whole(c_im), _whole(coef)],
        out_specs=[row(w), _whole(bb_re), _whole(bb_im), _whole(c_re), _whole(c_im),
                   pl.BlockSpec((8, nl), lambda i: (0, 0)), pl.BlockSpec((8, nl), lambda i: (0, 0))],
        out_shape=[jax.ShapeDtypeStruct((s, w), F32), jax.ShapeDtypeStruct(bb_re.shape, F32),
                   jax.ShapeDtypeStruct(bb_im.shape, F32), jax.ShapeDtypeStruct(c_re.shape, F32),
                   jax.ShapeDtypeStruct(c_im.shape, F32), jax.ShapeDtypeStruct((8, nl), F32),
                   jax.ShapeDtypeStruct((8, nl), F32)],
        scratch_shapes=[pltpu.VMEM((ts, nl), F32), pltpu.VMEM((ts, nl), F32), pltpu.VMEM((8, nl), F32),
                        pltpu.VMEM((8, nl), F32)],
        compiler_params=_cparams(("arbitrary",)))(dy, xb, h_re, h_im, h_re, h_im, bb_re, bb_im, c_re, c_im, coef)


def _final(x, mo, target, fg, ts):
    s, d = x.shape

    def f(x, mo, fg, tgt):
        y = _rms(x + mo, fg)[0]
        err = y - tgt
        return 0.5 * jnp.sum(jnp.mean(err * err, axis=-1, keepdims=True), axis=0, keepdims=True)

    def body(x_ref, mo_ref, t_ref, fg_ref, dh_ref, dfg_ref, loss_ref):
        @pl.when(pl.program_id(0) == 0)
        def _():
            dfg_ref[...] = jnp.zeros_like(dfg_ref)
            loss_ref[...] = jnp.zeros_like(loss_ref)

        loss, vjp = jax.vjp(f, x_ref[...], mo_ref[...], fg_ref[...], t_ref[...])
        _, dmo, dfg, _ = vjp(jnp.ones((1, 1), F32))
        dh_ref[...] = dmo
        dfg_ref[...] += dfg
        loss_ref[...] += jnp.broadcast_to(loss, loss_ref.shape)

    row = pl.BlockSpec((ts, d), lambda i: (i, 0))
    return pl.pallas_call(
        body, name="final", grid=(s // ts,), in_specs=[row, row, row, _whole(fg)],
        out_specs=[row, _whole(fg), pl.BlockSpec((8, 128), lambda i: (0, 0))],
        out_shape=[jax.ShapeDtypeStruct((s, d), F32), jax.ShapeDtypeStruct(fg.shape, F32),
                   jax.ShapeDtypeStruct((8, 128), F32)],
        compiler_params=_cparams(("arbitrary",)))(x, mo, target, fg)


def _other_chips(x, y):
    return [((1 - x, y), 2 * (1 - x) + y), ((x, 1 - y), 2 * x + 1 - y), ((1 - x, 1 - y), 2 * (1 - x) + 1 - y)]


def _comm_call(body, name, srcs, out_shapes, n_sems):
    n = len(srcs)
    return pl.pallas_call(
        body, name=name, in_specs=[pl.BlockSpec(memory_space=pl.ANY)] * n,
        out_specs=[pl.BlockSpec(memory_space=pl.ANY)] * n, out_shape=out_shapes,
        scratch_shapes=[pltpu.SemaphoreType.DMA((n, n_sems)), pltpu.SemaphoreType.DMA((n, n_sems)),
                        pltpu.SemaphoreType.DMA((n,))],
        compiler_params=pltpu.CompilerParams(has_side_effects=True))(*srcs)


def _gather(srcs, name):
    n = len(srcs)

    def body(*refs):
        src, out = refs[:n], refs[n:2 * n]
        send_sems, recv_sems, local_sems = refs[2 * n:]
        x, y, c = lax.axis_index("x"), lax.axis_index("y"), lax.axis_index("c")
        me, sib_slot, sib = 4 * x + 2 * y + c, 4 * x + 2 * y + 1 - c, (x, y, 1 - c)
        chips = _other_chips(x, y)

        def cp(a, k, src_ref, slot, to):
            return pltpu.make_async_remote_copy(
                src_ref=src_ref, dst_ref=out[a].at[slot], send_sem=send_sems.at[a, k], recv_sem=recv_sems.at[a, k],
                device_id=to, device_id_type=pl.DeviceIdType.MESH)

        local = [pltpu.make_async_copy(src[a], out[a].at[me], local_sems.at[a]) for a in range(n)]
        first = [cp(a, 0, src[a], me, sib) for a in range(n)]
        first += [cp(a, 1 + j, src[a], me, (*chip, c)) for j, (chip, _) in enumerate(chips) for a in range(n)]
        for d in local + first:
            d.start()
        passed = []
        for j, (chip, q) in enumerate(chips):
            for a in range(n):
                cp(a, 1 + j, src[a], 2 * q + c, sib).wait_recv()
                fwd = cp(a, 4 + j, out[a].at[2 * q + c], 2 * q + c, sib)
                fwd.start()
                passed.append(fwd)
        for a in range(n):
            cp(a, 0, src[a], sib_slot, sib).wait_recv()
        for j, (chip, q) in enumerate(chips):
            for a in range(n):
                cp(a, 4 + j, src[a], 2 * q + 1 - c, sib).wait_recv()
        for d in first + passed:
            d.wait_send()
        for d in local:
            d.wait()

    return _comm_call(body, name, srcs, [jax.ShapeDtypeStruct((N_DEV,) + s.shape, s.dtype) for s in srcs], 7)


def _all_peers(x, y, c):
    out = []
    for k in range(1, N_DEV):
        px = 1 - x if k & 4 else x
        py = 1 - y if k & 2 else y
        pc = 1 - c if k & 1 else c
        out.append(((px, py, pc), 4 * px + 2 * py + pc))
    return out


_HBM = pl.BlockSpec(memory_space=pltpu.HBM)
_SEM = pl.BlockSpec(memory_space=pltpu.SEMAPHORE)
_DATAFLOW = pltpu.SideEffectType.DATAFLOW_SIDE_EFFECTING


def _send_whole(ref, slot):
    return ref


def _send_slot(ref, slot):
    return ref.at[slot]


def _direct_copies(src, land, send_sems, recv_sems, picks, arriving):
    x, y, c = lax.axis_index("x"), lax.axis_index("y"), lax.axis_index("c")
    me = 4 * x + 2 * y + c
    out = []
    for k, (pos, slot) in enumerate(_all_peers(x, y, c)):
        for a in range(len(src)):
            sem = a * (N_DEV - 1) + k
            out.append(pltpu.make_async_remote_copy(
                src_ref=picks[a](src[a], slot), dst_ref=land[a].at[slot if arriving else me],
                send_sem=send_sems.at[sem], recv_sem=recv_sems.at[sem], device_id=pos,
                device_id_type=pl.DeviceIdType.MESH))
    return out


def _direct_start(srcs, lands, picks, name):
    n = len(srcs)

    def body(*refs):
        src, land = refs[:n], refs[n:2 * n]
        send_sems, recv_sems = refs[2 * n], refs[2 * n + 1]
        for push in _direct_copies(src, land, send_sems, recv_sems, picks, False):
            push.start()
        refs[-1][...] = jnp.zeros_like(refs[-1])

    arrays = [pltpu.with_memory_space_constraint(t, pltpu.HBM) for t in list(srcs) + list(lands)]
    outs = pl.pallas_call(
        body, name=name, in_specs=[_HBM] * (2 * n),
        out_specs=(_SEM, _SEM, *[_HBM] * (2 * n), pl.BlockSpec(memory_space=pltpu.VMEM)),
        out_shape=(pltpu.SemaphoreType.DMA((n * (N_DEV - 1),)), pltpu.SemaphoreType.DMA((n * (N_DEV - 1),)),
                   *[pltpu.HBM(t.shape, t.dtype) for t in arrays], jax.ShapeDtypeStruct((8, 128), F32)),
        input_output_aliases={i: 2 + i for i in range(2 * n)},
        compiler_params=pltpu.CompilerParams(has_side_effects=_DATAFLOW))(*arrays)
    return outs[:-1], outs[-1]


def _direct_wait(started, picks, after, name):
    send_sems, recv_sems, *thru = started
    n = len(thru) // 2

    def body(*refs):
        src, land = refs[:n], refs[n:2 * n]
        for arrive in _direct_copies(src, land, refs[2 * n], refs[2 * n + 1], picks, True):
            arrive.wait_send()
            arrive.wait_recv()

    outs = pl.pallas_call(
        body, name=name, in_specs=[_HBM] * (2 * n) + [_SEM, _SEM, pl.BlockSpec(memory_space=pl.ANY)],
        out_specs=[_HBM] * (2 * n), out_shape=[pltpu.HBM(t.shape, t.dtype) for t in thru],
        input_output_aliases={i: i for i in range(2 * n)},
        compiler_params=pltpu.CompilerParams(has_side_effects=_DATAFLOW))(*thru, send_sems, recv_sems, after)
    return outs[:n], outs[n:]


def _pick_rows(r, pref):
    t = (min(pref, r) // 8) * 8
    while t >= 8:
        if r % t == 0:
            return t
        t -= 8
    return r


def _w_in_from_shards(t, lo, hi):
    n, r, cs = t.shape
    tr = _pick_rows(r, 256)
    wm = n * cs - (hi - lo)

    def body(t_ref, m_ref, b_ref):
        full = jnp.concatenate([t_ref[j] for j in range(n)], axis=1)
        m_ref[...] = jnp.concatenate([full[:, :lo], full[:, hi:]], axis=1)
        b_ref[...] = jnp.concatenate([full[:, lo:hi], jnp.zeros((tr, 128 - (hi - lo)), full.dtype)], axis=1)

    return pl.pallas_call(
        body, name="w_in_layout", grid=(r // tr,), in_specs=[pl.BlockSpec((n, tr, cs), lambda i: (0, i, 0))],
        out_specs=[pl.BlockSpec((tr, wm), lambda i: (i, 0)), pl.BlockSpec((tr, 128), lambda i: (i, 0))],
        out_shape=[jax.ShapeDtypeStruct((r, wm), t.dtype), jax.ShapeDtypeStruct((r, 128), t.dtype)],
        compiler_params=_cparams(("parallel",)))(t)


def _w_in_to_shards(gm, gb, lo, hi, dtype, first_row=0, name="dw_in_layout"):
    r, wm = gm.shape
    cs = (wm + hi - lo) // N_DEV
    tr = _pick_rows(r, 64)
    assert first_row % tr == 0
    b0 = first_row // tr

    def body(m_ref, b_ref, o_ref):
        m = m_ref[...]
        full = jnp.concatenate([m[:, :lo], b_ref[:, :hi - lo], m[:, lo:]], axis=1)
        for j in range(N_DEV):
            o_ref[j] = full[:, j * cs:(j + 1) * cs].astype(o_ref.dtype)

    return pl.pallas_call(
        body, name=name, grid=(r // tr,),
        in_specs=[pl.BlockSpec((tr, wm), lambda i: (i, 0)), pl.BlockSpec((tr, 128), lambda i: (i + b0, 0))],
        out_specs=pl.BlockSpec((N_DEV, tr, cs), lambda i: (0, i, 0)),
        out_shape=jax.ShapeDtypeStruct((N_DEV, r, cs), dtype), compiler_params=_cparams(("parallel",)))(gm, gb)


def _pack(arrs, dtype, lead=()):
    nlead = len(lead)
    flat = jnp.concatenate([a.astype(dtype).reshape(lead + (-1,)) for a in arrs], axis=nlead)
    n = flat.shape[-1]
    unit = PACK_WIDTH * PACK_ROWS
    pad = (-n) % unit
    flat = jnp.pad(flat, [(0, 0)] * nlead + [(0, pad)])
    return flat.reshape(lead + ((n + pad) // PACK_WIDTH, PACK_WIDTH))


def _unpack(buf, shapes, lead=()):
    flat = buf.reshape(lead + (-1,))
    out, off = [], 0
    for shp in shapes:
        n = math.prod(shp)
        out.append(flat[..., off:off + n].reshape(lead + tuple(shp)))
        off += n
    return out


def _adam_math(w, g, m, v):
    m = ADAM_B1 * m + (1.0 - ADAM_B1) * g
    v = ADAM_B2 * v + (1.0 - ADAM_B2) * (g * g)
    m_hat = m / (1.0 - ADAM_B1 ** ADAM_STEP)
    v_hat = v / (1.0 - ADAM_B2 ** ADAM_STEP)
    delta = -ADAM_LR * (m_hat / (jnp.sqrt(v_hat) + ADAM_EPS) + ADAM_WD * w)
    return delta, m, v


def _sum_adam(parts, w, m, v, name, own=None, me=None):
    r, c = w.shape
    parts_list = list(parts) if isinstance(parts, (list, tuple)) else [parts]
    own_list = list(own) if isinstance(own, (list, tuple)) else [own]
    nparts = parts_list[0].shape[0]
    lanes = -(-c // 128) * 128
    tr = _pick_rows(r // len(parts_list), max(8, (6 * 1024 * 1024) // (nparts * lanes * 4)))

    def finish(g, w_ref, m_ref, v_ref, g_ref, d_ref, nm_ref, nv_ref):
        d, nm, nv = _adam_math(w_ref[...], g, m_ref[...], v_ref[...])
        g_ref[...] = g
        d_ref[...] = d
        nm_ref[...] = nm
        nv_ref[...] = nv

    out_shape = [jax.ShapeDtypeStruct((r, c), F32)] * 4
    if own is None:
        def body(p_ref, *rest):
            g = p_ref[0].astype(F32)
            for j in range(1, nparts):
                g = g + p_ref[j].astype(F32)
            finish(g, *rest)

        row = pl.BlockSpec((tr, c), lambda i: (i, 0))
        return pl.pallas_call(
            body, name=name, grid=(r // tr,),
            in_specs=[pl.BlockSpec((nparts, tr, c), lambda i: (0, i, 0)), row, row, row],
            out_specs=[row] * 4, out_shape=out_shape, compiler_params=_cparams(("parallel",)))(parts, w, m, v)

    nch = len(parts_list)
    tpc = r // nch // tr

    def body(me_ref, *refs):
        p_refs, o_refs, rest = refs[:nch], refs[nch:2 * nch], refs[2 * nch:]
        i = pl.program_id(0)

        def of_chunk(vals):
            out = vals[0]
            for q in range(1, nch):
                out = jnp.where(i >= q * tpc, vals[q], out)
            return out

        mine = of_chunk([o[...].astype(F32) for o in o_refs])
        g = None
        for j in range(nparts):
            t = jnp.where(me_ref[0] == j, mine, of_chunk([p[j].astype(F32) for p in p_refs]))
            g = t if g is None else g + t
        finish(g, *rest)

    row = pl.BlockSpec((tr, c), lambda i, me_ref: (i, 0))
    step = lambda i, q: jnp.clip(i - q * tpc, 0, tpc - 1)
    return pl.pallas_call(
        body, name=name,
        grid_spec=pltpu.PrefetchScalarGridSpec(
            num_scalar_prefetch=1, grid=(r // tr,),
            in_specs=[pl.BlockSpec((nparts, tr, c), lambda i, me_ref, q=q: (0, step(i, q), 0)) for q in range(nch)]
            + [pl.BlockSpec((None, tr, c), lambda i, me_ref, q=q: (me_ref[0], step(i, q), 0)) for q in range(nch)]
            + [row, row, row],
            out_specs=[row] * 4),
        out_shape=out_shape, compiler_params=_cparams(("parallel",)))(me, *parts_list, *own_list, w, m, v)


def _block_diag(t):
    nb, g, a, b = t.shape
    eye = jnp.eye(g, dtype=t.dtype)
    return jnp.einsum('ngab,gh->ngahb', t, eye).reshape(nb, g * a, g * b)


def _diag_blocks(t, a, b):
    nb = t.shape[0]
    g = S5_GROUPS_PER_BLOCK
    t = t.reshape(nb, g, a, g, b)
    return jnp.stack([t[:, j, :, j, :] for j in range(g)], axis=1)


def _local_step(x, mem, target, p, late_weights=None, early_grads=None, last_grads=None):
    s, d = x.shape
    gw = d // 2
    nh = gw // GDN_HEAD_DIM
    ng = gw // S5_GROUP
    nb = ng // S5_GROUPS_PER_BLOCK
    nl = ng * S5_STATE
    ts = min(256, s)
    nt = s // ts
    grads = {}

    w_main, w_ba = p['w_main'], p['w_ba']
    CB_ZA, CB_XB, CB_ZB, CB_QC, CB_ZC, CB_G = 3, 4, 5, 6, 7, 8

    u = _tile_fwd(_rms, "rms_fwd", nt, [_rt(x, ts)], [p['norm_g']],
                  [((s, d), BF16, (ts, d), lambda i: (i, 0))])[0]
    proj = _mm(u, w_main, tm=1024, tn=2048, tk=512, out_dtype=BF16, name="proj_main")
    pba = _mm(u, w_ba, name="proj_ba")
    if late_weights is not None:
        p = {**p, **late_weights(proj)}

    conv_w = p['conv_w']
    col = lambda arr, cb: (arr, (s, GDN_HEAD_DIM), lambda i, cb=cb: (0, cb + i))
    qkv = []
    for j, mode in enumerate(('q', 'k', 'v')):
        off = j * nh
        qkv.append(_tile_fwd(
            _gdn_pre(mode), "gdn_pre_" + mode, nh, [col(proj, off), (conv_w, (CONV_WIDTH, GDN_HEAD_DIM), lambda i, off=off: (0, off + i))],
            [], [((s, gw), F32, (s, GDN_HEAD_DIM), lambda i: (0, i))])[0])
    q, k, v = qkv
    lane = jnp.arange(128)[:, None]
    colh = jnp.arange(gw)[None, :] // GDN_HEAD_DIM
    e_beta = (lane == colh).astype(F32)
    e_g = (lane == colh + nh).astype(F32)
    alog_row = jnp.pad(p['gdn_a_log'], ((0, 0), (nh, 128 - 2 * nh)))
    dtb_row = jnp.pad(p['gdn_dt_bias'], ((0, 0), (nh, 128 - 2 * nh)))
    row_gw = lambda: ((s, gw), F32, (ts, gw), lambda i: (i, 0))
    betab, gb = _tile_fwd(_gdn_gates, "gdn_gates", nt, [_rt(pba, ts)], [alog_row, dtb_row, e_beta, e_g],
                          [row_gw(), row_gw()])
    *intra, t_inv = _gdn_intra_fwd(q, k, v, gb, betab)
    o_raw, states = _gdn_inter_fwd(*intra, gb)
    ga = _tile_fwd(_gdn_post, "gdn_post", nt, [_rt(o_raw, ts), _rt(proj, ts, CB_ZA, gw)], [p['gdn_norm_g']],
                   [((s, gw), BF16, (ts, gw), lambda i: (i, 0))])[0]

    e_rep = (jnp.arange(S5_STATE)[:, None] == jnp.arange(S5_STATE * S5_GROUP)[None, :] // S5_GROUP).astype(F32)
    s5_in = [p['s5_lambda_re'], p['s5_lambda_im'], p['s5_log_dt'].reshape(ng, 1),
             p['s5_b_re'].reshape(ng, S5_STATE * S5_GROUP), p['s5_b_im'].reshape(ng, S5_STATE * S5_GROUP), e_rep]
    one = lambda shp: (shp, F32, shp, lambda i, n=len(shp): (0,) * n)
    ab_re, ab_im, bbr, bbi = _tile_fwd(_s5_params, "s5_params", 1, [], s5_in,
                                       [one((ng, S5_STATE)), one((ng, S5_STATE)), one((ng, S5_STATE * S5_GROUP)),
                                        one((ng, S5_STATE * S5_GROUP))])
    coef = _s5_coef(ab_re.reshape(1, nl), ab_im.reshape(1, nl))
    to_bd_b = lambda t: _block_diag(t.reshape(nb, S5_GROUPS_PER_BLOCK, S5_STATE, S5_GROUP).transpose(0, 1, 3, 2))
    to_bd_c = lambda t: _block_diag(t.reshape(nb, S5_GROUPS_PER_BLOCK, S5_GROUP, S5_STATE).transpose(0, 1, 3, 2))
    bbd_re, bbd_im = to_bd_b(bbr).astype(BF16), to_bd_b(bbi).astype(BF16)
    cbd_re, cbd_im = to_bd_c(p['s5_c_re']).astype(BF16), to_bd_c(p['s5_c_im']).astype(BF16)
    xb_arr = lax.slice_in_dim(proj, CB_XB * gw, (CB_XB + 1) * gw, axis=1)
    h_re, h_im, ylin = _s5_fwd(xb_arr, bbd_re, bbd_im, cbd_re, cbd_im, coef, ts)
    gl = _tile_fwd(_s5_post1, "s5_post1", nt, [_rt(ylin, ts), _rt(proj, ts, CB_XB, gw)], [p['s5_d']],
                   [((s, gw), BF16, (ts, gw), lambda i: (i, 0))])[0]
    tglu = _mm(gl, p['s5_w_glu'], b_shards=True, name="s5_glu")
    gbb = _tile_fwd(_s5_post2, "s5_post2", nt, [_rt(tglu, ts), _rt(proj, ts, CB_ZB, gw)], [],
                    [((s, gw), BF16, (ts, gw), lambda i: (i, 0))])[0]

    m_len = mem.shape[0]
    mem_n = _tile_fwd(_rms, "mem_rms", 1, [_rt(mem, m_len)], [p['mem_norm_g']],
                      [((m_len, d), BF16, (m_len, d), lambda i: (i, 0))])[0]
    kv = _mm(mem_n, p['w_kv_mem'], name="mem_kv")
    gcc = _tile_fwd(_attn, "attn", nt, [_rt(proj, ts, CB_QC, gw), _rt(proj, ts, CB_ZC, gw)], [kv],
                    [((s, gw), BF16, (ts, gw), lambda i: (i, 0))])[0]

    p_a = _mm(ga, p['w_br_a'], b_shards=True, out_dtype=BF16, name="br_a")
    p_b = _mm(gbb, p['w_br_b'], b_shards=True, out_dtype=BF16, name="br_b")
    p_c = _mm(gcc, p['w_br_c'], b_shards=True, out_dtype=BF16, name="br_c")
    gate_acts = [_rt(proj, ts, CB_G // 2 + j, d) for j in range(3)]
    merged = _tile_fwd(_merge, "merge", nt, gate_acts + [_rt(p_a, ts), _rt(p_b, ts), _rt(p_c, ts)], [],
                       [((s, d), BF16, (ts, d), lambda i: (i, 0))])[0]
    mo = _mm(merged, p['w_out'], name="out_proj")
    dh, dfg, loss = _final(x, mo, target, p['final_g'].reshape(1, d), ts)
    grads['final_g'] = dfg.reshape(d)

    dmerged = _mm(dh, p['w_out'], tb=True, name="d_merged")
    grads['w_out'] = _mm(merged, dh, ta=True, name="dw_out")
    row_d = lambda dt: ((s, d), dt, (ts, d), lambda i: (i, 0))
    dg0, dg1, dg2, dpa, dpb, dpc = _tile_bwd(
        _merge, "merge_bwd", nt, gate_acts + [_rt(p_a, ts), _rt(p_b, ts), _rt(p_c, ts)], [], [_rt(dmerged, ts)],
        [row_d(BF16)] * 6, [])
    dga = _mm(dpa, p['w_br_a'], tb=True, b_shards=True, name="d_ga")
    dgbb = _mm(dpb, p['w_br_b'], tb=True, b_shards=True, name="d_gb")
    dgcc = _mm(dpc, p['w_br_c'], tb=True, b_shards=True, name="d_gc")
    grads['w_br_a'] = _mm(ga, dpa, ta=True, out_shards=N_DEV, name="dw_br_a")
    grads['w_br_b'] = _mm(gbb, dpb, ta=True, out_shards=N_DEV, name="dw_br_b")
    grads['w_br_c'] = _mm(gcc, dpc, ta=True, out_shards=N_DEV, name="dw_br_c")
    row_h = lambda dt: ((s, gw), dt, (ts, gw), lambda i: (i, 0))

    dqc, dzc, dkv = _tile_bwd(_attn, "attn_bwd", nt, [_rt(proj, ts, CB_QC, gw), _rt(proj, ts, CB_ZC, gw)], [kv],
                              [_rt(dgcc, ts)], [row_h(BF16), row_h(BF16)], [True])
    grads['w_kv_mem'] = _mm(mem_n, dkv, ta=True, name="dw_kv")
    dmem_n = _mm(dkv, p['w_kv_mem'], tb=True, name="d_mem_n")
    grads['mem_norm_g'] = _tile_bwd(_rms, "mem_rms_bwd", 1, [_rt(mem, m_len)], [p['mem_norm_g']],
                                    [_rt(dmem_n, m_len)], [None], [True])[0]

    dtglu, dzb = _tile_bwd(_s5_post2, "s5_post2_bwd", nt, [_rt(tglu, ts), _rt(proj, ts, CB_ZB, gw)], [],
                           [_rt(dgbb, ts)], [((s, 2 * gw), BF16, (ts, 2 * gw), lambda i: (i, 0)), row_h(BF16)], [])
    grads['s5_w_glu'] = _mm(gl, dtglu, ta=True, out_shards=N_DEV, name="dw_glu")
    s5_d = p['s5_d']
    if early_grads is not None:
        s5_d = s5_d + early_grads(grads)[:1, :1]
    dgl = _mm(dtglu, p['s5_w_glu'], tb=True, b_shards=True, name="d_gl")
    dylin, dxb1, dd = _tile_bwd(_s5_post1, "s5_post1_bwd", nt, [_rt(ylin, ts), _rt(proj, ts, CB_XB, gw)],
                                [s5_d], [_rt(dgl, ts)], [row_h(F32), row_h(F32)], [True])
    grads['s5_d'] = dd
    dxb2, dbbd_re, dbbd_im, dcbd_re, dcbd_im, da_re, da_im = _s5_bwd(dylin, xb_arr, h_re, h_im, bbd_re, bbd_im,
                                                                    cbd_re, cbd_im, coef, ts)
    from_bd_b = lambda t: _diag_blocks(t, S5_GROUP, S5_STATE).transpose(0, 1, 3, 2).reshape(ng, S5_STATE * S5_GROUP)
    from_bd_c = lambda t: _diag_blocks(t, S5_STATE, S5_GROUP).transpose(0, 1, 3, 2).reshape(1, ng, S5_GROUP, S5_STATE)
    grads['s5_c_re'], grads['s5_c_im'] = from_bd_c(dcbd_re), from_bd_c(dcbd_im)
    s5_cts = [jnp.sum(da_re, axis=0).reshape(ng, S5_STATE), jnp.sum(da_im, axis=0).reshape(ng, S5_STATE),
              from_bd_b(dbbd_re), from_bd_b(dbbd_im)]
    dlr, dli, dlogdt, dbr, dbi = _tile_bwd(_s5_params, "s5_params_bwd", 1, [], s5_in,
                                           [(c, c.shape, lambda i: (0, 0)) for c in s5_cts], [],
                                           [True, True, True, True, True, False])
    grads['s5_lambda_re'], grads['s5_lambda_im'] = dlr[None], dli[None]
    grads['s5_log_dt'] = dlogdt.reshape(1, ng)
    grads['s5_b_re'] = dbr.reshape(1, ng, S5_STATE, S5_GROUP)
    grads['s5_b_im'] = dbi.reshape(1, ng, S5_STATE, S5_GROUP)
    dxb = (dxb1 + dxb2).astype(BF16)

    do_raw, dza, dgng = _tile_bwd(_gdn_post, "gdn_post_bwd", nt, [_rt(o_raw, ts), _rt(proj, ts, CB_ZA, gw)],
                                  [p['gdn_norm_g']], [_rt(dga, ts)], [row_h(F32), row_h(BF16)], [True])
    grads['gdn_norm_g'] = dgng
    *intra_cts, dgb_inter = _gdn_inter_bwd(*intra, gb, states, do_raw)
    dq, dk, dv, dgb, dbetab = _gdn_intra_bwd(q, k, v, gb, betab, t_inv, intra_cts, dgb_inter)
    dpba, dalog, ddtb = _tile_bwd(_gdn_gates, "gdn_gates_bwd", nt, [_rt(pba, ts)], [alog_row, dtb_row, e_beta, e_g],
                                  [_rt(dbetab, ts), _rt(dgb, ts)], [((s, 128), BF16, (ts, 128), lambda i: (i, 0))],
                                  [True, True, False, False])
    grads['gdn_a_log'] = dalog[:, nh:2 * nh]
    grads['gdn_dt_bias'] = ddtb[:, nh:2 * nh]
    dqkv, dconv = [], []
    for j, (mode, ct) in enumerate((('q', dq), ('k', dk), ('v', dv))):
        off = j * nh
        wspec = (conv_w, (CONV_WIDTH, GDN_HEAD_DIM), lambda i, off=off: (0, off + i))
        dxc, dwc = _tile_bwd(
            _gdn_pre(mode), "gdn_pre_bwd_" + mode, nh, [col(proj, off), wspec], [], [col(ct, 0)],
            [((s, gw), BF16, (s, GDN_HEAD_DIM), lambda i: (0, i)),
             ((CONV_WIDTH, gw), F32, (CONV_WIDTH, GDN_HEAD_DIM), lambda i: (0, i))], [])
        dqkv.append(dxc)
        dconv.append(dwc)
    grads['conv_w'] = jnp.concatenate(dconv, axis=1)

    dproj = jnp.concatenate(dqkv + [dza, dxb, dzb, dqc, dzc, dg0, dg1, dg2], axis=1)
    grads['w_ba'] = _mm(u, dpba, ta=True, name="dw_ba")
    if last_grads is None:
        grads['w_main'] = _mm(u, dproj, ta=True, tm=1024, tn=2048, tk=512, name="dw_main")
    else:
        uu = u
        for h in range(2):
            gm = _mm(uu, dproj, ta=True, tm=1024, tn=2048, tk=512, rows=(h * (d // 2), d // 2), name=f"dw_main_{h}")
            tok = last_grads(h, gm, grads)[:1, :1].astype(BF16)
            if h == 0:
                uu = uu + tok
            else:
                dpba = dpba + tok
    du = _mm(dpba, w_ba, tb=True, name="du_ba")
    du = _mm(dproj, w_main, tb=True, addend=du, tm=512, tn=2048, tk=1024, name="du_main")
    grad_x, dng = _tile_bwd(_rms, "rms_bwd", nt, [_rt(x, ts)], [p['norm_g']], [_rt(du, ts)],
                            [row_d(F32) + (dh,)], [True])
    grads['norm_g'] = dng
    return loss, grad_x, grads


def _to_shards(name, g):
    if SHARDED[name] == 'row':
        return g.reshape((N_DEV, g.shape[0] // N_DEV) + g.shape[1:])
    r, c = g.shape
    return g.reshape(r, N_DEV, c // N_DEV).transpose(1, 0, 2)


def _from_shards(name, t):
    if SHARDED[name] == 'row':
        return t.reshape((t.shape[0] * t.shape[1],) + t.shape[2:])
    n, r, c = t.shape
    return t.transpose(1, 0, 2).reshape(r, n * c)


def _step(x, mem, target, w, m, v):
    sharded = list(SHARDED)
    shard_shapes = {n: tuple(w[n].shape[1:]) for n in sharded}
    d = x.shape[-1]
    ba_lo = 2 * d
    ba_hi = ba_lo + 2 * (d // 2 // GDN_HEAD_DIM)

    w_in_all = _gather([w['w_in'][0].astype(BF16)], "gather_w_in")[0]
    after_w_in = (w_in_all[0, :1, :1] * 0).astype(F32)
    late = [w[n][0].astype(BF16) for n in OVERLAPPED] + [w['conv_w'][0] + after_w_in]
    every = [_send_whole] * len(late)
    lands = [jnp.broadcast_to(t[None], (N_DEV,) + t.shape) for t in late]
    gather_started, token = _direct_start(late, lands, every, "gather_rest_start")
    full = {}
    full['w_main'], full['w_ba'] = _w_in_from_shards(w_in_all, ba_lo, ba_hi)
    for n in REPLICATED:
        full[n] = w[n]
    for n in ('s5_lambda_re', 's5_lambda_im', 's5_c_re', 's5_c_im'):
        full[n] = w[n][0]
    full['norm_g'] = w['norm_g'] + token[:1, :1]

    def late_weights(proj):
        got = dict(zip(OVERLAPPED + ['conv_w'], _direct_wait(gather_started, every, proj, "gather_rest_wait")[1]))
        for n in ('w_kv_mem', 'w_out', 'conv_w'):
            got[n] = _from_shards(n, got[n])
        return got

    slots = [_send_slot] * len(OVERLAPPED)
    scatter_started = []

    def early_grads(grads):
        gs = [_to_shards(n, grads[n]) if SHARDED[n] == 'row' else grads[n] for n in OVERLAPPED]
        started, tok = _direct_start(gs, [lax.empty(g.shape, g.dtype) for g in gs], slots, "scatter_early_start")
        scatter_started.append(started)
        return tok

    def last_grads(h, gm, grads):
        gs = [_w_in_to_shards(gm, grads['w_ba'], ba_lo, ba_hi, BF16, h * gm.shape[0], f"dw_in_layout_{h}")]
        if h == 0:
            gs.append(_to_shards('conv_w', grads['conv_w']))
        started, tok = _direct_start(gs, [lax.empty(g.shape, g.dtype) for g in gs], slots[:len(gs)],
                                     f"scatter_last_start_{h}")
        scatter_started.append(started)
        return tok

    loss, grad_x, grads = _local_step(x[0], mem[0], target[0], full, late_weights, early_grads, last_grads)
    res = {}
    me = (4 * lax.axis_index("x") + 2 * lax.axis_index("y") + lax.axis_index("c")).astype(jnp.int32).reshape(1)

    def update(names, exchanged):
        for n, own, part in zip(names, *exchanged):
            outs = _sum_adam(part, w[n][0], m[n][0], v[n][0], name="adam_" + n, own=own, me=me)
            for kind, t in zip(('grad', 'delta', 'new_m', 'new_v'), outs):
                res[kind, n] = t[None]

    update(OVERLAPPED, _direct_wait(scatter_started[0], slots, grad_x, "scatter_early_wait"))

    small = _pack([grads[n].reshape(w[n].shape) for n in REPLICATED] + [loss[:1, :1]], F32)
    allp = _gather([small], "gather_small")[0]
    zero = jnp.zeros((1, 1), F32)
    outs = _sum_adam(allp, *[_pack([t[n] for n in REPLICATED] + [zero], F32) for t in (w, m, v)], name="adam_small")
    shapes = [w[n].shape for n in REPLICATED] + [(1, 1)]
    for kind, buf in zip(('grad', 'delta', 'new_m', 'new_v'), outs):
        got = _unpack(buf, shapes)
        for n, t in zip(REPLICATED, got):
            res[kind, n] = t
        if kind == 'grad':
            total_loss = got[-1].reshape(())
    (own0, own_conv), (got0, got_conv) = _direct_wait(scatter_started[1], slots[:2], outs[0], "scatter_last_wait_0")
    (own1,), (got1,) = _direct_wait(scatter_started[2], slots[:1], outs[0], "scatter_last_wait_1")
    update(['w_in', 'conv_w'], ([[own0, own1], own_conv], [[got0, got1], got_conv]))
    out = [total_loss, grad_x[None]]
    for kind in ('grad', 'delta', 'new_m', 'new_v'):
        out += [res[kind, n] for n in WEIGHTS]
    return tuple(out)


def kernel(x, mem, norm_g, w_in, conv_w, gdn_a_log, gdn_dt_bias, gdn_norm_g, s5_lambda_re, s5_lambda_im, s5_log_dt, s5_b_re, s5_b_im, s5_c_re, s5_c_im, s5_d, s5_w_glu, mem_norm_g, w_kv_mem, w_br_a, w_br_b, w_br_c, w_out, final_g, loss_target, m_norm_g, m_w_in, m_conv_w, m_gdn_a_log, m_gdn_dt_bias, m_gdn_norm_g, m_s5_lambda_re, m_s5_lambda_im, m_s5_log_dt, m_s5_b_re, m_s5_b_im, m_s5_c_re, m_s5_c_im, m_s5_d, m_s5_w_glu, m_mem_norm_g, m_w_kv_mem, m_w_br_a, m_w_br_b, m_w_br_c, m_w_out, m_final_g, v_norm_g, v_w_in, v_conv_w, v_gdn_a_log, v_gdn_dt_bias, v_gdn_norm_g, v_s5_lambda_re, v_s5_lambda_im, v_s5_log_dt, v_s5_b_re, v_s5_b_im, v_s5_c_re, v_s5_c_im, v_s5_d, v_s5_w_glu, v_mem_norm_g, v_w_kv_mem, v_w_br_a, v_w_br_b, v_w_br_c, v_w_out, v_final_g):
    a = dict(locals())
    w = {n: a[n] for n in WEIGHTS}
    m = {n: a['m_' + n] for n in WEIGHTS}
    v = {n: a['v_' + n] for n in WEIGHTS}
    return _step(x, mem, loss_target, w, m, v)
```

```python
import functools
import math

import jax
import jax.numpy as jnp
from jax import lax
from jax.experimental import pallas as pl
from jax.experimental.pallas import tpu as pltpu

F32 = jnp.float32
BF16 = jnp.bfloat16
HI = lax.Precision.HIGHEST

EPS = 1e-6
CHUNK = 64
GDN_HEAD_DIM = 128
CONV_WIDTH = 4
S5_GROUP = 16
S5_STATE = 64
S5_GROUPS_PER_BLOCK = 8
XA_HEADS = 4
N_DEV = 8
ADAM_LR, ADAM_B1, ADAM_B2, ADAM_EPS, ADAM_WD, ADAM_STEP = 0.001, 0.9, 0.999, 1e-08, 0.01, 10

VMEM_LIMIT_BYTES = 56 * 1024 * 1024
SCAN_LANES = 512
PACK_WIDTH = 512
PACK_ROWS = 256

WEIGHTS = ['norm_g', 'w_in', 'conv_w', 'gdn_a_log', 'gdn_dt_bias', 'gdn_norm_g', 's5_lambda_re', 's5_lambda_im',
           's5_log_dt', 's5_b_re', 's5_b_im', 's5_c_re', 's5_c_im', 's5_d', 's5_w_glu', 'mem_norm_g', 'w_kv_mem',
           'w_br_a', 'w_br_b', 'w_br_c', 'w_out', 'final_g']
SHARDED = {'w_in': 'col', 'conv_w': 'col', 's5_w_glu': 'col', 'w_kv_mem': 'row', 'w_br_a': 'col', 'w_br_b': 'col',
           'w_br_c': 'col', 'w_out': 'row'}
GATHER_BF16 = ['w_in', 's5_w_glu', 'w_kv_mem', 'w_br_a', 'w_br_b', 'w_br_c', 'w_out']
REPLICATED = [n for n in WEIGHTS if n not in SHARDED]
OVERLAPPED = ['s5_w_glu', 'w_kv_mem', 'w_br_a', 'w_br_b', 'w_br_c', 'w_out']


def _cparams(sem=None):
    return pltpu.CompilerParams(dimension_semantics=sem, vmem_limit_bytes=VMEM_LIMIT_BYTES)


def _pick(dim, pref):
    t = (min(pref, dim) // 128) * 128
    while t >= 128:
        if dim % t == 0:
            return t
        t -= 128
    return dim


def _make_dots(prep, precision):
    def raw(a, b, dims):
        return lax.dot_general(prep(a), prep(b), (dims, ((), ())), preferred_element_type=F32, precision=precision)

    @jax.custom_vjp
    def nn(a, b):
        return raw(a, b, ((1,), (0,)))

    @jax.custom_vjp
    def nt(a, b):
        return raw(a, b, ((1,), (1,)))

    @jax.custom_vjp
    def tn(a, b):
        return raw(a, b, ((0,), (0,)))

    nn.defvjp(lambda a, b: (nn(a, b), (a, b)), lambda r, ct: (nt(ct, r[1]), tn(r[0], ct)))
    nt.defvjp(lambda a, b: (nt(a, b), (a, b)), lambda r, ct: (nn(ct, r[1]), tn(ct, r[0])))
    tn.defvjp(lambda a, b: (tn(a, b), (a, b)), lambda r, ct: (nt(r[1], ct), nn(r[0], ct)))
    return nn, nt, tn


_bnn, _bnt, _btn = _make_dots(lambda a: a.astype(BF16), None)
_hnn, _hnt, _htn = _make_dots(lambda a: a.astype(F32), HI)
_mnn, _mnt, _mtn = _make_dots(lambda a: a.astype(F32), lax.Precision.HIGH)


def _mm(a, b, *, name, ta=False, tb=False, out_dtype=F32, addend=None, tm=512, tn=1024, tk=1024, b_shards=False,
        out_shards=0, rows=None):
    m, k = (a.shape[1], a.shape[0]) if ta else a.shape
    brows, bcols = (b.shape[1], b.shape[0] * b.shape[2]) if b_shards else b.shape
    n = brows if tb else bcols
    assert (bcols if tb else brows) == k, (a.shape, b.shape, ta, tb)
    first_row = 0
    if rows is not None:
        first_row, m = rows
    tm, tn, tk = _pick(m, tm), _pick(n, tn), _pick(k, tk)
    assert first_row % tm == 0
    r0 = first_row // tm
    bcs = ocs = 0
    if b_shards:
        bcs = b.shape[2]
        assert bcs % 128 == 0 and (tk if tb else tn) % bcs == 0
    if out_shards:
        ocs = n // out_shards
        assert ocs % 128 == 0 and tn % ocs == 0
    nk = k // tk
    dims = ((0 if ta else 1,), (1 if tb else 0,))

    def body(*refs):
        if addend is None:
            a_ref, b_ref, o_ref, acc_ref = refs
        else:
            a_ref, b_ref, add_ref, o_ref, acc_ref = refs
        kk = pl.program_id(2)

        @pl.when(kk == 0)
        def _():
            acc_ref[...] = jnp.zeros_like(acc_ref)

        dot = lambda x, y: lax.dot_general(x.astype(BF16), y.astype(BF16), (dims, ((), ())), preferred_element_type=F32)
        if not b_shards:
            acc_ref[...] += dot(a_ref[...], b_ref[...])
        elif tb:
            for g in range(tk // bcs):
                acc_ref[...] += dot(a_ref[:, g * bcs:(g + 1) * bcs], b_ref[g])
        else:
            for g in range(tn // bcs):
                acc_ref[:, g * bcs:(g + 1) * bcs] += dot(a_ref[...], b_ref[g])

        @pl.when(kk == nk - 1)
        def _():
            r = acc_ref[...]
            if addend is not None:
                r = r + add_ref[...].astype(F32)
            if out_shards:
                for g in range(tn // ocs):
                    o_ref[g] = r[:, g * ocs:(g + 1) * ocs].astype(o_ref.dtype)
            else:
                o_ref[...] = r.astype(o_ref.dtype)

    a_spec = (pl.BlockSpec((tk, tm), lambda i, j, kk: (kk, i + r0)) if ta
              else pl.BlockSpec((tm, tk), lambda i, j, kk: (i + r0, kk)))
    if b_shards:
        b_spec = (pl.BlockSpec((tk // bcs, tn, bcs), lambda i, j, kk: (kk, j, 0)) if tb
                  else pl.BlockSpec((tn // bcs, tk, bcs), lambda i, j, kk: (j, kk, 0)))
    else:
        b_spec = (pl.BlockSpec((tn, tk), lambda i, j, kk: (j, kk)) if tb
                  else pl.BlockSpec((tk, tn), lambda i, j, kk: (kk, j)))
    if out_shards:
        o_spec = pl.BlockSpec((tn // ocs, tm, ocs), lambda i, j, kk: (j, i, 0))
        out_shape = jax.ShapeDtypeStruct((out_shards, m, ocs), out_dtype)
    else:
        o_spec = pl.BlockSpec((tm, tn), lambda i, j, kk: (i, j))
        out_shape = jax.ShapeDtypeStruct((m, n), out_dtype)
    in_specs = [a_spec, b_spec] + ([o_spec] if addend is not None else [])
    args = (a, b) + ((addend,) if addend is not None else ())
    return pl.pallas_call(
        body, name=name, grid=(m // tm, n // tn, nk), in_specs=in_specs, out_specs=o_spec,
        out_shape=out_shape, scratch_shapes=[pltpu.VMEM((tm, tn), F32)],
        compiler_params=_cparams(("parallel", "parallel", "arbitrary")))(*args)


def _rt(arr, ts, cb=0, w=None):
    w = arr.shape[1] if w is None else w
    return (arr, (ts, w), lambda i, cb=cb: (i, cb))


def _whole(p):
    return pl.BlockSpec(p.shape, lambda i, nd=p.ndim: (0,) * nd)


def _tile_fwd(f, name, n, acts, params, outs):
    na, npar = len(acts), len(params)

    def body(*refs):
        res = f(*[r[...] for r in refs[:na + npar]])
        for r, v in zip(refs[na + npar:], res):
            r[...] = v.astype(r.dtype)

    in_specs = [pl.BlockSpec(b, m) for _, b, m in acts] + [_whole(p) for p in params]
    out = pl.pallas_call(
        body, name=name, grid=(n,), in_specs=in_specs,
        out_specs=[pl.BlockSpec(b, m) for _, _, b, m in outs],
        out_shape=[jax.ShapeDtypeStruct(s, d) for s, d, _, _ in outs],
        compiler_params=_cparams(("parallel",)))(*[a for a, _, _ in acts], *params)
    return out


def _tile_bwd(f, name, n, acts, params, cts, agrads, pgrads):
    na, npar, nc = len(acts), len(params), len(cts)
    adds = [g[4] for g in agrads if g is not None and len(g) == 5]

    def body(*refs):
        i = pl.program_id(0)
        ins = [r[...] for r in refs[:na + npar]]
        outs, vjp = jax.vjp(f, *ins)
        g = vjp(tuple(c[...].astype(o.dtype) for c, o in zip(refs[na + npar:na + npar + nc], outs)))
        add_refs = refs[na + npar + nc:na + npar + nc + len(adds)]
        orefs = refs[na + npar + nc + len(adds):]
        k = 0
        for j in range(na):
            if agrads[j] is not None:
                val = g[j]
                if len(agrads[j]) == 5:
                    val = val + add_refs[[id(t) for t in adds].index(id(agrads[j][4]))][...]
                orefs[k][...] = val.astype(orefs[k].dtype)
                k += 1
        for j in range(npar):
            if pgrads[j]:
                o = orefs[k]

                @pl.when(i == 0)
                def _(o=o):
                    o[...] = jnp.zeros_like(o)

                o[...] += g[na + j].astype(F32)
                k += 1

    in_specs = ([pl.BlockSpec(b, m) for _, b, m in acts] + [_whole(p) for p in params]
                + [pl.BlockSpec(b, m) for _, b, m in cts]
                + [pl.BlockSpec(g[2], g[3]) for g in agrads if g is not None and len(g) == 5])
    out_specs = [pl.BlockSpec(g[2], g[3]) for g in agrads if g is not None]
    out_shape = [jax.ShapeDtypeStruct(g[0], g[1]) for g in agrads if g is not None]
    for p, flag in zip(params, pgrads):
        if flag:
            out_specs.append(_whole(p))
            out_shape.append(jax.ShapeDtypeStruct(p.shape, F32))
    return pl.pallas_call(
        body, name=name, grid=(n,), in_specs=in_specs, out_specs=out_specs, out_shape=out_shape,
        compiler_params=_cparams(("arbitrary",)))(*[a for a, _, _ in acts], *params, *[c for c, _, _ in cts], *adds)


def _silu(x):
    return x * jax.nn.sigmoid(x)


def _rms(x, g):
    x = x.astype(F32)
    return (x * lax.rsqrt(jnp.mean(x * x, axis=-1, keepdims=True) + EPS) * g,)


def _shift_down(x, s):
    row = lax.broadcasted_iota(jnp.int32, x.shape, 0)
    return jnp.where(row >= s, pltpu.roll(x, s, 0), 0.0)


def _shift_up(x, s):
    n = x.shape[0]
    row = lax.broadcasted_iota(jnp.int32, x.shape, 0)
    return jnp.where(row < n - s, pltpu.roll(x, n - s, 0), 0.0)


@functools.partial(jax.custom_vjp, nondiff_argnums=(1,))
def _shift(x, s):
    return _shift_down(x, s)


_shift.defvjp(lambda x, s: (_shift_down(x, s), None), lambda s, _, ct: (_shift_up(ct, s),))


def _gdn_pre(mode):
    def f(x, w):
        x = x.astype(F32)
        y = x * w[CONV_WIDTH - 1:CONV_WIDTH, :]
        for j in range(CONV_WIDTH - 1):
            y = y + _shift(x, CONV_WIDTH - 1 - j) * w[j:j + 1, :]
        y = _silu(y)
        if mode != 'v':
            y = y * lax.rsqrt(jnp.sum(y * y, axis=-1, keepdims=True) + EPS)
        if mode == 'q':
            y = y * (GDN_HEAD_DIM ** -0.5)
        return (y,)
    return f


def _softplus(x):
    return jnp.maximum(x, 0.0) + jnp.log1p(jnp.exp(-jnp.abs(x)))


def _gdn_gates(ba, alog, dtb, e_beta, e_g):
    beta = jax.nn.sigmoid(ba)
    g = -jnp.exp(alog) * _softplus(ba + dtb)
    return _hnn(beta, lax.stop_gradient(e_beta)), _hnn(g, lax.stop_gradient(e_g))


@jax.custom_vjp
def _inverse_known(neg, t):
    return t


_inverse_known.defvjp(lambda neg, t: (t, t), lambda t, ct: (_mtn(t, _mnt(ct, t)), jnp.zeros_like(t)))


def _gdn_intra(q, k, v, gb, bb, t_known=None):
    n, c = len(q), q[0].shape[0]
    ri = lax.broadcasted_iota(jnp.int32, (c, c), 0)
    ci = lax.broadcasted_iota(jnp.int32, (c, c), 1)
    incl, strict = ri >= ci, ri > ci
    tri = incl.astype(F32)
    eye = (ri == ci).astype(F32)
    each = range(n)
    gc = [_hnn(tri, gb[i]) for i in each]
    decay = [jnp.exp(jnp.where(incl, gc[i][:, :c] - gc[i].T[:c, :], -1e30)) for i in each]
    kb = [k[i] * bb[i] for i in each]
    kk = [_bnt(kb[i], k[i]) for i in each]
    qk = [_bnt(q[i], k[i]) for i in each]
    p = [jnp.where(strict, -(kk[i] * decay[i]), 0.0) for i in each]
    if t_known is None:
        t = [eye + p[i] for i in each]
        for _ in range(int(math.log2(c)) - 1):
            p = [_mnn(p[i], p[i]) for i in each]
            tp = [_mnn(t[i], p[i]) for i in each]
            t = [t[i] + tp[i] for i in each]
    else:
        t = [_inverse_known(p[i], t_known[i]) for i in each]
    egc = [jnp.exp(gc[i]) for i in each]
    u_val = [_mnn(t[i], v[i] * bb[i]) for i in each]
    w_dec = [_mnn(t[i], kb[i] * egc[i]) for i in each]
    qk = [qk[i] * decay[i] for i in each]
    gl = [jnp.sum(gb[i], axis=0, keepdims=True) for i in each]
    return w_dec, u_val, qk, [q[i] * egc[i] for i in each], [k[i] * jnp.exp(gl[i] - gc[i]) for i in each], t


def _gdn_inter(w_dec, u_val, qk, q_dec, k_dec, gb, state):
    each = range(len(state))
    ws = [_bnn(w_dec[i], state[i]) for i in each]
    qs = [_bnn(q_dec[i], state[i]) for i in each]
    v_new = [u_val[i] - ws[i] for i in each]
    qv = [_bnn(qk[i], v_new[i]) for i in each]
    kv = [_btn(k_dec[i], v_new[i]) for i in each]
    decayed = [state[i] * jnp.exp(jnp.sum(gb[i], axis=0, keepdims=True)) for i in each]
    return [qs[i] + qv[i] for i in each], [decayed[i] + kv[i] for i in each]


def _gdn_post(o, z, g):
    parts = []
    for h in range(o.shape[1] // GDN_HEAD_DIM):
        oh = o[:, h * GDN_HEAD_DIM:(h + 1) * GDN_HEAD_DIM]
        parts.append(oh * lax.rsqrt(jnp.mean(oh * oh, axis=-1, keepdims=True) + EPS) * g)
    y = parts[0] if len(parts) == 1 else jnp.concatenate(parts, axis=1)
    return (y * _silu(z.astype(F32)),)


def _gelu(x):
    return 0.5 * x * (1.0 + jnp.tanh(0.7978845608028654 * (x + 0.044715 * x * x * x)))


def _s5_post1(ylin, xb, d):
    return (_gelu(ylin + d * xb.astype(F32)),)


def _s5_post2(t, z):
    w = t.shape[1] // 2
    return (t[:, :w] * jax.nn.sigmoid(t[:, w:]) * _silu(z.astype(F32)),)


def _attn(q, z, kv):
    w = q.shape[1]
    hd = w // XA_HEADS
    parts = []
    for h in range(XA_HEADS):
        s = _bnt(q[:, h * hd:(h + 1) * hd], kv[:, h * hd:(h + 1) * hd]) * (hd ** -0.5)
        s = s - jnp.max(s, axis=-1, keepdims=True)
        e = jnp.exp(s)
        p = e / jnp.sum(e, axis=-1, keepdims=True)
        parts.append(_bnn(p, kv[:, w + h * hd:w + (h + 1) * hd]))
    return (jnp.concatenate(parts, axis=1) * _silu(z.astype(F32)),)


def _merge(g0, g1, g2, pa, pb, pc):
    s0, s1, s2 = (jax.nn.sigmoid(t.astype(F32)) for t in (g0, g1, g2))
    return (s0 * pa + s1 * pb + s2 * pc,)


def _s5_params(lr, li, logdt, br, bi, e):
    dt = jnp.exp(logdt)
    mag = jnp.exp(lr * dt)
    ab_re, ab_im = mag * jnp.cos(li * dt), mag * jnp.sin(li * dt)
    den = lr * lr + li * li
    nr, ni = ab_re - 1.0, ab_im
    e = lax.stop_gradient(e)
    cre = _hnn((nr * lr + ni * li) / den, e)
    cim = _hnn((ni * lr - nr * li) / den, e)
    return ab_re, ab_im, cre * br - cim * bi, cre * bi + cim * br


def _gdn_blocks(s, w, per_step):
    nh, nc = w // GDN_HEAD_DIM, s // CHUNK
    cpb = math.gcd(per_step, nc)
    return nh, nc, cpb, nc // cpb, (cpb * CHUNK, w), (cpb * CHUNK, nh * CHUNK)


def _gdn_pairs(cpb, nh):
    wide, narrow = [], []
    for cb in range(cpb):
        rows = slice(cb * CHUNK, (cb + 1) * CHUNK)
        for h in range(nh):
            wide.append((rows, slice(h * GDN_HEAD_DIM, (h + 1) * GDN_HEAD_DIM)))
            narrow.append((rows, slice(h * CHUNK, (h + 1) * CHUNK)))
    return wide, narrow


def _gdn_intra_fwd(q, k, v, gb, bb, per_step=4):
    s, w = q.shape
    nh, nc, cpb, n, wide, narrow = _gdn_blocks(s, w, per_step)

    def body(q_ref, k_ref, v_ref, g_ref, b_ref, wd_ref, uv_ref, qk_ref, qd_ref, kd_ref, t_ref):
        wide, narrow = _gdn_pairs(cpb, nh)
        res = _gdn_intra(*[[r[ix] for ix in wide] for r in (q_ref, k_ref, v_ref, g_ref, b_ref)])
        for ref, vals, where in zip((wd_ref, uv_ref, qk_ref, qd_ref, kd_ref, t_ref), res,
                                    (wide, wide, narrow, wide, wide, narrow)):
            for ix, val in zip(where, vals):
                ref[ix] = val

    bw = pl.BlockSpec(wide, lambda i: (i, 0))
    bn = pl.BlockSpec(narrow, lambda i: (i, 0))
    fw = jax.ShapeDtypeStruct((s, w), F32)
    fn = jax.ShapeDtypeStruct((s, nh * CHUNK), F32)
    return pl.pallas_call(
        body, name="gdn_intra", grid=(n,), in_specs=[bw] * 5, out_specs=[bw, bw, bn, bw, bw, bn],
        out_shape=[fw, fw, fn, fw, fw, fn], compiler_params=_cparams(("parallel",)))(q, k, v, gb, bb)


def _gdn_intra_bwd(q, k, v, gb, bb, t, cts, dgb_inter, per_step=4):
    s, w = q.shape
    nh, nc, cpb, n, wide, narrow = _gdn_blocks(s, w, per_step)

    def body(q_ref, k_ref, v_ref, g_ref, b_ref, t_ref, cwd, cuv, cqk, cqd, ckd, dgi, dq_ref, dk_ref, dv_ref, dg_ref,
             db_ref):
        wide, narrow = _gdn_pairs(cpb, nh)
        t_known = [t_ref[ix] for ix in narrow]
        _, vjp = jax.vjp(lambda *a: _gdn_intra(*a, t_known=t_known)[:5],
                         *[[r[ix] for ix in wide] for r in (q_ref, k_ref, v_ref, g_ref, b_ref)])
        cts = tuple([r[ix] for ix in where] for r, where in zip((cwd, cuv, cqk, cqd, ckd),
                                                               (wide, wide, narrow, wide, wide)))
        dq, dk, dv, dg, db = vjp(cts)
        for j, ix in enumerate(wide):
            dq_ref[ix], dk_ref[ix], dv_ref[ix], db_ref[ix] = dq[j], dk[j], dv[j], db[j]
            dg_ref[ix] = dg[j] + dgi[ix]

    bw = pl.BlockSpec(wide, lambda i: (i, 0))
    bn = pl.BlockSpec(narrow, lambda i: (i, 0))
    return pl.pallas_call(
        body, name="gdn_intra_bwd", grid=(n,), in_specs=[bw] * 5 + [bn, bw, bw, bn, bw, bw, bw], out_specs=[bw] * 5,
        out_shape=[jax.ShapeDtypeStruct((s, w), F32)] * 5,
        compiler_params=_cparams(("parallel",)))(q, k, v, gb, bb, t, *cts, dgb_inter)


def _gdn_inter_fwd(wd, uv, qk, qd, kd, gb, per_step=4):
    s, w = wd.shape
    nh, nc, cpb, n, wide, narrow = _gdn_blocks(s, w, per_step)
    hd = GDN_HEAD_DIM

    def body(wd_ref, uv_ref, qk_ref, qd_ref, kd_ref, g_ref, o_ref, st_ref, state):
        @pl.when(pl.program_id(0) == 0)
        def _():
            state[...] = jnp.zeros_like(state)

        wide, narrow = _gdn_pairs(cpb, nh)
        st = [state[h] for h in range(nh)]
        for cb in range(cpb):
            wi, na = wide[cb * nh:(cb + 1) * nh], narrow[cb * nh:(cb + 1) * nh]
            for h in range(nh):
                st_ref[cb, h] = st[h]
            o, st = _gdn_inter([wd_ref[ix] for ix in wi], [uv_ref[ix] for ix in wi], [qk_ref[ix] for ix in na],
                               [qd_ref[ix] for ix in wi], [kd_ref[ix] for ix in wi], [g_ref[ix] for ix in wi], st)
            for h in range(nh):
                o_ref[wi[h]] = o[h]
        for h in range(nh):
            state[h] = st[h]

    bw = pl.BlockSpec(wide, lambda i: (i, 0))
    bn = pl.BlockSpec(narrow, lambda i: (i, 0))
    return pl.pallas_call(
        body, name="gdn_inter", grid=(n,), in_specs=[bw, bw, bn, bw, bw, bw],
        out_specs=[bw, pl.BlockSpec((cpb, nh, hd, hd), lambda i: (i, 0, 0, 0))],
        out_shape=[jax.ShapeDtypeStruct((s, w), F32), jax.ShapeDtypeStruct((nc, nh, hd, hd), F32)],
        scratch_shapes=[pltpu.VMEM((nh, hd, hd), F32)],
        compiler_params=_cparams(("arbitrary",)))(wd, uv, qk, qd, kd, gb)


def _gdn_inter_bwd(wd, uv, qk, qd, kd, gb, states, do, per_step=4):
    s, w = wd.shape
    nh, nc, cpb, n, wide, narrow = _gdn_blocks(s, w, per_step)
    hd = GDN_HEAD_DIM

    def body(wd_ref, uv_ref, qk_ref, qd_ref, kd_ref, g_ref, st_ref, do_ref, cwd, cuv, cqk, cqd, ckd, dg_ref, dstate):
        @pl.when(pl.program_id(0) == 0)
        def _():
            dstate[...] = jnp.zeros_like(dstate)

        wide, narrow = _gdn_pairs(cpb, nh)
        dst = [dstate[h] for h in range(nh)]
        for cb in reversed(range(cpb)):
            wi, na = wide[cb * nh:(cb + 1) * nh], narrow[cb * nh:(cb + 1) * nh]
            _, vjp = jax.vjp(_gdn_inter, [wd_ref[ix] for ix in wi], [uv_ref[ix] for ix in wi],
                             [qk_ref[ix] for ix in na], [qd_ref[ix] for ix in wi], [kd_ref[ix] for ix in wi],
                             [g_ref[ix] for ix in wi], [st_ref[cb, h] for h in range(nh)])
            dwd, duv, dqk, dqd, dkd, dg, dst = vjp(([do_ref[ix] for ix in wi], dst))
            for h in range(nh):
                cwd[wi[h]], cuv[wi[h]], cqk[na[h]], cqd[wi[h]], ckd[wi[h]], dg_ref[wi[h]] = (
                    dwd[h], duv[h], dqk[h], dqd[h], dkd[h], dg[h])
        for h in range(nh):
            dstate[h] = dst[h]

    bw = pl.BlockSpec(wide, lambda i: (n - 1 - i, 0))
    bn = pl.BlockSpec(narrow, lambda i: (n - 1 - i, 0))
    fw = jax.ShapeDtypeStruct((s, w), F32)
    return pl.pallas_call(
        body, name="gdn_inter_bwd", grid=(n,),
        in_specs=[bw, bw, bn, bw, bw, bw, pl.BlockSpec((cpb, nh, hd, hd), lambda i: (n - 1 - i, 0, 0, 0)), bw],
        out_specs=[bw, bw, bn, bw, bw, bw],
        out_shape=[fw, fw, jax.ShapeDtypeStruct((s, nh * CHUNK), F32), fw, fw, fw],
        scratch_shapes=[pltpu.VMEM((nh, hd, hd), F32)],
        compiler_params=_cparams(("arbitrary",)))(wd, uv, qk, qd, kd, gb, states, do)


def _s5_coef(ar, ai):
    nl = ar.shape[1]

    def body(ar_ref, ai_ref, o_ref):
        row = lax.broadcasted_iota(jnp.int32, (8, nl), 0)
        for base, sign in ((0, 1.0), (8, -1.0)):
            pr = [jnp.broadcast_to(ar_ref[...], (8, nl))]
            pi = [jnp.broadcast_to(ai_ref[...], (8, nl)) * sign]
            for _ in range(7):
                pr.append(pr[-1] * pr[0] - pi[-1] * pi[0])
                pi.append(pr[-2] * pi[0] + pi[-1] * pr[0])
            for j, d in enumerate((1, 2, 4)):
                m = (row >= d) if base == 0 else (row <= 7 - d)
                o_ref[base + 2 * j] = jnp.where(m, pr[d - 1], 0.0)
                o_ref[base + 2 * j + 1] = jnp.where(m, pi[d - 1], 0.0)
            cr, ci = jnp.zeros((8, nl), F32), jnp.zeros((8, nl), F32)
            for t in range(8):
                e = t if base == 0 else 7 - t
                cr = jnp.where(row == t, pr[e], cr)
                ci = jnp.where(row == t, pi[e], ci)
            o_ref[base + 6] = cr
            o_ref[base + 7] = ci

    return pl.pallas_call(body, name="s5_coef", out_shape=jax.ShapeDtypeStruct((16, 8, nl), F32),
                          compiler_params=_cparams())(ar, ai)


def _scan_tile(src_re, src_im, dst_re, dst_im, coef_ref, carry_re, carry_im, ts, reverse, extra=None):
    nl = src_re.shape[1]
    base = 8 if reverse else 0
    ng = ts // 8
    for lc in range(nl // SCAN_LANES):
        ln = slice(lc * SCAN_LANES, (lc + 1) * SCAN_LANES)
        m = [coef_ref[base + j, :, ln] for j in range(8)]
        row = lax.broadcasted_iota(jnp.int32, (8, SCAN_LANES), 0)

        def step(r, carry, ln=ln, m=m, row=row):
            grp = (ng - 1 - r) if reverse else r
            rows = pl.ds(pl.multiple_of(grp * 8, 8), 8)
            xr, xi = src_re[rows, ln], src_im[rows, ln]
            for j, d in enumerate((1, 2, 4)):
                sh = 8 - d if reverse else d
                sr, si = pltpu.roll(xr, sh, 0), pltpu.roll(xi, sh, 0)
                mr, mi = m[2 * j], m[2 * j + 1]
                xr, xi = xr + mr * sr - mi * si, xi + mr * si + mi * sr
            cr, ci = carry[0], carry[1]
            hr = xr + m[6] * cr - m[7] * ci
            hi = xi + m[6] * ci + m[7] * cr
            dst_re[rows, ln] = hr
            dst_im[rows, ln] = hi
            edge = 0 if reverse else 7
            out = (jnp.broadcast_to(hr[edge:edge + 1, :], hr.shape), jnp.broadcast_to(hi[edge:edge + 1, :], hi.shape))
            if extra is not None:
                h_re, h_im, halo_re, halo_im, first, _, _ = extra
                prev = pl.ds(pl.multiple_of(jnp.maximum(grp - 1, 0) * 8, 8), 8)
                use_halo = grp == 0
                pr = jnp.where(use_halo, halo_re[:, ln] * first, h_re[prev, ln])
                pi = jnp.where(use_halo, halo_im[:, ln] * first, h_im[prev, ln])
                qr = jnp.where(row == 0, jnp.broadcast_to(pr[7:8, :], pr.shape), pltpu.roll(h_re[rows, ln], 1, 0))
                qi = jnp.where(row == 0, jnp.broadcast_to(pi[7:8, :], pi.shape), pltpu.roll(h_im[rows, ln], 1, 0))
                out = out + (carry[2] + hr * qr + hi * qi, carry[3] + hi * qr - hr * qi)
            return out

        init = (carry_re[:, ln], carry_im[:, ln])
        if extra is not None:
            init = init + (extra[5][:, ln], extra[6][:, ln])
        fin = lax.fori_loop(0, ng, step, init)
        carry_re[:, ln] = fin[0]
        carry_im[:, ln] = fin[1]
        if extra is not None:
            extra[5][:, ln] = fin[2]
            extra[6][:, ln] = fin[3]


def _s5_fwd(xb, bb_re, bb_im, c_re, c_im, coef, ts):
    s, w = xb.shape
    nb = bb_re.shape[0]
    nl = nb * 512

    def body(x_ref, bre_ref, bim_ref, cre_ref, cim_ref, coef_ref, hre_ref, him_ref, y_ref, ure, uim, car_re, car_im):
        @pl.when(pl.program_id(0) == 0)
        def _():
            car_re[...] = jnp.zeros_like(car_re)
            car_im[...] = jnp.zeros_like(car_im)

        for b in range(nb):
            xs = x_ref[:, b * 128:(b + 1) * 128].astype(BF16)
            ure[:, b * 512:(b + 1) * 512] = jnp.dot(xs, bre_ref[b], preferred_element_type=F32)
            uim[:, b * 512:(b + 1) * 512] = jnp.dot(xs, bim_ref[b], preferred_element_type=F32)
        _scan_tile(ure, uim, hre_ref, him_ref, coef_ref, car_re, car_im, ts, False)
        for b in range(nb):
            hr = hre_ref[:, b * 512:(b + 1) * 512].astype(BF16)
            hi = him_ref[:, b * 512:(b + 1) * 512].astype(BF16)
            y_ref[:, b * 128:(b + 1) * 128] = (jnp.dot(hr, cre_ref[b], preferred_element_type=F32)
                                               - jnp.dot(hi, cim_ref[b], preferred_element_type=F32))

    row = lambda wd: pl.BlockSpec((ts, wd), lambda i: (i, 0))
    return pl.pallas_call(
        body, name="s5_fwd", grid=(s // ts,),
        in_specs=[row(w), _whole(bb_re), _whole(bb_im), _whole(c_re), _whole(c_im), _whole(coef)],
        out_specs=[row(nl), row(nl), row(w)],
        out_shape=[jax.ShapeDtypeStruct((s, nl), F32), jax.ShapeDtypeStruct((s, nl), F32),
                   jax.ShapeDtypeStruct((s, w), F32)],
        scratch_shapes=[pltpu.VMEM((ts, nl), F32), pltpu.VMEM((ts, nl), F32), pltpu.VMEM((8, nl), F32),
                        pltpu.VMEM((8, nl), F32)],
        compiler_params=_cparams(("arbitrary",)))(xb, bb_re, bb_im, c_re, c_im, coef)


def _s5_bwd(dy, xb, h_re, h_im, bb_re, bb_im, c_re, c_im, coef, ts):
    s, w = xb.shape
    nb = bb_re.shape[0]
    nl = nb * 512
    nt = s // ts

    def body(dy_ref, x_ref, hre_ref, him_ref, halo_re, halo_im, bre_ref, bim_ref, cre_ref, cim_ref, coef_ref,
             dx_ref, dbre_ref, dbim_ref, dcre_ref, dcim_ref, dare_ref, daim_ref, gre, gim, car_re, car_im):
        i = pl.program_id(0)

        @pl.when(i == 0)
        def _():
            for r in (car_re, car_im, dbre_ref, dbim_ref, dcre_ref, dcim_ref, dare_ref, daim_ref):
                r[...] = jnp.zeros_like(r)

        for b in range(nb):
            dyb = dy_ref[:, b * 128:(b + 1) * 128].astype(BF16)
            gre[:, b * 512:(b + 1) * 512] = lax.dot_general(dyb, cre_ref[b], (((1,), (1,)), ((), ())),
                                                            preferred_element_type=F32)
            gim[:, b * 512:(b + 1) * 512] = -lax.dot_general(dyb, cim_ref[b], (((1,), (1,)), ((), ())),
                                                             preferred_element_type=F32)
            hr = hre_ref[:, b * 512:(b + 1) * 512].astype(BF16)
            hi = him_ref[:, b * 512:(b + 1) * 512].astype(BF16)
            dcre_ref[b] += lax.dot_general(hr, dyb, (((0,), (0,)), ((), ())), preferred_element_type=F32)
            dcim_ref[b] -= lax.dot_general(hi, dyb, (((0,), (0,)), ((), ())), preferred_element_type=F32)
        first = (i != nt - 1).astype(F32)
        _scan_tile(gre, gim, gre, gim, coef_ref, car_re, car_im, ts, True,
                   extra=(hre_ref, him_ref, halo_re, halo_im, first, dare_ref, daim_ref))
        for b in range(nb):
            gr = gre[:, b * 512:(b + 1) * 512].astype(BF16)
            gi = gim[:, b * 512:(b + 1) * 512].astype(BF16)
            xs = x_ref[:, b * 128:(b + 1) * 128].astype(BF16)
            dx_ref[:, b * 128:(b + 1) * 128] = (
                lax.dot_general(gr, bre_ref[b], (((1,), (1,)), ((), ())), preferred_element_type=F32)
                + lax.dot_general(gi, bim_ref[b], (((1,), (1,)), ((), ())), preferred_element_type=F32))
            dbre_ref[b] += lax.dot_general(xs, gr, (((0,), (0,)), ((), ())), preferred_element_type=F32)
            dbim_ref[b] += lax.dot_general(xs, gi, (((0,), (0,)), ((), ())), preferred_element_type=F32)

    row = lambda wd: pl.BlockSpec((ts, wd), lambda i: (nt - 1 - i, 0))
    halo = pl.BlockSpec((8, nl), lambda i: (jnp.maximum((nt - 1 - i) * (ts // 8) - 1, 0), 0))
    return pl.pallas_call(
        body, name="s5_bwd", grid=(nt,),
        in_specs=[row(w), row(w), row(nl), row(nl), halo, halo, _whole(bb_re), _whole(bb_im), _whole(c_re),
                  _whole(c_im), _whole(coef)],
        out_specs=[row(w), _whole(bb_re), _whole(bb_im), _whole(c_re), _whole(c_im),
                   pl.BlockSpec((8, nl), lambda i: (0, 0)), pl.BlockSpec((8, nl), lambda i: (0, 0))],
        out_shape=[jax.ShapeDtypeStruct((s, w), F32), jax.ShapeDtypeStruct(bb_re.shape, F32),
                   jax.ShapeDtypeStruct(bb_im.shape, F32), jax.ShapeDtypeStruct(c_re.shape, F32),
                   jax.ShapeDtypeStruct(c_im.shape, F32), jax.ShapeDtypeStruct((8, nl), F32),
                   jax.ShapeDtypeStruct((8, nl), F32)],
        scratch_shapes=[pltpu.VMEM((ts, nl), F32), pltpu.VMEM((ts, nl), F32), pltpu.VMEM((8, nl), F32),
                        pltpu.VMEM((8, nl), F32)],
        compiler_params=_cparams(("arbitrary",)))(dy, xb, h_re, h_im, h_re, h_im, bb_re, bb_im, c_re, c_im, coef)


def _final(x, mo, target, fg, ts):
    s, d = x.shape

    def f(x, mo, fg, tgt):
        y = _rms(x + mo, fg)[0]
        err = y - tgt
        return 0.5 * jnp.sum(jnp.mean(err * err, axis=-1, keepdims=True), axis=0, keepdims=True)

    def body(x_ref, mo_ref, t_ref, fg_ref, dh_ref, dfg_ref, loss_ref):
        @pl.when(pl.program_id(0) == 0)
        def _():
            dfg_ref[...] = jnp.zeros_like(dfg_ref)
            loss_ref[...] = jnp.zeros_like(loss_ref)

        loss, vjp = jax.vjp(f, x_ref[...], mo_ref[...], fg_ref[...], t_ref[...])
        _, dmo, dfg, _ = vjp(jnp.ones((1, 1), F32))
        dh_ref[...] = dmo
        dfg_ref[...] += dfg
        loss_ref[...] += jnp.broadcast_to(loss, loss_ref.shape)

    row = pl.BlockSpec((ts, d), lambda i: (i, 0))
    return pl.pallas_call(
        body, name="final", grid=(s // ts,), in_specs=[row, row, row, _whole(fg)],
        out_specs=[row, _whole(fg), pl.BlockSpec((8, 128), lambda i: (0, 0))],
        out_shape=[jax.ShapeDtypeStruct((s, d), F32), jax.ShapeDtypeStruct(fg.shape, F32),
                   jax.ShapeDtypeStruct((8, 128), F32)],
        compiler_params=_cparams(("arbitrary",)))(x, mo, target, fg)


def _other_chips(x, y):
    return [((1 - x, y), 2 * (1 - x) + y), ((x, 1 - y), 2 * x + 1 - y), ((1 - x, 1 - y), 2 * (1 - x) + 1 - y)]


def _comm_call(body, name, srcs, out_shapes, n_sems):
    n = len(srcs)
    return pl.pallas_call(
        body, name=name, in_specs=[pl.BlockSpec(memory_space=pl.ANY)] * n,
        out_specs=[pl.BlockSpec(memory_space=pl.ANY)] * n, out_shape=out_shapes,
        scratch_shapes=[pltpu.SemaphoreType.DMA((n, n_sems)), pltpu.SemaphoreType.DMA((n, n_sems)),
                        pltpu.SemaphoreType.DMA((n,))],
        compiler_params=pltpu.CompilerParams(has_side_effects=True))(*srcs)


def _gather(srcs, name):
    n = len(srcs)

    def body(*refs):
        src, out = refs[:n], refs[n:2 * n]
        send_sems, recv_sems, local_sems = refs[2 * n:]
        x, y, c = lax.axis_index("x"), lax.axis_index("y"), lax.axis_index("c")
        me, sib_slot, sib = 4 * x + 2 * y + c, 4 * x + 2 * y + 1 - c, (x, y, 1 - c)
        chips = _other_chips(x, y)

        def cp(a, k, src_ref, slot, to):
            return pltpu.make_async_remote_copy(
                src_ref=src_ref, dst_ref=out[a].at[slot], send_sem=send_sems.at[a, k], recv_sem=recv_sems.at[a, k],
                device_id=to, device_id_type=pl.DeviceIdType.MESH)

        local = [pltpu.make_async_copy(src[a], out[a].at[me], local_sems.at[a]) for a in range(n)]
        first = [cp(a, 0, src[a], me, sib) for a in range(n)]
        first += [cp(a, 1 + j, src[a], me, (*chip, c)) for j, (chip, _) in enumerate(chips) for a in range(n)]
        for d in local + first:
            d.start()
        passed = []
        for j, (chip, q) in enumerate(chips):
            for a in range(n):
                cp(a, 1 + j, src[a], 2 * q + c, sib).wait_recv()
                fwd = cp(a, 4 + j, out[a].at[2 * q + c], 2 * q + c, sib)
                fwd.start()
                passed.append(fwd)
        for a in range(n):
            cp(a, 0, src[a], sib_slot, sib).wait_recv()
        for j, (chip, q) in enumerate(chips):
            for a in range(n):
                cp(a, 4 + j, src[a], 2 * q + 1 - c, sib).wait_recv()
        for d in first + passed:
            d.wait_send()
        for d in local:
            d.wait()

    return _comm_call(body, name, srcs, [jax.ShapeDtypeStruct((N_DEV,) + s.shape, s.dtype) for s in srcs], 7)


def _all_peers(x, y, c):
    out = []
    for k in range(1, N_DEV):
        px = 1 - x if k & 4 else x
        py = 1 - y if k & 2 else y
        pc = 1 - c if k & 1 else c
        out.append(((px, py, pc), 4 * px + 2 * py + pc))
    return out


_HBM = pl.BlockSpec(memory_space=pltpu.HBM)
_SEM = pl.BlockSpec(memory_space=pltpu.SEMAPHORE)
_DATAFLOW = pltpu.SideEffectType.DATAFLOW_SIDE_EFFECTING


def _send_whole(ref, slot):
    return ref


def _send_slot(ref, slot):
    return ref.at[slot]


def _direct_copies(src, land, send_sems, recv_sems, picks, arriving):
    x, y, c = lax.axis_index("x"), lax.axis_index("y"), lax.axis_index("c")
    me = 4 * x + 2 * y + c
    out = []
    for k, (pos, slot) in enumerate(_all_peers(x, y, c)):
        for a in range(len(src)):
            sem = a * (N_DEV - 1) + k
            out.append(pltpu.make_async_remote_copy(
                src_ref=picks[a](src[a], slot), dst_ref=land[a].at[slot if arriving else me],
                send_sem=send_sems.at[sem], recv_sem=recv_sems.at[sem], device_id=pos,
                device_id_type=pl.DeviceIdType.MESH))
    return out


def _direct_start(srcs, lands, picks, name):
    n = len(srcs)

    def body(*refs):
        src, land = refs[:n], refs[n:2 * n]
        send_sems, recv_sems = refs[2 * n], refs[2 * n + 1]
        for push in _direct_copies(src, land, send_sems, recv_sems, picks, False):
            push.start()
        refs[-1][...] = jnp.zeros_like(refs[-1])

    arrays = [pltpu.with_memory_space_constraint(t, pltpu.HBM) for t in list(srcs) + list(lands)]
    outs = pl.pallas_call(
        body, name=name, in_specs=[_HBM] * (2 * n),
        out_specs=(_SEM, _SEM, *[_HBM] * (2 * n), pl.BlockSpec(memory_space=pltpu.VMEM)),
        out_shape=(pltpu.SemaphoreType.DMA((n * (N_DEV - 1),)), pltpu.SemaphoreType.DMA((n * (N_DEV - 1),)),
                   *[pltpu.HBM(t.shape, t.dtype) for t in arrays], jax.ShapeDtypeStruct((8, 128), F32)),
        input_output_aliases={i: 2 + i for i in range(2 * n)},
        compiler_params=pltpu.CompilerParams(has_side_effects=_DATAFLOW))(*arrays)
    return outs[:-1], outs[-1]


def _direct_wait(started, picks, after, name):
    send_sems, recv_sems, *thru = started
    n = len(thru) // 2

    def body(*refs):
        src, land = refs[:n], refs[n:2 * n]
        for arrive in _direct_copies(src, land, refs[2 * n], refs[2 * n + 1], picks, True):
            arrive.wait_send()
            arrive.wait_recv()

    outs = pl.pallas_call(
        body, name=name, in_specs=[_HBM] * (2 * n) + [_SEM, _SEM, pl.BlockSpec(memory_space=pl.ANY)],
        out_specs=[_HBM] * (2 * n), out_shape=[pltpu.HBM(t.shape, t.dtype) for t in thru],
        input_output_aliases={i: i for i in range(2 * n)},
        compiler_params=pltpu.CompilerParams(has_side_effects=_DATAFLOW))(*thru, send_sems, recv_sems, after)
    return outs[:n], outs[n:]


def _pick_rows(r, pref):
    t = (min(pref, r) // 8) * 8
    while t >= 8:
        if r % t == 0:
            return t
        t -= 8
    return r


def _w_in_from_shards(t, lo, hi):
    n, r, cs = t.shape
    tr = _pick_rows(r, 256)
    wm = n * cs - (hi - lo)

    def body(t_ref, m_ref, b_ref):
        full = jnp.concatenate([t_ref[j] for j in range(n)], axis=1)
        m_ref[...] = jnp.concatenate([full[:, :lo], full[:, hi:]], axis=1)
        b_ref[...] = jnp.concatenate([full[:, lo:hi], jnp.zeros((tr, 128 - (hi - lo)), full.dtype)], axis=1)

    return pl.pallas_call(
        body, name="w_in_layout", grid=(r // tr,), in_specs=[pl.BlockSpec((n, tr, cs), lambda i: (0, i, 0))],
        out_specs=[pl.BlockSpec((tr, wm), lambda i: (i, 0)), pl.BlockSpec((tr, 128), lambda i: (i, 0))],
        out_shape=[jax.ShapeDtypeStruct((r, wm), t.dtype), jax.ShapeDtypeStruct((r, 128), t.dtype)],
        compiler_params=_cparams(("parallel",)))(t)


def _w_in_to_shards(gm, gb, lo, hi, dtype, first_row=0, name="dw_in_layout"):
    r, wm = gm.shape
    cs = (wm + hi - lo) // N_DEV
    tr = _pick_rows(r, 64)
    assert first_row % tr == 0
    b0 = first_row // tr

    def body(m_ref, b_ref, o_ref):
        m = m_ref[...]
        full = jnp.concatenate([m[:, :lo], b_ref[:, :hi - lo], m[:, lo:]], axis=1)
        for j in range(N_DEV):
            o_ref[j] = full[:, j * cs:(j + 1) * cs].astype(o_ref.dtype)

    return pl.pallas_call(
        body, name=name, grid=(r // tr,),
        in_specs=[pl.BlockSpec((tr, wm), lambda i: (i, 0)), pl.BlockSpec((tr, 128), lambda i: (i + b0, 0))],
        out_specs=pl.BlockSpec((N_DEV, tr, cs), lambda i: (0, i, 0)),
        out_shape=jax.ShapeDtypeStruct((N_DEV, r, cs), dtype), compiler_params=_cparams(("parallel",)))(gm, gb)


def _pack(arrs, dtype, lead=()):
    nlead = len(lead)
    flat = jnp.concatenate([a.astype(dtype).reshape(lead + (-1,)) for a in arrs], axis=nlead)
    n = flat.shape[-1]
    unit = PACK_WIDTH * PACK_ROWS
    pad = (-n) % unit
    flat = jnp.pad(flat, [(0, 0)] * nlead + [(0, pad)])
    return flat.reshape(lead + ((n + pad) // PACK_WIDTH, PACK_WIDTH))


def _unpack(buf, shapes, lead=()):
    flat = buf.reshape(lead + (-1,))
    out, off = [], 0
    for shp in shapes:
        n = math.prod(shp)
        out.append(flat[..., off:off + n].reshape(lead + tuple(shp)))
        off += n
    return out


def _adam_math(w, g, m, v):
    m = ADAM_B1 * m + (1.0 - ADAM_B1) * g
    v = ADAM_B2 * v + (1.0 - ADAM_B2) * (g * g)
    m_hat = m / (1.0 - ADAM_B1 ** ADAM_STEP)
    v_hat = v / (1.0 - ADAM_B2 ** ADAM_STEP)
    delta = -ADAM_LR * (m_hat / (jnp.sqrt(v_hat) + ADAM_EPS) + ADAM_WD * w)
    return delta, m, v


def _sum_adam(parts, w, m, v, name, own=None, me=None):
    r, c = w.shape
    parts_list = list(parts) if isinstance(parts, (list, tuple)) else [parts]
    own_list = list(own) if isinstance(own, (list, tuple)) else [own]
    nparts = parts_list[0].shape[0]
    lanes = -(-c // 128) * 128
    tr = _pick_rows(r // len(parts_list), max(8, (6 * 1024 * 1024) // (nparts * lanes * 4)))

    def finish(g, w_ref, m_ref, v_ref, g_ref, d_ref, nm_ref, nv_ref):
        d, nm, nv = _adam_math(w_ref[...], g, m_ref[...], v_ref[...])
        g_ref[...] = g
        d_ref[...] = d
        nm_ref[...] = nm
        nv_ref[...] = nv

    out_shape = [jax.ShapeDtypeStruct((r, c), F32)] * 4
    if own is None:
        def body(p_ref, *rest):
            g = p_ref[0].astype(F32)
            for j in range(1, nparts):
                g = g + p_ref[j].astype(F32)
            finish(g, *rest)

        row = pl.BlockSpec((tr, c), lambda i: (i, 0))
        return pl.pallas_call(
            body, name=name, grid=(r // tr,),
            in_specs=[pl.BlockSpec((nparts, tr, c), lambda i: (0, i, 0)), row, row, row],
            out_specs=[row] * 4, out_shape=out_shape, compiler_params=_cparams(("parallel",)))(parts, w, m, v)

    nch = len(parts_list)
    tpc = r // nch // tr

    def body(me_ref, *refs):
        p_refs, o_refs, rest = refs[:nch], refs[nch:2 * nch], refs[2 * nch:]
        i = pl.program_id(0)

        def of_chunk(vals):
            out = vals[0]
            for q in range(1, nch):
                out = jnp.where(i >= q * tpc, vals[q], out)
            return out

        mine = of_chunk([o[...].astype(F32) for o in o_refs])
        g = None
        for j in range(nparts):
            t = jnp.where(me_ref[0] == j, mine, of_chunk([p[j].astype(F32) for p in p_refs]))
            g = t if g is None else g + t
        finish(g, *rest)

    row = pl.BlockSpec((tr, c), lambda i, me_ref: (i, 0))
    step = lambda i, q: jnp.clip(i - q * tpc, 0, tpc - 1)
    return pl.pallas_call(
        body, name=name,
        grid_spec=pltpu.PrefetchScalarGridSpec(
            num_scalar_prefetch=1, grid=(r // tr,),
            in_specs=[pl.BlockSpec((nparts, tr, c), lambda i, me_ref, q=q: (0, step(i, q), 0)) for q in range(nch)]
            + [pl.BlockSpec((None, tr, c), lambda i, me_ref, q=q: (me_ref[0], step(i, q), 0)) for q in range(nch)]
            + [row, row, row],
            out_specs=[row] * 4),
        out_shape=out_shape, compiler_params=_cparams(("parallel",)))(me, *parts_list, *own_list, w, m, v)


def _block_diag(t):
    nb, g, a, b = t.shape
    eye = jnp.eye(g, dtype=t.dtype)
    return jnp.einsum('ngab,gh->ngahb', t, eye).reshape(nb, g * a, g * b)


def _diag_blocks(t, a, b):
    nb = t.shape[0]
    g = S5_GROUPS_PER_BLOCK
    t = t.reshape(nb, g, a, g, b)
    return jnp.stack([t[:, j, :, j, :] for j in range(g)], axis=1)


def _local_step(x, mem, target, p, late_weights=None, early_grads=None, last_grads=None):
    s, d = x.shape
    gw = d // 2
    nh = gw // GDN_HEAD_DIM
    ng = gw // S5_GROUP
    nb = ng // S5_GROUPS_PER_BLOCK
    nl = ng * S5_STATE
    ts = min(256, s)
    nt = s // ts
    grads = {}

    w_main, w_ba = p['w_main'], p['w_ba']
    CB_ZA, CB_XB, CB_ZB, CB_QC, CB_ZC, CB_G = 3, 4, 5, 6, 7, 8

    u = _tile_fwd(_rms, "rms_fwd", nt, [_rt(x, ts)], [p['norm_g']],
                  [((s, d), BF16, (ts, d), lambda i: (i, 0))])[0]
    proj = _mm(u, w_main, tm=1024, tn=2048, tk=1024, out_dtype=BF16, name="proj_main")
    pba = _mm(u, w_ba, name="proj_ba")
    if late_weights is not None:
        p = {**p, **late_weights(proj)}

    conv_w = p['conv_w']
    col = lambda arr, cb: (arr, (s, GDN_HEAD_DIM), lambda i, cb=cb: (0, cb + i))
    qkv = []
    for j, mode in enumerate(('q', 'k', 'v')):
        off = j * nh
        qkv.append(_tile_fwd(
            _gdn_pre(mode), "gdn_pre_" + mode, nh, [col(proj, off), (conv_w, (CONV_WIDTH, GDN_HEAD_DIM), lambda i, off=off: (0, off + i))],
            [], [((s, gw), F32, (s, GDN_HEAD_DIM), lambda i: (0, i))])[0])
    q, k, v = qkv
    lane = jnp.arange(128)[:, None]
    colh = jnp.arange(gw)[None, :] // GDN_HEAD_DIM
    e_beta = (lane == colh).astype(F32)
    e_g = (lane == colh + nh).astype(F32)
    alog_row = jnp.pad(p['gdn_a_log'], ((0, 0), (nh, 128 - 2 * nh)))
    dtb_row = jnp.pad(p['gdn_dt_bias'], ((0, 0), (nh, 128 - 2 * nh)))
    row_gw = lambda: ((s, gw), F32, (ts, gw), lambda i: (i, 0))
    betab, gb = _tile_fwd(_gdn_gates, "gdn_gates", nt, [_rt(pba, ts)], [alog_row, dtb_row, e_beta, e_g],
                          [row_gw(), row_gw()])
    *intra, t_inv = _gdn_intra_fwd(q, k, v, gb, betab)
    o_raw, states = _gdn_inter_fwd(*intra, gb)
    ga = _tile_fwd(_gdn_post, "gdn_post", nt, [_rt(o_raw, ts), _rt(proj, ts, CB_ZA, gw)], [p['gdn_norm_g']],
                   [((s, gw), BF16, (ts, gw), lambda i: (i, 0))])[0]

    e_rep = (jnp.arange(S5_STATE)[:, None] == jnp.arange(S5_STATE * S5_GROUP)[None, :] // S5_GROUP).astype(F32)
    s5_in = [p['s5_lambda_re'], p['s5_lambda_im'], p['s5_log_dt'].reshape(ng, 1),
             p['s5_b_re'].reshape(ng, S5_STATE * S5_GROUP), p['s5_b_im'].reshape(ng, S5_STATE * S5_GROUP), e_rep]
    one = lambda shp: (shp, F32, shp, lambda i, n=len(shp): (0,) * n)
    ab_re, ab_im, bbr, bbi = _tile_fwd(_s5_params, "s5_params", 1, [], s5_in,
                                       [one((ng, S5_STATE)), one((ng, S5_STATE)), one((ng, S5_STATE * S5_GROUP)),
                                        one((ng, S5_STATE * S5_GROUP))])
    coef = _s5_coef(ab_re.reshape(1, nl), ab_im.reshape(1, nl))
    to_bd_b = lambda t: _block_diag(t.reshape(nb, S5_GROUPS_PER_BLOCK, S5_STATE, S5_GROUP).transpose(0, 1, 3, 2))
    to_bd_c = lambda t: _block_diag(t.reshape(nb, S5_GROUPS_PER_BLOCK, S5_GROUP, S5_STATE).transpose(0, 1, 3, 2))
    bbd_re, bbd_im = to_bd_b(bbr).astype(BF16), to_bd_b(bbi).astype(BF16)
    cbd_re, cbd_im = to_bd_c(p['s5_c_re']).astype(BF16), to_bd_c(p['s5_c_im']).astype(BF16)
    xb_arr = lax.slice_in_dim(proj, CB_XB * gw, (CB_XB + 1) * gw, axis=1)
    h_re, h_im, ylin = _s5_fwd(xb_arr, bbd_re, bbd_im, cbd_re, cbd_im, coef, ts)
    gl = _tile_fwd(_s5_post1, "s5_post1", nt, [_rt(ylin, ts), _rt(proj, ts, CB_XB, gw)], [p['s5_d']],
                   [((s, gw), BF16, (ts, gw), lambda i: (i, 0))])[0]
    tglu = _mm(gl, p['s5_w_glu'], b_shards=True, name="s5_glu")
    gbb = _tile_fwd(_s5_post2, "s5_post2", nt, [_rt(tglu, ts), _rt(proj, ts, CB_ZB, gw)], [],
                    [((s, gw), BF16, (ts, gw), lambda i: (i, 0))])[0]

    m_len = mem.shape[0]
    mem_n = _tile_fwd(_rms, "mem_rms", 1, [_rt(mem, m_len)], [p['mem_norm_g']],
                      [((m_len, d), BF16, (m_len, d), lambda i: (i, 0))])[0]
    kv = _mm(mem_n, p['w_kv_mem'], name="mem_kv")
    gcc = _tile_fwd(_attn, "attn", nt, [_rt(proj, ts, CB_QC, gw), _rt(proj, ts, CB_ZC, gw)], [kv],
                    [((s, gw), BF16, (ts, gw), lambda i: (i, 0))])[0]

    p_a = _mm(ga, p['w_br_a'], b_shards=True, name="br_a")
    p_b = _mm(gbb, p['w_br_b'], b_shards=True, name="br_b")
    p_c = _mm(gcc, p['w_br_c'], b_shards=True, name="br_c")
    gate_acts = [_rt(proj, ts, CB_G // 2 + j, d) for j in range(3)]
    merged = _tile_fwd(_merge, "merge", nt, gate_acts + [_rt(p_a, ts), _rt(p_b, ts), _rt(p_c, ts)], [],
                       [((s, d), BF16, (ts, d), lambda i: (i, 0))])[0]
    mo = _mm(merged, p['w_out'], name="out_proj")
    dh, dfg, loss = _final(x, mo, target, p['final_g'].reshape(1, d), ts)
    grads['final_g'] = dfg.reshape(d)

    dmerged = _mm(dh, p['w_out'], tb=True, name="d_merged")
    grads['w_out'] = _mm(merged, dh, ta=True, name="dw_out")
    row_d = lambda dt: ((s, d), dt, (ts, d), lambda i: (i, 0))
    dg0, dg1, dg2, dpa, dpb, dpc = _tile_bwd(
        _merge, "merge_bwd", nt, gate_acts + [_rt(p_a, ts), _rt(p_b, ts), _rt(p_c, ts)], [], [_rt(dmerged, ts)],
        [row_d(BF16)] * 6, [])
    dga = _mm(dpa, p['w_br_a'], tb=True, b_shards=True, name="d_ga")
    dgbb = _mm(dpb, p['w_br_b'], tb=True, b_shards=True, name="d_gb")
    dgcc = _mm(dpc, p['w_br_c'], tb=True, b_shards=True, name="d_gc")
    grads['w_br_a'] = _mm(ga, dpa, ta=True, out_shards=N_DEV, name="dw_br_a")
    grads['w_br_b'] = _mm(gbb, dpb, ta=True, out_shards=N_DEV, name="dw_br_b")
    grads['w_br_c'] = _mm(gcc, dpc, ta=True, out_shards=N_DEV, name="dw_br_c")
    row_h = lambda dt: ((s, gw), dt, (ts, gw), lambda i: (i, 0))

    dqc, dzc, dkv = _tile_bwd(_attn, "attn_bwd", nt, [_rt(proj, ts, CB_QC, gw), _rt(proj, ts, CB_ZC, gw)], [kv],
                              [_rt(dgcc, ts)], [row_h(BF16), row_h(BF16)], [True])
    grads['w_kv_mem'] = _mm(mem_n, dkv, ta=True, name="dw_kv")
    dmem_n = _mm(dkv, p['w_kv_mem'], tb=True, name="d_mem_n")
    grads['mem_norm_g'] = _tile_bwd(_rms, "mem_rms_bwd", 1, [_rt(mem, m_len)], [p['mem_norm_g']],
                                    [_rt(dmem_n, m_len)], [None], [True])[0]

    dtglu, dzb = _tile_bwd(_s5_post2, "s5_post2_bwd", nt, [_rt(tglu, ts), _rt(proj, ts, CB_ZB, gw)], [],
                           [_rt(dgbb, ts)], [((s, 2 * gw), BF16, (ts, 2 * gw), lambda i: (i, 0)), row_h(BF16)], [])
    grads['s5_w_glu'] = _mm(gl, dtglu, ta=True, out_shards=N_DEV, name="dw_glu")
    s5_d = p['s5_d']
    if early_grads is not None:
        s5_d = s5_d + early_grads(grads)[:1, :1]
    dgl = _mm(dtglu, p['s5_w_glu'], tb=True, b_shards=True, name="d_gl")
    dylin, dxb1, dd = _tile_bwd(_s5_post1, "s5_post1_bwd", nt, [_rt(ylin, ts), _rt(proj, ts, CB_XB, gw)],
                                [s5_d], [_rt(dgl, ts)], [row_h(F32), row_h(F32)], [True])
    grads['s5_d'] = dd
    dxb2, dbbd_re, dbbd_im, dcbd_re, dcbd_im, da_re, da_im = _s5_bwd(dylin, xb_arr, h_re, h_im, bbd_re, bbd_im,
                                                                    cbd_re, cbd_im, coef, ts)
    from_bd_b = lambda t: _diag_blocks(t, S5_GROUP, S5_STATE).transpose(0, 1, 3, 2).reshape(ng, S5_STATE * S5_GROUP)
    from_bd_c = lambda t: _diag_blocks(t, S5_STATE, S5_GROUP).transpose(0, 1, 3, 2).reshape(1, ng, S5_GROUP, S5_STATE)
    grads['s5_c_re'], grads['s5_c_im'] = from_bd_c(dcbd_re), from_bd_c(dcbd_im)
    s5_cts = [jnp.sum(da_re, axis=0).reshape(ng, S5_STATE), jnp.sum(da_im, axis=0).reshape(ng, S5_STATE),
              from_bd_b(dbbd_re), from_bd_b(dbbd_im)]
    dlr, dli, dlogdt, dbr, dbi = _tile_bwd(_s5_params, "s5_params_bwd", 1, [], s5_in,
                                           [(c, c.shape, lambda i: (0, 0)) for c in s5_cts], [],
                                           [True, True, True, True, True, False])
    grads['s5_lambda_re'], grads['s5_lambda_im'] = dlr[None], dli[None]
    grads['s5_log_dt'] = dlogdt.reshape(1, ng)
    grads['s5_b_re'] = dbr.reshape(1, ng, S5_STATE, S5_GROUP)
    grads['s5_b_im'] = dbi.reshape(1, ng, S5_STATE, S5_GROUP)
    dxb = (dxb1 + dxb2).astype(BF16)

    do_raw, dza, dgng = _tile_bwd(_gdn_post, "gdn_post_bwd", nt, [_rt(o_raw, ts), _rt(proj, ts, CB_ZA, gw)],
                                  [p['gdn_norm_g']], [_rt(dga, ts)], [row_h(F32), row_h(BF16)], [True])
    grads['gdn_norm_g'] = dgng
    *intra_cts, dgb_inter = _gdn_inter_bwd(*intra, gb, states, do_raw)
    dq, dk, dv, dgb, dbetab = _gdn_intra_bwd(q, k, v, gb, betab, t_inv, intra_cts, dgb_inter)
    dpba, dalog, ddtb = _tile_bwd(_gdn_gates, "gdn_gates_bwd", nt, [_rt(pba, ts)], [alog_row, dtb_row, e_beta, e_g],
                                  [_rt(dbetab, ts), _rt(dgb, ts)], [((s, 128), BF16, (ts, 128), lambda i: (i, 0))],
                                  [True, True, False, False])
    grads['gdn_a_log'] = dalog[:, nh:2 * nh]
    grads['gdn_dt_bias'] = ddtb[:, nh:2 * nh]
    dqkv, dconv = [], []
    for j, (mode, ct) in enumerate((('q', dq), ('k', dk), ('v', dv))):
        off = j * nh
        wspec = (conv_w, (CONV_WIDTH, GDN_HEAD_DIM), lambda i, off=off: (0, off + i))
        dxc, dwc = _tile_bwd(
            _gdn_pre(mode), "gdn_pre_bwd_" + mode, nh, [col(proj, off), wspec], [], [col(ct, 0)],
            [((s, gw), BF16, (s, GDN_HEAD_DIM), lambda i: (0, i)),
             ((CONV_WIDTH, gw), F32, (CONV_WIDTH, GDN_HEAD_DIM), lambda i: (0, i))], [])
        dqkv.append(dxc)
        dconv.append(dwc)
    grads['conv_w'] = jnp.concatenate(dconv, axis=1)

    dproj = jnp.concatenate(dqkv + [dza, dxb, dzb, dqc, dzc, dg0, dg1, dg2], axis=1)
    grads['w_ba'] = _mm(u, dpba, ta=True, name="dw_ba")
    if last_grads is None:
        grads['w_main'] = _mm(u, dproj, ta=True, tm=1024, tn=2048, tk=512, name="dw_main")
    else:
        uu = u
        for h in range(2):
            gm = _mm(uu, dproj, ta=True, tm=1024, tn=2048, tk=1024, rows=(h * (d // 2), d // 2), name=f"dw_main_{h}")
            tok = last_grads(h, gm, grads)[:1, :1].astype(BF16)
            if h == 0:
                uu = uu + tok
            else:
                dpba = dpba + tok
    du = _mm(dpba, w_ba, tb=True, name="du_ba")
    du = _mm(dproj, w_main, tb=True, addend=du, tm=512, tn=2048, tk=2048, name="du_main")
    grad_x, dng = _tile_bwd(_rms, "rms_bwd", nt, [_rt(x, ts)], [p['norm_g']], [_rt(du, ts)],
                            [row_d(F32) + (dh,)], [True])
    grads['norm_g'] = dng
    return loss, grad_x, grads


def _to_shards(name, g):
    if SHARDED[name] == 'row':
        return g.reshape((N_DEV, g.shape[0] // N_DEV) + g.shape[1:])
    r, c = g.shape
    return g.reshape(r, N_DEV, c // N_DEV).transpose(1, 0, 2)


def _from_shards(name, t):
    if SHARDED[name] == 'row':
        return t.reshape((t.shape[0] * t.shape[1],) + t.shape[2:])
    n, r, c = t.shape
    return t.transpose(1, 0, 2).reshape(r, n * c)


def _step(x, mem, target, w, m, v):
    sharded = list(SHARDED)
    shard_shapes = {n: tuple(w[n].shape[1:]) for n in sharded}
    d = x.shape[-1]
    ba_lo = 2 * d
    ba_hi = ba_lo + 2 * (d // 2 // GDN_HEAD_DIM)

    w_in_all = _gather([w['w_in'][0].astype(BF16)], "gather_w_in")[0]
    after_w_in = (w_in_all[0, :1, :1] * 0).astype(F32)
    late = [w[n][0].astype(BF16) for n in OVERLAPPED] + [w['conv_w'][0] + after_w_in]
    every = [_send_whole] * len(late)
    lands = [jnp.broadcast_to(t[None], (N_DEV,) + t.shape) for t in late]
    gather_started, token = _direct_start(late, lands, every, "gather_rest_start")
    full = {}
    full['w_main'], full['w_ba'] = _w_in_from_shards(w_in_all, ba_lo, ba_hi)
    for n in REPLICATED:
        full[n] = w[n]
    for n in ('s5_lambda_re', 's5_lambda_im', 's5_c_re', 's5_c_im'):
        full[n] = w[n][0]
    full['norm_g'] = w['norm_g'] + token[:1, :1]

    def late_weights(proj):
        got = dict(zip(OVERLAPPED + ['conv_w'], _direct_wait(gather_started, every, proj, "gather_rest_wait")[1]))
        for n in ('w_kv_mem', 'w_out', 'conv_w'):
            got[n] = _from_shards(n, got[n])
        return got

    slots = [_send_slot] * len(OVERLAPPED)
    scatter_started = []

    def early_grads(grads):
        gs = [_to_shards(n, grads[n]) if SHARDED[n] == 'row' else grads[n] for n in OVERLAPPED]
        started, tok = _direct_start(gs, [lax.empty(g.shape, g.dtype) for g in gs], slots, "scatter_early_start")
        scatter_started.append(started)
        return tok

    def last_grads(h, gm, grads):
        gs = [_w_in_to_shards(gm, grads['w_ba'], ba_lo, ba_hi, BF16, h * gm.shape[0], f"dw_in_layout_{h}")]
        if h == 0:
            gs.append(_to_shards('conv_w', grads['conv_w']))
        started, tok = _direct_start(gs, [lax.empty(g.shape, g.dtype) for g in gs], slots[:len(gs)],
                                     f"scatter_last_start_{h}")
        scatter_started.append(started)
        return tok

    loss, grad_x, grads = _local_step(x[0], mem[0], target[0], full, late_weights, early_grads, last_grads)
    res = {}
    me = (4 * lax.axis_index("x") + 2 * lax.axis_index("y") + lax.axis_index("c")).astype(jnp.int32).reshape(1)

    def update(names, exchanged):
        for n, own, part in zip(names, *exchanged):
            outs = _sum_adam(part, w[n][0], m[n][0], v[n][0], name="adam_" + n, own=own, me=me)
            for kind, t in zip(('grad', 'delta', 'new_m', 'new_v'), outs):
                res[kind, n] = t[None]

    update(OVERLAPPED, _direct_wait(scatter_started[0], slots, grad_x, "scatter_early_wait"))

    small = _pack([grads[n].reshape(w[n].shape) for n in REPLICATED] + [loss[:1, :1]], F32)
    allp = _gather([small], "gather_small")[0]
    zero = jnp.zeros((1, 1), F32)
    outs = _sum_adam(allp, *[_pack([t[n] for n in REPLICATED] + [zero], F32) for t in (w, m, v)], name="adam_small")
    shapes = [w[n].shape for n in REPLICATED] + [(1, 1)]
    for kind, buf in zip(('grad', 'delta', 'new_m', 'new_v'), outs):
        got = _unpack(buf, shapes)
        for n, t in zip(REPLICATED, got):
            res[kind, n] = t
        if kind == 'grad':
            total_loss = got[-1].reshape(())
    (own0, own_conv), (got0, got_conv) = _direct_wait(scatter_started[1], slots[:2], outs[0], "scatter_last_wait_0")
    (own1,), (got1,) = _direct_wait(scatter_started[2], slots[:1], outs[0], "scatter_last_wait_1")
    update(['w_in', 'conv_w'], ([[own0, own1], own_conv], [[got0, got1], got_conv]))
    out = [total_loss, grad_x[None]]
    for kind in ('grad', 'delta', 'new_m', 'new_v'):
        out += [res[kind, n] for n in WEIGHTS]
    return tuple(out)


def kernel(x, mem, norm_g, w_in, conv_w, gdn_a_log, gdn_dt_bias, gdn_norm_g, s5_lambda_re, s5_lambda_im, s5_log_dt, s5_b_re, s5_b_im, s5_c_re, s5_c_im, s5_d, s5_w_glu, mem_norm_g, w_kv_mem, w_br_a, w_br_b, w_br_c, w_out, final_g, loss_target, m_norm_g, m_w_in, m_conv_w, m_gdn_a_log, m_gdn_dt_bias, m_gdn_norm_g, m_s5_lambda_re, m_s5_lambda_im, m_s5_log_dt, m_s5_b_re, m_s5_b_im, m_s5_c_re, m_s5_c_im, m_s5_d, m_s5_w_glu, m_mem_norm_g, m_w_kv_mem, m_w_br_a, m_w_br_b, m_w_br_c, m_w_out, m_final_g, v_norm_g, v_w_in, v_conv_w, v_gdn_a_log, v_gdn_dt_bias, v_gdn_norm_g, v_s5_lambda_re, v_s5_lambda_im, v_s5_log_dt, v_s5_b_re, v_s5_b_im, v_s5_c_re, v_s5_c_im, v_s5_d, v_s5_w_glu, v_mem_norm_g, v_w_kv_mem, v_w_br_a, v_w_br_b, v_w_br_c, v_w_out, v_final_g):
    a = dict(locals())
    w = {n: a[n] for n in WEIGHTS}
    m = {n: a['m_' + n] for n in WEIGHTS}
    v = {n: a['v_' + n] for n in WEIGHTS}
    return _step(x, mem, loss_target, w, m, v)
```
